```python
import jax, jax.numpy as jnp
from jax import lax
import numpy as np

D_MODEL = 1024
BATCH = 32
SEQ = 256
DEPTH = 2
DEC_BATCH = 4
DEC_SEQ = 1024
PAST_LEN = 256

GRID_W = 64
ROPE_THETA = 10000.0
BLOCK_Q = 128
WINDOW = 128
HD = 64
H_A = 8
NOPE_A = 64
ROPE_A = 32
V_A = 64
Q_LORA = 256
KV_LORA = 128
H_B = 8
KVH_B = 2
H_C = 16
KVH_C = 4
AB_IN = Q_LORA + KV_LORA + ROPE_A + H_B * HD + 2 * KVH_B * HD
AB_OUT = H_A * V_A + H_B * HD
C_IN = H_C * HD + 2 * KVH_C * HD
C_OUT = H_C * HD
N_EXPERTS = 16
N_GROUPS = 4
EXPERTS_PER_GROUP = N_EXPERTS // N_GROUPS
TOP_K = 2
D_EXPERT = 512
ALPHA = (2 * DEPTH) ** 0.25
BETA = (8 * DEPTH) ** -0.25
N_EVEN = (DEPTH + 1) // 2
N_ODD = DEPTH // 2
NEG_INF = -1e30
EPS = 1e-6

kernel_name = "hybrid_mla_swa_gqa_moe_diffusion_step"


def rms_norm(x, g):
    xf = x.astype(jnp.float32)
    y = xf * lax.rsqrt(jnp.mean(xf * xf, axis=-1, keepdims=True) + EPS)
    return (y * g.astype(jnp.float32)).astype(x.dtype)


def layer_norm(x, g, b):
    xf = x.astype(jnp.float32)
    mu = jnp.mean(xf, axis=-1, keepdims=True)
    xc = xf - mu
    var = jnp.mean(xc * xc, axis=-1, keepdims=True)
    y = xc * lax.rsqrt(var + EPS) * g.astype(jnp.float32) + b.astype(jnp.float32)
    return y.astype(x.dtype)


def grid_angles(n_tok, rot_dim):
    rows = n_tok // GRID_W
    row = jnp.repeat(jnp.arange(rows, dtype=jnp.float32), GRID_W)
    col = jnp.tile(jnp.arange(GRID_W, dtype=jnp.float32), rows)
    n_freq = rot_dim // 4
    inv = 1.0 / (ROPE_THETA ** (jnp.arange(n_freq, dtype=jnp.float32) / n_freq))
    ang = jnp.concatenate([row[:, None] * inv, col[:, None] * inv], axis=-1)
    return jnp.cos(ang), jnp.sin(ang)


def apply_rope(x, cos, sin):
    half = x.shape[-1] // 2
    shp = (1, x.shape[1]) + (1,) * (x.ndim - 3) + (half,)
    cos = cos.reshape(shp)
    sin = sin.reshape(shp)
    x1 = x[..., :half].astype(jnp.float32)
    x2 = x[..., half:].astype(jnp.float32)
    return jnp.concatenate([x1 * cos - x2 * sin, x2 * cos + x1 * sin], axis=-1).astype(x.dtype)


def softmax_attend(q, k, v, sink=None):
    scale = q.shape[-1] ** -0.5
    s = jnp.einsum("bqhgd,bkhd->bhgqk", q, k).astype(jnp.float32) * scale
    if sink is not None:
        hk, g = q.shape[2], q.shape[3]
        s_sink = jnp.broadcast_to(sink.reshape(1, hk, g, 1, 1).astype(jnp.float32), s.shape[:-1] + (1,))
        s = jnp.concatenate([s, s_sink], axis=-1)
    p = jax.nn.softmax(s, axis=-1)[..., :k.shape[1]].astype(v.dtype)
    return jnp.einsum("bhgqk,bkhd->bqhgd", p, v)


def block_sweep(q, k, v, sink=None):
    b, t = q.shape[:2]
    nb = t // BLOCK_Q
    qb = jnp.swapaxes(q.reshape((b, nb, BLOCK_Q) + q.shape[2:]), 0, 1)
    ob = lax.map(lambda qi: softmax_attend(qi, k, v, sink), qb)
    return jnp.swapaxes(ob, 0, 1).reshape((b, t) + ob.shape[3:])


def band_attend(q, k, v, k_ctx, v_ctx, sink):
    b, t, hk, g, d = q.shape
    dv = v.shape[-1]
    nb = t // BLOCK_Q
    scale = d ** -0.5
    halo = [(0, 0), (BLOCK_Q, BLOCK_Q), (0, 0), (0, 0)]
    kp = jnp.pad(k, halo).reshape(b, nb + 2, BLOCK_Q, hk, d)
    vp = jnp.pad(v, halo).reshape(b, nb + 2, BLOCK_Q, hk, dv)
    kb = jnp.concatenate([kp[:, :-2], kp[:, 1:-1], kp[:, 2:]], axis=2)
    vb = jnp.concatenate([vp[:, :-2], vp[:, 1:-1], vp[:, 2:]], axis=2)
    qb = q.reshape(b, nb, BLOCK_Q, hk, g, d)
    rel_q = jnp.arange(BLOCK_Q)[:, None]
    rel_k = jnp.arange(3 * BLOCK_Q)[None, :] - BLOCK_Q
    abs_k = jnp.arange(nb)[:, None, None] * BLOCK_Q + rel_k[None]
    mask = (jnp.abs(rel_q - rel_k) <= WINDOW)[None] & (abs_k >= 0) & (abs_k < t)
    s_loc = jnp.einsum("bnqhgd,bnkhd->bhgnqk", qb, kb).astype(jnp.float32) * scale
    s_loc = jnp.where(mask[None, None, None], s_loc, NEG_INF)
    s_ctx = jnp.einsum("bnqhgd,bkhd->bhgnqk", qb, k_ctx).astype(jnp.float32) * scale
    s_sink = jnp.broadcast_to(sink.reshape(1, hk, g, 1, 1, 1).astype(jnp.float32), s_loc.shape[:-1] + (1,))
    p = jax.nn.softmax(jnp.concatenate([s_loc, s_ctx, s_sink], axis=-1), axis=-1).astype(v.dtype)
    n_loc = 3 * BLOCK_Q
    n_ctx = k_ctx.shape[1]
    o = (jnp.einsum("bhgnqk,bnkhd->bnqhgd", p[..., :n_loc], vb)
         + jnp.einsum("bhgnqk,bkhd->bnqhgd", p[..., n_loc:n_loc + n_ctx], v_ctx))
    return o.reshape(b, t, hk, g, dv)


def modulation(cond, w_mod, b_mod):
    m = jax.nn.silu(cond) @ w_mod + b_mod
    return jnp.split(m[..., None, :], 6, axis=-1)


def modulate(x, shift, scale):
    return x * (1.0 + scale) + shift


def ab_project(h, w_in, q_norm, w_uq, kv_norm):
    b, t, _ = h.shape
    z = h @ w_in
    cuts = [Q_LORA, Q_LORA + KV_LORA, Q_LORA + KV_LORA + ROPE_A, Q_LORA + KV_LORA + ROPE_A + H_B * HD,
            Q_LORA + KV_LORA + ROPE_A + H_B * HD + KVH_B * HD]
    q_lat, ckv, kpe, qb, kb, vb = jnp.split(z, cuts, axis=-1)
    q_a = (rms_norm(q_lat, q_norm) @ w_uq).reshape(b, t, H_A, NOPE_A + ROPE_A)
    ckv = rms_norm(ckv, kv_norm)
    qb = qb.reshape(b, t, KVH_B, H_B // KVH_B, HD)
    kb = kb.reshape(b, t, KVH_B, HD)
    vb = vb.reshape(b, t, KVH_B, HD)
    return q_a, ckv, kpe, qb, kb, vb


def mla_keys(ckv, kpe, w_ukv):
    b, t, _ = ckv.shape
    kv = (ckv @ w_ukv).reshape(b, t, H_A, NOPE_A + V_A)
    k = jnp.concatenate([kv[..., :NOPE_A], jnp.broadcast_to(kpe[:, :, None, :], (b, t, H_A, ROPE_A))], axis=-1)
    return k, kv[..., NOPE_A:]


def ab_context(h, w_in, q_norm, w_uq, kv_norm, w_ukv, sink, w_out):
    b, t, _ = h.shape
    q_a, ckv, kpe, qb, kb, vb = ab_project(h, w_in, q_norm, w_uq, kv_norm)
    k_a, v_a = mla_keys(ckv, kpe, w_ukv)
    o_a = block_sweep(q_a[:, :, :, None, :], k_a, v_a)
    o_b = block_sweep(qb, kb, vb, sink)
    out = jnp.concatenate([o_a.reshape(b, t, -1), o_b.reshape(b, t, -1)], axis=-1) @ w_out
    return out, ckv, kpe, kb, vb


def ab_latent(h, ckv_ctx, kpe_ctx, kb_ctx, vb_ctx, w_in, q_norm, w_uq, kv_norm, w_ukv, sink, w_out):
    b, t, _ = h.shape
    q_a, ckv, kpe, qb, kb, vb = ab_project(h, w_in, q_norm, w_uq, kv_norm)
    cos_a, sin_a = grid_angles(t, ROPE_A)
    cos_b, sin_b = grid_angles(t, HD)
    q_a = jnp.concatenate([q_a[..., :NOPE_A], apply_rope(q_a[..., NOPE_A:], cos_a, sin_a)], axis=-1)
    kpe = apply_rope(kpe, cos_a, sin_a)
    k_a, v_a = mla_keys(jnp.concatenate([ckv, ckv_ctx], axis=1), jnp.concatenate([kpe, kpe_ctx], axis=1), w_ukv)
    o_a = block_sweep(q_a[:, :, :, None, :], k_a, v_a)
    qb = apply_rope(qb, cos_b, sin_b)
    kb = apply_rope(kb, cos_b, sin_b)
    o_b = band_attend(qb, kb, vb, kb_ctx, vb_ctx, sink)
    return jnp.concatenate([o_a.reshape(b, t, -1), o_b.reshape(b, t, -1)], axis=-1) @ w_out


def c_project(h, w_in, q_norm, k_norm):
    b, t, _ = h.shape
    q, k, v = jnp.split(h @ w_in, [H_C * HD, H_C * HD + KVH_C * HD], axis=-1)
    q = rms_norm(q.reshape(b, t, KVH_C, H_C // KVH_C, HD), q_norm)
    k = rms_norm(k.reshape(b, t, KVH_C, HD), k_norm)
    return q, k, v.reshape(b, t, KVH_C, HD)


def c_context(h, w_in, q_norm, k_norm, w_out):
    b, t, _ = h.shape
    q, k, v = c_project(h, w_in, q_norm, k_norm)
    o = block_sweep(q, k, v)
    return o.reshape(b, t, -1) @ w_out, k, v


def c_latent(h, k_ctx, v_ctx, w_in, q_norm, k_norm, w_out):
    b, t, _ = h.shape
    q, k, v = c_project(h, w_in, q_norm, k_norm)
    cos, sin = grid_angles(t, HD)
    q = apply_rope(q, cos, sin)
    k = apply_rope(k, cos, sin)
    o = block_sweep(q, jnp.concatenate([k, k_ctx], axis=1), jnp.concatenate([v, v_ctx], axis=1))
    return o.reshape(b, t, -1) @ w_out


def moe(h, router_w, router_bias, w_gate, w_up, w_down):
    b, t, d = h.shape
    x = h.reshape(-1, d)
    aff = jax.nn.sigmoid((x @ router_w).astype(jnp.float32))
    sel = aff + router_bias.astype(jnp.float32)
    grp_score = lax.top_k(sel.reshape(-1, N_GROUPS, EXPERTS_PER_GROUP), TOP_K)[0].sum(-1)
    g_idx = jnp.argmax(grp_score, axis=-1)
    in_group = (jnp.arange(N_EXPERTS) // EXPERTS_PER_GROUP)[None, :] == g_idx[:, None]
    _, e_idx = lax.top_k(jnp.where(in_group, sel, -jnp.inf), TOP_K)
    w = jnp.take_along_axis(aff, e_idx, axis=-1)
    w = w / jnp.sum(w, axis=-1, keepdims=True)
    gates = jnp.sum(jax.nn.one_hot(e_idx, N_EXPERTS, dtype=jnp.float32) * w[..., None], axis=1)
    hid = jax.nn.silu(jnp.einsum("nd,edf->nef", x, w_gate)) * jnp.einsum("nd,edf->nef", x, w_up)
    y = jnp.einsum("nef,efd->nd", hid * gates[:, :, None].astype(hid.dtype), w_down)
    return y.reshape(b, t, d)


def setup_inputs(seed: int = 0) -> dict:
    key = jax.random.key(seed)
    ks = iter(jax.random.split(key, 40))

    def nrm(shape, scale):
        return jax.random.normal(next(ks), shape, jnp.float32) * scale

    d = D_MODEL
    return {
        "x_prompt": nrm((BATCH, SEQ, d), 1.0),
        "x_sample": nrm((DEC_BATCH, DEC_SEQ, d), 1.0),
        "c": nrm((DEC_BATCH, d), 1.0),
        "cache_mla_ckv": nrm((DEC_BATCH, N_EVEN, PAST_LEN, KV_LORA), 1.0),
        "cache_mla_kpe": nrm((DEC_BATCH, N_EVEN, PAST_LEN, ROPE_A), 1.0),
        "cache_swa_k": nrm((DEC_BATCH, N_EVEN, PAST_LEN, KVH_B, HD), 1.0),
        "cache_swa_v": nrm((DEC_BATCH, N_EVEN, PAST_LEN, KVH_B, HD), 1.0),
        "cache_gqa_k": nrm((DEC_BATCH, N_ODD, PAST_LEN, KVH_C, HD), 1.0),
        "cache_gqa_v": nrm((DEC_BATCH, N_ODD, PAST_LEN, KVH_C, HD), 1.0),
        "c_ctx": nrm((d,), 1.0),
        "w_mod": nrm((DEPTH, d, 6 * d), 0.5 * d ** -0.5),
        "b_mod": nrm((DEPTH, 6 * d), 0.02),
        "ln_g": 1.0 + nrm((DEPTH, 2, d), 0.02),
        "ln_b": nrm((DEPTH, 2, d), 0.02),
        "w_in_ab": nrm((N_EVEN, d, AB_IN), d ** -0.5),
        "mla_q_norm": 1.0 + nrm((N_EVEN, Q_LORA), 0.02),
        "mla_w_uq": nrm((N_EVEN, Q_LORA, H_A * (NOPE_A + ROPE_A)), Q_LORA ** -0.5),
        "mla_kv_norm": 1.0 + nrm((N_EVEN, KV_LORA), 0.02),
        "mla_w_ukv": nrm((N_EVEN, KV_LORA, H_A * (NOPE_A + V_A)), KV_LORA ** -0.5),
        "swa_sink": nrm((N_EVEN, H_B), 0.5),
        "w_out_ab": nrm((N_EVEN, AB_OUT, d), BETA * AB_OUT ** -0.5),
        "w_in_c": nrm((N_ODD, d, C_IN), d ** -0.5),
        "gqa_q_norm": 1.0 + nrm((N_ODD, HD), 0.02),
        "gqa_k_norm": 1.0 + nrm((N_ODD, HD), 0.02),
        "w_out_c": nrm((N_ODD, C_OUT, d), BETA * C_OUT ** -0.5),
        "router_w": nrm((d, N_EXPERTS), d ** -0.5),
        "router_bias": nrm((N_EXPERTS,), 0.01),
        "exp_w_gate": nrm((DEPTH, N_EXPERTS, d, D_EXPERT), d ** -0.5),
        "exp_w_up": nrm((DEPTH, N_EXPERTS, d, D_EXPERT), d ** -0.5),
        "exp_w_down": nrm((DEPTH, N_EXPERTS, D_EXPERT, d), BETA * D_EXPERT ** -0.5),
    }


def reference(x_prompt, x_sample, c, cache_mla_ckv, cache_mla_kpe, cache_swa_k, cache_swa_v, cache_gqa_k,
              cache_gqa_v, c_ctx, w_mod, b_mod, ln_g, ln_b, w_in_ab, mla_q_norm, mla_w_uq, mla_kv_norm, mla_w_ukv,
              swa_sink, w_out_ab, w_in_c, gqa_q_norm, gqa_k_norm, w_out_c, router_w, router_bias, exp_w_gate,
              exp_w_up, exp_w_down):
    xp = x_prompt
    xs = x_sample
    st_ckv, st_kpe, st_swk, st_swv, st_gk, st_gv = [], [], [], [], [], []
    for l in range(DEPTH):
        i = l // 2
        mp = modulation(c_ctx, w_mod[l], b_mod[l])
        ms = modulation(c, w_mod[l], b_mod[l])
        hp = modulate(xp, mp[0], mp[1])
        hs = modulate(xs, ms[0], ms[1])
        if l % 2 == 0:
            sink = swa_sink[i].reshape(KVH_B, H_B // KVH_B)
            op, ckv, kpe, kb, vb = ab_context(hp, w_in_ab[i], mla_q_norm[i], mla_w_uq[i], mla_kv_norm[i],
                                              mla_w_ukv[i], sink, w_out_ab[i])
            os_ = ab_latent(hs, cache_mla_ckv[:, i], cache_mla_kpe[:, i], cache_swa_k[:, i], cache_swa_v[:, i],
                            w_in_ab[i], mla_q_norm[i], mla_w_uq[i], mla_kv_norm[i], mla_w_ukv[i], sink, w_out_ab[i])
            st_ckv.append(ckv)
            st_kpe.append(kpe)
            st_swk.append(kb)
            st_swv.append(vb)
        else:
            op, kc, vc = c_context(hp, w_in_c[i], gqa_q_norm[i], gqa_k_norm[i], w_out_c[i])
            os_ = c_latent(hs, cache_gqa_k[:, i], cache_gqa_v[:, i], w_in_c[i], gqa_q_norm[i], gqa_k_norm[i],
                           w_out_c[i])
            st_gk.append(kc)
            st_gv.append(vc)
        xp = layer_norm(ALPHA * xp + mp[2] * op, ln_g[l, 0], ln_b[l, 0])
        xs = layer_norm(ALPHA * xs + ms[2] * os_, ln_g[l, 0], ln_b[l, 0])
        hp = modulate(xp, mp[3], mp[4])
        hs = modulate(xs, ms[3], ms[4])
        fp = moe(hp, router_w, router_bias, exp_w_gate[l], exp_w_up[l], exp_w_down[l])
        fs = moe(hs, router_w, router_bias, exp_w_gate[l], exp_w_up[l], exp_w_down[l])
        xp = layer_norm(ALPHA * xp + mp[5] * fp, ln_g[l, 1], ln_b[l, 1])
        xs = layer_norm(ALPHA * xs + ms[5] * fs, ln_g[l, 1], ln_b[l, 1])
    return (xp, xs, jnp.stack(st_ckv, axis=1), jnp.stack(st_kpe, axis=1), jnp.stack(st_swk, axis=1),
            jnp.stack(st_swv, axis=1), jnp.stack(st_gk, axis=1), jnp.stack(st_gv, axis=1))
```

```python
import functools

import numpy as np
import jax
import jax.numpy as jnp
from jax import lax
from jax.experimental import pallas as pl
from jax.experimental.pallas import tpu as pltpu

D_MODEL = 1024
BATCH = 32
SEQ = 256
DEPTH = 2
DEC_BATCH = 4
DEC_SEQ = 1024
PAST_LEN = 256
GRID_W = 64
ROPE_THETA = 10000.0
WINDOW = 128
HD = 64
H_A = 8
NOPE_A = 64
ROPE_A = 32
V_A = 64
Q_LORA = 256
KV_LORA = 128
H_B = 8
KVH_B = 2
H_C = 16
KVH_C = 4
N_EXPERTS = 16
N_GROUPS = 4
EXPERTS_PER_GROUP = 4
D_EXPERT = 512
ALPHA = (2 * DEPTH) ** 0.25
NEG_INF = -1e30
EPS = 1e-6

LANES = 128
N_PROMPT = BATCH * SEQ
N_SAMPLE = DEC_BATCH * DEC_SEQ
N_TOK = N_PROMPT + N_SAMPLE
TM = 256
N_TILES = N_TOK // TM
PROMPT_TILES = N_PROMPT // TM
SAMPLE_TILES_PER_BATCH = DEC_SEQ // TM
BQ_S = 128
MOD_ROWS = 8

PAIRS_PER_GROUP = 6
N_BUCKETS = N_GROUPS * PAIRS_PER_GROUP
SLOT_A_LOCAL = (0, 2, 2, 3, 3, 3)
SLOT_B_LOCAL = (1, 1, 0, 0, 1, 2)
TME = 256
MOE_TILES = -(-(N_TOK + N_BUCKETS * (TME - 1)) // TME)
MOE_ROWS = MOE_TILES * TME

AB_COLS = 1280
C_NORM_COLS = H_C * HD + KVH_C * HD

BF16 = jnp.bfloat16
F32 = jnp.float32
VMEM_LIMIT = 52 * 1024 * 1024


def _mod_row(t):
    return jnp.where(t < PROMPT_TILES, 0, 1 + (t - PROMPT_TILES) // SAMPLE_TILES_PER_BATCH)


def _mod_spec(layer, chunk):
    return pl.BlockSpec((None, None, None, 1, D_MODEL), lambda t: (layer, _mod_row(t), chunk, 0, 0))


def _const_spec(shape):
    nd = len(shape)
    return pl.BlockSpec(shape, lambda *_: (0,) * nd)


def _x_specs():
    return [
        pl.BlockSpec((TM, D_MODEL), lambda t: (jnp.minimum(t, PROMPT_TILES - 1), 0)),
        pl.BlockSpec((TM, D_MODEL), lambda t: (jnp.maximum(t - PROMPT_TILES, 0), 0)),
    ]


def _pick(t, a_ref, b_ref):
    return jnp.where(t < PROMPT_TILES, a_ref[...], b_ref[...])


def _params(sem):
    return pltpu.CompilerParams(dimension_semantics=sem, vmem_limit_bytes=VMEM_LIMIT)


def _mod_kernel(cond_ref, w_ref, b_ref, o_ref):
    c = cond_ref[...]
    s = (c / (1.0 + jnp.exp(-c))).astype(BF16)
    o_ref[...] = jnp.dot(s, w_ref[...].astype(BF16), preferred_element_type=F32) + b_ref[...]


def _modulation(cond, w_mod, b_mod):
    tn = 1536
    return pl.pallas_call(
        _mod_kernel,
        grid=(DEPTH, 6 * D_MODEL // tn),
        in_specs=[
            pl.BlockSpec((MOD_ROWS, D_MODEL), lambda l, j: (0, 0)),
            pl.BlockSpec((None, D_MODEL, tn), lambda l, j: (l, 0, j)),
            pl.BlockSpec((None, 1, tn), lambda l, j: (l, 0, j)),
        ],
        out_specs=pl.BlockSpec((None, MOD_ROWS, tn), lambda l, j: (l, 0, j)),
        out_shape=jax.ShapeDtypeStruct((DEPTH, MOD_ROWS, 6 * D_MODEL), F32),
        compiler_params=_params(("arbitrary", "arbitrary")),
        name="modulation",
    )(cond, w_mod, b_mod.reshape(DEPTH, 1, 6 * D_MODEL))


def _rope_cs(half):
    n_freq = half // 2
    rows = DEC_SEQ // GRID_W
    row = jnp.repeat(jnp.arange(rows, dtype=F32), GRID_W)
    col = jnp.tile(jnp.arange(GRID_W, dtype=F32), rows)
    inv = 1.0 / (ROPE_THETA ** (jnp.arange(n_freq, dtype=F32) / n_freq))
    ang = jnp.concatenate([row[:, None] * inv, col[:, None] * inv], axis=-1)
    cos, sin = jnp.cos(ang), jnp.sin(ang)
    reps = LANES // (2 * half)
    c = jnp.tile(jnp.concatenate([cos, cos], axis=-1), (1, reps))
    s = jnp.tile(jnp.concatenate([-sin, sin], axis=-1), (1, reps))
    return c, s


def _rope(x, c_ref, s_ref, half):
    w = x.shape[1]
    reps = w // LANES
    c = c_ref[...]
    s = s_ref[...]
    if reps > 1:
        c = jnp.concatenate([c] * reps, axis=1)
        s = jnp.concatenate([s] * reps, axis=1)
    ahead = pltpu.roll(x, w - half, 1)
    behind = pltpu.roll(x, half, 1)
    lane = lax.broadcasted_iota(jnp.int32, x.shape, 1)
    swapped = jnp.where((lane & (2 * half - 1)) < half, ahead, behind)
    return x * c + swapped * s


def _rope_specs():
    def idx(t):
        return (jnp.maximum(t - PROMPT_TILES, 0) % SAMPLE_TILES_PER_BATCH, 0)
    return [pl.BlockSpec((TM, LANES), idx)] * 4


def _inproj_ab_kernel(xp_ref, xs_ref, shift_ref, scale_ref, w_in_ref, qn_ref, kvn_ref, w_uq_ref, w_ukv_ref,
                      c64_ref, s64_ref, c32_ref, s32_ref,
                      kv_ref, qan_ref, qar_ref, qb_ref, ckv_ref, kb_ref, vb_ref, kpe_ref):
    t = pl.program_id(0)
    x = _pick(t, xp_ref, xs_ref)
    h = (x * (1.0 + scale_ref[...]) + shift_ref[...]).astype(BF16)
    z = jnp.dot(h, w_in_ref[...], preferred_element_type=F32)

    ql = z[:, :Q_LORA]
    ql = ql * lax.rsqrt(jnp.mean(ql * ql, axis=-1, keepdims=True) + EPS) * qn_ref[...]
    qa = jnp.dot(ql.astype(BF16), w_uq_ref[...], preferred_element_type=F32)
    c = z[:, Q_LORA:Q_LORA + KV_LORA]
    ckv = c * lax.rsqrt(jnp.mean(c * c, axis=-1, keepdims=True) + EPS) * kvn_ref[...]
    ckv_ref[...] = ckv
    kv_ref[...] = jnp.dot(ckv.astype(BF16), w_ukv_ref[...], preferred_element_type=F32).astype(BF16)
    vb_ref[...] = z[:, 1024:1152]
    qan_ref[...] = qa[:, :H_A * NOPE_A].astype(BF16)

    qar = qa[:, H_A * NOPE_A:]
    qb = z[:, 384:896]
    kb = z[:, 896:1024]
    kpe = z[:, 1152:1280]

    @pl.when(t < PROMPT_TILES)
    def _():
        qar_ref[...] = qar.astype(BF16)
        qb_ref[...] = qb.astype(BF16)
        kb_ref[...] = kb
        kpe_ref[...] = kpe

    @pl.when(t >= PROMPT_TILES)
    def _():
        qar_ref[...] = _rope(qar, c32_ref, s32_ref, ROPE_A // 2).astype(BF16)
        qb_ref[...] = _rope(qb, c64_ref, s64_ref, HD // 2).astype(BF16)
        kb_ref[...] = _rope(kb, c64_ref, s64_ref, HD // 2)
        kpe_ref[...] = _rope(kpe, c32_ref, s32_ref, ROPE_A // 2)


def _inproj_ab(xp, xs, mod, w_in, q_norm, kv_norm, w_uq, w_ukv, tabs):
    def tok(w):
        return pl.BlockSpec((TM, w), lambda t: (t, 0))
    widths = [(H_A * (NOPE_A + V_A), BF16), (H_A * NOPE_A, BF16), (H_A * ROPE_A, BF16), (H_B * HD, BF16),
              (KV_LORA, F32), (KVH_B * HD, F32), (KVH_B * HD, F32), (LANES, F32)]
    return pl.pallas_call(
        _inproj_ab_kernel,
        grid=(N_TILES,),
        in_specs=_x_specs() + [_mod_spec(0, 0), _mod_spec(0, 1), _const_spec((D_MODEL, AB_COLS)),
                               _const_spec((1, Q_LORA)), _const_spec((1, KV_LORA)),
                               _const_spec((Q_LORA, H_A * (NOPE_A + ROPE_A))),
                               _const_spec((KV_LORA, H_A * (NOPE_A + V_A)))] + _rope_specs(),
        out_specs=[tok(w) for w, _ in widths],
        out_shape=[jax.ShapeDtypeStruct((N_TOK, w), dt) for w, dt in widths],
        compiler_params=_params(("arbitrary",)),
        name="inproj_ab",
    )(xp, xs, mod, mod, w_in, q_norm, kv_norm, w_uq, w_ukv, *tabs)


def _inproj_c_kernel(xp_ref, xs_ref, shift_ref, scale_ref, w_in_ref, g_ref, r_ref, rt_ref,
                     c64_ref, s64_ref, c32_ref, s32_ref, q_ref, k_ref, v_ref):
    del c32_ref, s32_ref
    t = pl.program_id(0)
    x = _pick(t, xp_ref, xs_ref)
    h = (x * (1.0 + scale_ref[...]) + shift_ref[...]).astype(BF16)
    z = jnp.dot(h, w_in_ref[...], preferred_element_type=F32)
    v_ref[...] = z[:, C_NORM_COLS:]
    qk = z[:, :C_NORM_COLS]
    ss = jnp.dot(qk * qk, r_ref[...], preferred_element_type=F32, precision=lax.Precision.HIGHEST)
    inv = lax.rsqrt(ss * (1.0 / HD) + EPS)
    inv = jnp.dot(inv, rt_ref[...], preferred_element_type=F32, precision=lax.Precision.HIGHEST)
    qk = qk * inv * g_ref[...]

    @pl.when(t < PROMPT_TILES)
    def _():
        q_ref[...] = qk[:, :H_C * HD].astype(BF16)
        k_ref[...] = qk[:, H_C * HD:]

    @pl.when(t >= PROMPT_TILES)
    def _():
        q_ref[...] = _rope(qk[:, :H_C * HD], c64_ref, s64_ref, HD // 2).astype(BF16)
        k_ref[...] = _rope(qk[:, H_C * HD:], c64_ref, s64_ref, HD // 2)


def _inproj_c(xp, xs, mod, w_in, g_full, r_mat, rt_mat, tabs):
    def tok(w):
        return pl.BlockSpec((TM, w), lambda t: (t, 0))
    widths = [(H_C * HD, BF16), (KVH_C * HD, F32), (KVH_C * HD, F32)]
    return pl.pallas_call(
        _inproj_c_kernel,
        grid=(N_TILES,),
        in_specs=_x_specs() + [_mod_spec(1, 0), _mod_spec(1, 1), _const_spec((D_MODEL, C_NORM_COLS + KVH_C * HD)),
                               _const_spec((1, C_NORM_COLS)), _const_spec((C_NORM_COLS, LANES)),
                               _const_spec((LANES, C_NORM_COLS))] + _rope_specs(),
        out_specs=[tok(w) for w, _ in widths],
        out_shape=[jax.ShapeDtypeStruct((N_TOK, w), dt) for w, dt in widths],
        compiler_params=_params(("arbitrary",)),
        name="inproj_c",
    )(xp, xs, mod, mod, w_in, g_full, r_mat, rt_mat, *tabs)


def _qk(q, k):
    return lax.dot_general(q, k, (((1,), (1,)), ((), ())), preferred_element_type=F32)


def _softmax_pv(scores, values, sink=None):
    m = scores[0].max(axis=-1, keepdims=True)
    for s in scores[1:]:
        m = jnp.maximum(m, s.max(axis=-1, keepdims=True))
    if sink is not None:
        m = jnp.maximum(m, sink)
    den = None
    out = None
    for s, v in zip(scores, values):
        p = jnp.exp(s - m)
        d = p.sum(axis=-1, keepdims=True)
        o = jnp.dot(p.astype(BF16), v, preferred_element_type=F32)
        den = d if den is None else den + d
        out = o if out is None else out + o
    if sink is not None:
        den = den + jnp.exp(sink - m)
    return out / den


def _mla_heads(qan_ref, qar_ref, kv_refs, kpe_list, o_ref):
    scale = (NOPE_A + ROPE_A) ** -0.5
    for h in range(H_A):
        qn = qan_ref[:, h * NOPE_A:(h + 1) * NOPE_A]
        qr = qar_ref[:, h * ROPE_A:(h + 1) * ROPE_A]
        scores, values = [], []
        for kv_ref, kpe in zip(kv_refs, kpe_list):
            base = h * (NOPE_A + V_A)
            s = _qk(qn, kv_ref[:, base:base + NOPE_A]) + _qk(qr, kpe)
            scores.append(s * scale)
            values.append(kv_ref[:, base + NOPE_A:base + NOPE_A + V_A])
        o_ref[:, h * V_A:(h + 1) * V_A] = _softmax_pv(scores, values).astype(o_ref.dtype)


def _attn_ab_prompt_kernel(sink_ref, qan_ref, qar_ref, kv_ref, kpe_ref, qb_ref, kb_ref, vb_ref, o_ref):
    kpe = kpe_ref[:, :ROPE_A].astype(BF16)
    _mla_heads(qan_ref, qar_ref, [kv_ref], [kpe], o_ref)
    scale = HD ** -0.5
    for kh in range(KVH_B):
        k = kb_ref[:, kh * HD:(kh + 1) * HD].astype(BF16)
        v = vb_ref[:, kh * HD:(kh + 1) * HD].astype(BF16)
        for g in range(H_B // KVH_B):
            hq = kh * (H_B // KVH_B) + g
            s = _qk(qb_ref[:, hq * HD:(hq + 1) * HD], k) * scale
            col = H_A * V_A + hq * HD
            o_ref[:, col:col + HD] = _softmax_pv([s], [v], sink_ref[hq]).astype(o_ref.dtype)


def _attn_ab_prompt(sink, kv, qan, qar, qb, kb, vb, kpe):
    def blk(w):
        return pl.BlockSpec((SEQ, w), lambda b: (b, 0))
    return pl.pallas_call(
        _attn_ab_prompt_kernel,
        grid=(BATCH,),
        in_specs=[pl.BlockSpec(memory_space=pltpu.SMEM), blk(512), blk(256), blk(1024), blk(LANES), blk(512),
                  blk(128), blk(128)],
        out_specs=blk(D_MODEL),
        out_shape=jax.ShapeDtypeStruct((N_PROMPT, H_A * V_A + H_B * HD), BF16),
        compiler_params=_params(("arbitrary",)),
        name="attn_ab_prompt",
    )(sink, qan, qar, kv, kpe, qb, kb, vb)


def _attn_ab_sample_kernel(sink_ref, qan_ref, qar_ref, kv_ref, kpe_ref, qb_ref, kb_ref, vb_ref,
                           ckv_ctx_ref, kpe_ctx_ref, kb_ctx_ref, vb_ctx_ref, w_ukv_ref, o_ref, kv_ctx_ref):
    i = pl.program_id(1)

    @pl.when(i == 0)
    def _():
        kv_ctx_ref[...] = jnp.dot(ckv_ctx_ref[...].astype(BF16), w_ukv_ref[...],
                                  preferred_element_type=F32).astype(BF16)

    kpe = kpe_ref[:, :ROPE_A].astype(BF16)
    kpe_ctx = kpe_ctx_ref[...].astype(BF16)
    _mla_heads(qan_ref, qar_ref, [kv_ref, kv_ctx_ref], [kpe, kpe_ctx], o_ref)

    n_win = 3 * BQ_S
    start = pl.multiple_of(jnp.clip((i - 1) * BQ_S, 0, DEC_SEQ - n_win), BQ_S)
    qpos = i * BQ_S + lax.broadcasted_iota(jnp.int32, (BQ_S, n_win), 0)
    kpos = start + lax.broadcasted_iota(jnp.int32, (BQ_S, n_win), 1)
    in_band = jnp.abs(qpos - kpos) <= WINDOW
    scale = HD ** -0.5
    kwin = kb_ref[pl.ds(start, n_win), :].astype(BF16)
    vwin = vb_ref[pl.ds(start, n_win), :].astype(BF16)
    kctx = kb_ctx_ref[...].astype(BF16)
    vctx = vb_ctx_ref[...].astype(BF16)
    for kh in range(KVH_B):
        sl = slice(kh * HD, (kh + 1) * HD)
        for g in range(H_B // KVH_B):
            hq = kh * (H_B // KVH_B) + g
            q = qb_ref[:, hq * HD:(hq + 1) * HD]
            s_loc = jnp.where(in_band, _qk(q, kwin[:, sl]) * scale, NEG_INF)
            s_ctx = _qk(q, kctx[:, sl]) * scale
            col = H_A * V_A + hq * HD
            o_ref[:, col:col + HD] = _softmax_pv([s_loc, s_ctx], [vwin[:, sl], vctx[:, sl]],
                                                 sink_ref[hq]).astype(o_ref.dtype)


def _attn_ab_sample(sink, kv, qan, qar, qb, kb, vb, kpe, ckv_ctx, kpe_ctx, kb_ctx, vb_ctx, w_ukv):
    qoff = N_PROMPT // BQ_S
    nq = DEC_SEQ // BQ_S
    boff = N_PROMPT // DEC_SEQ

    def qblk(w):
        return pl.BlockSpec((BQ_S, w), lambda b, i: (qoff + b * nq + i, 0))

    def bblk(w):
        return pl.BlockSpec((DEC_SEQ, w), lambda b, i: (boff + b, 0))

    def cblk(w):
        return pl.BlockSpec((PAST_LEN, w), lambda b, i: (b, 0))

    return pl.pallas_call(
        _attn_ab_sample_kernel,
        grid=(DEC_BATCH, nq),
        in_specs=[pl.BlockSpec(memory_space=pltpu.SMEM), qblk(512), qblk(256), bblk(1024), bblk(LANES), qblk(512),
                  bblk(128), bblk(128), cblk(KV_LORA), cblk(ROPE_A), cblk(128), cblk(128),
                  _const_spec((KV_LORA, H_A * (NOPE_A + V_A)))],
        out_specs=pl.BlockSpec((BQ_S, D_MODEL), lambda b, i: (b * nq + i, 0)),
        out_shape=jax.ShapeDtypeStruct((N_SAMPLE, H_A * V_A + H_B * HD), BF16),
        scratch_shapes=[pltpu.VMEM((PAST_LEN, H_A * (NOPE_A + V_A)), BF16)],
        compiler_params=_params(("arbitrary", "arbitrary")),
        name="attn_ab_sample",
    )(sink, qan, qar, kv, kpe, qb, kb, vb, ckv_ctx, kpe_ctx, kb_ctx, vb_ctx, w_ukv)


def _gqa_heads(q_ref, k_list, v_list, o_ref):
    scale = HD ** -0.5
    for kh in range(KVH_C):
        sl = slice(kh * HD, (kh + 1) * HD)
        ks = [k[:, sl] for k in k_list]
        vs = [v[:, sl] for v in v_list]
        for g in range(H_C // KVH_C):
            hq = kh * (H_C // KVH_C) + g
            q = q_ref[:, hq * HD:(hq + 1) * HD]
            scores = [_qk(q, k) * scale for k in ks]
            o_ref[:, hq * HD:(hq + 1) * HD] = _softmax_pv(scores, vs).astype(o_ref.dtype)


def _attn_c_prompt_kernel(q_ref, k_ref, v_ref, o_ref):
    _gqa_heads(q_ref, [k_ref[...].astype(BF16)], [v_ref[...].astype(BF16)], o_ref)


def _attn_c_prompt(q, k, v):
    def blk(w):
        return pl.BlockSpec((SEQ, w), lambda b: (b, 0))
    return pl.pallas_call(
        _attn_c_prompt_kernel,
        grid=(BATCH,),
        in_specs=[blk(H_C * HD), blk(KVH_C * HD), blk(KVH_C * HD)],
        out_specs=blk(H_C * HD),
        out_shape=jax.ShapeDtypeStruct((N_PROMPT, H_C * HD), BF16),
        compiler_params=_params(("arbitrary",)),
        name="attn_c_prompt",
    )(q, k, v)


def _attn_c_sample_kernel(q_ref, k_ref, v_ref, kc_ref, vc_ref, o_ref):
    _gqa_heads(q_ref, [k_ref[...].astype(BF16), kc_ref[...].astype(BF16)],
               [v_ref[...].astype(BF16), vc_ref[...].astype(BF16)], o_ref)


def _attn_c_sample(q, k, v, k_ctx, v_ctx):
    qoff = N_PROMPT // BQ_S
    nq = DEC_SEQ // BQ_S
    boff = N_PROMPT // DEC_SEQ
    w = KVH_C * HD
    return pl.pallas_call(
        _attn_c_sample_kernel,
        grid=(DEC_BATCH, nq),
        in_specs=[pl.BlockSpec((BQ_S, H_C * HD), lambda b, i: (qoff + b * nq + i, 0)),
                  pl.BlockSpec((DEC_SEQ, w), lambda b, i: (boff + b, 0)),
                  pl.BlockSpec((DEC_SEQ, w), lambda b, i: (boff + b, 0)),
                  pl.BlockSpec((PAST_LEN, w), lambda b, i: (b, 0)),
                  pl.BlockSpec((PAST_LEN, w), lambda b, i: (b, 0))],
        out_specs=pl.BlockSpec((BQ_S, H_C * HD), lambda b, i: (b * nq + i, 0)),
        out_shape=jax.ShapeDtypeStruct((N_SAMPLE, H_C * HD), BF16),
        compiler_params=_params(("arbitrary", "arbitrary")),
        name="attn_c_sample",
    )(q, k, v, k_ctx, v_ctx)


def _layer_norm(y, g, b):
    mu = jnp.mean(y, axis=-1, keepdims=True)
    yc = y - mu
    var = jnp.mean(yc * yc, axis=-1, keepdims=True)
    return yc * lax.rsqrt(var + EPS) * g + b


def _route(sel, aff):
    def row(a, j):
        return a[j:j + 1, :]

    scores = []
    for g in range(N_GROUPS):
        a0, a1, a2, a3 = (row(sel, 4 * g + j) for j in range(4))
        hi01, lo01 = jnp.maximum(a0, a1), jnp.minimum(a0, a1)
        hi23, lo23 = jnp.maximum(a2, a3), jnp.minimum(a2, a3)
        top1 = jnp.maximum(hi01, hi23)
        top2 = jnp.maximum(jnp.minimum(hi01, hi23), jnp.maximum(lo01, lo23))
        scores.append(top1 + top2)
    best = scores[0]
    gi = jnp.zeros(best.shape, jnp.int32)
    for g in range(1, N_GROUPS):
        better = scores[g] > best
        gi = jnp.where(better, g, gi)
        best = jnp.where(better, scores[g], best)

    def in_group(a, j):
        out = row(a, j)
        for g in range(1, N_GROUPS):
            out = jnp.where(gi == g, row(a, 4 * g + j), out)
        return out

    v = [in_group(sel, j) for j in range(4)]
    a = [in_group(aff, j) for j in range(4)]
    chosen = []
    for j in range(4):
        rank = jnp.zeros(best.shape, jnp.int32)
        for k in range(4):
            if k == j:
                continue
            ahead = (v[k] >= v[j]) if k < j else (v[k] > v[j])
            rank = rank + ahead.astype(jnp.int32)
        chosen.append(rank < 2)
    total = sum(jnp.where(chosen[j], a[j], 0.0) for j in range(4))
    w = [jnp.where(chosen[j], a[j], 0.0) / total for j in range(4)]
    pair = jnp.zeros(best.shape, jnp.int32)
    for p, (ja, jb) in enumerate(zip(SLOT_A_LOCAL, SLOT_B_LOCAL)):
        pair = jnp.where(chosen[ja] & chosen[jb], p, pair)
    gate_a = jnp.where(pair == 0, w[0], jnp.where(pair <= 2, w[2], w[3]))
    gate_b = jnp.where((pair == 0) | (pair == 1) | (pair == 4), w[1], jnp.where(pair == 5, w[2], w[0]))
    return gi * PAIRS_PER_GROUP + pair, gate_a, gate_b


def _post_attn_kernel(op_ref, os_ref, xp_ref, xs_ref, w_out_ref, gate_ref, shift_ref, scale_ref, lng_ref, lnb_ref,
                      rw_ref, rb_ref, x1_ref, h2_ref, route_ref):
    t = pl.program_id(0)
    o = _pick(t, op_ref, os_ref)
    x = _pick(t, xp_ref, xs_ref)
    a = jnp.dot(o, w_out_ref[...], preferred_element_type=F32)
    x1 = _layer_norm(ALPHA * x + gate_ref[...] * a, lng_ref[...], lnb_ref[...])
    x1_ref[...] = x1
    h2 = x1 * (1.0 + scale_ref[...]) + shift_ref[...]
    h2_ref[...] = h2.astype(BF16)
    logits = lax.dot_general(rw_ref[...], h2, (((1,), (1,)), ((), ())), preferred_element_type=F32,
                             precision=lax.Precision.HIGHEST)
    aff = 1.0 / (1.0 + jnp.exp(-logits))
    bucket, gate_a, gate_b = _route(aff + rb_ref[...], aff)
    route_ref[...] = jnp.zeros((8, TM), F32)
    route_ref[0:1, :] = bucket.astype(F32)
    route_ref[1:2, :] = gate_a
    route_ref[2:3, :] = gate_b


def _post_attn(layer, o_p, o_s, xp, xs, mod, w_out, ln_g, ln_b, rw_t, rb):
    def o_specs():
        return [
            pl.BlockSpec((TM, D_MODEL), lambda t: (jnp.minimum(t, PROMPT_TILES - 1), 0)),
            pl.BlockSpec((TM, D_MODEL), lambda t: (jnp.maximum(t - PROMPT_TILES, 0), 0)),
        ]
    return pl.pallas_call(
        _post_attn_kernel,
        grid=(N_TILES,),
        in_specs=o_specs() + _x_specs() + [_const_spec((D_MODEL, D_MODEL)), _mod_spec(layer, 2), _mod_spec(layer, 3),
                                           _mod_spec(layer, 4), _const_spec((1, D_MODEL)), _const_spec((1, D_MODEL)),
                                           _const_spec((N_EXPERTS, D_MODEL)), _const_spec((N_EXPERTS, 1))],
        out_specs=[pl.BlockSpec((TM, D_MODEL), lambda t: (t, 0)), pl.BlockSpec((TM, D_MODEL), lambda t: (t, 0)),
                   pl.BlockSpec((8, TM), lambda t: (0, t))],
        out_shape=[jax.ShapeDtypeStruct((N_TOK, D_MODEL), F32), jax.ShapeDtypeStruct((N_TOK, D_MODEL), BF16),
                   jax.ShapeDtypeStruct((8, N_TOK), F32)],
        compiler_params=_params(("arbitrary",)),
        name="post_attn",
    )(o_p, o_s, xp, xs, w_out, mod, mod, mod, ln_g, ln_b, rw_t, rb)


def _moe_kernel(ta_ref, tb_ref, nt_ref, x_ref, g_ref, wga_ref, wua_ref, wda_ref, wgb_ref, wub_ref, wdb_ref, o_ref):
    del ta_ref, tb_ref

    @pl.when(pl.program_id(0) < nt_ref[0])
    def _():
        x = x_ref[...]
        acc = None
        for slot, (wg_ref, wu_ref, wd_ref) in enumerate(((wga_ref, wua_ref, wda_ref), (wgb_ref, wub_ref, wdb_ref))):
            gt = jnp.dot(x, wg_ref[...].astype(BF16), preferred_element_type=F32)
            up = jnp.dot(x, wu_ref[...].astype(BF16), preferred_element_type=F32)
            hid = (gt / (1.0 + jnp.exp(-gt))) * up * g_ref[:, slot:slot + 1]
            y = jnp.dot(hid.astype(BF16), wd_ref[...].astype(BF16), preferred_element_type=F32)
            acc = y if acc is None else acc + y
        o_ref[...] = acc


def _moe(tile_a, tile_b, n_tiles, xs, gates, w_gate, w_up, w_down):
    def row_idx(i, ta, tb, nt):
        return (jnp.minimum(i, nt[0] - 1), 0)

    def wspec(shape, which):
        if which == 0:
            return pl.BlockSpec((None,) + shape, lambda i, ta, tb, nt: (ta[i], 0, 0))
        return pl.BlockSpec((None,) + shape, lambda i, ta, tb, nt: (tb[i], 0, 0))

    up_shape, down_shape = (D_MODEL, D_EXPERT), (D_EXPERT, D_MODEL)
    grid_spec = pltpu.PrefetchScalarGridSpec(
        num_scalar_prefetch=3,
        grid=(MOE_TILES,),
        in_specs=[pl.BlockSpec((TME, D_MODEL), row_idx), pl.BlockSpec((TME, 2), row_idx),
                  wspec(up_shape, 0), wspec(up_shape, 0), wspec(down_shape, 0),
                  wspec(up_shape, 1), wspec(up_shape, 1), wspec(down_shape, 1)],
        out_specs=pl.BlockSpec((TME, D_MODEL), row_idx),
    )
    return pl.pallas_call(
        _moe_kernel,
        grid_spec=grid_spec,
        out_shape=jax.ShapeDtypeStruct((MOE_ROWS, D_MODEL), F32),
        compiler_params=_params(("arbitrary",)),
        name="moe_experts",
    )(tile_a, tile_b, n_tiles, xs, gates, w_gate, w_up, w_down, w_gate, w_up, w_down)


def _dispatch(route):
    bucket = route[0].astype(jnp.int32)
    order = jnp.argsort(bucket, stable=True).astype(jnp.int32)
    sorted_bucket = bucket[order]
    counts = jnp.sum(bucket[None, :] == jnp.arange(N_BUCKETS, dtype=jnp.int32)[:, None], axis=1).astype(jnp.int32)
    start = jnp.cumsum(counts) - counts
    padded = ((counts + TME - 1) // TME) * TME
    pend = jnp.cumsum(padded)
    pstart = pend - padded
    pos_sorted = pstart[sorted_bucket] + (jnp.arange(N_TOK, dtype=jnp.int32) - start[sorted_bucket])
    src = jnp.zeros((MOE_ROWS,), jnp.int32).at[pos_sorted].set(order)
    pos = jnp.zeros((N_TOK,), jnp.int32).at[order].set(pos_sorted)
    n_tiles = (pend[-1] // TME).astype(jnp.int32)
    tile_start = jnp.minimum(jnp.arange(MOE_TILES, dtype=jnp.int32), n_tiles - 1) * TME
    tile_bucket = jnp.minimum(jnp.searchsorted(pend, tile_start, side="right"), N_BUCKETS - 1).astype(jnp.int32)
    group, pair = tile_bucket // PAIRS_PER_GROUP, tile_bucket % PAIRS_PER_GROUP
    tile_a = group * EXPERTS_PER_GROUP + jnp.asarray(SLOT_A_LOCAL, jnp.int32)[pair]
    tile_b = group * EXPERTS_PER_GROUP + jnp.asarray(SLOT_B_LOCAL, jnp.int32)[pair]
    return src, pos, tile_a, tile_b, n_tiles.reshape(1)


def _post_moe_kernel(f_ref, x1_ref, gate_ref, lng_ref, lnb_ref, o_ref):
    o_ref[...] = _layer_norm(ALPHA * x1_ref[...] + gate_ref[...] * f_ref[...], lng_ref[...], lnb_ref[...])


def _post_moe(layer, f, x1, mod, ln_g, ln_b):
    tok = pl.BlockSpec((TM, D_MODEL), lambda t: (t, 0))
    return pl.pallas_call(
        _post_moe_kernel,
        grid=(N_TILES,),
        in_specs=[tok, tok, _mod_spec(layer, 5), _const_spec((1, D_MODEL)), _const_spec((1, D_MODEL))],
        out_specs=tok,
        out_shape=jax.ShapeDtypeStruct((N_TOK, D_MODEL), F32),
        compiler_params=_params(("arbitrary",)),
        name="post_moe",
    )(f, x1, mod, ln_g, ln_b)


def _ffn(layer, o_p, o_s, xp, xs, mod, w_out, ln_g, ln_b, rw_t, rb, w_gate, w_up, w_down):
    x1, h2, route = _post_attn(layer, o_p, o_s, xp, xs, mod, w_out, ln_g[layer, 0][None], ln_b[layer, 0][None],
                               rw_t, rb)
    src, pos, tile_a, tile_b, n_tiles = _dispatch(route)
    xs_sorted = jnp.take(h2, src, axis=0)
    gates = jnp.take(route[1:3].T, src, axis=0)
    f_sorted = _moe(tile_a, tile_b, n_tiles, xs_sorted, gates, w_gate, w_up, w_down)
    f = jnp.take(f_sorted, pos, axis=0)
    return _post_moe(layer, f, x1, mod, ln_g[layer, 1][None], ln_b[layer, 1][None])


def kernel(x_prompt, x_sample, c, cache_mla_ckv, cache_mla_kpe, cache_swa_k, cache_swa_v, cache_gqa_k, cache_gqa_v, c_ctx, w_mod, b_mod, ln_g, ln_b, w_in_ab, mla_q_norm, mla_w_uq, mla_kv_norm, mla_w_ukv, swa_sink, w_out_ab, w_in_c, gqa_q_norm, gqa_k_norm, w_out_c, router_w, router_bias, exp_w_gate, exp_w_up, exp_w_down):
    xp = x_prompt.reshape(N_PROMPT, D_MODEL)
    xs = x_sample.reshape(N_SAMPLE, D_MODEL)
    cond = jnp.concatenate([c_ctx[None], c, jnp.zeros((MOD_ROWS - 1 - DEC_BATCH, D_MODEL), F32)], axis=0)
    mod = _modulation(cond, w_mod, b_mod).reshape(DEPTH, MOD_ROWS, 6, 1, D_MODEL)
    tabs = _rope_cs(HD // 2) + _rope_cs(ROPE_A // 2)
    rw_t = router_w.T
    rb = router_bias.reshape(N_EXPERTS, 1)

    w = w_in_ab[0]
    pad = jnp.zeros((D_MODEL, AB_COLS - w.shape[1]), F32)
    w_in = jnp.concatenate([w[:, :384], w[:, 416:], w[:, 384:416], pad], axis=1).astype(BF16)
    w_uq = mla_w_uq[0].reshape(Q_LORA, H_A, NOPE_A + ROPE_A)
    w_uq = jnp.concatenate([w_uq[:, :, :NOPE_A].reshape(Q_LORA, -1), w_uq[:, :, NOPE_A:].reshape(Q_LORA, -1)],
                           axis=1).astype(BF16)
    w_ukv = mla_w_ukv[0].astype(BF16)
    kv, qan, qar, qb, ckv, kb, vb, kpe = _inproj_ab(xp, xs, mod, w_in, mla_q_norm[0][None], mla_kv_norm[0][None],
                                                    w_uq, w_ukv, tabs)
    sink = swa_sink[0]
    o_p = _attn_ab_prompt(sink, kv, qan, qar, qb, kb, vb, kpe)
    o_s = _attn_ab_sample(sink, kv, qan, qar, qb, kb, vb, kpe,
                          cache_mla_ckv[:, 0].reshape(-1, KV_LORA), cache_mla_kpe[:, 0].reshape(-1, ROPE_A),
                          cache_swa_k[:, 0].reshape(-1, KVH_B * HD), cache_swa_v[:, 0].reshape(-1, KVH_B * HD), w_ukv)
    x2 = _ffn(0, o_p, o_s, xp, xs, mod, w_out_ab[0].astype(BF16), ln_g, ln_b, rw_t, rb,
              exp_w_gate[0], exp_w_up[0], exp_w_down[0])
    xp1, xs1 = x2[:N_PROMPT], x2[N_PROMPT:]

    g_full = jnp.concatenate([jnp.tile(gqa_q_norm[0], H_C), jnp.tile(gqa_k_norm[0], KVH_C)])[None]
    r_np = (np.arange(C_NORM_COLS)[:, None] // HD == np.arange(LANES)[None, :]).astype(np.float32)
    qc, kc, vc = _inproj_c(xp1, xs1, mod, w_in_c[0].astype(BF16), g_full, jnp.asarray(r_np), jnp.asarray(r_np.T), tabs)
    oc_p = _attn_c_prompt(qc, kc, vc)
    oc_s = _attn_c_sample(qc, kc, vc, cache_gqa_k[:, 0].reshape(-1, KVH_C * HD),
                          cache_gqa_v[:, 0].reshape(-1, KVH_C * HD))
    x3 = _ffn(1, oc_p, oc_s, xp1, xs1, mod, w_out_c[0].astype(BF16), ln_g, ln_b, rw_t, rb,
              exp_w_gate[1], exp_w_up[1], exp_w_down[1])

    y_prompt = x3[:N_PROMPT].reshape(BATCH, SEQ, D_MODEL)
    y_sample = x3[N_PROMPT:].reshape(DEC_BATCH, DEC_SEQ, D_MODEL)
    new_ckv = ckv[:N_PROMPT].reshape(BATCH, 1, SEQ, KV_LORA)
    new_kpe = kpe[:N_PROMPT, :ROPE_A].reshape(BATCH, 1, SEQ, ROPE_A)
    new_swk = kb[:N_PROMPT].reshape(BATCH, 1, SEQ, KVH_B, HD)
    new_swv = vb[:N_PROMPT].reshape(BATCH, 1, SEQ, KVH_B, HD)
    new_gk = kc[:N_PROMPT].reshape(BATCH, 1, SEQ, KVH_C, HD)
    new_gv = vc[:N_PROMPT].reshape(BATCH, 1, SEQ, KVH_C, HD)
    return y_prompt, y_sample, new_ckv, new_kpe, new_swk, new_swv, new_gk, new_gv
```

```python
import functools

import numpy as np
import jax
import jax.numpy as jnp
from jax import lax
from jax.experimental import pallas as pl
from jax.experimental.pallas import tpu as pltpu

D_MODEL = 1024
BATCH = 32
SEQ = 256
DEPTH = 2
DEC_BATCH = 4
DEC_SEQ = 1024
PAST_LEN = 256
GRID_W = 64
ROPE_THETA = 10000.0
WINDOW = 128
HD = 64
H_A = 8
NOPE_A = 64
ROPE_A = 32
V_A = 64
Q_LORA = 256
KV_LORA = 128
H_B = 8
KVH_B = 2
H_C = 16
KVH_C = 4
N_EXPERTS = 16
N_GROUPS = 4
EXPERTS_PER_GROUP = 4
D_EXPERT = 512
ALPHA = (2 * DEPTH) ** 0.25
NEG_INF = -1e30
EPS = 1e-6

LANES = 128
N_PROMPT = BATCH * SEQ
N_SAMPLE = DEC_BATCH * DEC_SEQ
N_TOK = N_PROMPT + N_SAMPLE
TM = 256
N_TILES = N_TOK // TM
PROMPT_TILES = N_PROMPT // TM
SAMPLE_TILES_PER_BATCH = DEC_SEQ // TM
BQ_S = 128
MOD_ROWS = 8

PAIRS_PER_GROUP = 6
N_BUCKETS = N_GROUPS * PAIRS_PER_GROUP
SLOT_A_LOCAL = (0, 2, 2, 3, 3, 3)
SLOT_B_LOCAL = (1, 1, 0, 0, 1, 2)
TME = 256
MOE_TILES = -(-(N_TOK + N_BUCKETS * (TME - 1)) // TME)
MOE_ROWS = MOE_TILES * TME

BUCKET_ROWS = 32
ROW_W = D_MODEL + LANES
DMA_CHUNK = 256

AB_COLS = 1280
C_NORM_COLS = H_C * HD + KVH_C * HD
NORM_BLOCK = 256

BF16 = jnp.bfloat16
F32 = jnp.float32
VMEM_LIMIT = 52 * 1024 * 1024


def _mod_row(t):
    return jnp.where(t < PROMPT_TILES, 0, 1 + (t - PROMPT_TILES) // SAMPLE_TILES_PER_BATCH)


def _mod_spec(layer, chunk):
    return pl.BlockSpec((None, None, None, 1, D_MODEL), lambda t: (layer, _mod_row(t), chunk, 0, 0))


def _const_spec(shape):
    nd = len(shape)
    return pl.BlockSpec(shape, lambda *_: (0,) * nd)


def _x_specs():
    return [
        pl.BlockSpec((TM, D_MODEL), lambda t: (jnp.minimum(t, PROMPT_TILES - 1), 0)),
        pl.BlockSpec((TM, D_MODEL), lambda t: (jnp.maximum(t - PROMPT_TILES, 0), 0)),
    ]


def _pick(t, a_ref, b_ref):
    return jnp.where(t < PROMPT_TILES, a_ref[...], b_ref[...])


def _params(sem):
    return pltpu.CompilerParams(dimension_semantics=sem, vmem_limit_bytes=VMEM_LIMIT)


def _mod_kernel(cond_ref, w_ref, b_ref, o_ref):
    c = cond_ref[...]
    s = (c / (1.0 + jnp.exp(-c))).astype(BF16)
    o_ref[...] = jnp.dot(s, w_ref[...].astype(BF16), preferred_element_type=F32) + b_ref[...]


def _modulation(cond, w_mod, b_mod):
    tn = 1536
    return pl.pallas_call(
        _mod_kernel,
        grid=(DEPTH, 6 * D_MODEL // tn),
        in_specs=[
            pl.BlockSpec((MOD_ROWS, D_MODEL), lambda l, j: (0, 0)),
            pl.BlockSpec((None, D_MODEL, tn), lambda l, j: (l, 0, j)),
            pl.BlockSpec((None, 1, tn), lambda l, j: (l, 0, j)),
        ],
        out_specs=pl.BlockSpec((None, MOD_ROWS, tn), lambda l, j: (l, 0, j)),
        out_shape=jax.ShapeDtypeStruct((DEPTH, MOD_ROWS, 6 * D_MODEL), F32),
        compiler_params=_params(("arbitrary", "arbitrary")),
        name="modulation",
    )(cond, w_mod, b_mod.reshape(DEPTH, 1, 6 * D_MODEL))


def _rope_cs(half):
    n_freq = half // 2
    rows = DEC_SEQ // GRID_W
    row = jnp.repeat(jnp.arange(rows, dtype=F32), GRID_W)
    col = jnp.tile(jnp.arange(GRID_W, dtype=F32), rows)
    inv = 1.0 / (ROPE_THETA ** (jnp.arange(n_freq, dtype=F32) / n_freq))
    ang = jnp.concatenate([row[:, None] * inv, col[:, None] * inv], axis=-1)
    cos, sin = jnp.cos(ang), jnp.sin(ang)
    reps = LANES // (2 * half)
    c = jnp.tile(jnp.concatenate([cos, cos], axis=-1), (1, reps))
    s = jnp.tile(jnp.concatenate([-sin, sin], axis=-1), (1, reps))
    return c, s


def _rope(x, c_ref, s_ref, half):
    w = x.shape[1]
    reps = w // LANES
    c = c_ref[...]
    s = s_ref[...]
    if reps > 1:
        c = jnp.concatenate([c] * reps, axis=1)
        s = jnp.concatenate([s] * reps, axis=1)
    ahead = pltpu.roll(x, w - half, 1)
    behind = pltpu.roll(x, half, 1)
    lane = lax.broadcasted_iota(jnp.int32, x.shape, 1)
    swapped = jnp.where((lane & (2 * half - 1)) < half, ahead, behind)
    return x * c + swapped * s


def _rope_specs():
    def idx(t):
        return (jnp.maximum(t - PROMPT_TILES, 0) % SAMPLE_TILES_PER_BATCH, 0)
    return [pl.BlockSpec((TM, LANES), idx)] * 4


def _inproj_ab_kernel(xp_ref, xs_ref, shift_ref, scale_ref, w_in_ref, qn_ref, kvn_ref, w_uq_ref, w_ukv_ref,
                      c64_ref, s64_ref, c32_ref, s32_ref,
                      kv_ref, qan_ref, qar_ref, qb_ref, ckv_ref, kb_ref, vb_ref, kpe_ref):
    t = pl.program_id(0)
    x = _pick(t, xp_ref, xs_ref)
    h = (x * (1.0 + scale_ref[...]) + shift_ref[...]).astype(BF16)
    z = jnp.dot(h, w_in_ref[...], preferred_element_type=F32)

    ql = z[:, :Q_LORA]
    ql = ql * lax.rsqrt(jnp.mean(ql * ql, axis=-1, keepdims=True) + EPS) * qn_ref[...]
    qa = jnp.dot(ql.astype(BF16), w_uq_ref[...], preferred_element_type=F32)
    c = z[:, Q_LORA:Q_LORA + KV_LORA]
    ckv = c * lax.rsqrt(jnp.mean(c * c, axis=-1, keepdims=True) + EPS) * kvn_ref[...]
    ckv_ref[...] = ckv
    kv_ref[...] = jnp.dot(ckv.astype(BF16), w_ukv_ref[...], preferred_element_type=F32).astype(BF16)
    vb_ref[...] = z[:, 1024:1152]
    qan_ref[...] = qa[:, :H_A * NOPE_A].astype(BF16)

    qar = qa[:, H_A * NOPE_A:]
    qb = z[:, 384:896]
    kb = z[:, 896:1024]
    kpe = z[:, 1152:1280]

    @pl.when(t < PROMPT_TILES)
    def _():
        qar_ref[...] = qar.astype(BF16)
        qb_ref[...] = qb.astype(BF16)
        kb_ref[...] = kb
        kpe_ref[...] = kpe

    @pl.when(t >= PROMPT_TILES)
    def _():
        qar_ref[...] = _rope(qar, c32_ref, s32_ref, ROPE_A // 2).astype(BF16)
        qb_ref[...] = _rope(qb, c64_ref, s64_ref, HD // 2).astype(BF16)
        kb_ref[...] = _rope(kb, c64_ref, s64_ref, HD // 2)
        kpe_ref[...] = _rope(kpe, c32_ref, s32_ref, ROPE_A // 2)


def _inproj_ab(xp, xs, mod, w_in, q_norm, kv_norm, w_uq, w_ukv, tabs):
    def tok(w):
        return pl.BlockSpec((TM, w), lambda t: (t, 0))
    widths = [(H_A * (NOPE_A + V_A), BF16), (H_A * NOPE_A, BF16), (H_A * ROPE_A, BF16), (H_B * HD, BF16),
              (KV_LORA, F32), (KVH_B * HD, F32), (KVH_B * HD, F32), (LANES, F32)]
    return pl.pallas_call(
        _inproj_ab_kernel,
        grid=(N_TILES,),
        in_specs=_x_specs() + [_mod_spec(0, 0), _mod_spec(0, 1), _const_spec((D_MODEL, AB_COLS)),
                               _const_spec((1, Q_LORA)), _const_spec((1, KV_LORA)),
                               _const_spec((Q_LORA, H_A * (NOPE_A + ROPE_A))),
                               _const_spec((KV_LORA, H_A * (NOPE_A + V_A)))] + _rope_specs(),
        out_specs=[tok(w) for w, _ in widths],
        out_shape=[jax.ShapeDtypeStruct((N_TOK, w), dt) for w, dt in widths],
        compiler_params=_params(("arbitrary",)),
        name="inproj_ab",
    )(xp, xs, mod, mod, w_in, q_norm, kv_norm, w_uq, w_ukv, *tabs)


def _inproj_c_kernel(x_ref, shift_ref, scale_ref, w_in_ref, g_ref, ones_ref,
                     c64_ref, s64_ref, c32_ref, s32_ref, q_ref, k_ref, v_ref):
    del c32_ref, s32_ref
    t = pl.program_id(0)
    h = (x_ref[...] * (1.0 + scale_ref[...]) + shift_ref[...]).astype(BF16)
    z = jnp.dot(h, w_in_ref[...], preferred_element_type=F32)
    v_ref[...] = z[:, C_NORM_COLS:]
    qk = z[:, :C_NORM_COLS]
    sq = qk * qk
    sq_hi = sq.astype(BF16)
    sq_lo = (sq - sq_hi.astype(F32)).astype(BF16)
    blocks = []
    for j in range(C_NORM_COLS // NORM_BLOCK):
        sl = slice(j * NORM_BLOCK, (j + 1) * NORM_BLOCK)
        blocks.append(jnp.dot(sq_hi[:, sl], ones_ref[...], preferred_element_type=F32)
                      + jnp.dot(sq_lo[:, sl], ones_ref[...], preferred_element_type=F32))
    ss = jnp.concatenate(blocks, axis=1)
    qk = qk * lax.rsqrt(ss * (1.0 / HD) + EPS) * g_ref[...]

    @pl.when(t < PROMPT_TILES)
    def _():
        q_ref[...] = qk[:, :H_C * HD].astype(BF16)
        k_ref[...] = qk[:, H_C * HD:]

    @pl.when(t >= PROMPT_TILES)
    def _():
        q_ref[...] = _rope(qk[:, :H_C * HD], c64_ref, s64_ref, HD // 2).astype(BF16)
        k_ref[...] = _rope(qk[:, H_C * HD:], c64_ref, s64_ref, HD // 2)


def _inproj_c(x, mod, w_in, g_full, tabs):
    def tok(w):
        return pl.BlockSpec((TM, w), lambda t: (t, 0))
    widths = [(H_C * HD, BF16), (KVH_C * HD, F32), (KVH_C * HD, F32)]
    head = np.arange(NORM_BLOCK) // HD
    ones = jnp.asarray(head[:, None] == head[None, :], BF16)
    return pl.pallas_call(
        _inproj_c_kernel,
        grid=(N_TILES,),
        in_specs=[tok(D_MODEL), _mod_spec(1, 0), _mod_spec(1, 1), _const_spec((D_MODEL, C_NORM_COLS + KVH_C * HD)),
                  _const_spec((1, C_NORM_COLS)), _const_spec((NORM_BLOCK, NORM_BLOCK))] + _rope_specs(),
        out_specs=[tok(w) for w, _ in widths],
        out_shape=[jax.ShapeDtypeStruct((N_TOK, w), dt) for w, dt in widths],
        compiler_params=_params(("arbitrary",)),
        name="inproj_c",
    )(x, mod, mod, w_in, g_full, ones, *tabs)


def _qk(q, k):
    return lax.dot_general(q, k, (((1,), (1,)), ((), ())), preferred_element_type=F32)


def _softmax_pv(scores, values, sink=None):
    m = scores[0].max(axis=-1, keepdims=True)
    for s in scores[1:]:
        m = jnp.maximum(m, s.max(axis=-1, keepdims=True))
    if sink is not None:
        m = jnp.maximum(m, sink)
    den = None
    out = None
    for s, v in zip(scores, values):
        p = jnp.exp(s - m)
        d = p.sum(axis=-1, keepdims=True)
        o = jnp.dot(p.astype(BF16), v, preferred_element_type=F32)
        den = d if den is None else den + d
        out = o if out is None else out + o
    if sink is not None:
        den = den + jnp.exp(sink - m)
    return out / den


def _mla_heads(qan_ref, qar_ref, kv_refs, kpe_list, o_ref):
    scale = (NOPE_A + ROPE_A) ** -0.5
    for h in range(H_A):
        qn = qan_ref[:, h * NOPE_A:(h + 1) * NOPE_A]
        qr = qar_ref[:, h * ROPE_A:(h + 1) * ROPE_A]
        scores, values = [], []
        for kv_ref, kpe in zip(kv_refs, kpe_list):
            base = h * (NOPE_A + V_A)
            s = _qk(qn, kv_ref[:, base:base + NOPE_A]) + _qk(qr, kpe)
            scores.append(s * scale)
            values.append(kv_ref[:, base + NOPE_A:base + NOPE_A + V_A])
        o_ref[:, h * V_A:(h + 1) * V_A] = _softmax_pv(scores, values).astype(o_ref.dtype)


def _attn_ab_prompt_kernel(sink_ref, qan_ref, qar_ref, kv_ref, kpe_ref, qb_ref, kb_ref, vb_ref, o_ref):
    kpe = kpe_ref[:, :ROPE_A].astype(BF16)
    _mla_heads(qan_ref, qar_ref, [kv_ref], [kpe], o_ref)
    scale = HD ** -0.5
    for kh in range(KVH_B):
        k = kb_ref[:, kh * HD:(kh + 1) * HD].astype(BF16)
        v = vb_ref[:, kh * HD:(kh + 1) * HD].astype(BF16)
        for g in range(H_B // KVH_B):
            hq = kh * (H_B // KVH_B) + g
            s = _qk(qb_ref[:, hq * HD:(hq + 1) * HD], k) * scale
            col = H_A * V_A + hq * HD
            o_ref[:, col:col + HD] = _softmax_pv([s], [v], sink_ref[hq]).astype(o_ref.dtype)


def _attn_ab_prompt(sink, kv, qan, qar, qb, kb, vb, kpe):
    def blk(w):
        return pl.BlockSpec((SEQ, w), lambda b: (b, 0))
    return pl.pallas_call(
        _attn_ab_prompt_kernel,
        grid=(BATCH,),
        in_specs=[pl.BlockSpec(memory_space=pltpu.SMEM), blk(512), blk(256), blk(1024), blk(LANES), blk(512),
                  blk(128), blk(128)],
        out_specs=blk(D_MODEL),
        out_shape=jax.ShapeDtypeStruct((N_PROMPT, H_A * V_A + H_B * HD), BF16),
        compiler_params=_params(("arbitrary",)),
        name="attn_ab_prompt",
    )(sink, qan, qar, kv, kpe, qb, kb, vb)


def _attn_ab_sample_kernel(sink_ref, qan_ref, qar_ref, kv_ref, kpe_ref, qb_ref, kb_ref, vb_ref,
                           ckv_ctx_ref, kpe_ctx_ref, kb_ctx_ref, vb_ctx_ref, w_ukv_ref, o_ref, kv_ctx_ref):
    i = pl.program_id(1)

    @pl.when(i == 0)
    def _():
        kv_ctx_ref[...] = jnp.dot(ckv_ctx_ref[...].astype(BF16), w_ukv_ref[...],
                                  preferred_element_type=F32).astype(BF16)

    kpe = kpe_ref[:, :ROPE_A].astype(BF16)
    kpe_ctx = kpe_ctx_ref[...].astype(BF16)
    _mla_heads(qan_ref, qar_ref, [kv_ref, kv_ctx_ref], [kpe, kpe_ctx], o_ref)

    n_win = 3 * BQ_S
    start = pl.multiple_of(jnp.clip((i - 1) * BQ_S, 0, DEC_SEQ - n_win), BQ_S)
    qpos = i * BQ_S + lax.broadcasted_iota(jnp.int32, (BQ_S, n_win), 0)
    kpos = start + lax.broadcasted_iota(jnp.int32, (BQ_S, n_win), 1)
    in_band = jnp.abs(qpos - kpos) <= WINDOW
    scale = HD ** -0.5
    kwin = kb_ref[pl.ds(start, n_win), :].astype(BF16)
    vwin = vb_ref[pl.ds(start, n_win), :].astype(BF16)
    kctx = kb_ctx_ref[...].astype(BF16)
    vctx = vb_ctx_ref[...].astype(BF16)
    for kh in range(KVH_B):
        sl = slice(kh * HD, (kh + 1) * HD)
        for g in range(H_B // KVH_B):
            hq = kh * (H_B // KVH_B) + g
            q = qb_ref[:, hq * HD:(hq + 1) * HD]
            s_loc = jnp.where(in_band, _qk(q, kwin[:, sl]) * scale, NEG_INF)
            s_ctx = _qk(q, kctx[:, sl]) * scale
            col = H_A * V_A + hq * HD
            o_ref[:, col:col + HD] = _softmax_pv([s_loc, s_ctx], [vwin[:, sl], vctx[:, sl]],
                                                 sink_ref[hq]).astype(o_ref.dtype)


def _attn_ab_sample(sink, kv, qan, qar, qb, kb, vb, kpe, ckv_ctx, kpe_ctx, kb_ctx, vb_ctx, w_ukv):
    qoff = N_PROMPT // BQ_S
    nq = DEC_SEQ // BQ_S
    boff = N_PROMPT // DEC_SEQ

    def qblk(w):
        return pl.BlockSpec((BQ_S, w), lambda b, i: (qoff + b * nq + i, 0))

    def bblk(w):
        return pl.BlockSpec((DEC_SEQ, w), lambda b, i: (boff + b, 0))

    def cblk(w):
        return pl.BlockSpec((PAST_LEN, w), lambda b, i: (b, 0))

    return pl.pallas_call(
        _attn_ab_sample_kernel,
        grid=(DEC_BATCH, nq),
        in_specs=[pl.BlockSpec(memory_space=pltpu.SMEM), qblk(512), qblk(256), bblk(1024), bblk(LANES), qblk(512),
                  bblk(128), bblk(128), cblk(KV_LORA), cblk(ROPE_A), cblk(128), cblk(128),
                  _const_spec((KV_LORA, H_A * (NOPE_A + V_A)))],
        out_specs=pl.BlockSpec((BQ_S, D_MODEL), lambda b, i: (b * nq + i, 0)),
        out_shape=jax.ShapeDtypeStruct((N_SAMPLE, H_A * V_A + H_B * HD), BF16),
        scratch_shapes=[pltpu.VMEM((PAST_LEN, H_A * (NOPE_A + V_A)), BF16)],
        compiler_params=_params(("arbitrary", "arbitrary")),
        name="attn_ab_sample",
    )(sink, qan, qar, kv, kpe, qb, kb, vb, ckv_ctx, kpe_ctx, kb_ctx, vb_ctx, w_ukv)


def _gqa_heads(q_ref, k_list, v_list, o_ref):
    scale = HD ** -0.5
    for kh in range(KVH_C):
        sl = slice(kh * HD, (kh + 1) * HD)
        ks = [k[:, sl] for k in k_list]
        vs = [v[:, sl] for v in v_list]
        for g in range(H_C // KVH_C):
            hq = kh * (H_C // KVH_C) + g
            q = q_ref[:, hq * HD:(hq + 1) * HD]
            scores = [_qk(q, k) * scale for k in ks]
            o_ref[:, hq * HD:(hq + 1) * HD] = _softmax_pv(scores, vs).astype(o_ref.dtype)


def _attn_c_prompt_kernel(q_ref, k_ref, v_ref, o_ref):
    _gqa_heads(q_ref, [k_ref[...].astype(BF16)], [v_ref[...].astype(BF16)], o_ref)


def _attn_c_prompt(q, k, v):
    def blk(w):
        return pl.BlockSpec((SEQ, w), lambda b: (b, 0))
    return pl.pallas_call(
        _attn_c_prompt_kernel,
        grid=(BATCH,),
        in_specs=[blk(H_C * HD), blk(KVH_C * HD), blk(KVH_C * HD)],
        out_specs=blk(H_C * HD),
        out_shape=jax.ShapeDtypeStruct((N_PROMPT, H_C * HD), BF16),
        compiler_params=_params(("arbitrary",)),
        name="attn_c_prompt",
    )(q, k, v)


def _attn_c_sample_kernel(q_ref, k_ref, v_ref, kc_ref, vc_ref, o_ref):
    _gqa_heads(q_ref, [k_ref[...].astype(BF16), kc_ref[...].astype(BF16)],
               [v_ref[...].astype(BF16), vc_ref[...].astype(BF16)], o_ref)


def _attn_c_sample(q, k, v, k_ctx, v_ctx):
    qoff = N_PROMPT // BQ_S
    nq = DEC_SEQ // BQ_S
    boff = N_PROMPT // DEC_SEQ
    w = KVH_C * HD
    return pl.pallas_call(
        _attn_c_sample_kernel,
        grid=(DEC_BATCH, nq),
        in_specs=[pl.BlockSpec((BQ_S, H_C * HD), lambda b, i: (qoff + b * nq + i, 0)),
                  pl.BlockSpec((DEC_SEQ, w), lambda b, i: (boff + b, 0)),
                  pl.BlockSpec((DEC_SEQ, w), lambda b, i: (boff + b, 0)),
                  pl.BlockSpec((PAST_LEN, w), lambda b, i: (b, 0)),
                  pl.BlockSpec((PAST_LEN, w), lambda b, i: (b, 0))],
        out_specs=pl.BlockSpec((BQ_S, H_C * HD), lambda b, i: (b * nq + i, 0)),
        out_shape=jax.ShapeDtypeStruct((N_SAMPLE, H_C * HD), BF16),
        compiler_params=_params(("arbitrary", "arbitrary")),
        name="attn_c_sample",
    )(q, k, v, k_ctx, v_ctx)


def _layer_norm(y, g, b):
    mu = jnp.mean(y, axis=-1, keepdims=True)
    yc = y - mu
    var = jnp.mean(yc * yc, axis=-1, keepdims=True)
    return yc * lax.rsqrt(var + EPS) * g + b


def _route(sel, aff):
    def row(a, j):
        return a[j:j + 1, :]

    scores = []
    for g in range(N_GROUPS):
        a0, a1, a2, a3 = (row(sel, 4 * g + j) for j in range(4))
        hi01, lo01 = jnp.maximum(a0, a1), jnp.minimum(a0, a1)
        hi23, lo23 = jnp.maximum(a2, a3), jnp.minimum(a2, a3)
        top1 = jnp.maximum(hi01, hi23)
        top2 = jnp.maximum(jnp.minimum(hi01, hi23), jnp.maximum(lo01, lo23))
        scores.append(top1 + top2)
    best = scores[0]
    gi = jnp.zeros(best.shape, jnp.int32)
    for g in range(1, N_GROUPS):
        better = scores[g] > best
        gi = jnp.where(better, g, gi)
        best = jnp.where(better, scores[g], best)

    def in_group(a, j):
        out = row(a, j)
        for g in range(1, N_GROUPS):
            out = jnp.where(gi == g, row(a, 4 * g + j), out)
        return out

    v = [in_group(sel, j) for j in range(4)]
    a = [in_group(aff, j) for j in range(4)]
    chosen = []
    for j in range(4):
        rank = jnp.zeros(best.shape, jnp.int32)
        for k in range(4):
            if k == j:
                continue
            ahead = (v[k] >= v[j]) if k < j else (v[k] > v[j])
            rank = rank + ahead.astype(jnp.int32)
        chosen.append(rank < 2)
    total = sum(jnp.where(chosen[j], a[j], 0.0) for j in range(4))
    w = [jnp.where(chosen[j], a[j], 0.0) / total for j in range(4)]
    pair = jnp.zeros(best.shape, jnp.int32)
    for p, (ja, jb) in enumerate(zip(SLOT_A_LOCAL, SLOT_B_LOCAL)):
        pair = jnp.where(chosen[ja] & chosen[jb], p, pair)
    gate_a = jnp.where(pair == 0, w[0], jnp.where(pair <= 2, w[2], w[3]))
    gate_b = jnp.where((pair == 0) | (pair == 1) | (pair == 4), w[1], jnp.where(pair == 5, w[2], w[0]))
    return gi * PAIRS_PER_GROUP + pair, gate_a, gate_b


def _post_attn_kernel(split_x, *refs):
    t = pl.program_id(0)
    op_ref, os_ref = refs[:2]
    if split_x:
        x = _pick(t, refs[2], refs[3])
        refs = refs[4:]
    else:
        x = refs[2][...]
        refs = refs[3:]
    (w_out_ref, gate_ref, shift_ref, scale_ref, lng_ref, lnb_ref, rw_ref, rb_ref, tri_ref,
     x1_ref, row_ref, route_ref, counts_ref, carry_ref) = refs
    o = _pick(t, op_ref, os_ref)
    a = jnp.dot(o, w_out_ref[...], preferred_element_type=F32)
    x1 = _layer_norm(ALPHA * x + gate_ref[...] * a, lng_ref[...], lnb_ref[...])
    x1_ref[...] = x1
    h2 = x1 * (1.0 + scale_ref[...]) + shift_ref[...]
    row_ref[:, :D_MODEL] = h2
    logits = lax.dot_general(rw_ref[...], h2, (((1,), (1,)), ((), ())), preferred_element_type=F32,
                             precision=lax.Precision.HIGHEST)
    aff = 1.0 / (1.0 + jnp.exp(-logits))
    bucket, gate_a, gate_b = _route(aff + rb_ref[...], aff)

    lane = lax.broadcasted_iota(jnp.int32, (LANES, TM), 0)
    meta_t = jnp.where(lane == 0, gate_a, jnp.where(lane == 1, gate_b, 0.0))
    row_ref[:, D_MODEL:] = meta_t.T

    @pl.when(t == 0)
    def _():
        carry_ref[...] = jnp.zeros_like(carry_ref)

    onehot = lax.broadcasted_iota(jnp.int32, (BUCKET_ROWS, TM), 0) == bucket
    prefix = jnp.dot(onehot.astype(BF16), tri_ref[...], preferred_element_type=F32)
    carry = carry_ref[...]
    rank = jnp.sum(jnp.where(onehot, prefix + carry[:, 0:1], 0.0), axis=0, keepdims=True)
    carry = carry + jnp.sum(onehot.astype(F32), axis=1, keepdims=True)
    carry_ref[...] = carry
    counts_ref[...] = carry
    route_ref[...] = jnp.zeros((8, TM), F32)
    route_ref[0:1, :] = bucket.astype(F32)
    route_ref[1:2, :] = rank


def _post_attn(layer, o_p, o_s, x_list, mod, w_out, ln_g, ln_b, rw_t, rb):
    split_x = len(x_list) == 2
    tok = pl.BlockSpec((TM, D_MODEL), lambda t: (t, 0))
    o_specs = [
        pl.BlockSpec((TM, D_MODEL), lambda t: (jnp.minimum(t, PROMPT_TILES - 1), 0)),
        pl.BlockSpec((TM, D_MODEL), lambda t: (jnp.maximum(t - PROMPT_TILES, 0), 0)),
    ]
    tri = jnp.asarray(np.arange(TM)[:, None] < np.arange(TM)[None, :], BF16)
    return pl.pallas_call(
        functools.partial(_post_attn_kernel, split_x),
        grid=(N_TILES,),
        in_specs=o_specs + (_x_specs() if split_x else [tok]) + [
            _const_spec((D_MODEL, D_MODEL)), _mod_spec(layer, 2), _mod_spec(layer, 3), _mod_spec(layer, 4),
            _const_spec((1, D_MODEL)), _const_spec((1, D_MODEL)), _const_spec((N_EXPERTS, D_MODEL)),
            _const_spec((N_EXPERTS, 1)), _const_spec((TM, TM))],
        out_specs=[tok, pl.BlockSpec((TM, ROW_W), lambda t: (t, 0)), pl.BlockSpec((8, TM), lambda t: (0, t)),
                   _const_spec((BUCKET_ROWS, LANES))],
        out_shape=[jax.ShapeDtypeStruct((N_TOK, D_MODEL), F32), jax.ShapeDtypeStruct((N_TOK, ROW_W), F32),
                   jax.ShapeDtypeStruct((8, N_TOK), F32), jax.ShapeDtypeStruct((BUCKET_ROWS, LANES), F32)],
        scratch_shapes=[pltpu.VMEM((BUCKET_ROWS, LANES), F32)],
        compiler_params=_params(("arbitrary",)),
        name="post_attn",
    )(o_p, o_s, *x_list, w_out, mod, mod, mod, ln_g, ln_b, rw_t, rb, tri)


def _moe_kernel(ta_ref, tb_ref, nt_ref, x_ref, wga_ref, wua_ref, wda_ref, wgb_ref, wub_ref, wdb_ref, o_ref,
                wga_s, wua_s, wda_s, wgb_s, wub_s, wdb_s):
    i = pl.program_id(0)
    prev = jnp.maximum(i - 1, 0)
    slots = ((ta_ref, (wga_ref, wua_ref, wda_ref), (wga_s, wua_s, wda_s)),
             (tb_ref, (wgb_ref, wub_ref, wdb_ref), (wgb_s, wub_s, wdb_s)))

    for t_ref, w_refs, w_scr in slots:
        @pl.when((i == 0) | (t_ref[i] != t_ref[prev]))
        def _(w_refs=w_refs, w_scr=w_scr):
            for w_ref, s_ref in zip(w_refs, w_scr):
                s_ref[...] = w_ref[...].astype(BF16)

    @pl.when(i < nt_ref[0])
    def _():
        x = x_ref[:, :D_MODEL].astype(BF16)
        acc = None
        for slot, (_, _, (wg_s, wu_s, wd_s)) in enumerate(slots):
            gt = jnp.dot(x, wg_s[...], preferred_element_type=F32)
            up = jnp.dot(x, wu_s[...], preferred_element_type=F32)
            gate = x_ref[:, D_MODEL + slot:D_MODEL + slot + 1]
            hid = (gt / (1.0 + jnp.exp(-gt))) * up * gate
            y = jnp.dot(hid.astype(BF16), wd_s[...], preferred_element_type=F32)
            acc = y if acc is None else acc + y
        o_ref[...] = acc

    @pl.when(i >= nt_ref[0])
    def _():
        o_ref[...] = jnp.zeros_like(o_ref)


def _moe(tile_a, tile_b, n_tiles, rows, w_gate, w_up, w_down):
    def row_idx(i, ta, tb, nt):
        return (jnp.minimum(i, nt[0] - 1), 0)

    def wspec(shape, which):
        if which == 0:
            return pl.BlockSpec((None,) + shape, lambda i, ta, tb, nt: (ta[i], 0, 0))
        return pl.BlockSpec((None,) + shape, lambda i, ta, tb, nt: (tb[i], 0, 0))

    up_shape, down_shape = (D_MODEL, D_EXPERT), (D_EXPERT, D_MODEL)
    grid_spec = pltpu.PrefetchScalarGridSpec(
        num_scalar_prefetch=3,
        grid=(MOE_TILES,),
        in_specs=[pl.BlockSpec((TME, ROW_W), row_idx),
                  wspec(up_shape, 0), wspec(up_shape, 0), wspec(down_shape, 0),
                  wspec(up_shape, 1), wspec(up_shape, 1), wspec(down_shape, 1)],
        out_specs=pl.BlockSpec((TME, D_MODEL), lambda i, ta, tb, nt: (i, 0)),
        scratch_shapes=[pltpu.VMEM(s, BF16) for s in (up_shape, up_shape, down_shape) * 2],
    )
    return pl.pallas_call(
        _moe_kernel,
        grid_spec=grid_spec,
        out_shape=jax.ShapeDtypeStruct((MOE_ROWS, D_MODEL), F32),
        compiler_params=_params(("arbitrary",)),
        name="moe_experts",
    )(tile_a, tile_b, n_tiles, rows, w_gate, w_up, w_down, w_gate, w_up, w_down)


def _row_copy(src_ref, src_row, dst_ref, dst_row, sem):
    return pltpu.make_async_copy(src_ref.at[pl.ds(src_row, 1), :], dst_ref.at[pl.ds(dst_row, 1), :], sem)


def _dispatch_kernel(pos_ref, pend_ref, cnt_ref, nt_ref, src_ref, out_ref, zero_ref, sem, zsem):
    zero_ref[...] = jnp.zeros_like(zero_ref)

    def zero_tile(row0):
        return pltpu.make_async_copy(zero_ref, out_ref.at[pl.ds(pl.multiple_of(row0, TME), TME), :], zsem)

    for b in range(N_BUCKETS):
        @pl.when(cnt_ref[b] > 0)
        def _(b=b):
            zero_tile(pend_ref[b] - TME).start()

    def start_unused(i, carry):
        zero_tile(i * TME).start()
        return carry
    lax.fori_loop(nt_ref[0], MOE_TILES, start_unused, 0)

    for b in range(N_BUCKETS):
        @pl.when(cnt_ref[b] > 0)
        def _(b=b):
            zero_tile(pend_ref[b] - TME).wait()

    def wait_unused(i, carry):
        zero_tile(i * TME).wait()
        return carry
    lax.fori_loop(nt_ref[0], MOE_TILES, wait_unused, 0)

    def chunk(c, carry):
        def issue(r, carry2):
            t = c * DMA_CHUNK + r
            _row_copy(src_ref, t, out_ref, pos_ref[t], sem).start()
            return carry2
        lax.fori_loop(0, DMA_CHUNK, issue, 0, unroll=8)

        @pl.when(c > 0)
        def _():
            pltpu.make_async_copy(src_ref.at[pl.ds(0, DMA_CHUNK), :], out_ref.at[pl.ds(0, DMA_CHUNK), :], sem).wait()
        return carry
    lax.fori_loop(0, N_TOK // DMA_CHUNK, chunk, 0)
    pltpu.make_async_copy(src_ref.at[pl.ds(0, DMA_CHUNK), :], out_ref.at[pl.ds(0, DMA_CHUNK), :], sem).wait()


def _dispatch(pos, pend, counts, n_tiles, rows):
    return pl.pallas_call(
        _dispatch_kernel,
        grid_spec=pltpu.PrefetchScalarGridSpec(
            num_scalar_prefetch=4, grid=(1,),
            in_specs=[pl.BlockSpec(memory_space=pl.ANY)],
            out_specs=pl.BlockSpec(memory_space=pl.ANY),
            scratch_shapes=[pltpu.VMEM((TME, ROW_W), F32), pltpu.SemaphoreType.DMA, pltpu.SemaphoreType.DMA]),
        out_shape=jax.ShapeDtypeStruct((MOE_ROWS, ROW_W), F32),
        compiler_params=_params(("arbitrary",)),
        name="dispatch_rows",
    )(pos, pend, counts, n_tiles, rows)


def _plan(route, counts):
    bucket = route[0].astype(jnp.int32)
    rank = route[1].astype(jnp.int32)
    counts = counts[:N_BUCKETS, 0].astype(jnp.int32)
    padded = ((counts + TME - 1) // TME) * TME
    pend = jnp.cumsum(padded)
    pstart = pend - padded
    ids = jnp.arange(N_BUCKETS, dtype=jnp.int32)
    pos = rank + jnp.sum(jnp.where(bucket[None, :] == ids[:, None], pstart[:, None], 0), axis=0)
    n_tiles = pend[-1] // TME
    tile_start = jnp.minimum(jnp.arange(MOE_TILES, dtype=jnp.int32), n_tiles - 1) * TME
    tile_bucket = jnp.minimum(jnp.sum(tile_start[:, None] >= pend[None, :], axis=1), N_BUCKETS - 1)
    group, pair = tile_bucket // PAIRS_PER_GROUP, tile_bucket % PAIRS_PER_GROUP
    slot_a = jnp.asarray(SLOT_A_LOCAL, jnp.int32)
    slot_b = jnp.asarray(SLOT_B_LOCAL, jnp.int32)
    pair_hot = pair[:, None] == jnp.arange(PAIRS_PER_GROUP, dtype=jnp.int32)[None, :]
    tile_a = group * EXPERTS_PER_GROUP + jnp.sum(jnp.where(pair_hot, slot_a[None, :], 0), axis=1)
    tile_b = group * EXPERTS_PER_GROUP + jnp.sum(jnp.where(pair_hot, slot_b[None, :], 0), axis=1)
    return pos, pend, counts, tile_a.astype(jnp.int32), tile_b.astype(jnp.int32), n_tiles.reshape(1)


def _post_moe_kernel(split_out, pos_ref, f_ref, x1_ref, gate_ref, lng_ref, lnb_ref, *refs):
    out_refs, (fbuf, sem) = refs[:-2], refs[-2:]
    t = pl.program_id(0)

    def gather(tile, slot):
        def issue(r, carry):
            _row_copy(f_ref, pos_ref[tile * TM + r], fbuf.at[slot], r, sem.at[slot]).start()
            return carry
        lax.fori_loop(0, TM, issue, 0, unroll=8)

    @pl.when(t == 0)
    def _():
        gather(0, 0)

    @pl.when(t + 1 < N_TILES)
    def _():
        gather(t + 1, (t + 1) % 2)

    slot = t % 2
    pltpu.make_async_copy(f_ref.at[pl.ds(0, TM), :], fbuf.at[slot], sem.at[slot]).wait()
    y = _layer_norm(ALPHA * x1_ref[...] + gate_ref[...] * fbuf[slot], lng_ref[...], lnb_ref[...])
    if split_out:
        @pl.when(t < PROMPT_TILES)
        def _():
            out_refs[0][...] = y

        @pl.when(t >= PROMPT_TILES)
        def _():
            out_refs[1][...] = y
    else:
        out_refs[0][...] = y


def _post_moe(layer, pos, f_sorted, x1, mod, ln_g, ln_b, split_out):
    tok = pl.BlockSpec((TM, D_MODEL), lambda t, p: (t, 0))
    if split_out:
        out_specs = [pl.BlockSpec((TM, D_MODEL), lambda t, p: (jnp.minimum(t, PROMPT_TILES - 1), 0)),
                     pl.BlockSpec((TM, D_MODEL), lambda t, p: (jnp.maximum(t - PROMPT_TILES, 0), 0))]
        out_shape = [jax.ShapeDtypeStruct((N_PROMPT, D_MODEL), F32), jax.ShapeDtypeStruct((N_SAMPLE, D_MODEL), F32)]
    else:
        out_specs = [tok]
        out_shape = [jax.ShapeDtypeStruct((N_TOK, D_MODEL), F32)]
    mod_spec = pl.BlockSpec((None, None, None, 1, D_MODEL), lambda t, p: (layer, _mod_row(t), 5, 0, 0))
    return pl.pallas_call(
        functools.partial(_post_moe_kernel, split_out),
        grid_spec=pltpu.PrefetchScalarGridSpec(
            num_scalar_prefetch=1, grid=(N_TILES,),
            in_specs=[pl.BlockSpec(memory_space=pl.ANY), tok, mod_spec,
                      pl.BlockSpec((1, D_MODEL), lambda t, p: (0, 0)), pl.BlockSpec((1, D_MODEL), lambda t, p: (0, 0))],
            out_specs=out_specs,
            scratch_shapes=[pltpu.VMEM((2, TM, D_MODEL), F32), pltpu.SemaphoreType.DMA((2,))]),
        out_shape=out_shape,
        compiler_params=_params(("arbitrary",)),
        name="post_moe",
    )(pos, f_sorted, x1, mod, ln_g, ln_b)


def _ffn(layer, o_p, o_s, x_list, mod, w_out, ln_g, ln_b, rw_t, rb, w_gate, w_up, w_down, split_out):
    x1, rows, route, counts = _post_attn(layer, o_p, o_s, x_list, mod, w_out, ln_g[layer, 0][None],
                                         ln_b[layer, 0][None], rw_t, rb)
    pos, pend, counts, tile_a, tile_b, n_tiles = _plan(route, counts)
    rows_sorted = _dispatch(pos, pend, counts, n_tiles, rows)
    f_sorted = _moe(tile_a, tile_b, n_tiles, rows_sorted, w_gate, w_up, w_down)
    return _post_moe(layer, pos, f_sorted, x1, mod, ln_g[layer, 1][None], ln_b[layer, 1][None], split_out)


def kernel(x_prompt, x_sample, c, cache_mla_ckv, cache_mla_kpe, cache_swa_k, cache_swa_v, cache_gqa_k, cache_gqa_v, c_ctx, w_mod, b_mod, ln_g, ln_b, w_in_ab, mla_q_norm, mla_w_uq, mla_kv_norm, mla_w_ukv, swa_sink, w_out_ab, w_in_c, gqa_q_norm, gqa_k_norm, w_out_c, router_w, router_bias, exp_w_gate, exp_w_up, exp_w_down):
    xp = x_prompt.reshape(N_PROMPT, D_MODEL)
    xs = x_sample.reshape(N_SAMPLE, D_MODEL)
    cond = jnp.concatenate([c_ctx[None], c, jnp.zeros((MOD_ROWS - 1 - DEC_BATCH, D_MODEL), F32)], axis=0)
    mod = _modulation(cond, w_mod, b_mod).reshape(DEPTH, MOD_ROWS, 6, 1, D_MODEL)
    tabs = _rope_cs(HD // 2) + _rope_cs(ROPE_A // 2)
    rw_t = router_w.T
    rb = router_bias.reshape(N_EXPERTS, 1)

    w = w_in_ab[0]
    pad = jnp.zeros((D_MODEL, AB_COLS - w.shape[1]), F32)
    w_in = jnp.concatenate([w[:, :384], w[:, 416:], w[:, 384:416], pad], axis=1).astype(BF16)
    w_uq = mla_w_uq[0].reshape(Q_LORA, H_A, NOPE_A + ROPE_A)
    w_uq = jnp.concatenate([w_uq[:, :, :NOPE_A].reshape(Q_LORA, -1), w_uq[:, :, NOPE_A:].reshape(Q_LORA, -1)],
                           axis=1).astype(BF16)
    w_ukv = mla_w_ukv[0].astype(BF16)
    kv, qan, qar, qb, ckv, kb, vb, kpe = _inproj_ab(xp, xs, mod, w_in, mla_q_norm[0][None], mla_kv_norm[0][None],
                                                    w_uq, w_ukv, tabs)
    sink = swa_sink[0]
    o_p = _attn_ab_prompt(sink, kv, qan, qar, qb, kb, vb, kpe)
    o_s = _attn_ab_sample(sink, kv, qan, qar, qb, kb, vb, kpe,
                          cache_mla_ckv[:, 0].reshape(-1, KV_LORA), cache_mla_kpe[:, 0].reshape(-1, ROPE_A),
                          cache_swa_k[:, 0].reshape(-1, KVH_B * HD), cache_swa_v[:, 0].reshape(-1, KVH_B * HD), w_ukv)
    (x2,) = _ffn(0, o_p, o_s, [xp, xs], mod, w_out_ab[0].astype(BF16), ln_g, ln_b, rw_t, rb,
                 exp_w_gate[0], exp_w_up[0], exp_w_down[0], split_out=False)

    g_full = jnp.concatenate([jnp.tile(gqa_q_norm[0], H_C), jnp.tile(gqa_k_norm[0], KVH_C)])[None]
    qc, kc, vc = _inproj_c(x2, mod, w_in_c[0].astype(BF16), g_full, tabs)
    oc_p = _attn_c_prompt(qc, kc, vc)
    oc_s = _attn_c_sample(qc, kc, vc, cache_gqa_k[:, 0].reshape(-1, KVH_C * HD),
                          cache_gqa_v[:, 0].reshape(-1, KVH_C * HD))
    y_p, y_s = _ffn(1, oc_p, oc_s, [x2], mod, w_out_c[0].astype(BF16), ln_g, ln_b, rw_t, rb,
                    exp_w_gate[1], exp_w_up[1], exp_w_down[1], split_out=True)

    y_prompt = y_p.reshape(BATCH, SEQ, D_MODEL)
    y_sample = y_s.reshape(DEC_BATCH, DEC_SEQ, D_MODEL)
    new_ckv = ckv[:N_PROMPT].reshape(BATCH, 1, SEQ, KV_LORA)
    new_kpe = kpe[:N_PROMPT, :ROPE_A].reshape(BATCH, 1, SEQ, ROPE_A)
    new_swk = kb[:N_PROMPT].reshape(BATCH, 1, SEQ, KVH_B, HD)
    new_swv = vb[:N_PROMPT].reshape(BATCH, 1, SEQ, KVH_B, HD)
    new_gk = kc[:N_PROMPT].reshape(BATCH, 1, SEQ, KVH_C, HD)
    new_gv = vc[:N_PROMPT].reshape(BATCH, 1, SEQ, KVH_C, HD)
    return y_prompt, y_sample, new_ckv, new_kpe, new_swk, new_swv, new_gk, new_gv
```

```python
import functools

import numpy as np
import jax
import jax.numpy as jnp
from jax import lax
from jax.experimental import pallas as pl
from jax.experimental.pallas import tpu as pltpu

D_MODEL = 1024
BATCH = 32
SEQ = 256
DEPTH = 2
DEC_BATCH = 4
DEC_SEQ = 1024
PAST_LEN = 256
GRID_W = 64
ROPE_THETA = 10000.0
WINDOW = 128
HD = 64
H_A = 8
NOPE_A = 64
ROPE_A = 32
V_A = 64
Q_LORA = 256
KV_LORA = 128
H_B = 8
KVH_B = 2
H_C = 16
KVH_C = 4
N_EXPERTS = 16
N_GROUPS = 4
EXPERTS_PER_GROUP = 4
D_EXPERT = 512
ALPHA = (2 * DEPTH) ** 0.25
NEG_INF = -1e30
EPS = 1e-6

LANES = 128
N_PROMPT = BATCH * SEQ
N_SAMPLE = DEC_BATCH * DEC_SEQ
N_TOK = N_PROMPT + N_SAMPLE
TM = 256
N_TILES = N_TOK // TM
PROMPT_TILES = N_PROMPT // TM
SAMPLE_TILES_PER_BATCH = DEC_SEQ // TM
BQ_S = 128
MOD_ROWS = 8

PAIRS_PER_GROUP = 6
N_BUCKETS = N_GROUPS * PAIRS_PER_GROUP
SLOT_A_LOCAL = (0, 2, 2, 3, 3, 3)
SLOT_B_LOCAL = (1, 1, 0, 0, 1, 2)
TME = 256
MOE_TILES = -(-(N_TOK + N_BUCKETS * (TME - 1)) // TME)
MOE_ROWS = MOE_TILES * TME

BUCKET_ROWS = 32
ROW_W = D_MODEL + LANES

AB_COLS = 1280
C_NORM_COLS = H_C * HD + KVH_C * HD
NORM_BLOCK = 256

BF16 = jnp.bfloat16
F32 = jnp.float32
VMEM_LIMIT = 52 * 1024 * 1024


def _mod_row(t):
    return jnp.where(t < PROMPT_TILES, 0, 1 + (t - PROMPT_TILES) // SAMPLE_TILES_PER_BATCH)


def _mod_spec(layer, chunk):
    return pl.BlockSpec((None, None, None, 1, D_MODEL), lambda t: (layer, _mod_row(t), chunk, 0, 0))


def _const_spec(shape):
    nd = len(shape)
    return pl.BlockSpec(shape, lambda *_: (0,) * nd)


def _x_specs():
    return [
        pl.BlockSpec((TM, D_MODEL), lambda t: (jnp.minimum(t, PROMPT_TILES - 1), 0)),
        pl.BlockSpec((TM, D_MODEL), lambda t: (jnp.maximum(t - PROMPT_TILES, 0), 0)),
    ]


def _pick(t, a_ref, b_ref):
    return jnp.where(t < PROMPT_TILES, a_ref[...], b_ref[...])


def _params(sem):
    return pltpu.CompilerParams(dimension_semantics=sem, vmem_limit_bytes=VMEM_LIMIT)


def _mod_kernel(cond_ref, w_ref, b_ref, o_ref):
    c = cond_ref[...]
    s = (c / (1.0 + jnp.exp(-c))).astype(BF16)
    o_ref[...] = jnp.dot(s, w_ref[...].astype(BF16), preferred_element_type=F32) + b_ref[...]


def _modulation(cond, w_mod, b_mod):
    tn = 1536
    return pl.pallas_call(
        _mod_kernel,
        grid=(DEPTH, 6 * D_MODEL // tn),
        in_specs=[
            pl.BlockSpec((MOD_ROWS, D_MODEL), lambda l, j: (0, 0)),
            pl.BlockSpec((None, D_MODEL, tn), lambda l, j: (l, 0, j)),
            pl.BlockSpec((None, 1, tn), lambda l, j: (l, 0, j)),
        ],
        out_specs=pl.BlockSpec((None, MOD_ROWS, tn), lambda l, j: (l, 0, j)),
        out_shape=jax.ShapeDtypeStruct((DEPTH, MOD_ROWS, 6 * D_MODEL), F32),
        compiler_params=_params(("arbitrary", "arbitrary")),
        name="modulation",
    )(cond, w_mod, b_mod.reshape(DEPTH, 1, 6 * D_MODEL))


def _rope_cs(half):
    n_freq = half // 2
    rows = DEC_SEQ // GRID_W
    row = jnp.repeat(jnp.arange(rows, dtype=F32), GRID_W)
    col = jnp.tile(jnp.arange(GRID_W, dtype=F32), rows)
    inv = 1.0 / (ROPE_THETA ** (jnp.arange(n_freq, dtype=F32) / n_freq))
    ang = jnp.concatenate([row[:, None] * inv, col[:, None] * inv], axis=-1)
    cos, sin = jnp.cos(ang), jnp.sin(ang)
    reps = LANES // (2 * half)
    c = jnp.tile(jnp.concatenate([cos, cos], axis=-1), (1, reps))
    s = jnp.tile(jnp.concatenate([-sin, sin], axis=-1), (1, reps))
    return c, s


def _rope(x, c_ref, s_ref, half):
    w = x.shape[1]
    reps = w // LANES
    c = c_ref[...]
    s = s_ref[...]
    if reps > 1:
        c = jnp.concatenate([c] * reps, axis=1)
        s = jnp.concatenate([s] * reps, axis=1)
    ahead = pltpu.roll(x, w - half, 1)
    behind = pltpu.roll(x, half, 1)
    lane = lax.broadcasted_iota(jnp.int32, x.shape, 1)
    swapped = jnp.where((lane & (2 * half - 1)) < half, ahead, behind)
    return x * c + swapped * s


def _rope_specs():
    def idx(t):
        return (jnp.maximum(t - PROMPT_TILES, 0) % SAMPLE_TILES_PER_BATCH, 0)
    return [pl.BlockSpec((TM, LANES), idx)] * 4


def _inproj_ab_kernel(xp_ref, xs_ref, shift_ref, scale_ref, w_in_ref, qn_ref, kvn_ref, w_uq_ref, w_ukv_ref,
                      c64_ref, s64_ref, c32_ref, s32_ref,
                      kv_ref, qan_ref, qar_ref, qb_ref, ckv_ref, kb_ref, vb_ref, kpe_ref):
    t = pl.program_id(0)
    x = _pick(t, xp_ref, xs_ref)
    h = (x * (1.0 + scale_ref[...]) + shift_ref[...]).astype(BF16)
    z = jnp.dot(h, w_in_ref[...], preferred_element_type=F32)

    ql = z[:, :Q_LORA]
    ql = ql * lax.rsqrt(jnp.mean(ql * ql, axis=-1, keepdims=True) + EPS) * qn_ref[...]
    qa = jnp.dot(ql.astype(BF16), w_uq_ref[...], preferred_element_type=F32)
    c = z[:, Q_LORA:Q_LORA + KV_LORA]
    ckv = c * lax.rsqrt(jnp.mean(c * c, axis=-1, keepdims=True) + EPS) * kvn_ref[...]
    ckv_ref[...] = ckv
    kv_ref[...] = jnp.dot(ckv.astype(BF16), w_ukv_ref[...], preferred_element_type=F32).astype(BF16)
    vb_ref[...] = z[:, 1024:1152]
    qan_ref[...] = qa[:, :H_A * NOPE_A].astype(BF16)

    qar = qa[:, H_A * NOPE_A:]
    qb = z[:, 384:896]
    kb = z[:, 896:1024]
    kpe = z[:, 1152:1280]

    @pl.when(t < PROMPT_TILES)
    def _():
        qar_ref[...] = qar.astype(BF16)
        qb_ref[...] = qb.astype(BF16)
        kb_ref[...] = kb
        kpe_ref[...] = kpe

    @pl.when(t >= PROMPT_TILES)
    def _():
        qar_ref[...] = _rope(qar, c32_ref, s32_ref, ROPE_A // 2).astype(BF16)
        qb_ref[...] = _rope(qb, c64_ref, s64_ref, HD // 2).astype(BF16)
        kb_ref[...] = _rope(kb, c64_ref, s64_ref, HD // 2)
        kpe_ref[...] = _rope(kpe, c32_ref, s32_ref, ROPE_A // 2)


def _inproj_ab(xp, xs, mod, w_in, q_norm, kv_norm, w_uq, w_ukv, tabs):
    def tok(w):
        return pl.BlockSpec((TM, w), lambda t: (t, 0))
    widths = [(H_A * (NOPE_A + V_A), BF16), (H_A * NOPE_A, BF16), (H_A * ROPE_A, BF16), (H_B * HD, BF16),
              (KV_LORA, F32), (KVH_B * HD, F32), (KVH_B * HD, F32), (LANES, F32)]
    return pl.pallas_call(
        _inproj_ab_kernel,
        grid=(N_TILES,),
        in_specs=_x_specs() + [_mod_spec(0, 0), _mod_spec(0, 1), _const_spec((D_MODEL, AB_COLS)),
                               _const_spec((1, Q_LORA)), _const_spec((1, KV_LORA)),
                               _const_spec((Q_LORA, H_A * (NOPE_A + ROPE_A))),
                               _const_spec((KV_LORA, H_A * (NOPE_A + V_A)))] + _rope_specs(),
        out_specs=[tok(w) for w, _ in widths],
        out_shape=[jax.ShapeDtypeStruct((N_TOK, w), dt) for w, dt in widths],
        compiler_params=_params(("arbitrary",)),
        name="inproj_ab",
    )(xp, xs, mod, mod, w_in, q_norm, kv_norm, w_uq, w_ukv, *tabs)


def _inproj_c_kernel(x_ref, shift_ref, scale_ref, w_in_ref, g_ref, ones_ref,
                     c64_ref, s64_ref, c32_ref, s32_ref, q_ref, k_ref, v_ref):
    del c32_ref, s32_ref
    t = pl.program_id(0)
    h = (x_ref[...] * (1.0 + scale_ref[...]) + shift_ref[...]).astype(BF16)
    z = jnp.dot(h, w_in_ref[...], preferred_element_type=F32)
    v_ref[...] = z[:, C_NORM_COLS:]
    qk = z[:, :C_NORM_COLS]
    sq = qk * qk
    sq_hi = sq.astype(BF16)
    sq_lo = (sq - sq_hi.astype(F32)).astype(BF16)
    blocks = []
    for j in range(C_NORM_COLS // NORM_BLOCK):
        sl = slice(j * NORM_BLOCK, (j + 1) * NORM_BLOCK)
        blocks.append(jnp.dot(sq_hi[:, sl], ones_ref[...], preferred_element_type=F32)
                      + jnp.dot(sq_lo[:, sl], ones_ref[...], preferred_element_type=F32))
    ss = jnp.concatenate(blocks, axis=1)
    qk = qk * lax.rsqrt(ss * (1.0 / HD) + EPS) * g_ref[...]

    @pl.when(t < PROMPT_TILES)
    def _():
        q_ref[...] = qk[:, :H_C * HD].astype(BF16)
        k_ref[...] = qk[:, H_C * HD:]

    @pl.when(t >= PROMPT_TILES)
    def _():
        q_ref[...] = _rope(qk[:, :H_C * HD], c64_ref, s64_ref, HD // 2).astype(BF16)
        k_ref[...] = _rope(qk[:, H_C * HD:], c64_ref, s64_ref, HD // 2)


def _inproj_c(x, mod, w_in, g_full, tabs):
    def tok(w):
        return pl.BlockSpec((TM, w), lambda t: (t, 0))
    widths = [(H_C * HD, BF16), (KVH_C * HD, F32), (KVH_C * HD, F32)]
    head = np.arange(NORM_BLOCK) // HD
    ones = jnp.asarray(head[:, None] == head[None, :], BF16)
    return pl.pallas_call(
        _inproj_c_kernel,
        grid=(N_TILES,),
        in_specs=[tok(D_MODEL), _mod_spec(1, 0), _mod_spec(1, 1), _const_spec((D_MODEL, C_NORM_COLS + KVH_C * HD)),
                  _const_spec((1, C_NORM_COLS)), _const_spec((NORM_BLOCK, NORM_BLOCK))] + _rope_specs(),
        out_specs=[tok(w) for w, _ in widths],
        out_shape=[jax.ShapeDtypeStruct((N_TOK, w), dt) for w, dt in widths],
        compiler_params=_params(("arbitrary",)),
        name="inproj_c",
    )(x, mod, mod, w_in, g_full, ones, *tabs)


def _qk(q, k):
    return lax.dot_general(q, k, (((1,), (1,)), ((), ())), preferred_element_type=F32)


def _softmax_pv(scores, values, sink=None):
    m = scores[0].max(axis=-1, keepdims=True)
    for s in scores[1:]:
        m = jnp.maximum(m, s.max(axis=-1, keepdims=True))
    if sink is not None:
        m = jnp.maximum(m, sink)
    den = None
    out = None
    for s, v in zip(scores, values):
        p = jnp.exp(s - m)
        d = p.sum(axis=-1, keepdims=True)
        o = jnp.dot(p.astype(BF16), v, preferred_element_type=F32)
        den = d if den is None else den + d
        out = o if out is None else out + o
    if sink is not None:
        den = den + jnp.exp(sink - m)
    return out / den


def _mla_heads(qan_ref, qar_ref, kv_refs, kpe_list, o_ref):
    scale = (NOPE_A + ROPE_A) ** -0.5
    for h in range(H_A):
        qn = qan_ref[:, h * NOPE_A:(h + 1) * NOPE_A]
        qr = qar_ref[:, h * ROPE_A:(h + 1) * ROPE_A]
        scores, values = [], []
        for kv_ref, kpe in zip(kv_refs, kpe_list):
            base = h * (NOPE_A + V_A)
            s = _qk(qn, kv_ref[:, base:base + NOPE_A]) + _qk(qr, kpe)
            scores.append(s * scale)
            values.append(kv_ref[:, base + NOPE_A:base + NOPE_A + V_A])
        o_ref[:, h * V_A:(h + 1) * V_A] = _softmax_pv(scores, values).astype(o_ref.dtype)


def _attn_ab_prompt_kernel(sink_ref, qan_ref, qar_ref, kv_ref, kpe_ref, qb_ref, kb_ref, vb_ref, o_ref):
    kpe = kpe_ref[:, :ROPE_A].astype(BF16)
    _mla_heads(qan_ref, qar_ref, [kv_ref], [kpe], o_ref)
    scale = HD ** -0.5
    for kh in range(KVH_B):
        k = kb_ref[:, kh * HD:(kh + 1) * HD].astype(BF16)
        v = vb_ref[:, kh * HD:(kh + 1) * HD].astype(BF16)
        for g in range(H_B // KVH_B):
            hq = kh * (H_B // KVH_B) + g
            s = _qk(qb_ref[:, hq * HD:(hq + 1) * HD], k) * scale
            col = H_A * V_A + hq * HD
            o_ref[:, col:col + HD] = _softmax_pv([s], [v], sink_ref[hq]).astype(o_ref.dtype)


def _attn_ab_prompt(sink, kv, qan, qar, qb, kb, vb, kpe):
    def blk(w):
        return pl.BlockSpec((SEQ, w), lambda b: (b, 0))
    return pl.pallas_call(
        _attn_ab_prompt_kernel,
        grid=(BATCH,),
        in_specs=[pl.BlockSpec(memory_space=pltpu.SMEM), blk(512), blk(256), blk(1024), blk(LANES), blk(512),
                  blk(128), blk(128)],
        out_specs=blk(D_MODEL),
        out_shape=jax.ShapeDtypeStruct((N_PROMPT, H_A * V_A + H_B * HD), BF16),
        compiler_params=_params(("arbitrary",)),
        name="attn_ab_prompt",
    )(sink, qan, qar, kv, kpe, qb, kb, vb)


def _attn_ab_sample_kernel(sink_ref, qan_ref, qar_ref, kv_ref, kpe_ref, qb_ref, kb_ref, vb_ref,
                           ckv_ctx_ref, kpe_ctx_ref, kb_ctx_ref, vb_ctx_ref, w_ukv_ref, o_ref, kv_ctx_ref):
    i = pl.program_id(1)

    @pl.when(i == 0)
    def _():
        kv_ctx_ref[...] = jnp.dot(ckv_ctx_ref[...].astype(BF16), w_ukv_ref[...],
                                  preferred_element_type=F32).astype(BF16)

    kpe = kpe_ref[:, :ROPE_A].astype(BF16)
    kpe_ctx = kpe_ctx_ref[...].astype(BF16)
    _mla_heads(qan_ref, qar_ref, [kv_ref, kv_ctx_ref], [kpe, kpe_ctx], o_ref)

    n_win = 3 * BQ_S
    start = pl.multiple_of(jnp.clip((i - 1) * BQ_S, 0, DEC_SEQ - n_win), BQ_S)
    qpos = i * BQ_S + lax.broadcasted_iota(jnp.int32, (BQ_S, n_win), 0)
    kpos = start + lax.broadcasted_iota(jnp.int32, (BQ_S, n_win), 1)
    in_band = jnp.abs(qpos - kpos) <= WINDOW
    scale = HD ** -0.5
    kwin = kb_ref[pl.ds(start, n_win), :].astype(BF16)
    vwin = vb_ref[pl.ds(start, n_win), :].astype(BF16)
    kctx = kb_ctx_ref[...].astype(BF16)
    vctx = vb_ctx_ref[...].astype(BF16)
    for kh in range(KVH_B):
        sl = slice(kh * HD, (kh + 1) * HD)
        for g in range(H_B // KVH_B):
            hq = kh * (H_B // KVH_B) + g
            q = qb_ref[:, hq * HD:(hq + 1) * HD]
            s_loc = jnp.where(in_band, _qk(q, kwin[:, sl]) * scale, NEG_INF)
            s_ctx = _qk(q, kctx[:, sl]) * scale
            col = H_A * V_A + hq * HD
            o_ref[:, col:col + HD] = _softmax_pv([s_loc, s_ctx], [vwin[:, sl], vctx[:, sl]],
                                                 sink_ref[hq]).astype(o_ref.dtype)


def _attn_ab_sample(sink, kv, qan, qar, qb, kb, vb, kpe, ckv_ctx, kpe_ctx, kb_ctx, vb_ctx, w_ukv):
    qoff = N_PROMPT // BQ_S
    nq = DEC_SEQ // BQ_S
    boff = N_PROMPT // DEC_SEQ

    def qblk(w):
        return pl.BlockSpec((BQ_S, w), lambda b, i: (qoff + b * nq + i, 0))

    def bblk(w):
        return pl.BlockSpec((DEC_SEQ, w), lambda b, i: (boff + b, 0))

    def cblk(w):
        return pl.BlockSpec((PAST_LEN, w), lambda b, i: (b, 0))

    return pl.pallas_call(
        _attn_ab_sample_kernel,
        grid=(DEC_BATCH, nq),
        in_specs=[pl.BlockSpec(memory_space=pltpu.SMEM), qblk(512), qblk(256), bblk(1024), bblk(LANES), qblk(512),
                  bblk(128), bblk(128), cblk(KV_LORA), cblk(ROPE_A), cblk(128), cblk(128),
                  _const_spec((KV_LORA, H_A * (NOPE_A + V_A)))],
        out_specs=pl.BlockSpec((BQ_S, D_MODEL), lambda b, i: (b * nq + i, 0)),
        out_shape=jax.ShapeDtypeStruct((N_SAMPLE, H_A * V_A + H_B * HD), BF16),
        scratch_shapes=[pltpu.VMEM((PAST_LEN, H_A * (NOPE_A + V_A)), BF16)],
        compiler_params=_params(("arbitrary", "arbitrary")),
        name="attn_ab_sample",
    )(sink, qan, qar, kv, kpe, qb, kb, vb, ckv_ctx, kpe_ctx, kb_ctx, vb_ctx, w_ukv)


def _gqa_heads(q_ref, k_list, v_list, o_ref):
    scale = HD ** -0.5
    for kh in range(KVH_C):
        sl = slice(kh * HD, (kh + 1) * HD)
        ks = [k[:, sl] for k in k_list]
        vs = [v[:, sl] for v in v_list]
        for g in range(H_C // KVH_C):
            hq = kh * (H_C // KVH_C) + g
            q = q_ref[:, hq * HD:(hq + 1) * HD]
            scores = [_qk(q, k) * scale for k in ks]
            o_ref[:, hq * HD:(hq + 1) * HD] = _softmax_pv(scores, vs).astype(o_ref.dtype)


def _attn_c_prompt_kernel(q_ref, k_ref, v_ref, o_ref):
    _gqa_heads(q_ref, [k_ref[...].astype(BF16)], [v_ref[...].astype(BF16)], o_ref)


def _attn_c_prompt(q, k, v):
    def blk(w):
        return pl.BlockSpec((SEQ, w), lambda b: (b, 0))
    return pl.pallas_call(
        _attn_c_prompt_kernel,
        grid=(BATCH,),
        in_specs=[blk(H_C * HD), blk(KVH_C * HD), blk(KVH_C * HD)],
        out_specs=blk(H_C * HD),
        out_shape=jax.ShapeDtypeStruct((N_PROMPT, H_C * HD), BF16),
        compiler_params=_params(("arbitrary",)),
        name="attn_c_prompt",
    )(q, k, v)


def _attn_c_sample_kernel(q_ref, k_ref, v_ref, kc_ref, vc_ref, o_ref):
    _gqa_heads(q_ref, [k_ref[...].astype(BF16), kc_ref[...].astype(BF16)],
               [v_ref[...].astype(BF16), vc_ref[...].astype(BF16)], o_ref)


def _attn_c_sample(q, k, v, k_ctx, v_ctx):
    qoff = N_PROMPT // BQ_S
    nq = DEC_SEQ // BQ_S
    boff = N_PROMPT // DEC_SEQ
    w = KVH_C * HD
    return pl.pallas_call(
        _attn_c_sample_kernel,
        grid=(DEC_BATCH, nq),
        in_specs=[pl.BlockSpec((BQ_S, H_C * HD), lambda b, i: (qoff + b * nq + i, 0)),
                  pl.BlockSpec((DEC_SEQ, w), lambda b, i: (boff + b, 0)),
                  pl.BlockSpec((DEC_SEQ, w), lambda b, i: (boff + b, 0)),
                  pl.BlockSpec((PAST_LEN, w), lambda b, i: (b, 0)),
                  pl.BlockSpec((PAST_LEN, w), lambda b, i: (b, 0))],
        out_specs=pl.BlockSpec((BQ_S, H_C * HD), lambda b, i: (b * nq + i, 0)),
        out_shape=jax.ShapeDtypeStruct((N_SAMPLE, H_C * HD), BF16),
        compiler_params=_params(("arbitrary", "arbitrary")),
        name="attn_c_sample",
    )(q, k, v, k_ctx, v_ctx)


def _layer_norm(y, g, b):
    mu = jnp.mean(y, axis=-1, keepdims=True)
    yc = y - mu
    var = jnp.mean(yc * yc, axis=-1, keepdims=True)
    return yc * lax.rsqrt(var + EPS) * g + b


def _route(sel, aff):
    def row(a, j):
        return a[j:j + 1, :]

    scores = []
    for g in range(N_GROUPS):
        a0, a1, a2, a3 = (row(sel, 4 * g + j) for j in range(4))
        hi01, lo01 = jnp.maximum(a0, a1), jnp.minimum(a0, a1)
        hi23, lo23 = jnp.maximum(a2, a3), jnp.minimum(a2, a3)
        top1 = jnp.maximum(hi01, hi23)
        top2 = jnp.maximum(jnp.minimum(hi01, hi23), jnp.maximum(lo01, lo23))
        scores.append(top1 + top2)
    best = scores[0]
    gi = jnp.zeros(best.shape, jnp.int32)
    for g in range(1, N_GROUPS):
        better = scores[g] > best
        gi = jnp.where(better, g, gi)
        best = jnp.where(better, scores[g], best)

    def in_group(a, j):
        out = row(a, j)
        for g in range(1, N_GROUPS):
            out = jnp.where(gi == g, row(a, 4 * g + j), out)
        return out

    v = [in_group(sel, j) for j in range(4)]
    a = [in_group(aff, j) for j in range(4)]
    chosen = []
    for j in range(4):
        rank = jnp.zeros(best.shape, jnp.int32)
        for k in range(4):
            if k == j:
                continue
            ahead = (v[k] >= v[j]) if k < j else (v[k] > v[j])
            rank = rank + ahead.astype(jnp.int32)
        chosen.append(rank < 2)
    total = sum(jnp.where(chosen[j], a[j], 0.0) for j in range(4))
    w = [jnp.where(chosen[j], a[j], 0.0) / total for j in range(4)]
    pair = jnp.zeros(best.shape, jnp.int32)
    for p, (ja, jb) in enumerate(zip(SLOT_A_LOCAL, SLOT_B_LOCAL)):
        pair = jnp.where(chosen[ja] & chosen[jb], p, pair)
    gate_a = jnp.where(pair == 0, w[0], jnp.where(pair <= 2, w[2], w[3]))
    gate_b = jnp.where((pair == 0) | (pair == 1) | (pair == 4), w[1], jnp.where(pair == 5, w[2], w[0]))
    return gi * PAIRS_PER_GROUP + pair, gate_a, gate_b


def _post_attn_kernel(split_x, *refs):
    t = pl.program_id(0)
    op_ref, os_ref = refs[:2]
    if split_x:
        x = _pick(t, refs[2], refs[3])
        refs = refs[4:]
    else:
        x = refs[2][...]
        refs = refs[3:]
    (w_out_ref, gate_ref, shift_ref, scale_ref, lng_ref, lnb_ref, rw_ref, rb_ref, tri_ref,
     x1_ref, row_ref, route_ref, counts_ref, carry_ref) = refs
    o = _pick(t, op_ref, os_ref)
    a = jnp.dot(o, w_out_ref[...], preferred_element_type=F32)
    x1 = _layer_norm(ALPHA * x + gate_ref[...] * a, lng_ref[...], lnb_ref[...])
    x1_ref[...] = x1
    h2 = x1 * (1.0 + scale_ref[...]) + shift_ref[...]
    row_ref[:, :D_MODEL] = h2
    logits = lax.dot_general(rw_ref[...], h2, (((1,), (1,)), ((), ())), preferred_element_type=F32,
                             precision=lax.Precision.HIGHEST)
    aff = 1.0 / (1.0 + jnp.exp(-logits))
    bucket, gate_a, gate_b = _route(aff + rb_ref[...], aff)

    lane = lax.broadcasted_iota(jnp.int32, (LANES, TM), 0)
    meta_t = jnp.where(lane == 0, gate_a, jnp.where(lane == 1, gate_b, 0.0))
    row_ref[:, D_MODEL:] = meta_t.T

    @pl.when(t == 0)
    def _():
        carry_ref[...] = jnp.zeros_like(carry_ref)

    onehot = lax.broadcasted_iota(jnp.int32, (BUCKET_ROWS, TM), 0) == bucket
    prefix = jnp.dot(onehot.astype(BF16), tri_ref[...], preferred_element_type=F32)
    carry = carry_ref[...]
    rank = jnp.sum(jnp.where(onehot, prefix + carry[:, 0:1], 0.0), axis=0, keepdims=True)
    carry = carry + jnp.sum(onehot.astype(F32), axis=1, keepdims=True)
    carry_ref[...] = carry
    counts_ref[...] = carry
    route_ref[...] = jnp.zeros((8, TM), F32)
    route_ref[0:1, :] = bucket.astype(F32)
    route_ref[1:2, :] = rank


def _post_attn(layer, o_p, o_s, x_list, mod, w_out, ln_g, ln_b, rw_t, rb):
    split_x = len(x_list) == 2
    tok = pl.BlockSpec((TM, D_MODEL), lambda t: (t, 0))
    o_specs = [
        pl.BlockSpec((TM, D_MODEL), lambda t: (jnp.minimum(t, PROMPT_TILES - 1), 0)),
        pl.BlockSpec((TM, D_MODEL), lambda t: (jnp.maximum(t - PROMPT_TILES, 0), 0)),
    ]
    tri = jnp.asarray(np.arange(TM)[:, None] < np.arange(TM)[None, :], BF16)
    return pl.pallas_call(
        functools.partial(_post_attn_kernel, split_x),
        grid=(N_TILES,),
        in_specs=o_specs + (_x_specs() if split_x else [tok]) + [
            _const_spec((D_MODEL, D_MODEL)), _mod_spec(layer, 2), _mod_spec(layer, 3), _mod_spec(layer, 4),
            _const_spec((1, D_MODEL)), _const_spec((1, D_MODEL)), _const_spec((N_EXPERTS, D_MODEL)),
            _const_spec((N_EXPERTS, 1)), _const_spec((TM, TM))],
        out_specs=[tok, pl.BlockSpec((TM, ROW_W), lambda t: (t, 0)), pl.BlockSpec((8, TM), lambda t: (0, t)),
                   _const_spec((BUCKET_ROWS, LANES))],
        out_shape=[jax.ShapeDtypeStruct((N_TOK, D_MODEL), F32), jax.ShapeDtypeStruct((N_TOK, ROW_W), F32),
                   jax.ShapeDtypeStruct((8, N_TOK), F32), jax.ShapeDtypeStruct((BUCKET_ROWS, LANES), F32)],
        scratch_shapes=[pltpu.VMEM((BUCKET_ROWS, LANES), F32)],
        compiler_params=_params(("arbitrary",)),
        name="post_attn",
    )(o_p, o_s, *x_list, w_out, mod, mod, mod, ln_g, ln_b, rw_t, rb, tri)


def _moe_kernel(ta_ref, tb_ref, nt_ref, x_ref, wga_ref, wua_ref, wda_ref, wgb_ref, wub_ref, wdb_ref, o_ref,
                wga_s, wua_s, wda_s, wgb_s, wub_s, wdb_s):
    i = pl.program_id(0)
    prev = jnp.maximum(i - 1, 0)
    slots = ((ta_ref, (wga_ref, wua_ref, wda_ref), (wga_s, wua_s, wda_s)),
             (tb_ref, (wgb_ref, wub_ref, wdb_ref), (wgb_s, wub_s, wdb_s)))

    for t_ref, w_refs, w_scr in slots:
        @pl.when((i == 0) | (t_ref[i] != t_ref[prev]))
        def _(w_refs=w_refs, w_scr=w_scr):
            for w_ref, s_ref in zip(w_refs, w_scr):
                s_ref[...] = w_ref[...].astype(BF16)

    @pl.when(i < nt_ref[0])
    def _():
        x = x_ref[:, :D_MODEL].astype(BF16)
        acc = None
        for slot, (_, _, (wg_s, wu_s, wd_s)) in enumerate(slots):
            gt = jnp.dot(x, wg_s[...], preferred_element_type=F32)
            up = jnp.dot(x, wu_s[...], preferred_element_type=F32)
            gate = x_ref[:, D_MODEL + slot:D_MODEL + slot + 1]
            hid = (gt / (1.0 + jnp.exp(-gt))) * up * gate
            y = jnp.dot(hid.astype(BF16), wd_s[...], preferred_element_type=F32)
            acc = y if acc is None else acc + y
        o_ref[...] = acc

    @pl.when(i >= nt_ref[0])
    def _():
        o_ref[...] = jnp.zeros_like(o_ref)


def _moe(tile_a, tile_b, n_tiles, rows, w_gate, w_up, w_down):
    def row_idx(i, ta, tb, nt):
        return (jnp.minimum(i, nt[0] - 1), 0)

    def wspec(shape, which):
        if which == 0:
            return pl.BlockSpec((None,) + shape, lambda i, ta, tb, nt: (ta[i], 0, 0))
        return pl.BlockSpec((None,) + shape, lambda i, ta, tb, nt: (tb[i], 0, 0))

    up_shape, down_shape = (D_MODEL, D_EXPERT), (D_EXPERT, D_MODEL)
    grid_spec = pltpu.PrefetchScalarGridSpec(
        num_scalar_prefetch=3,
        grid=(MOE_TILES,),
        in_specs=[pl.BlockSpec((TME, ROW_W), row_idx),
                  wspec(up_shape, 0), wspec(up_shape, 0), wspec(down_shape, 0),
                  wspec(up_shape, 1), wspec(up_shape, 1), wspec(down_shape, 1)],
        out_specs=pl.BlockSpec((TME, D_MODEL), lambda i, ta, tb, nt: (i, 0)),
        scratch_shapes=[pltpu.VMEM(s, BF16) for s in (up_shape, up_shape, down_shape) * 2],
    )
    return pl.pallas_call(
        _moe_kernel,
        grid_spec=grid_spec,
        out_shape=jax.ShapeDtypeStruct((MOE_ROWS, D_MODEL), F32),
        compiler_params=_params(("arbitrary",)),
        name="moe_experts",
    )(tile_a, tile_b, n_tiles, rows, w_gate, w_up, w_down, w_gate, w_up, w_down)


def _row_copy(src_ref, src_row, dst_ref, dst_row, sem):
    return pltpu.make_async_copy(src_ref.at[pl.ds(src_row, 1), :], dst_ref.at[pl.ds(dst_row, 1), :], sem)


def _dispatch_kernel(pos_ref, pend_ref, cnt_ref, nt_ref, src_ref, out_ref, zero_ref, sem, zsem):
    t = pl.program_id(0)

    @pl.when(t == 0)
    def _():
        zero_ref[...] = jnp.zeros_like(zero_ref)

        def zero_tile(row0):
            return pltpu.make_async_copy(zero_ref, out_ref.at[pl.ds(pl.multiple_of(row0, TME), TME), :], zsem)

        for b in range(N_BUCKETS):
            @pl.when(cnt_ref[b] > 0)
            def _(b=b):
                zero_tile(pend_ref[b] - TME).start()

        def start_unused(i, carry):
            zero_tile(i * TME).start()
            return carry
        lax.fori_loop(nt_ref[0], MOE_TILES, start_unused, 0)

        for b in range(N_BUCKETS):
            @pl.when(cnt_ref[b] > 0)
            def _(b=b):
                zero_tile(pend_ref[b] - TME).wait()

        def wait_unused(i, carry):
            zero_tile(i * TME).wait()
            return carry
        lax.fori_loop(nt_ref[0], MOE_TILES, wait_unused, 0)

    def issue(r, carry):
        _row_copy(src_ref, r, out_ref, pos_ref[t * TM + r], sem).start()
        return carry
    lax.fori_loop(0, TM, issue, 0, unroll=8)
    pltpu.make_async_copy(src_ref, out_ref.at[pl.ds(0, TM), :], sem).wait()


def _dispatch(pos, pend, counts, n_tiles, rows):
    return pl.pallas_call(
        _dispatch_kernel,
        grid_spec=pltpu.PrefetchScalarGridSpec(
            num_scalar_prefetch=4, grid=(N_TILES,),
            in_specs=[pl.BlockSpec((TM, ROW_W), lambda t, *_: (t, 0))],
            out_specs=pl.BlockSpec(memory_space=pl.ANY),
            scratch_shapes=[pltpu.VMEM((TME, ROW_W), F32), pltpu.SemaphoreType.DMA, pltpu.SemaphoreType.DMA]),
        out_shape=jax.ShapeDtypeStruct((MOE_ROWS, ROW_W), F32),
        compiler_params=_params(("arbitrary",)),
        name="dispatch_rows",
    )(pos, pend, counts, n_tiles, rows)


def _plan(route, counts):
    bucket = route[0].astype(jnp.int32)
    rank = route[1].astype(jnp.int32)
    counts = counts[:N_BUCKETS, 0].astype(jnp.int32)
    padded = ((counts + TME - 1) // TME) * TME
    pend = jnp.cumsum(padded)
    pstart = pend - padded
    ids = jnp.arange(N_BUCKETS, dtype=jnp.int32)
    pos = rank + jnp.sum(jnp.where(bucket[None, :] == ids[:, None], pstart[:, None], 0), axis=0)
    n_tiles = pend[-1] // TME
    tile_start = jnp.minimum(jnp.arange(MOE_TILES, dtype=jnp.int32), n_tiles - 1) * TME
    tile_bucket = jnp.minimum(jnp.sum(tile_start[:, None] >= pend[None, :], axis=1), N_BUCKETS - 1)
    group, pair = tile_bucket // PAIRS_PER_GROUP, tile_bucket % PAIRS_PER_GROUP
    slot_a = jnp.asarray(SLOT_A_LOCAL, jnp.int32)
    slot_b = jnp.asarray(SLOT_B_LOCAL, jnp.int32)
    pair_hot = pair[:, None] == jnp.arange(PAIRS_PER_GROUP, dtype=jnp.int32)[None, :]
    tile_a = group * EXPERTS_PER_GROUP + jnp.sum(jnp.where(pair_hot, slot_a[None, :], 0), axis=1)
    tile_b = group * EXPERTS_PER_GROUP + jnp.sum(jnp.where(pair_hot, slot_b[None, :], 0), axis=1)
    return pos, pend, counts, tile_a.astype(jnp.int32), tile_b.astype(jnp.int32), n_tiles.reshape(1)


def _post_moe_kernel(split_out, pos_ref, f_ref, x1_ref, gate_ref, lng_ref, lnb_ref, *refs):
    out_refs, (fbuf, sem) = refs[:-2], refs[-2:]
    t = pl.program_id(0)

    def gather(tile, slot):
        def issue(r, carry):
            _row_copy(f_ref, pos_ref[tile * TM + r], fbuf.at[slot], r, sem.at[slot]).start()
            return carry
        lax.fori_loop(0, TM, issue, 0, unroll=8)

    @pl.when(t == 0)
    def _():
        gather(0, 0)

    @pl.when(t + 1 < N_TILES)
    def _():
        gather(t + 1, (t + 1) % 2)

    slot = t % 2
    pltpu.make_async_copy(f_ref.at[pl.ds(0, TM), :], fbuf.at[slot], sem.at[slot]).wait()
    y = _layer_norm(ALPHA * x1_ref[...] + gate_ref[...] * fbuf[slot], lng_ref[...], lnb_ref[...])
    if split_out:
        @pl.when(t < PROMPT_TILES)
        def _():
            out_refs[0][...] = y

        @pl.when(t >= PROMPT_TILES)
        def _():
            out_refs[1][...] = y
    else:
        out_refs[0][...] = y


def _post_moe(layer, pos, f_sorted, x1, mod, ln_g, ln_b, split_out):
    tok = pl.BlockSpec((TM, D_MODEL), lambda t, p: (t, 0))
    if split_out:
        out_specs = [pl.BlockSpec((TM, D_MODEL), lambda t, p: (jnp.minimum(t, PROMPT_TILES - 1), 0)),
                     pl.BlockSpec((TM, D_MODEL), lambda t, p: (jnp.maximum(t - PROMPT_TILES, 0), 0))]
        out_shape = [jax.ShapeDtypeStruct((N_PROMPT, D_MODEL), F32), jax.ShapeDtypeStruct((N_SAMPLE, D_MODEL), F32)]
    else:
        out_specs = [tok]
        out_shape = [jax.ShapeDtypeStruct((N_TOK, D_MODEL), F32)]
    mod_spec = pl.BlockSpec((None, None, None, 1, D_MODEL), lambda t, p: (layer, _mod_row(t), 5, 0, 0))
    return pl.pallas_call(
        functools.partial(_post_moe_kernel, split_out),
        grid_spec=pltpu.PrefetchScalarGridSpec(
            num_scalar_prefetch=1, grid=(N_TILES,),
            in_specs=[pl.BlockSpec(memory_space=pl.ANY), tok, mod_spec,
                      pl.BlockSpec((1, D_MODEL), lambda t, p: (0, 0)), pl.BlockSpec((1, D_MODEL), lambda t, p: (0, 0))],
            out_specs=out_specs,
            scratch_shapes=[pltpu.VMEM((2, TM, D_MODEL), F32), pltpu.SemaphoreType.DMA((2,))]),
        out_shape=out_shape,
        compiler_params=_params(("arbitrary",)),
        name="post_moe",
    )(pos, f_sorted, x1, mod, ln_g, ln_b)


def _ffn(layer, o_p, o_s, x_list, mod, w_out, ln_g, ln_b, rw_t, rb, w_gate, w_up, w_down, split_out):
    x1, rows, route, counts = _post_attn(layer, o_p, o_s, x_list, mod, w_out, ln_g[layer, 0][None],
                                         ln_b[layer, 0][None], rw_t, rb)
    pos, pend, counts, tile_a, tile_b, n_tiles = _plan(route, counts)
    rows_sorted = _dispatch(pos, pend, counts, n_tiles, rows)
    f_sorted = _moe(tile_a, tile_b, n_tiles, rows_sorted, w_gate, w_up, w_down)
    return _post_moe(layer, pos, f_sorted, x1, mod, ln_g[layer, 1][None], ln_b[layer, 1][None], split_out)


def kernel(x_prompt, x_sample, c, cache_mla_ckv, cache_mla_kpe, cache_swa_k, cache_swa_v, cache_gqa_k, cache_gqa_v, c_ctx, w_mod, b_mod, ln_g, ln_b, w_in_ab, mla_q_norm, mla_w_uq, mla_kv_norm, mla_w_ukv, swa_sink, w_out_ab, w_in_c, gqa_q_norm, gqa_k_norm, w_out_c, router_w, router_bias, exp_w_gate, exp_w_up, exp_w_down):
    xp = x_prompt.reshape(N_PROMPT, D_MODEL)
    xs = x_sample.reshape(N_SAMPLE, D_MODEL)
    cond = jnp.concatenate([c_ctx[None], c, jnp.zeros((MOD_ROWS - 1 - DEC_BATCH, D_MODEL), F32)], axis=0)
    mod = _modulation(cond, w_mod, b_mod).reshape(DEPTH, MOD_ROWS, 6, 1, D_MODEL)
    tabs = _rope_cs(HD // 2) + _rope_cs(ROPE_A // 2)
    rw_t = router_w.T
    rb = router_bias.reshape(N_EXPERTS, 1)

    w = w_in_ab[0]
    pad = jnp.zeros((D_MODEL, AB_COLS - w.shape[1]), F32)
    w_in = jnp.concatenate([w[:, :384], w[:, 416:], w[:, 384:416], pad], axis=1).astype(BF16)
    w_uq = mla_w_uq[0].reshape(Q_LORA, H_A, NOPE_A + ROPE_A)
    w_uq = jnp.concatenate([w_uq[:, :, :NOPE_A].reshape(Q_LORA, -1), w_uq[:, :, NOPE_A:].reshape(Q_LORA, -1)],
                           axis=1).astype(BF16)
    w_ukv = mla_w_ukv[0].astype(BF16)
    kv, qan, qar, qb, ckv, kb, vb, kpe = _inproj_ab(xp, xs, mod, w_in, mla_q_norm[0][None], mla_kv_norm[0][None],
                                                    w_uq, w_ukv, tabs)
    sink = swa_sink[0]
    o_p = _attn_ab_prompt(sink, kv, qan, qar, qb, kb, vb, kpe)
    o_s = _attn_ab_sample(sink, kv, qan, qar, qb, kb, vb, kpe,
                          cache_mla_ckv[:, 0].reshape(-1, KV_LORA), cache_mla_kpe[:, 0].reshape(-1, ROPE_A),
                          cache_swa_k[:, 0].reshape(-1, KVH_B * HD), cache_swa_v[:, 0].reshape(-1, KVH_B * HD), w_ukv)
    (x2,) = _ffn(0, o_p, o_s, [xp, xs], mod, w_out_ab[0].astype(BF16), ln_g, ln_b, rw_t, rb,
                 exp_w_gate[0], exp_w_up[0], exp_w_down[0], split_out=False)

    g_full = jnp.concatenate([jnp.tile(gqa_q_norm[0], H_C), jnp.tile(gqa_k_norm[0], KVH_C)])[None]
    qc, kc, vc = _inproj_c(x2, mod, w_in_c[0].astype(BF16), g_full, tabs)
    oc_p = _attn_c_prompt(qc, kc, vc)
    oc_s = _attn_c_sample(qc, kc, vc, cache_gqa_k[:, 0].reshape(-1, KVH_C * HD),
                          cache_gqa_v[:, 0].reshape(-1, KVH_C * HD))
    y_p, y_s = _ffn(1, oc_p, oc_s, [x2], mod, w_out_c[0].astype(BF16), ln_g, ln_b, rw_t, rb,
                    exp_w_gate[1], exp_w_up[1], exp_w_down[1], split_out=True)

    y_prompt = y_p.reshape(BATCH, SEQ, D_MODEL)
    y_sample = y_s.reshape(DEC_BATCH, DEC_SEQ, D_MODEL)
    new_ckv = ckv[:N_PROMPT].reshape(BATCH, 1, SEQ, KV_LORA)
    new_kpe = kpe[:N_PROMPT, :ROPE_A].reshape(BATCH, 1, SEQ, ROPE_A)
    new_swk = kb[:N_PROMPT].reshape(BATCH, 1, SEQ, KVH_B, HD)
    new_swv = vb[:N_PROMPT].reshape(BATCH, 1, SEQ, KVH_B, HD)
    new_gk = kc[:N_PROMPT].reshape(BATCH, 1, SEQ, KVH_C, HD)
    new_gv = vc[:N_PROMPT].reshape(BATCH, 1, SEQ, KVH_C, HD)
    return y_prompt, y_sample, new_ckv, new_kpe, new_swk, new_swv, new_gk, new_gv
```

```python
import functools

import numpy as np
import jax
import jax.numpy as jnp
from jax import lax
from jax.experimental import pallas as pl
from jax.experimental.pallas import tpu as pltpu

D_MODEL = 1024
BATCH = 32
SEQ = 256
DEPTH = 2
DEC_BATCH = 4
DEC_SEQ = 1024
PAST_LEN = 256
GRID_W = 64
ROPE_THETA = 10000.0
WINDOW = 128
HD = 64
H_A = 8
NOPE_A = 64
ROPE_A = 32
V_A = 64
Q_LORA = 256
KV_LORA = 128
H_B = 8
KVH_B = 2
H_C = 16
KVH_C = 4
N_EXPERTS = 16
N_GROUPS = 4
EXPERTS_PER_GROUP = 4
D_EXPERT = 512
ALPHA = (2 * DEPTH) ** 0.25
NEG_INF = -1e30
EPS = 1e-6

LANES = 128
N_PROMPT = BATCH * SEQ
N_SAMPLE = DEC_BATCH * DEC_SEQ
N_TOK = N_PROMPT + N_SAMPLE
TM = 256
N_TILES = N_TOK // TM
PROMPT_TILES = N_PROMPT // TM
SAMPLE_TILES_PER_BATCH = DEC_SEQ // TM
BQ_S = 256
LOG2E = 1.4426950408889634
MOD_ROWS = 8

PAIRS_PER_GROUP = 6
N_BUCKETS = N_GROUPS * PAIRS_PER_GROUP
SLOT_A_LOCAL = (0, 2, 2, 3, 3, 3)
SLOT_B_LOCAL = (1, 1, 0, 0, 1, 2)
TME = 256
MOE_TILES = -(-(N_TOK + N_BUCKETS * (TME - 1)) // TME)
MOE_ROWS = MOE_TILES * TME

BUCKET_ROWS = 32
ROW_W = D_MODEL + LANES

AB_COLS = 1280
C_NORM_COLS = H_C * HD + KVH_C * HD
NORM_BLOCK = 256

BF16 = jnp.bfloat16
F32 = jnp.float32
VMEM_LIMIT = 52 * 1024 * 1024


def _mod_row(t):
    return jnp.where(t < PROMPT_TILES, 0, 1 + (t - PROMPT_TILES) // SAMPLE_TILES_PER_BATCH)


def _mod_spec(layer, chunk):
    return pl.BlockSpec((None, None, None, 1, D_MODEL), lambda t: (layer, _mod_row(t), chunk, 0, 0))


def _const_spec(shape):
    nd = len(shape)
    return pl.BlockSpec(shape, lambda *_: (0,) * nd)


def _x_specs():
    return [
        pl.BlockSpec((TM, D_MODEL), lambda t: (jnp.minimum(t, PROMPT_TILES - 1), 0)),
        pl.BlockSpec((TM, D_MODEL), lambda t: (jnp.maximum(t - PROMPT_TILES, 0), 0)),
    ]


def _pick(t, a_ref, b_ref):
    return jnp.where(t < PROMPT_TILES, a_ref[...], b_ref[...])


def _params(sem):
    return pltpu.CompilerParams(dimension_semantics=sem, vmem_limit_bytes=VMEM_LIMIT)


def _mod_kernel(cond_ref, w_ref, b_ref, o_ref):
    c = cond_ref[...]
    s = (c / (1.0 + jnp.exp(-c))).astype(BF16)
    o_ref[...] = jnp.dot(s, w_ref[...].astype(BF16), preferred_element_type=F32) + b_ref[...]


def _modulation(cond, w_mod, b_mod):
    tn = 1536
    return pl.pallas_call(
        _mod_kernel,
        grid=(DEPTH, 6 * D_MODEL // tn),
        in_specs=[
            pl.BlockSpec((MOD_ROWS, D_MODEL), lambda l, j: (0, 0)),
            pl.BlockSpec((None, D_MODEL, tn), lambda l, j: (l, 0, j)),
            pl.BlockSpec((None, 1, tn), lambda l, j: (l, 0, j)),
        ],
        out_specs=pl.BlockSpec((None, MOD_ROWS, tn), lambda l, j: (l, 0, j)),
        out_shape=jax.ShapeDtypeStruct((DEPTH, MOD_ROWS, 6 * D_MODEL), F32),
        compiler_params=_params(("arbitrary", "arbitrary")),
        name="modulation",
    )(cond, w_mod, b_mod.reshape(DEPTH, 1, 6 * D_MODEL))


def _grid_cos_sin(half):
    n_freq = half // 2
    rows = DEC_SEQ // GRID_W
    row = jnp.repeat(jnp.arange(rows, dtype=F32), GRID_W)
    col = jnp.tile(jnp.arange(GRID_W, dtype=F32), rows)
    inv = 1.0 / (ROPE_THETA ** (jnp.arange(n_freq, dtype=F32) / n_freq))
    ang = jnp.concatenate([row[:, None] * inv, col[:, None] * inv], axis=-1)
    return jnp.cos(ang), jnp.sin(ang)


def _rope_tables():
    tabs = []
    for half in (HD // 2, ROPE_A // 2):
        cos, sin = _grid_cos_sin(half)
        reps = LANES // (2 * half)
        tabs.append(jnp.tile(jnp.concatenate([cos, cos], axis=-1), (1, reps)))
        tabs.append(jnp.tile(jnp.concatenate([-sin, sin], axis=-1), (1, reps)))
    half = ROPE_A // 2
    one = jnp.ones((DEC_SEQ, NOPE_A), F32)
    zero = jnp.zeros((DEC_SEQ, NOPE_A), F32)
    pad1 = jnp.ones((DEC_SEQ, LANES - NOPE_A - ROPE_A), F32)
    pad0 = jnp.zeros((DEC_SEQ, LANES - NOPE_A - ROPE_A), F32)
    zh = jnp.zeros((DEC_SEQ, half), F32)
    tabs.append(jnp.concatenate([one, cos, cos, pad1], axis=-1))
    tabs.append(jnp.concatenate([zero, -sin, zh, pad0], axis=-1))
    tabs.append(jnp.concatenate([zero, zh, sin, pad0], axis=-1))
    return tuple(tabs)


def _rope(x, c_ref, s_ref, half):
    w = x.shape[1]
    reps = w // LANES
    c = c_ref[...]
    s = s_ref[...]
    if reps > 1:
        c = jnp.concatenate([c] * reps, axis=1)
        s = jnp.concatenate([s] * reps, axis=1)
    ahead = pltpu.roll(x, w - half, 1)
    behind = pltpu.roll(x, half, 1)
    lane = lax.broadcasted_iota(jnp.int32, x.shape, 1)
    swapped = jnp.where((lane & (2 * half - 1)) < half, ahead, behind)
    return x * c + swapped * s


def _rope_specs(n):
    def idx(t):
        return (jnp.maximum(t - PROMPT_TILES, 0) % SAMPLE_TILES_PER_BATCH, 0)
    return [pl.BlockSpec((TM, LANES), idx)] * n


def _inproj_ab_kernel(xp_ref, xs_ref, shift_ref, scale_ref, w_in_ref, qn_ref, kvn_ref, w_uq_ref, w_kn_ref, w_v_ref,
                      place_ref, c64_ref, s64_ref, c32_ref, s32_ref, ca_ref, sa_ref, sb_ref,
                      qa_ref, ka_ref, va_ref, qb_ref, ckv_ref, kb_ref, vb_ref, kpe_ref):
    t = pl.program_id(0)
    x = _pick(t, xp_ref, xs_ref)
    h = (x * (1.0 + scale_ref[...]) + shift_ref[...]).astype(BF16)
    z = jnp.dot(h, w_in_ref[...], preferred_element_type=F32)

    ql = z[:, :Q_LORA]
    ql = ql * lax.rsqrt(jnp.mean(ql * ql, axis=-1, keepdims=True) + EPS) * qn_ref[...]
    qa = jnp.dot(ql.astype(BF16), w_uq_ref[...], preferred_element_type=F32)
    c = z[:, Q_LORA:Q_LORA + KV_LORA]
    ckv = c * lax.rsqrt(jnp.mean(c * c, axis=-1, keepdims=True) + EPS) * kvn_ref[...]
    ckv_ref[...] = ckv
    ckv16 = ckv.astype(BF16)
    k_nope = jnp.dot(ckv16, w_kn_ref[...], preferred_element_type=F32)
    va_ref[...] = jnp.dot(ckv16, w_v_ref[...], preferred_element_type=F32).astype(BF16)
    vb_ref[...] = z[:, 1024:1152]
    qb = z[:, 384:896]
    kb = z[:, 896:1024]
    kpe = z[:, 1152:1280]

    def finish(qa_out, qb_out, kb_out, kpe_out):
        qa_ref[...] = qa_out.astype(BF16)
        qb_ref[...] = qb_out.astype(BF16)
        kb_ref[...] = kb_out
        kpe_ref[...] = kpe_out
        ka_ref[...] = (k_nope + jnp.dot(kpe_out.astype(BF16), place_ref[...],
                                        preferred_element_type=F32)).astype(BF16)

    @pl.when(t < PROMPT_TILES)
    def _():
        finish(qa, qb, kb, kpe)

    @pl.when(t >= PROMPT_TILES)
    def _():
        w = qa.shape[1]
        reps = w // LANES
        ca = jnp.concatenate([ca_ref[...]] * reps, axis=1)
        sa = jnp.concatenate([sa_ref[...]] * reps, axis=1)
        sb = jnp.concatenate([sb_ref[...]] * reps, axis=1)
        half = ROPE_A // 2
        qa_rot = qa * ca + pltpu.roll(qa, w - half, 1) * sa + pltpu.roll(qa, half, 1) * sb
        finish(qa_rot, _rope(qb, c64_ref, s64_ref, HD // 2), _rope(kb, c64_ref, s64_ref, HD // 2),
               _rope(kpe, c32_ref, s32_ref, ROPE_A // 2))


def _mla_weights(w_uq, w_ukv):
    pad = LANES - NOPE_A - ROPE_A
    uq = w_uq.reshape(Q_LORA, H_A, NOPE_A + ROPE_A)
    uq = jnp.pad(uq, ((0, 0), (0, 0), (0, pad))).reshape(Q_LORA, H_A * LANES)
    ukv = w_ukv.reshape(KV_LORA, H_A, NOPE_A + V_A)
    kn = jnp.pad(ukv[:, :, :NOPE_A], ((0, 0), (0, 0), (0, LANES - NOPE_A))).reshape(KV_LORA, H_A * LANES)
    wv = ukv[:, :, NOPE_A:].reshape(KV_LORA, H_A * V_A)
    src = np.arange(LANES)[:, None]
    dst = np.arange(H_A * LANES)[None, :] % LANES
    place = (src < ROPE_A) & (dst == src + NOPE_A)
    return uq.astype(BF16), kn.astype(BF16), wv.astype(BF16), jnp.asarray(place, BF16)


def _inproj_ab(xp, xs, mod, w_in, q_norm, kv_norm, w_uq, w_kn, w_v, place, tabs):
    def tok(w):
        return pl.BlockSpec((TM, w), lambda t: (t, 0))
    widths = [(H_A * LANES, BF16), (H_A * LANES, BF16), (H_A * V_A, BF16), (H_B * HD, BF16),
              (KV_LORA, F32), (KVH_B * HD, F32), (KVH_B * HD, F32), (LANES, F32)]
    return pl.pallas_call(
        _inproj_ab_kernel,
        grid=(N_TILES,),
        in_specs=_x_specs() + [_mod_spec(0, 0), _mod_spec(0, 1), _const_spec((D_MODEL, AB_COLS)),
                               _const_spec((1, Q_LORA)), _const_spec((1, KV_LORA)),
                               _const_spec((Q_LORA, H_A * LANES)), _const_spec((KV_LORA, H_A * LANES)),
                               _const_spec((KV_LORA, H_A * V_A)), _const_spec((LANES, H_A * LANES))]
        + _rope_specs(7),
        out_specs=[tok(w) for w, _ in widths],
        out_shape=[jax.ShapeDtypeStruct((N_TOK, w), dt) for w, dt in widths],
        compiler_params=_params(("arbitrary",)),
        name="inproj_ab",
    )(xp, xs, mod, mod, w_in, q_norm, kv_norm, w_uq, w_kn, w_v, place, *tabs)


def _inproj_c_kernel(x_ref, shift_ref, scale_ref, w_in_ref, g_ref, ones_ref, c64_ref, s64_ref, q_ref, k_ref, v_ref):
    t = pl.program_id(0)
    h = (x_ref[...] * (1.0 + scale_ref[...]) + shift_ref[...]).astype(BF16)
    z = jnp.dot(h, w_in_ref[...], preferred_element_type=F32)
    v_ref[...] = z[:, C_NORM_COLS:]
    qk = z[:, :C_NORM_COLS]
    sq = qk * qk
    sq_hi = sq.astype(BF16)
    sq_lo = (sq - sq_hi.astype(F32)).astype(BF16)
    blocks = []
    for j in range(C_NORM_COLS // NORM_BLOCK):
        sl = slice(j * NORM_BLOCK, (j + 1) * NORM_BLOCK)
        blocks.append(jnp.dot(sq_hi[:, sl], ones_ref[...], preferred_element_type=F32)
                      + jnp.dot(sq_lo[:, sl], ones_ref[...], preferred_element_type=F32))
    ss = jnp.concatenate(blocks, axis=1)
    qk = qk * lax.rsqrt(ss * (1.0 / HD) + EPS) * g_ref[...]

    @pl.when(t < PROMPT_TILES)
    def _():
        q_ref[...] = qk[:, :H_C * HD].astype(BF16)
        k_ref[...] = qk[:, H_C * HD:]

    @pl.when(t >= PROMPT_TILES)
    def _():
        q_ref[...] = _rope(qk[:, :H_C * HD], c64_ref, s64_ref, HD // 2).astype(BF16)
        k_ref[...] = _rope(qk[:, H_C * HD:], c64_ref, s64_ref, HD // 2)


def _inproj_c(x, mod, w_in, g_full, tabs):
    def tok(w):
        return pl.BlockSpec((TM, w), lambda t: (t, 0))
    widths = [(H_C * HD, BF16), (KVH_C * HD, F32), (KVH_C * HD, F32)]
    head = np.arange(NORM_BLOCK) // HD
    ones = jnp.asarray(head[:, None] == head[None, :], BF16)
    return pl.pallas_call(
        _inproj_c_kernel,
        grid=(N_TILES,),
        in_specs=[tok(D_MODEL), _mod_spec(1, 0), _mod_spec(1, 1), _const_spec((D_MODEL, C_NORM_COLS + KVH_C * HD)),
                  _const_spec((1, C_NORM_COLS)), _const_spec((NORM_BLOCK, NORM_BLOCK))] + _rope_specs(2),
        out_specs=[tok(w) for w, _ in widths],
        out_shape=[jax.ShapeDtypeStruct((N_TOK, w), dt) for w, dt in widths],
        compiler_params=_params(("arbitrary",)),
        name="inproj_c",
    )(x, mod, mod, w_in, g_full, ones, *tabs[:2])


def _qk(q, k):
    return lax.dot_general(q, k, (((1,), (1,)), ((), ())), preferred_element_type=F32)


def _softmax_pv(scores, values, scale, sink=None):
    c = scale * LOG2E
    m = scores[0].max(axis=-1, keepdims=True)
    for s in scores[1:]:
        m = jnp.maximum(m, s.max(axis=-1, keepdims=True))
    m = m * c
    if sink is not None:
        sink = sink * LOG2E
        m = jnp.maximum(m, sink)
    den = None
    out = None
    for s, v in zip(scores, values):
        p = jnp.exp2(s * c - m)
        d = p.sum(axis=-1, keepdims=True)
        o = jnp.dot(p.astype(BF16), v, preferred_element_type=F32)
        den = d if den is None else den + d
        out = o if out is None else out + o
    if sink is not None:
        den = den + jnp.exp2(sink - m)
    return out * (1.0 / den)


def _mla_heads(qa_ref, ka_refs, va_refs, o_ref):
    scale = (NOPE_A + ROPE_A) ** -0.5
    for h0 in range(0, H_A, 2):
        pair = []
        for h in (h0, h0 + 1):
            q = qa_ref[:, h * LANES:(h + 1) * LANES]
            scores = [_qk(q, ka_ref[:, h * LANES:(h + 1) * LANES]) for ka_ref in ka_refs]
            values = [va_ref[:, h * V_A:(h + 1) * V_A] for va_ref in va_refs]
            pair.append(_softmax_pv(scores, values, scale))
        o_ref[:, h0 * V_A:(h0 + 2) * V_A] = jnp.concatenate(pair, axis=1).astype(o_ref.dtype)


def _gqa_group(q_ref, col0, n_q, ks, vs, mask=None, sink=None):
    rows = q_ref.shape[0]
    q = jnp.concatenate([q_ref[:, col0 + g * HD:col0 + (g + 1) * HD] for g in range(n_q)], axis=0)
    scores = [_qk(q, k) for k in ks]
    if mask is not None:
        scores[0] = jnp.where(mask, scores[0], NEG_INF)
    o = _softmax_pv(scores, vs, HD ** -0.5, sink)
    return jnp.concatenate([o[g * rows:(g + 1) * rows] for g in range(n_q)], axis=1)


def _stacked_sink(sink_ref, h0, n_q, rows):
    head = lax.broadcasted_iota(jnp.int32, (n_q * rows, 1), 0) // rows
    col = jnp.full((n_q * rows, 1), sink_ref[h0], F32)
    for g in range(1, n_q):
        col = jnp.where(head == g, sink_ref[h0 + g], col)
    return col


def _attn_ab_prompt_kernel(sink_ref, qa_ref, ka_ref, va_ref, qb_ref, kb_ref, vb_ref, o_ref):
    _mla_heads(qa_ref, [ka_ref], [va_ref], o_ref)
    kb = kb_ref[...].astype(BF16)
    vb = vb_ref[...].astype(BF16)
    gq = H_B // KVH_B
    for kh in range(KVH_B):
        sl = slice(kh * HD, (kh + 1) * HD)
        o = _gqa_group(qb_ref, kh * gq * HD, gq, [kb[:, sl]], [vb[:, sl]],
                       sink=_stacked_sink(sink_ref, kh * gq, gq, SEQ))
        col = H_A * V_A + kh * gq * HD
        o_ref[:, col:col + gq * HD] = o.astype(o_ref.dtype)


def _attn_ab_prompt(sink, qa, ka, va, qb, kb, vb):
    def blk(w):
        return pl.BlockSpec((SEQ, w), lambda b: (b, 0))
    return pl.pallas_call(
        _attn_ab_prompt_kernel,
        grid=(BATCH,),
        in_specs=[pl.BlockSpec(memory_space=pltpu.SMEM), blk(H_A * LANES), blk(H_A * LANES), blk(H_A * V_A),
                  blk(H_B * HD), blk(KVH_B * HD), blk(KVH_B * HD)],
        out_specs=blk(D_MODEL),
        out_shape=jax.ShapeDtypeStruct((N_PROMPT, H_A * V_A + H_B * HD), BF16),
        compiler_params=_params(("arbitrary",)),
        name="attn_ab_prompt",
    )(sink, qa, ka, va, qb, kb, vb)


def _attn_ab_sample_kernel(sink_ref, qa_ref, ka_ref, va_ref, qb_ref, kb_ref, vb_ref,
                           ckv_ctx_ref, kpe_ctx_ref, kb_ctx_ref, vb_ctx_ref, w_kn_ref, w_v_ref, place_ref,
                           o_ref, ka_ctx_ref, va_ctx_ref):
    i = pl.program_id(1)

    @pl.when(i == 0)
    def _():
        ckv16 = ckv_ctx_ref[...].astype(BF16)
        k_nope = jnp.dot(ckv16, w_kn_ref[...], preferred_element_type=F32)
        k_rope = jnp.dot(kpe_ctx_ref[...].astype(BF16), place_ref[...], preferred_element_type=F32)
        ka_ctx_ref[...] = (k_nope + k_rope).astype(BF16)
        va_ctx_ref[...] = jnp.dot(ckv16, w_v_ref[...], preferred_element_type=F32).astype(BF16)

    _mla_heads(qa_ref, [ka_ref, ka_ctx_ref], [va_ref, va_ctx_ref], o_ref)

    gq = H_B // KVH_B
    n_win = BQ_S + 2 * WINDOW
    start = pl.multiple_of(jnp.clip(i * BQ_S - WINDOW, 0, DEC_SEQ - n_win), WINDOW)
    row = lax.broadcasted_iota(jnp.int32, (gq * BQ_S, n_win), 0)
    qpos = i * BQ_S + (row & (BQ_S - 1))
    kpos = start + lax.broadcasted_iota(jnp.int32, (gq * BQ_S, n_win), 1)
    in_band = jnp.abs(qpos - kpos) <= WINDOW
    kwin = kb_ref[pl.ds(start, n_win), :].astype(BF16)
    vwin = vb_ref[pl.ds(start, n_win), :].astype(BF16)
    kctx = kb_ctx_ref[...].astype(BF16)
    vctx = vb_ctx_ref[...].astype(BF16)
    for kh in range(KVH_B):
        sl = slice(kh * HD, (kh + 1) * HD)
        o = _gqa_group(qb_ref, kh * gq * HD, gq, [kwin[:, sl], kctx[:, sl]], [vwin[:, sl], vctx[:, sl]],
                       mask=in_band, sink=_stacked_sink(sink_ref, kh * gq, gq, BQ_S))
        col = H_A * V_A + kh * gq * HD
        o_ref[:, col:col + gq * HD] = o.astype(o_ref.dtype)


def _attn_ab_sample(sink, qa, ka, va, qb, kb, vb, ckv_ctx, kpe_ctx, kb_ctx, vb_ctx, w_kn, w_v, place):
    qoff = N_PROMPT // BQ_S
    nq = DEC_SEQ // BQ_S
    boff = N_PROMPT // DEC_SEQ

    def qblk(w):
        return pl.BlockSpec((BQ_S, w), lambda b, i: (qoff + b * nq + i, 0))

    def bblk(w):
        return pl.BlockSpec((DEC_SEQ, w), lambda b, i: (boff + b, 0))

    def cblk(w):
        return pl.BlockSpec((PAST_LEN, w), lambda b, i: (b, 0))

    return pl.pallas_call(
        _attn_ab_sample_kernel,
        grid=(DEC_BATCH, nq),
        in_specs=[pl.BlockSpec(memory_space=pltpu.SMEM), qblk(H_A * LANES), bblk(H_A * LANES), bblk(H_A * V_A),
                  qblk(H_B * HD), bblk(KVH_B * HD), bblk(KVH_B * HD),
                  cblk(KV_LORA), cblk(LANES), cblk(KVH_B * HD), cblk(KVH_B * HD),
                  _const_spec((KV_LORA, H_A * LANES)), _const_spec((KV_LORA, H_A * V_A)),
                  _const_spec((LANES, H_A * LANES))],
        out_specs=pl.BlockSpec((BQ_S, D_MODEL), lambda b, i: (b * nq + i, 0)),
        out_shape=jax.ShapeDtypeStruct((N_SAMPLE, H_A * V_A + H_B * HD), BF16),
        scratch_shapes=[pltpu.VMEM((PAST_LEN, H_A * LANES), BF16), pltpu.VMEM((PAST_LEN, H_A * V_A), BF16)],
        compiler_params=_params(("arbitrary", "arbitrary")),
        name="attn_ab_sample",
    )(sink, qa, ka, va, qb, kb, vb, ckv_ctx, kpe_ctx, kb_ctx, vb_ctx, w_kn, w_v, place)


def _gqa_heads(q_ref, k_list, v_list, o_ref):
    gq = H_C // KVH_C
    for kh in range(KVH_C):
        sl = slice(kh * HD, (kh + 1) * HD)
        o = _gqa_group(q_ref, kh * gq * HD, gq, [k[:, sl] for k in k_list], [v[:, sl] for v in v_list])
        o_ref[:, kh * gq * HD:(kh + 1) * gq * HD] = o.astype(o_ref.dtype)


def _attn_c_prompt_kernel(q_ref, k_ref, v_ref, o_ref):
    _gqa_heads(q_ref, [k_ref[...].astype(BF16)], [v_ref[...].astype(BF16)], o_ref)


def _attn_c_prompt(q, k, v):
    def blk(w):
        return pl.BlockSpec((SEQ, w), lambda b: (b, 0))
    return pl.pallas_call(
        _attn_c_prompt_kernel,
        grid=(BATCH,),
        in_specs=[blk(H_C * HD), blk(KVH_C * HD), blk(KVH_C * HD)],
        out_specs=blk(H_C * HD),
        out_shape=jax.ShapeDtypeStruct((N_PROMPT, H_C * HD), BF16),
        compiler_params=_params(("arbitrary",)),
        name="attn_c_prompt",
    )(q, k, v)


def _attn_c_sample_kernel(q_ref, k_ref, v_ref, kc_ref, vc_ref, o_ref):
    _gqa_heads(q_ref, [k_ref[...].astype(BF16), kc_ref[...].astype(BF16)],
               [v_ref[...].astype(BF16), vc_ref[...].astype(BF16)], o_ref)


def _attn_c_sample(q, k, v, k_ctx, v_ctx):
    qoff = N_PROMPT // BQ_S
    nq = DEC_SEQ // BQ_S
    boff = N_PROMPT // DEC_SEQ
    w = KVH_C * HD
    return pl.pallas_call(
        _attn_c_sample_kernel,
        grid=(DEC_BATCH, nq),
        in_specs=[pl.BlockSpec((BQ_S, H_C * HD), lambda b, i: (qoff + b * nq + i, 0)),
                  pl.BlockSpec((DEC_SEQ, w), lambda b, i: (boff + b, 0)),
                  pl.BlockSpec((DEC_SEQ, w), lambda b, i: (boff + b, 0)),
                  pl.BlockSpec((PAST_LEN, w), lambda b, i: (b, 0)),
                  pl.BlockSpec((PAST_LEN, w), lambda b, i: (b, 0))],
        out_specs=pl.BlockSpec((BQ_S, H_C * HD), lambda b, i: (b * nq + i, 0)),
        out_shape=jax.ShapeDtypeStruct((N_SAMPLE, H_C * HD), BF16),
        compiler_params=_params(("arbitrary", "arbitrary")),
        name="attn_c_sample",
    )(q, k, v, k_ctx, v_ctx)


def _layer_norm(y, g, b):
    mu = jnp.mean(y, axis=-1, keepdims=True)
    yc = y - mu
    var = jnp.mean(yc * yc, axis=-1, keepdims=True)
    return yc * lax.rsqrt(var + EPS) * g + b


def _route(sel, aff):
    def row(a, j):
        return a[j:j + 1, :]

    scores = []
    for g in range(N_GROUPS):
        a0, a1, a2, a3 = (row(sel, 4 * g + j) for j in range(4))
        hi01, lo01 = jnp.maximum(a0, a1), jnp.minimum(a0, a1)
        hi23, lo23 = jnp.maximum(a2, a3), jnp.minimum(a2, a3)
        top1 = jnp.maximum(hi01, hi23)
        top2 = jnp.maximum(jnp.minimum(hi01, hi23), jnp.maximum(lo01, lo23))
        scores.append(top1 + top2)
    best = scores[0]
    gi = jnp.zeros(best.shape, jnp.int32)
    for g in range(1, N_GROUPS):
        better = scores[g] > best
        gi = jnp.where(better, g, gi)
        best = jnp.where(better, scores[g], best)

    def in_group(a, j):
        out = row(a, j)
        for g in range(1, N_GROUPS):
            out = jnp.where(gi == g, row(a, 4 * g + j), out)
        return out

    v = [in_group(sel, j) for j in range(4)]
    a = [in_group(aff, j) for j in range(4)]
    chosen = []
    for j in range(4):
        rank = jnp.zeros(best.shape, jnp.int32)
        for k in range(4):
            if k == j:
                continue
            ahead = (v[k] >= v[j]) if k < j else (v[k] > v[j])
            rank = rank + ahead.astype(jnp.int32)
        chosen.append(rank < 2)
    total = sum(jnp.where(chosen[j], a[j], 0.0) for j in range(4))
    w = [jnp.where(chosen[j], a[j], 0.0) / total for j in range(4)]
    pair = jnp.zeros(best.shape, jnp.int32)
    for p, (ja, jb) in enumerate(zip(SLOT_A_LOCAL, SLOT_B_LOCAL)):
        pair = jnp.where(chosen[ja] & chosen[jb], p, pair)
    gate_a = jnp.where(pair == 0, w[0], jnp.where(pair <= 2, w[2], w[3]))
    gate_b = jnp.where((pair == 0) | (pair == 1) | (pair == 4), w[1], jnp.where(pair == 5, w[2], w[0]))
    return gi * PAIRS_PER_GROUP + pair, gate_a, gate_b


def _post_attn_kernel(split_x, *refs):
    t = pl.program_id(0)
    op_ref, os_ref = refs[:2]
    if split_x:
        x = _pick(t, refs[2], refs[3])
        refs = refs[4:]
    else:
        x = refs[2][...]
        refs = refs[3:]
    (w_out_ref, gate_ref, shift_ref, scale_ref, lng_ref, lnb_ref, rw_ref, rb_ref, tri_ref,
     x1_ref, row_ref, route_ref, counts_ref, carry_ref) = refs
    o = _pick(t, op_ref, os_ref)
    a = jnp.dot(o, w_out_ref[...], preferred_element_type=F32)
    x1 = _layer_norm(ALPHA * x + gate_ref[...] * a, lng_ref[...], lnb_ref[...])
    x1_ref[...] = x1
    h2 = x1 * (1.0 + scale_ref[...]) + shift_ref[...]
    row_ref[:, :D_MODEL] = h2
    logits = lax.dot_general(rw_ref[...], h2, (((1,), (1,)), ((), ())), preferred_element_type=F32,
                             precision=lax.Precision.HIGHEST)
    aff = 1.0 / (1.0 + jnp.exp(-logits))
    bucket, gate_a, gate_b = _route(aff + rb_ref[...], aff)

    lane = lax.broadcasted_iota(jnp.int32, (LANES, TM), 0)
    meta_t = jnp.where(lane == 0, gate_a, jnp.where(lane == 1, gate_b, 0.0))
    row_ref[:, D_MODEL:] = meta_t.T

    @pl.when(t == 0)
    def _():
        carry_ref[...] = jnp.zeros_like(carry_ref)

    onehot = lax.broadcasted_iota(jnp.int32, (BUCKET_ROWS, TM), 0) == bucket
    prefix = jnp.dot(onehot.astype(BF16), tri_ref[...], preferred_element_type=F32)
    carry = carry_ref[...]
    rank = jnp.sum(jnp.where(onehot, prefix + carry[:, 0:1], 0.0), axis=0, keepdims=True)
    carry = carry + jnp.sum(onehot.astype(F32), axis=1, keepdims=True)
    carry_ref[...] = carry
    counts_ref[...] = carry
    route_ref[...] = jnp.zeros((8, TM), F32)
    route_ref[0:1, :] = bucket.astype(F32)
    route_ref[1:2, :] = rank


def _post_attn(layer, o_p, o_s, x_list, mod, w_out, ln_g, ln_b, rw_t, rb):
    split_x = len(x_list) == 2
    tok = pl.BlockSpec((TM, D_MODEL), lambda t: (t, 0))
    o_specs = [
        pl.BlockSpec((TM, D_MODEL), lambda t: (jnp.minimum(t, PROMPT_TILES - 1), 0)),
        pl.BlockSpec((TM, D_MODEL), lambda t: (jnp.maximum(t - PROMPT_TILES, 0), 0)),
    ]
    tri = jnp.asarray(np.arange(TM)[:, None] < np.arange(TM)[None, :], BF16)
    return pl.pallas_call(
        functools.partial(_post_attn_kernel, split_x),
        grid=(N_TILES,),
        in_specs=o_specs + (_x_specs() if split_x else [tok]) + [
            _const_spec((D_MODEL, D_MODEL)), _mod_spec(layer, 2), _mod_spec(layer, 3), _mod_spec(layer, 4),
            _const_spec((1, D_MODEL)), _const_spec((1, D_MODEL)), _const_spec((N_EXPERTS, D_MODEL)),
            _const_spec((N_EXPERTS, 1)), _const_spec((TM, TM))],
        out_specs=[tok, pl.BlockSpec((TM, ROW_W), lambda t: (t, 0)), pl.BlockSpec((8, TM), lambda t: (0, t)),
                   _const_spec((BUCKET_ROWS, LANES))],
        out_shape=[jax.ShapeDtypeStruct((N_TOK, D_MODEL), F32), jax.ShapeDtypeStruct((N_TOK, ROW_W), F32),
                   jax.ShapeDtypeStruct((8, N_TOK), F32), jax.ShapeDtypeStruct((BUCKET_ROWS, LANES), F32)],
        scratch_shapes=[pltpu.VMEM((BUCKET_ROWS, LANES), F32)],
        compiler_params=_params(("arbitrary",)),
        name="post_attn",
    )(o_p, o_s, *x_list, w_out, mod, mod, mod, ln_g, ln_b, rw_t, rb, tri)


def _moe_kernel(ta_ref, tb_ref, nt_ref, x_ref, wga_ref, wua_ref, wda_ref, wgb_ref, wub_ref, wdb_ref, o_ref,
                wga_s, wua_s, wda_s, wgb_s, wub_s, wdb_s):
    i = pl.program_id(0)
    prev = jnp.maximum(i - 1, 0)
    slots = ((ta_ref, (wga_ref, wua_ref, wda_ref), (wga_s, wua_s, wda_s)),
             (tb_ref, (wgb_ref, wub_ref, wdb_ref), (wgb_s, wub_s, wdb_s)))

    for t_ref, w_refs, w_scr in slots:
        @pl.when((i == 0) | (t_ref[i] != t_ref[prev]))
        def _(w_refs=w_refs, w_scr=w_scr):
            for w_ref, s_ref in zip(w_refs, w_scr):
                s_ref[...] = w_ref[...].astype(BF16)

    @pl.when(i < nt_ref[0])
    def _():
        x = x_ref[:, :D_MODEL].astype(BF16)
        acc = None
        for slot, (_, _, (wg_s, wu_s, wd_s)) in enumerate(slots):
            gt = jnp.dot(x, wg_s[...], preferred_element_type=F32)
            up = jnp.dot(x, wu_s[...], preferred_element_type=F32)
            gate = x_ref[:, D_MODEL + slot:D_MODEL + slot + 1]
            hid = (gt / (1.0 + jnp.exp(-gt))) * up * gate
            y = jnp.dot(hid.astype(BF16), wd_s[...], preferred_element_type=F32)
            acc = y if acc is None else acc + y
        o_ref[...] = acc

    @pl.when(i >= nt_ref[0])
    def _():
        o_ref[...] = jnp.zeros_like(o_ref)


def _moe(tile_a, tile_b, n_tiles, rows, w_gate, w_up, w_down):
    def row_idx(i, ta, tb, nt):
        return (jnp.minimum(i, nt[0] - 1), 0)

    def wspec(shape, which):
        if which == 0:
            return pl.BlockSpec((None,) + shape, lambda i, ta, tb, nt: (ta[i], 0, 0))
        return pl.BlockSpec((None,) + shape, lambda i, ta, tb, nt: (tb[i], 0, 0))

    up_shape, down_shape = (D_MODEL, D_EXPERT), (D_EXPERT, D_MODEL)
    grid_spec = pltpu.PrefetchScalarGridSpec(
        num_scalar_prefetch=3,
        grid=(MOE_TILES,),
        in_specs=[pl.BlockSpec((TME, ROW_W), row_idx),
                  wspec(up_shape, 0), wspec(up_shape, 0), wspec(down_shape, 0),
                  wspec(up_shape, 1), wspec(up_shape, 1), wspec(down_shape, 1)],
        out_specs=pl.BlockSpec((TME, D_MODEL), lambda i, ta, tb, nt: (i, 0)),
        scratch_shapes=[pltpu.VMEM(s, BF16) for s in (up_shape, up_shape, down_shape) * 2],
    )
    return pl.pallas_call(
        _moe_kernel,
        grid_spec=grid_spec,
        out_shape=jax.ShapeDtypeStruct((MOE_ROWS, D_MODEL), F32),
        compiler_params=_params(("arbitrary",)),
        name="moe_experts",
    )(tile_a, tile_b, n_tiles, rows, w_gate, w_up, w_down, w_gate, w_up, w_down)


def _row_copy(src_ref, src_row, dst_ref, dst_row, sem):
    return pltpu.make_async_copy(src_ref.at[pl.ds(src_row, 1), :], dst_ref.at[pl.ds(dst_row, 1), :], sem)


def _dispatch_kernel(pos_ref, pend_ref, cnt_ref, nt_ref, src_ref, out_ref, zero_ref, sem, zsem):
    t = pl.program_id(0)

    @pl.when(t == 0)
    def _():
        zero_ref[...] = jnp.zeros_like(zero_ref)

        def zero_tile(row0):
            return pltpu.make_async_copy(zero_ref, out_ref.at[pl.ds(pl.multiple_of(row0, TME), TME), :], zsem)

        for b in range(N_BUCKETS):
            @pl.when(cnt_ref[b] > 0)
            def _(b=b):
                zero_tile(pend_ref[b] - TME).start()

        def start_unused(i, carry):
            zero_tile(i * TME).start()
            return carry
        lax.fori_loop(nt_ref[0], MOE_TILES, start_unused, 0)

        for b in range(N_BUCKETS):
            @pl.when(cnt_ref[b] > 0)
            def _(b=b):
                zero_tile(pend_ref[b] - TME).wait()

        def wait_unused(i, carry):
            zero_tile(i * TME).wait()
            return carry
        lax.fori_loop(nt_ref[0], MOE_TILES, wait_unused, 0)

    def issue(r, carry):
        _row_copy(src_ref, r, out_ref, pos_ref[t * TM + r], sem).start()
        return carry
    lax.fori_loop(0, TM, issue, 0, unroll=8)
    pltpu.make_async_copy(src_ref, out_ref.at[pl.ds(0, TM), :], sem).wait()


def _dispatch(pos, pend, counts, n_tiles, rows):
    return pl.pallas_call(
        _dispatch_kernel,
        grid_spec=pltpu.PrefetchScalarGridSpec(
            num_scalar_prefetch=4, grid=(N_TILES,),
            in_specs=[pl.BlockSpec((TM, ROW_W), lambda t, *_: (t, 0))],
            out_specs=pl.BlockSpec(memory_space=pl.ANY),
            scratch_shapes=[pltpu.VMEM((TME, ROW_W), F32), pltpu.SemaphoreType.DMA, pltpu.SemaphoreType.DMA]),
        out_shape=jax.ShapeDtypeStruct((MOE_ROWS, ROW_W), F32),
        compiler_params=_params(("arbitrary",)),
        name="dispatch_rows",
    )(pos, pend, counts, n_tiles, rows)


def _plan(route, counts):
    bucket = route[0].astype(jnp.int32)
    rank = route[1].astype(jnp.int32)
    counts = counts[:N_BUCKETS, 0].astype(jnp.int32)
    padded = ((counts + TME - 1) // TME) * TME
    pend = jnp.cumsum(padded)
    pstart = pend - padded
    ids = jnp.arange(N_BUCKETS, dtype=jnp.int32)
    pos = rank + jnp.sum(jnp.where(bucket[None, :] == ids[:, None], pstart[:, None], 0), axis=0)
    n_tiles = pend[-1] // TME
    tile_start = jnp.minimum(jnp.arange(MOE_TILES, dtype=jnp.int32), n_tiles - 1) * TME
    tile_bucket = jnp.minimum(jnp.sum(tile_start[:, None] >= pend[None, :], axis=1), N_BUCKETS - 1)
    group, pair = tile_bucket // PAIRS_PER_GROUP, tile_bucket % PAIRS_PER_GROUP
    slot_a = jnp.asarray(SLOT_A_LOCAL, jnp.int32)
    slot_b = jnp.asarray(SLOT_B_LOCAL, jnp.int32)
    pair_hot = pair[:, None] == jnp.arange(PAIRS_PER_GROUP, dtype=jnp.int32)[None, :]
    tile_a = group * EXPERTS_PER_GROUP + jnp.sum(jnp.where(pair_hot, slot_a[None, :], 0), axis=1)
    tile_b = group * EXPERTS_PER_GROUP + jnp.sum(jnp.where(pair_hot, slot_b[None, :], 0), axis=1)
    return pos, pend, counts, tile_a.astype(jnp.int32), tile_b.astype(jnp.int32), n_tiles.reshape(1)


def _post_moe_kernel(split_out, pos_ref, f_ref, x1_ref, gate_ref, lng_ref, lnb_ref, *refs):
    out_refs, (fbuf, sem) = refs[:-2], refs[-2:]
    t = pl.program_id(0)

    def gather(tile, slot):
        def issue(r, carry):
            _row_copy(f_ref, pos_ref[tile * TM + r], fbuf.at[slot], r, sem.at[slot]).start()
            return carry
        lax.fori_loop(0, TM, issue, 0, unroll=8)

    @pl.when(t == 0)
    def _():
        gather(0, 0)

    @pl.when(t + 1 < N_TILES)
    def _():
        gather(t + 1, (t + 1) % 2)

    slot = t % 2
    pltpu.make_async_copy(f_ref.at[pl.ds(0, TM), :], fbuf.at[slot], sem.at[slot]).wait()
    y = _layer_norm(ALPHA * x1_ref[...] + gate_ref[...] * fbuf[slot], lng_ref[...], lnb_ref[...])
    if split_out:
        @pl.when(t < PROMPT_TILES)
        def _():
            out_refs[0][...] = y

        @pl.when(t >= PROMPT_TILES)
        def _():
            out_refs[1][...] = y
    else:
        out_refs[0][...] = y


def _post_moe(layer, pos, f_sorted, x1, mod, ln_g, ln_b, split_out):
    tok = pl.BlockSpec((TM, D_MODEL), lambda t, p: (t, 0))
    if split_out:
        out_specs = [pl.BlockSpec((TM, D_MODEL), lambda t, p: (jnp.minimum(t, PROMPT_TILES - 1), 0)),
                     pl.BlockSpec((TM, D_MODEL), lambda t, p: (jnp.maximum(t - PROMPT_TILES, 0), 0))]
        out_shape = [jax.ShapeDtypeStruct((N_PROMPT, D_MODEL), F32), jax.ShapeDtypeStruct((N_SAMPLE, D_MODEL), F32)]
    else:
        out_specs = [tok]
        out_shape = [jax.ShapeDtypeStruct((N_TOK, D_MODEL), F32)]
    mod_spec = pl.BlockSpec((None, None, None, 1, D_MODEL), lambda t, p: (layer, _mod_row(t), 5, 0, 0))
    return pl.pallas_call(
        functools.partial(_post_moe_kernel, split_out),
        grid_spec=pltpu.PrefetchScalarGridSpec(
            num_scalar_prefetch=1, grid=(N_TILES,),
            in_specs=[pl.BlockSpec(memory_space=pl.ANY), tok, mod_spec,
                      pl.BlockSpec((1, D_MODEL), lambda t, p: (0, 0)), pl.BlockSpec((1, D_MODEL), lambda t, p: (0, 0))],
            out_specs=out_specs,
            scratch_shapes=[pltpu.VMEM((2, TM, D_MODEL), F32), pltpu.SemaphoreType.DMA((2,))]),
        out_shape=out_shape,
        compiler_params=_params(("arbitrary",)),
        name="post_moe",
    )(pos, f_sorted, x1, mod, ln_g, ln_b)


def _ffn(layer, o_p, o_s, x_list, mod, w_out, ln_g, ln_b, rw_t, rb, w_gate, w_up, w_down, split_out):
    x1, rows, route, counts = _post_attn(layer, o_p, o_s, x_list, mod, w_out, ln_g[layer, 0][None],
                                         ln_b[layer, 0][None], rw_t, rb)
    pos, pend, counts, tile_a, tile_b, n_tiles = _plan(route, counts)
    rows_sorted = _dispatch(pos, pend, counts, n_tiles, rows)
    f_sorted = _moe(tile_a, tile_b, n_tiles, rows_sorted, w_gate, w_up, w_down)
    return _post_moe(layer, pos, f_sorted, x1, mod, ln_g[layer, 1][None], ln_b[layer, 1][None], split_out)


def kernel(x_prompt, x_sample, c, cache_mla_ckv, cache_mla_kpe, cache_swa_k, cache_swa_v, cache_gqa_k, cache_gqa_v, c_ctx, w_mod, b_mod, ln_g, ln_b, w_in_ab, mla_q_norm, mla_w_uq, mla_kv_norm, mla_w_ukv, swa_sink, w_out_ab, w_in_c, gqa_q_norm, gqa_k_norm, w_out_c, router_w, router_bias, exp_w_gate, exp_w_up, exp_w_down):
    xp = x_prompt.reshape(N_PROMPT, D_MODEL)
    xs = x_sample.reshape(N_SAMPLE, D_MODEL)
    cond = jnp.concatenate([c_ctx[None], c, jnp.zeros((MOD_ROWS - 1 - DEC_BATCH, D_MODEL), F32)], axis=0)
    mod = _modulation(cond, w_mod, b_mod).reshape(DEPTH, MOD_ROWS, 6, 1, D_MODEL)
    tabs = _rope_tables()
    rw_t = router_w.T
    rb = router_bias.reshape(N_EXPERTS, 1)

    w = w_in_ab[0]
    pad = jnp.zeros((D_MODEL, AB_COLS - w.shape[1]), F32)
    w_in = jnp.concatenate([w[:, :384], w[:, 416:], w[:, 384:416], pad], axis=1).astype(BF16)
    w_uq, w_kn, w_v, place = _mla_weights(mla_w_uq[0], mla_w_ukv[0])
    qa, ka, va, qb, ckv, kb, vb, kpe = _inproj_ab(xp, xs, mod, w_in, mla_q_norm[0][None], mla_kv_norm[0][None],
                                                  w_uq, w_kn, w_v, place, tabs)
    sink = swa_sink[0]
    o_p = _attn_ab_prompt(sink, qa, ka, va, qb, kb, vb)
    kpe_ctx = jnp.pad(cache_mla_kpe[:, 0].reshape(-1, ROPE_A), ((0, 0), (0, LANES - ROPE_A)))
    o_s = _attn_ab_sample(sink, qa, ka, va, qb, kb, vb,
                          cache_mla_ckv[:, 0].reshape(-1, KV_LORA), kpe_ctx,
                          cache_swa_k[:, 0].reshape(-1, KVH_B * HD), cache_swa_v[:, 0].reshape(-1, KVH_B * HD),
                          w_kn, w_v, place)
    (x2,) = _ffn(0, o_p, o_s, [xp, xs], mod, w_out_ab[0].astype(BF16), ln_g, ln_b, rw_t, rb,
                 exp_w_gate[0], exp_w_up[0], exp_w_down[0], split_out=False)

    g_full = jnp.concatenate([jnp.tile(gqa_q_norm[0], H_C), jnp.tile(gqa_k_norm[0], KVH_C)])[None]
    qc, kc, vc = _inproj_c(x2, mod, w_in_c[0].astype(BF16), g_full, tabs)
    oc_p = _attn_c_prompt(qc, kc, vc)
    oc_s = _attn_c_sample(qc, kc, vc, cache_gqa_k[:, 0].reshape(-1, KVH_C * HD),
                          cache_gqa_v[:, 0].reshape(-1, KVH_C * HD))
    y_p, y_s = _ffn(1, oc_p, oc_s, [x2], mod, w_out_c[0].astype(BF16), ln_g, ln_b, rw_t, rb,
                    exp_w_gate[1], exp_w_up[1], exp_w_down[1], split_out=True)

    y_prompt = y_p.reshape(BATCH, SEQ, D_MODEL)
    y_sample = y_s.reshape(DEC_BATCH, DEC_SEQ, D_MODEL)
    new_ckv = ckv[:N_PROMPT].reshape(BATCH, 1, SEQ, KV_LORA)
    new_kpe = kpe[:N_PROMPT, :ROPE_A].reshape(BATCH, 1, SEQ, ROPE_A)
    new_swk = kb[:N_PROMPT].reshape(BATCH, 1, SEQ, KVH_B, HD)
    new_swv = vb[:N_PROMPT].reshape(BATCH, 1, SEQ, KVH_B, HD)
    new_gk = kc[:N_PROMPT].reshape(BATCH, 1, SEQ, KVH_C, HD)
    new_gv = vc[:N_PROMPT].reshape(BATCH, 1, SEQ, KVH_C, HD)
    return y_prompt, y_sample, new_ckv, new_kpe, new_swk, new_swv, new_gk, new_gv
```

```python
import functools

import numpy as np
import jax
import jax.numpy as jnp
from jax import lax
from jax.experimental import pallas as pl
from jax.experimental.pallas import tpu as pltpu

D_MODEL = 1024
BATCH = 32
SEQ = 256
DEPTH = 2
DEC_BATCH = 4
DEC_SEQ = 1024
PAST_LEN = 256
GRID_W = 64
ROPE_THETA = 10000.0
WINDOW = 128
HD = 64
H_A = 8
NOPE_A = 64
ROPE_A = 32
V_A = 64
Q_LORA = 256
KV_LORA = 128
H_B = 8
KVH_B = 2
H_C = 16
KVH_C = 4
N_EXPERTS = 16
N_GROUPS = 4
EXPERTS_PER_GROUP = 4
D_EXPERT = 512
ALPHA = (2 * DEPTH) ** 0.25
NEG_INF = -1e30
EPS = 1e-6

LANES = 128
N_PROMPT = BATCH * SEQ
N_SAMPLE = DEC_BATCH * DEC_SEQ
N_TOK = N_PROMPT + N_SAMPLE
TM = 256
N_TILES = N_TOK // TM
PROMPT_TILES = N_PROMPT // TM
SAMPLE_TILES_PER_BATCH = DEC_SEQ // TM
BQ_S = 256
LOG2E = 1.4426950408889634
MOD_ROWS = 8

PAIRS_PER_GROUP = 6
N_BUCKETS = N_GROUPS * PAIRS_PER_GROUP
SLOT_A_LOCAL = (0, 2, 2, 3, 3, 3)
SLOT_B_LOCAL = (1, 1, 0, 0, 1, 2)
TME = 256
MOE_TILES = -(-(N_TOK + N_BUCKETS * (TME - 1)) // TME)
MOE_ROWS = MOE_TILES * TME

BUCKET_ROWS = 32
ROW_W = D_MODEL + LANES
STAGE_SLOTS = 3
GATHER_SLOTS = 3

AB_COLS = 1280
C_NORM_COLS = H_C * HD + KVH_C * HD
NORM_BLOCK = 256

BF16 = jnp.bfloat16
F32 = jnp.float32
VMEM_LIMIT = 52 * 1024 * 1024


def _mod_row(t):
    return jnp.where(t < PROMPT_TILES, 0, 1 + (t - PROMPT_TILES) // SAMPLE_TILES_PER_BATCH)


def _mod_spec(layer, chunk):
    return pl.BlockSpec((None, None, None, 1, D_MODEL), lambda t: (layer, _mod_row(t), chunk, 0, 0))


def _const_spec(shape):
    nd = len(shape)
    return pl.BlockSpec(shape, lambda *_: (0,) * nd)


def _x_specs():
    return [
        pl.BlockSpec((TM, D_MODEL), lambda t: (jnp.minimum(t, PROMPT_TILES - 1), 0)),
        pl.BlockSpec((TM, D_MODEL), lambda t: (jnp.maximum(t - PROMPT_TILES, 0), 0)),
    ]


def _pick(t, a_ref, b_ref):
    return jnp.where(t < PROMPT_TILES, a_ref[...], b_ref[...])


def _params(sem):
    return pltpu.CompilerParams(dimension_semantics=sem, vmem_limit_bytes=VMEM_LIMIT)


def _mod_kernel(cond_ref, w_ref, b_ref, o_ref):
    c = cond_ref[...]
    s = (c / (1.0 + jnp.exp(-c))).astype(BF16)
    o_ref[...] = jnp.dot(s, w_ref[...].astype(BF16), preferred_element_type=F32) + b_ref[...]


def _modulation(cond, w_mod, b_mod):
    tn = 1536
    return pl.pallas_call(
        _mod_kernel,
        grid=(DEPTH, 6 * D_MODEL // tn),
        in_specs=[
            pl.BlockSpec((MOD_ROWS, D_MODEL), lambda l, j: (0, 0)),
            pl.BlockSpec((None, D_MODEL, tn), lambda l, j: (l, 0, j)),
            pl.BlockSpec((None, 1, tn), lambda l, j: (l, 0, j)),
        ],
        out_specs=pl.BlockSpec((None, MOD_ROWS, tn), lambda l, j: (l, 0, j)),
        out_shape=jax.ShapeDtypeStruct((DEPTH, MOD_ROWS, 6 * D_MODEL), F32),
        compiler_params=_params(("arbitrary", "arbitrary")),
        name="modulation",
    )(cond, w_mod, b_mod.reshape(DEPTH, 1, 6 * D_MODEL))


def _grid_cos_sin(half):
    n_freq = half // 2
    rows = DEC_SEQ // GRID_W
    row = jnp.repeat(jnp.arange(rows, dtype=F32), GRID_W)
    col = jnp.tile(jnp.arange(GRID_W, dtype=F32), rows)
    inv = 1.0 / (ROPE_THETA ** (jnp.arange(n_freq, dtype=F32) / n_freq))
    ang = jnp.concatenate([row[:, None] * inv, col[:, None] * inv], axis=-1)
    return jnp.cos(ang), jnp.sin(ang)


def _rope_tables():
    tabs = []
    for half in (HD // 2, ROPE_A // 2):
        cos, sin = _grid_cos_sin(half)
        reps = LANES // (2 * half)
        tabs.append(jnp.tile(jnp.concatenate([cos, cos], axis=-1), (1, reps)))
        tabs.append(jnp.tile(jnp.concatenate([-sin, sin], axis=-1), (1, reps)))
    half = ROPE_A // 2
    one = jnp.ones((DEC_SEQ, NOPE_A), F32)
    zero = jnp.zeros((DEC_SEQ, NOPE_A), F32)
    pad1 = jnp.ones((DEC_SEQ, LANES - NOPE_A - ROPE_A), F32)
    pad0 = jnp.zeros((DEC_SEQ, LANES - NOPE_A - ROPE_A), F32)
    zh = jnp.zeros((DEC_SEQ, half), F32)
    tabs.append(jnp.concatenate([one, cos, cos, pad1], axis=-1))
    tabs.append(jnp.concatenate([zero, -sin, zh, pad0], axis=-1))
    tabs.append(jnp.concatenate([zero, zh, sin, pad0], axis=-1))
    return tuple(tabs)


def _rope(x, c_ref, s_ref, half):
    w = x.shape[1]
    reps = w // LANES
    c = c_ref[...]
    s = s_ref[...]
    if reps > 1:
        c = jnp.concatenate([c] * reps, axis=1)
        s = jnp.concatenate([s] * reps, axis=1)
    ahead = pltpu.roll(x, w - half, 1)
    behind = pltpu.roll(x, half, 1)
    lane = lax.broadcasted_iota(jnp.int32, x.shape, 1)
    swapped = jnp.where((lane & (2 * half - 1)) < half, ahead, behind)
    return x * c + swapped * s


def _rope_specs(n):
    def idx(t):
        return (jnp.maximum(t - PROMPT_TILES, 0) % SAMPLE_TILES_PER_BATCH, 0)
    return [pl.BlockSpec((TM, LANES), idx)] * n


def _inproj_ab_kernel(xp_ref, xs_ref, shift_ref, scale_ref, w_in_ref, qn_ref, kvn_ref, w_uq_ref, w_kn_ref, w_v_ref,
                      place_ref, c64_ref, s64_ref, c32_ref, s32_ref, ca_ref, sa_ref, sb_ref,
                      qa_ref, ka_ref, va_ref, qb_ref, ckv_ref, kb_ref, vb_ref, kpe_ref):
    t = pl.program_id(0)
    x = _pick(t, xp_ref, xs_ref)
    h = (x * (1.0 + scale_ref[...]) + shift_ref[...]).astype(BF16)
    z = jnp.dot(h, w_in_ref[...], preferred_element_type=F32)

    ql = z[:, :Q_LORA]
    ql = ql * lax.rsqrt(jnp.mean(ql * ql, axis=-1, keepdims=True) + EPS) * qn_ref[...]
    qa = jnp.dot(ql.astype(BF16), w_uq_ref[...], preferred_element_type=F32)
    c = z[:, Q_LORA:Q_LORA + KV_LORA]
    ckv = c * lax.rsqrt(jnp.mean(c * c, axis=-1, keepdims=True) + EPS) * kvn_ref[...]
    ckv_ref[...] = ckv
    ckv16 = ckv.astype(BF16)
    k_nope = jnp.dot(ckv16, w_kn_ref[...], preferred_element_type=F32)
    va_ref[...] = jnp.dot(ckv16, w_v_ref[...], preferred_element_type=F32).astype(BF16)
    vb_ref[...] = z[:, 1024:1152]
    qb = z[:, 384:896]
    kb = z[:, 896:1024]
    kpe = z[:, 1152:1280]

    def finish(qa_out, qb_out, kb_out, kpe_out):
        qa_ref[...] = qa_out.astype(BF16)
        qb_ref[...] = qb_out.astype(BF16)
        kb_ref[...] = kb_out
        kpe_ref[...] = kpe_out
        ka_ref[...] = (k_nope + jnp.dot(kpe_out.astype(BF16), place_ref[...],
                                        preferred_element_type=F32)).astype(BF16)

    @pl.when(t < PROMPT_TILES)
    def _():
        finish(qa, qb, kb, kpe)

    @pl.when(t >= PROMPT_TILES)
    def _():
        w = qa.shape[1]
        reps = w // LANES
        ca = jnp.concatenate([ca_ref[...]] * reps, axis=1)
        sa = jnp.concatenate([sa_ref[...]] * reps, axis=1)
        sb = jnp.concatenate([sb_ref[...]] * reps, axis=1)
        half = ROPE_A // 2
        qa_rot = qa * ca + pltpu.roll(qa, w - half, 1) * sa + pltpu.roll(qa, half, 1) * sb
        finish(qa_rot, _rope(qb, c64_ref, s64_ref, HD // 2), _rope(kb, c64_ref, s64_ref, HD // 2),
               _rope(kpe, c32_ref, s32_ref, ROPE_A // 2))


def _mla_weights(w_uq, w_ukv):
    pad = LANES - NOPE_A - ROPE_A
    uq = w_uq.reshape(Q_LORA, H_A, NOPE_A + ROPE_A)
    uq = jnp.pad(uq, ((0, 0), (0, 0), (0, pad))).reshape(Q_LORA, H_A * LANES)
    ukv = w_ukv.reshape(KV_LORA, H_A, NOPE_A + V_A)
    kn = jnp.pad(ukv[:, :, :NOPE_A], ((0, 0), (0, 0), (0, LANES - NOPE_A))).reshape(KV_LORA, H_A * LANES)
    wv = ukv[:, :, NOPE_A:].reshape(KV_LORA, H_A * V_A)
    src = np.arange(LANES)[:, None]
    dst = np.arange(H_A * LANES)[None, :] % LANES
    place = (src < ROPE_A) & (dst == src + NOPE_A)
    return uq.astype(BF16), kn.astype(BF16), wv.astype(BF16), jnp.asarray(place, BF16)


def _inproj_ab(xp, xs, mod, w_in, q_norm, kv_norm, w_uq, w_kn, w_v, place, tabs):
    def tok(w):
        return pl.BlockSpec((TM, w), lambda t: (t, 0))
    widths = [(H_A * LANES, BF16), (H_A * LANES, BF16), (H_A * V_A, BF16), (H_B * HD, BF16),
              (KV_LORA, F32), (KVH_B * HD, F32), (KVH_B * HD, F32), (LANES, F32)]
    return pl.pallas_call(
        _inproj_ab_kernel,
        grid=(N_TILES,),
        in_specs=_x_specs() + [_mod_spec(0, 0), _mod_spec(0, 1), _const_spec((D_MODEL, AB_COLS)),
                               _const_spec((1, Q_LORA)), _const_spec((1, KV_LORA)),
                               _const_spec((Q_LORA, H_A * LANES)), _const_spec((KV_LORA, H_A * LANES)),
                               _const_spec((KV_LORA, H_A * V_A)), _const_spec((LANES, H_A * LANES))]
        + _rope_specs(7),
        out_specs=[tok(w) for w, _ in widths],
        out_shape=[jax.ShapeDtypeStruct((N_TOK, w), dt) for w, dt in widths],
        compiler_params=_params(("arbitrary",)),
        name="inproj_ab",
    )(xp, xs, mod, mod, w_in, q_norm, kv_norm, w_uq, w_kn, w_v, place, *tabs)


def _inproj_c_kernel(x_ref, shift_ref, scale_ref, w_in_ref, g_ref, ones_ref, c64_ref, s64_ref, q_ref, k_ref, v_ref):
    t = pl.program_id(0)
    h = (x_ref[...] * (1.0 + scale_ref[...]) + shift_ref[...]).astype(BF16)
    z = jnp.dot(h, w_in_ref[...], preferred_element_type=F32)
    v_ref[...] = z[:, C_NORM_COLS:]
    qk = z[:, :C_NORM_COLS]
    sq = qk * qk
    sq_hi = sq.astype(BF16)
    sq_lo = (sq - sq_hi.astype(F32)).astype(BF16)
    blocks = []
    for j in range(C_NORM_COLS // NORM_BLOCK):
        sl = slice(j * NORM_BLOCK, (j + 1) * NORM_BLOCK)
        blocks.append(jnp.dot(sq_hi[:, sl], ones_ref[...], preferred_element_type=F32)
                      + jnp.dot(sq_lo[:, sl], ones_ref[...], preferred_element_type=F32))
    ss = jnp.concatenate(blocks, axis=1)
    qk = qk * lax.rsqrt(ss * (1.0 / HD) + EPS) * g_ref[...]

    @pl.when(t < PROMPT_TILES)
    def _():
        q_ref[...] = qk[:, :H_C * HD].astype(BF16)
        k_ref[...] = qk[:, H_C * HD:]

    @pl.when(t >= PROMPT_TILES)
    def _():
        q_ref[...] = _rope(qk[:, :H_C * HD], c64_ref, s64_ref, HD // 2).astype(BF16)
        k_ref[...] = _rope(qk[:, H_C * HD:], c64_ref, s64_ref, HD // 2)


def _inproj_c(x, mod, w_in, g_full, tabs):
    def tok(w):
        return pl.BlockSpec((TM, w), lambda t: (t, 0))
    widths = [(H_C * HD, BF16), (KVH_C * HD, F32), (KVH_C * HD, F32)]
    head = np.arange(NORM_BLOCK) // HD
    ones = jnp.asarray(head[:, None] == head[None, :], BF16)
    return pl.pallas_call(
        _inproj_c_kernel,
        grid=(N_TILES,),
        in_specs=[tok(D_MODEL), _mod_spec(1, 0), _mod_spec(1, 1), _const_spec((D_MODEL, C_NORM_COLS + KVH_C * HD)),
                  _const_spec((1, C_NORM_COLS)), _const_spec((NORM_BLOCK, NORM_BLOCK))] + _rope_specs(2),
        out_specs=[tok(w) for w, _ in widths],
        out_shape=[jax.ShapeDtypeStruct((N_TOK, w), dt) for w, dt in widths],
        compiler_params=_params(("arbitrary",)),
        name="inproj_c",
    )(x, mod, mod, w_in, g_full, ones, *tabs[:2])


def _qk(q, k):
    return lax.dot_general(q, k, (((1,), (1,)), ((), ())), preferred_element_type=F32)


def _softmax_pv(scores, values, scale, sink=None):
    c = scale * LOG2E
    m = scores[0].max(axis=-1, keepdims=True)
    for s in scores[1:]:
        m = jnp.maximum(m, s.max(axis=-1, keepdims=True))
    m = m * c
    if sink is not None:
        sink = sink * LOG2E
        m = jnp.maximum(m, sink)
    den = None
    out = None
    for s, v in zip(scores, values):
        p = jnp.exp2(s * c - m)
        d = p.sum(axis=-1, keepdims=True)
        o = jnp.dot(p.astype(BF16), v, preferred_element_type=F32)
        den = d if den is None else den + d
        out = o if out is None else out + o
    if sink is not None:
        den = den + jnp.exp2(sink - m)
    return out * (1.0 / den)


def _mla_heads(qa_ref, ka_refs, va_refs, o_ref):
    scale = (NOPE_A + ROPE_A) ** -0.5
    for h0 in range(0, H_A, 2):
        pair = []
        for h in (h0, h0 + 1):
            q = qa_ref[:, h * LANES:(h + 1) * LANES]
            scores = [_qk(q, ka_ref[:, h * LANES:(h + 1) * LANES]) for ka_ref in ka_refs]
            values = [va_ref[:, h * V_A:(h + 1) * V_A] for va_ref in va_refs]
            pair.append(_softmax_pv(scores, values, scale))
        o_ref[:, h0 * V_A:(h0 + 2) * V_A] = jnp.concatenate(pair, axis=1).astype(o_ref.dtype)


def _gqa_group(q_ref, col0, n_q, ks, vs, mask=None, sink=None):
    rows = q_ref.shape[0]
    q = jnp.concatenate([q_ref[:, col0 + g * HD:col0 + (g + 1) * HD] for g in range(n_q)], axis=0)
    scores = [_qk(q, k) for k in ks]
    if mask is not None:
        scores[0] = jnp.where(mask, scores[0], NEG_INF)
    o = _softmax_pv(scores, vs, HD ** -0.5, sink)
    return jnp.concatenate([o[g * rows:(g + 1) * rows] for g in range(n_q)], axis=1)


def _stacked_sink(sink_ref, h0, n_q, rows):
    head = lax.broadcasted_iota(jnp.int32, (n_q * rows, 1), 0) // rows
    col = jnp.full((n_q * rows, 1), sink_ref[h0], F32)
    for g in range(1, n_q):
        col = jnp.where(head == g, sink_ref[h0 + g], col)
    return col


def _attn_ab_prompt_kernel(sink_ref, qa_ref, ka_ref, va_ref, qb_ref, kb_ref, vb_ref, o_ref):
    _mla_heads(qa_ref, [ka_ref], [va_ref], o_ref)
    kb = kb_ref[...].astype(BF16)
    vb = vb_ref[...].astype(BF16)
    gq = H_B // KVH_B
    for kh in range(KVH_B):
        sl = slice(kh * HD, (kh + 1) * HD)
        o = _gqa_group(qb_ref, kh * gq * HD, gq, [kb[:, sl]], [vb[:, sl]],
                       sink=_stacked_sink(sink_ref, kh * gq, gq, SEQ))
        col = H_A * V_A + kh * gq * HD
        o_ref[:, col:col + gq * HD] = o.astype(o_ref.dtype)


def _attn_ab_prompt(sink, qa, ka, va, qb, kb, vb):
    def blk(w):
        return pl.BlockSpec((SEQ, w), lambda b: (b, 0))
    return pl.pallas_call(
        _attn_ab_prompt_kernel,
        grid=(BATCH,),
        in_specs=[pl.BlockSpec(memory_space=pltpu.SMEM), blk(H_A * LANES), blk(H_A * LANES), blk(H_A * V_A),
                  blk(H_B * HD), blk(KVH_B * HD), blk(KVH_B * HD)],
        out_specs=blk(D_MODEL),
        out_shape=jax.ShapeDtypeStruct((N_PROMPT, H_A * V_A + H_B * HD), BF16),
        compiler_params=_params(("arbitrary",)),
        name="attn_ab_prompt",
    )(sink, qa, ka, va, qb, kb, vb)


def _attn_ab_sample_kernel(sink_ref, qa_ref, ka_ref, va_ref, qb_ref, kb_ref, vb_ref,
                           ckv_ctx_ref, kpe_ctx_ref, kb_ctx_ref, vb_ctx_ref, w_kn_ref, w_v_ref, place_ref,
                           o_ref, ka_ctx_ref, va_ctx_ref):
    i = pl.program_id(1)

    @pl.when(i == 0)
    def _():
        ckv16 = ckv_ctx_ref[...].astype(BF16)
        k_nope = jnp.dot(ckv16, w_kn_ref[...], preferred_element_type=F32)
        k_rope = jnp.dot(kpe_ctx_ref[...].astype(BF16), place_ref[...], preferred_element_type=F32)
        ka_ctx_ref[...] = (k_nope + k_rope).astype(BF16)
        va_ctx_ref[...] = jnp.dot(ckv16, w_v_ref[...], preferred_element_type=F32).astype(BF16)

    _mla_heads(qa_ref, [ka_ref, ka_ctx_ref], [va_ref, va_ctx_ref], o_ref)

    gq = H_B // KVH_B
    n_win = BQ_S + 2 * WINDOW
    start = pl.multiple_of(jnp.clip(i * BQ_S - WINDOW, 0, DEC_SEQ - n_win), WINDOW)
    row = lax.broadcasted_iota(jnp.int32, (gq * BQ_S, n_win), 0)
    qpos = i * BQ_S + (row & (BQ_S - 1))
    kpos = start + lax.broadcasted_iota(jnp.int32, (gq * BQ_S, n_win), 1)
    in_band = jnp.abs(qpos - kpos) <= WINDOW
    kwin = kb_ref[pl.ds(start, n_win), :].astype(BF16)
    vwin = vb_ref[pl.ds(start, n_win), :].astype(BF16)
    kctx = kb_ctx_ref[...].astype(BF16)
    vctx = vb_ctx_ref[...].astype(BF16)
    for kh in range(KVH_B):
        sl = slice(kh * HD, (kh + 1) * HD)
        o = _gqa_group(qb_ref, kh * gq * HD, gq, [kwin[:, sl], kctx[:, sl]], [vwin[:, sl], vctx[:, sl]],
                       mask=in_band, sink=_stacked_sink(sink_ref, kh * gq, gq, BQ_S))
        col = H_A * V_A + kh * gq * HD
        o_ref[:, col:col + gq * HD] = o.astype(o_ref.dtype)


def _attn_ab_sample(sink, qa, ka, va, qb, kb, vb, ckv_ctx, kpe_ctx, kb_ctx, vb_ctx, w_kn, w_v, place):
    qoff = N_PROMPT // BQ_S
    nq = DEC_SEQ // BQ_S
    boff = N_PROMPT // DEC_SEQ

    def qblk(w):
        return pl.BlockSpec((BQ_S, w), lambda b, i: (qoff + b * nq + i, 0))

    def bblk(w):
        return pl.BlockSpec((DEC_SEQ, w), lambda b, i: (boff + b, 0))

    def cblk(w):
        return pl.BlockSpec((PAST_LEN, w), lambda b, i: (b, 0))

    return pl.pallas_call(
        _attn_ab_sample_kernel,
        grid=(DEC_BATCH, nq),
        in_specs=[pl.BlockSpec(memory_space=pltpu.SMEM), qblk(H_A * LANES), bblk(H_A * LANES), bblk(H_A * V_A),
                  qblk(H_B * HD), bblk(KVH_B * HD), bblk(KVH_B * HD),
                  cblk(KV_LORA), cblk(LANES), cblk(KVH_B * HD), cblk(KVH_B * HD),
                  _const_spec((KV_LORA, H_A * LANES)), _const_spec((KV_LORA, H_A * V_A)),
                  _const_spec((LANES, H_A * LANES))],
        out_specs=pl.BlockSpec((BQ_S, D_MODEL), lambda b, i: (b * nq + i, 0)),
        out_shape=jax.ShapeDtypeStruct((N_SAMPLE, H_A * V_A + H_B * HD), BF16),
        scratch_shapes=[pltpu.VMEM((PAST_LEN, H_A * LANES), BF16), pltpu.VMEM((PAST_LEN, H_A * V_A), BF16)],
        compiler_params=_params(("arbitrary", "arbitrary")),
        name="attn_ab_sample",
    )(sink, qa, ka, va, qb, kb, vb, ckv_ctx, kpe_ctx, kb_ctx, vb_ctx, w_kn, w_v, place)


def _gqa_heads(q_ref, k_list, v_list, o_ref):
    gq = H_C // KVH_C
    for kh in range(KVH_C):
        sl = slice(kh * HD, (kh + 1) * HD)
        o = _gqa_group(q_ref, kh * gq * HD, gq, [k[:, sl] for k in k_list], [v[:, sl] for v in v_list])
        o_ref[:, kh * gq * HD:(kh + 1) * gq * HD] = o.astype(o_ref.dtype)


def _attn_c_prompt_kernel(q_ref, k_ref, v_ref, o_ref):
    _gqa_heads(q_ref, [k_ref[...].astype(BF16)], [v_ref[...].astype(BF16)], o_ref)


def _attn_c_prompt(q, k, v):
    def blk(w):
        return pl.BlockSpec((SEQ, w), lambda b: (b, 0))
    return pl.pallas_call(
        _attn_c_prompt_kernel,
        grid=(BATCH,),
        in_specs=[blk(H_C * HD), blk(KVH_C * HD), blk(KVH_C * HD)],
        out_specs=blk(H_C * HD),
        out_shape=jax.ShapeDtypeStruct((N_PROMPT, H_C * HD), BF16),
        compiler_params=_params(("arbitrary",)),
        name="attn_c_prompt",
    )(q, k, v)


def _attn_c_sample_kernel(q_ref, k_ref, v_ref, kc_ref, vc_ref, o_ref):
    _gqa_heads(q_ref, [k_ref[...].astype(BF16), kc_ref[...].astype(BF16)],
               [v_ref[...].astype(BF16), vc_ref[...].astype(BF16)], o_ref)


def _attn_c_sample(q, k, v, k_ctx, v_ctx):
    qoff = N_PROMPT // BQ_S
    nq = DEC_SEQ // BQ_S
    boff = N_PROMPT // DEC_SEQ
    w = KVH_C * HD
    return pl.pallas_call(
        _attn_c_sample_kernel,
        grid=(DEC_BATCH, nq),
        in_specs=[pl.BlockSpec((BQ_S, H_C * HD), lambda b, i: (qoff + b * nq + i, 0)),
                  pl.BlockSpec((DEC_SEQ, w), lambda b, i: (boff + b, 0)),
                  pl.BlockSpec((DEC_SEQ, w), lambda b, i: (boff + b, 0)),
                  pl.BlockSpec((PAST_LEN, w), lambda b, i: (b, 0)),
                  pl.BlockSpec((PAST_LEN, w), lambda b, i: (b, 0))],
        out_specs=pl.BlockSpec((BQ_S, H_C * HD), lambda b, i: (b * nq + i, 0)),
        out_shape=jax.ShapeDtypeStruct((N_SAMPLE, H_C * HD), BF16),
        compiler_params=_params(("arbitrary", "arbitrary")),
        name="attn_c_sample",
    )(q, k, v, k_ctx, v_ctx)


def _layer_norm(y, g, b):
    mu = jnp.mean(y, axis=-1, keepdims=True)
    yc = y - mu
    var = jnp.mean(yc * yc, axis=-1, keepdims=True)
    return yc * lax.rsqrt(var + EPS) * g + b


def _route(sel, aff):
    def row(a, j):
        return a[j:j + 1, :]

    scores = []
    for g in range(N_GROUPS):
        a0, a1, a2, a3 = (row(sel, 4 * g + j) for j in range(4))
        hi01, lo01 = jnp.maximum(a0, a1), jnp.minimum(a0, a1)
        hi23, lo23 = jnp.maximum(a2, a3), jnp.minimum(a2, a3)
        top1 = jnp.maximum(hi01, hi23)
        top2 = jnp.maximum(jnp.minimum(hi01, hi23), jnp.maximum(lo01, lo23))
        scores.append(top1 + top2)
    best = scores[0]
    gi = jnp.zeros(best.shape, jnp.int32)
    for g in range(1, N_GROUPS):
        better = scores[g] > best
        gi = jnp.where(better, g, gi)
        best = jnp.where(better, scores[g], best)

    def in_group(a, j):
        out = row(a, j)
        for g in range(1, N_GROUPS):
            out = jnp.where(gi == g, row(a, 4 * g + j), out)
        return out

    v = [in_group(sel, j) for j in range(4)]
    a = [in_group(aff, j) for j in range(4)]
    chosen = []
    for j in range(4):
        rank = jnp.zeros(best.shape, jnp.int32)
        for k in range(4):
            if k == j:
                continue
            ahead = (v[k] >= v[j]) if k < j else (v[k] > v[j])
            rank = rank + ahead.astype(jnp.int32)
        chosen.append(rank < 2)
    total = sum(jnp.where(chosen[j], a[j], 0.0) for j in range(4))
    w = [jnp.where(chosen[j], a[j], 0.0) / total for j in range(4)]
    pair = jnp.zeros(best.shape, jnp.int32)
    for p, (ja, jb) in enumerate(zip(SLOT_A_LOCAL, SLOT_B_LOCAL)):
        pair = jnp.where(chosen[ja] & chosen[jb], p, pair)
    gate_a = jnp.where(pair == 0, w[0], jnp.where(pair <= 2, w[2], w[3]))
    gate_b = jnp.where((pair == 0) | (pair == 1) | (pair == 4), w[1], jnp.where(pair == 5, w[2], w[0]))
    return gi * PAIRS_PER_GROUP + pair, gate_a, gate_b


def _post_attn_kernel(split_x, *refs):
    t = pl.program_id(0)
    op_ref, os_ref = refs[:2]
    if split_x:
        x = _pick(t, refs[2], refs[3])
        refs = refs[4:]
    else:
        x = refs[2][...]
        refs = refs[3:]
    (w_out_ref, gate_ref, shift_ref, scale_ref, lng_ref, lnb_ref, rw_ref, rb_ref, tri_ref,
     x1_ref, row_ref, route_ref, counts_ref, carry_ref) = refs
    o = _pick(t, op_ref, os_ref)
    a = jnp.dot(o, w_out_ref[...], preferred_element_type=F32)
    x1 = _layer_norm(ALPHA * x + gate_ref[...] * a, lng_ref[...], lnb_ref[...])
    x1_ref[...] = x1
    h2 = x1 * (1.0 + scale_ref[...]) + shift_ref[...]
    row_ref[:, :D_MODEL] = h2
    logits = lax.dot_general(rw_ref[...], h2, (((1,), (1,)), ((), ())), preferred_element_type=F32,
                             precision=lax.Precision.HIGHEST)
    aff = 1.0 / (1.0 + jnp.exp(-logits))
    bucket, gate_a, gate_b = _route(aff + rb_ref[...], aff)

    lane = lax.broadcasted_iota(jnp.int32, (LANES, TM), 0)
    meta_t = jnp.where(lane == 0, gate_a, jnp.where(lane == 1, gate_b, 0.0))
    row_ref[:, D_MODEL:] = meta_t.T

    @pl.when(t == 0)
    def _():
        carry_ref[...] = jnp.zeros_like(carry_ref)

    onehot = lax.broadcasted_iota(jnp.int32, (BUCKET_ROWS, TM), 0) == bucket
    prefix = jnp.dot(onehot.astype(BF16), tri_ref[...], preferred_element_type=F32)
    carry = carry_ref[...]
    rank = jnp.sum(jnp.where(onehot, prefix + carry[:, 0:1], 0.0), axis=0, keepdims=True)
    carry = carry + jnp.sum(onehot.astype(F32), axis=1, keepdims=True)
    carry_ref[...] = carry
    counts_ref[...] = carry
    route_ref[...] = jnp.zeros((8, TM), F32)
    route_ref[0:1, :] = bucket.astype(F32)
    route_ref[1:2, :] = rank


def _post_attn(layer, o_p, o_s, x_list, mod, w_out, ln_g, ln_b, rw_t, rb):
    split_x = len(x_list) == 2
    tok = pl.BlockSpec((TM, D_MODEL), lambda t: (t, 0))
    o_specs = [
        pl.BlockSpec((TM, D_MODEL), lambda t: (jnp.minimum(t, PROMPT_TILES - 1), 0)),
        pl.BlockSpec((TM, D_MODEL), lambda t: (jnp.maximum(t - PROMPT_TILES, 0), 0)),
    ]
    tri = jnp.asarray(np.arange(TM)[:, None] < np.arange(TM)[None, :], BF16)
    return pl.pallas_call(
        functools.partial(_post_attn_kernel, split_x),
        grid=(N_TILES,),
        in_specs=o_specs + (_x_specs() if split_x else [tok]) + [
            _const_spec((D_MODEL, D_MODEL)), _mod_spec(layer, 2), _mod_spec(layer, 3), _mod_spec(layer, 4),
            _const_spec((1, D_MODEL)), _const_spec((1, D_MODEL)), _const_spec((N_EXPERTS, D_MODEL)),
            _const_spec((N_EXPERTS, 1)), _const_spec((TM, TM))],
        out_specs=[tok, pl.BlockSpec((TM, ROW_W), lambda t: (t, 0)), pl.BlockSpec((8, TM), lambda t: (0, t)),
                   _const_spec((BUCKET_ROWS, LANES))],
        out_shape=[jax.ShapeDtypeStruct((N_TOK, D_MODEL), F32), jax.ShapeDtypeStruct((N_TOK, ROW_W), F32),
                   jax.ShapeDtypeStruct((8, N_TOK), F32), jax.ShapeDtypeStruct((BUCKET_ROWS, LANES), F32)],
        scratch_shapes=[pltpu.VMEM((BUCKET_ROWS, LANES), F32)],
        compiler_params=_params(("arbitrary",)),
        name="post_attn",
    )(o_p, o_s, *x_list, w_out, mod, mod, mod, ln_g, ln_b, rw_t, rb, tri)


def _moe_kernel(ta_ref, tb_ref, nt_ref, x_ref, wga_ref, wua_ref, wda_ref, wgb_ref, wub_ref, wdb_ref, o_ref,
                wga_s, wua_s, wda_s, wgb_s, wub_s, wdb_s):
    i = pl.program_id(0)
    prev = jnp.maximum(i - 1, 0)
    slots = ((ta_ref, (wga_ref, wua_ref, wda_ref), (wga_s, wua_s, wda_s)),
             (tb_ref, (wgb_ref, wub_ref, wdb_ref), (wgb_s, wub_s, wdb_s)))

    for t_ref, w_refs, w_scr in slots:
        @pl.when((i == 0) | (t_ref[i] != t_ref[prev]))
        def _(w_refs=w_refs, w_scr=w_scr):
            for w_ref, s_ref in zip(w_refs, w_scr):
                s_ref[...] = w_ref[...].astype(BF16)

    @pl.when(i < nt_ref[0])
    def _():
        x = x_ref[:, :D_MODEL].astype(BF16)
        acc = None
        for slot, (_, _, (wg_s, wu_s, wd_s)) in enumerate(slots):
            gt = jnp.dot(x, wg_s[...], preferred_element_type=F32)
            up = jnp.dot(x, wu_s[...], preferred_element_type=F32)
            gate = x_ref[:, D_MODEL + slot:D_MODEL + slot + 1]
            hid = (gt / (1.0 + jnp.exp(-gt))) * up * gate
            y = jnp.dot(hid.astype(BF16), wd_s[...], preferred_element_type=F32)
            acc = y if acc is None else acc + y
        o_ref[...] = acc

    @pl.when(i >= nt_ref[0])
    def _():
        o_ref[...] = jnp.zeros_like(o_ref)


def _moe(layer, tile_a, tile_b, n_tiles, rows, w_gate, w_up, w_down):
    def row_idx(i, ta, tb, nt):
        return (jnp.minimum(i, nt[0] - 1), 0)

    def wspec(shape, which):
        if which == 0:
            return pl.BlockSpec((None, None) + shape, lambda i, ta, tb, nt: (layer, ta[i], 0, 0))
        return pl.BlockSpec((None, None) + shape, lambda i, ta, tb, nt: (layer, tb[i], 0, 0))

    up_shape, down_shape = (D_MODEL, D_EXPERT), (D_EXPERT, D_MODEL)
    grid_spec = pltpu.PrefetchScalarGridSpec(
        num_scalar_prefetch=3,
        grid=(MOE_TILES,),
        in_specs=[pl.BlockSpec((TME, ROW_W), row_idx),
                  wspec(up_shape, 0), wspec(up_shape, 0), wspec(down_shape, 0),
                  wspec(up_shape, 1), wspec(up_shape, 1), wspec(down_shape, 1)],
        out_specs=pl.BlockSpec((TME, D_MODEL), lambda i, ta, tb, nt: (i, 0)),
        scratch_shapes=[pltpu.VMEM(s, BF16) for s in (up_shape, up_shape, down_shape) * 2],
    )
    return pl.pallas_call(
        _moe_kernel,
        grid_spec=grid_spec,
        out_shape=jax.ShapeDtypeStruct((MOE_ROWS, D_MODEL), F32),
        compiler_params=_params(("arbitrary",)),
        name="moe_experts",
    )(tile_a, tile_b, n_tiles, rows, w_gate, w_up, w_down, w_gate, w_up, w_down)


def _row_copy(src_ref, src_row, dst_ref, dst_row, sem):
    return pltpu.make_async_copy(src_ref.at[pl.ds(src_row, 1), :], dst_ref.at[pl.ds(dst_row, 1), :], sem)


def _dispatch_kernel(pos_ref, pend_ref, cnt_ref, nt_ref, src_ref, out_ref, buf, zero_ref, in_sem, sem, zsem):
    t = pl.program_id(0)

    def fetch(tile):
        return pltpu.make_async_copy(src_ref.at[pl.ds(pl.multiple_of(tile * TM, TM), TM), :],
                                     buf.at[tile % STAGE_SLOTS], in_sem.at[tile % STAGE_SLOTS])

    @pl.when(t == 0)
    def _():
        fetch(0).start()
        zero_ref[...] = jnp.zeros_like(zero_ref)

        def zero_tile(row0):
            return pltpu.make_async_copy(zero_ref, out_ref.at[pl.ds(pl.multiple_of(row0, TME), TME), :], zsem)

        for b in range(N_BUCKETS):
            @pl.when(cnt_ref[b] > 0)
            def _(b=b):
                zero_tile(pend_ref[b] - TME).start()

        def start_unused(i, carry):
            zero_tile(i * TME).start()
            return carry
        lax.fori_loop(nt_ref[0], MOE_TILES, start_unused, 0)

        for b in range(N_BUCKETS):
            @pl.when(cnt_ref[b] > 0)
            def _(b=b):
                zero_tile(pend_ref[b] - TME).wait()

        def wait_unused(i, carry):
            zero_tile(i * TME).wait()
            return carry
        lax.fori_loop(nt_ref[0], MOE_TILES, wait_unused, 0)

    @pl.when(t + 1 < N_TILES)
    def _():
        fetch(t + 1).start()

    fetch(t).wait()
    tile_ref = buf.at[t % STAGE_SLOTS]

    def issue(r, carry):
        _row_copy(tile_ref, r, out_ref, pos_ref[t * TM + r], sem.at[t % 2]).start()
        return carry
    lax.fori_loop(0, TM, issue, 0, unroll=8)

    def drain(tile):
        pltpu.make_async_copy(buf.at[0], out_ref.at[pl.ds(0, TM), :], sem.at[tile % 2]).wait()

    @pl.when(t > 0)
    def _():
        drain(t - 1)

    @pl.when(t == N_TILES - 1)
    def _():
        drain(t)


def _dispatch(pos, pend, counts, n_tiles, rows):
    return pl.pallas_call(
        _dispatch_kernel,
        grid_spec=pltpu.PrefetchScalarGridSpec(
            num_scalar_prefetch=4, grid=(N_TILES,),
            in_specs=[pl.BlockSpec(memory_space=pl.ANY)],
            out_specs=pl.BlockSpec(memory_space=pl.ANY),
            scratch_shapes=[pltpu.VMEM((STAGE_SLOTS, TM, ROW_W), F32), pltpu.VMEM((TME, ROW_W), F32),
                            pltpu.SemaphoreType.DMA((STAGE_SLOTS,)), pltpu.SemaphoreType.DMA((2,)),
                            pltpu.SemaphoreType.DMA]),
        out_shape=jax.ShapeDtypeStruct((MOE_ROWS, ROW_W), F32),
        compiler_params=_params(("arbitrary",)),
        name="dispatch_rows",
    )(pos, pend, counts, n_tiles, rows)


def _plan(route, counts):
    bucket = route[0].astype(jnp.int32)
    rank = route[1].astype(jnp.int32)
    counts = counts[:N_BUCKETS, 0].astype(jnp.int32)
    padded = ((counts + TME - 1) // TME) * TME
    pend = jnp.cumsum(padded)
    pstart = pend - padded
    ids = jnp.arange(N_BUCKETS, dtype=jnp.int32)
    pos = rank + jnp.sum(jnp.where(bucket[None, :] == ids[:, None], pstart[:, None], 0), axis=0)
    n_tiles = pend[-1] // TME
    tile_start = jnp.minimum(jnp.arange(MOE_TILES, dtype=jnp.int32), n_tiles - 1) * TME
    tile_bucket = jnp.minimum(jnp.sum(tile_start[:, None] >= pend[None, :], axis=1), N_BUCKETS - 1)
    group, pair = tile_bucket // PAIRS_PER_GROUP, tile_bucket % PAIRS_PER_GROUP
    slot_a = jnp.asarray(SLOT_A_LOCAL, jnp.int32)
    slot_b = jnp.asarray(SLOT_B_LOCAL, jnp.int32)
    pair_hot = pair[:, None] == jnp.arange(PAIRS_PER_GROUP, dtype=jnp.int32)[None, :]
    tile_a = group * EXPERTS_PER_GROUP + jnp.sum(jnp.where(pair_hot, slot_a[None, :], 0), axis=1)
    tile_b = group * EXPERTS_PER_GROUP + jnp.sum(jnp.where(pair_hot, slot_b[None, :], 0), axis=1)
    return pos, pend, counts, tile_a.astype(jnp.int32), tile_b.astype(jnp.int32), n_tiles.reshape(1)


def _post_moe_kernel(split_out, pos_ref, f_ref, x1_ref, gate_ref, lng_ref, lnb_ref, *refs):
    out_refs, (fbuf, sem) = refs[:-2], refs[-2:]
    t = pl.program_id(0)

    def wait_tile(slot):
        pltpu.make_async_copy(f_ref.at[pl.ds(0, TM), :], fbuf.at[slot], sem.at[slot]).wait()

    @pl.when(t == 0)
    def _():
        def issue(r, carry):
            _row_copy(f_ref, pos_ref[r], fbuf.at[r // TM], r % TM, sem.at[r // TM]).start()
            return carry
        lax.fori_loop(0, (GATHER_SLOTS - 1) * TM, issue, 0, unroll=8)

    slot = t % GATHER_SLOTS
    wait_tile(slot)
    ahead = jnp.minimum(t + GATHER_SLOTS - 1, N_TILES - 1)
    aslot = (t + GATHER_SLOTS - 1) % GATHER_SLOTS
    for r in range(TM):
        _row_copy(f_ref, pos_ref[ahead * TM + r], fbuf.at[aslot], r, sem.at[aslot]).start()
    y = _layer_norm(ALPHA * x1_ref[...] + gate_ref[...] * fbuf[slot], lng_ref[...], lnb_ref[...])

    @pl.when(t == N_TILES - 1)
    def _():
        for k in range(1, GATHER_SLOTS):
            wait_tile((t + k) % GATHER_SLOTS)
    if split_out:
        @pl.when(t < PROMPT_TILES)
        def _():
            out_refs[0][...] = y

        @pl.when(t >= PROMPT_TILES)
        def _():
            out_refs[1][...] = y
    else:
        out_refs[0][...] = y


def _post_moe(layer, pos, f_sorted, x1, mod, ln_g, ln_b, split_out):
    tok = pl.BlockSpec((TM, D_MODEL), lambda t, p: (t, 0))
    if split_out:
        out_specs = [pl.BlockSpec((TM, D_MODEL), lambda t, p: (jnp.minimum(t, PROMPT_TILES - 1), 0)),
                     pl.BlockSpec((TM, D_MODEL), lambda t, p: (jnp.maximum(t - PROMPT_TILES, 0), 0))]
        out_shape = [jax.ShapeDtypeStruct((N_PROMPT, D_MODEL), F32), jax.ShapeDtypeStruct((N_SAMPLE, D_MODEL), F32)]
    else:
        out_specs = [tok]
        out_shape = [jax.ShapeDtypeStruct((N_TOK, D_MODEL), F32)]
    mod_spec = pl.BlockSpec((None, None, None, 1, D_MODEL), lambda t, p: (layer, _mod_row(t), 5, 0, 0))
    return pl.pallas_call(
        functools.partial(_post_moe_kernel, split_out),
        grid_spec=pltpu.PrefetchScalarGridSpec(
            num_scalar_prefetch=1, grid=(N_TILES,),
            in_specs=[pl.BlockSpec(memory_space=pl.ANY), tok, mod_spec,
                      pl.BlockSpec((1, D_MODEL), lambda t, p: (0, 0)), pl.BlockSpec((1, D_MODEL), lambda t, p: (0, 0))],
            out_specs=out_specs,
            scratch_shapes=[pltpu.VMEM((GATHER_SLOTS, TM, D_MODEL), F32),
                            pltpu.SemaphoreType.DMA((GATHER_SLOTS,))]),
        out_shape=out_shape,
        compiler_params=_params(("arbitrary",)),
        name="post_moe",
    )(pos, f_sorted, x1, mod, ln_g, ln_b)


def _ffn(layer, o_p, o_s, x_list, mod, w_out, ln_g, ln_b, rw_t, rb, w_gate, w_up, w_down, split_out):
    x1, rows, route, counts = _post_attn(layer, o_p, o_s, x_list, mod, w_out, ln_g[layer, 0][None],
                                         ln_b[layer, 0][None], rw_t, rb)
    pos, pend, counts, tile_a, tile_b, n_tiles = _plan(route, counts)
    rows_sorted = _dispatch(pos, pend, counts, n_tiles, rows)
    f_sorted = _moe(layer, tile_a, tile_b, n_tiles, rows_sorted, w_gate, w_up, w_down)
    return _post_moe(layer, pos, f_sorted, x1, mod, ln_g[layer, 1][None], ln_b[layer, 1][None], split_out)


def kernel(x_prompt, x_sample, c, cache_mla_ckv, cache_mla_kpe, cache_swa_k, cache_swa_v, cache_gqa_k, cache_gqa_v, c_ctx, w_mod, b_mod, ln_g, ln_b, w_in_ab, mla_q_norm, mla_w_uq, mla_kv_norm, mla_w_ukv, swa_sink, w_out_ab, w_in_c, gqa_q_norm, gqa_k_norm, w_out_c, router_w, router_bias, exp_w_gate, exp_w_up, exp_w_down):
    xp = x_prompt.reshape(N_PROMPT, D_MODEL)
    xs = x_sample.reshape(N_SAMPLE, D_MODEL)
    cond = jnp.concatenate([c_ctx[None], c, jnp.zeros((MOD_ROWS - 1 - DEC_BATCH, D_MODEL), F32)], axis=0)
    mod = _modulation(cond, w_mod, b_mod).reshape(DEPTH, MOD_ROWS, 6, 1, D_MODEL)
    tabs = _rope_tables()
    rw_t = router_w.T
    rb = router_bias.reshape(N_EXPERTS, 1)

    w = w_in_ab[0]
    pad = jnp.zeros((D_MODEL, AB_COLS - w.shape[1]), F32)
    w_in = jnp.concatenate([w[:, :384], w[:, 416:], w[:, 384:416], pad], axis=1).astype(BF16)
    w_uq, w_kn, w_v, place = _mla_weights(mla_w_uq[0], mla_w_ukv[0])
    qa, ka, va, qb, ckv, kb, vb, kpe = _inproj_ab(xp, xs, mod, w_in, mla_q_norm[0][None], mla_kv_norm[0][None],
                                                  w_uq, w_kn, w_v, place, tabs)
    sink = swa_sink[0]
    o_p = _attn_ab_prompt(sink, qa, ka, va, qb, kb, vb)
    kpe_ctx = jnp.pad(cache_mla_kpe[:, 0].reshape(-1, ROPE_A), ((0, 0), (0, LANES - ROPE_A)))
    o_s = _attn_ab_sample(sink, qa, ka, va, qb, kb, vb,
                          cache_mla_ckv[:, 0].reshape(-1, KV_LORA), kpe_ctx,
                          cache_swa_k[:, 0].reshape(-1, KVH_B * HD), cache_swa_v[:, 0].reshape(-1, KVH_B * HD),
                          w_kn, w_v, place)
    (x2,) = _ffn(0, o_p, o_s, [xp, xs], mod, w_out_ab[0].astype(BF16), ln_g, ln_b, rw_t, rb,
                 exp_w_gate, exp_w_up, exp_w_down, split_out=False)

    g_full = jnp.concatenate([jnp.tile(gqa_q_norm[0], H_C), jnp.tile(gqa_k_norm[0], KVH_C)])[None]
    qc, kc, vc = _inproj_c(x2, mod, w_in_c[0].astype(BF16), g_full, tabs)
    oc_p = _attn_c_prompt(qc, kc, vc)
    oc_s = _attn_c_sample(qc, kc, vc, cache_gqa_k[:, 0].reshape(-1, KVH_C * HD),
                          cache_gqa_v[:, 0].reshape(-1, KVH_C * HD))
    y_p, y_s = _ffn(1, oc_p, oc_s, [x2], mod, w_out_c[0].astype(BF16), ln_g, ln_b, rw_t, rb,
                    exp_w_gate, exp_w_up, exp_w_down, split_out=True)

    y_prompt = y_p.reshape(BATCH, SEQ, D_MODEL)
    y_sample = y_s.reshape(DEC_BATCH, DEC_SEQ, D_MODEL)
    new_ckv = ckv[:N_PROMPT].reshape(BATCH, 1, SEQ, KV_LORA)
    new_kpe = kpe[:N_PROMPT, :ROPE_A].reshape(BATCH, 1, SEQ, ROPE_A)
    new_swk = kb[:N_PROMPT].reshape(BATCH, 1, SEQ, KVH_B, HD)
    new_swv = vb[:N_PROMPT].reshape(BATCH, 1, SEQ, KVH_B, HD)
    new_gk = kc[:N_PROMPT].reshape(BATCH, 1, SEQ, KVH_C, HD)
    new_gv = vc[:N_PROMPT].reshape(BATCH, 1, SEQ, KVH_C, HD)
    return y_prompt, y_sample, new_ckv, new_kpe, new_swk, new_swv, new_gk, new_gv
```

```python
import functools

import numpy as np
import jax
import jax.numpy as jnp
from jax import lax
from jax.experimental import pallas as pl
from jax.experimental.pallas import tpu as pltpu

D_MODEL = 1024
BATCH = 32
SEQ = 256
DEPTH = 2
DEC_BATCH = 4
DEC_SEQ = 1024
PAST_LEN = 256
GRID_W = 64
ROPE_THETA = 10000.0
WINDOW = 128
HD = 64
H_A = 8
NOPE_A = 64
ROPE_A = 32
V_A = 64
Q_LORA = 256
KV_LORA = 128
H_B = 8
KVH_B = 2
H_C = 16
KVH_C = 4
N_EXPERTS = 16
N_GROUPS = 4
EXPERTS_PER_GROUP = 4
D_EXPERT = 512
ALPHA = (2 * DEPTH) ** 0.25
NEG_INF = -1e30
EPS = 1e-6

LANES = 128
N_PROMPT = BATCH * SEQ
N_SAMPLE = DEC_BATCH * DEC_SEQ
N_TOK = N_PROMPT + N_SAMPLE
TM = 256
TP = 512
N_TILES = N_TOK // TM
PROMPT_TILES = N_PROMPT // TM
BQ_S = 256
LOG2E = 1.4426950408889634
MOD_ROWS = 8

PAIRS_PER_GROUP = 6
N_BUCKETS = N_GROUPS * PAIRS_PER_GROUP
SLOT_A_LOCAL = (0, 2, 2, 3, 3, 3)
SLOT_B_LOCAL = (1, 1, 0, 0, 1, 2)
TME = 256
MOE_TILES = -(-(N_TOK + N_BUCKETS * (TME - 1)) // TME)
MOE_ROWS = MOE_TILES * TME

BUCKET_ROWS = 32
ROW_W = D_MODEL + LANES
STAGE_SLOTS = 3
GATHER_SLOTS = 3

AB_COLS = 1280
C_NORM_COLS = H_C * HD + KVH_C * HD
NORM_BLOCK = 256

BF16 = jnp.bfloat16
F32 = jnp.float32
VMEM_LIMIT = 52 * 1024 * 1024


def _mod_row(t, tile=TM):
    p_tiles = N_PROMPT // tile
    return jnp.where(t < p_tiles, 0, 1 + (t - p_tiles) // (DEC_SEQ // tile))


def _mod_spec(layer, chunk, tile=TM):
    return pl.BlockSpec((None, None, None, 1, D_MODEL), lambda t: (layer, _mod_row(t, tile), chunk, 0, 0))


def _const_spec(shape):
    nd = len(shape)
    return pl.BlockSpec(shape, lambda *_: (0,) * nd)


def _split_specs(tile=TM):
    p_tiles = N_PROMPT // tile
    return [
        pl.BlockSpec((tile, D_MODEL), lambda t: (jnp.minimum(t, p_tiles - 1), 0)),
        pl.BlockSpec((tile, D_MODEL), lambda t: (jnp.maximum(t - p_tiles, 0), 0)),
    ]


def _pick(t, a_ref, b_ref):
    return jnp.where(t < N_PROMPT // a_ref.shape[0], a_ref[...], b_ref[...])


def _sub_tiles(rows):
    return [slice(i * TM, (i + 1) * TM) for i in range(rows // TM)]


def _params(sem):
    return pltpu.CompilerParams(dimension_semantics=sem, vmem_limit_bytes=VMEM_LIMIT)


def _mod_kernel(cond_ref, w_ref, b_ref, o_ref):
    c = cond_ref[...]
    s = (c / (1.0 + jnp.exp(-c))).astype(BF16)
    o_ref[...] = jnp.dot(s, w_ref[...].astype(BF16), preferred_element_type=F32) + b_ref[...]


def _modulation(cond, w_mod, b_mod):
    tn = 1536
    return pl.pallas_call(
        _mod_kernel,
        grid=(DEPTH, 6 * D_MODEL // tn),
        in_specs=[
            pl.BlockSpec((MOD_ROWS, D_MODEL), lambda l, j: (0, 0)),
            pl.BlockSpec((None, D_MODEL, tn), lambda l, j: (l, 0, j)),
            pl.BlockSpec((None, 1, tn), lambda l, j: (l, 0, j)),
        ],
        out_specs=pl.BlockSpec((None, MOD_ROWS, tn), lambda l, j: (l, 0, j)),
        out_shape=jax.ShapeDtypeStruct((DEPTH, MOD_ROWS, 6 * D_MODEL), F32),
        compiler_params=_params(("arbitrary", "arbitrary")),
        name="modulation",
    )(cond, w_mod, b_mod.reshape(DEPTH, 1, 6 * D_MODEL))


def _grid_cos_sin(half):
    n_freq = half // 2
    rows = DEC_SEQ // GRID_W
    row = jnp.repeat(jnp.arange(rows, dtype=F32), GRID_W)
    col = jnp.tile(jnp.arange(GRID_W, dtype=F32), rows)
    inv = 1.0 / (ROPE_THETA ** (jnp.arange(n_freq, dtype=F32) / n_freq))
    ang = jnp.concatenate([row[:, None] * inv, col[:, None] * inv], axis=-1)
    return jnp.cos(ang), jnp.sin(ang)


def _rope_tables():
    tabs = []
    for half in (HD // 2, ROPE_A // 2):
        cos, sin = _grid_cos_sin(half)
        reps = LANES // (2 * half)
        tabs.append(jnp.tile(jnp.concatenate([cos, cos], axis=-1), (1, reps)))
        tabs.append(jnp.tile(jnp.concatenate([-sin, sin], axis=-1), (1, reps)))
    half = ROPE_A // 2
    one = jnp.ones((DEC_SEQ, NOPE_A), F32)
    zero = jnp.zeros((DEC_SEQ, NOPE_A), F32)
    pad1 = jnp.ones((DEC_SEQ, LANES - NOPE_A - ROPE_A), F32)
    pad0 = jnp.zeros((DEC_SEQ, LANES - NOPE_A - ROPE_A), F32)
    zh = jnp.zeros((DEC_SEQ, half), F32)
    tabs.append(jnp.concatenate([one, cos, cos, pad1], axis=-1))
    tabs.append(jnp.concatenate([zero, -sin, zh, pad0], axis=-1))
    tabs.append(jnp.concatenate([zero, zh, sin, pad0], axis=-1))
    return tuple(tabs)


def _rope(x, c, s, half):
    w = x.shape[1]
    reps = w // LANES
    if reps > 1:
        c = jnp.concatenate([c] * reps, axis=1)
        s = jnp.concatenate([s] * reps, axis=1)
    ahead = pltpu.roll(x, w - half, 1)
    behind = pltpu.roll(x, half, 1)
    lane = lax.broadcasted_iota(jnp.int32, x.shape, 1)
    swapped = jnp.where((lane & (2 * half - 1)) < half, ahead, behind)
    return x * c + swapped * s


def _rope_specs(n, tile):
    p_tiles = N_PROMPT // tile

    def idx(t):
        return (jnp.maximum(t - p_tiles, 0) % (DEC_SEQ // tile), 0)
    return [pl.BlockSpec((tile, LANES), idx)] * n


def _rms(x, g_ref):
    return x * lax.rsqrt(jnp.mean(x * x, axis=-1, keepdims=True) + EPS) * g_ref[...]


def _inproj_ab_kernel(xp_ref, xs_ref, shift_ref, scale_ref, w_in_ref, qn_ref, kvn_ref, w_uq_ref, w_kn_ref, w_v_ref,
                      place_ref, c64_ref, s64_ref, c32_ref, s32_ref, ca_ref, sa_ref, sb_ref,
                      qa_ref, ka_ref, va_ref, qb_ref, ckv_ref, kb_ref, vb_ref, kpe_ref):
    t = pl.program_id(0)
    x = _pick(t, xp_ref, xs_ref)
    subs = _sub_tiles(x.shape[0])
    h = [(x[s] * (1.0 + scale_ref[...]) + shift_ref[...]).astype(BF16) for s in subs]
    z = [jnp.dot(h_s, w_in_ref[...], preferred_element_type=F32) for h_s in h]
    ql = [_rms(z_s[:, :Q_LORA], qn_ref).astype(BF16) for z_s in z]
    qa = [jnp.dot(ql_s, w_uq_ref[...], preferred_element_type=F32) for ql_s in ql]
    ckv = [_rms(z_s[:, Q_LORA:Q_LORA + KV_LORA], kvn_ref) for z_s in z]
    k_nope = [jnp.dot(c_s.astype(BF16), w_kn_ref[...], preferred_element_type=F32) for c_s in ckv]
    va = [jnp.dot(c_s.astype(BF16), w_v_ref[...], preferred_element_type=F32).astype(BF16) for c_s in ckv]
    for s, z_s, c_s, va_s in zip(subs, z, ckv, va):
        ckv_ref[s, :] = c_s
        va_ref[s, :] = va_s
        vb_ref[s, :] = z_s[:, 1024:1152]
    qb = [z_s[:, 384:896] for z_s in z]
    kb = [z_s[:, 896:1024] for z_s in z]
    kpe = [z_s[:, 1152:1280] for z_s in z]

    def finish(qa_out, qb_out, kb_out, kpe_out):
        k_rope = [jnp.dot(kpe_s.astype(BF16), place_ref[...], preferred_element_type=F32) for kpe_s in kpe_out]
        for i, s in enumerate(subs):
            qa_ref[s, :] = qa_out[i].astype(BF16)
            qb_ref[s, :] = qb_out[i].astype(BF16)
            kb_ref[s, :] = kb_out[i]
            kpe_ref[s, :] = kpe_out[i]
            ka_ref[s, :] = (k_nope[i] + k_rope[i]).astype(BF16)

    p_tiles = N_PROMPT // x.shape[0]

    @pl.when(t < p_tiles)
    def _():
        finish(qa, qb, kb, kpe)

    @pl.when(t >= p_tiles)
    def _():
        w = qa[0].shape[1]
        reps = w // LANES
        half = ROPE_A // 2
        qa_rot, qb_rot, kb_rot, kpe_rot = [], [], [], []
        for i, s in enumerate(subs):
            ca = jnp.concatenate([ca_ref[s, :]] * reps, axis=1)
            sa = jnp.concatenate([sa_ref[s, :]] * reps, axis=1)
            sb = jnp.concatenate([sb_ref[s, :]] * reps, axis=1)
            qa_rot.append(qa[i] * ca + pltpu.roll(qa[i], w - half, 1) * sa + pltpu.roll(qa[i], half, 1) * sb)
            qb_rot.append(_rope(qb[i], c64_ref[s, :], s64_ref[s, :], HD // 2))
            kb_rot.append(_rope(kb[i], c64_ref[s, :], s64_ref[s, :], HD // 2))
            kpe_rot.append(_rope(kpe[i], c32_ref[s, :], s32_ref[s, :], ROPE_A // 2))
        finish(qa_rot, qb_rot, kb_rot, kpe_rot)


def _mla_weights(w_uq, w_ukv):
    pad = LANES - NOPE_A - ROPE_A
    uq = w_uq.reshape(Q_LORA, H_A, NOPE_A + ROPE_A)
    uq = jnp.pad(uq, ((0, 0), (0, 0), (0, pad))).reshape(Q_LORA, H_A * LANES)
    ukv = w_ukv.reshape(KV_LORA, H_A, NOPE_A + V_A)
    kn = jnp.pad(ukv[:, :, :NOPE_A], ((0, 0), (0, 0), (0, LANES - NOPE_A))).reshape(KV_LORA, H_A * LANES)
    wv = ukv[:, :, NOPE_A:].reshape(KV_LORA, H_A * V_A)
    src = np.arange(LANES)[:, None]
    dst = np.arange(H_A * LANES)[None, :] % LANES
    place = (src < ROPE_A) & (dst == src + NOPE_A)
    return uq.astype(BF16), kn.astype(BF16), wv.astype(BF16), jnp.asarray(place, BF16)


def _inproj_ab(xp, xs, mod, w_in, q_norm, kv_norm, w_uq, w_kn, w_v, place, tabs):
    def tok(w):
        return pl.BlockSpec((TP, w), lambda t: (t, 0))
    widths = [(H_A * LANES, BF16), (H_A * LANES, BF16), (H_A * V_A, BF16), (H_B * HD, BF16),
              (KV_LORA, F32), (KVH_B * HD, F32), (KVH_B * HD, F32), (LANES, F32)]
    return pl.pallas_call(
        _inproj_ab_kernel,
        grid=(N_TOK // TP,),
        in_specs=_split_specs(TP) + [_mod_spec(0, 0, TP), _mod_spec(0, 1, TP), _const_spec((D_MODEL, AB_COLS)),
                                     _const_spec((1, Q_LORA)), _const_spec((1, KV_LORA)),
                                     _const_spec((Q_LORA, H_A * LANES)), _const_spec((KV_LORA, H_A * LANES)),
                                     _const_spec((KV_LORA, H_A * V_A)), _const_spec((LANES, H_A * LANES))]
        + _rope_specs(7, TP),
        out_specs=[tok(w) for w, _ in widths],
        out_shape=[jax.ShapeDtypeStruct((N_TOK, w), dt) for w, dt in widths],
        compiler_params=_params(("arbitrary",)),
        name="inproj_ab",
    )(xp, xs, mod, mod, w_in, q_norm, kv_norm, w_uq, w_kn, w_v, place, *tabs)


def _inproj_c_kernel(x_ref, shift_ref, scale_ref, w_in_ref, g_ref, ones_ref, c64_ref, s64_ref, q_ref, k_ref, v_ref):
    t = pl.program_id(0)
    subs = _sub_tiles(x_ref.shape[0])
    h = [(x_ref[s, :] * (1.0 + scale_ref[...]) + shift_ref[...]).astype(BF16) for s in subs]
    z = [jnp.dot(h_s, w_in_ref[...], preferred_element_type=F32) for h_s in h]
    for s, z_s in zip(subs, z):
        v_ref[s, :] = z_s[:, C_NORM_COLS:]
    qk = [z_s[:, :C_NORM_COLS] for z_s in z]
    sq = [qk_s * qk_s for qk_s in qk]
    sq_hi = [sq_s.astype(BF16) for sq_s in sq]
    sq_lo = [(sq_s - hi_s.astype(F32)).astype(BF16) for sq_s, hi_s in zip(sq, sq_hi)]
    ss = []
    for hi_s, lo_s in zip(sq_hi, sq_lo):
        blocks = []
        for j in range(C_NORM_COLS // NORM_BLOCK):
            sl = slice(j * NORM_BLOCK, (j + 1) * NORM_BLOCK)
            blocks.append(jnp.dot(hi_s[:, sl], ones_ref[...], preferred_element_type=F32)
                          + jnp.dot(lo_s[:, sl], ones_ref[...], preferred_element_type=F32))
        ss.append(jnp.concatenate(blocks, axis=1))
    qk = [qk_s * lax.rsqrt(ss_s * (1.0 / HD) + EPS) * g_ref[...] for qk_s, ss_s in zip(qk, ss)]
    p_tiles = N_PROMPT // x_ref.shape[0]

    @pl.when(t < p_tiles)
    def _():
        for s, qk_s in zip(subs, qk):
            q_ref[s, :] = qk_s[:, :H_C * HD].astype(BF16)
            k_ref[s, :] = qk_s[:, H_C * HD:]

    @pl.when(t >= p_tiles)
    def _():
        for s, qk_s in zip(subs, qk):
            q_ref[s, :] = _rope(qk_s[:, :H_C * HD], c64_ref[s, :], s64_ref[s, :], HD // 2).astype(BF16)
            k_ref[s, :] = _rope(qk_s[:, H_C * HD:], c64_ref[s, :], s64_ref[s, :], HD // 2)


def _inproj_c(x, mod, w_in, g_full, tabs):
    def tok(w):
        return pl.BlockSpec((TP, w), lambda t: (t, 0))
    widths = [(H_C * HD, BF16), (KVH_C * HD, F32), (KVH_C * HD, F32)]
    head = np.arange(NORM_BLOCK) // HD
    ones = jnp.asarray(head[:, None] == head[None, :], BF16)
    return pl.pallas_call(
        _inproj_c_kernel,
        grid=(N_TOK // TP,),
        in_specs=[tok(D_MODEL), _mod_spec(1, 0, TP), _mod_spec(1, 1, TP),
                  _const_spec((D_MODEL, C_NORM_COLS + KVH_C * HD)),
                  _const_spec((1, C_NORM_COLS)), _const_spec((NORM_BLOCK, NORM_BLOCK))] + _rope_specs(2, TP),
        out_specs=[tok(w) for w, _ in widths],
        out_shape=[jax.ShapeDtypeStruct((N_TOK, w), dt) for w, dt in widths],
        compiler_params=_params(("arbitrary",)),
        name="inproj_c",
    )(x, mod, mod, w_in, g_full, ones, *tabs[:2])


def _qk(q, k):
    return lax.dot_general(q, k, (((1,), (1,)), ((), ())), preferred_element_type=F32)


def _softmax_pv(scores, values, scale, sink=None):
    c = scale * LOG2E
    m = scores[0].max(axis=-1, keepdims=True)
    for s in scores[1:]:
        m = jnp.maximum(m, s.max(axis=-1, keepdims=True))
    m = m * c
    if sink is not None:
        sink = sink * LOG2E
        m = jnp.maximum(m, sink)
    den = None
    out = None
    for s, v in zip(scores, values):
        p = jnp.exp2(s * c - m)
        d = p.sum(axis=-1, keepdims=True)
        o = jnp.dot(p.astype(BF16), v, preferred_element_type=F32)
        den = d if den is None else den + d
        out = o if out is None else out + o
    if sink is not None:
        den = den + jnp.exp2(sink - m)
    return out * (1.0 / den)


def _mla_heads(qa_ref, ka_refs, va_refs, o_ref):
    scale = (NOPE_A + ROPE_A) ** -0.5
    for h0 in range(0, H_A, 2):
        pair = []
        for h in (h0, h0 + 1):
            q = qa_ref[:, h * LANES:(h + 1) * LANES]
            scores = [_qk(q, ka_ref[:, h * LANES:(h + 1) * LANES]) for ka_ref in ka_refs]
            values = [va_ref[:, h * V_A:(h + 1) * V_A] for va_ref in va_refs]
            pair.append(_softmax_pv(scores, values, scale))
        o_ref[:, h0 * V_A:(h0 + 2) * V_A] = jnp.concatenate(pair, axis=1).astype(o_ref.dtype)


def _gqa_group(q_ref, col0, n_q, ks, vs, mask=None, sink=None):
    rows = q_ref.shape[0]
    q = jnp.concatenate([q_ref[:, col0 + g * HD:col0 + (g + 1) * HD] for g in range(n_q)], axis=0)
    scores = [_qk(q, k) for k in ks]
    if mask is not None:
        scores[0] = jnp.where(mask, scores[0], NEG_INF)
    o = _softmax_pv(scores, vs, HD ** -0.5, sink)
    return jnp.concatenate([o[g * rows:(g + 1) * rows] for g in range(n_q)], axis=1)


def _stacked_sink(sink_ref, h0, n_q, rows):
    head = lax.broadcasted_iota(jnp.int32, (n_q * rows, 1), 0) // rows
    col = jnp.full((n_q * rows, 1), sink_ref[h0], F32)
    for g in range(1, n_q):
        col = jnp.where(head == g, sink_ref[h0 + g], col)
    return col


def _attn_ab_prompt_kernel(sink_ref, qa_ref, ka_ref, va_ref, qb_ref, kb_ref, vb_ref, o_ref):
    _mla_heads(qa_ref, [ka_ref], [va_ref], o_ref)
    kb = kb_ref[...].astype(BF16)
    vb = vb_ref[...].astype(BF16)
    gq = H_B // KVH_B
    for kh in range(KVH_B):
        sl = slice(kh * HD, (kh + 1) * HD)
        o = _gqa_group(qb_ref, kh * gq * HD, gq, [kb[:, sl]], [vb[:, sl]],
                       sink=_stacked_sink(sink_ref, kh * gq, gq, SEQ))
        col = H_A * V_A + kh * gq * HD
        o_ref[:, col:col + gq * HD] = o.astype(o_ref.dtype)


def _attn_ab_prompt(sink, qa, ka, va, qb, kb, vb):
    def blk(w):
        return pl.BlockSpec((SEQ, w), lambda b: (b, 0))
    return pl.pallas_call(
        _attn_ab_prompt_kernel,
        grid=(BATCH,),
        in_specs=[pl.BlockSpec(memory_space=pltpu.SMEM), blk(H_A * LANES), blk(H_A * LANES), blk(H_A * V_A),
                  blk(H_B * HD), blk(KVH_B * HD), blk(KVH_B * HD)],
        out_specs=blk(D_MODEL),
        out_shape=jax.ShapeDtypeStruct((N_PROMPT, H_A * V_A + H_B * HD), BF16),
        compiler_params=_params(("arbitrary",)),
        name="attn_ab_prompt",
    )(sink, qa, ka, va, qb, kb, vb)


def _attn_ab_sample_kernel(sink_ref, qa_ref, ka_ref, va_ref, qb_ref, kb_ref, vb_ref,
                           ckv_ctx_ref, kpe_ctx_ref, kb_ctx_ref, vb_ctx_ref, w_kn_ref, w_v_ref, place_ref,
                           o_ref, ka_ctx_ref, va_ctx_ref):
    i = pl.program_id(1)

    @pl.when(i == 0)
    def _():
        ckv16 = ckv_ctx_ref[...].astype(BF16)
        k_nope = jnp.dot(ckv16, w_kn_ref[...], preferred_element_type=F32)
        k_rope = jnp.dot(kpe_ctx_ref[...].astype(BF16), place_ref[...], preferred_element_type=F32)
        ka_ctx_ref[...] = (k_nope + k_rope).astype(BF16)
        va_ctx_ref[...] = jnp.dot(ckv16, w_v_ref[...], preferred_element_type=F32).astype(BF16)

    _mla_heads(qa_ref, [ka_ref, ka_ctx_ref], [va_ref, va_ctx_ref], o_ref)

    gq = H_B // KVH_B
    n_win = BQ_S + 2 * WINDOW
    start = pl.multiple_of(jnp.clip(i * BQ_S - WINDOW, 0, DEC_SEQ - n_win), WINDOW)
    row = lax.broadcasted_iota(jnp.int32, (gq * BQ_S, n_win), 0)
    qpos = i * BQ_S + (row & (BQ_S - 1))
    kpos = start + lax.broadcasted_iota(jnp.int32, (gq * BQ_S, n_win), 1)
    in_band = jnp.abs(qpos - kpos) <= WINDOW
    kwin = kb_ref[pl.ds(start, n_win), :].astype(BF16)
    vwin = vb_ref[pl.ds(start, n_win), :].astype(BF16)
    kctx = kb_ctx_ref[...].astype(BF16)
    vctx = vb_ctx_ref[...].astype(BF16)
    for kh in range(KVH_B):
        sl = slice(kh * HD, (kh + 1) * HD)
        o = _gqa_group(qb_ref, kh * gq * HD, gq, [kwin[:, sl], kctx[:, sl]], [vwin[:, sl], vctx[:, sl]],
                       mask=in_band, sink=_stacked_sink(sink_ref, kh * gq, gq, BQ_S))
        col = H_A * V_A + kh * gq * HD
        o_ref[:, col:col + gq * HD] = o.astype(o_ref.dtype)


def _attn_ab_sample(sink, qa, ka, va, qb, kb, vb, ckv_ctx, kpe_ctx, kb_ctx, vb_ctx, w_kn, w_v, place):
    qoff = N_PROMPT // BQ_S
    nq = DEC_SEQ // BQ_S
    boff = N_PROMPT // DEC_SEQ

    def qblk(w):
        return pl.BlockSpec((BQ_S, w), lambda b, i: (qoff + b * nq + i, 0))

    def bblk(w):
        return pl.BlockSpec((DEC_SEQ, w), lambda b, i: (boff + b, 0))

    def cblk(w):
        return pl.BlockSpec((PAST_LEN, w), lambda b, i: (b, 0))

    return pl.pallas_call(
        _attn_ab_sample_kernel,
        grid=(DEC_BATCH, nq),
        in_specs=[pl.BlockSpec(memory_space=pltpu.SMEM), qblk(H_A * LANES), bblk(H_A * LANES), bblk(H_A * V_A),
                  qblk(H_B * HD), bblk(KVH_B * HD), bblk(KVH_B * HD),
                  cblk(KV_LORA), cblk(LANES), cblk(KVH_B * HD), cblk(KVH_B * HD),
                  _const_spec((KV_LORA, H_A * LANES)), _const_spec((KV_LORA, H_A * V_A)),
                  _const_spec((LANES, H_A * LANES))],
        out_specs=pl.BlockSpec((BQ_S, D_MODEL), lambda b, i: (b * nq + i, 0)),
        out_shape=jax.ShapeDtypeStruct((N_SAMPLE, H_A * V_A + H_B * HD), BF16),
        scratch_shapes=[pltpu.VMEM((PAST_LEN, H_A * LANES), BF16), pltpu.VMEM((PAST_LEN, H_A * V_A), BF16)],
        compiler_params=_params(("arbitrary", "arbitrary")),
        name="attn_ab_sample",
    )(sink, qa, ka, va, qb, kb, vb, ckv_ctx, kpe_ctx, kb_ctx, vb_ctx, w_kn, w_v, place)


def _gqa_heads(q_ref, k_list, v_list, o_ref):
    gq = H_C // KVH_C
    for kh in range(KVH_C):
        sl = slice(kh * HD, (kh + 1) * HD)
        o = _gqa_group(q_ref, kh * gq * HD, gq, [k[:, sl] for k in k_list], [v[:, sl] for v in v_list])
        o_ref[:, kh * gq * HD:(kh + 1) * gq * HD] = o.astype(o_ref.dtype)


def _attn_c_prompt_kernel(q_ref, k_ref, v_ref, o_ref):
    _gqa_heads(q_ref, [k_ref[...].astype(BF16)], [v_ref[...].astype(BF16)], o_ref)


def _attn_c_prompt(q, k, v):
    def blk(w):
        return pl.BlockSpec((SEQ, w), lambda b: (b, 0))
    return pl.pallas_call(
        _attn_c_prompt_kernel,
        grid=(BATCH,),
        in_specs=[blk(H_C * HD), blk(KVH_C * HD), blk(KVH_C * HD)],
        out_specs=blk(H_C * HD),
        out_shape=jax.ShapeDtypeStruct((N_PROMPT, H_C * HD), BF16),
        compiler_params=_params(("arbitrary",)),
        name="attn_c_prompt",
    )(q, k, v)


def _attn_c_sample_kernel(q_ref, k_ref, v_ref, kc_ref, vc_ref, o_ref):
    _gqa_heads(q_ref, [k_ref[...].astype(BF16), kc_ref[...].astype(BF16)],
               [v_ref[...].astype(BF16), vc_ref[...].astype(BF16)], o_ref)


def _attn_c_sample(q, k, v, k_ctx, v_ctx):
    qoff = N_PROMPT // BQ_S
    nq = DEC_SEQ // BQ_S
    boff = N_PROMPT // DEC_SEQ
    w = KVH_C * HD
    return pl.pallas_call(
        _attn_c_sample_kernel,
        grid=(DEC_BATCH, nq),
        in_specs=[pl.BlockSpec((BQ_S, H_C * HD), lambda b, i: (qoff + b * nq + i, 0)),
                  pl.BlockSpec((DEC_SEQ, w), lambda b, i: (boff + b, 0)),
                  pl.BlockSpec((DEC_SEQ, w), lambda b, i: (boff + b, 0)),
                  pl.BlockSpec((PAST_LEN, w), lambda b, i: (b, 0)),
                  pl.BlockSpec((PAST_LEN, w), lambda b, i: (b, 0))],
        out_specs=pl.BlockSpec((BQ_S, H_C * HD), lambda b, i: (b * nq + i, 0)),
        out_shape=jax.ShapeDtypeStruct((N_SAMPLE, H_C * HD), BF16),
        compiler_params=_params(("arbitrary", "arbitrary")),
        name="attn_c_sample",
    )(q, k, v, k_ctx, v_ctx)


def _layer_norm(y, g, b):
    mu = jnp.mean(y, axis=-1, keepdims=True)
    yc = y - mu
    var = jnp.mean(yc * yc, axis=-1, keepdims=True)
    return yc * lax.rsqrt(var + EPS) * g + b


def _route(sel, aff):
    def row(a, j):
        return a[j:j + 1, :]

    scores = []
    for g in range(N_GROUPS):
        a0, a1, a2, a3 = (row(sel, 4 * g + j) for j in range(4))
        hi01, lo01 = jnp.maximum(a0, a1), jnp.minimum(a0, a1)
        hi23, lo23 = jnp.maximum(a2, a3), jnp.minimum(a2, a3)
        top1 = jnp.maximum(hi01, hi23)
        top2 = jnp.maximum(jnp.minimum(hi01, hi23), jnp.maximum(lo01, lo23))
        scores.append(top1 + top2)
    best = scores[0]
    gi = jnp.zeros(best.shape, jnp.int32)
    for g in range(1, N_GROUPS):
        better = scores[g] > best
        gi = jnp.where(better, g, gi)
        best = jnp.where(better, scores[g], best)

    def in_group(a, j):
        out = row(a, j)
        for g in range(1, N_GROUPS):
            out = jnp.where(gi == g, row(a, 4 * g + j), out)
        return out

    v = [in_group(sel, j) for j in range(4)]
    a = [in_group(aff, j) for j in range(4)]
    chosen = []
    for j in range(4):
        rank = jnp.zeros(best.shape, jnp.int32)
        for k in range(4):
            if k == j:
                continue
            ahead = (v[k] >= v[j]) if k < j else (v[k] > v[j])
            rank = rank + ahead.astype(jnp.int32)
        chosen.append(rank < 2)
    total = sum(jnp.where(chosen[j], a[j], 0.0) for j in range(4))
    w = [jnp.where(chosen[j], a[j], 0.0) / total for j in range(4)]
    pair = jnp.zeros(best.shape, jnp.int32)
    for p, (ja, jb) in enumerate(zip(SLOT_A_LOCAL, SLOT_B_LOCAL)):
        pair = jnp.where(chosen[ja] & chosen[jb], p, pair)
    gate_a = jnp.where(pair == 0, w[0], jnp.where(pair <= 2, w[2], w[3]))
    gate_b = jnp.where((pair == 0) | (pair == 1) | (pair == 4), w[1], jnp.where(pair == 5, w[2], w[0]))
    return gi * PAIRS_PER_GROUP + pair, gate_a, gate_b


def _post_attn_kernel(split_x, *refs):
    t = pl.program_id(0)
    op_ref, os_ref = refs[:2]
    if split_x:
        x = _pick(t, refs[2], refs[3])
        refs = refs[4:]
    else:
        x = refs[2][...]
        refs = refs[3:]
    (w_out_ref, gate_ref, shift_ref, scale_ref, lng_ref, lnb_ref, rw_ref, rb_ref, tri_ref,
     x1_ref, row_ref, route_ref, counts_ref, carry_ref) = refs
    o = _pick(t, op_ref, os_ref)
    subs = _sub_tiles(o.shape[0])

    a = [jnp.dot(o[s], w_out_ref[...], preferred_element_type=F32) for s in subs]
    x1 = [_layer_norm(ALPHA * x[s] + gate_ref[...] * a_s, lng_ref[...], lnb_ref[...]) for s, a_s in zip(subs, a)]
    h2 = [x1_s * (1.0 + scale_ref[...]) + shift_ref[...] for x1_s in x1]
    for s, x1_s, h2_s in zip(subs, x1, h2):
        x1_ref[s, :] = x1_s
        row_ref[s, :D_MODEL] = h2_s
    logits = [lax.dot_general(rw_ref[...], h2_s, (((1,), (1,)), ((), ())), preferred_element_type=F32,
                              precision=lax.Precision.HIGHEST) for h2_s in h2]
    aff = [1.0 / (1.0 + jnp.exp(-l)) for l in logits]
    routed = [_route(aff_s + rb_ref[...], aff_s) for aff_s in aff]

    lane = lax.broadcasted_iota(jnp.int32, (LANES, TM), 0)
    for s, (_, gate_a, gate_b) in zip(subs, routed):
        meta_t = jnp.where(lane == 0, gate_a, jnp.where(lane == 1, gate_b, 0.0))
        row_ref[s, D_MODEL:] = meta_t.T

    @pl.when(t == 0)
    def _():
        carry_ref[...] = jnp.zeros_like(carry_ref)

    onehots = [lax.broadcasted_iota(jnp.int32, (BUCKET_ROWS, TM), 0) == bucket for bucket, _, _ in routed]
    prefixes = [jnp.dot(oh.astype(BF16), tri_ref[...], preferred_element_type=F32) for oh in onehots]
    carry = carry_ref[...]
    route_ref[...] = jnp.zeros(route_ref.shape, F32)
    for s, (bucket, _, _), onehot, prefix in zip(subs, routed, onehots, prefixes):
        rank = jnp.sum(jnp.where(onehot, prefix + carry[:, 0:1], 0.0), axis=0, keepdims=True)
        carry = carry + jnp.sum(onehot.astype(F32), axis=1, keepdims=True)
        route_ref[0:1, s] = bucket.astype(F32)
        route_ref[1:2, s] = rank
    carry_ref[...] = carry
    counts_ref[...] = carry


def _post_attn(layer, o_p, o_s, x_list, mod, w_out, ln_g, ln_b, rw_t, rb):
    split_x = len(x_list) == 2
    tok = pl.BlockSpec((TP, D_MODEL), lambda t: (t, 0))
    tri = jnp.asarray(np.arange(TM)[:, None] < np.arange(TM)[None, :], BF16)
    return pl.pallas_call(
        functools.partial(_post_attn_kernel, split_x),
        grid=(N_TOK // TP,),
        in_specs=_split_specs(TP) + (_split_specs(TP) if split_x else [tok]) + [
            _const_spec((D_MODEL, D_MODEL)), _mod_spec(layer, 2, TP), _mod_spec(layer, 3, TP),
            _mod_spec(layer, 4, TP),
            _const_spec((1, D_MODEL)), _const_spec((1, D_MODEL)), _const_spec((N_EXPERTS, D_MODEL)),
            _const_spec((N_EXPERTS, 1)), _const_spec((TM, TM))],
        out_specs=[tok, pl.BlockSpec((TP, ROW_W), lambda t: (t, 0)), pl.BlockSpec((8, TP), lambda t: (0, t)),
                   _const_spec((BUCKET_ROWS, LANES))],
        out_shape=[jax.ShapeDtypeStruct((N_TOK, D_MODEL), F32), jax.ShapeDtypeStruct((N_TOK, ROW_W), F32),
                   jax.ShapeDtypeStruct((8, N_TOK), F32), jax.ShapeDtypeStruct((BUCKET_ROWS, LANES), F32)],
        scratch_shapes=[pltpu.VMEM((BUCKET_ROWS, LANES), F32)],
        compiler_params=_params(("arbitrary",)),
        name="post_attn",
    )(o_p, o_s, *x_list, w_out, mod, mod, mod, ln_g, ln_b, rw_t, rb, tri)


def _moe_kernel(ta_ref, tb_ref, nt_ref, x_ref, wga_ref, wua_ref, wda_ref, wgb_ref, wub_ref, wdb_ref, o_ref,
                wga_s, wua_s, wda_s, wgb_s, wub_s, wdb_s):
    i = pl.program_id(0)
    prev = jnp.maximum(i - 1, 0)
    slots = ((ta_ref, (wga_ref, wua_ref, wda_ref), (wga_s, wua_s, wda_s)),
             (tb_ref, (wgb_ref, wub_ref, wdb_ref), (wgb_s, wub_s, wdb_s)))

    for t_ref, w_refs, w_scr in slots:
        @pl.when((i == 0) | (t_ref[i] != t_ref[prev]))
        def _(w_refs=w_refs, w_scr=w_scr):
            for w_ref, s_ref in zip(w_refs, w_scr):
                s_ref[...] = w_ref[...].astype(BF16)

    @pl.when(i < nt_ref[0])
    def _():
        x = x_ref[:, :D_MODEL].astype(BF16)
        acc = None
        for slot, (_, _, (wg_s, wu_s, wd_s)) in enumerate(slots):
            gt = jnp.dot(x, wg_s[...], preferred_element_type=F32)
            up = jnp.dot(x, wu_s[...], preferred_element_type=F32)
            gate = x_ref[:, D_MODEL + slot:D_MODEL + slot + 1]
            hid = (gt / (1.0 + jnp.exp(-gt))) * up * gate
            y = jnp.dot(hid.astype(BF16), wd_s[...], preferred_element_type=F32)
            acc = y if acc is None else acc + y
        o_ref[...] = acc

    @pl.when(i >= nt_ref[0])
    def _():
        o_ref[...] = jnp.zeros_like(o_ref)


def _moe(layer, tile_a, tile_b, n_tiles, rows, w_gate, w_up, w_down):
    def row_idx(i, ta, tb, nt):
        return (jnp.minimum(i, nt[0] - 1), 0)

    def wspec(shape, which):
        if which == 0:
            return pl.BlockSpec((None, None) + shape, lambda i, ta, tb, nt: (layer, ta[i], 0, 0))
        return pl.BlockSpec((None, None) + shape, lambda i, ta, tb, nt: (layer, tb[i], 0, 0))

    up_shape, down_shape = (D_MODEL, D_EXPERT), (D_EXPERT, D_MODEL)
    grid_spec = pltpu.PrefetchScalarGridSpec(
        num_scalar_prefetch=3,
        grid=(MOE_TILES,),
        in_specs=[pl.BlockSpec((TME, ROW_W), row_idx),
                  wspec(up_shape, 0), wspec(up_shape, 0), wspec(down_shape, 0),
                  wspec(up_shape, 1), wspec(up_shape, 1), wspec(down_shape, 1)],
        out_specs=pl.BlockSpec((TME, D_MODEL), lambda i, ta, tb, nt: (i, 0)),
        scratch_shapes=[pltpu.VMEM(s, BF16) for s in (up_shape, up_shape, down_shape) * 2],
    )
    return pl.pallas_call(
        _moe_kernel,
        grid_spec=grid_spec,
        out_shape=jax.ShapeDtypeStruct((MOE_ROWS, D_MODEL), F32),
        compiler_params=_params(("arbitrary",)),
        name="moe_experts",
    )(tile_a, tile_b, n_tiles, rows, w_gate, w_up, w_down, w_gate, w_up, w_down)


def _row_copy(src_ref, src_row, dst_ref, dst_row, sem):
    return pltpu.make_async_copy(src_ref.at[pl.ds(src_row, 1), :], dst_ref.at[pl.ds(dst_row, 1), :], sem)


def _dispatch_kernel(pos_ref, pend_ref, cnt_ref, nt_ref, src_ref, out_ref, buf, zero_ref, in_sem, sem, zsem):
    t = pl.program_id(0)

    def fetch(tile):
        return pltpu.make_async_copy(src_ref.at[pl.ds(pl.multiple_of(tile * TM, TM), TM), :],
                                     buf.at[tile % STAGE_SLOTS], in_sem.at[tile % STAGE_SLOTS])

    @pl.when(t == 0)
    def _():
        fetch(0).start()
        zero_ref[...] = jnp.zeros_like(zero_ref)

        def zero_tile(row0):
            return pltpu.make_async_copy(zero_ref, out_ref.at[pl.ds(pl.multiple_of(row0, TME), TME), :], zsem)

        for b in range(N_BUCKETS):
            @pl.when(cnt_ref[b] > 0)
            def _(b=b):
                zero_tile(pend_ref[b] - TME).start()

        def start_unused(i, carry):
            zero_tile(i * TME).start()
            return carry
        lax.fori_loop(nt_ref[0], MOE_TILES, start_unused, 0)

        for b in range(N_BUCKETS):
            @pl.when(cnt_ref[b] > 0)
            def _(b=b):
                zero_tile(pend_ref[b] - TME).wait()

        def wait_unused(i, carry):
            zero_tile(i * TME).wait()
            return carry
        lax.fori_loop(nt_ref[0], MOE_TILES, wait_unused, 0)

    @pl.when(t + 1 < N_TILES)
    def _():
        fetch(t + 1).start()

    fetch(t).wait()
    tile_ref = buf.at[t % STAGE_SLOTS]

    def issue(r, carry):
        _row_copy(tile_ref, r, out_ref, pos_ref[t * TM + r], sem.at[t % 2]).start()
        return carry
    lax.fori_loop(0, TM, issue, 0, unroll=8)

    def drain(tile):
        pltpu.make_async_copy(buf.at[0], out_ref.at[pl.ds(0, TM), :], sem.at[tile % 2]).wait()

    @pl.when(t > 0)
    def _():
        drain(t - 1)

    @pl.when(t == N_TILES - 1)
    def _():
        drain(t)


def _dispatch(pos, pend, counts, n_tiles, rows):
    return pl.pallas_call(
        _dispatch_kernel,
        grid_spec=pltpu.PrefetchScalarGridSpec(
            num_scalar_prefetch=4, grid=(N_TILES,),
            in_specs=[pl.BlockSpec(memory_space=pl.ANY)],
            out_specs=pl.BlockSpec(memory_space=pl.ANY),
            scratch_shapes=[pltpu.VMEM((STAGE_SLOTS, TM, ROW_W), F32), pltpu.VMEM((TME, ROW_W), F32),
                            pltpu.SemaphoreType.DMA((STAGE_SLOTS,)), pltpu.SemaphoreType.DMA((2,)),
                            pltpu.SemaphoreType.DMA]),
        out_shape=jax.ShapeDtypeStruct((MOE_ROWS, ROW_W), F32),
        compiler_params=_params(("arbitrary",)),
        name="dispatch_rows",
    )(pos, pend, counts, n_tiles, rows)


def _plan(route, counts):
    bucket = route[0].astype(jnp.int32)
    rank = route[1].astype(jnp.int32)
    counts = counts[:N_BUCKETS, 0].astype(jnp.int32)
    padded = ((counts + TME - 1) // TME) * TME
    pend = jnp.cumsum(padded)
    pstart = pend - padded
    ids = jnp.arange(N_BUCKETS, dtype=jnp.int32)
    pos = rank + jnp.sum(jnp.where(bucket[None, :] == ids[:, None], pstart[:, None], 0), axis=0)
    n_tiles = pend[-1] // TME
    tile_start = jnp.minimum(jnp.arange(MOE_TILES, dtype=jnp.int32), n_tiles - 1) * TME
    tile_bucket = jnp.minimum(jnp.sum(tile_start[:, None] >= pend[None, :], axis=1), N_BUCKETS - 1)
    group, pair = tile_bucket // PAIRS_PER_GROUP, tile_bucket % PAIRS_PER_GROUP
    slot_a = jnp.asarray(SLOT_A_LOCAL, jnp.int32)
    slot_b = jnp.asarray(SLOT_B_LOCAL, jnp.int32)
    pair_hot = pair[:, None] == jnp.arange(PAIRS_PER_GROUP, dtype=jnp.int32)[None, :]
    tile_a = group * EXPERTS_PER_GROUP + jnp.sum(jnp.where(pair_hot, slot_a[None, :], 0), axis=1)
    tile_b = group * EXPERTS_PER_GROUP + jnp.sum(jnp.where(pair_hot, slot_b[None, :], 0), axis=1)
    return pos, pend, counts, tile_a.astype(jnp.int32), tile_b.astype(jnp.int32), n_tiles.reshape(1)


def _post_moe_kernel(split_out, pos_ref, f_ref, x1_ref, gate_ref, lng_ref, lnb_ref, *refs):
    out_refs, (fbuf, sem) = refs[:-2], refs[-2:]
    t = pl.program_id(0)

    def wait_tile(slot):
        pltpu.make_async_copy(f_ref.at[pl.ds(0, TM), :], fbuf.at[slot], sem.at[slot]).wait()

    @pl.when(t == 0)
    def _():
        def issue(r, carry):
            _row_copy(f_ref, pos_ref[r], fbuf.at[r // TM], r % TM, sem.at[r // TM]).start()
            return carry
        lax.fori_loop(0, (GATHER_SLOTS - 1) * TM, issue, 0, unroll=8)

    slot = t % GATHER_SLOTS
    wait_tile(slot)
    ahead = jnp.minimum(t + GATHER_SLOTS - 1, N_TILES - 1)
    aslot = (t + GATHER_SLOTS - 1) % GATHER_SLOTS
    for r in range(TM):
        _row_copy(f_ref, pos_ref[ahead * TM + r], fbuf.at[aslot], r, sem.at[aslot]).start()
    y = _layer_norm(ALPHA * x1_ref[...] + gate_ref[...] * fbuf[slot], lng_ref[...], lnb_ref[...])

    @pl.when(t == N_TILES - 1)
    def _():
        for k in range(1, GATHER_SLOTS):
            wait_tile((t + k) % GATHER_SLOTS)
    if split_out:
        @pl.when(t < PROMPT_TILES)
        def _():
            out_refs[0][...] = y

        @pl.when(t >= PROMPT_TILES)
        def _():
            out_refs[1][...] = y
    else:
        out_refs[0][...] = y


def _post_moe(layer, pos, f_sorted, x1, mod, ln_g, ln_b, split_out):
    tok = pl.BlockSpec((TM, D_MODEL), lambda t, p: (t, 0))
    if split_out:
        out_specs = [pl.BlockSpec((TM, D_MODEL), lambda t, p: (jnp.minimum(t, PROMPT_TILES - 1), 0)),
                     pl.BlockSpec((TM, D_MODEL), lambda t, p: (jnp.maximum(t - PROMPT_TILES, 0), 0))]
        out_shape = [jax.ShapeDtypeStruct((N_PROMPT, D_MODEL), F32), jax.ShapeDtypeStruct((N_SAMPLE, D_MODEL), F32)]
    else:
        out_specs = [tok]
        out_shape = [jax.ShapeDtypeStruct((N_TOK, D_MODEL), F32)]
    mod_spec = pl.BlockSpec((None, None, None, 1, D_MODEL), lambda t, p: (layer, _mod_row(t), 5, 0, 0))
    return pl.pallas_call(
        functools.partial(_post_moe_kernel, split_out),
        grid_spec=pltpu.PrefetchScalarGridSpec(
            num_scalar_prefetch=1, grid=(N_TILES,),
            in_specs=[pl.BlockSpec(memory_space=pl.ANY), tok, mod_spec,
                      pl.BlockSpec((1, D_MODEL), lambda t, p: (0, 0)), pl.BlockSpec((1, D_MODEL), lambda t, p: (0, 0))],
            out_specs=out_specs,
            scratch_shapes=[pltpu.VMEM((GATHER_SLOTS, TM, D_MODEL), F32),
                            pltpu.SemaphoreType.DMA((GATHER_SLOTS,))]),
        out_shape=out_shape,
        compiler_params=_params(("arbitrary",)),
        name="post_moe",
    )(pos, f_sorted, x1, mod, ln_g, ln_b)


def _ffn(layer, o_p, o_s, x_list, mod, w_out, ln_g, ln_b, rw_t, rb, w_gate, w_up, w_down, split_out):
    x1, rows, route, counts = _post_attn(layer, o_p, o_s, x_list, mod, w_out, ln_g[layer, 0][None],
                                         ln_b[layer, 0][None], rw_t, rb)
    pos, pend, counts, tile_a, tile_b, n_tiles = _plan(route, counts)
    rows_sorted = _dispatch(pos, pend, counts, n_tiles, rows)
    f_sorted = _moe(layer, tile_a, tile_b, n_tiles, rows_sorted, w_gate, w_up, w_down)
    return _post_moe(layer, pos, f_sorted, x1, mod, ln_g[layer, 1][None], ln_b[layer, 1][None], split_out)


def kernel(x_prompt, x_sample, c, cache_mla_ckv, cache_mla_kpe, cache_swa_k, cache_swa_v, cache_gqa_k, cache_gqa_v, c_ctx, w_mod, b_mod, ln_g, ln_b, w_in_ab, mla_q_norm, mla_w_uq, mla_kv_norm, mla_w_ukv, swa_sink, w_out_ab, w_in_c, gqa_q_norm, gqa_k_norm, w_out_c, router_w, router_bias, exp_w_gate, exp_w_up, exp_w_down):
    xp = x_prompt.reshape(N_PROMPT, D_MODEL)
    xs = x_sample.reshape(N_SAMPLE, D_MODEL)
    cond = jnp.concatenate([c_ctx[None], c, jnp.zeros((MOD_ROWS - 1 - DEC_BATCH, D_MODEL), F32)], axis=0)
    mod = _modulation(cond, w_mod, b_mod).reshape(DEPTH, MOD_ROWS, 6, 1, D_MODEL)
    tabs = _rope_tables()
    rw_t = router_w.T
    rb = router_bias.reshape(N_EXPERTS, 1)

    w = w_in_ab[0]
    pad = jnp.zeros((D_MODEL, AB_COLS - w.shape[1]), F32)
    w_in = jnp.concatenate([w[:, :384], w[:, 416:], w[:, 384:416], pad], axis=1).astype(BF16)
    w_uq, w_kn, w_v, place = _mla_weights(mla_w_uq[0], mla_w_ukv[0])
    qa, ka, va, qb, ckv, kb, vb, kpe = _inproj_ab(xp, xs, mod, w_in, mla_q_norm[0][None], mla_kv_norm[0][None],
                                                  w_uq, w_kn, w_v, place, tabs)
    sink = swa_sink[0]
    o_p = _attn_ab_prompt(sink, qa, ka, va, qb, kb, vb)
    kpe_ctx = jnp.pad(cache_mla_kpe[:, 0].reshape(-1, ROPE_A), ((0, 0), (0, LANES - ROPE_A)))
    o_s = _attn_ab_sample(sink, qa, ka, va, qb, kb, vb,
                          cache_mla_ckv[:, 0].reshape(-1, KV_LORA), kpe_ctx,
                          cache_swa_k[:, 0].reshape(-1, KVH_B * HD), cache_swa_v[:, 0].reshape(-1, KVH_B * HD),
                          w_kn, w_v, place)
    (x2,) = _ffn(0, o_p, o_s, [xp, xs], mod, w_out_ab[0].astype(BF16), ln_g, ln_b, rw_t, rb,
                 exp_w_gate, exp_w_up, exp_w_down, split_out=False)

    g_full = jnp.concatenate([jnp.tile(gqa_q_norm[0], H_C), jnp.tile(gqa_k_norm[0], KVH_C)])[None]
    qc, kc, vc = _inproj_c(x2, mod, w_in_c[0].astype(BF16), g_full, tabs)
    oc_p = _attn_c_prompt(qc, kc, vc)
    oc_s = _attn_c_sample(qc, kc, vc, cache_gqa_k[:, 0].reshape(-1, KVH_C * HD),
                          cache_gqa_v[:, 0].reshape(-1, KVH_C * HD))
    y_p, y_s = _ffn(1, oc_p, oc_s, [x2], mod, w_out_c[0].astype(BF16), ln_g, ln_b, rw_t, rb,
                    exp_w_gate, exp_w_up, exp_w_down, split_out=True)

    y_prompt = y_p.reshape(BATCH, SEQ, D_MODEL)
    y_sample = y_s.reshape(DEC_BATCH, DEC_SEQ, D_MODEL)
    new_ckv = ckv[:N_PROMPT].reshape(BATCH, 1, SEQ, KV_LORA)
    new_kpe = kpe[:N_PROMPT, :ROPE_A].reshape(BATCH, 1, SEQ, ROPE_A)
    new_swk = kb[:N_PROMPT].reshape(BATCH, 1, SEQ, KVH_B, HD)
    new_swv = vb[:N_PROMPT].reshape(BATCH, 1, SEQ, KVH_B, HD)
    new_gk = kc[:N_PROMPT].reshape(BATCH, 1, SEQ, KVH_C, HD)
    new_gv = vc[:N_PROMPT].reshape(BATCH, 1, SEQ, KVH_C, HD)
    return y_prompt, y_sample, new_ckv, new_kpe, new_swk, new_swv, new_gk, new_gv
```

```python
import functools

import numpy as np
import jax
import jax.numpy as jnp
from jax import lax
from jax.experimental import pallas as pl
from jax.experimental.pallas import tpu as pltpu

D_MODEL = 1024
BATCH = 32
SEQ = 256
DEPTH = 2
DEC_BATCH = 4
DEC_SEQ = 1024
PAST_LEN = 256
GRID_W = 64
ROPE_THETA = 10000.0
WINDOW = 128
HD = 64
H_A = 8
NOPE_A = 64
ROPE_A = 32
V_A = 64
Q_LORA = 256
KV_LORA = 128
H_B = 8
KVH_B = 2
H_C = 16
KVH_C = 4
N_EXPERTS = 16
N_GROUPS = 4
EXPERTS_PER_GROUP = 4
D_EXPERT = 512
ALPHA = (2 * DEPTH) ** 0.25
NEG_INF = -1e30
EPS = 1e-6

LANES = 128
N_PROMPT = BATCH * SEQ
N_SAMPLE = DEC_BATCH * DEC_SEQ
N_TOK = N_PROMPT + N_SAMPLE
TM = 256
TP = 512
N_TILES = N_TOK // TM
PROMPT_TILES = N_PROMPT // TM
BQ_S = 256
LOG2E = 1.4426950408889634
MOD_ROWS = 8

PAIRS_PER_GROUP = 6
N_BUCKETS = N_GROUPS * PAIRS_PER_GROUP
SLOT_A_LOCAL = (0, 2, 2, 3, 3, 3)
SLOT_B_LOCAL = (1, 1, 0, 0, 1, 2)
TME = 256
MOE_TILES = -(-(N_TOK + N_BUCKETS * (TME - 1)) // TME)
MOE_ROWS = MOE_TILES * TME

BUCKET_ROWS = 32
ROW_W = D_MODEL + LANES
STAGE_SLOTS = 3
GATHER_SLOTS = 3

AB_COLS = 1280
C_NORM_COLS = H_C * HD + KVH_C * HD
NORM_BLOCK = 256

BF16 = jnp.bfloat16
F32 = jnp.float32
VMEM_LIMIT = 52 * 1024 * 1024


def _mod_row(t, tile=TM):
    p_tiles = N_PROMPT // tile
    return jnp.where(t < p_tiles, 0, 1 + (t - p_tiles) // (DEC_SEQ // tile))


def _mod_spec(layer, chunk, tile=TM):
    return pl.BlockSpec((None, None, None, 1, D_MODEL), lambda t: (layer, _mod_row(t, tile), chunk, 0, 0))


def _const_spec(shape):
    nd = len(shape)
    return pl.BlockSpec(shape, lambda *_: (0,) * nd)


def _split_specs(tile=TM):
    p_tiles = N_PROMPT // tile
    return [
        pl.BlockSpec((tile, D_MODEL), lambda t: (jnp.minimum(t, p_tiles - 1), 0)),
        pl.BlockSpec((tile, D_MODEL), lambda t: (jnp.maximum(t - p_tiles, 0), 0)),
    ]


def _group_spec(tile, width, prompt):
    p_tiles = N_PROMPT // tile
    if prompt:
        return pl.BlockSpec((tile, width), lambda t: (jnp.minimum(t, p_tiles - 1), 0))
    return pl.BlockSpec((tile, width), lambda t: (jnp.maximum(t - p_tiles, 0), 0))


def _pick(t, a_ref, b_ref):
    return jnp.where(t < N_PROMPT // a_ref.shape[0], a_ref[...], b_ref[...])


def _sub_tiles(rows):
    return [slice(i * TM, (i + 1) * TM) for i in range(rows // TM)]


def _params(sem):
    return pltpu.CompilerParams(dimension_semantics=sem, vmem_limit_bytes=VMEM_LIMIT)


def _mod_kernel(cond_ref, w_ref, b_ref, o_ref):
    c = cond_ref[...]
    s = (c / (1.0 + jnp.exp(-c))).astype(BF16)
    o_ref[...] = jnp.dot(s, w_ref[...].astype(BF16), preferred_element_type=F32) + b_ref[...]


def _modulation(cond, w_mod, b_mod):
    tn = 1536
    return pl.pallas_call(
        _mod_kernel,
        grid=(DEPTH, 6 * D_MODEL // tn),
        in_specs=[
            pl.BlockSpec((MOD_ROWS, D_MODEL), lambda l, j: (0, 0)),
            pl.BlockSpec((None, D_MODEL, tn), lambda l, j: (l, 0, j)),
            pl.BlockSpec((None, 1, tn), lambda l, j: (l, 0, j)),
        ],
        out_specs=pl.BlockSpec((None, MOD_ROWS, tn), lambda l, j: (l, 0, j)),
        out_shape=jax.ShapeDtypeStruct((DEPTH, MOD_ROWS, 6 * D_MODEL), F32),
        compiler_params=_params(("arbitrary", "arbitrary")),
        name="modulation",
    )(cond, w_mod, b_mod.reshape(DEPTH, 1, 6 * D_MODEL))


def _grid_cos_sin(half):
    n_freq = half // 2
    rows = DEC_SEQ // GRID_W
    row = jnp.repeat(jnp.arange(rows, dtype=F32), GRID_W)
    col = jnp.tile(jnp.arange(GRID_W, dtype=F32), rows)
    inv = 1.0 / (ROPE_THETA ** (jnp.arange(n_freq, dtype=F32) / n_freq))
    ang = jnp.concatenate([row[:, None] * inv, col[:, None] * inv], axis=-1)
    return jnp.cos(ang), jnp.sin(ang)


def _rope_tables():
    tabs = []
    for half in (HD // 2, ROPE_A // 2):
        cos, sin = _grid_cos_sin(half)
        reps = LANES // (2 * half)
        tabs.append(jnp.tile(jnp.concatenate([cos, cos], axis=-1), (1, reps)))
        tabs.append(jnp.tile(jnp.concatenate([-sin, sin], axis=-1), (1, reps)))
    half = ROPE_A // 2
    one = jnp.ones((DEC_SEQ, NOPE_A), F32)
    zero = jnp.zeros((DEC_SEQ, NOPE_A), F32)
    pad1 = jnp.ones((DEC_SEQ, LANES - NOPE_A - ROPE_A), F32)
    pad0 = jnp.zeros((DEC_SEQ, LANES - NOPE_A - ROPE_A), F32)
    zh = jnp.zeros((DEC_SEQ, half), F32)
    tabs.append(jnp.concatenate([one, cos, cos, pad1], axis=-1))
    tabs.append(jnp.concatenate([zero, -sin, zh, pad0], axis=-1))
    tabs.append(jnp.concatenate([zero, zh, sin, pad0], axis=-1))
    return tuple(tabs)


def _rope(x, c, s, half):
    w = x.shape[1]
    reps = w // LANES
    if reps > 1:
        c = jnp.concatenate([c] * reps, axis=1)
        s = jnp.concatenate([s] * reps, axis=1)
    ahead = pltpu.roll(x, w - half, 1)
    behind = pltpu.roll(x, half, 1)
    lane = lax.broadcasted_iota(jnp.int32, x.shape, 1)
    swapped = jnp.where((lane & (2 * half - 1)) < half, ahead, behind)
    return x * c + swapped * s


def _rope_specs(n, tile):
    p_tiles = N_PROMPT // tile

    def idx(t):
        return (jnp.maximum(t - p_tiles, 0) % (DEC_SEQ // tile), 0)
    return [pl.BlockSpec((tile, LANES), idx)] * n


def _rms(x, g_ref):
    return x * lax.rsqrt(jnp.mean(x * x, axis=-1, keepdims=True) + EPS) * g_ref[...]


def _inproj_ab_kernel(xp_ref, xs_ref, shift_ref, scale_ref, w_in_ref, qn_ref, kvn_ref, w_uq_ref, w_kn_ref, w_v_ref,
                      place_ref, c64_ref, s64_ref, c32_ref, s32_ref, ca_ref, sa_ref, sb_ref,
                      qa_ref, ka_ref, va_ref, qb_ref, ckv_p_ref, kpe_p_ref, kb_p_ref, vb_p_ref, kb_s_ref, vb_s_ref):
    t = pl.program_id(0)
    x = _pick(t, xp_ref, xs_ref)
    subs = _sub_tiles(x.shape[0])
    h = [(x[s] * (1.0 + scale_ref[...]) + shift_ref[...]).astype(BF16) for s in subs]
    z = [jnp.dot(h_s, w_in_ref[...], preferred_element_type=F32) for h_s in h]
    ql = [_rms(z_s[:, :Q_LORA], qn_ref).astype(BF16) for z_s in z]
    qa = [jnp.dot(ql_s, w_uq_ref[...], preferred_element_type=F32) for ql_s in ql]
    ckv = [_rms(z_s[:, Q_LORA:Q_LORA + KV_LORA], kvn_ref) for z_s in z]
    k_nope = [jnp.dot(c_s.astype(BF16), w_kn_ref[...], preferred_element_type=F32) for c_s in ckv]
    va = [jnp.dot(c_s.astype(BF16), w_v_ref[...], preferred_element_type=F32).astype(BF16) for c_s in ckv]
    for s, va_s in zip(subs, va):
        va_ref[s, :] = va_s
    qb = [z_s[:, 384:896] for z_s in z]
    kb = [z_s[:, 896:1024] for z_s in z]
    vb = [z_s[:, 1024:1152] for z_s in z]
    kpe = [z_s[:, 1152:1280] for z_s in z]

    def finish(qa_out, qb_out, kb_out, kpe_out, kb_ref, vb_ref):
        k_rope = [jnp.dot(kpe_s.astype(BF16), place_ref[...], preferred_element_type=F32) for kpe_s in kpe_out]
        for i, s in enumerate(subs):
            qa_ref[s, :] = qa_out[i].astype(BF16)
            qb_ref[s, :] = qb_out[i].astype(BF16)
            kb_ref[s, :] = kb_out[i]
            vb_ref[s, :] = vb[i]
            ka_ref[s, :] = (k_nope[i] + k_rope[i]).astype(BF16)

    p_tiles = N_PROMPT // x.shape[0]

    @pl.when(t < p_tiles)
    def _():
        finish(qa, qb, kb, kpe, kb_p_ref, vb_p_ref)
        for i, s in enumerate(subs):
            ckv_p_ref[s, :] = ckv[i]
            kpe_p_ref[s, :] = kpe[i][:, :ROPE_A]

    @pl.when(t >= p_tiles)
    def _():
        w = qa[0].shape[1]
        reps = w // LANES
        half = ROPE_A // 2
        qa_rot, qb_rot, kb_rot, kpe_rot = [], [], [], []
        for i, s in enumerate(subs):
            ca = jnp.concatenate([ca_ref[s, :]] * reps, axis=1)
            sa = jnp.concatenate([sa_ref[s, :]] * reps, axis=1)
            sb = jnp.concatenate([sb_ref[s, :]] * reps, axis=1)
            qa_rot.append(qa[i] * ca + pltpu.roll(qa[i], w - half, 1) * sa + pltpu.roll(qa[i], half, 1) * sb)
            qb_rot.append(_rope(qb[i], c64_ref[s, :], s64_ref[s, :], HD // 2))
            kb_rot.append(_rope(kb[i], c64_ref[s, :], s64_ref[s, :], HD // 2))
            kpe_rot.append(_rope(kpe[i], c32_ref[s, :], s32_ref[s, :], ROPE_A // 2))
        finish(qa_rot, qb_rot, kb_rot, kpe_rot, kb_s_ref, vb_s_ref)


def _mla_weights(w_uq, w_ukv):
    pad = LANES - NOPE_A - ROPE_A
    uq = w_uq.reshape(Q_LORA, H_A, NOPE_A + ROPE_A)
    uq = jnp.pad(uq, ((0, 0), (0, 0), (0, pad))).reshape(Q_LORA, H_A * LANES)
    ukv = w_ukv.reshape(KV_LORA, H_A, NOPE_A + V_A)
    kn = jnp.pad(ukv[:, :, :NOPE_A], ((0, 0), (0, 0), (0, LANES - NOPE_A))).reshape(KV_LORA, H_A * LANES)
    wv = ukv[:, :, NOPE_A:].reshape(KV_LORA, H_A * V_A)
    src = np.arange(LANES)[:, None]
    dst = np.arange(H_A * LANES)[None, :] % LANES
    place = (src < ROPE_A) & (dst == src + NOPE_A)
    return uq.astype(BF16), kn.astype(BF16), wv.astype(BF16), jnp.asarray(place, BF16)


def _inproj_ab(xp, xs, mod, w_in, q_norm, kv_norm, w_uq, w_kn, w_v, place, tabs):
    widths = [(H_A * LANES, BF16), (H_A * LANES, BF16), (H_A * V_A, BF16), (H_B * HD, BF16)]
    out_specs = [pl.BlockSpec((TP, w), lambda t: (t, 0)) for w, _ in widths]
    out_shape = [jax.ShapeDtypeStruct((N_TOK, w), dt) for w, dt in widths]
    for w in (KV_LORA, ROPE_A, KVH_B * HD, KVH_B * HD):
        out_specs.append(_group_spec(TP, w, prompt=True))
        out_shape.append(jax.ShapeDtypeStruct((N_PROMPT, w), F32))
    for w in (KVH_B * HD, KVH_B * HD):
        out_specs.append(_group_spec(TP, w, prompt=False))
        out_shape.append(jax.ShapeDtypeStruct((N_SAMPLE, w), F32))
    return pl.pallas_call(
        _inproj_ab_kernel,
        grid=(N_TOK // TP,),
        in_specs=_split_specs(TP) + [_mod_spec(0, 0, TP), _mod_spec(0, 1, TP), _const_spec((D_MODEL, AB_COLS)),
                                     _const_spec((1, Q_LORA)), _const_spec((1, KV_LORA)),
                                     _const_spec((Q_LORA, H_A * LANES)), _const_spec((KV_LORA, H_A * LANES)),
                                     _const_spec((KV_LORA, H_A * V_A)), _const_spec((LANES, H_A * LANES))]
        + _rope_specs(7, TP),
        out_specs=out_specs,
        out_shape=out_shape,
        compiler_params=_params(("arbitrary",)),
        name="inproj_ab",
    )(xp, xs, mod, mod, w_in, q_norm, kv_norm, w_uq, w_kn, w_v, place, *tabs)


def _inproj_c_kernel(x_ref, shift_ref, scale_ref, w_in_ref, g_ref, ones_ref, c64_ref, s64_ref,
                     q_ref, k_p_ref, v_p_ref, k_s_ref, v_s_ref):
    t = pl.program_id(0)
    subs = _sub_tiles(x_ref.shape[0])
    h = [(x_ref[s, :] * (1.0 + scale_ref[...]) + shift_ref[...]).astype(BF16) for s in subs]
    z = [jnp.dot(h_s, w_in_ref[...], preferred_element_type=F32) for h_s in h]
    qk = [z_s[:, :C_NORM_COLS] for z_s in z]
    sq = [qk_s * qk_s for qk_s in qk]
    sq_hi = [sq_s.astype(BF16) for sq_s in sq]
    sq_lo = [(sq_s - hi_s.astype(F32)).astype(BF16) for sq_s, hi_s in zip(sq, sq_hi)]
    ss = []
    for hi_s, lo_s in zip(sq_hi, sq_lo):
        blocks = []
        for j in range(C_NORM_COLS // NORM_BLOCK):
            sl = slice(j * NORM_BLOCK, (j + 1) * NORM_BLOCK)
            blocks.append(jnp.dot(hi_s[:, sl], ones_ref[...], preferred_element_type=F32)
                          + jnp.dot(lo_s[:, sl], ones_ref[...], preferred_element_type=F32))
        ss.append(jnp.concatenate(blocks, axis=1))
    qk = [qk_s * lax.rsqrt(ss_s * (1.0 / HD) + EPS) * g_ref[...] for qk_s, ss_s in zip(qk, ss)]
    p_tiles = N_PROMPT // x_ref.shape[0]

    @pl.when(t < p_tiles)
    def _():
        for s, qk_s, z_s in zip(subs, qk, z):
            q_ref[s, :] = qk_s[:, :H_C * HD].astype(BF16)
            k_p_ref[s, :] = qk_s[:, H_C * HD:]
            v_p_ref[s, :] = z_s[:, C_NORM_COLS:]

    @pl.when(t >= p_tiles)
    def _():
        for s, qk_s, z_s in zip(subs, qk, z):
            q_ref[s, :] = _rope(qk_s[:, :H_C * HD], c64_ref[s, :], s64_ref[s, :], HD // 2).astype(BF16)
            k_s_ref[s, :] = _rope(qk_s[:, H_C * HD:], c64_ref[s, :], s64_ref[s, :], HD // 2)
            v_s_ref[s, :] = z_s[:, C_NORM_COLS:]


def _inproj_c(x, mod, w_in, g_full, tabs):
    def tok(w):
        return pl.BlockSpec((TP, w), lambda t: (t, 0))
    kv_w = KVH_C * HD
    head = np.arange(NORM_BLOCK) // HD
    ones = jnp.asarray(head[:, None] == head[None, :], BF16)
    return pl.pallas_call(
        _inproj_c_kernel,
        grid=(N_TOK // TP,),
        in_specs=[tok(D_MODEL), _mod_spec(1, 0, TP), _mod_spec(1, 1, TP),
                  _const_spec((D_MODEL, C_NORM_COLS + KVH_C * HD)),
                  _const_spec((1, C_NORM_COLS)), _const_spec((NORM_BLOCK, NORM_BLOCK))] + _rope_specs(2, TP),
        out_specs=[tok(H_C * HD), _group_spec(TP, kv_w, True), _group_spec(TP, kv_w, True),
                   _group_spec(TP, kv_w, False), _group_spec(TP, kv_w, False)],
        out_shape=[jax.ShapeDtypeStruct((N_TOK, H_C * HD), BF16),
                   jax.ShapeDtypeStruct((N_PROMPT, kv_w), F32), jax.ShapeDtypeStruct((N_PROMPT, kv_w), F32),
                   jax.ShapeDtypeStruct((N_SAMPLE, kv_w), F32), jax.ShapeDtypeStruct((N_SAMPLE, kv_w), F32)],
        compiler_params=_params(("arbitrary",)),
        name="inproj_c",
    )(x, mod, mod, w_in, g_full, ones, *tabs[:2])


def _qk(q, k):
    return lax.dot_general(q, k, (((1,), (1,)), ((), ())), preferred_element_type=F32)


def _softmax_pv(scores, values, scale, sink=None):
    c = scale * LOG2E
    m = scores[0].max(axis=-1, keepdims=True)
    for s in scores[1:]:
        m = jnp.maximum(m, s.max(axis=-1, keepdims=True))
    m = m * c
    if sink is not None:
        sink = sink * LOG2E
        m = jnp.maximum(m, sink)
    den = None
    out = None
    for s, v in zip(scores, values):
        p = jnp.exp2(s * c - m)
        d = p.sum(axis=-1, keepdims=True)
        o = jnp.dot(p.astype(BF16), v, preferred_element_type=F32)
        den = d if den is None else den + d
        out = o if out is None else out + o
    if sink is not None:
        den = den + jnp.exp2(sink - m)
    return out * (1.0 / den)


def _mla_heads(qa_ref, ka_refs, va_refs, o_ref):
    scale = (NOPE_A + ROPE_A) ** -0.5
    for h0 in range(0, H_A, 2):
        pair = []
        for h in (h0, h0 + 1):
            q = qa_ref[:, h * LANES:(h + 1) * LANES]
            scores = [_qk(q, ka_ref[:, h * LANES:(h + 1) * LANES]) for ka_ref in ka_refs]
            values = [va_ref[:, h * V_A:(h + 1) * V_A] for va_ref in va_refs]
            pair.append(_softmax_pv(scores, values, scale))
        o_ref[:, h0 * V_A:(h0 + 2) * V_A] = jnp.concatenate(pair, axis=1).astype(o_ref.dtype)


def _gqa_group(q_ref, col0, n_q, ks, vs, mask=None, sink=None):
    rows = q_ref.shape[0]
    q = jnp.concatenate([q_ref[:, col0 + g * HD:col0 + (g + 1) * HD] for g in range(n_q)], axis=0)
    scores = [_qk(q, k) for k in ks]
    if mask is not None:
        scores[0] = jnp.where(mask, scores[0], NEG_INF)
    o = _softmax_pv(scores, vs, HD ** -0.5, sink)
    return jnp.concatenate([o[g * rows:(g + 1) * rows] for g in range(n_q)], axis=1)


def _stacked_sink(sink_ref, h0, n_q, rows):
    head = lax.broadcasted_iota(jnp.int32, (n_q * rows, 1), 0) // rows
    col = jnp.full((n_q * rows, 1), sink_ref[h0], F32)
    for g in range(1, n_q):
        col = jnp.where(head == g, sink_ref[h0 + g], col)
    return col


def _attn_ab_prompt_kernel(sink_ref, qa_ref, ka_ref, va_ref, qb_ref, kb_ref, vb_ref, o_ref):
    _mla_heads(qa_ref, [ka_ref], [va_ref], o_ref)
    kb = kb_ref[...].astype(BF16)
    vb = vb_ref[...].astype(BF16)
    gq = H_B // KVH_B
    for kh in range(KVH_B):
        sl = slice(kh * HD, (kh + 1) * HD)
        o = _gqa_group(qb_ref, kh * gq * HD, gq, [kb[:, sl]], [vb[:, sl]],
                       sink=_stacked_sink(sink_ref, kh * gq, gq, SEQ))
        col = H_A * V_A + kh * gq * HD
        o_ref[:, col:col + gq * HD] = o.astype(o_ref.dtype)


def _attn_ab_prompt(sink, qa, ka, va, qb, kb, vb):
    def blk(w):
        return pl.BlockSpec((SEQ, w), lambda b: (b, 0))
    return pl.pallas_call(
        _attn_ab_prompt_kernel,
        grid=(BATCH,),
        in_specs=[pl.BlockSpec(memory_space=pltpu.SMEM), blk(H_A * LANES), blk(H_A * LANES), blk(H_A * V_A),
                  blk(H_B * HD), blk(KVH_B * HD), blk(KVH_B * HD)],
        out_specs=blk(D_MODEL),
        out_shape=jax.ShapeDtypeStruct((N_PROMPT, H_A * V_A + H_B * HD), BF16),
        compiler_params=_params(("arbitrary",)),
        name="attn_ab_prompt",
    )(sink, qa, ka, va, qb, kb, vb)


def _attn_ab_sample_kernel(sink_ref, qa_ref, ka_ref, va_ref, qb_ref, kb_ref, vb_ref,
                           ckv_ctx_ref, kpe_ctx_ref, kb_ctx_ref, vb_ctx_ref, w_kn_ref, w_v_ref, place_ref,
                           o_ref, ka_ctx_ref, va_ctx_ref):
    i = pl.program_id(1)

    @pl.when(i == 0)
    def _():
        ckv16 = ckv_ctx_ref[...].astype(BF16)
        k_nope = jnp.dot(ckv16, w_kn_ref[...], preferred_element_type=F32)
        k_rope = jnp.dot(kpe_ctx_ref[...].astype(BF16), place_ref[...], preferred_element_type=F32)
        ka_ctx_ref[...] = (k_nope + k_rope).astype(BF16)
        va_ctx_ref[...] = jnp.dot(ckv16, w_v_ref[...], preferred_element_type=F32).astype(BF16)

    _mla_heads(qa_ref, [ka_ref, ka_ctx_ref], [va_ref, va_ctx_ref], o_ref)

    gq = H_B // KVH_B
    n_win = BQ_S + 2 * WINDOW
    start = pl.multiple_of(jnp.clip(i * BQ_S - WINDOW, 0, DEC_SEQ - n_win), WINDOW)
    row = lax.broadcasted_iota(jnp.int32, (gq * BQ_S, n_win), 0)
    qpos = i * BQ_S + (row & (BQ_S - 1))
    kpos = start + lax.broadcasted_iota(jnp.int32, (gq * BQ_S, n_win), 1)
    in_band = jnp.abs(qpos - kpos) <= WINDOW
    kwin = kb_ref[pl.ds(start, n_win), :].astype(BF16)
    vwin = vb_ref[pl.ds(start, n_win), :].astype(BF16)
    kctx = kb_ctx_ref[...].astype(BF16)
    vctx = vb_ctx_ref[...].astype(BF16)
    for kh in range(KVH_B):
        sl = slice(kh * HD, (kh + 1) * HD)
        o = _gqa_group(qb_ref, kh * gq * HD, gq, [kwin[:, sl], kctx[:, sl]], [vwin[:, sl], vctx[:, sl]],
                       mask=in_band, sink=_stacked_sink(sink_ref, kh * gq, gq, BQ_S))
        col = H_A * V_A + kh * gq * HD
        o_ref[:, col:col + gq * HD] = o.astype(o_ref.dtype)


def _attn_ab_sample(sink, qa, ka, va, qb, kb, vb, ckv_ctx, kpe_ctx, kb_ctx, vb_ctx, w_kn, w_v, place):
    qoff = N_PROMPT // BQ_S
    nq = DEC_SEQ // BQ_S
    boff = N_PROMPT // DEC_SEQ

    def qblk(w):
        return pl.BlockSpec((BQ_S, w), lambda b, i: (qoff + b * nq + i, 0))

    def bblk(w):
        return pl.BlockSpec((DEC_SEQ, w), lambda b, i: (boff + b, 0))

    def sblk(w):
        return pl.BlockSpec((DEC_SEQ, w), lambda b, i: (b, 0))

    def cblk(w):
        return pl.BlockSpec((PAST_LEN, w), lambda b, i: (b, 0))

    return pl.pallas_call(
        _attn_ab_sample_kernel,
        grid=(DEC_BATCH, nq),
        in_specs=[pl.BlockSpec(memory_space=pltpu.SMEM), qblk(H_A * LANES), bblk(H_A * LANES), bblk(H_A * V_A),
                  qblk(H_B * HD), sblk(KVH_B * HD), sblk(KVH_B * HD),
                  cblk(KV_LORA), cblk(LANES), cblk(KVH_B * HD), cblk(KVH_B * HD),
                  _const_spec((KV_LORA, H_A * LANES)), _const_spec((KV_LORA, H_A * V_A)),
                  _const_spec((LANES, H_A * LANES))],
        out_specs=pl.BlockSpec((BQ_S, D_MODEL), lambda b, i: (b * nq + i, 0)),
        out_shape=jax.ShapeDtypeStruct((N_SAMPLE, H_A * V_A + H_B * HD), BF16),
        scratch_shapes=[pltpu.VMEM((PAST_LEN, H_A * LANES), BF16), pltpu.VMEM((PAST_LEN, H_A * V_A), BF16)],
        compiler_params=_params(("arbitrary", "arbitrary")),
        name="attn_ab_sample",
    )(sink, qa, ka, va, qb, kb, vb, ckv_ctx, kpe_ctx, kb_ctx, vb_ctx, w_kn, w_v, place)


def _gqa_heads(q_ref, k_list, v_list, o_ref):
    gq = H_C // KVH_C
    for kh in range(KVH_C):
        sl = slice(kh * HD, (kh + 1) * HD)
        o = _gqa_group(q_ref, kh * gq * HD, gq, [k[:, sl] for k in k_list], [v[:, sl] for v in v_list])
        o_ref[:, kh * gq * HD:(kh + 1) * gq * HD] = o.astype(o_ref.dtype)


def _attn_c_prompt_kernel(q_ref, k_ref, v_ref, o_ref):
    _gqa_heads(q_ref, [k_ref[...].astype(BF16)], [v_ref[...].astype(BF16)], o_ref)


def _attn_c_prompt(q, k, v):
    def blk(w):
        return pl.BlockSpec((SEQ, w), lambda b: (b, 0))
    return pl.pallas_call(
        _attn_c_prompt_kernel,
        grid=(BATCH,),
        in_specs=[blk(H_C * HD), blk(KVH_C * HD), blk(KVH_C * HD)],
        out_specs=blk(H_C * HD),
        out_shape=jax.ShapeDtypeStruct((N_PROMPT, H_C * HD), BF16),
        compiler_params=_params(("arbitrary",)),
        name="attn_c_prompt",
    )(q, k, v)


def _attn_c_sample_kernel(q_ref, k_ref, v_ref, kc_ref, vc_ref, o_ref):
    _gqa_heads(q_ref, [k_ref[...].astype(BF16), kc_ref[...].astype(BF16)],
               [v_ref[...].astype(BF16), vc_ref[...].astype(BF16)], o_ref)


def _attn_c_sample(q, k, v, k_ctx, v_ctx):
    qoff = N_PROMPT // BQ_S
    nq = DEC_SEQ // BQ_S
    w = KVH_C * HD
    return pl.pallas_call(
        _attn_c_sample_kernel,
        grid=(DEC_BATCH, nq),
        in_specs=[pl.BlockSpec((BQ_S, H_C * HD), lambda b, i: (qoff + b * nq + i, 0)),
                  pl.BlockSpec((DEC_SEQ, w), lambda b, i: (b, 0)),
                  pl.BlockSpec((DEC_SEQ, w), lambda b, i: (b, 0)),
                  pl.BlockSpec((PAST_LEN, w), lambda b, i: (b, 0)),
                  pl.BlockSpec((PAST_LEN, w), lambda b, i: (b, 0))],
        out_specs=pl.BlockSpec((BQ_S, H_C * HD), lambda b, i: (b * nq + i, 0)),
        out_shape=jax.ShapeDtypeStruct((N_SAMPLE, H_C * HD), BF16),
        compiler_params=_params(("arbitrary", "arbitrary")),
        name="attn_c_sample",
    )(q, k, v, k_ctx, v_ctx)


def _layer_norm(y, g, b):
    mu = jnp.mean(y, axis=-1, keepdims=True)
    yc = y - mu
    var = jnp.mean(yc * yc, axis=-1, keepdims=True)
    return yc * lax.rsqrt(var + EPS) * g + b


def _route(sel, aff):
    def row(a, j):
        return a[j:j + 1, :]

    scores = []
    for g in range(N_GROUPS):
        a0, a1, a2, a3 = (row(sel, 4 * g + j) for j in range(4))
        hi01, lo01 = jnp.maximum(a0, a1), jnp.minimum(a0, a1)
        hi23, lo23 = jnp.maximum(a2, a3), jnp.minimum(a2, a3)
        top1 = jnp.maximum(hi01, hi23)
        top2 = jnp.maximum(jnp.minimum(hi01, hi23), jnp.maximum(lo01, lo23))
        scores.append(top1 + top2)
    best = scores[0]
    gi = jnp.zeros(best.shape, jnp.int32)
    for g in range(1, N_GROUPS):
        better = scores[g] > best
        gi = jnp.where(better, g, gi)
        best = jnp.where(better, scores[g], best)

    def in_group(a, j):
        out = row(a, j)
        for g in range(1, N_GROUPS):
            out = jnp.where(gi == g, row(a, 4 * g + j), out)
        return out

    v = [in_group(sel, j) for j in range(4)]
    a = [in_group(aff, j) for j in range(4)]
    chosen = []
    for j in range(4):
        rank = jnp.zeros(best.shape, jnp.int32)
        for k in range(4):
            if k == j:
                continue
            ahead = (v[k] >= v[j]) if k < j else (v[k] > v[j])
            rank = rank + ahead.astype(jnp.int32)
        chosen.append(rank < 2)
    total = sum(jnp.where(chosen[j], a[j], 0.0) for j in range(4))
    w = [jnp.where(chosen[j], a[j], 0.0) / total for j in range(4)]
    pair = jnp.zeros(best.shape, jnp.int32)
    for p, (ja, jb) in enumerate(zip(SLOT_A_LOCAL, SLOT_B_LOCAL)):
        pair = jnp.where(chosen[ja] & chosen[jb], p, pair)
    gate_a = jnp.where(pair == 0, w[0], jnp.where(pair <= 2, w[2], w[3]))
    gate_b = jnp.where((pair == 0) | (pair == 1) | (pair == 4), w[1], jnp.where(pair == 5, w[2], w[0]))
    return gi * PAIRS_PER_GROUP + pair, gate_a, gate_b


def _post_attn_kernel(split_x, *refs):
    t = pl.program_id(0)
    op_ref, os_ref = refs[:2]
    if split_x:
        x = _pick(t, refs[2], refs[3])
        refs = refs[4:]
    else:
        x = refs[2][...]
        refs = refs[3:]
    (w_out_ref, gate_ref, shift_ref, scale_ref, lng_ref, lnb_ref, rw_ref, rb_ref, tri_ref,
     x1_ref, row_ref, route_ref, counts_ref, carry_ref) = refs
    o = _pick(t, op_ref, os_ref)
    subs = _sub_tiles(o.shape[0])

    a = [jnp.dot(o[s], w_out_ref[...], preferred_element_type=F32) for s in subs]
    x1 = [_layer_norm(ALPHA * x[s] + gate_ref[...] * a_s, lng_ref[...], lnb_ref[...]) for s, a_s in zip(subs, a)]
    h2 = [x1_s * (1.0 + scale_ref[...]) + shift_ref[...] for x1_s in x1]
    for s, x1_s, h2_s in zip(subs, x1, h2):
        x1_ref[s, :] = x1_s
        row_ref[s, :D_MODEL] = h2_s
    logits = [lax.dot_general(rw_ref[...], h2_s, (((1,), (1,)), ((), ())), preferred_element_type=F32,
                              precision=lax.Precision.HIGHEST) for h2_s in h2]
    aff = [1.0 / (1.0 + jnp.exp(-l)) for l in logits]
    routed = [_route(aff_s + rb_ref[...], aff_s) for aff_s in aff]

    lane = lax.broadcasted_iota(jnp.int32, (LANES, TM), 0)
    for s, (_, gate_a, gate_b) in zip(subs, routed):
        meta_t = jnp.where(lane == 0, gate_a, jnp.where(lane == 1, gate_b, 0.0))
        row_ref[s, D_MODEL:] = meta_t.T

    @pl.when(t == 0)
    def _():
        carry_ref[...] = jnp.zeros_like(carry_ref)

    onehots = [lax.broadcasted_iota(jnp.int32, (BUCKET_ROWS, TM), 0) == bucket for bucket, _, _ in routed]
    prefixes = [jnp.dot(oh.astype(BF16), tri_ref[...], preferred_element_type=F32) for oh in onehots]
    carry = carry_ref[...]
    route_ref[...] = jnp.zeros(route_ref.shape, F32)
    for s, (bucket, _, _), onehot, prefix in zip(subs, routed, onehots, prefixes):
        rank = jnp.sum(jnp.where(onehot, prefix + carry[:, 0:1], 0.0), axis=0, keepdims=True)
        carry = carry + jnp.sum(onehot.astype(F32), axis=1, keepdims=True)
        route_ref[0:1, s] = bucket.astype(F32)
        route_ref[1:2, s] = rank
    carry_ref[...] = carry
    counts_ref[...] = carry


def _post_attn(layer, o_p, o_s, x_list, mod, w_out, ln_g, ln_b, rw_t, rb):
    split_x = len(x_list) == 2
    tok = pl.BlockSpec((TP, D_MODEL), lambda t: (t, 0))
    tri = jnp.asarray(np.arange(TM)[:, None] < np.arange(TM)[None, :], BF16)
    return pl.pallas_call(
        functools.partial(_post_attn_kernel, split_x),
        grid=(N_TOK // TP,),
        in_specs=_split_specs(TP) + (_split_specs(TP) if split_x else [tok]) + [
            _const_spec((D_MODEL, D_MODEL)), _mod_spec(layer, 2, TP), _mod_spec(layer, 3, TP),
            _mod_spec(layer, 4, TP),
            _const_spec((1, D_MODEL)), _const_spec((1, D_MODEL)), _const_spec((N_EXPERTS, D_MODEL)),
            _const_spec((N_EXPERTS, 1)), _const_spec((TM, TM))],
        out_specs=[tok, pl.BlockSpec((TP, ROW_W), lambda t: (t, 0)), pl.BlockSpec((8, TP), lambda t: (0, t)),
                   _const_spec((BUCKET_ROWS, LANES))],
        out_shape=[jax.ShapeDtypeStruct((N_TOK, D_MODEL), F32), jax.ShapeDtypeStruct((N_TOK, ROW_W), F32),
                   jax.ShapeDtypeStruct((8, N_TOK), F32), jax.ShapeDtypeStruct((BUCKET_ROWS, LANES), F32)],
        scratch_shapes=[pltpu.VMEM((BUCKET_ROWS, LANES), F32)],
        compiler_params=_params(("arbitrary",)),
        name="post_attn",
    )(o_p, o_s, *x_list, w_out, mod, mod, mod, ln_g, ln_b, rw_t, rb, tri)


def _moe_kernel(ta_ref, tb_ref, nt_ref, x_ref, wga_ref, wua_ref, wda_ref, wgb_ref, wub_ref, wdb_ref, o_ref,
                wga_s, wua_s, wda_s, wgb_s, wub_s, wdb_s):
    i = pl.program_id(0)
    prev = jnp.maximum(i - 1, 0)
    slots = ((ta_ref, (wga_ref, wua_ref, wda_ref), (wga_s, wua_s, wda_s)),
             (tb_ref, (wgb_ref, wub_ref, wdb_ref), (wgb_s, wub_s, wdb_s)))

    for t_ref, w_refs, w_scr in slots:
        @pl.when((i == 0) | (t_ref[i] != t_ref[prev]))
        def _(w_refs=w_refs, w_scr=w_scr):
            for w_ref, s_ref in zip(w_refs, w_scr):
                s_ref[...] = w_ref[...].astype(BF16)

    @pl.when(i < nt_ref[0])
    def _():
        x = x_ref[:, :D_MODEL].astype(BF16)
        acc = None
        for slot, (_, _, (wg_s, wu_s, wd_s)) in enumerate(slots):
            gt = jnp.dot(x, wg_s[...], preferred_element_type=F32)
            up = jnp.dot(x, wu_s[...], preferred_element_type=F32)
            gate = x_ref[:, D_MODEL + slot:D_MODEL + slot + 1]
            hid = (gt / (1.0 + jnp.exp(-gt))) * up * gate
            y = jnp.dot(hid.astype(BF16), wd_s[...], preferred_element_type=F32)
            acc = y if acc is None else acc + y
        o_ref[...] = acc

    @pl.when(i >= nt_ref[0])
    def _():
        o_ref[...] = jnp.zeros_like(o_ref)


def _moe(layer, tile_a, tile_b, n_tiles, rows, w_gate, w_up, w_down):
    def row_idx(i, ta, tb, nt):
        return (jnp.maximum(jnp.minimum(i, nt[0] - 1), 0), 0)

    def wspec(shape, which):
        if which == 0:
            return pl.BlockSpec((None, None) + shape, lambda i, ta, tb, nt: (layer, ta[i], 0, 0))
        return pl.BlockSpec((None, None) + shape, lambda i, ta, tb, nt: (layer, tb[i], 0, 0))

    up_shape, down_shape = (D_MODEL, D_EXPERT), (D_EXPERT, D_MODEL)
    grid_spec = pltpu.PrefetchScalarGridSpec(
        num_scalar_prefetch=3,
        grid=(MOE_TILES,),
        in_specs=[pl.BlockSpec((TME, ROW_W), row_idx),
                  wspec(up_shape, 0), wspec(up_shape, 0), wspec(down_shape, 0),
                  wspec(up_shape, 1), wspec(up_shape, 1), wspec(down_shape, 1)],
        out_specs=pl.BlockSpec((TME, D_MODEL), lambda i, ta, tb, nt: (i, 0)),
        scratch_shapes=[pltpu.VMEM(s, BF16) for s in (up_shape, up_shape, down_shape) * 2],
    )
    return pl.pallas_call(
        _moe_kernel,
        grid_spec=grid_spec,
        out_shape=jax.ShapeDtypeStruct((MOE_ROWS, D_MODEL), F32),
        compiler_params=_params(("arbitrary",)),
        name="moe_experts",
    )(tile_a, tile_b, n_tiles, rows, w_gate, w_up, w_down, w_gate, w_up, w_down)


def _row_copy(src_ref, src_row, dst_ref, dst_row, sem):
    return pltpu.make_async_copy(src_ref.at[pl.ds(src_row, 1), :], dst_ref.at[pl.ds(dst_row, 1), :], sem)


def _dispatch_kernel(pos_ref, pend_ref, cnt_ref, nt_ref, src_ref, out_ref, buf, zero_ref, in_sem, sem, zsem):
    t = pl.program_id(0)

    def fetch(tile):
        return pltpu.make_async_copy(src_ref.at[pl.ds(pl.multiple_of(tile * TM, TM), TM), :],
                                     buf.at[tile % STAGE_SLOTS], in_sem.at[tile % STAGE_SLOTS])

    @pl.when(t == 0)
    def _():
        fetch(0).start()
        zero_ref[...] = jnp.zeros_like(zero_ref)

        def zero_tile(row0):
            return pltpu.make_async_copy(zero_ref, out_ref.at[pl.ds(pl.multiple_of(row0, TME), TME), :], zsem)

        for b in range(N_BUCKETS):
            @pl.when(cnt_ref[b] > 0)
            def _(b=b):
                zero_tile(pend_ref[b] - TME).start()

        def start_unused(i, carry):
            zero_tile(i * TME).start()
            return carry
        lax.fori_loop(nt_ref[0], MOE_TILES, start_unused, 0)

        for b in range(N_BUCKETS):
            @pl.when(cnt_ref[b] > 0)
            def _(b=b):
                zero_tile(pend_ref[b] - TME).wait()

        def wait_unused(i, carry):
            zero_tile(i * TME).wait()
            return carry
        lax.fori_loop(nt_ref[0], MOE_TILES, wait_unused, 0)

    @pl.when(t + 1 < N_TILES)
    def _():
        fetch(t + 1).start()

    fetch(t).wait()
    tile_ref = buf.at[t % STAGE_SLOTS]

    def issue(i, carry):
        for k in range(2):
            r = 2 * i + k
            _row_copy(tile_ref, r, out_ref, pos_ref[t * TM + r], sem.at[t % 2]).start(priority=k)
        return carry
    lax.fori_loop(0, TM // 2, issue, 0, unroll=4)

    def drain(tile):
        pltpu.make_async_copy(buf.at[0], out_ref.at[pl.ds(0, TM), :], sem.at[tile % 2]).wait()

    @pl.when(t > 0)
    def _():
        drain(t - 1)

    @pl.when(t == N_TILES - 1)
    def _():
        drain(t)


def _dispatch(pos, pend, counts, n_tiles, rows):
    return pl.pallas_call(
        _dispatch_kernel,
        grid_spec=pltpu.PrefetchScalarGridSpec(
            num_scalar_prefetch=4, grid=(N_TILES,),
            in_specs=[pl.BlockSpec(memory_space=pl.ANY)],
            out_specs=pl.BlockSpec(memory_space=pl.ANY),
            scratch_shapes=[pltpu.VMEM((STAGE_SLOTS, TM, ROW_W), F32), pltpu.VMEM((TME, ROW_W), F32),
                            pltpu.SemaphoreType.DMA((STAGE_SLOTS,)), pltpu.SemaphoreType.DMA((2,)),
                            pltpu.SemaphoreType.DMA]),
        out_shape=jax.ShapeDtypeStruct((MOE_ROWS, ROW_W), F32),
        compiler_params=_params(("arbitrary",)),
        name="dispatch_rows",
    )(pos, pend, counts, n_tiles, rows)


def _plan(route, counts):
    bucket = route[0].astype(jnp.int32)
    rank = route[1].astype(jnp.int32)
    counts = counts[:N_BUCKETS, 0].astype(jnp.int32)
    padded = ((counts + TME - 1) // TME) * TME
    pend = jnp.cumsum(padded)
    pstart = pend - padded
    ids = jnp.arange(N_BUCKETS, dtype=jnp.int32)
    pos = rank + jnp.sum(jnp.where(bucket[None, :] == ids[:, None], pstart[:, None], 0), axis=0)
    n_tiles = pend[-1] // TME
    tile_start = jnp.minimum(jnp.arange(MOE_TILES, dtype=jnp.int32), n_tiles - 1) * TME
    tile_bucket = jnp.minimum(jnp.sum(tile_start[:, None] >= pend[None, :], axis=1), N_BUCKETS - 1)
    group, pair = tile_bucket // PAIRS_PER_GROUP, tile_bucket % PAIRS_PER_GROUP
    slot_a = jnp.asarray(SLOT_A_LOCAL, jnp.int32)
    slot_b = jnp.asarray(SLOT_B_LOCAL, jnp.int32)
    pair_hot = pair[:, None] == jnp.arange(PAIRS_PER_GROUP, dtype=jnp.int32)[None, :]
    tile_a = group * EXPERTS_PER_GROUP + jnp.sum(jnp.where(pair_hot, slot_a[None, :], 0), axis=1)
    tile_b = group * EXPERTS_PER_GROUP + jnp.sum(jnp.where(pair_hot, slot_b[None, :], 0), axis=1)
    return pos, pend, counts, tile_a.astype(jnp.int32), tile_b.astype(jnp.int32), n_tiles.reshape(1)


def _post_moe_kernel(split_out, pos_ref, f_ref, x1_ref, gate_ref, lng_ref, lnb_ref, *refs):
    out_refs, (fbuf, sem) = refs[:-2], refs[-2:]
    t = pl.program_id(0)

    def wait_tile(slot):
        pltpu.make_async_copy(f_ref.at[pl.ds(0, TM), :], fbuf.at[slot], sem.at[slot]).wait()

    @pl.when(t == 0)
    def _():
        def issue(r, carry):
            _row_copy(f_ref, pos_ref[r], fbuf.at[r // TM], r % TM, sem.at[r // TM]).start()
            return carry
        lax.fori_loop(0, (GATHER_SLOTS - 1) * TM, issue, 0, unroll=8)

    slot = t % GATHER_SLOTS
    wait_tile(slot)
    ahead = jnp.minimum(t + GATHER_SLOTS - 1, N_TILES - 1)
    aslot = (t + GATHER_SLOTS - 1) % GATHER_SLOTS
    for r in range(TM):
        _row_copy(f_ref, pos_ref[ahead * TM + r], fbuf.at[aslot], r, sem.at[aslot]).start(priority=r % 2)
    y = _layer_norm(ALPHA * x1_ref[...] + gate_ref[...] * fbuf[slot], lng_ref[...], lnb_ref[...])

    @pl.when(t == N_TILES - 1)
    def _():
        for k in range(1, GATHER_SLOTS):
            wait_tile((t + k) % GATHER_SLOTS)
    if split_out:
        @pl.when(t < PROMPT_TILES)
        def _():
            out_refs[0][...] = y

        @pl.when(t >= PROMPT_TILES)
        def _():
            out_refs[1][...] = y
    else:
        out_refs[0][...] = y


def _post_moe(layer, pos, f_sorted, x1, mod, ln_g, ln_b, split_out):
    tok = pl.BlockSpec((TM, D_MODEL), lambda t, p: (t, 0))
    if split_out:
        out_specs = [pl.BlockSpec((TM, D_MODEL), lambda t, p: (jnp.minimum(t, PROMPT_TILES - 1), 0)),
                     pl.BlockSpec((TM, D_MODEL), lambda t, p: (jnp.maximum(t - PROMPT_TILES, 0), 0))]
        out_shape = [jax.ShapeDtypeStruct((N_PROMPT, D_MODEL), F32), jax.ShapeDtypeStruct((N_SAMPLE, D_MODEL), F32)]
    else:
        out_specs = [tok]
        out_shape = [jax.ShapeDtypeStruct((N_TOK, D_MODEL), F32)]
    mod_spec = pl.BlockSpec((None, None, None, 1, D_MODEL), lambda t, p: (layer, _mod_row(t), 5, 0, 0))
    return pl.pallas_call(
        functools.partial(_post_moe_kernel, split_out),
        grid_spec=pltpu.PrefetchScalarGridSpec(
            num_scalar_prefetch=1, grid=(N_TILES,),
            in_specs=[pl.BlockSpec(memory_space=pl.ANY), tok, mod_spec,
                      pl.BlockSpec((1, D_MODEL), lambda t, p: (0, 0)), pl.BlockSpec((1, D_MODEL), lambda t, p: (0, 0))],
            out_specs=out_specs,
            scratch_shapes=[pltpu.VMEM((GATHER_SLOTS, TM, D_MODEL), F32),
                            pltpu.SemaphoreType.DMA((GATHER_SLOTS,))]),
        out_shape=out_shape,
        compiler_params=_params(("arbitrary",)),
        name="post_moe",
    )(pos, f_sorted, x1, mod, ln_g, ln_b)


def _ffn(layer, o_p, o_s, x_list, mod, w_out, ln_g, ln_b, rw_t, rb, w_gate, w_up, w_down, split_out):
    x1, rows, route, counts = _post_attn(layer, o_p, o_s, x_list, mod, w_out, ln_g[layer, 0][None],
                                         ln_b[layer, 0][None], rw_t, rb)
    pos, pend, counts, tile_a, tile_b, n_tiles = _plan(route, counts)
    rows_sorted = _dispatch(pos, pend, counts, n_tiles, rows)
    f_sorted = _moe(layer, tile_a, tile_b, n_tiles, rows_sorted, w_gate, w_up, w_down)
    return _post_moe(layer, pos, f_sorted, x1, mod, ln_g[layer, 1][None], ln_b[layer, 1][None], split_out)


def kernel(x_prompt, x_sample, c, cache_mla_ckv, cache_mla_kpe, cache_swa_k, cache_swa_v, cache_gqa_k, cache_gqa_v, c_ctx, w_mod, b_mod, ln_g, ln_b, w_in_ab, mla_q_norm, mla_w_uq, mla_kv_norm, mla_w_ukv, swa_sink, w_out_ab, w_in_c, gqa_q_norm, gqa_k_norm, w_out_c, router_w, router_bias, exp_w_gate, exp_w_up, exp_w_down):
    xp = x_prompt.reshape(N_PROMPT, D_MODEL)
    xs = x_sample.reshape(N_SAMPLE, D_MODEL)
    cond = jnp.concatenate([c_ctx[None], c, jnp.zeros((MOD_ROWS - 1 - DEC_BATCH, D_MODEL), F32)], axis=0)
    mod = _modulation(cond, w_mod, b_mod).reshape(DEPTH, MOD_ROWS, 6, 1, D_MODEL)
    tabs = _rope_tables()
    rw_t = router_w.T
    rb = router_bias.reshape(N_EXPERTS, 1)

    w = w_in_ab[0]
    pad = jnp.zeros((D_MODEL, AB_COLS - w.shape[1]), F32)
    w_in = jnp.concatenate([w[:, :384], w[:, 416:], w[:, 384:416], pad], axis=1).astype(BF16)
    w_uq, w_kn, w_v, place = _mla_weights(mla_w_uq[0], mla_w_ukv[0])
    qa, ka, va, qb, ckv_p, kpe_p, kb_p, vb_p, kb_s, vb_s = _inproj_ab(
        xp, xs, mod, w_in, mla_q_norm[0][None], mla_kv_norm[0][None], w_uq, w_kn, w_v, place, tabs)
    sink = swa_sink[0]
    o_p = _attn_ab_prompt(sink, qa, ka, va, qb, kb_p, vb_p)
    kpe_ctx = jnp.pad(cache_mla_kpe[:, 0].reshape(-1, ROPE_A), ((0, 0), (0, LANES - ROPE_A)))
    o_s = _attn_ab_sample(sink, qa, ka, va, qb, kb_s, vb_s,
                          cache_mla_ckv[:, 0].reshape(-1, KV_LORA), kpe_ctx,
                          cache_swa_k[:, 0].reshape(-1, KVH_B * HD), cache_swa_v[:, 0].reshape(-1, KVH_B * HD),
                          w_kn, w_v, place)
    (x2,) = _ffn(0, o_p, o_s, [xp, xs], mod, w_out_ab[0].astype(BF16), ln_g, ln_b, rw_t, rb,
                 exp_w_gate, exp_w_up, exp_w_down, split_out=False)

    g_full = jnp.concatenate([jnp.tile(gqa_q_norm[0], H_C), jnp.tile(gqa_k_norm[0], KVH_C)])[None]
    qc, kc_p, vc_p, kc_s, vc_s = _inproj_c(x2, mod, w_in_c[0].astype(BF16), g_full, tabs)
    oc_p = _attn_c_prompt(qc, kc_p, vc_p)
    oc_s = _attn_c_sample(qc, kc_s, vc_s, cache_gqa_k[:, 0].reshape(-1, KVH_C * HD),
                          cache_gqa_v[:, 0].reshape(-1, KVH_C * HD))
    y_p, y_s = _ffn(1, oc_p, oc_s, [x2], mod, w_out_c[0].astype(BF16), ln_g, ln_b, rw_t, rb,
                    exp_w_gate, exp_w_up, exp_w_down, split_out=True)

    y_prompt = y_p.reshape(BATCH, SEQ, D_MODEL)
    y_sample = y_s.reshape(DEC_BATCH, DEC_SEQ, D_MODEL)
    new_ckv = ckv_p.reshape(BATCH, 1, SEQ, KV_LORA)
    new_kpe = kpe_p.reshape(BATCH, 1, SEQ, ROPE_A)
    new_swk = kb_p.reshape(BATCH, 1, SEQ, KVH_B, HD)
    new_swv = vb_p.reshape(BATCH, 1, SEQ, KVH_B, HD)
    new_gk = kc_p.reshape(BATCH, 1, SEQ, KVH_C, HD)
    new_gv = vc_p.reshape(BATCH, 1, SEQ, KVH_C, HD)
    return y_prompt, y_sample, new_ckv, new_kpe, new_swk, new_swv, new_gk, new_gv
```

```python
import functools

import numpy as np
import jax
import jax.numpy as jnp
from jax import lax
from jax.experimental import pallas as pl
from jax.experimental.pallas import tpu as pltpu

D_MODEL = 1024
BATCH = 32
SEQ = 256
DEPTH = 2
DEC_BATCH = 4
DEC_SEQ = 1024
PAST_LEN = 256
GRID_W = 64
ROPE_THETA = 10000.0
WINDOW = 128
HD = 64
H_A = 8
NOPE_A = 64
ROPE_A = 32
V_A = 64
Q_LORA = 256
KV_LORA = 128
H_B = 8
KVH_B = 2
H_C = 16
KVH_C = 4
N_EXPERTS = 16
N_GROUPS = 4
EXPERTS_PER_GROUP = 4
D_EXPERT = 512
ALPHA = (2 * DEPTH) ** 0.25
NEG_INF = -1e30
EPS = 1e-6

LANES = 128
N_PROMPT = BATCH * SEQ
N_SAMPLE = DEC_BATCH * DEC_SEQ
N_TOK = N_PROMPT + N_SAMPLE
TM = 256
TP = 512
N_TILES = N_TOK // TM
PROMPT_TILES = N_PROMPT // TM
BQ_S = 256
LOG2E = 1.4426950408889634
MOD_ROWS = 8

PAIRS_PER_GROUP = 6
N_BUCKETS = N_GROUPS * PAIRS_PER_GROUP
SLOT_A_LOCAL = (0, 2, 2, 3, 3, 3)
SLOT_B_LOCAL = (1, 1, 0, 0, 1, 2)
TME = 256
MOE_TILES = -(-(N_TOK + N_BUCKETS * (TME - 1)) // TME)
MOE_ROWS = MOE_TILES * TME

BUCKET_ROWS = 32
ROW_W = D_MODEL + LANES
STAGE_SLOTS = 3
GATHER_SLOTS = 3

AB_COLS = 1280
C_NORM_COLS = H_C * HD + KVH_C * HD
NORM_BLOCK = 256

BF16 = jnp.bfloat16
F32 = jnp.float32
VMEM_LIMIT = 52 * 1024 * 1024


def _mod_row(t, tile=TM):
    p_tiles = N_PROMPT // tile
    return jnp.where(t < p_tiles, 0, 1 + (t - p_tiles) // (DEC_SEQ // tile))


def _mod_spec(layer, chunk, tile=TM):
    return pl.BlockSpec((None, None, None, 1, D_MODEL), lambda t: (layer, _mod_row(t, tile), chunk, 0, 0))


def _const_spec(shape):
    nd = len(shape)
    return pl.BlockSpec(shape, lambda *_: (0,) * nd)


def _split_specs(tile=TM):
    p_tiles = N_PROMPT // tile
    return [
        pl.BlockSpec((tile, D_MODEL), lambda t: (jnp.minimum(t, p_tiles - 1), 0)),
        pl.BlockSpec((tile, D_MODEL), lambda t: (jnp.maximum(t - p_tiles, 0), 0)),
    ]


def _group_spec(tile, width, prompt):
    p_tiles = N_PROMPT // tile
    if prompt:
        return pl.BlockSpec((tile, width), lambda t: (jnp.minimum(t, p_tiles - 1), 0))
    return pl.BlockSpec((tile, width), lambda t: (jnp.maximum(t - p_tiles, 0), 0))


def _pick(t, a_ref, b_ref):
    return jnp.where(t < N_PROMPT // a_ref.shape[0], a_ref[...], b_ref[...])


def _sub_tiles(rows):
    return [slice(i * TM, (i + 1) * TM) for i in range(rows // TM)]


def _params(sem):
    return pltpu.CompilerParams(dimension_semantics=sem, vmem_limit_bytes=VMEM_LIMIT)


def _mod_kernel(cond_ref, w_ref, b_ref, o_ref):
    c = cond_ref[...]
    s = (c / (1.0 + jnp.exp(-c))).astype(BF16)
    o_ref[...] = jnp.dot(s, w_ref[...].astype(BF16), preferred_element_type=F32) + b_ref[...]


def _modulation(cond, w_mod, b_mod):
    tn = 1536
    return pl.pallas_call(
        _mod_kernel,
        grid=(DEPTH, 6 * D_MODEL // tn),
        in_specs=[
            pl.BlockSpec((MOD_ROWS, D_MODEL), lambda l, j: (0, 0)),
            pl.BlockSpec((None, D_MODEL, tn), lambda l, j: (l, 0, j)),
            pl.BlockSpec((None, 1, tn), lambda l, j: (l, 0, j)),
        ],
        out_specs=pl.BlockSpec((None, MOD_ROWS, tn), lambda l, j: (l, 0, j)),
        out_shape=jax.ShapeDtypeStruct((DEPTH, MOD_ROWS, 6 * D_MODEL), F32),
        compiler_params=_params(("arbitrary", "arbitrary")),
        name="modulation",
    )(cond, w_mod, b_mod.reshape(DEPTH, 1, 6 * D_MODEL))


def _grid_cos_sin(half):
    n_freq = half // 2
    rows = DEC_SEQ // GRID_W
    row = jnp.repeat(jnp.arange(rows, dtype=F32), GRID_W)
    col = jnp.tile(jnp.arange(GRID_W, dtype=F32), rows)
    inv = 1.0 / (ROPE_THETA ** (jnp.arange(n_freq, dtype=F32) / n_freq))
    ang = jnp.concatenate([row[:, None] * inv, col[:, None] * inv], axis=-1)
    return jnp.cos(ang), jnp.sin(ang)


def _rope_tables():
    tabs = []
    for half in (HD // 2, ROPE_A // 2):
        cos, sin = _grid_cos_sin(half)
        reps = LANES // (2 * half)
        tabs.append(jnp.tile(jnp.concatenate([cos, cos], axis=-1), (1, reps)))
        tabs.append(jnp.tile(jnp.concatenate([-sin, sin], axis=-1), (1, reps)))
    half = ROPE_A // 2
    one = jnp.ones((DEC_SEQ, NOPE_A), F32)
    zero = jnp.zeros((DEC_SEQ, NOPE_A), F32)
    pad1 = jnp.ones((DEC_SEQ, LANES - NOPE_A - ROPE_A), F32)
    pad0 = jnp.zeros((DEC_SEQ, LANES - NOPE_A - ROPE_A), F32)
    zh = jnp.zeros((DEC_SEQ, half), F32)
    tabs.append(jnp.concatenate([one, cos, cos, pad1], axis=-1))
    tabs.append(jnp.concatenate([zero, -sin, zh, pad0], axis=-1))
    tabs.append(jnp.concatenate([zero, zh, sin, pad0], axis=-1))
    return tuple(tabs)


def _rope(x, c, s, half):
    w = x.shape[1]
    reps = w // LANES
    if reps > 1:
        c = jnp.concatenate([c] * reps, axis=1)
        s = jnp.concatenate([s] * reps, axis=1)
    ahead = pltpu.roll(x, w - half, 1)
    behind = pltpu.roll(x, half, 1)
    lane = lax.broadcasted_iota(jnp.int32, x.shape, 1)
    swapped = jnp.where((lane & (2 * half - 1)) < half, ahead, behind)
    return x * c + swapped * s


def _rope_specs(n, tile):
    p_tiles = N_PROMPT // tile

    def idx(t):
        return (jnp.maximum(t - p_tiles, 0) % (DEC_SEQ // tile), 0)
    return [pl.BlockSpec((tile, LANES), idx)] * n


def _rms(x, g_ref):
    return x * lax.rsqrt(jnp.mean(x * x, axis=-1, keepdims=True) + EPS) * g_ref[...]


def _inproj_ab_kernel(xp_ref, xs_ref, shift_ref, scale_ref, w_in_ref, qn_ref, kvn_ref, w_uq_ref, w_kn_ref, w_v_ref,
                      place_ref, c64_ref, s64_ref, c32_ref, s32_ref, ca_ref, sa_ref, sb_ref,
                      qa_ref, ka_ref, va_ref, qb_ref, ckv_p_ref, kpe_p_ref, kbt_p_ref, vbt_p_ref, kb_s_ref, vb_s_ref):
    t = pl.program_id(0)
    x = _pick(t, xp_ref, xs_ref)
    subs = _sub_tiles(x.shape[0])
    h = [(x[s] * (1.0 + scale_ref[...]) + shift_ref[...]).astype(BF16) for s in subs]
    z = [jnp.dot(h_s, w_in_ref[...], preferred_element_type=F32) for h_s in h]
    ql = [_rms(z_s[:, :Q_LORA], qn_ref).astype(BF16) for z_s in z]
    qa = [jnp.dot(ql_s, w_uq_ref[...], preferred_element_type=F32) for ql_s in ql]
    ckv = [_rms(z_s[:, Q_LORA:Q_LORA + KV_LORA], kvn_ref) for z_s in z]
    k_nope = [jnp.dot(c_s.astype(BF16), w_kn_ref[...], preferred_element_type=F32) for c_s in ckv]
    va = [jnp.dot(c_s.astype(BF16), w_v_ref[...], preferred_element_type=F32).astype(BF16) for c_s in ckv]
    for s, va_s in zip(subs, va):
        va_ref[s, :] = va_s
    qb = [z_s[:, 384:896] for z_s in z]
    kb = [z_s[:, 896:1024] for z_s in z]
    vb = [z_s[:, 1024:1152] for z_s in z]
    kpe = [z_s[:, 1152:1280] for z_s in z]

    def finish(qa_out, qb_out, kpe_out):
        k_rope = [jnp.dot(kpe_s.astype(BF16), place_ref[...], preferred_element_type=F32) for kpe_s in kpe_out]
        for i, s in enumerate(subs):
            qa_ref[s, :] = qa_out[i].astype(BF16)
            qb_ref[s, :] = qb_out[i].astype(BF16)
            ka_ref[s, :] = (k_nope[i] + k_rope[i]).astype(BF16)

    p_tiles = N_PROMPT // x.shape[0]

    @pl.when(t < p_tiles)
    def _():
        finish(qa, qb, kpe)
        for i, s in enumerate(subs):
            ckv_p_ref[s, :] = ckv[i]
            kpe_p_ref[s, :] = kpe[i][:, :ROPE_A]
            kbt_p_ref[i] = kb[i].T
            vbt_p_ref[i] = vb[i].T

    @pl.when(t >= p_tiles)
    def _():
        w = qa[0].shape[1]
        reps = w // LANES
        half = ROPE_A // 2
        qa_rot, qb_rot, kb_rot, kpe_rot = [], [], [], []
        for i, s in enumerate(subs):
            ca = jnp.concatenate([ca_ref[s, :]] * reps, axis=1)
            sa = jnp.concatenate([sa_ref[s, :]] * reps, axis=1)
            sb = jnp.concatenate([sb_ref[s, :]] * reps, axis=1)
            qa_rot.append(qa[i] * ca + pltpu.roll(qa[i], w - half, 1) * sa + pltpu.roll(qa[i], half, 1) * sb)
            qb_rot.append(_rope(qb[i], c64_ref[s, :], s64_ref[s, :], HD // 2))
            kb_rot.append(_rope(kb[i], c64_ref[s, :], s64_ref[s, :], HD // 2))
            kpe_rot.append(_rope(kpe[i], c32_ref[s, :], s32_ref[s, :], ROPE_A // 2))
        finish(qa_rot, qb_rot, kpe_rot)
        for i, s in enumerate(subs):
            kb_s_ref[s, :] = kb_rot[i]
            vb_s_ref[s, :] = vb[i]


def _mla_weights(w_uq, w_ukv):
    pad = LANES - NOPE_A - ROPE_A
    uq = w_uq.reshape(Q_LORA, H_A, NOPE_A + ROPE_A)
    uq = jnp.pad(uq, ((0, 0), (0, 0), (0, pad))).reshape(Q_LORA, H_A * LANES)
    ukv = w_ukv.reshape(KV_LORA, H_A, NOPE_A + V_A)
    kn = jnp.pad(ukv[:, :, :NOPE_A], ((0, 0), (0, 0), (0, LANES - NOPE_A))).reshape(KV_LORA, H_A * LANES)
    wv = ukv[:, :, NOPE_A:].reshape(KV_LORA, H_A * V_A)
    src = np.arange(LANES)[:, None]
    dst = np.arange(H_A * LANES)[None, :] % LANES
    place = (src < ROPE_A) & (dst == src + NOPE_A)
    return uq.astype(BF16), kn.astype(BF16), wv.astype(BF16), jnp.asarray(place, BF16)


def _inproj_ab(xp, xs, mod, w_in, q_norm, kv_norm, w_uq, w_kn, w_v, place, tabs):
    widths = [(H_A * LANES, BF16), (H_A * LANES, BF16), (H_A * V_A, BF16), (H_B * HD, BF16)]
    out_specs = [pl.BlockSpec((TP, w), lambda t: (t, 0)) for w, _ in widths]
    out_shape = [jax.ShapeDtypeStruct((N_TOK, w), dt) for w, dt in widths]
    for w in (KV_LORA, ROPE_A):
        out_specs.append(_group_spec(TP, w, prompt=True))
        out_shape.append(jax.ShapeDtypeStruct((N_PROMPT, w), F32))
    assert SEQ == TM
    p_tiles = N_PROMPT // TP
    for _ in range(2):
        out_specs.append(pl.BlockSpec((TP // SEQ, KVH_B * HD, SEQ), lambda t: (jnp.minimum(t, p_tiles - 1), 0, 0)))
        out_shape.append(jax.ShapeDtypeStruct((BATCH, KVH_B * HD, SEQ), F32))
    for w in (KVH_B * HD, KVH_B * HD):
        out_specs.append(_group_spec(TP, w, prompt=False))
        out_shape.append(jax.ShapeDtypeStruct((N_SAMPLE, w), F32))
    return pl.pallas_call(
        _inproj_ab_kernel,
        grid=(N_TOK // TP,),
        in_specs=_split_specs(TP) + [_mod_spec(0, 0, TP), _mod_spec(0, 1, TP), _const_spec((D_MODEL, AB_COLS)),
                                     _const_spec((1, Q_LORA)), _const_spec((1, KV_LORA)),
                                     _const_spec((Q_LORA, H_A * LANES)), _const_spec((KV_LORA, H_A * LANES)),
                                     _const_spec((KV_LORA, H_A * V_A)), _const_spec((LANES, H_A * LANES))]
        + _rope_specs(7, TP),
        out_specs=out_specs,
        out_shape=out_shape,
        compiler_params=_params(("arbitrary",)),
        name="inproj_ab",
    )(xp, xs, mod, mod, w_in, q_norm, kv_norm, w_uq, w_kn, w_v, place, *tabs)


def _inproj_c_kernel(x_ref, shift_ref, scale_ref, w_in_ref, g_ref, ones_ref, c64_ref, s64_ref,
                     q_ref, k_p_ref, v_p_ref, k_s_ref, v_s_ref):
    t = pl.program_id(0)
    subs = _sub_tiles(x_ref.shape[0])
    h = [(x_ref[s, :] * (1.0 + scale_ref[...]) + shift_ref[...]).astype(BF16) for s in subs]
    z = [jnp.dot(h_s, w_in_ref[...], preferred_element_type=F32) for h_s in h]
    qk = [z_s[:, :C_NORM_COLS] for z_s in z]
    sq = [qk_s * qk_s for qk_s in qk]
    sq_hi = [sq_s.astype(BF16) for sq_s in sq]
    sq_lo = [(sq_s - hi_s.astype(F32)).astype(BF16) for sq_s, hi_s in zip(sq, sq_hi)]
    ss = []
    for hi_s, lo_s in zip(sq_hi, sq_lo):
        blocks = []
        for j in range(C_NORM_COLS // NORM_BLOCK):
            sl = slice(j * NORM_BLOCK, (j + 1) * NORM_BLOCK)
            blocks.append(jnp.dot(hi_s[:, sl], ones_ref[...], preferred_element_type=F32)
                          + jnp.dot(lo_s[:, sl], ones_ref[...], preferred_element_type=F32))
        ss.append(jnp.concatenate(blocks, axis=1))
    qk = [qk_s * lax.rsqrt(ss_s * (1.0 / HD) + EPS) * g_ref[...] for qk_s, ss_s in zip(qk, ss)]
    p_tiles = N_PROMPT // x_ref.shape[0]

    @pl.when(t < p_tiles)
    def _():
        for s, qk_s, z_s in zip(subs, qk, z):
            q_ref[s, :] = qk_s[:, :H_C * HD].astype(BF16)
            k_p_ref[s, :] = qk_s[:, H_C * HD:]
            v_p_ref[s, :] = z_s[:, C_NORM_COLS:]

    @pl.when(t >= p_tiles)
    def _():
        for s, qk_s, z_s in zip(subs, qk, z):
            q_ref[s, :] = _rope(qk_s[:, :H_C * HD], c64_ref[s, :], s64_ref[s, :], HD // 2).astype(BF16)
            k_s_ref[s, :] = _rope(qk_s[:, H_C * HD:], c64_ref[s, :], s64_ref[s, :], HD // 2)
            v_s_ref[s, :] = z_s[:, C_NORM_COLS:]


def _inproj_c(x, mod, w_in, g_full, tabs):
    def tok(w):
        return pl.BlockSpec((TP, w), lambda t: (t, 0))
    kv_w = KVH_C * HD
    head = np.arange(NORM_BLOCK) // HD
    ones = jnp.asarray(head[:, None] == head[None, :], BF16)
    return pl.pallas_call(
        _inproj_c_kernel,
        grid=(N_TOK // TP,),
        in_specs=[tok(D_MODEL), _mod_spec(1, 0, TP), _mod_spec(1, 1, TP),
                  _const_spec((D_MODEL, C_NORM_COLS + KVH_C * HD)),
                  _const_spec((1, C_NORM_COLS)), _const_spec((NORM_BLOCK, NORM_BLOCK))] + _rope_specs(2, TP),
        out_specs=[tok(H_C * HD), _group_spec(TP, kv_w, True), _group_spec(TP, kv_w, True),
                   _group_spec(TP, kv_w, False), _group_spec(TP, kv_w, False)],
        out_shape=[jax.ShapeDtypeStruct((N_TOK, H_C * HD), BF16),
                   jax.ShapeDtypeStruct((N_PROMPT, kv_w), F32), jax.ShapeDtypeStruct((N_PROMPT, kv_w), F32),
                   jax.ShapeDtypeStruct((N_SAMPLE, kv_w), F32), jax.ShapeDtypeStruct((N_SAMPLE, kv_w), F32)],
        compiler_params=_params(("arbitrary",)),
        name="inproj_c",
    )(x, mod, mod, w_in, g_full, ones, *tabs[:2])


def _qk(q, k):
    return lax.dot_general(q, k, (((1,), (1,)), ((), ())), preferred_element_type=F32)


def _softmax_pv(scores, values, scale, sink=None, kv_t=False):
    c = scale * LOG2E
    m = scores[0].max(axis=-1, keepdims=True)
    for s in scores[1:]:
        m = jnp.maximum(m, s.max(axis=-1, keepdims=True))
    m = m * c
    if sink is not None:
        sink = sink * LOG2E
        m = jnp.maximum(m, sink)
    den = None
    out = None
    for s, v in zip(scores, values):
        p = jnp.exp2(s * c - m)
        d = p.sum(axis=-1, keepdims=True)
        o = _qk(p.astype(BF16), v) if kv_t else jnp.dot(p.astype(BF16), v, preferred_element_type=F32)
        den = d if den is None else den + d
        out = o if out is None else out + o
    if sink is not None:
        den = den + jnp.exp2(sink - m)
    return out * (1.0 / den)


def _mla_heads(qa_ref, ka_refs, va_refs, o_ref):
    scale = (NOPE_A + ROPE_A) ** -0.5
    for h0 in range(0, H_A, 2):
        pair = []
        for h in (h0, h0 + 1):
            q = qa_ref[:, h * LANES:(h + 1) * LANES]
            scores = [_qk(q, ka_ref[:, h * LANES:(h + 1) * LANES]) for ka_ref in ka_refs]
            values = [va_ref[:, h * V_A:(h + 1) * V_A] for va_ref in va_refs]
            pair.append(_softmax_pv(scores, values, scale))
        o_ref[:, h0 * V_A:(h0 + 2) * V_A] = jnp.concatenate(pair, axis=1).astype(o_ref.dtype)


def _gqa_group(q_ref, col0, n_q, ks, vs, mask=None, sink=None, kv_t=False):
    rows = q_ref.shape[0]
    q = jnp.concatenate([q_ref[:, col0 + g * HD:col0 + (g + 1) * HD] for g in range(n_q)], axis=0)
    if kv_t:
        scores = [jnp.dot(q, k, preferred_element_type=F32) for k in ks]
    else:
        scores = [_qk(q, k) for k in ks]
    if mask is not None:
        scores[0] = jnp.where(mask, scores[0], NEG_INF)
    o = _softmax_pv(scores, vs, HD ** -0.5, sink, kv_t)
    return jnp.concatenate([o[g * rows:(g + 1) * rows] for g in range(n_q)], axis=1)


def _stacked_sink(sink_ref, h0, n_q, rows):
    head = lax.broadcasted_iota(jnp.int32, (n_q * rows, 1), 0) // rows
    col = jnp.full((n_q * rows, 1), sink_ref[h0], F32)
    for g in range(1, n_q):
        col = jnp.where(head == g, sink_ref[h0 + g], col)
    return col


def _attn_ab_prompt_kernel(sink_ref, qa_ref, ka_ref, va_ref, qb_ref, kbt_ref, vbt_ref, o_ref):
    _mla_heads(qa_ref, [ka_ref], [va_ref], o_ref)
    kbt = kbt_ref[...].astype(BF16)
    vbt = vbt_ref[...].astype(BF16)
    gq = H_B // KVH_B
    for kh in range(KVH_B):
        sl = slice(kh * HD, (kh + 1) * HD)
        o = _gqa_group(qb_ref, kh * gq * HD, gq, [kbt[sl, :]], [vbt[sl, :]],
                       sink=_stacked_sink(sink_ref, kh * gq, gq, SEQ), kv_t=True)
        col = H_A * V_A + kh * gq * HD
        o_ref[:, col:col + gq * HD] = o.astype(o_ref.dtype)


def _attn_ab_prompt(sink, qa, ka, va, qb, kbt, vbt):
    def blk(w):
        return pl.BlockSpec((SEQ, w), lambda b: (b, 0))
    tblk = pl.BlockSpec((None, KVH_B * HD, SEQ), lambda b: (b, 0, 0))
    return pl.pallas_call(
        _attn_ab_prompt_kernel,
        grid=(BATCH,),
        in_specs=[pl.BlockSpec(memory_space=pltpu.SMEM), blk(H_A * LANES), blk(H_A * LANES), blk(H_A * V_A),
                  blk(H_B * HD), tblk, tblk],
        out_specs=blk(D_MODEL),
        out_shape=jax.ShapeDtypeStruct((N_PROMPT, H_A * V_A + H_B * HD), BF16),
        compiler_params=_params(("arbitrary",)),
        name="attn_ab_prompt",
    )(sink, qa, ka, va, qb, kbt, vbt)


def _attn_ab_sample_kernel(sink_ref, qa_ref, ka_ref, va_ref, qb_ref, kb_ref, vb_ref,
                           ckv_ctx_ref, kpe_ctx_ref, kb_ctx_ref, vb_ctx_ref, w_kn_ref, w_v_ref, place_ref,
                           o_ref, ka_ctx_ref, va_ctx_ref):
    i = pl.program_id(1)

    @pl.when(i == 0)
    def _():
        ckv16 = ckv_ctx_ref[...].astype(BF16)
        k_nope = jnp.dot(ckv16, w_kn_ref[...], preferred_element_type=F32)
        k_rope = jnp.dot(kpe_ctx_ref[...].astype(BF16), place_ref[...], preferred_element_type=F32)
        ka_ctx_ref[...] = (k_nope + k_rope).astype(BF16)
        va_ctx_ref[...] = jnp.dot(ckv16, w_v_ref[...], preferred_element_type=F32).astype(BF16)

    _mla_heads(qa_ref, [ka_ref, ka_ctx_ref], [va_ref, va_ctx_ref], o_ref)

    gq = H_B // KVH_B
    n_win = BQ_S + 2 * WINDOW
    start = pl.multiple_of(jnp.clip(i * BQ_S - WINDOW, 0, DEC_SEQ - n_win), WINDOW)
    row = lax.broadcasted_iota(jnp.int32, (gq * BQ_S, n_win), 0)
    qpos = i * BQ_S + (row & (BQ_S - 1))
    kpos = start + lax.broadcasted_iota(jnp.int32, (gq * BQ_S, n_win), 1)
    in_band = jnp.abs(qpos - kpos) <= WINDOW
    kwin = kb_ref[pl.ds(start, n_win), :].astype(BF16)
    vwin = vb_ref[pl.ds(start, n_win), :].astype(BF16)
    kctx = kb_ctx_ref[...].astype(BF16)
    vctx = vb_ctx_ref[...].astype(BF16)
    for kh in range(KVH_B):
        sl = slice(kh * HD, (kh + 1) * HD)
        o = _gqa_group(qb_ref, kh * gq * HD, gq, [kwin[:, sl], kctx[:, sl]], [vwin[:, sl], vctx[:, sl]],
                       mask=in_band, sink=_stacked_sink(sink_ref, kh * gq, gq, BQ_S))
        col = H_A * V_A + kh * gq * HD
        o_ref[:, col:col + gq * HD] = o.astype(o_ref.dtype)


def _attn_ab_sample(sink, qa, ka, va, qb, kb, vb, ckv_ctx, kpe_ctx, kb_ctx, vb_ctx, w_kn, w_v, place):
    qoff = N_PROMPT // BQ_S
    nq = DEC_SEQ // BQ_S
    boff = N_PROMPT // DEC_SEQ

    def qblk(w):
        return pl.BlockSpec((BQ_S, w), lambda b, i: (qoff + b * nq + i, 0))

    def bblk(w):
        return pl.BlockSpec((DEC_SEQ, w), lambda b, i: (boff + b, 0))

    def sblk(w):
        return pl.BlockSpec((DEC_SEQ, w), lambda b, i: (b, 0))

    def cblk(w):
        return pl.BlockSpec((PAST_LEN, w), lambda b, i: (b, 0))

    return pl.pallas_call(
        _attn_ab_sample_kernel,
        grid=(DEC_BATCH, nq),
        in_specs=[pl.BlockSpec(memory_space=pltpu.SMEM), qblk(H_A * LANES), bblk(H_A * LANES), bblk(H_A * V_A),
                  qblk(H_B * HD), sblk(KVH_B * HD), sblk(KVH_B * HD),
                  cblk(KV_LORA), cblk(LANES), cblk(KVH_B * HD), cblk(KVH_B * HD),
                  _const_spec((KV_LORA, H_A * LANES)), _const_spec((KV_LORA, H_A * V_A)),
                  _const_spec((LANES, H_A * LANES))],
        out_specs=pl.BlockSpec((BQ_S, D_MODEL), lambda b, i: (b * nq + i, 0)),
        out_shape=jax.ShapeDtypeStruct((N_SAMPLE, H_A * V_A + H_B * HD), BF16),
        scratch_shapes=[pltpu.VMEM((PAST_LEN, H_A * LANES), BF16), pltpu.VMEM((PAST_LEN, H_A * V_A), BF16)],
        compiler_params=_params(("arbitrary", "arbitrary")),
        name="attn_ab_sample",
    )(sink, qa, ka, va, qb, kb, vb, ckv_ctx, kpe_ctx, kb_ctx, vb_ctx, w_kn, w_v, place)


def _gqa_heads(q_ref, k_list, v_list, o_ref):
    gq = H_C // KVH_C
    for kh in range(KVH_C):
        sl = slice(kh * HD, (kh + 1) * HD)
        o = _gqa_group(q_ref, kh * gq * HD, gq, [k[:, sl] for k in k_list], [v[:, sl] for v in v_list])
        o_ref[:, kh * gq * HD:(kh + 1) * gq * HD] = o.astype(o_ref.dtype)


def _attn_c_prompt_kernel(q_ref, k_ref, v_ref, o_ref):
    _gqa_heads(q_ref, [k_ref[...].astype(BF16)], [v_ref[...].astype(BF16)], o_ref)


def _attn_c_prompt(q, k, v):
    def blk(w):
        return pl.BlockSpec((SEQ, w), lambda b: (b, 0))
    return pl.pallas_call(
        _attn_c_prompt_kernel,
        grid=(BATCH,),
        in_specs=[blk(H_C * HD), blk(KVH_C * HD), blk(KVH_C * HD)],
        out_specs=blk(H_C * HD),
        out_shape=jax.ShapeDtypeStruct((N_PROMPT, H_C * HD), BF16),
        compiler_params=_params(("arbitrary",)),
        name="attn_c_prompt",
    )(q, k, v)


def _attn_c_sample_kernel(q_ref, k_ref, v_ref, kc_ref, vc_ref, o_ref):
    _gqa_heads(q_ref, [k_ref[...].astype(BF16), kc_ref[...].astype(BF16)],
               [v_ref[...].astype(BF16), vc_ref[...].astype(BF16)], o_ref)


def _attn_c_sample(q, k, v, k_ctx, v_ctx):
    qoff = N_PROMPT // BQ_S
    nq = DEC_SEQ // BQ_S
    w = KVH_C * HD
    return pl.pallas_call(
        _attn_c_sample_kernel,
        grid=(DEC_BATCH, nq),
        in_specs=[pl.BlockSpec((BQ_S, H_C * HD), lambda b, i: (qoff + b * nq + i, 0)),
                  pl.BlockSpec((DEC_SEQ, w), lambda b, i: (b, 0)),
                  pl.BlockSpec((DEC_SEQ, w), lambda b, i: (b, 0)),
                  pl.BlockSpec((PAST_LEN, w), lambda b, i: (b, 0)),
                  pl.BlockSpec((PAST_LEN, w), lambda b, i: (b, 0))],
        out_specs=pl.BlockSpec((BQ_S, H_C * HD), lambda b, i: (b * nq + i, 0)),
        out_shape=jax.ShapeDtypeStruct((N_SAMPLE, H_C * HD), BF16),
        compiler_params=_params(("arbitrary", "arbitrary")),
        name="attn_c_sample",
    )(q, k, v, k_ctx, v_ctx)


def _layer_norm(y, g, b):
    mu = jnp.mean(y, axis=-1, keepdims=True)
    yc = y - mu
    var = jnp.mean(yc * yc, axis=-1, keepdims=True)
    return yc * lax.rsqrt(var + EPS) * g + b


def _route(sel, aff):
    def row(a, j):
        return a[j:j + 1, :]

    scores = []
    for g in range(N_GROUPS):
        a0, a1, a2, a3 = (row(sel, 4 * g + j) for j in range(4))
        hi01, lo01 = jnp.maximum(a0, a1), jnp.minimum(a0, a1)
        hi23, lo23 = jnp.maximum(a2, a3), jnp.minimum(a2, a3)
        top1 = jnp.maximum(hi01, hi23)
        top2 = jnp.maximum(jnp.minimum(hi01, hi23), jnp.maximum(lo01, lo23))
        scores.append(top1 + top2)
    best = scores[0]
    gi = jnp.zeros(best.shape, jnp.int32)
    for g in range(1, N_GROUPS):
        better = scores[g] > best
        gi = jnp.where(better, g, gi)
        best = jnp.where(better, scores[g], best)

    def in_group(a, j):
        out = row(a, j)
        for g in range(1, N_GROUPS):
            out = jnp.where(gi == g, row(a, 4 * g + j), out)
        return out

    v = [in_group(sel, j) for j in range(4)]
    a = [in_group(aff, j) for j in range(4)]
    chosen = []
    for j in range(4):
        rank = jnp.zeros(best.shape, jnp.int32)
        for k in range(4):
            if k == j:
                continue
            ahead = (v[k] >= v[j]) if k < j else (v[k] > v[j])
            rank = rank + ahead.astype(jnp.int32)
        chosen.append(rank < 2)
    total = sum(jnp.where(chosen[j], a[j], 0.0) for j in range(4))
    w = [jnp.where(chosen[j], a[j], 0.0) / total for j in range(4)]
    pair = jnp.zeros(best.shape, jnp.int32)
    for p, (ja, jb) in enumerate(zip(SLOT_A_LOCAL, SLOT_B_LOCAL)):
        pair = jnp.where(chosen[ja] & chosen[jb], p, pair)
    gate_a = jnp.where(pair == 0, w[0], jnp.where(pair <= 2, w[2], w[3]))
    gate_b = jnp.where((pair == 0) | (pair == 1) | (pair == 4), w[1], jnp.where(pair == 5, w[2], w[0]))
    return gi * PAIRS_PER_GROUP + pair, gate_a, gate_b


def _post_attn_kernel(split_x, *refs):
    t = pl.program_id(0)
    op_ref, os_ref = refs[:2]
    if split_x:
        x = _pick(t, refs[2], refs[3])
        refs = refs[4:]
    else:
        x = refs[2][...]
        refs = refs[3:]
    (w_out_ref, gate_ref, shift_ref, scale_ref, lng_ref, lnb_ref, rw_ref, rb_ref, tri_ref,
     x1_ref, row_ref, route_ref, counts_ref, carry_ref) = refs
    o = _pick(t, op_ref, os_ref)
    subs = _sub_tiles(o.shape[0])

    a = [jnp.dot(o[s], w_out_ref[...], preferred_element_type=F32) for s in subs]
    x1 = [_layer_norm(ALPHA * x[s] + gate_ref[...] * a_s, lng_ref[...], lnb_ref[...]) for s, a_s in zip(subs, a)]
    h2 = [x1_s * (1.0 + scale_ref[...]) + shift_ref[...] for x1_s in x1]
    for s, x1_s, h2_s in zip(subs, x1, h2):
        x1_ref[s, :] = x1_s
        row_ref[s, :D_MODEL] = h2_s
    rw = rw_ref[...]
    rw_hi = rw.astype(BF16)
    rw_lo = (rw - rw_hi.astype(F32)).astype(BF16)
    rw_both = jnp.concatenate([rw_hi, rw_lo], axis=0)
    logits = []
    for h2_s in h2:
        h2_hi = h2_s.astype(BF16)
        h2_lo = (h2_s - h2_hi.astype(F32)).astype(BF16)
        by_hi = _qk(rw_both, h2_hi)
        logits.append(by_hi[:N_EXPERTS] + by_hi[N_EXPERTS:] + _qk(rw_hi, h2_lo))
    aff = [1.0 / (1.0 + jnp.exp(-l)) for l in logits]
    routed = [_route(aff_s + rb_ref[...], aff_s) for aff_s in aff]

    lane = lax.broadcasted_iota(jnp.int32, (LANES, TM), 0)
    for s, (_, gate_a, gate_b) in zip(subs, routed):
        meta_t = jnp.where(lane == 0, gate_a, jnp.where(lane == 1, gate_b, 0.0))
        row_ref[s, D_MODEL:] = meta_t.T

    @pl.when(t == 0)
    def _():
        carry_ref[...] = jnp.zeros_like(carry_ref)

    onehots = [lax.broadcasted_iota(jnp.int32, (BUCKET_ROWS, TM), 0) == bucket for bucket, _, _ in routed]
    prefixes = [jnp.dot(oh.astype(BF16), tri_ref[...], preferred_element_type=F32) for oh in onehots]
    carry = carry_ref[...]
    route_ref[...] = jnp.zeros(route_ref.shape, F32)
    for s, (bucket, _, _), onehot, prefix in zip(subs, routed, onehots, prefixes):
        rank = jnp.sum(jnp.where(onehot, prefix + carry[:, 0:1], 0.0), axis=0, keepdims=True)
        carry = carry + jnp.sum(onehot.astype(F32), axis=1, keepdims=True)
        route_ref[0:1, s] = bucket.astype(F32)
        route_ref[1:2, s] = rank
    carry_ref[...] = carry
    counts_ref[...] = carry


def _post_attn(layer, o_p, o_s, x_list, mod, w_out, ln_g, ln_b, rw_t, rb):
    split_x = len(x_list) == 2
    tok = pl.BlockSpec((TP, D_MODEL), lambda t: (t, 0))
    tri = jnp.asarray(np.arange(TM)[:, None] < np.arange(TM)[None, :], BF16)
    return pl.pallas_call(
        functools.partial(_post_attn_kernel, split_x),
        grid=(N_TOK // TP,),
        in_specs=_split_specs(TP) + (_split_specs(TP) if split_x else [tok]) + [
            _const_spec((D_MODEL, D_MODEL)), _mod_spec(layer, 2, TP), _mod_spec(layer, 3, TP),
            _mod_spec(layer, 4, TP),
            _const_spec((1, D_MODEL)), _const_spec((1, D_MODEL)), _const_spec((N_EXPERTS, D_MODEL)),
            _const_spec((N_EXPERTS, 1)), _const_spec((TM, TM))],
        out_specs=[tok, pl.BlockSpec((TP, ROW_W), lambda t: (t, 0)), pl.BlockSpec((8, TP), lambda t: (0, t)),
                   _const_spec((BUCKET_ROWS, LANES))],
        out_shape=[jax.ShapeDtypeStruct((N_TOK, D_MODEL), F32), jax.ShapeDtypeStruct((N_TOK, ROW_W), F32),
                   jax.ShapeDtypeStruct((8, N_TOK), F32), jax.ShapeDtypeStruct((BUCKET_ROWS, LANES), F32)],
        scratch_shapes=[pltpu.VMEM((BUCKET_ROWS, LANES), F32)],
        compiler_params=_params(("arbitrary",)),
        name="post_attn",
    )(o_p, o_s, *x_list, w_out, mod, mod, mod, ln_g, ln_b, rw_t, rb, tri)


def _moe_kernel(ta_ref, tb_ref, nt_ref, x_ref, wga_ref, wua_ref, wda_ref, wgb_ref, wub_ref, wdb_ref, o_ref,
                wga_s, wua_s, wda_s, wgb_s, wub_s, wdb_s):
    i = pl.program_id(0)
    prev = jnp.maximum(i - 1, 0)
    slots = ((ta_ref, (wga_ref, wua_ref, wda_ref), (wga_s, wua_s, wda_s)),
             (tb_ref, (wgb_ref, wub_ref, wdb_ref), (wgb_s, wub_s, wdb_s)))

    for t_ref, w_refs, w_scr in slots:
        @pl.when((i == 0) | (t_ref[i] != t_ref[prev]))
        def _(w_refs=w_refs, w_scr=w_scr):
            for w_ref, s_ref in zip(w_refs, w_scr):
                s_ref[...] = w_ref[...].astype(BF16)

    @pl.when(i < nt_ref[0])
    def _():
        x = x_ref[:, :D_MODEL].astype(BF16)
        acc = None
        for slot, (_, _, (wg_s, wu_s, wd_s)) in enumerate(slots):
            gt = jnp.dot(x, wg_s[...], preferred_element_type=F32)
            up = jnp.dot(x, wu_s[...], preferred_element_type=F32)
            gate = x_ref[:, D_MODEL + slot:D_MODEL + slot + 1]
            hid = (gt / (1.0 + jnp.exp(-gt))) * up * gate
            y = jnp.dot(hid.astype(BF16), wd_s[...], preferred_element_type=F32)
            acc = y if acc is None else acc + y
        o_ref[...] = acc

    @pl.when(i >= nt_ref[0])
    def _():
        o_ref[...] = jnp.zeros_like(o_ref)


def _moe(layer, tile_a, tile_b, n_tiles, rows, w_gate, w_up, w_down):
    def row_idx(i, ta, tb, nt):
        return (jnp.maximum(jnp.minimum(i, nt[0] - 1), 0), 0)

    def wspec(shape, which):
        if which == 0:
            return pl.BlockSpec((None, None) + shape, lambda i, ta, tb, nt: (layer, ta[i], 0, 0))
        return pl.BlockSpec((None, None) + shape, lambda i, ta, tb, nt: (layer, tb[i], 0, 0))

    up_shape, down_shape = (D_MODEL, D_EXPERT), (D_EXPERT, D_MODEL)
    grid_spec = pltpu.PrefetchScalarGridSpec(
        num_scalar_prefetch=3,
        grid=(MOE_TILES,),
        in_specs=[pl.BlockSpec((TME, ROW_W), row_idx),
                  wspec(up_shape, 0), wspec(up_shape, 0), wspec(down_shape, 0),
                  wspec(up_shape, 1), wspec(up_shape, 1), wspec(down_shape, 1)],
        out_specs=pl.BlockSpec((TME, D_MODEL), lambda i, ta, tb, nt: (i, 0)),
        scratch_shapes=[pltpu.VMEM(s, BF16) for s in (up_shape, up_shape, down_shape) * 2],
    )
    return pl.pallas_call(
        _moe_kernel,
        grid_spec=grid_spec,
        out_shape=jax.ShapeDtypeStruct((MOE_ROWS, D_MODEL), F32),
        compiler_params=_params(("arbitrary",)),
        name="moe_experts",
    )(tile_a, tile_b, n_tiles, rows, w_gate, w_up, w_down, w_gate, w_up, w_down)


def _row_copy(src_ref, src_row, dst_ref, dst_row, sem):
    return pltpu.make_async_copy(src_ref.at[pl.ds(src_row, 1), :], dst_ref.at[pl.ds(dst_row, 1), :], sem)


def _dispatch_kernel(pos_ref, pend_ref, cnt_ref, nt_ref, src_ref, out_ref, buf, zero_ref, in_sem, sem, zsem):
    t = pl.program_id(0)

    def fetch(tile):
        return pltpu.make_async_copy(src_ref.at[pl.ds(pl.multiple_of(tile * TM, TM), TM), :],
                                     buf.at[tile % STAGE_SLOTS], in_sem.at[tile % STAGE_SLOTS])

    @pl.when(t == 0)
    def _():
        fetch(0).start()
        zero_ref[...] = jnp.zeros_like(zero_ref)

        def zero_tile(row0):
            return pltpu.make_async_copy(zero_ref, out_ref.at[pl.ds(pl.multiple_of(row0, TME), TME), :], zsem)

        for b in range(N_BUCKETS):
            @pl.when(cnt_ref[b] > 0)
            def _(b=b):
                zero_tile(pend_ref[b] - TME).start()

        def start_unused(i, carry):
            zero_tile(i * TME).start()
            return carry
        lax.fori_loop(nt_ref[0], MOE_TILES, start_unused, 0)

        for b in range(N_BUCKETS):
            @pl.when(cnt_ref[b] > 0)
            def _(b=b):
                zero_tile(pend_ref[b] - TME).wait()

        def wait_unused(i, carry):
            zero_tile(i * TME).wait()
            return carry
        lax.fori_loop(nt_ref[0], MOE_TILES, wait_unused, 0)

    @pl.when(t + 1 < N_TILES)
    def _():
        fetch(t + 1).start()

    fetch(t).wait()
    tile_ref = buf.at[t % STAGE_SLOTS]

    def issue(i, carry):
        for k in range(2):
            r = 2 * i + k
            _row_copy(tile_ref, r, out_ref, pos_ref[t * TM + r], sem.at[t % 2]).start(priority=k)
        return carry
    lax.fori_loop(0, TM // 2, issue, 0, unroll=4)

    def drain(tile):
        pltpu.make_async_copy(buf.at[0], out_ref.at[pl.ds(0, TM), :], sem.at[tile % 2]).wait()

    @pl.when(t > 0)
    def _():
        drain(t - 1)

    @pl.when(t == N_TILES - 1)
    def _():
        drain(t)


def _dispatch(pos, pend, counts, n_tiles, rows):
    return pl.pallas_call(
        _dispatch_kernel,
        grid_spec=pltpu.PrefetchScalarGridSpec(
            num_scalar_prefetch=4, grid=(N_TILES,),
            in_specs=[pl.BlockSpec(memory_space=pl.ANY)],
            out_specs=pl.BlockSpec(memory_space=pl.ANY),
            scratch_shapes=[pltpu.VMEM((STAGE_SLOTS, TM, ROW_W), F32), pltpu.VMEM((TME, ROW_W), F32),
                            pltpu.SemaphoreType.DMA((STAGE_SLOTS,)), pltpu.SemaphoreType.DMA((2,)),
                            pltpu.SemaphoreType.DMA]),
        out_shape=jax.ShapeDtypeStruct((MOE_ROWS, ROW_W), F32),
        compiler_params=_params(("arbitrary",)),
        name="dispatch_rows",
    )(pos, pend, counts, n_tiles, rows)


def _plan(route, counts):
    bucket = route[0].astype(jnp.int32)
    rank = route[1].astype(jnp.int32)
    counts = counts[:N_BUCKETS, 0].astype(jnp.int32)
    padded = ((counts + TME - 1) // TME) * TME
    pend = jnp.cumsum(padded)
    pstart = pend - padded
    ids = jnp.arange(N_BUCKETS, dtype=jnp.int32)
    pos = rank + jnp.sum(jnp.where(bucket[None, :] == ids[:, None], pstart[:, None], 0), axis=0)
    n_tiles = pend[-1] // TME
    tile_start = jnp.minimum(jnp.arange(MOE_TILES, dtype=jnp.int32), n_tiles - 1) * TME
    tile_bucket = jnp.minimum(jnp.sum(tile_start[:, None] >= pend[None, :], axis=1), N_BUCKETS - 1)
    group, pair = tile_bucket // PAIRS_PER_GROUP, tile_bucket % PAIRS_PER_GROUP
    slot_a = jnp.asarray(SLOT_A_LOCAL, jnp.int32)
    slot_b = jnp.asarray(SLOT_B_LOCAL, jnp.int32)
    pair_hot = pair[:, None] == jnp.arange(PAIRS_PER_GROUP, dtype=jnp.int32)[None, :]
    tile_a = group * EXPERTS_PER_GROUP + jnp.sum(jnp.where(pair_hot, slot_a[None, :], 0), axis=1)
    tile_b = group * EXPERTS_PER_GROUP + jnp.sum(jnp.where(pair_hot, slot_b[None, :], 0), axis=1)
    return pos, pend, counts, tile_a.astype(jnp.int32), tile_b.astype(jnp.int32), n_tiles.reshape(1)


def _post_moe_kernel(split_out, pos_ref, f_ref, x1_ref, gate_ref, lng_ref, lnb_ref, *refs):
    out_refs, (fbuf, sem) = refs[:-2], refs[-2:]
    t = pl.program_id(0)

    def wait_tile(slot):
        pltpu.make_async_copy(f_ref.at[pl.ds(0, TM), :], fbuf.at[slot], sem.at[slot]).wait()

    @pl.when(t == 0)
    def _():
        def issue(r, carry):
            _row_copy(f_ref, pos_ref[r], fbuf.at[r // TM], r % TM, sem.at[r // TM]).start()
            return carry
        lax.fori_loop(0, (GATHER_SLOTS - 1) * TM, issue, 0, unroll=8)

    slot = t % GATHER_SLOTS
    wait_tile(slot)
    ahead = jnp.minimum(t + GATHER_SLOTS - 1, N_TILES - 1)
    aslot = (t + GATHER_SLOTS - 1) % GATHER_SLOTS
    for r in range(TM):
        _row_copy(f_ref, pos_ref[ahead * TM + r], fbuf.at[aslot], r, sem.at[aslot]).start(priority=r % 2)
    y = _layer_norm(ALPHA * x1_ref[...] + gate_ref[...] * fbuf[slot], lng_ref[...], lnb_ref[...])

    @pl.when(t == N_TILES - 1)
    def _():
        for k in range(1, GATHER_SLOTS):
            wait_tile((t + k) % GATHER_SLOTS)
    if split_out:
        @pl.when(t < PROMPT_TILES)
        def _():
            out_refs[0][...] = y

        @pl.when(t >= PROMPT_TILES)
        def _():
            out_refs[1][...] = y
    else:
        out_refs[0][...] = y


def _post_moe(layer, pos, f_sorted, x1, mod, ln_g, ln_b, split_out):
    tok = pl.BlockSpec((TM, D_MODEL), lambda t, p: (t, 0))
    if split_out:
        out_specs = [pl.BlockSpec((TM, D_MODEL), lambda t, p: (jnp.minimum(t, PROMPT_TILES - 1), 0)),
                     pl.BlockSpec((TM, D_MODEL), lambda t, p: (jnp.maximum(t - PROMPT_TILES, 0), 0))]
        out_shape = [jax.ShapeDtypeStruct((N_PROMPT, D_MODEL), F32), jax.ShapeDtypeStruct((N_SAMPLE, D_MODEL), F32)]
    else:
        out_specs = [tok]
        out_shape = [jax.ShapeDtypeStruct((N_TOK, D_MODEL), F32)]
    mod_spec = pl.BlockSpec((None, None, None, 1, D_MODEL), lambda t, p: (layer, _mod_row(t), 5, 0, 0))
    return pl.pallas_call(
        functools.partial(_post_moe_kernel, split_out),
        grid_spec=pltpu.PrefetchScalarGridSpec(
            num_scalar_prefetch=1, grid=(N_TILES,),
            in_specs=[pl.BlockSpec(memory_space=pl.ANY), tok, mod_spec,
                      pl.BlockSpec((1, D_MODEL), lambda t, p: (0, 0)), pl.BlockSpec((1, D_MODEL), lambda t, p: (0, 0))],
            out_specs=out_specs,
            scratch_shapes=[pltpu.VMEM((GATHER_SLOTS, TM, D_MODEL), F32),
                            pltpu.SemaphoreType.DMA((GATHER_SLOTS,))]),
        out_shape=out_shape,
        compiler_params=_params(("arbitrary",)),
        name="post_moe",
    )(pos, f_sorted, x1, mod, ln_g, ln_b)


def _ffn(layer, o_p, o_s, x_list, mod, w_out, ln_g, ln_b, rw_t, rb, w_gate, w_up, w_down, split_out):
    x1, rows, route, counts = _post_attn(layer, o_p, o_s, x_list, mod, w_out, ln_g[layer, 0][None],
                                         ln_b[layer, 0][None], rw_t, rb)
    pos, pend, counts, tile_a, tile_b, n_tiles = _plan(route, counts)
    rows_sorted = _dispatch(pos, pend, counts, n_tiles, rows)
    f_sorted = _moe(layer, tile_a, tile_b, n_tiles, rows_sorted, w_gate, w_up, w_down)
    return _post_moe(layer, pos, f_sorted, x1, mod, ln_g[layer, 1][None], ln_b[layer, 1][None], split_out)


def kernel(x_prompt, x_sample, c, cache_mla_ckv, cache_mla_kpe, cache_swa_k, cache_swa_v, cache_gqa_k, cache_gqa_v, c_ctx, w_mod, b_mod, ln_g, ln_b, w_in_ab, mla_q_norm, mla_w_uq, mla_kv_norm, mla_w_ukv, swa_sink, w_out_ab, w_in_c, gqa_q_norm, gqa_k_norm, w_out_c, router_w, router_bias, exp_w_gate, exp_w_up, exp_w_down):
    xp = x_prompt.reshape(N_PROMPT, D_MODEL)
    xs = x_sample.reshape(N_SAMPLE, D_MODEL)
    cond = jnp.concatenate([c_ctx[None], c, jnp.zeros((MOD_ROWS - 1 - DEC_BATCH, D_MODEL), F32)], axis=0)
    mod = _modulation(cond, w_mod, b_mod).reshape(DEPTH, MOD_ROWS, 6, 1, D_MODEL)
    tabs = _rope_tables()
    rw_t = router_w.T
    rb = router_bias.reshape(N_EXPERTS, 1)

    w = w_in_ab[0]
    pad = jnp.zeros((D_MODEL, AB_COLS - w.shape[1]), F32)
    w_in = jnp.concatenate([w[:, :384], w[:, 416:], w[:, 384:416], pad], axis=1).astype(BF16)
    w_uq, w_kn, w_v, place = _mla_weights(mla_w_uq[0], mla_w_ukv[0])
    qa, ka, va, qb, ckv_p, kpe_p, kbt_p, vbt_p, kb_s, vb_s = _inproj_ab(
        xp, xs, mod, w_in, mla_q_norm[0][None], mla_kv_norm[0][None], w_uq, w_kn, w_v, place, tabs)
    sink = swa_sink[0]
    o_p = _attn_ab_prompt(sink, qa, ka, va, qb, kbt_p, vbt_p)
    kpe_ctx = jnp.pad(cache_mla_kpe[:, 0].reshape(-1, ROPE_A), ((0, 0), (0, LANES - ROPE_A)))
    o_s = _attn_ab_sample(sink, qa, ka, va, qb, kb_s, vb_s,
                          cache_mla_ckv[:, 0].reshape(-1, KV_LORA), kpe_ctx,
                          cache_swa_k[:, 0].reshape(-1, KVH_B * HD), cache_swa_v[:, 0].reshape(-1, KVH_B * HD),
                          w_kn, w_v, place)
    (x2,) = _ffn(0, o_p, o_s, [xp, xs], mod, w_out_ab[0].astype(BF16), ln_g, ln_b, rw_t, rb,
                 exp_w_gate, exp_w_up, exp_w_down, split_out=False)

    g_full = jnp.concatenate([jnp.tile(gqa_q_norm[0], H_C), jnp.tile(gqa_k_norm[0], KVH_C)])[None]
    qc, kc_p, vc_p, kc_s, vc_s = _inproj_c(x2, mod, w_in_c[0].astype(BF16), g_full, tabs)
    oc_p = _attn_c_prompt(qc, kc_p, vc_p)
    oc_s = _attn_c_sample(qc, kc_s, vc_s, cache_gqa_k[:, 0].reshape(-1, KVH_C * HD),
                          cache_gqa_v[:, 0].reshape(-1, KVH_C * HD))
    y_p, y_s = _ffn(1, oc_p, oc_s, [x2], mod, w_out_c[0].astype(BF16), ln_g, ln_b, rw_t, rb,
                    exp_w_gate, exp_w_up, exp_w_down, split_out=True)

    y_prompt = y_p.reshape(BATCH, SEQ, D_MODEL)
    y_sample = y_s.reshape(DEC_BATCH, DEC_SEQ, D_MODEL)
    new_ckv = ckv_p.reshape(BATCH, 1, SEQ, KV_LORA)
    new_kpe = kpe_p.reshape(BATCH, 1, SEQ, ROPE_A)
    new_swk = kbt_p.reshape(BATCH, 1, KVH_B, HD, SEQ).transpose(0, 1, 4, 2, 3)
    new_swv = vbt_p.reshape(BATCH, 1, KVH_B, HD, SEQ).transpose(0, 1, 4, 2, 3)
    new_gk = kc_p.reshape(BATCH, 1, SEQ, KVH_C, HD)
    new_gv = vc_p.reshape(BATCH, 1, SEQ, KVH_C, HD)
    return y_prompt, y_sample, new_ckv, new_kpe, new_swk, new_swv, new_gk, new_gv
```

```python
import functools

import numpy as np
import jax
import jax.numpy as jnp
from jax import lax
from jax.experimental import pallas as pl
from jax.experimental.pallas import tpu as pltpu

D_MODEL = 1024
BATCH = 32
SEQ = 256
DEPTH = 2
DEC_BATCH = 4
DEC_SEQ = 1024
PAST_LEN = 256
GRID_W = 64
ROPE_THETA = 10000.0
WINDOW = 128
HD = 64
H_A = 8
NOPE_A = 64
ROPE_A = 32
V_A = 64
Q_LORA = 256
KV_LORA = 128
H_B = 8
KVH_B = 2
H_C = 16
KVH_C = 4
N_EXPERTS = 16
N_GROUPS = 4
EXPERTS_PER_GROUP = 4
D_EXPERT = 512
ALPHA = (2 * DEPTH) ** 0.25
NEG_INF = -1e30
EPS = 1e-6

LANES = 128
N_PROMPT = BATCH * SEQ
N_SAMPLE = DEC_BATCH * DEC_SEQ
N_TOK = N_PROMPT + N_SAMPLE
TM = 256
TP = 512
N_TILES = N_TOK // TM
PROMPT_TILES = N_PROMPT // TM
BQ_S = 256
LOG2E = 1.4426950408889634
MLA_STACK_PROMPT = 4
MLA_STACK_SAMPLE = 4
GQA_STACK_PROMPT = 4
GQA_STACK_SAMPLE = 2
MOD_ROWS = 8

PAIRS_PER_GROUP = 6
N_BUCKETS = N_GROUPS * PAIRS_PER_GROUP
SLOT_A_LOCAL = (0, 2, 2, 3, 3, 3)
SLOT_B_LOCAL = (1, 1, 0, 0, 1, 2)
TME = 256
MOE_TILES = -(-(N_TOK + N_BUCKETS * (TME - 1)) // TME)
MOE_ROWS = MOE_TILES * TME

BUCKET_ROWS = 32
ROW_W = D_MODEL + LANES
STAGE_SLOTS = 3
GATHER_SLOTS = 3

AB_COLS = 1280
C_NORM_COLS = H_C * HD + KVH_C * HD
NORM_BLOCK = 256

BF16 = jnp.bfloat16
F32 = jnp.float32
VMEM_LIMIT = 52 * 1024 * 1024


def _mod_row(t, tile=TM):
    p_tiles = N_PROMPT // tile
    return jnp.where(t < p_tiles, 0, 1 + (t - p_tiles) // (DEC_SEQ // tile))


def _mod_spec(layer, chunk, tile=TM):
    return pl.BlockSpec((None, None, None, 1, D_MODEL), lambda t: (layer, _mod_row(t, tile), chunk, 0, 0))


def _const_spec(shape):
    nd = len(shape)
    return pl.BlockSpec(shape, lambda *_: (0,) * nd)


def _split_specs(tile=TM):
    p_tiles = N_PROMPT // tile
    return [
        pl.BlockSpec((tile, D_MODEL), lambda t: (jnp.minimum(t, p_tiles - 1), 0)),
        pl.BlockSpec((tile, D_MODEL), lambda t: (jnp.maximum(t - p_tiles, 0), 0)),
    ]


def _group_spec(tile, width, prompt):
    p_tiles = N_PROMPT // tile
    if prompt:
        return pl.BlockSpec((tile, width), lambda t: (jnp.minimum(t, p_tiles - 1), 0))
    return pl.BlockSpec((tile, width), lambda t: (jnp.maximum(t - p_tiles, 0), 0))


def _pick(t, a_ref, b_ref):
    return jnp.where(t < N_PROMPT // a_ref.shape[0], a_ref[...], b_ref[...])


def _sub_tiles(rows):
    return [slice(i * TM, (i + 1) * TM) for i in range(rows // TM)]


def _params(sem):
    return pltpu.CompilerParams(dimension_semantics=sem, vmem_limit_bytes=VMEM_LIMIT)


def _mod_kernel(cond_ref, w_ref, b_ref, o_ref):
    c = cond_ref[...]
    s = (c / (1.0 + jnp.exp(-c))).astype(BF16)
    o_ref[...] = jnp.dot(s, w_ref[...].astype(BF16), preferred_element_type=F32) + b_ref[...]


def _modulation(cond, w_mod, b_mod):
    tn = 1536
    return pl.pallas_call(
        _mod_kernel,
        grid=(DEPTH, 6 * D_MODEL // tn),
        in_specs=[
            pl.BlockSpec((MOD_ROWS, D_MODEL), lambda l, j: (0, 0)),
            pl.BlockSpec((None, D_MODEL, tn), lambda l, j: (l, 0, j)),
            pl.BlockSpec((None, 1, tn), lambda l, j: (l, 0, j)),
        ],
        out_specs=pl.BlockSpec((None, MOD_ROWS, tn), lambda l, j: (l, 0, j)),
        out_shape=jax.ShapeDtypeStruct((DEPTH, MOD_ROWS, 6 * D_MODEL), F32),
        compiler_params=_params(("arbitrary", "arbitrary")),
        name="modulation",
    )(cond, w_mod, b_mod.reshape(DEPTH, 1, 6 * D_MODEL))


def _grid_cos_sin(half):
    n_freq = half // 2
    rows = DEC_SEQ // GRID_W
    row = jnp.repeat(jnp.arange(rows, dtype=F32), GRID_W)
    col = jnp.tile(jnp.arange(GRID_W, dtype=F32), rows)
    inv = 1.0 / (ROPE_THETA ** (jnp.arange(n_freq, dtype=F32) / n_freq))
    ang = jnp.concatenate([row[:, None] * inv, col[:, None] * inv], axis=-1)
    return jnp.cos(ang), jnp.sin(ang)


def _rope_tables():
    tabs = []
    for half in (HD // 2, ROPE_A // 2):
        cos, sin = _grid_cos_sin(half)
        reps = LANES // (2 * half)
        tabs.append(jnp.tile(jnp.concatenate([cos, cos], axis=-1), (1, reps)))
        tabs.append(jnp.tile(jnp.concatenate([-sin, sin], axis=-1), (1, reps)))
    half = ROPE_A // 2
    one = jnp.ones((DEC_SEQ, NOPE_A), F32)
    zero = jnp.zeros((DEC_SEQ, NOPE_A), F32)
    pad1 = jnp.ones((DEC_SEQ, LANES - NOPE_A - ROPE_A), F32)
    pad0 = jnp.zeros((DEC_SEQ, LANES - NOPE_A - ROPE_A), F32)
    zh = jnp.zeros((DEC_SEQ, half), F32)
    tabs.append(jnp.concatenate([one, cos, cos, pad1], axis=-1))
    tabs.append(jnp.concatenate([zero, -sin, zh, pad0], axis=-1))
    tabs.append(jnp.concatenate([zero, zh, sin, pad0], axis=-1))
    return tuple(tabs)


def _rope(x, c, s, half):
    w = x.shape[1]
    reps = w // LANES
    if reps > 1:
        c = jnp.concatenate([c] * reps, axis=1)
        s = jnp.concatenate([s] * reps, axis=1)
    ahead = pltpu.roll(x, w - half, 1)
    behind = pltpu.roll(x, half, 1)
    lane = lax.broadcasted_iota(jnp.int32, x.shape, 1)
    swapped = jnp.where((lane & (2 * half - 1)) < half, ahead, behind)
    return x * c + swapped * s


def _rope_specs(n, tile):
    p_tiles = N_PROMPT // tile

    def idx(t):
        return (jnp.maximum(t - p_tiles, 0) % (DEC_SEQ // tile), 0)
    return [pl.BlockSpec((tile, LANES), idx)] * n


def _rms(x, g_ref):
    return x * lax.rsqrt(jnp.mean(x * x, axis=-1, keepdims=True) + EPS) * g_ref[...]


def _inproj_ab_kernel(xp_ref, xs_ref, shift_ref, scale_ref, w_in_ref, qn_ref, kvn_ref, w_uq_ref, w_kn_ref, w_v_ref,
                      place_ref, c64_ref, s64_ref, c32_ref, s32_ref, ca_ref, sa_ref, sb_ref,
                      qa_ref, ka_ref, va_ref, qb_ref, ckv_p_ref, kpe_p_ref, kbt_p_ref, vbt_p_ref, kb_s_ref, vb_s_ref):
    t = pl.program_id(0)
    x = _pick(t, xp_ref, xs_ref)
    subs = _sub_tiles(x.shape[0])
    h = [(x[s] * (1.0 + scale_ref[...]) + shift_ref[...]).astype(BF16) for s in subs]
    z = [jnp.dot(h_s, w_in_ref[...], preferred_element_type=F32) for h_s in h]
    ql = [_rms(z_s[:, :Q_LORA], qn_ref).astype(BF16) for z_s in z]
    qa = [jnp.dot(ql_s, w_uq_ref[...], preferred_element_type=F32) for ql_s in ql]
    ckv = [_rms(z_s[:, Q_LORA:Q_LORA + KV_LORA], kvn_ref) for z_s in z]
    k_nope = [jnp.dot(c_s.astype(BF16), w_kn_ref[...], preferred_element_type=F32) for c_s in ckv]
    va = [jnp.dot(c_s.astype(BF16), w_v_ref[...], preferred_element_type=F32).astype(BF16) for c_s in ckv]
    for s, va_s in zip(subs, va):
        va_ref[s, :] = va_s
    qb = [z_s[:, 384:896] for z_s in z]
    kb = [z_s[:, 896:1024] for z_s in z]
    vb = [z_s[:, 1024:1152] for z_s in z]
    kpe = [z_s[:, 1152:1280] for z_s in z]

    def finish(qa_out, qb_out, kpe_out):
        k_rope = [jnp.dot(kpe_s.astype(BF16), place_ref[...], preferred_element_type=F32) for kpe_s in kpe_out]
        for i, s in enumerate(subs):
            qa_ref[s, :] = qa_out[i].astype(BF16)
            qb_ref[s, :] = qb_out[i].astype(BF16)
            ka_ref[s, :] = (k_nope[i] + k_rope[i]).astype(BF16)

    p_tiles = N_PROMPT // x.shape[0]

    @pl.when(t < p_tiles)
    def _():
        finish(qa, qb, kpe)
        for i, s in enumerate(subs):
            ckv_p_ref[s, :] = ckv[i]
            kpe_p_ref[s, :] = kpe[i][:, :ROPE_A]
            kbt_p_ref[i] = kb[i].T
            vbt_p_ref[i] = vb[i].T

    @pl.when(t >= p_tiles)
    def _():
        w = qa[0].shape[1]
        reps = w // LANES
        half = ROPE_A // 2
        qa_rot, qb_rot, kb_rot, kpe_rot = [], [], [], []
        for i, s in enumerate(subs):
            ca = jnp.concatenate([ca_ref[s, :]] * reps, axis=1)
            sa = jnp.concatenate([sa_ref[s, :]] * reps, axis=1)
            sb = jnp.concatenate([sb_ref[s, :]] * reps, axis=1)
            qa_rot.append(qa[i] * ca + pltpu.roll(qa[i], w - half, 1) * sa + pltpu.roll(qa[i], half, 1) * sb)
            qb_rot.append(_rope(qb[i], c64_ref[s, :], s64_ref[s, :], HD // 2))
            kb_rot.append(_rope(kb[i], c64_ref[s, :], s64_ref[s, :], HD // 2))
            kpe_rot.append(_rope(kpe[i], c32_ref[s, :], s32_ref[s, :], ROPE_A // 2))
        finish(qa_rot, qb_rot, kpe_rot)
        for i, s in enumerate(subs):
            kb_s_ref[s, :] = kb_rot[i]
            vb_s_ref[s, :] = vb[i]


def _mla_weights(w_uq, w_ukv):
    pad = LANES - NOPE_A - ROPE_A
    uq = w_uq.reshape(Q_LORA, H_A, NOPE_A + ROPE_A)
    uq = jnp.pad(uq, ((0, 0), (0, 0), (0, pad))).reshape(Q_LORA, H_A * LANES)
    ukv = w_ukv.reshape(KV_LORA, H_A, NOPE_A + V_A)
    kn = jnp.pad(ukv[:, :, :NOPE_A], ((0, 0), (0, 0), (0, LANES - NOPE_A))).reshape(KV_LORA, H_A * LANES)
    wv = ukv[:, :, NOPE_A:].reshape(KV_LORA, H_A * V_A)
    src = np.arange(LANES)[:, None]
    dst = np.arange(H_A * LANES)[None, :] % LANES
    place = (src < ROPE_A) & (dst == src + NOPE_A)
    return uq.astype(BF16), kn.astype(BF16), wv.astype(BF16), jnp.asarray(place, BF16)


def _inproj_ab(xp, xs, mod, w_in, q_norm, kv_norm, w_uq, w_kn, w_v, place, tabs):
    widths = [(H_A * LANES, BF16), (H_A * LANES, BF16), (H_A * V_A, BF16), (H_B * HD, BF16)]
    out_specs = [pl.BlockSpec((TP, w), lambda t: (t, 0)) for w, _ in widths]
    out_shape = [jax.ShapeDtypeStruct((N_TOK, w), dt) for w, dt in widths]
    for w in (KV_LORA, ROPE_A):
        out_specs.append(_group_spec(TP, w, prompt=True))
        out_shape.append(jax.ShapeDtypeStruct((N_PROMPT, w), F32))
    assert SEQ == TM
    p_tiles = N_PROMPT // TP
    for _ in range(2):
        out_specs.append(pl.BlockSpec((TP // SEQ, KVH_B * HD, SEQ), lambda t: (jnp.minimum(t, p_tiles - 1), 0, 0)))
        out_shape.append(jax.ShapeDtypeStruct((BATCH, KVH_B * HD, SEQ), F32))
    for w in (KVH_B * HD, KVH_B * HD):
        out_specs.append(_group_spec(TP, w, prompt=False))
        out_shape.append(jax.ShapeDtypeStruct((N_SAMPLE, w), F32))
    return pl.pallas_call(
        _inproj_ab_kernel,
        grid=(N_TOK // TP,),
        in_specs=_split_specs(TP) + [_mod_spec(0, 0, TP), _mod_spec(0, 1, TP), _const_spec((D_MODEL, AB_COLS)),
                                     _const_spec((1, Q_LORA)), _const_spec((1, KV_LORA)),
                                     _const_spec((Q_LORA, H_A * LANES)), _const_spec((KV_LORA, H_A * LANES)),
                                     _const_spec((KV_LORA, H_A * V_A)), _const_spec((LANES, H_A * LANES))]
        + _rope_specs(7, TP),
        out_specs=out_specs,
        out_shape=out_shape,
        compiler_params=_params(("arbitrary",)),
        name="inproj_ab",
    )(xp, xs, mod, mod, w_in, q_norm, kv_norm, w_uq, w_kn, w_v, place, *tabs)


def _inproj_c_kernel(x_ref, shift_ref, scale_ref, w_in_ref, g_ref, ones_ref, c64_ref, s64_ref,
                     q_ref, k_p_ref, v_p_ref, k_s_ref, v_s_ref):
    t = pl.program_id(0)
    subs = _sub_tiles(x_ref.shape[0])
    h = [(x_ref[s, :] * (1.0 + scale_ref[...]) + shift_ref[...]).astype(BF16) for s in subs]
    z = [jnp.dot(h_s, w_in_ref[...], preferred_element_type=F32) for h_s in h]
    qk = [z_s[:, :C_NORM_COLS] for z_s in z]
    sq = [qk_s * qk_s for qk_s in qk]
    sq_hi = [sq_s.astype(BF16) for sq_s in sq]
    sq_lo = [(sq_s - hi_s.astype(F32)).astype(BF16) for sq_s, hi_s in zip(sq, sq_hi)]
    ss = []
    for hi_s, lo_s in zip(sq_hi, sq_lo):
        blocks = []
        for j in range(C_NORM_COLS // NORM_BLOCK):
            sl = slice(j * NORM_BLOCK, (j + 1) * NORM_BLOCK)
            blocks.append(jnp.dot(hi_s[:, sl], ones_ref[...], preferred_element_type=F32)
                          + jnp.dot(lo_s[:, sl], ones_ref[...], preferred_element_type=F32))
        ss.append(jnp.concatenate(blocks, axis=1))
    qk = [qk_s * lax.rsqrt(ss_s * (1.0 / HD) + EPS) * g_ref[...] for qk_s, ss_s in zip(qk, ss)]
    p_tiles = N_PROMPT // x_ref.shape[0]

    @pl.when(t < p_tiles)
    def _():
        for s, qk_s, z_s in zip(subs, qk, z):
            q_ref[s, :] = qk_s[:, :H_C * HD].astype(BF16)
            k_p_ref[s, :] = qk_s[:, H_C * HD:]
            v_p_ref[s, :] = z_s[:, C_NORM_COLS:]

    @pl.when(t >= p_tiles)
    def _():
        for s, qk_s, z_s in zip(subs, qk, z):
            q_ref[s, :] = _rope(qk_s[:, :H_C * HD], c64_ref[s, :], s64_ref[s, :], HD // 2).astype(BF16)
            k_s_ref[s, :] = _rope(qk_s[:, H_C * HD:], c64_ref[s, :], s64_ref[s, :], HD // 2)
            v_s_ref[s, :] = z_s[:, C_NORM_COLS:]


def _inproj_c(x, mod, w_in, g_full, tabs):
    def tok(w):
        return pl.BlockSpec((TP, w), lambda t: (t, 0))
    kv_w = KVH_C * HD
    head = np.arange(NORM_BLOCK) // HD
    ones = jnp.asarray(head[:, None] == head[None, :], BF16)
    return pl.pallas_call(
        _inproj_c_kernel,
        grid=(N_TOK // TP,),
        in_specs=[tok(D_MODEL), _mod_spec(1, 0, TP), _mod_spec(1, 1, TP),
                  _const_spec((D_MODEL, C_NORM_COLS + KVH_C * HD)),
                  _const_spec((1, C_NORM_COLS)), _const_spec((NORM_BLOCK, NORM_BLOCK))] + _rope_specs(2, TP),
        out_specs=[tok(H_C * HD), _group_spec(TP, kv_w, True), _group_spec(TP, kv_w, True),
                   _group_spec(TP, kv_w, False), _group_spec(TP, kv_w, False)],
        out_shape=[jax.ShapeDtypeStruct((N_TOK, H_C * HD), BF16),
                   jax.ShapeDtypeStruct((N_PROMPT, kv_w), F32), jax.ShapeDtypeStruct((N_PROMPT, kv_w), F32),
                   jax.ShapeDtypeStruct((N_SAMPLE, kv_w), F32), jax.ShapeDtypeStruct((N_SAMPLE, kv_w), F32)],
        compiler_params=_params(("arbitrary",)),
        name="inproj_c",
    )(x, mod, mod, w_in, g_full, ones, *tabs[:2])


def _qk(q, k):
    return lax.dot_general(q, k, (((1,), (1,)), ((), ())), preferred_element_type=F32)


def _softmax_pv(scores, values, scale, sink=None, kv_t=False):
    c = scale * LOG2E
    m = scores[0].max(axis=-1, keepdims=True)
    for s in scores[1:]:
        m = jnp.maximum(m, s.max(axis=-1, keepdims=True))
    m = m * c
    if sink is not None:
        sink = sink * LOG2E
        m = jnp.maximum(m, sink)
    den = None
    out = None
    for s, v in zip(scores, values):
        p = jnp.exp2(s * c - m)
        d = p.sum(axis=-1, keepdims=True)
        p = p.astype(BF16)
        if isinstance(v, (list, tuple)):
            rows = p.shape[0] // len(v)
            o = jnp.concatenate([_qk(p[b * rows:(b + 1) * rows], v_b) if kv_t
                                 else jnp.dot(p[b * rows:(b + 1) * rows], v_b, preferred_element_type=F32)
                                 for b, v_b in enumerate(v)], axis=0)
        else:
            o = _qk(p, v) if kv_t else jnp.dot(p, v, preferred_element_type=F32)
        den = d if den is None else den + d
        out = o if out is None else out + o
    if sink is not None:
        den = den + jnp.exp2(sink - m)
    return out * (1.0 / den)


def _mla_heads(qa_ref, ka_refs, va_refs, o_ref, stack):
    scale = (NOPE_A + ROPE_A) ** -0.5
    rows = qa_ref.shape[0]
    for h0 in range(0, H_A, stack):
        heads = range(h0, h0 + stack)
        scores = [jnp.concatenate([_qk(qa_ref[:, h * LANES:(h + 1) * LANES], ka_ref[:, h * LANES:(h + 1) * LANES])
                                   for h in heads], axis=0) for ka_ref in ka_refs]
        values = [[va_ref[:, h * V_A:(h + 1) * V_A] for h in heads] for va_ref in va_refs]
        o = _softmax_pv(scores, values, scale)
        o = jnp.concatenate([o[b * rows:(b + 1) * rows] for b in range(stack)], axis=1)
        o_ref[:, h0 * V_A:(h0 + stack) * V_A] = o.astype(o_ref.dtype)


def _gqa_groups(q_ref, col0, n_q, ks, vs, mask=None, sink=None, kv_t=False):
    rows = q_ref.shape[0]
    n_groups = len(ks)
    n_seg = len(ks[0])
    per_group = []
    for g in range(n_groups):
        base = col0 + g * n_q * HD
        q = jnp.concatenate([q_ref[:, base + i * HD:base + (i + 1) * HD] for i in range(n_q)], axis=0)
        if kv_t:
            per_group.append([jnp.dot(q, k, preferred_element_type=F32) for k in ks[g]])
        else:
            per_group.append([_qk(q, k) for k in ks[g]])
    scores = [jnp.concatenate([per_group[g][j] for g in range(n_groups)], axis=0) for j in range(n_seg)]
    values = [[vs[g][j] for g in range(n_groups)] for j in range(n_seg)]
    if mask is not None:
        scores[0] = jnp.where(mask, scores[0], NEG_INF)
    o = _softmax_pv(scores, values, HD ** -0.5, sink, kv_t)
    return jnp.concatenate([o[i * rows:(i + 1) * rows] for i in range(n_groups * n_q)], axis=1)


def _stacked_sink(sink_ref, h0, n_q, rows):
    head = lax.broadcasted_iota(jnp.int32, (n_q * rows, 1), 0) // rows
    col = jnp.full((n_q * rows, 1), sink_ref[h0], F32)
    for g in range(1, n_q):
        col = jnp.where(head == g, sink_ref[h0 + g], col)
    return col


def _attn_ab_prompt_kernel(sink_ref, qa_ref, ka_ref, va_ref, qb_ref, kbt_ref, vbt_ref, o_ref):
    _mla_heads(qa_ref, [ka_ref], [va_ref], o_ref, MLA_STACK_PROMPT)
    kbt = kbt_ref[...].astype(BF16)
    vbt = vbt_ref[...].astype(BF16)
    sls = [slice(kh * HD, (kh + 1) * HD) for kh in range(KVH_B)]
    o = _gqa_groups(qb_ref, 0, H_B // KVH_B, [[kbt[sl, :]] for sl in sls], [[vbt[sl, :]] for sl in sls],
                    sink=_stacked_sink(sink_ref, 0, H_B, SEQ), kv_t=True)
    o_ref[:, H_A * V_A:] = o.astype(o_ref.dtype)


def _attn_ab_prompt(sink, qa, ka, va, qb, kbt, vbt):
    def blk(w):
        return pl.BlockSpec((SEQ, w), lambda b: (b, 0))
    tblk = pl.BlockSpec((None, KVH_B * HD, SEQ), lambda b: (b, 0, 0))
    return pl.pallas_call(
        _attn_ab_prompt_kernel,
        grid=(BATCH,),
        in_specs=[pl.BlockSpec(memory_space=pltpu.SMEM), blk(H_A * LANES), blk(H_A * LANES), blk(H_A * V_A),
                  blk(H_B * HD), tblk, tblk],
        out_specs=blk(D_MODEL),
        out_shape=jax.ShapeDtypeStruct((N_PROMPT, H_A * V_A + H_B * HD), BF16),
        compiler_params=_params(("arbitrary",)),
        name="attn_ab_prompt",
    )(sink, qa, ka, va, qb, kbt, vbt)


def _attn_ab_sample_kernel(sink_ref, qa_ref, ka_ref, va_ref, qb_ref, kb_ref, vb_ref,
                           ckv_ctx_ref, kpe_ctx_ref, kb_ctx_ref, vb_ctx_ref, w_kn_ref, w_v_ref, place_ref,
                           o_ref, ka_ctx_ref, va_ctx_ref):
    i = pl.program_id(1)

    @pl.when(i == 0)
    def _():
        ckv16 = ckv_ctx_ref[...].astype(BF16)
        k_nope = jnp.dot(ckv16, w_kn_ref[...], preferred_element_type=F32)
        k_rope = jnp.dot(kpe_ctx_ref[...].astype(BF16), place_ref[...], preferred_element_type=F32)
        ka_ctx_ref[...] = (k_nope + k_rope).astype(BF16)
        va_ctx_ref[...] = jnp.dot(ckv16, w_v_ref[...], preferred_element_type=F32).astype(BF16)

    _mla_heads(qa_ref, [ka_ref, ka_ctx_ref], [va_ref, va_ctx_ref], o_ref, MLA_STACK_SAMPLE)

    n_win = BQ_S + 2 * WINDOW
    start = pl.multiple_of(jnp.clip(i * BQ_S - WINDOW, 0, DEC_SEQ - n_win), WINDOW)
    row = lax.broadcasted_iota(jnp.int32, (H_B * BQ_S, n_win), 0)
    qpos = i * BQ_S + (row & (BQ_S - 1))
    kpos = start + lax.broadcasted_iota(jnp.int32, (H_B * BQ_S, n_win), 1)
    in_band = jnp.abs(qpos - kpos) <= WINDOW
    kwin = kb_ref[pl.ds(start, n_win), :].astype(BF16)
    vwin = vb_ref[pl.ds(start, n_win), :].astype(BF16)
    kctx = kb_ctx_ref[...].astype(BF16)
    vctx = vb_ctx_ref[...].astype(BF16)
    sls = [slice(kh * HD, (kh + 1) * HD) for kh in range(KVH_B)]
    o = _gqa_groups(qb_ref, 0, H_B // KVH_B, [[kwin[:, sl], kctx[:, sl]] for sl in sls],
                    [[vwin[:, sl], vctx[:, sl]] for sl in sls],
                    mask=in_band, sink=_stacked_sink(sink_ref, 0, H_B, BQ_S))
    o_ref[:, H_A * V_A:] = o.astype(o_ref.dtype)


def _attn_ab_sample(sink, qa, ka, va, qb, kb, vb, ckv_ctx, kpe_ctx, kb_ctx, vb_ctx, w_kn, w_v, place):
    qoff = N_PROMPT // BQ_S
    nq = DEC_SEQ // BQ_S
    boff = N_PROMPT // DEC_SEQ

    def qblk(w):
        return pl.BlockSpec((BQ_S, w), lambda b, i: (qoff + b * nq + i, 0))

    def bblk(w):
        return pl.BlockSpec((DEC_SEQ, w), lambda b, i: (boff + b, 0))

    def sblk(w):
        return pl.BlockSpec((DEC_SEQ, w), lambda b, i: (b, 0))

    def cblk(w):
        return pl.BlockSpec((PAST_LEN, w), lambda b, i: (b, 0))

    return pl.pallas_call(
        _attn_ab_sample_kernel,
        grid=(DEC_BATCH, nq),
        in_specs=[pl.BlockSpec(memory_space=pltpu.SMEM), qblk(H_A * LANES), bblk(H_A * LANES), bblk(H_A * V_A),
                  qblk(H_B * HD), sblk(KVH_B * HD), sblk(KVH_B * HD),
                  cblk(KV_LORA), cblk(LANES), cblk(KVH_B * HD), cblk(KVH_B * HD),
                  _const_spec((KV_LORA, H_A * LANES)), _const_spec((KV_LORA, H_A * V_A)),
                  _const_spec((LANES, H_A * LANES))],
        out_specs=pl.BlockSpec((BQ_S, D_MODEL), lambda b, i: (b * nq + i, 0)),
        out_shape=jax.ShapeDtypeStruct((N_SAMPLE, H_A * V_A + H_B * HD), BF16),
        scratch_shapes=[pltpu.VMEM((PAST_LEN, H_A * LANES), BF16), pltpu.VMEM((PAST_LEN, H_A * V_A), BF16)],
        compiler_params=_params(("arbitrary", "arbitrary")),
        name="attn_ab_sample",
    )(sink, qa, ka, va, qb, kb, vb, ckv_ctx, kpe_ctx, kb_ctx, vb_ctx, w_kn, w_v, place)


def _gqa_heads(q_ref, k_list, v_list, o_ref, stack):
    gq = H_C // KVH_C
    for kh0 in range(0, KVH_C, stack):
        sls = [slice(kh * HD, (kh + 1) * HD) for kh in range(kh0, kh0 + stack)]
        o = _gqa_groups(q_ref, kh0 * gq * HD, gq, [[k[:, sl] for k in k_list] for sl in sls],
                        [[v[:, sl] for v in v_list] for sl in sls])
        o_ref[:, kh0 * gq * HD:(kh0 + stack) * gq * HD] = o.astype(o_ref.dtype)


def _attn_c_prompt_kernel(q_ref, k_ref, v_ref, o_ref):
    _gqa_heads(q_ref, [k_ref[...].astype(BF16)], [v_ref[...].astype(BF16)], o_ref, GQA_STACK_PROMPT)


def _attn_c_prompt(q, k, v):
    def blk(w):
        return pl.BlockSpec((SEQ, w), lambda b: (b, 0))
    return pl.pallas_call(
        _attn_c_prompt_kernel,
        grid=(BATCH,),
        in_specs=[blk(H_C * HD), blk(KVH_C * HD), blk(KVH_C * HD)],
        out_specs=blk(H_C * HD),
        out_shape=jax.ShapeDtypeStruct((N_PROMPT, H_C * HD), BF16),
        compiler_params=_params(("arbitrary",)),
        name="attn_c_prompt",
    )(q, k, v)


def _attn_c_sample_kernel(q_ref, k_ref, v_ref, kc_ref, vc_ref, o_ref):
    _gqa_heads(q_ref, [k_ref[...].astype(BF16), kc_ref[...].astype(BF16)],
               [v_ref[...].astype(BF16), vc_ref[...].astype(BF16)], o_ref, GQA_STACK_SAMPLE)


def _attn_c_sample(q, k, v, k_ctx, v_ctx):
    qoff = N_PROMPT // BQ_S
    nq = DEC_SEQ // BQ_S
    w = KVH_C * HD
    return pl.pallas_call(
        _attn_c_sample_kernel,
        grid=(DEC_BATCH, nq),
        in_specs=[pl.BlockSpec((BQ_S, H_C * HD), lambda b, i: (qoff + b * nq + i, 0)),
                  pl.BlockSpec((DEC_SEQ, w), lambda b, i: (b, 0)),
                  pl.BlockSpec((DEC_SEQ, w), lambda b, i: (b, 0)),
                  pl.BlockSpec((PAST_LEN, w), lambda b, i: (b, 0)),
                  pl.BlockSpec((PAST_LEN, w), lambda b, i: (b, 0))],
        out_specs=pl.BlockSpec((BQ_S, H_C * HD), lambda b, i: (b * nq + i, 0)),
        out_shape=jax.ShapeDtypeStruct((N_SAMPLE, H_C * HD), BF16),
        compiler_params=_params(("arbitrary", "arbitrary")),
        name="attn_c_sample",
    )(q, k, v, k_ctx, v_ctx)


def _layer_norm(y, g, b):
    mu = jnp.mean(y, axis=-1, keepdims=True)
    yc = y - mu
    var = jnp.mean(yc * yc, axis=-1, keepdims=True)
    return yc * lax.rsqrt(var + EPS) * g + b


def _route(sel, aff):
    def row(a, j):
        return a[j:j + 1, :]

    scores = []
    for g in range(N_GROUPS):
        a0, a1, a2, a3 = (row(sel, 4 * g + j) for j in range(4))
        hi01, lo01 = jnp.maximum(a0, a1), jnp.minimum(a0, a1)
        hi23, lo23 = jnp.maximum(a2, a3), jnp.minimum(a2, a3)
        top1 = jnp.maximum(hi01, hi23)
        top2 = jnp.maximum(jnp.minimum(hi01, hi23), jnp.maximum(lo01, lo23))
        scores.append(top1 + top2)
    best = scores[0]
    gi = jnp.zeros(best.shape, jnp.int32)
    for g in range(1, N_GROUPS):
        better = scores[g] > best
        gi = jnp.where(better, g, gi)
        best = jnp.where(better, scores[g], best)

    def in_group(a, j):
        out = row(a, j)
        for g in range(1, N_GROUPS):
            out = jnp.where(gi == g, row(a, 4 * g + j), out)
        return out

    v = [in_group(sel, j) for j in range(4)]
    a = [in_group(aff, j) for j in range(4)]
    chosen = []
    for j in range(4):
        rank = jnp.zeros(best.shape, jnp.int32)
        for k in range(4):
            if k == j:
                continue
            ahead = (v[k] >= v[j]) if k < j else (v[k] > v[j])
            rank = rank + ahead.astype(jnp.int32)
        chosen.append(rank < 2)
    total = sum(jnp.where(chosen[j], a[j], 0.0) for j in range(4))
    w = [jnp.where(chosen[j], a[j], 0.0) / total for j in range(4)]
    pair = jnp.zeros(best.shape, jnp.int32)
    for p, (ja, jb) in enumerate(zip(SLOT_A_LOCAL, SLOT_B_LOCAL)):
        pair = jnp.where(chosen[ja] & chosen[jb], p, pair)
    gate_a = jnp.where(pair == 0, w[0], jnp.where(pair <= 2, w[2], w[3]))
    gate_b = jnp.where((pair == 0) | (pair == 1) | (pair == 4), w[1], jnp.where(pair == 5, w[2], w[0]))
    return gi * PAIRS_PER_GROUP + pair, gate_a, gate_b


def _post_attn_kernel(split_x, *refs):
    t = pl.program_id(0)
    op_ref, os_ref = refs[:2]
    if split_x:
        x = _pick(t, refs[2], refs[3])
        refs = refs[4:]
    else:
        x = refs[2][...]
        refs = refs[3:]
    (w_out_ref, gate_ref, shift_ref, scale_ref, lng_ref, lnb_ref, rw_ref, rb_ref, tri_ref,
     x1_ref, row_ref, route_ref, counts_ref, carry_ref) = refs
    o = _pick(t, op_ref, os_ref)
    subs = _sub_tiles(o.shape[0])

    a = [jnp.dot(o[s], w_out_ref[...], preferred_element_type=F32) for s in subs]
    x1 = [_layer_norm(ALPHA * x[s] + gate_ref[...] * a_s, lng_ref[...], lnb_ref[...]) for s, a_s in zip(subs, a)]
    h2 = [x1_s * (1.0 + scale_ref[...]) + shift_ref[...] for x1_s in x1]
    for s, x1_s, h2_s in zip(subs, x1, h2):
        x1_ref[s, :] = x1_s
        row_ref[s, :D_MODEL] = h2_s
    rw = rw_ref[...]
    rw_hi = rw.astype(BF16)
    rw_lo = (rw - rw_hi.astype(F32)).astype(BF16)
    rw_both = jnp.concatenate([rw_hi, rw_lo], axis=0)
    logits = []
    for h2_s in h2:
        h2_hi = h2_s.astype(BF16)
        h2_lo = (h2_s - h2_hi.astype(F32)).astype(BF16)
        by_hi = _qk(rw_both, h2_hi)
        logits.append(by_hi[:N_EXPERTS] + by_hi[N_EXPERTS:] + _qk(rw_hi, h2_lo))
    aff = [1.0 / (1.0 + jnp.exp(-l)) for l in logits]
    routed = [_route(aff_s + rb_ref[...], aff_s) for aff_s in aff]

    lane = lax.broadcasted_iota(jnp.int32, (LANES, TM), 0)
    for s, (_, gate_a, gate_b) in zip(subs, routed):
        meta_t = jnp.where(lane == 0, gate_a, jnp.where(lane == 1, gate_b, 0.0))
        row_ref[s, D_MODEL:] = meta_t.T

    @pl.when(t == 0)
    def _():
        carry_ref[...] = jnp.zeros_like(carry_ref)

    onehots = [lax.broadcasted_iota(jnp.int32, (BUCKET_ROWS, TM), 0) == bucket for bucket, _, _ in routed]
    prefixes = [jnp.dot(oh.astype(BF16), tri_ref[...], preferred_element_type=F32) for oh in onehots]
    carry = carry_ref[...]
    route_ref[...] = jnp.zeros(route_ref.shape, F32)
    for s, (bucket, _, _), onehot, prefix in zip(subs, routed, onehots, prefixes):
        rank = jnp.sum(jnp.where(onehot, prefix + carry[:, 0:1], 0.0), axis=0, keepdims=True)
        carry = carry + jnp.sum(onehot.astype(F32), axis=1, keepdims=True)
        route_ref[0:1, s] = bucket.astype(F32)
        route_ref[1:2, s] = rank
    carry_ref[...] = carry
    counts_ref[...] = carry


def _post_attn(layer, o_p, o_s, x_list, mod, w_out, ln_g, ln_b, rw_t, rb):
    split_x = len(x_list) == 2
    tok = pl.BlockSpec((TP, D_MODEL), lambda t: (t, 0))
    tri = jnp.asarray(np.arange(TM)[:, None] < np.arange(TM)[None, :], BF16)
    return pl.pallas_call(
        functools.partial(_post_attn_kernel, split_x),
        grid=(N_TOK // TP,),
        in_specs=_split_specs(TP) + (_split_specs(TP) if split_x else [tok]) + [
            _const_spec((D_MODEL, D_MODEL)), _mod_spec(layer, 2, TP), _mod_spec(layer, 3, TP),
            _mod_spec(layer, 4, TP),
            _const_spec((1, D_MODEL)), _const_spec((1, D_MODEL)), _const_spec((N_EXPERTS, D_MODEL)),
            _const_spec((N_EXPERTS, 1)), _const_spec((TM, TM))],
        out_specs=[tok, pl.BlockSpec((TP, ROW_W), lambda t: (t, 0)), pl.BlockSpec((8, TP), lambda t: (0, t)),
                   _const_spec((BUCKET_ROWS, LANES))],
        out_shape=[jax.ShapeDtypeStruct((N_TOK, D_MODEL), F32), jax.ShapeDtypeStruct((N_TOK, ROW_W), F32),
                   jax.ShapeDtypeStruct((8, N_TOK), F32), jax.ShapeDtypeStruct((BUCKET_ROWS, LANES), F32)],
        scratch_shapes=[pltpu.VMEM((BUCKET_ROWS, LANES), F32)],
        compiler_params=_params(("arbitrary",)),
        name="post_attn",
    )(o_p, o_s, *x_list, w_out, mod, mod, mod, ln_g, ln_b, rw_t, rb, tri)


def _moe_kernel(ta_ref, tb_ref, nt_ref, x_ref, wga_ref, wua_ref, wda_ref, wgb_ref, wub_ref, wdb_ref, o_ref,
                wga_s, wua_s, wda_s, wgb_s, wub_s, wdb_s):
    i = pl.program_id(0)
    prev = jnp.maximum(i - 1, 0)
    slots = ((ta_ref, (wga_ref, wua_ref, wda_ref), (wga_s, wua_s, wda_s)),
             (tb_ref, (wgb_ref, wub_ref, wdb_ref), (wgb_s, wub_s, wdb_s)))

    for t_ref, w_refs, w_scr in slots:
        @pl.when((i == 0) | (t_ref[i] != t_ref[prev]))
        def _(w_refs=w_refs, w_scr=w_scr):
            for w_ref, s_ref in zip(w_refs, w_scr):
                s_ref[...] = w_ref[...].astype(BF16)

    @pl.when(i < nt_ref[0])
    def _():
        x = x_ref[:, :D_MODEL].astype(BF16)
        acc = None
        for slot, (_, _, (wg_s, wu_s, wd_s)) in enumerate(slots):
            gt = jnp.dot(x, wg_s[...], preferred_element_type=F32)
            up = jnp.dot(x, wu_s[...], preferred_element_type=F32)
            gate = x_ref[:, D_MODEL + slot:D_MODEL + slot + 1]
            hid = (gt / (1.0 + jnp.exp(-gt))) * up * gate
            y = jnp.dot(hid.astype(BF16), wd_s[...], preferred_element_type=F32)
            acc = y if acc is None else acc + y
        o_ref[...] = acc

    @pl.when(i >= nt_ref[0])
    def _():
        o_ref[...] = jnp.zeros_like(o_ref)


def _moe(layer, tile_a, tile_b, n_tiles, rows, w_gate, w_up, w_down):
    def row_idx(i, ta, tb, nt):
        return (jnp.maximum(jnp.minimum(i, nt[0] - 1), 0), 0)

    def wspec(shape, which):
        if which == 0:
            return pl.BlockSpec((None, None) + shape, lambda i, ta, tb, nt: (layer, ta[i], 0, 0))
        return pl.BlockSpec((None, None) + shape, lambda i, ta, tb, nt: (layer, tb[i], 0, 0))

    up_shape, down_shape = (D_MODEL, D_EXPERT), (D_EXPERT, D_MODEL)
    grid_spec = pltpu.PrefetchScalarGridSpec(
        num_scalar_prefetch=3,
        grid=(MOE_TILES,),
        in_specs=[pl.BlockSpec((TME, ROW_W), row_idx),
                  wspec(up_shape, 0), wspec(up_shape, 0), wspec(down_shape, 0),
                  wspec(up_shape, 1), wspec(up_shape, 1), wspec(down_shape, 1)],
        out_specs=pl.BlockSpec((TME, D_MODEL), lambda i, ta, tb, nt: (i, 0)),
        scratch_shapes=[pltpu.VMEM(s, BF16) for s in (up_shape, up_shape, down_shape) * 2],
    )
    return pl.pallas_call(
        _moe_kernel,
        grid_spec=grid_spec,
        out_shape=jax.ShapeDtypeStruct((MOE_ROWS, D_MODEL), F32),
        compiler_params=_params(("arbitrary",)),
        name="moe_experts",
    )(tile_a, tile_b, n_tiles, rows, w_gate, w_up, w_down, w_gate, w_up, w_down)


def _row_copy(src_ref, src_row, dst_ref, dst_row, sem):
    return pltpu.make_async_copy(src_ref.at[pl.ds(src_row, 1), :], dst_ref.at[pl.ds(dst_row, 1), :], sem)


def _dispatch_kernel(pos_ref, pend_ref, cnt_ref, nt_ref, src_ref, out_ref, buf, zero_ref, in_sem, sem, zsem):
    t = pl.program_id(0)

    def fetch(tile):
        return pltpu.make_async_copy(src_ref.at[pl.ds(pl.multiple_of(tile * TM, TM), TM), :],
                                     buf.at[tile % STAGE_SLOTS], in_sem.at[tile % STAGE_SLOTS])

    @pl.when(t == 0)
    def _():
        fetch(0).start()
        zero_ref[...] = jnp.zeros_like(zero_ref)

        def zero_tile(row0):
            return pltpu.make_async_copy(zero_ref, out_ref.at[pl.ds(pl.multiple_of(row0, TME), TME), :], zsem)

        for b in range(N_BUCKETS):
            @pl.when(cnt_ref[b] > 0)
            def _(b=b):
                zero_tile(pend_ref[b] - TME).start()

        def start_unused(i, carry):
            zero_tile(i * TME).start()
            return carry
        lax.fori_loop(nt_ref[0], MOE_TILES, start_unused, 0)

        for b in range(N_BUCKETS):
            @pl.when(cnt_ref[b] > 0)
            def _(b=b):
                zero_tile(pend_ref[b] - TME).wait()

        def wait_unused(i, carry):
            zero_tile(i * TME).wait()
            return carry
        lax.fori_loop(nt_ref[0], MOE_TILES, wait_unused, 0)

    @pl.when(t + 1 < N_TILES)
    def _():
        fetch(t + 1).start()

    fetch(t).wait()
    tile_ref = buf.at[t % STAGE_SLOTS]

    for r in range(TM):
        _row_copy(tile_ref, r, out_ref, pos_ref[t * TM + r], sem.at[t % 2]).start(priority=r % 2)

    def drain(tile):
        pltpu.make_async_copy(buf.at[0], out_ref.at[pl.ds(0, TM), :], sem.at[tile % 2]).wait()

    @pl.when(t > 0)
    def _():
        drain(t - 1)

    @pl.when(t == N_TILES - 1)
    def _():
        drain(t)


def _dispatch(pos, pend, counts, n_tiles, rows):
    return pl.pallas_call(
        _dispatch_kernel,
        grid_spec=pltpu.PrefetchScalarGridSpec(
            num_scalar_prefetch=4, grid=(N_TILES,),
            in_specs=[pl.BlockSpec(memory_space=pl.ANY)],
            out_specs=pl.BlockSpec(memory_space=pl.ANY),
            scratch_shapes=[pltpu.VMEM((STAGE_SLOTS, TM, ROW_W), F32), pltpu.VMEM((TME, ROW_W), F32),
                            pltpu.SemaphoreType.DMA((STAGE_SLOTS,)), pltpu.SemaphoreType.DMA((2,)),
                            pltpu.SemaphoreType.DMA]),
        out_shape=jax.ShapeDtypeStruct((MOE_ROWS, ROW_W), F32),
        compiler_params=_params(("arbitrary",)),
        name="dispatch_rows",
    )(pos, pend, counts, n_tiles, rows)


def _plan(route, counts):
    bucket = route[0].astype(jnp.int32)
    rank = route[1].astype(jnp.int32)
    counts = counts[:N_BUCKETS, 0].astype(jnp.int32)
    padded = ((counts + TME - 1) // TME) * TME
    pend = jnp.cumsum(padded)
    pstart = pend - padded
    ids = jnp.arange(N_BUCKETS, dtype=jnp.int32)
    pos = rank + jnp.sum(jnp.where(bucket[None, :] == ids[:, None], pstart[:, None], 0), axis=0)
    n_tiles = pend[-1] // TME
    tile_start = jnp.minimum(jnp.arange(MOE_TILES, dtype=jnp.int32), n_tiles - 1) * TME
    tile_bucket = jnp.minimum(jnp.sum(tile_start[:, None] >= pend[None, :], axis=1), N_BUCKETS - 1)
    group, pair = tile_bucket // PAIRS_PER_GROUP, tile_bucket % PAIRS_PER_GROUP
    slot_a = jnp.asarray(SLOT_A_LOCAL, jnp.int32)
    slot_b = jnp.asarray(SLOT_B_LOCAL, jnp.int32)
    pair_hot = pair[:, None] == jnp.arange(PAIRS_PER_GROUP, dtype=jnp.int32)[None, :]
    tile_a = group * EXPERTS_PER_GROUP + jnp.sum(jnp.where(pair_hot, slot_a[None, :], 0), axis=1)
    tile_b = group * EXPERTS_PER_GROUP + jnp.sum(jnp.where(pair_hot, slot_b[None, :], 0), axis=1)
    return pos, pend, counts, tile_a.astype(jnp.int32), tile_b.astype(jnp.int32), n_tiles.reshape(1)


def _post_moe_kernel(split_out, pos_ref, f_ref, x1_ref, gate_ref, lng_ref, lnb_ref, *refs):
    out_refs, (fbuf, sem) = refs[:-2], refs[-2:]
    t = pl.program_id(0)

    def wait_tile(slot):
        pltpu.make_async_copy(f_ref.at[pl.ds(0, TM), :], fbuf.at[slot], sem.at[slot]).wait()

    @pl.when(t == 0)
    def _():
        def issue(r, carry):
            _row_copy(f_ref, pos_ref[r], fbuf.at[r // TM], r % TM, sem.at[r // TM]).start()
            return carry
        lax.fori_loop(0, (GATHER_SLOTS - 1) * TM, issue, 0, unroll=8)

    slot = t % GATHER_SLOTS
    wait_tile(slot)
    ahead = jnp.minimum(t + GATHER_SLOTS - 1, N_TILES - 1)
    aslot = (t + GATHER_SLOTS - 1) % GATHER_SLOTS
    for r in range(TM):
        _row_copy(f_ref, pos_ref[ahead * TM + r], fbuf.at[aslot], r, sem.at[aslot]).start(priority=r % 2)
    y = _layer_norm(ALPHA * x1_ref[...] + gate_ref[...] * fbuf[slot], lng_ref[...], lnb_ref[...])

    @pl.when(t == N_TILES - 1)
    def _():
        for k in range(1, GATHER_SLOTS):
            wait_tile((t + k) % GATHER_SLOTS)
    if split_out:
        @pl.when(t < PROMPT_TILES)
        def _():
            out_refs[0][...] = y

        @pl.when(t >= PROMPT_TILES)
        def _():
            out_refs[1][...] = y
    else:
        out_refs[0][...] = y


def _post_moe(layer, pos, f_sorted, x1, mod, ln_g, ln_b, split_out):
    tok = pl.BlockSpec((TM, D_MODEL), lambda t, p: (t, 0))
    if split_out:
        out_specs = [pl.BlockSpec((TM, D_MODEL), lambda t, p: (jnp.minimum(t, PROMPT_TILES - 1), 0)),
                     pl.BlockSpec((TM, D_MODEL), lambda t, p: (jnp.maximum(t - PROMPT_TILES, 0), 0))]
        out_shape = [jax.ShapeDtypeStruct((N_PROMPT, D_MODEL), F32), jax.ShapeDtypeStruct((N_SAMPLE, D_MODEL), F32)]
    else:
        out_specs = [tok]
        out_shape = [jax.ShapeDtypeStruct((N_TOK, D_MODEL), F32)]
    mod_spec = pl.BlockSpec((None, None, None, 1, D_MODEL), lambda t, p: (layer, _mod_row(t), 5, 0, 0))
    return pl.pallas_call(
        functools.partial(_post_moe_kernel, split_out),
        grid_spec=pltpu.PrefetchScalarGridSpec(
            num_scalar_prefetch=1, grid=(N_TILES,),
            in_specs=[pl.BlockSpec(memory_space=pl.ANY), tok, mod_spec,
                      pl.BlockSpec((1, D_MODEL), lambda t, p: (0, 0)), pl.BlockSpec((1, D_MODEL), lambda t, p: (0, 0))],
            out_specs=out_specs,
            scratch_shapes=[pltpu.VMEM((GATHER_SLOTS, TM, D_MODEL), F32),
                            pltpu.SemaphoreType.DMA((GATHER_SLOTS,))]),
        out_shape=out_shape,
        compiler_params=_params(("arbitrary",)),
        name="post_moe",
    )(pos, f_sorted, x1, mod, ln_g, ln_b)


def _ffn(layer, o_p, o_s, x_list, mod, w_out, ln_g, ln_b, rw_t, rb, w_gate, w_up, w_down, split_out):
    x1, rows, route, counts = _post_attn(layer, o_p, o_s, x_list, mod, w_out, ln_g[layer, 0][None],
                                         ln_b[layer, 0][None], rw_t, rb)
    pos, pend, counts, tile_a, tile_b, n_tiles = _plan(route, counts)
    rows_sorted = _dispatch(pos, pend, counts, n_tiles, rows)
    f_sorted = _moe(layer, tile_a, tile_b, n_tiles, rows_sorted, w_gate, w_up, w_down)
    return _post_moe(layer, pos, f_sorted, x1, mod, ln_g[layer, 1][None], ln_b[layer, 1][None], split_out)


def kernel(x_prompt, x_sample, c, cache_mla_ckv, cache_mla_kpe, cache_swa_k, cache_swa_v, cache_gqa_k, cache_gqa_v, c_ctx, w_mod, b_mod, ln_g, ln_b, w_in_ab, mla_q_norm, mla_w_uq, mla_kv_norm, mla_w_ukv, swa_sink, w_out_ab, w_in_c, gqa_q_norm, gqa_k_norm, w_out_c, router_w, router_bias, exp_w_gate, exp_w_up, exp_w_down):
    xp = x_prompt.reshape(N_PROMPT, D_MODEL)
    xs = x_sample.reshape(N_SAMPLE, D_MODEL)
    cond = jnp.concatenate([c_ctx[None], c, jnp.zeros((MOD_ROWS - 1 - DEC_BATCH, D_MODEL), F32)], axis=0)
    mod = _modulation(cond, w_mod, b_mod).reshape(DEPTH, MOD_ROWS, 6, 1, D_MODEL)
    tabs = _rope_tables()
    rw_t = router_w.T
    rb = router_bias.reshape(N_EXPERTS, 1)

    w = w_in_ab[0]
    pad = jnp.zeros((D_MODEL, AB_COLS - w.shape[1]), F32)
    w_in = jnp.concatenate([w[:, :384], w[:, 416:], w[:, 384:416], pad], axis=1).astype(BF16)
    w_uq, w_kn, w_v, place = _mla_weights(mla_w_uq[0], mla_w_ukv[0])
    qa, ka, va, qb, ckv_p, kpe_p, kbt_p, vbt_p, kb_s, vb_s = _inproj_ab(
        xp, xs, mod, w_in, mla_q_norm[0][None], mla_kv_norm[0][None], w_uq, w_kn, w_v, place, tabs)
    sink = swa_sink[0]
    o_p = _attn_ab_prompt(sink, qa, ka, va, qb, kbt_p, vbt_p)
    kpe_ctx = jnp.pad(cache_mla_kpe[:, 0].reshape(-1, ROPE_A), ((0, 0), (0, LANES - ROPE_A)))
    o_s = _attn_ab_sample(sink, qa, ka, va, qb, kb_s, vb_s,
                          cache_mla_ckv[:, 0].reshape(-1, KV_LORA), kpe_ctx,
                          cache_swa_k[:, 0].reshape(-1, KVH_B * HD), cache_swa_v[:, 0].reshape(-1, KVH_B * HD),
                          w_kn, w_v, place)
    (x2,) = _ffn(0, o_p, o_s, [xp, xs], mod, w_out_ab[0].astype(BF16), ln_g, ln_b, rw_t, rb,
                 exp_w_gate, exp_w_up, exp_w_down, split_out=False)

    g_full = jnp.concatenate([jnp.tile(gqa_q_norm[0], H_C), jnp.tile(gqa_k_norm[0], KVH_C)])[None]
    qc, kc_p, vc_p, kc_s, vc_s = _inproj_c(x2, mod, w_in_c[0].astype(BF16), g_full, tabs)
    oc_p = _attn_c_prompt(qc, kc_p, vc_p)
    oc_s = _attn_c_sample(qc, kc_s, vc_s, cache_gqa_k[:, 0].reshape(-1, KVH_C * HD),
                          cache_gqa_v[:, 0].reshape(-1, KVH_C * HD))
    y_p, y_s = _ffn(1, oc_p, oc_s, [x2], mod, w_out_c[0].astype(BF16), ln_g, ln_b, rw_t, rb,
                    exp_w_gate, exp_w_up, exp_w_down, split_out=True)

    y_prompt = y_p.reshape(BATCH, SEQ, D_MODEL)
    y_sample = y_s.reshape(DEC_BATCH, DEC_SEQ, D_MODEL)
    new_ckv = ckv_p.reshape(BATCH, 1, SEQ, KV_LORA)
    new_kpe = kpe_p.reshape(BATCH, 1, SEQ, ROPE_A)
    new_swk = kbt_p.reshape(BATCH, 1, KVH_B, HD, SEQ).transpose(0, 1, 4, 2, 3)
    new_swv = vbt_p.reshape(BATCH, 1, KVH_B, HD, SEQ).transpose(0, 1, 4, 2, 3)
    new_gk = kc_p.reshape(BATCH, 1, SEQ, KVH_C, HD)
    new_gv = vc_p.reshape(BATCH, 1, SEQ, KVH_C, HD)
    return y_prompt, y_sample, new_ckv, new_kpe, new_swk, new_swv, new_gk, new_gv
```

```python
import functools

import numpy as np
import jax
import jax.numpy as jnp
from jax import lax
from jax.experimental import pallas as pl
from jax.experimental.pallas import tpu as pltpu

D_MODEL = 1024
BATCH = 32
SEQ = 256
DEPTH = 2
DEC_BATCH = 4
DEC_SEQ = 1024
PAST_LEN = 256
GRID_W = 64
ROPE_THETA = 10000.0
WINDOW = 128
HD = 64
H_A = 8
NOPE_A = 64
ROPE_A = 32
V_A = 64
Q_LORA = 256
KV_LORA = 128
H_B = 8
KVH_B = 2
H_C = 16
KVH_C = 4
N_EXPERTS = 16
N_GROUPS = 4
EXPERTS_PER_GROUP = 4
D_EXPERT = 512
ALPHA = (2 * DEPTH) ** 0.25
NEG_INF = -1e30
EPS = 1e-6

LANES = 128
N_PROMPT = BATCH * SEQ
N_SAMPLE = DEC_BATCH * DEC_SEQ
N_TOK = N_PROMPT + N_SAMPLE
TM = 256
TP = 512
N_TILES = N_TOK // TM
PROMPT_TILES = N_PROMPT // TM
BQ_S = 256
LOG2E = 1.4426950408889634
MLA_STACK_PROMPT = 4
MLA_STACK_SAMPLE = 4
GQA_STACK_PROMPT = 4
GQA_STACK_SAMPLE = 2
MOD_ROWS = 8

PAIRS_PER_GROUP = 6
N_BUCKETS = N_GROUPS * PAIRS_PER_GROUP
SLOT_A_LOCAL = (0, 2, 2, 3, 3, 3)
SLOT_B_LOCAL = (1, 1, 0, 0, 1, 2)
TME = 256
MOE_TILES = -(-(N_TOK + N_BUCKETS * (TME - 1)) // TME)
MOE_ROWS = MOE_TILES * TME

BUCKET_ROWS = 32
ROW_W = D_MODEL + LANES
POST_STEPS = N_TOK // TP
GATHER_SLOTS = 3

AB_COLS = 1280
C_NORM_COLS = H_C * HD + KVH_C * HD
NORM_BLOCK = 256

BF16 = jnp.bfloat16
F32 = jnp.float32
VMEM_LIMIT = 52 * 1024 * 1024


def _mod_row(t, tile=TM):
    p_tiles = N_PROMPT // tile
    return jnp.where(t < p_tiles, 0, 1 + (t - p_tiles) // (DEC_SEQ // tile))


def _mod_spec(layer, chunk, tile=TM):
    return pl.BlockSpec((None, None, None, 1, D_MODEL), lambda t: (layer, _mod_row(t, tile), chunk, 0, 0))


def _const_spec(shape):
    nd = len(shape)
    return pl.BlockSpec(shape, lambda *_: (0,) * nd)


def _split_specs(tile=TM):
    p_tiles = N_PROMPT // tile
    return [
        pl.BlockSpec((tile, D_MODEL), lambda t: (jnp.minimum(t, p_tiles - 1), 0)),
        pl.BlockSpec((tile, D_MODEL), lambda t: (jnp.maximum(t - p_tiles, 0), 0)),
    ]


def _group_spec(tile, width, prompt):
    p_tiles = N_PROMPT // tile
    if prompt:
        return pl.BlockSpec((tile, width), lambda t: (jnp.minimum(t, p_tiles - 1), 0))
    return pl.BlockSpec((tile, width), lambda t: (jnp.maximum(t - p_tiles, 0), 0))


def _pick(t, a_ref, b_ref):
    return jnp.where(t < N_PROMPT // a_ref.shape[0], a_ref[...], b_ref[...])


def _sub_tiles(rows):
    return [slice(i * TM, (i + 1) * TM) for i in range(rows // TM)]


def _params(sem):
    return pltpu.CompilerParams(dimension_semantics=sem, vmem_limit_bytes=VMEM_LIMIT)


def _mod_kernel(cond_ref, w_ref, b_ref, o_ref):
    c = cond_ref[...]
    s = (c / (1.0 + jnp.exp(-c))).astype(BF16)
    o_ref[...] = jnp.dot(s, w_ref[...].astype(BF16), preferred_element_type=F32) + b_ref[...]


def _modulation(cond, w_mod, b_mod):
    tn = 1536
    return pl.pallas_call(
        _mod_kernel,
        grid=(DEPTH, 6 * D_MODEL // tn),
        in_specs=[
            pl.BlockSpec((MOD_ROWS, D_MODEL), lambda l, j: (0, 0)),
            pl.BlockSpec((None, D_MODEL, tn), lambda l, j: (l, 0, j)),
            pl.BlockSpec((None, 1, tn), lambda l, j: (l, 0, j)),
        ],
        out_specs=pl.BlockSpec((None, MOD_ROWS, tn), lambda l, j: (l, 0, j)),
        out_shape=jax.ShapeDtypeStruct((DEPTH, MOD_ROWS, 6 * D_MODEL), F32),
        compiler_params=_params(("arbitrary", "arbitrary")),
        name="modulation",
    )(cond, w_mod, b_mod.reshape(DEPTH, 1, 6 * D_MODEL))


def _grid_cos_sin(half):
    n_freq = half // 2
    rows = DEC_SEQ // GRID_W
    row = jnp.repeat(jnp.arange(rows, dtype=F32), GRID_W)
    col = jnp.tile(jnp.arange(GRID_W, dtype=F32), rows)
    inv = 1.0 / (ROPE_THETA ** (jnp.arange(n_freq, dtype=F32) / n_freq))
    ang = jnp.concatenate([row[:, None] * inv, col[:, None] * inv], axis=-1)
    return jnp.cos(ang), jnp.sin(ang)


def _rope_tables():
    tabs = []
    for half in (HD // 2, ROPE_A // 2):
        cos, sin = _grid_cos_sin(half)
        reps = LANES // (2 * half)
        tabs.append(jnp.tile(jnp.concatenate([cos, cos], axis=-1), (1, reps)))
        tabs.append(jnp.tile(jnp.concatenate([-sin, sin], axis=-1), (1, reps)))
    half = ROPE_A // 2
    one = jnp.ones((DEC_SEQ, NOPE_A), F32)
    zero = jnp.zeros((DEC_SEQ, NOPE_A), F32)
    pad1 = jnp.ones((DEC_SEQ, LANES - NOPE_A - ROPE_A), F32)
    pad0 = jnp.zeros((DEC_SEQ, LANES - NOPE_A - ROPE_A), F32)
    zh = jnp.zeros((DEC_SEQ, half), F32)
    tabs.append(jnp.concatenate([one, cos, cos, pad1], axis=-1))
    tabs.append(jnp.concatenate([zero, -sin, zh, pad0], axis=-1))
    tabs.append(jnp.concatenate([zero, zh, sin, pad0], axis=-1))
    return tuple(tabs)


def _rope(x, c, s, half):
    w = x.shape[1]
    reps = w // LANES
    if reps > 1:
        c = jnp.concatenate([c] * reps, axis=1)
        s = jnp.concatenate([s] * reps, axis=1)
    ahead = pltpu.roll(x, w - half, 1)
    behind = pltpu.roll(x, half, 1)
    lane = lax.broadcasted_iota(jnp.int32, x.shape, 1)
    swapped = jnp.where((lane & (2 * half - 1)) < half, ahead, behind)
    return x * c + swapped * s


def _rope_specs(n, tile):
    p_tiles = N_PROMPT // tile

    def idx(t):
        return (jnp.maximum(t - p_tiles, 0) % (DEC_SEQ // tile), 0)
    return [pl.BlockSpec((tile, LANES), idx)] * n


def _rms(x, g_ref):
    return x * lax.rsqrt(jnp.mean(x * x, axis=-1, keepdims=True) + EPS) * g_ref[...]


def _inproj_ab_kernel(xp_ref, xs_ref, shift_ref, scale_ref, w_in_ref, qn_ref, kvn_ref, w_uq_ref, w_kn_ref, w_v_ref,
                      place_ref, c64_ref, s64_ref, c32_ref, s32_ref, ca_ref, sa_ref, sb_ref,
                      qa_ref, ka_ref, va_ref, qb_ref, ckv_p_ref, kpe_p_ref, kbt_p_ref, vbt_p_ref, kb_s_ref, vb_s_ref):
    t = pl.program_id(0)
    x = _pick(t, xp_ref, xs_ref)
    subs = _sub_tiles(x.shape[0])
    h = [(x[s] * (1.0 + scale_ref[...]) + shift_ref[...]).astype(BF16) for s in subs]
    z = [jnp.dot(h_s, w_in_ref[...], preferred_element_type=F32) for h_s in h]
    ql = [_rms(z_s[:, :Q_LORA], qn_ref).astype(BF16) for z_s in z]
    qa = [jnp.dot(ql_s, w_uq_ref[...], preferred_element_type=F32) for ql_s in ql]
    ckv = [_rms(z_s[:, Q_LORA:Q_LORA + KV_LORA], kvn_ref) for z_s in z]
    k_nope = [jnp.dot(c_s.astype(BF16), w_kn_ref[...], preferred_element_type=F32) for c_s in ckv]
    va = [jnp.dot(c_s.astype(BF16), w_v_ref[...], preferred_element_type=F32).astype(BF16) for c_s in ckv]
    for s, va_s in zip(subs, va):
        va_ref[s, :] = va_s
    qb = [z_s[:, 384:896] for z_s in z]
    kb = [z_s[:, 896:1024] for z_s in z]
    vb = [z_s[:, 1024:1152] for z_s in z]
    kpe = [z_s[:, 1152:1280] for z_s in z]

    def finish(qa_out, qb_out, kpe_out):
        k_rope = [jnp.dot(kpe_s.astype(BF16), place_ref[...], preferred_element_type=F32) for kpe_s in kpe_out]
        for i, s in enumerate(subs):
            qa_ref[s, :] = qa_out[i].astype(BF16)
            qb_ref[s, :] = qb_out[i].astype(BF16)
            ka_ref[s, :] = (k_nope[i] + k_rope[i]).astype(BF16)

    p_tiles = N_PROMPT // x.shape[0]

    @pl.when(t < p_tiles)
    def _():
        finish(qa, qb, kpe)
        for i, s in enumerate(subs):
            ckv_p_ref[s, :] = ckv[i]
            kpe_p_ref[s, :] = kpe[i][:, :ROPE_A]
            kbt_p_ref[i] = kb[i].T
            vbt_p_ref[i] = vb[i].T

    @pl.when(t >= p_tiles)
    def _():
        w = qa[0].shape[1]
        reps = w // LANES
        half = ROPE_A // 2
        qa_rot, qb_rot, kb_rot, kpe_rot = [], [], [], []
        for i, s in enumerate(subs):
            ca = jnp.concatenate([ca_ref[s, :]] * reps, axis=1)
            sa = jnp.concatenate([sa_ref[s, :]] * reps, axis=1)
            sb = jnp.concatenate([sb_ref[s, :]] * reps, axis=1)
            qa_rot.append(qa[i] * ca + pltpu.roll(qa[i], w - half, 1) * sa + pltpu.roll(qa[i], half, 1) * sb)
            qb_rot.append(_rope(qb[i], c64_ref[s, :], s64_ref[s, :], HD // 2))
            kb_rot.append(_rope(kb[i], c64_ref[s, :], s64_ref[s, :], HD // 2))
            kpe_rot.append(_rope(kpe[i], c32_ref[s, :], s32_ref[s, :], ROPE_A // 2))
        finish(qa_rot, qb_rot, kpe_rot)
        for i, s in enumerate(subs):
            kb_s_ref[s, :] = kb_rot[i]
            vb_s_ref[s, :] = vb[i]


def _mla_weights(w_uq, w_ukv):
    pad = LANES - NOPE_A - ROPE_A
    uq = w_uq.reshape(Q_LORA, H_A, NOPE_A + ROPE_A)
    uq = jnp.pad(uq, ((0, 0), (0, 0), (0, pad))).reshape(Q_LORA, H_A * LANES)
    ukv = w_ukv.reshape(KV_LORA, H_A, NOPE_A + V_A)
    kn = jnp.pad(ukv[:, :, :NOPE_A], ((0, 0), (0, 0), (0, LANES - NOPE_A))).reshape(KV_LORA, H_A * LANES)
    wv = ukv[:, :, NOPE_A:].reshape(KV_LORA, H_A * V_A)
    src = np.arange(LANES)[:, None]
    dst = np.arange(H_A * LANES)[None, :] % LANES
    place = (src < ROPE_A) & (dst == src + NOPE_A)
    return uq.astype(BF16), kn.astype(BF16), wv.astype(BF16), jnp.asarray(place, BF16)


def _inproj_ab(xp, xs, mod, w_in, q_norm, kv_norm, w_uq, w_kn, w_v, place, tabs):
    widths = [(H_A * LANES, BF16), (H_A * LANES, BF16), (H_A * V_A, BF16), (H_B * HD, BF16)]
    out_specs = [pl.BlockSpec((TP, w), lambda t: (t, 0)) for w, _ in widths]
    out_shape = [jax.ShapeDtypeStruct((N_TOK, w), dt) for w, dt in widths]
    for w in (KV_LORA, ROPE_A):
        out_specs.append(_group_spec(TP, w, prompt=True))
        out_shape.append(jax.ShapeDtypeStruct((N_PROMPT, w), F32))
    assert SEQ == TM
    p_tiles = N_PROMPT // TP
    for _ in range(2):
        out_specs.append(pl.BlockSpec((TP // SEQ, KVH_B * HD, SEQ), lambda t: (jnp.minimum(t, p_tiles - 1), 0, 0)))
        out_shape.append(jax.ShapeDtypeStruct((BATCH, KVH_B * HD, SEQ), F32))
    for w in (KVH_B * HD, KVH_B * HD):
        out_specs.append(_group_spec(TP, w, prompt=False))
        out_shape.append(jax.ShapeDtypeStruct((N_SAMPLE, w), F32))
    return pl.pallas_call(
        _inproj_ab_kernel,
        grid=(N_TOK // TP,),
        in_specs=_split_specs(TP) + [_mod_spec(0, 0, TP), _mod_spec(0, 1, TP), _const_spec((D_MODEL, AB_COLS)),
                                     _const_spec((1, Q_LORA)), _const_spec((1, KV_LORA)),
                                     _const_spec((Q_LORA, H_A * LANES)), _const_spec((KV_LORA, H_A * LANES)),
                                     _const_spec((KV_LORA, H_A * V_A)), _const_spec((LANES, H_A * LANES))]
        + _rope_specs(7, TP),
        out_specs=out_specs,
        out_shape=out_shape,
        compiler_params=_params(("arbitrary",)),
        name="inproj_ab",
    )(xp, xs, mod, mod, w_in, q_norm, kv_norm, w_uq, w_kn, w_v, place, *tabs)


def _inproj_c_kernel(x_ref, shift_ref, scale_ref, w_in_ref, g_ref, ones_ref, c64_ref, s64_ref,
                     q_ref, k_p_ref, v_p_ref, k_s_ref, v_s_ref):
    t = pl.program_id(0)
    subs = _sub_tiles(x_ref.shape[0])
    h = [(x_ref[s, :] * (1.0 + scale_ref[...]) + shift_ref[...]).astype(BF16) for s in subs]
    z = [jnp.dot(h_s, w_in_ref[...], preferred_element_type=F32) for h_s in h]
    qk = [z_s[:, :C_NORM_COLS] for z_s in z]
    sq = [qk_s * qk_s for qk_s in qk]
    sq_hi = [sq_s.astype(BF16) for sq_s in sq]
    sq_lo = [(sq_s - hi_s.astype(F32)).astype(BF16) for sq_s, hi_s in zip(sq, sq_hi)]
    ss = []
    for hi_s, lo_s in zip(sq_hi, sq_lo):
        blocks = []
        for j in range(C_NORM_COLS // NORM_BLOCK):
            sl = slice(j * NORM_BLOCK, (j + 1) * NORM_BLOCK)
            blocks.append(jnp.dot(hi_s[:, sl], ones_ref[...], preferred_element_type=F32)
                          + jnp.dot(lo_s[:, sl], ones_ref[...], preferred_element_type=F32))
        ss.append(jnp.concatenate(blocks, axis=1))
    qk = [qk_s * lax.rsqrt(ss_s * (1.0 / HD) + EPS) * g_ref[...] for qk_s, ss_s in zip(qk, ss)]
    p_tiles = N_PROMPT // x_ref.shape[0]

    @pl.when(t < p_tiles)
    def _():
        for s, qk_s, z_s in zip(subs, qk, z):
            q_ref[s, :] = qk_s[:, :H_C * HD].astype(BF16)
            k_p_ref[s, :] = qk_s[:, H_C * HD:]
            v_p_ref[s, :] = z_s[:, C_NORM_COLS:]

    @pl.when(t >= p_tiles)
    def _():
        for s, qk_s, z_s in zip(subs, qk, z):
            q_ref[s, :] = _rope(qk_s[:, :H_C * HD], c64_ref[s, :], s64_ref[s, :], HD // 2).astype(BF16)
            k_s_ref[s, :] = _rope(qk_s[:, H_C * HD:], c64_ref[s, :], s64_ref[s, :], HD // 2)
            v_s_ref[s, :] = z_s[:, C_NORM_COLS:]


def _inproj_c(x, mod, w_in, g_full, tabs):
    def tok(w):
        return pl.BlockSpec((TP, w), lambda t: (t, 0))
    kv_w = KVH_C * HD
    head = np.arange(NORM_BLOCK) // HD
    ones = jnp.asarray(head[:, None] == head[None, :], BF16)
    return pl.pallas_call(
        _inproj_c_kernel,
        grid=(N_TOK // TP,),
        in_specs=[tok(D_MODEL), _mod_spec(1, 0, TP), _mod_spec(1, 1, TP),
                  _const_spec((D_MODEL, C_NORM_COLS + KVH_C * HD)),
                  _const_spec((1, C_NORM_COLS)), _const_spec((NORM_BLOCK, NORM_BLOCK))] + _rope_specs(2, TP),
        out_specs=[tok(H_C * HD), _group_spec(TP, kv_w, True), _group_spec(TP, kv_w, True),
                   _group_spec(TP, kv_w, False), _group_spec(TP, kv_w, False)],
        out_shape=[jax.ShapeDtypeStruct((N_TOK, H_C * HD), BF16),
                   jax.ShapeDtypeStruct((N_PROMPT, kv_w), F32), jax.ShapeDtypeStruct((N_PROMPT, kv_w), F32),
                   jax.ShapeDtypeStruct((N_SAMPLE, kv_w), F32), jax.ShapeDtypeStruct((N_SAMPLE, kv_w), F32)],
        compiler_params=_params(("arbitrary",)),
        name="inproj_c",
    )(x, mod, mod, w_in, g_full, ones, *tabs[:2])


def _qk(q, k):
    return lax.dot_general(q, k, (((1,), (1,)), ((), ())), preferred_element_type=F32)


def _softmax_pv(scores, values, scale, sink=None, kv_t=False):
    c = scale * LOG2E
    m = scores[0].max(axis=-1, keepdims=True)
    for s in scores[1:]:
        m = jnp.maximum(m, s.max(axis=-1, keepdims=True))
    m = m * c
    if sink is not None:
        sink = sink * LOG2E
        m = jnp.maximum(m, sink)
    den = None
    out = None
    for s, v in zip(scores, values):
        p = jnp.exp2(s * c - m)
        d = p.sum(axis=-1, keepdims=True)
        p = p.astype(BF16)
        if isinstance(v, (list, tuple)):
            rows = p.shape[0] // len(v)
            o = jnp.concatenate([_qk(p[b * rows:(b + 1) * rows], v_b) if kv_t
                                 else jnp.dot(p[b * rows:(b + 1) * rows], v_b, preferred_element_type=F32)
                                 for b, v_b in enumerate(v)], axis=0)
        else:
            o = _qk(p, v) if kv_t else jnp.dot(p, v, preferred_element_type=F32)
        den = d if den is None else den + d
        out = o if out is None else out + o
    if sink is not None:
        den = den + jnp.exp2(sink - m)
    return out * (1.0 / den)


def _mla_heads(qa_ref, ka_refs, va_refs, o_ref, stack):
    scale = (NOPE_A + ROPE_A) ** -0.5
    rows = qa_ref.shape[0]
    for h0 in range(0, H_A, stack):
        heads = range(h0, h0 + stack)
        scores = [jnp.concatenate([_qk(qa_ref[:, h * LANES:(h + 1) * LANES], ka_ref[:, h * LANES:(h + 1) * LANES])
                                   for h in heads], axis=0) for ka_ref in ka_refs]
        values = [[va_ref[:, h * V_A:(h + 1) * V_A] for h in heads] for va_ref in va_refs]
        o = _softmax_pv(scores, values, scale)
        o = jnp.concatenate([o[b * rows:(b + 1) * rows] for b in range(stack)], axis=1)
        o_ref[:, h0 * V_A:(h0 + stack) * V_A] = o.astype(o_ref.dtype)


def _gqa_groups(q_ref, col0, n_q, ks, vs, mask=None, sink=None, kv_t=False):
    rows = q_ref.shape[0]
    n_groups = len(ks)
    n_seg = len(ks[0])
    per_group = []
    for g in range(n_groups):
        base = col0 + g * n_q * HD
        q = jnp.concatenate([q_ref[:, base + i * HD:base + (i + 1) * HD] for i in range(n_q)], axis=0)
        if kv_t:
            per_group.append([jnp.dot(q, k, preferred_element_type=F32) for k in ks[g]])
        else:
            per_group.append([_qk(q, k) for k in ks[g]])
    scores = [jnp.concatenate([per_group[g][j] for g in range(n_groups)], axis=0) for j in range(n_seg)]
    values = [[vs[g][j] for g in range(n_groups)] for j in range(n_seg)]
    if mask is not None:
        scores[0] = jnp.where(mask, scores[0], NEG_INF)
    o = _softmax_pv(scores, values, HD ** -0.5, sink, kv_t)
    return jnp.concatenate([o[i * rows:(i + 1) * rows] for i in range(n_groups * n_q)], axis=1)


def _stacked_sink(sink_ref, h0, n_q, rows):
    head = lax.broadcasted_iota(jnp.int32, (n_q * rows, 1), 0) // rows
    col = jnp.full((n_q * rows, 1), sink_ref[h0], F32)
    for g in range(1, n_q):
        col = jnp.where(head == g, sink_ref[h0 + g], col)
    return col


def _attn_ab_prompt_kernel(sink_ref, qa_ref, ka_ref, va_ref, qb_ref, kbt_ref, vbt_ref, o_ref):
    _mla_heads(qa_ref, [ka_ref], [va_ref], o_ref, MLA_STACK_PROMPT)
    kbt = kbt_ref[...].astype(BF16)
    vbt = vbt_ref[...].astype(BF16)
    sls = [slice(kh * HD, (kh + 1) * HD) for kh in range(KVH_B)]
    o = _gqa_groups(qb_ref, 0, H_B // KVH_B, [[kbt[sl, :]] for sl in sls], [[vbt[sl, :]] for sl in sls],
                    sink=_stacked_sink(sink_ref, 0, H_B, SEQ), kv_t=True)
    o_ref[:, H_A * V_A:] = o.astype(o_ref.dtype)


def _attn_ab_prompt(sink, qa, ka, va, qb, kbt, vbt):
    def blk(w):
        return pl.BlockSpec((SEQ, w), lambda b: (b, 0))
    tblk = pl.BlockSpec((None, KVH_B * HD, SEQ), lambda b: (b, 0, 0))
    return pl.pallas_call(
        _attn_ab_prompt_kernel,
        grid=(BATCH,),
        in_specs=[pl.BlockSpec(memory_space=pltpu.SMEM), blk(H_A * LANES), blk(H_A * LANES), blk(H_A * V_A),
                  blk(H_B * HD), tblk, tblk],
        out_specs=blk(D_MODEL),
        out_shape=jax.ShapeDtypeStruct((N_PROMPT, H_A * V_A + H_B * HD), BF16),
        compiler_params=_params(("arbitrary",)),
        name="attn_ab_prompt",
    )(sink, qa, ka, va, qb, kbt, vbt)


def _attn_ab_sample_kernel(sink_ref, qa_ref, ka_ref, va_ref, qb_ref, kb_ref, vb_ref,
                           ckv_ctx_ref, kpe_ctx_ref, kb_ctx_ref, vb_ctx_ref, w_kn_ref, w_v_ref, place_ref,
                           o_ref, ka_ctx_ref, va_ctx_ref):
    i = pl.program_id(1)

    @pl.when(i == 0)
    def _():
        ckv16 = ckv_ctx_ref[...].astype(BF16)
        k_nope = jnp.dot(ckv16, w_kn_ref[...], preferred_element_type=F32)
        k_rope = jnp.dot(kpe_ctx_ref[...].astype(BF16), place_ref[...], preferred_element_type=F32)
        ka_ctx_ref[...] = (k_nope + k_rope).astype(BF16)
        va_ctx_ref[...] = jnp.dot(ckv16, w_v_ref[...], preferred_element_type=F32).astype(BF16)

    _mla_heads(qa_ref, [ka_ref, ka_ctx_ref], [va_ref, va_ctx_ref], o_ref, MLA_STACK_SAMPLE)

    n_win = BQ_S + 2 * WINDOW
    start = pl.multiple_of(jnp.clip(i * BQ_S - WINDOW, 0, DEC_SEQ - n_win), WINDOW)
    row = lax.broadcasted_iota(jnp.int32, (H_B * BQ_S, n_win), 0)
    qpos = i * BQ_S + (row & (BQ_S - 1))
    kpos = start + lax.broadcasted_iota(jnp.int32, (H_B * BQ_S, n_win), 1)
    in_band = jnp.abs(qpos - kpos) <= WINDOW
    kwin = kb_ref[pl.ds(start, n_win), :].astype(BF16)
    vwin = vb_ref[pl.ds(start, n_win), :].astype(BF16)
    kctx = kb_ctx_ref[...].astype(BF16)
    vctx = vb_ctx_ref[...].astype(BF16)
    sls = [slice(kh * HD, (kh + 1) * HD) for kh in range(KVH_B)]
    o = _gqa_groups(qb_ref, 0, H_B // KVH_B, [[kwin[:, sl], kctx[:, sl]] for sl in sls],
                    [[vwin[:, sl], vctx[:, sl]] for sl in sls],
                    mask=in_band, sink=_stacked_sink(sink_ref, 0, H_B, BQ_S))
    o_ref[:, H_A * V_A:] = o.astype(o_ref.dtype)


def _attn_ab_sample(sink, qa, ka, va, qb, kb, vb, ckv_ctx, kpe_ctx, kb_ctx, vb_ctx, w_kn, w_v, place):
    qoff = N_PROMPT // BQ_S
    nq = DEC_SEQ // BQ_S
    boff = N_PROMPT // DEC_SEQ

    def qblk(w):
        return pl.BlockSpec((BQ_S, w), lambda b, i: (qoff + b * nq + i, 0))

    def bblk(w):
        return pl.BlockSpec((DEC_SEQ, w), lambda b, i: (boff + b, 0))

    def sblk(w):
        return pl.BlockSpec((DEC_SEQ, w), lambda b, i: (b, 0))

    def cblk(w):
        return pl.BlockSpec((PAST_LEN, w), lambda b, i: (b, 0))

    return pl.pallas_call(
        _attn_ab_sample_kernel,
        grid=(DEC_BATCH, nq),
        in_specs=[pl.BlockSpec(memory_space=pltpu.SMEM), qblk(H_A * LANES), bblk(H_A * LANES), bblk(H_A * V_A),
                  qblk(H_B * HD), sblk(KVH_B * HD), sblk(KVH_B * HD),
                  cblk(KV_LORA), cblk(LANES), cblk(KVH_B * HD), cblk(KVH_B * HD),
                  _const_spec((KV_LORA, H_A * LANES)), _const_spec((KV_LORA, H_A * V_A)),
                  _const_spec((LANES, H_A * LANES))],
        out_specs=pl.BlockSpec((BQ_S, D_MODEL), lambda b, i: (b * nq + i, 0)),
        out_shape=jax.ShapeDtypeStruct((N_SAMPLE, H_A * V_A + H_B * HD), BF16),
        scratch_shapes=[pltpu.VMEM((PAST_LEN, H_A * LANES), BF16), pltpu.VMEM((PAST_LEN, H_A * V_A), BF16)],
        compiler_params=_params(("arbitrary", "arbitrary")),
        name="attn_ab_sample",
    )(sink, qa, ka, va, qb, kb, vb, ckv_ctx, kpe_ctx, kb_ctx, vb_ctx, w_kn, w_v, place)


def _gqa_heads(q_ref, k_list, v_list, o_ref, stack):
    gq = H_C // KVH_C
    for kh0 in range(0, KVH_C, stack):
        sls = [slice(kh * HD, (kh + 1) * HD) for kh in range(kh0, kh0 + stack)]
        o = _gqa_groups(q_ref, kh0 * gq * HD, gq, [[k[:, sl] for k in k_list] for sl in sls],
                        [[v[:, sl] for v in v_list] for sl in sls])
        o_ref[:, kh0 * gq * HD:(kh0 + stack) * gq * HD] = o.astype(o_ref.dtype)


def _attn_c_prompt_kernel(q_ref, k_ref, v_ref, o_ref):
    _gqa_heads(q_ref, [k_ref[...].astype(BF16)], [v_ref[...].astype(BF16)], o_ref, GQA_STACK_PROMPT)


def _attn_c_prompt(q, k, v):
    def blk(w):
        return pl.BlockSpec((SEQ, w), lambda b: (b, 0))
    return pl.pallas_call(
        _attn_c_prompt_kernel,
        grid=(BATCH,),
        in_specs=[blk(H_C * HD), blk(KVH_C * HD), blk(KVH_C * HD)],
        out_specs=blk(H_C * HD),
        out_shape=jax.ShapeDtypeStruct((N_PROMPT, H_C * HD), BF16),
        compiler_params=_params(("arbitrary",)),
        name="attn_c_prompt",
    )(q, k, v)


def _attn_c_sample_kernel(q_ref, k_ref, v_ref, kc_ref, vc_ref, o_ref):
    _gqa_heads(q_ref, [k_ref[...].astype(BF16), kc_ref[...].astype(BF16)],
               [v_ref[...].astype(BF16), vc_ref[...].astype(BF16)], o_ref, GQA_STACK_SAMPLE)


def _attn_c_sample(q, k, v, k_ctx, v_ctx):
    qoff = N_PROMPT // BQ_S
    nq = DEC_SEQ // BQ_S
    w = KVH_C * HD
    return pl.pallas_call(
        _attn_c_sample_kernel,
        grid=(DEC_BATCH, nq),
        in_specs=[pl.BlockSpec((BQ_S, H_C * HD), lambda b, i: (qoff + b * nq + i, 0)),
                  pl.BlockSpec((DEC_SEQ, w), lambda b, i: (b, 0)),
                  pl.BlockSpec((DEC_SEQ, w), lambda b, i: (b, 0)),
                  pl.BlockSpec((PAST_LEN, w), lambda b, i: (b, 0)),
                  pl.BlockSpec((PAST_LEN, w), lambda b, i: (b, 0))],
        out_specs=pl.BlockSpec((BQ_S, H_C * HD), lambda b, i: (b * nq + i, 0)),
        out_shape=jax.ShapeDtypeStruct((N_SAMPLE, H_C * HD), BF16),
        compiler_params=_params(("arbitrary", "arbitrary")),
        name="attn_c_sample",
    )(q, k, v, k_ctx, v_ctx)


def _layer_norm(y, g, b):
    mu = jnp.mean(y, axis=-1, keepdims=True)
    yc = y - mu
    var = jnp.mean(yc * yc, axis=-1, keepdims=True)
    return yc * lax.rsqrt(var + EPS) * g + b


def _route(sel, aff):
    def row(a, j):
        return a[j:j + 1, :]

    scores = []
    for g in range(N_GROUPS):
        a0, a1, a2, a3 = (row(sel, 4 * g + j) for j in range(4))
        hi01, lo01 = jnp.maximum(a0, a1), jnp.minimum(a0, a1)
        hi23, lo23 = jnp.maximum(a2, a3), jnp.minimum(a2, a3)
        top1 = jnp.maximum(hi01, hi23)
        top2 = jnp.maximum(jnp.minimum(hi01, hi23), jnp.maximum(lo01, lo23))
        scores.append(top1 + top2)
    best = scores[0]
    gi = jnp.zeros(best.shape, jnp.int32)
    for g in range(1, N_GROUPS):
        better = scores[g] > best
        gi = jnp.where(better, g, gi)
        best = jnp.where(better, scores[g], best)

    def in_group(a, j):
        out = row(a, j)
        for g in range(1, N_GROUPS):
            out = jnp.where(gi == g, row(a, 4 * g + j), out)
        return out

    v = [in_group(sel, j) for j in range(4)]
    a = [in_group(aff, j) for j in range(4)]
    chosen = []
    for j in range(4):
        rank = jnp.zeros(best.shape, jnp.int32)
        for k in range(4):
            if k == j:
                continue
            ahead = (v[k] >= v[j]) if k < j else (v[k] > v[j])
            rank = rank + ahead.astype(jnp.int32)
        chosen.append(rank < 2)
    total = sum(jnp.where(chosen[j], a[j], 0.0) for j in range(4))
    w = [jnp.where(chosen[j], a[j], 0.0) / total for j in range(4)]
    pair = jnp.zeros(best.shape, jnp.int32)
    for p, (ja, jb) in enumerate(zip(SLOT_A_LOCAL, SLOT_B_LOCAL)):
        pair = jnp.where(chosen[ja] & chosen[jb], p, pair)
    gate_a = jnp.where(pair == 0, w[0], jnp.where(pair <= 2, w[2], w[3]))
    gate_b = jnp.where((pair == 0) | (pair == 1) | (pair == 4), w[1], jnp.where(pair == 5, w[2], w[0]))
    return gi * PAIRS_PER_GROUP + pair, gate_a, gate_b


def _post_attn_kernel(split_x, *refs):
    t = pl.program_id(0)
    op_ref, os_ref = refs[:2]
    if split_x:
        x = _pick(t, refs[2], refs[3])
        refs = refs[4:]
    else:
        x = refs[2][...]
        refs = refs[3:]
    (w_out_ref, gate_ref, shift_ref, scale_ref, lng_ref, lnb_ref, rw_ref, rb_ref, tri_ref, trib_ref,
     x1_ref, route_ref, alloc_ref, rows_hbm,
     rowbuf, posv, poss, fill_ref, cur_ref, free_ref, statev, states, zero_ref,
     pos_sem, row_sem, aux_sem) = refs
    o = _pick(t, op_ref, os_ref)
    subs = _sub_tiles(o.shape[0])
    slot = t % 2
    pslot = (t + 1) % 2
    live = (t < POST_STEPS).astype(F32)

    def rows_done(s):
        pltpu.make_async_copy(rowbuf.at[0], rows_hbm.at[pl.ds(0, TP), :], row_sem.at[s]).wait()

    def pos_copy(s):
        return pltpu.make_async_copy(posv.at[s, 0], poss.at[s], pos_sem.at[s])

    @pl.when(t == 0)
    def _():
        fill_ref[...] = jnp.zeros_like(fill_ref)
        cur_ref[...] = jnp.zeros_like(cur_ref)
        free_ref[...] = jnp.zeros_like(free_ref)
        zero_ref[...] = jnp.zeros_like(zero_ref)
        rowbuf[1] = jnp.zeros(rowbuf.shape[1:], F32)

        def init(r, carry):
            poss[1, r] = MOE_ROWS + r
            return carry
        lax.fori_loop(0, TP, init, 0)

    @pl.when(t > 0)
    def _():
        pos_copy(pslot).wait()
        rows_done(slot)

    for r in range(TP):
        _row_copy(rowbuf.at[pslot], r, rows_hbm, poss[pslot, r], row_sem.at[pslot]).start(priority=r % 2)

    a = [jnp.dot(o[s], w_out_ref[...], preferred_element_type=F32) for s in subs]
    x1 = [_layer_norm(ALPHA * x[s] + gate_ref[...] * a_s, lng_ref[...], lnb_ref[...]) for s, a_s in zip(subs, a)]
    h2 = [x1_s * (1.0 + scale_ref[...]) + shift_ref[...] for x1_s in x1]
    for s, x1_s, h2_s in zip(subs, x1, h2):
        x1_ref[s, :] = x1_s
        rowbuf[slot, s, :D_MODEL] = h2_s
    rw = rw_ref[...]
    rw_hi = rw.astype(BF16)
    rw_lo = (rw - rw_hi.astype(F32)).astype(BF16)
    rw_both = jnp.concatenate([rw_hi, rw_lo], axis=0)
    logits = []
    for h2_s in h2:
        h2_hi = h2_s.astype(BF16)
        h2_lo = (h2_s - h2_hi.astype(F32)).astype(BF16)
        by_hi = _qk(rw_both, h2_hi)
        logits.append(by_hi[:N_EXPERTS] + by_hi[N_EXPERTS:] + _qk(rw_hi, h2_lo))
    aff = [1.0 / (1.0 + jnp.exp(-l)) for l in logits]
    routed = [_route(aff_s + rb_ref[...], aff_s) for aff_s in aff]

    lane = lax.broadcasted_iota(jnp.int32, (LANES, TM), 0)
    for s, (_, gate_a, gate_b) in zip(subs, routed):
        meta_t = jnp.where(lane == 0, gate_a, jnp.where(lane == 1, gate_b, 0.0))
        rowbuf[slot, s, D_MODEL:] = meta_t.T

    onehots = [lax.broadcasted_iota(jnp.int32, (BUCKET_ROWS, TM), 0) == bucket for bucket, _, _ in routed]
    prefixes = [jnp.dot(oh.astype(BF16), tri_ref[...], preferred_element_type=F32) for oh in onehots]
    fill = fill_ref[...]
    cur = cur_ref[...]
    free = free_ref[...]
    route_ref[...] = jnp.zeros(route_ref.shape, F32)
    for i, (s, onehot, prefix) in enumerate(zip(subs, onehots, prefixes)):
        cnt = jnp.sum(onehot.astype(F32), axis=1, keepdims=True)
        used = fill - TME * jnp.floor(fill * (1.0 / TME))
        takes = ((used == 0.0) & (cnt > 0.0)) | ((used > 0.0) & (used + cnt > TME))
        takes_f = jnp.where(takes, live, 0.0)
        tile_new = free + jnp.dot(trib_ref[...], takes_f.astype(BF16), preferred_element_type=F32)
        tile_first = jnp.where(used == 0.0, tile_new, cur)
        local = jnp.sum(jnp.where(onehot, prefix + used[:, 0:1], 0.0), axis=0, keepdims=True)
        first = jnp.sum(jnp.where(onehot, tile_first[:, 0:1], 0.0), axis=0, keepdims=True)
        second = jnp.sum(jnp.where(onehot, tile_new[:, 0:1], 0.0), axis=0, keepdims=True)
        pos = jnp.where(local < TME, first * TME + local, second * TME + (local - TME))
        route_ref[0:1, s] = pos
        posv[slot, 0:1, s] = pos.astype(jnp.int32)
        alloc_ref[i] = jnp.where(takes_f > 0.0, tile_new, -1.0)
        cur = jnp.where(takes_f > 0.0, tile_new, cur)
        fill = fill + cnt * live
        free = free + jnp.sum(takes_f, axis=0, keepdims=True)
    fill_ref[...] = fill
    cur_ref[...] = cur
    free_ref[...] = free
    pos_copy(slot).start()

    @pl.when(t == POST_STEPS)
    def _():
        pos_copy(slot).wait()
        rows_done(pslot)
        statev[0] = (fill - TME * jnp.floor(fill * (1.0 / TME))).astype(jnp.int32)
        statev[1] = cur.astype(jnp.int32)
        statev[2] = free.astype(jnp.int32)
        state_copy = pltpu.make_async_copy(statev, states, aux_sem)
        state_copy.start()
        state_copy.wait()

        def zero_rows(row0, n):
            cp = pltpu.make_async_copy(zero_ref.at[pl.ds(0, n), :], rows_hbm.at[pl.ds(row0, n), :], aux_sem)
            cp.start()
            cp.wait()

        for b in range(N_BUCKETS):
            used_b = states[0, b, 0]

            @pl.when(used_b > 0)
            def _(b=b, used_b=used_b):
                base = states[1, b, 0] * TME
                aligned = ((used_b + 7) // 8) * 8

                def one(r, carry):
                    zero_rows(base + r, 1)
                    return carry
                lax.fori_loop(used_b, aligned, one, 0)

                def eight(j, carry):
                    zero_rows(pl.multiple_of(base + j * 8, 8), 8)
                    return carry
                lax.fori_loop(aligned // 8, TME // 8, eight, 0)

        def unused(i, carry):
            zero_rows(pl.multiple_of(i * TME, TME), TME)
            return carry
        lax.fori_loop(states[2, 0, 0], MOE_TILES, unused, 0)


def _post_attn(layer, o_p, o_s, x_list, mod, w_out, ln_g, ln_b, rw_t, rb):
    split_x = len(x_list) == 2
    last = POST_STEPS - 1
    p_tiles = N_PROMPT // TP
    n_sub = TP // TM

    def clamped(t):
        return jnp.minimum(t, last)

    tok = pl.BlockSpec((TP, D_MODEL), lambda t: (clamped(t), 0))

    def split_specs():
        return [pl.BlockSpec((TP, D_MODEL), lambda t: (jnp.minimum(t, p_tiles - 1), 0)),
                pl.BlockSpec((TP, D_MODEL), lambda t: (jnp.clip(t - p_tiles, 0, last - p_tiles), 0))]

    def mod_spec(chunk):
        return pl.BlockSpec((None, None, None, 1, D_MODEL),
                            lambda t: (layer, _mod_row(clamped(t), TP), chunk, 0, 0))

    tri = jnp.asarray(np.arange(TM)[:, None] < np.arange(TM)[None, :], BF16)
    trib = jnp.asarray(np.arange(BUCKET_ROWS)[None, :] < np.arange(BUCKET_ROWS)[:, None], BF16)
    x1, route, alloc, rows = pl.pallas_call(
        functools.partial(_post_attn_kernel, split_x),
        grid=(POST_STEPS + 1,),
        in_specs=split_specs() + (split_specs() if split_x else [tok]) + [
            _const_spec((D_MODEL, D_MODEL)), mod_spec(2), mod_spec(3), mod_spec(4),
            _const_spec((1, D_MODEL)), _const_spec((1, D_MODEL)), _const_spec((N_EXPERTS, D_MODEL)),
            _const_spec((N_EXPERTS, 1)), _const_spec((TM, TM)), _const_spec((BUCKET_ROWS, BUCKET_ROWS))],
        out_specs=[tok, pl.BlockSpec((8, TP), lambda t: (0, t)),
                   pl.BlockSpec((n_sub, BUCKET_ROWS, LANES), lambda t: (t, 0, 0)),
                   pl.BlockSpec(memory_space=pl.ANY)],
        out_shape=[jax.ShapeDtypeStruct((N_TOK, D_MODEL), F32),
                   jax.ShapeDtypeStruct((8, (POST_STEPS + 1) * TP), F32),
                   jax.ShapeDtypeStruct(((POST_STEPS + 1) * n_sub, BUCKET_ROWS, LANES), F32),
                   jax.ShapeDtypeStruct((MOE_ROWS + TP, ROW_W), F32)],
        scratch_shapes=[pltpu.VMEM((2, TP, ROW_W), F32), pltpu.VMEM((2, 8, TP), jnp.int32),
                        pltpu.SMEM((2, TP), jnp.int32),
                        pltpu.VMEM((BUCKET_ROWS, LANES), F32), pltpu.VMEM((BUCKET_ROWS, LANES), F32),
                        pltpu.VMEM((BUCKET_ROWS, LANES), F32),
                        pltpu.VMEM((3, BUCKET_ROWS, LANES), jnp.int32), pltpu.SMEM((3, BUCKET_ROWS, LANES), jnp.int32),
                        pltpu.VMEM((TME, ROW_W), F32),
                        pltpu.SemaphoreType.DMA((2,)), pltpu.SemaphoreType.DMA((2,)), pltpu.SemaphoreType.DMA],
        compiler_params=_params(("arbitrary",)),
        name="post_attn",
    )(o_p, o_s, *x_list, w_out, mod, mod, mod, ln_g, ln_b, rw_t, rb, tri, trib)
    return x1, route[0, :N_TOK], alloc[:POST_STEPS * n_sub, :, 0], rows


def _moe_kernel(ta_ref, tb_ref, rd_ref, wr_ref, nt_ref, x_ref, wga_ref, wua_ref, wda_ref, wgb_ref, wub_ref, wdb_ref,
                o_ref, wga_s, wua_s, wda_s, wgb_s, wub_s, wdb_s):
    del rd_ref, wr_ref
    i = pl.program_id(0)
    prev = jnp.maximum(i - 1, 0)
    slots = ((ta_ref, (wga_ref, wua_ref, wda_ref), (wga_s, wua_s, wda_s)),
             (tb_ref, (wgb_ref, wub_ref, wdb_ref), (wgb_s, wub_s, wdb_s)))

    for t_ref, w_refs, w_scr in slots:
        @pl.when((i == 0) | (t_ref[i] != t_ref[prev]))
        def _(w_refs=w_refs, w_scr=w_scr):
            for w_ref, s_ref in zip(w_refs, w_scr):
                s_ref[...] = w_ref[...].astype(BF16)

    @pl.when(i < nt_ref[0])
    def _():
        x = x_ref[:, :D_MODEL].astype(BF16)
        acc = None
        for slot, (_, _, (wg_s, wu_s, wd_s)) in enumerate(slots):
            gt = jnp.dot(x, wg_s[...], preferred_element_type=F32)
            up = jnp.dot(x, wu_s[...], preferred_element_type=F32)
            gate = x_ref[:, D_MODEL + slot:D_MODEL + slot + 1]
            hid = (gt / (1.0 + jnp.exp(-gt))) * up * gate
            y = jnp.dot(hid.astype(BF16), wd_s[...], preferred_element_type=F32)
            acc = y if acc is None else acc + y
        o_ref[...] = acc

    @pl.when(i >= nt_ref[0])
    def _():
        o_ref[...] = jnp.zeros_like(o_ref)


def _moe(layer, tile_a, tile_b, read_tile, write_tile, n_tiles, rows, w_gate, w_up, w_down):
    def wspec(shape, which):
        if which == 0:
            return pl.BlockSpec((None, None) + shape, lambda i, ta, tb, rd, wr, nt: (layer, ta[i], 0, 0))
        return pl.BlockSpec((None, None) + shape, lambda i, ta, tb, rd, wr, nt: (layer, tb[i], 0, 0))

    up_shape, down_shape = (D_MODEL, D_EXPERT), (D_EXPERT, D_MODEL)
    grid_spec = pltpu.PrefetchScalarGridSpec(
        num_scalar_prefetch=5,
        grid=(MOE_TILES,),
        in_specs=[pl.BlockSpec((TME, ROW_W), lambda i, ta, tb, rd, wr, nt: (rd[i], 0)),
                  wspec(up_shape, 0), wspec(up_shape, 0), wspec(down_shape, 0),
                  wspec(up_shape, 1), wspec(up_shape, 1), wspec(down_shape, 1)],
        out_specs=pl.BlockSpec((TME, D_MODEL), lambda i, ta, tb, rd, wr, nt: (wr[i], 0)),
        scratch_shapes=[pltpu.VMEM(s, BF16) for s in (up_shape, up_shape, down_shape) * 2],
    )
    return pl.pallas_call(
        _moe_kernel,
        grid_spec=grid_spec,
        out_shape=jax.ShapeDtypeStruct((MOE_ROWS, D_MODEL), F32),
        compiler_params=_params(("arbitrary",)),
        name="moe_experts",
    )(tile_a, tile_b, read_tile, write_tile, n_tiles, rows, w_gate, w_up, w_down, w_gate, w_up, w_down)


def _row_copy(src_ref, src_row, dst_ref, dst_row, sem):
    return pltpu.make_async_copy(src_ref.at[pl.ds(src_row, 1), :], dst_ref.at[pl.ds(dst_row, 1), :], sem)


def _plan(alloc):
    ids = jnp.arange(MOE_TILES, dtype=F32)
    hit = alloc[:, :, None] == ids[None, None, :]
    buckets = jnp.arange(BUCKET_ROWS, dtype=jnp.int32)
    tile_bucket = jnp.sum(jnp.where(hit, buckets[None, :, None], 0), axis=(0, 1))
    in_use = jnp.any(hit, axis=(0, 1))
    n_tiles = jnp.sum(in_use.astype(jnp.int32))
    key = jnp.where(in_use, tile_bucket, N_BUCKETS)
    idx = jnp.arange(MOE_TILES, dtype=jnp.int32)
    before = (key[None, :] < key[:, None]) | ((key[None, :] == key[:, None]) & (idx[None, :] < idx[:, None]))
    rank = jnp.sum(before.astype(jnp.int32), axis=1)
    at_step = rank[None, :] == idx[:, None]
    order = jnp.sum(jnp.where(at_step, idx[None, :], 0), axis=1)
    step_bucket = jnp.sum(jnp.where(at_step, key[None, :], 0), axis=1)
    last = jnp.maximum(n_tiles - 1, 0)
    live = idx < n_tiles
    read_tile = jnp.where(live, order, jnp.sum(jnp.where(idx == last, order, 0)))
    write_tile = jnp.where(live, order, idx)
    step_bucket = jnp.where(live, step_bucket, jnp.sum(jnp.where(idx == last, step_bucket, 0)))
    step_bucket = jnp.minimum(step_bucket, N_BUCKETS - 1)
    group, pair = step_bucket // PAIRS_PER_GROUP, step_bucket % PAIRS_PER_GROUP
    slot_a = jnp.asarray(SLOT_A_LOCAL, jnp.int32)
    slot_b = jnp.asarray(SLOT_B_LOCAL, jnp.int32)
    pair_hot = pair[:, None] == jnp.arange(PAIRS_PER_GROUP, dtype=jnp.int32)[None, :]
    tile_a = group * EXPERTS_PER_GROUP + jnp.sum(jnp.where(pair_hot, slot_a[None, :], 0), axis=1)
    tile_b = group * EXPERTS_PER_GROUP + jnp.sum(jnp.where(pair_hot, slot_b[None, :], 0), axis=1)
    return (tile_a.astype(jnp.int32), tile_b.astype(jnp.int32), read_tile.astype(jnp.int32),
            write_tile.astype(jnp.int32), n_tiles.reshape(1))


def _post_moe_kernel(split_out, pos_ref, f_ref, x1_ref, gate_ref, lng_ref, lnb_ref, *refs):
    out_refs, (fbuf, sem) = refs[:-2], refs[-2:]
    t = pl.program_id(0)

    def wait_tile(slot):
        pltpu.make_async_copy(f_ref.at[pl.ds(0, TM), :], fbuf.at[slot], sem.at[slot]).wait()

    @pl.when(t == 0)
    def _():
        def issue(r, carry):
            _row_copy(f_ref, pos_ref[r], fbuf.at[r // TM], r % TM, sem.at[r // TM]).start()
            return carry
        lax.fori_loop(0, (GATHER_SLOTS - 1) * TM, issue, 0, unroll=8)

    slot = t % GATHER_SLOTS
    wait_tile(slot)
    ahead = jnp.minimum(t + GATHER_SLOTS - 1, N_TILES - 1)
    aslot = (t + GATHER_SLOTS - 1) % GATHER_SLOTS
    for r in range(TM):
        _row_copy(f_ref, pos_ref[ahead * TM + r], fbuf.at[aslot], r, sem.at[aslot]).start(priority=r % 2)
    y = _layer_norm(ALPHA * x1_ref[...] + gate_ref[...] * fbuf[slot], lng_ref[...], lnb_ref[...])

    @pl.when(t == N_TILES - 1)
    def _():
        for k in range(1, GATHER_SLOTS):
            wait_tile((t + k) % GATHER_SLOTS)
    if split_out:
        @pl.when(t < PROMPT_TILES)
        def _():
            out_refs[0][...] = y

        @pl.when(t >= PROMPT_TILES)
        def _():
            out_refs[1][...] = y
    else:
        out_refs[0][...] = y


def _post_moe(layer, pos, f_sorted, x1, mod, ln_g, ln_b, split_out):
    tok = pl.BlockSpec((TM, D_MODEL), lambda t, p: (t, 0))
    if split_out:
        out_specs = [pl.BlockSpec((TM, D_MODEL), lambda t, p: (jnp.minimum(t, PROMPT_TILES - 1), 0)),
                     pl.BlockSpec((TM, D_MODEL), lambda t, p: (jnp.maximum(t - PROMPT_TILES, 0), 0))]
        out_shape = [jax.ShapeDtypeStruct((N_PROMPT, D_MODEL), F32), jax.ShapeDtypeStruct((N_SAMPLE, D_MODEL), F32)]
    else:
        out_specs = [tok]
        out_shape = [jax.ShapeDtypeStruct((N_TOK, D_MODEL), F32)]
    mod_spec = pl.BlockSpec((None, None, None, 1, D_MODEL), lambda t, p: (layer, _mod_row(t), 5, 0, 0))
    return pl.pallas_call(
        functools.partial(_post_moe_kernel, split_out),
        grid_spec=pltpu.PrefetchScalarGridSpec(
            num_scalar_prefetch=1, grid=(N_TILES,),
            in_specs=[pl.BlockSpec(memory_space=pl.ANY), tok, mod_spec,
                      pl.BlockSpec((1, D_MODEL), lambda t, p: (0, 0)), pl.BlockSpec((1, D_MODEL), lambda t, p: (0, 0))],
            out_specs=out_specs,
            scratch_shapes=[pltpu.VMEM((GATHER_SLOTS, TM, D_MODEL), F32),
                            pltpu.SemaphoreType.DMA((GATHER_SLOTS,))]),
        out_shape=out_shape,
        compiler_params=_params(("arbitrary",)),
        name="post_moe",
    )(pos, f_sorted, x1, mod, ln_g, ln_b)


def _ffn(layer, o_p, o_s, x_list, mod, w_out, ln_g, ln_b, rw_t, rb, w_gate, w_up, w_down, split_out):
    x1, pos, alloc, rows_sorted = _post_attn(layer, o_p, o_s, x_list, mod, w_out, ln_g[layer, 0][None],
                                             ln_b[layer, 0][None], rw_t, rb)
    tile_a, tile_b, read_tile, write_tile, n_tiles = _plan(alloc)
    f_sorted = _moe(layer, tile_a, tile_b, read_tile, write_tile, n_tiles, rows_sorted, w_gate, w_up, w_down)
    return _post_moe(layer, pos.astype(jnp.int32), f_sorted, x1, mod, ln_g[layer, 1][None], ln_b[layer, 1][None],
                     split_out)


def kernel(x_prompt, x_sample, c, cache_mla_ckv, cache_mla_kpe, cache_swa_k, cache_swa_v, cache_gqa_k, cache_gqa_v, c_ctx, w_mod, b_mod, ln_g, ln_b, w_in_ab, mla_q_norm, mla_w_uq, mla_kv_norm, mla_w_ukv, swa_sink, w_out_ab, w_in_c, gqa_q_norm, gqa_k_norm, w_out_c, router_w, router_bias, exp_w_gate, exp_w_up, exp_w_down):
    xp = x_prompt.reshape(N_PROMPT, D_MODEL)
    xs = x_sample.reshape(N_SAMPLE, D_MODEL)
    cond = jnp.concatenate([c_ctx[None], c, jnp.zeros((MOD_ROWS - 1 - DEC_BATCH, D_MODEL), F32)], axis=0)
    mod = _modulation(cond, w_mod, b_mod).reshape(DEPTH, MOD_ROWS, 6, 1, D_MODEL)
    tabs = _rope_tables()
    rw_t = router_w.T
    rb = router_bias.reshape(N_EXPERTS, 1)

    w = w_in_ab[0]
    pad = jnp.zeros((D_MODEL, AB_COLS - w.shape[1]), F32)
    w_in = jnp.concatenate([w[:, :384], w[:, 416:], w[:, 384:416], pad], axis=1).astype(BF16)
    w_uq, w_kn, w_v, place = _mla_weights(mla_w_uq[0], mla_w_ukv[0])
    qa, ka, va, qb, ckv_p, kpe_p, kbt_p, vbt_p, kb_s, vb_s = _inproj_ab(
        xp, xs, mod, w_in, mla_q_norm[0][None], mla_kv_norm[0][None], w_uq, w_kn, w_v, place, tabs)
    sink = swa_sink[0]
    o_p = _attn_ab_prompt(sink, qa, ka, va, qb, kbt_p, vbt_p)
    kpe_ctx = jnp.pad(cache_mla_kpe[:, 0].reshape(-1, ROPE_A), ((0, 0), (0, LANES - ROPE_A)))
    o_s = _attn_ab_sample(sink, qa, ka, va, qb, kb_s, vb_s,
                          cache_mla_ckv[:, 0].reshape(-1, KV_LORA), kpe_ctx,
                          cache_swa_k[:, 0].reshape(-1, KVH_B * HD), cache_swa_v[:, 0].reshape(-1, KVH_B * HD),
                          w_kn, w_v, place)
    (x2,) = _ffn(0, o_p, o_s, [xp, xs], mod, w_out_ab[0].astype(BF16), ln_g, ln_b, rw_t, rb,
                 exp_w_gate, exp_w_up, exp_w_down, split_out=False)

    g_full = jnp.concatenate([jnp.tile(gqa_q_norm[0], H_C), jnp.tile(gqa_k_norm[0], KVH_C)])[None]
    qc, kc_p, vc_p, kc_s, vc_s = _inproj_c(x2, mod, w_in_c[0].astype(BF16), g_full, tabs)
    oc_p = _attn_c_prompt(qc, kc_p, vc_p)
    oc_s = _attn_c_sample(qc, kc_s, vc_s, cache_gqa_k[:, 0].reshape(-1, KVH_C * HD),
                          cache_gqa_v[:, 0].reshape(-1, KVH_C * HD))
    y_p, y_s = _ffn(1, oc_p, oc_s, [x2], mod, w_out_c[0].astype(BF16), ln_g, ln_b, rw_t, rb,
                    exp_w_gate, exp_w_up, exp_w_down, split_out=True)

    y_prompt = y_p.reshape(BATCH, SEQ, D_MODEL)
    y_sample = y_s.reshape(DEC_BATCH, DEC_SEQ, D_MODEL)
    new_ckv = ckv_p.reshape(BATCH, 1, SEQ, KV_LORA)
    new_kpe = kpe_p.reshape(BATCH, 1, SEQ, ROPE_A)
    new_swk = kbt_p.reshape(BATCH, 1, KVH_B, HD, SEQ).transpose(0, 1, 4, 2, 3)
    new_swv = vbt_p.reshape(BATCH, 1, KVH_B, HD, SEQ).transpose(0, 1, 4, 2, 3)
    new_gk = kc_p.reshape(BATCH, 1, SEQ, KVH_C, HD)
    new_gv = vc_p.reshape(BATCH, 1, SEQ, KVH_C, HD)
    return y_prompt, y_sample, new_ckv, new_kpe, new_swk, new_swv, new_gk, new_gv
```

```python
import functools

import numpy as np
import jax
import jax.numpy as jnp
from jax import lax
from jax.experimental import pallas as pl
from jax.experimental.pallas import tpu as pltpu

D_MODEL = 1024
BATCH = 32
SEQ = 256
DEPTH = 2
DEC_BATCH = 4
DEC_SEQ = 1024
PAST_LEN = 256
GRID_W = 64
ROPE_THETA = 10000.0
WINDOW = 128
HD = 64
H_A = 8
NOPE_A = 64
ROPE_A = 32
V_A = 64
Q_LORA = 256
KV_LORA = 128
H_B = 8
KVH_B = 2
H_C = 16
KVH_C = 4
N_EXPERTS = 16
N_GROUPS = 4
EXPERTS_PER_GROUP = 4
D_EXPERT = 512
ALPHA = (2 * DEPTH) ** 0.25
NEG_INF = -1e30
EPS = 1e-6

LANES = 128
N_PROMPT = BATCH * SEQ
N_SAMPLE = DEC_BATCH * DEC_SEQ
N_TOK = N_PROMPT + N_SAMPLE
TM = 256
TP = 512
N_TILES = N_TOK // TM
PROMPT_TILES = N_PROMPT // TM
BQ_S = 256
LOG2E = 1.4426950408889634
MLA_STACK_PROMPT = 4
MLA_STACK_SAMPLE = 4
GQA_STACK_PROMPT = 4
GQA_STACK_SAMPLE = 2
MOD_ROWS = 8

PAIRS_PER_GROUP = 6
N_BUCKETS = N_GROUPS * PAIRS_PER_GROUP
SLOT_A_LOCAL = (0, 2, 2, 3, 3, 3)
SLOT_B_LOCAL = (1, 1, 0, 0, 1, 2)
TME = 256
MOE_TILES = -(-(N_TOK + N_BUCKETS * (TME - 1)) // TME)
MOE_ROWS = MOE_TILES * TME

BUCKET_ROWS = 32
ROW_W = D_MODEL + LANES
POST_STEPS = N_TOK // TP
GATHER_SLOTS = 3

AB_COLS = 1280
C_NORM_COLS = H_C * HD + KVH_C * HD
NORM_BLOCK = 256

BF16 = jnp.bfloat16
F32 = jnp.float32
VMEM_LIMIT = 52 * 1024 * 1024


def _mod_row(t, tile=TM):
    p_tiles = N_PROMPT // tile
    return jnp.where(t < p_tiles, 0, 1 + (t - p_tiles) // (DEC_SEQ // tile))


def _mod_spec(layer, chunk, tile=TM):
    return pl.BlockSpec((None, None, None, 1, D_MODEL), lambda t: (layer, _mod_row(t, tile), chunk, 0, 0))


def _const_spec(shape):
    nd = len(shape)
    return pl.BlockSpec(shape, lambda *_: (0,) * nd)


def _split_specs(tile=TM):
    p_tiles = N_PROMPT // tile
    return [
        pl.BlockSpec((tile, D_MODEL), lambda t: (jnp.minimum(t, p_tiles - 1), 0)),
        pl.BlockSpec((tile, D_MODEL), lambda t: (jnp.maximum(t - p_tiles, 0), 0)),
    ]


def _group_spec(tile, width, prompt):
    p_tiles = N_PROMPT // tile
    if prompt:
        return pl.BlockSpec((tile, width), lambda t: (jnp.minimum(t, p_tiles - 1), 0))
    return pl.BlockSpec((tile, width), lambda t: (jnp.maximum(t - p_tiles, 0), 0))


def _pick(t, a_ref, b_ref):
    return jnp.where(t < N_PROMPT // a_ref.shape[0], a_ref[...], b_ref[...])


def _sub_tiles(rows):
    return [slice(i * TM, (i + 1) * TM) for i in range(rows // TM)]


def _params(sem):
    return pltpu.CompilerParams(dimension_semantics=sem, vmem_limit_bytes=VMEM_LIMIT)


def _mod_kernel(cond_ref, w_ref, b_ref, o_ref):
    c = cond_ref[...]
    s = (c / (1.0 + jnp.exp(-c))).astype(BF16)
    o_ref[...] = jnp.dot(s, w_ref[...].astype(BF16), preferred_element_type=F32) + b_ref[...]


def _modulation(cond, w_mod, b_mod):
    tn = 1536
    return pl.pallas_call(
        _mod_kernel,
        grid=(DEPTH, 6 * D_MODEL // tn),
        in_specs=[
            pl.BlockSpec((MOD_ROWS, D_MODEL), lambda l, j: (0, 0)),
            pl.BlockSpec((None, D_MODEL, tn), lambda l, j: (l, 0, j)),
            pl.BlockSpec((None, 1, tn), lambda l, j: (l, 0, j)),
        ],
        out_specs=pl.BlockSpec((None, MOD_ROWS, tn), lambda l, j: (l, 0, j)),
        out_shape=jax.ShapeDtypeStruct((DEPTH, MOD_ROWS, 6 * D_MODEL), F32),
        compiler_params=_params(("arbitrary", "arbitrary")),
        name="modulation",
    )(cond, w_mod, b_mod.reshape(DEPTH, 1, 6 * D_MODEL))


def _grid_cos_sin(half):
    n_freq = half // 2
    rows = DEC_SEQ // GRID_W
    row = jnp.repeat(jnp.arange(rows, dtype=F32), GRID_W)
    col = jnp.tile(jnp.arange(GRID_W, dtype=F32), rows)
    inv = 1.0 / (ROPE_THETA ** (jnp.arange(n_freq, dtype=F32) / n_freq))
    ang = jnp.concatenate([row[:, None] * inv, col[:, None] * inv], axis=-1)
    return jnp.cos(ang), jnp.sin(ang)


def _rope_tables():
    tabs = []
    for half in (HD // 2, ROPE_A // 2):
        cos, sin = _grid_cos_sin(half)
        reps = LANES // (2 * half)
        tabs.append(jnp.tile(jnp.concatenate([cos, cos], axis=-1), (1, reps)))
        tabs.append(jnp.tile(jnp.concatenate([-sin, sin], axis=-1), (1, reps)))
    half = ROPE_A // 2
    one = jnp.ones((DEC_SEQ, NOPE_A), F32)
    zero = jnp.zeros((DEC_SEQ, NOPE_A), F32)
    pad1 = jnp.ones((DEC_SEQ, LANES - NOPE_A - ROPE_A), F32)
    pad0 = jnp.zeros((DEC_SEQ, LANES - NOPE_A - ROPE_A), F32)
    zh = jnp.zeros((DEC_SEQ, half), F32)
    tabs.append(jnp.concatenate([one, cos, cos, pad1], axis=-1))
    tabs.append(jnp.concatenate([zero, -sin, zh, pad0], axis=-1))
    tabs.append(jnp.concatenate([zero, zh, sin, pad0], axis=-1))
    return tuple(tabs)


def _rope(x, c, s, half):
    w = x.shape[1]
    reps = w // LANES
    if reps > 1:
        c = jnp.concatenate([c] * reps, axis=1)
        s = jnp.concatenate([s] * reps, axis=1)
    ahead = pltpu.roll(x, w - half, 1)
    behind = pltpu.roll(x, half, 1)
    lane = lax.broadcasted_iota(jnp.int32, x.shape, 1)
    swapped = jnp.where((lane & (2 * half - 1)) < half, ahead, behind)
    return x * c + swapped * s


def _rope_specs(n, tile):
    p_tiles = N_PROMPT // tile

    def idx(t):
        return (jnp.maximum(t - p_tiles, 0) % (DEC_SEQ // tile), 0)
    return [pl.BlockSpec((tile, LANES), idx)] * n


def _rms(x, g_ref):
    return x * lax.rsqrt(jnp.mean(x * x, axis=-1, keepdims=True) + EPS) * g_ref[...]


def _inproj_ab_kernel(xp_ref, xs_ref, shift_ref, scale_ref, w_in_ref, qn_ref, kvn_ref, w_uq_ref, w_kn_ref, w_v_ref,
                      place_ref, c64_ref, s64_ref, c32_ref, s32_ref, ca_ref, sa_ref, sb_ref,
                      qa_ref, ka_ref, va_ref, qb_ref, ckv_p_ref, kpe_p_ref, kbt_p_ref, vbt_p_ref, kb_s_ref, vb_s_ref):
    t = pl.program_id(0)
    x = _pick(t, xp_ref, xs_ref)
    subs = _sub_tiles(x.shape[0])
    h = [(x[s] * (1.0 + scale_ref[...]) + shift_ref[...]).astype(BF16) for s in subs]
    z = [jnp.dot(h_s, w_in_ref[...], preferred_element_type=F32) for h_s in h]
    ql = [_rms(z_s[:, :Q_LORA], qn_ref).astype(BF16) for z_s in z]
    qa = [jnp.dot(ql_s, w_uq_ref[...], preferred_element_type=F32) for ql_s in ql]
    ckv = [_rms(z_s[:, Q_LORA:Q_LORA + KV_LORA], kvn_ref) for z_s in z]
    k_nope = [jnp.dot(c_s.astype(BF16), w_kn_ref[...], preferred_element_type=F32) for c_s in ckv]
    va = [jnp.dot(c_s.astype(BF16), w_v_ref[...], preferred_element_type=F32).astype(BF16) for c_s in ckv]
    for s, va_s in zip(subs, va):
        va_ref[s, :] = va_s
    qb = [z_s[:, 384:896] for z_s in z]
    kb = [z_s[:, 896:1024] for z_s in z]
    vb = [z_s[:, 1024:1152] for z_s in z]
    kpe = [z_s[:, 1152:1280] for z_s in z]

    def finish(qa_out, qb_out, kpe_out):
        k_rope = [jnp.dot(kpe_s.astype(BF16), place_ref[...], preferred_element_type=F32) for kpe_s in kpe_out]
        for i, s in enumerate(subs):
            qa_ref[s, :] = qa_out[i].astype(BF16)
            qb_ref[s, :] = qb_out[i].astype(BF16)
            ka_ref[s, :] = (k_nope[i] + k_rope[i]).astype(BF16)

    p_tiles = N_PROMPT // x.shape[0]

    @pl.when(t < p_tiles)
    def _():
        finish(qa, qb, kpe)
        for i, s in enumerate(subs):
            ckv_p_ref[s, :] = ckv[i]
            kpe_p_ref[s, :] = kpe[i][:, :ROPE_A]
            kbt_p_ref[i] = kb[i].T
            vbt_p_ref[i] = vb[i].T

    @pl.when(t >= p_tiles)
    def _():
        w = qa[0].shape[1]
        reps = w // LANES
        half = ROPE_A // 2
        qa_rot, qb_rot, kb_rot, kpe_rot = [], [], [], []
        for i, s in enumerate(subs):
            ca = jnp.concatenate([ca_ref[s, :]] * reps, axis=1)
            sa = jnp.concatenate([sa_ref[s, :]] * reps, axis=1)
            sb = jnp.concatenate([sb_ref[s, :]] * reps, axis=1)
            qa_rot.append(qa[i] * ca + pltpu.roll(qa[i], w - half, 1) * sa + pltpu.roll(qa[i], half, 1) * sb)
            qb_rot.append(_rope(qb[i], c64_ref[s, :], s64_ref[s, :], HD // 2))
            kb_rot.append(_rope(kb[i], c64_ref[s, :], s64_ref[s, :], HD // 2))
            kpe_rot.append(_rope(kpe[i], c32_ref[s, :], s32_ref[s, :], ROPE_A // 2))
        finish(qa_rot, qb_rot, kpe_rot)
        for i, s in enumerate(subs):
            kb_s_ref[s, :] = kb_rot[i]
            vb_s_ref[s, :] = vb[i]


def _mla_weights(w_uq, w_ukv):
    pad = LANES - NOPE_A - ROPE_A
    uq = w_uq.reshape(Q_LORA, H_A, NOPE_A + ROPE_A)
    uq = jnp.pad(uq, ((0, 0), (0, 0), (0, pad))).reshape(Q_LORA, H_A * LANES)
    ukv = w_ukv.reshape(KV_LORA, H_A, NOPE_A + V_A)
    kn = jnp.pad(ukv[:, :, :NOPE_A], ((0, 0), (0, 0), (0, LANES - NOPE_A))).reshape(KV_LORA, H_A * LANES)
    wv = ukv[:, :, NOPE_A:].reshape(KV_LORA, H_A * V_A)
    src = np.arange(LANES)[:, None]
    dst = np.arange(H_A * LANES)[None, :] % LANES
    place = (src < ROPE_A) & (dst == src + NOPE_A)
    return uq.astype(BF16), kn.astype(BF16), wv.astype(BF16), jnp.asarray(place, BF16)


def _inproj_ab(xp, xs, mod, w_in, q_norm, kv_norm, w_uq, w_kn, w_v, place, tabs):
    widths = [(H_A * LANES, BF16), (H_A * LANES, BF16), (H_A * V_A, BF16), (H_B * HD, BF16)]
    out_specs = [pl.BlockSpec((TP, w), lambda t: (t, 0)) for w, _ in widths]
    out_shape = [jax.ShapeDtypeStruct((N_TOK, w), dt) for w, dt in widths]
    for w in (KV_LORA, ROPE_A):
        out_specs.append(_group_spec(TP, w, prompt=True))
        out_shape.append(jax.ShapeDtypeStruct((N_PROMPT, w), F32))
    assert SEQ == TM
    p_tiles = N_PROMPT // TP
    for _ in range(2):
        out_specs.append(pl.BlockSpec((TP // SEQ, KVH_B * HD, SEQ), lambda t: (jnp.minimum(t, p_tiles - 1), 0, 0)))
        out_shape.append(jax.ShapeDtypeStruct((BATCH, KVH_B * HD, SEQ), F32))
    for w in (KVH_B * HD, KVH_B * HD):
        out_specs.append(_group_spec(TP, w, prompt=False))
        out_shape.append(jax.ShapeDtypeStruct((N_SAMPLE, w), F32))
    return pl.pallas_call(
        _inproj_ab_kernel,
        grid=(N_TOK // TP,),
        in_specs=_split_specs(TP) + [_mod_spec(0, 0, TP), _mod_spec(0, 1, TP), _const_spec((D_MODEL, AB_COLS)),
                                     _const_spec((1, Q_LORA)), _const_spec((1, KV_LORA)),
                                     _const_spec((Q_LORA, H_A * LANES)), _const_spec((KV_LORA, H_A * LANES)),
                                     _const_spec((KV_LORA, H_A * V_A)), _const_spec((LANES, H_A * LANES))]
        + _rope_specs(7, TP),
        out_specs=out_specs,
        out_shape=out_shape,
        compiler_params=_params(("arbitrary",)),
        name="inproj_ab",
    )(xp, xs, mod, mod, w_in, q_norm, kv_norm, w_uq, w_kn, w_v, place, *tabs)


def _inproj_c_kernel(x_ref, shift_ref, scale_ref, w_in_ref, g_ref, ones_ref, c64_ref, s64_ref,
                     q_ref, k_p_ref, v_p_ref, k_s_ref, v_s_ref):
    t = pl.program_id(0)
    subs = _sub_tiles(x_ref.shape[0])
    h = [(x_ref[s, :] * (1.0 + scale_ref[...]) + shift_ref[...]).astype(BF16) for s in subs]
    z = [jnp.dot(h_s, w_in_ref[...], preferred_element_type=F32) for h_s in h]
    qk = [z_s[:, :C_NORM_COLS] for z_s in z]
    sq = [qk_s * qk_s for qk_s in qk]
    sq_hi = [sq_s.astype(BF16) for sq_s in sq]
    sq_lo = [(sq_s - hi_s.astype(F32)).astype(BF16) for sq_s, hi_s in zip(sq, sq_hi)]
    ss = []
    for hi_s, lo_s in zip(sq_hi, sq_lo):
        blocks = []
        for j in range(C_NORM_COLS // NORM_BLOCK):
            sl = slice(j * NORM_BLOCK, (j + 1) * NORM_BLOCK)
            blocks.append(jnp.dot(hi_s[:, sl], ones_ref[...], preferred_element_type=F32)
                          + jnp.dot(lo_s[:, sl], ones_ref[...], preferred_element_type=F32))
        ss.append(jnp.concatenate(blocks, axis=1))
    qk = [qk_s * lax.rsqrt(ss_s * (1.0 / HD) + EPS) * g_ref[...] for qk_s, ss_s in zip(qk, ss)]
    p_tiles = N_PROMPT // x_ref.shape[0]

    @pl.when(t < p_tiles)
    def _():
        for s, qk_s, z_s in zip(subs, qk, z):
            q_ref[s, :] = qk_s[:, :H_C * HD].astype(BF16)
            k_p_ref[s, :] = qk_s[:, H_C * HD:]
            v_p_ref[s, :] = z_s[:, C_NORM_COLS:]

    @pl.when(t >= p_tiles)
    def _():
        for s, qk_s, z_s in zip(subs, qk, z):
            q_ref[s, :] = _rope(qk_s[:, :H_C * HD], c64_ref[s, :], s64_ref[s, :], HD // 2).astype(BF16)
            k_s_ref[s, :] = _rope(qk_s[:, H_C * HD:], c64_ref[s, :], s64_ref[s, :], HD // 2)
            v_s_ref[s, :] = z_s[:, C_NORM_COLS:]


def _inproj_c(x, mod, w_in, g_full, tabs):
    def tok(w):
        return pl.BlockSpec((TP, w), lambda t: (t, 0))
    kv_w = KVH_C * HD
    head = np.arange(NORM_BLOCK) // HD
    ones = jnp.asarray(head[:, None] == head[None, :], BF16)
    return pl.pallas_call(
        _inproj_c_kernel,
        grid=(N_TOK // TP,),
        in_specs=[tok(D_MODEL), _mod_spec(1, 0, TP), _mod_spec(1, 1, TP),
                  _const_spec((D_MODEL, C_NORM_COLS + KVH_C * HD)),
                  _const_spec((1, C_NORM_COLS)), _const_spec((NORM_BLOCK, NORM_BLOCK))] + _rope_specs(2, TP),
        out_specs=[tok(H_C * HD), _group_spec(TP, kv_w, True), _group_spec(TP, kv_w, True),
                   _group_spec(TP, kv_w, False), _group_spec(TP, kv_w, False)],
        out_shape=[jax.ShapeDtypeStruct((N_TOK, H_C * HD), BF16),
                   jax.ShapeDtypeStruct((N_PROMPT, kv_w), F32), jax.ShapeDtypeStruct((N_PROMPT, kv_w), F32),
                   jax.ShapeDtypeStruct((N_SAMPLE, kv_w), F32), jax.ShapeDtypeStruct((N_SAMPLE, kv_w), F32)],
        compiler_params=_params(("arbitrary",)),
        name="inproj_c",
    )(x, mod, mod, w_in, g_full, ones, *tabs[:2])


def _qk(q, k):
    return lax.dot_general(q, k, (((1,), (1,)), ((), ())), preferred_element_type=F32)


def _softmax_pv(scores, values, scale, sink=None, kv_t=False):
    c = scale * LOG2E
    m = scores[0].max(axis=-1, keepdims=True)
    for s in scores[1:]:
        m = jnp.maximum(m, s.max(axis=-1, keepdims=True))
    m = m * c
    if sink is not None:
        sink = sink * LOG2E
        m = jnp.maximum(m, sink)
    den = None
    out = None
    for s, v in zip(scores, values):
        p = jnp.exp2(s * c - m)
        d = p.sum(axis=-1, keepdims=True)
        p = p.astype(BF16)
        if isinstance(v, (list, tuple)):
            rows = p.shape[0] // len(v)
            o = jnp.concatenate([_qk(p[b * rows:(b + 1) * rows], v_b) if kv_t
                                 else jnp.dot(p[b * rows:(b + 1) * rows], v_b, preferred_element_type=F32)
                                 for b, v_b in enumerate(v)], axis=0)
        else:
            o = _qk(p, v) if kv_t else jnp.dot(p, v, preferred_element_type=F32)
        den = d if den is None else den + d
        out = o if out is None else out + o
    if sink is not None:
        den = den + jnp.exp2(sink - m)
    return out * (1.0 / den)


def _mla_heads(qa_ref, ka_refs, va_refs, o_ref, stack):
    scale = (NOPE_A + ROPE_A) ** -0.5
    rows = qa_ref.shape[0]
    for h0 in range(0, H_A, stack):
        heads = range(h0, h0 + stack)
        scores = [jnp.concatenate([_qk(qa_ref[:, h * LANES:(h + 1) * LANES], ka_ref[:, h * LANES:(h + 1) * LANES])
                                   for h in heads], axis=0) for ka_ref in ka_refs]
        values = [[va_ref[:, h * V_A:(h + 1) * V_A] for h in heads] for va_ref in va_refs]
        o = _softmax_pv(scores, values, scale)
        o = jnp.concatenate([o[b * rows:(b + 1) * rows] for b in range(stack)], axis=1)
        o_ref[:, h0 * V_A:(h0 + stack) * V_A] = o.astype(o_ref.dtype)


def _gqa_groups(q_ref, col0, n_q, ks, vs, mask=None, sink=None, kv_t=False):
    rows = q_ref.shape[0]
    n_groups = len(ks)
    n_seg = len(ks[0])
    per_group = []
    for g in range(n_groups):
        base = col0 + g * n_q * HD
        q = jnp.concatenate([q_ref[:, base + i * HD:base + (i + 1) * HD] for i in range(n_q)], axis=0)
        if kv_t:
            per_group.append([jnp.dot(q, k, preferred_element_type=F32) for k in ks[g]])
        else:
            per_group.append([_qk(q, k) for k in ks[g]])
    scores = [jnp.concatenate([per_group[g][j] for g in range(n_groups)], axis=0) for j in range(n_seg)]
    values = [[vs[g][j] for g in range(n_groups)] for j in range(n_seg)]
    if mask is not None:
        scores[0] = jnp.where(mask, scores[0], NEG_INF)
    o = _softmax_pv(scores, values, HD ** -0.5, sink, kv_t)
    return jnp.concatenate([o[i * rows:(i + 1) * rows] for i in range(n_groups * n_q)], axis=1)


def _stacked_sink(sink_ref, h0, n_q, rows):
    head = lax.broadcasted_iota(jnp.int32, (n_q * rows, 1), 0) // rows
    col = jnp.full((n_q * rows, 1), sink_ref[h0], F32)
    for g in range(1, n_q):
        col = jnp.where(head == g, sink_ref[h0 + g], col)
    return col


def _attn_ab_prompt_kernel(sink_ref, qa_ref, ka_ref, va_ref, qb_ref, kbt_ref, vbt_ref, o_ref):
    _mla_heads(qa_ref, [ka_ref], [va_ref], o_ref, MLA_STACK_PROMPT)
    kbt = kbt_ref[...].astype(BF16)
    vbt = vbt_ref[...].astype(BF16)
    sls = [slice(kh * HD, (kh + 1) * HD) for kh in range(KVH_B)]
    o = _gqa_groups(qb_ref, 0, H_B // KVH_B, [[kbt[sl, :]] for sl in sls], [[vbt[sl, :]] for sl in sls],
                    sink=_stacked_sink(sink_ref, 0, H_B, SEQ), kv_t=True)
    o_ref[:, H_A * V_A:] = o.astype(o_ref.dtype)


def _attn_ab_prompt(sink, qa, ka, va, qb, kbt, vbt):
    def blk(w):
        return pl.BlockSpec((SEQ, w), lambda b: (b, 0))
    tblk = pl.BlockSpec((None, KVH_B * HD, SEQ), lambda b: (b, 0, 0))
    return pl.pallas_call(
        _attn_ab_prompt_kernel,
        grid=(BATCH,),
        in_specs=[pl.BlockSpec(memory_space=pltpu.SMEM), blk(H_A * LANES), blk(H_A * LANES), blk(H_A * V_A),
                  blk(H_B * HD), tblk, tblk],
        out_specs=blk(D_MODEL),
        out_shape=jax.ShapeDtypeStruct((N_PROMPT, H_A * V_A + H_B * HD), BF16),
        compiler_params=_params(("arbitrary",)),
        name="attn_ab_prompt",
    )(sink, qa, ka, va, qb, kbt, vbt)


def _attn_ab_sample_kernel(sink_ref, qa_ref, ka_ref, va_ref, qb_ref, kb_ref, vb_ref,
                           ckv_ctx_ref, kpe_ctx_ref, kb_ctx_ref, vb_ctx_ref, w_kn_ref, w_v_ref, place_ref,
                           o_ref, ka_ctx_ref, va_ctx_ref):
    i = pl.program_id(1)

    @pl.when(i == 0)
    def _():
        ckv16 = ckv_ctx_ref[...].astype(BF16)
        k_nope = jnp.dot(ckv16, w_kn_ref[...], preferred_element_type=F32)
        k_rope = jnp.dot(kpe_ctx_ref[...].astype(BF16), place_ref[...], preferred_element_type=F32)
        ka_ctx_ref[...] = (k_nope + k_rope).astype(BF16)
        va_ctx_ref[...] = jnp.dot(ckv16, w_v_ref[...], preferred_element_type=F32).astype(BF16)

    _mla_heads(qa_ref, [ka_ref, ka_ctx_ref], [va_ref, va_ctx_ref], o_ref, MLA_STACK_SAMPLE)

    n_win = BQ_S + 2 * WINDOW
    start = pl.multiple_of(jnp.clip(i * BQ_S - WINDOW, 0, DEC_SEQ - n_win), WINDOW)
    row = lax.broadcasted_iota(jnp.int32, (H_B * BQ_S, n_win), 0)
    qpos = i * BQ_S + (row & (BQ_S - 1))
    kpos = start + lax.broadcasted_iota(jnp.int32, (H_B * BQ_S, n_win), 1)
    in_band = jnp.abs(qpos - kpos) <= WINDOW
    kwin = kb_ref[pl.ds(start, n_win), :].astype(BF16)
    vwin = vb_ref[pl.ds(start, n_win), :].astype(BF16)
    kctx = kb_ctx_ref[...].astype(BF16)
    vctx = vb_ctx_ref[...].astype(BF16)
    sls = [slice(kh * HD, (kh + 1) * HD) for kh in range(KVH_B)]
    o = _gqa_groups(qb_ref, 0, H_B // KVH_B, [[kwin[:, sl], kctx[:, sl]] for sl in sls],
                    [[vwin[:, sl], vctx[:, sl]] for sl in sls],
                    mask=in_band, sink=_stacked_sink(sink_ref, 0, H_B, BQ_S))
    o_ref[:, H_A * V_A:] = o.astype(o_ref.dtype)


def _attn_ab_sample(sink, qa, ka, va, qb, kb, vb, ckv_ctx, kpe_ctx, kb_ctx, vb_ctx, w_kn, w_v, place):
    qoff = N_PROMPT // BQ_S
    nq = DEC_SEQ // BQ_S
    boff = N_PROMPT // DEC_SEQ

    def qblk(w):
        return pl.BlockSpec((BQ_S, w), lambda b, i: (qoff + b * nq + i, 0))

    def bblk(w):
        return pl.BlockSpec((DEC_SEQ, w), lambda b, i: (boff + b, 0))

    def sblk(w):
        return pl.BlockSpec((DEC_SEQ, w), lambda b, i: (b, 0))

    def cblk(w):
        return pl.BlockSpec((PAST_LEN, w), lambda b, i: (b, 0))

    return pl.pallas_call(
        _attn_ab_sample_kernel,
        grid=(DEC_BATCH, nq),
        in_specs=[pl.BlockSpec(memory_space=pltpu.SMEM), qblk(H_A * LANES), bblk(H_A * LANES), bblk(H_A * V_A),
                  qblk(H_B * HD), sblk(KVH_B * HD), sblk(KVH_B * HD),
                  cblk(KV_LORA), cblk(LANES), cblk(KVH_B * HD), cblk(KVH_B * HD),
                  _const_spec((KV_LORA, H_A * LANES)), _const_spec((KV_LORA, H_A * V_A)),
                  _const_spec((LANES, H_A * LANES))],
        out_specs=pl.BlockSpec((BQ_S, D_MODEL), lambda b, i: (b * nq + i, 0)),
        out_shape=jax.ShapeDtypeStruct((N_SAMPLE, H_A * V_A + H_B * HD), BF16),
        scratch_shapes=[pltpu.VMEM((PAST_LEN, H_A * LANES), BF16), pltpu.VMEM((PAST_LEN, H_A * V_A), BF16)],
        compiler_params=_params(("arbitrary", "arbitrary")),
        name="attn_ab_sample",
    )(sink, qa, ka, va, qb, kb, vb, ckv_ctx, kpe_ctx, kb_ctx, vb_ctx, w_kn, w_v, place)


def _gqa_heads(q_ref, k_list, v_list, o_ref, stack):
    gq = H_C // KVH_C
    for kh0 in range(0, KVH_C, stack):
        sls = [slice(kh * HD, (kh + 1) * HD) for kh in range(kh0, kh0 + stack)]
        o = _gqa_groups(q_ref, kh0 * gq * HD, gq, [[k[:, sl] for k in k_list] for sl in sls],
                        [[v[:, sl] for v in v_list] for sl in sls])
        o_ref[:, kh0 * gq * HD:(kh0 + stack) * gq * HD] = o.astype(o_ref.dtype)


def _attn_c_prompt_kernel(q_ref, k_ref, v_ref, o_ref):
    _gqa_heads(q_ref, [k_ref[...].astype(BF16)], [v_ref[...].astype(BF16)], o_ref, GQA_STACK_PROMPT)


def _attn_c_prompt(q, k, v):
    def blk(w):
        return pl.BlockSpec((SEQ, w), lambda b: (b, 0))
    return pl.pallas_call(
        _attn_c_prompt_kernel,
        grid=(BATCH,),
        in_specs=[blk(H_C * HD), blk(KVH_C * HD), blk(KVH_C * HD)],
        out_specs=blk(H_C * HD),
        out_shape=jax.ShapeDtypeStruct((N_PROMPT, H_C * HD), BF16),
        compiler_params=_params(("arbitrary",)),
        name="attn_c_prompt",
    )(q, k, v)


def _attn_c_sample_kernel(q_ref, k_ref, v_ref, kc_ref, vc_ref, o_ref):
    _gqa_heads(q_ref, [k_ref[...].astype(BF16), kc_ref[...].astype(BF16)],
               [v_ref[...].astype(BF16), vc_ref[...].astype(BF16)], o_ref, GQA_STACK_SAMPLE)


def _attn_c_sample(q, k, v, k_ctx, v_ctx):
    qoff = N_PROMPT // BQ_S
    nq = DEC_SEQ // BQ_S
    w = KVH_C * HD
    return pl.pallas_call(
        _attn_c_sample_kernel,
        grid=(DEC_BATCH, nq),
        in_specs=[pl.BlockSpec((BQ_S, H_C * HD), lambda b, i: (qoff + b * nq + i, 0)),
                  pl.BlockSpec((DEC_SEQ, w), lambda b, i: (b, 0)),
                  pl.BlockSpec((DEC_SEQ, w), lambda b, i: (b, 0)),
                  pl.BlockSpec((PAST_LEN, w), lambda b, i: (b, 0)),
                  pl.BlockSpec((PAST_LEN, w), lambda b, i: (b, 0))],
        out_specs=pl.BlockSpec((BQ_S, H_C * HD), lambda b, i: (b * nq + i, 0)),
        out_shape=jax.ShapeDtypeStruct((N_SAMPLE, H_C * HD), BF16),
        compiler_params=_params(("arbitrary", "arbitrary")),
        name="attn_c_sample",
    )(q, k, v, k_ctx, v_ctx)


def _layer_norm(y, g, b):
    mu = jnp.mean(y, axis=-1, keepdims=True)
    yc = y - mu
    var = jnp.mean(yc * yc, axis=-1, keepdims=True)
    return yc * lax.rsqrt(var + EPS) * g + b


def _route(sel, aff):
    def row(a, j):
        return a[j:j + 1, :]

    scores = []
    for g in range(N_GROUPS):
        a0, a1, a2, a3 = (row(sel, 4 * g + j) for j in range(4))
        hi01, lo01 = jnp.maximum(a0, a1), jnp.minimum(a0, a1)
        hi23, lo23 = jnp.maximum(a2, a3), jnp.minimum(a2, a3)
        top1 = jnp.maximum(hi01, hi23)
        top2 = jnp.maximum(jnp.minimum(hi01, hi23), jnp.maximum(lo01, lo23))
        scores.append(top1 + top2)
    best = scores[0]
    gi = jnp.zeros(best.shape, jnp.int32)
    for g in range(1, N_GROUPS):
        better = scores[g] > best
        gi = jnp.where(better, g, gi)
        best = jnp.where(better, scores[g], best)

    def in_group(a, j):
        out = row(a, j)
        for g in range(1, N_GROUPS):
            out = jnp.where(gi == g, row(a, 4 * g + j), out)
        return out

    v = [in_group(sel, j) for j in range(4)]
    a = [in_group(aff, j) for j in range(4)]
    chosen = []
    for j in range(4):
        rank = jnp.zeros(best.shape, jnp.int32)
        for k in range(4):
            if k == j:
                continue
            ahead = (v[k] >= v[j]) if k < j else (v[k] > v[j])
            rank = rank + ahead.astype(jnp.int32)
        chosen.append(rank < 2)
    total = sum(jnp.where(chosen[j], a[j], 0.0) for j in range(4))
    w = [jnp.where(chosen[j], a[j], 0.0) / total for j in range(4)]
    pair = jnp.zeros(best.shape, jnp.int32)
    for p, (ja, jb) in enumerate(zip(SLOT_A_LOCAL, SLOT_B_LOCAL)):
        pair = jnp.where(chosen[ja] & chosen[jb], p, pair)
    gate_a = jnp.where(pair == 0, w[0], jnp.where(pair <= 2, w[2], w[3]))
    gate_b = jnp.where((pair == 0) | (pair == 1) | (pair == 4), w[1], jnp.where(pair == 5, w[2], w[0]))
    return gi * PAIRS_PER_GROUP + pair, gate_a, gate_b


def _post_attn_kernel(split_x, *refs):
    t = pl.program_id(0)
    op_ref, os_ref = refs[:2]
    if split_x:
        x = _pick(t, refs[2], refs[3])
        refs = refs[4:]
    else:
        x = refs[2][...]
        refs = refs[3:]
    (w_out_ref, gate_ref, shift_ref, scale_ref, lng_ref, lnb_ref, rw_ref, rb_ref, tri_ref, trib_ref,
     x1_ref, route_ref, alloc_ref, rows_hbm,
     rowbuf, posv, poss, fill_ref, cur_ref, free_ref, statev, states, zero_ref,
     pos_sem, row_sem, aux_sem) = refs
    o = _pick(t, op_ref, os_ref)
    subs = _sub_tiles(o.shape[0])
    slot = t % 2
    pslot = (t + 1) % 2
    live = (t < POST_STEPS).astype(F32)

    def rows_done(s):
        pltpu.make_async_copy(rowbuf.at[0], rows_hbm.at[pl.ds(0, TP), :], row_sem.at[s]).wait()

    def pos_copy(s):
        return pltpu.make_async_copy(posv.at[s, 0], poss.at[s], pos_sem.at[s])

    @pl.when(t == 0)
    def _():
        fill_ref[...] = jnp.zeros_like(fill_ref)
        cur_ref[...] = jnp.zeros_like(cur_ref)
        free_ref[...] = jnp.zeros_like(free_ref)
        zero_ref[...] = jnp.zeros_like(zero_ref)
        rowbuf[1] = jnp.zeros(rowbuf.shape[1:], F32)

        def init(r, carry):
            poss[1, r] = MOE_ROWS + r
            return carry
        lax.fori_loop(0, TP, init, 0)

    @pl.when(t > 0)
    def _():
        pos_copy(pslot).wait()
        rows_done(slot)

    @pl.when(poss[pslot, 0] >= 0)
    def _():
        for r in range(TP):
            _row_copy(rowbuf.at[pslot], r, rows_hbm, poss[pslot, r], row_sem.at[pslot]).start(priority=r % 2)

    a = [jnp.dot(o[s], w_out_ref[...], preferred_element_type=F32) for s in subs]
    x1 = [_layer_norm(ALPHA * x[s] + gate_ref[...] * a_s, lng_ref[...], lnb_ref[...]) for s, a_s in zip(subs, a)]
    h2 = [x1_s * (1.0 + scale_ref[...]) + shift_ref[...] for x1_s in x1]
    for s, x1_s, h2_s in zip(subs, x1, h2):
        x1_ref[s, :] = x1_s
        rowbuf[slot, s, :D_MODEL] = h2_s
    rw = rw_ref[...]
    rw_hi = rw.astype(BF16)
    rw_lo = (rw - rw_hi.astype(F32)).astype(BF16)
    rw_both = jnp.concatenate([rw_hi, rw_lo], axis=0)
    logits = []
    for h2_s in h2:
        h2_hi = h2_s.astype(BF16)
        h2_lo = (h2_s - h2_hi.astype(F32)).astype(BF16)
        by_hi = _qk(rw_both, h2_hi)
        logits.append(by_hi[:N_EXPERTS] + by_hi[N_EXPERTS:] + _qk(rw_hi, h2_lo))
    aff = [1.0 / (1.0 + jnp.exp(-l)) for l in logits]
    routed = [_route(aff_s + rb_ref[...], aff_s) for aff_s in aff]

    lane = lax.broadcasted_iota(jnp.int32, (LANES, TM), 0)
    for s, (_, gate_a, gate_b) in zip(subs, routed):
        meta_t = jnp.where(lane == 0, gate_a, jnp.where(lane == 1, gate_b, 0.0))
        rowbuf[slot, s, D_MODEL:] = meta_t.T

    onehots = [lax.broadcasted_iota(jnp.int32, (BUCKET_ROWS, TM), 0) == bucket for bucket, _, _ in routed]
    prefixes = [jnp.dot(oh.astype(BF16), tri_ref[...], preferred_element_type=F32) for oh in onehots]
    fill = fill_ref[...]
    cur = cur_ref[...]
    free = free_ref[...]
    route_ref[...] = jnp.zeros(route_ref.shape, F32)
    for i, (s, onehot, prefix) in enumerate(zip(subs, onehots, prefixes)):
        cnt = jnp.sum(onehot.astype(F32), axis=1, keepdims=True)
        used = fill - TME * jnp.floor(fill * (1.0 / TME))
        takes = ((used == 0.0) & (cnt > 0.0)) | ((used > 0.0) & (used + cnt > TME))
        takes_f = jnp.where(takes, live, 0.0)
        tile_new = free + jnp.dot(trib_ref[...], takes_f.astype(BF16), preferred_element_type=F32)
        tile_first = jnp.where(used == 0.0, tile_new, cur)
        local = jnp.sum(jnp.where(onehot, prefix + used[:, 0:1], 0.0), axis=0, keepdims=True)
        first = jnp.sum(jnp.where(onehot, tile_first[:, 0:1], 0.0), axis=0, keepdims=True)
        second = jnp.sum(jnp.where(onehot, tile_new[:, 0:1], 0.0), axis=0, keepdims=True)
        pos = jnp.where(local < TME, first * TME + local, second * TME + (local - TME))
        route_ref[0:1, s] = pos
        posv[slot, 0:1, s] = pos.astype(jnp.int32)
        alloc_ref[i] = jnp.where(takes_f > 0.0, tile_new, -1.0)
        cur = jnp.where(takes_f > 0.0, tile_new, cur)
        fill = fill + cnt * live
        free = free + jnp.sum(takes_f, axis=0, keepdims=True)
    fill_ref[...] = fill
    cur_ref[...] = cur
    free_ref[...] = free
    pos_copy(slot).start()

    @pl.when(t == POST_STEPS)
    def _():
        pos_copy(slot).wait()
        rows_done(pslot)
        statev[0] = (fill - TME * jnp.floor(fill * (1.0 / TME))).astype(jnp.int32)
        statev[1] = cur.astype(jnp.int32)
        statev[2] = free.astype(jnp.int32)
        state_copy = pltpu.make_async_copy(statev, states, aux_sem)
        state_copy.start()
        state_copy.wait()

        def zero_rows(row0, n):
            cp = pltpu.make_async_copy(zero_ref.at[pl.ds(0, n), :], rows_hbm.at[pl.ds(row0, n), :], aux_sem)
            cp.start()
            cp.wait()

        for b in range(N_BUCKETS):
            used_b = states[0, b, 0]

            @pl.when(used_b > 0)
            def _(b=b, used_b=used_b):
                base = states[1, b, 0] * TME
                aligned = ((used_b + 7) // 8) * 8

                def one(r, carry):
                    zero_rows(base + r, 1)
                    return carry
                lax.fori_loop(used_b, aligned, one, 0)

                def eight(j, carry):
                    zero_rows(pl.multiple_of(base + j * 8, 8), 8)
                    return carry
                lax.fori_loop(aligned // 8, TME // 8, eight, 0)

        def unused(i, carry):
            zero_rows(pl.multiple_of(i * TME, TME), TME)
            return carry
        lax.fori_loop(states[2, 0, 0], MOE_TILES, unused, 0)


def _post_attn(layer, o_p, o_s, x_list, mod, w_out, ln_g, ln_b, rw_t, rb):
    split_x = len(x_list) == 2
    last = POST_STEPS - 1
    p_tiles = N_PROMPT // TP
    n_sub = TP // TM

    def clamped(t):
        return jnp.minimum(t, last)

    tok = pl.BlockSpec((TP, D_MODEL), lambda t: (clamped(t), 0))

    def split_specs():
        return [pl.BlockSpec((TP, D_MODEL), lambda t: (jnp.minimum(t, p_tiles - 1), 0)),
                pl.BlockSpec((TP, D_MODEL), lambda t: (jnp.clip(t - p_tiles, 0, last - p_tiles), 0))]

    def mod_spec(chunk):
        return pl.BlockSpec((None, None, None, 1, D_MODEL),
                            lambda t: (layer, _mod_row(clamped(t), TP), chunk, 0, 0))

    tri = jnp.asarray(np.arange(TM)[:, None] < np.arange(TM)[None, :], BF16)
    trib = jnp.asarray(np.arange(BUCKET_ROWS)[None, :] < np.arange(BUCKET_ROWS)[:, None], BF16)
    x1, route, alloc, rows = pl.pallas_call(
        functools.partial(_post_attn_kernel, split_x),
        grid=(POST_STEPS + 1,),
        in_specs=split_specs() + (split_specs() if split_x else [tok]) + [
            _const_spec((D_MODEL, D_MODEL)), mod_spec(2), mod_spec(3), mod_spec(4),
            _const_spec((1, D_MODEL)), _const_spec((1, D_MODEL)), _const_spec((N_EXPERTS, D_MODEL)),
            _const_spec((N_EXPERTS, 1)), _const_spec((TM, TM)), _const_spec((BUCKET_ROWS, BUCKET_ROWS))],
        out_specs=[tok, pl.BlockSpec((8, TP), lambda t: (0, t)),
                   pl.BlockSpec((n_sub, BUCKET_ROWS, LANES), lambda t: (t, 0, 0)),
                   pl.BlockSpec(memory_space=pl.ANY)],
        out_shape=[jax.ShapeDtypeStruct((N_TOK, D_MODEL), F32),
                   jax.ShapeDtypeStruct((8, (POST_STEPS + 1) * TP), F32),
                   jax.ShapeDtypeStruct(((POST_STEPS + 1) * n_sub, BUCKET_ROWS, LANES), F32),
                   jax.ShapeDtypeStruct((MOE_ROWS + TP, ROW_W), F32)],
        scratch_shapes=[pltpu.VMEM((2, TP, ROW_W), F32), pltpu.VMEM((2, 8, TP), jnp.int32),
                        pltpu.SMEM((2, TP), jnp.int32),
                        pltpu.VMEM((BUCKET_ROWS, LANES), F32), pltpu.VMEM((BUCKET_ROWS, LANES), F32),
                        pltpu.VMEM((BUCKET_ROWS, LANES), F32),
                        pltpu.VMEM((3, BUCKET_ROWS, LANES), jnp.int32), pltpu.SMEM((3, BUCKET_ROWS, LANES), jnp.int32),
                        pltpu.VMEM((TME, ROW_W), F32),
                        pltpu.SemaphoreType.DMA((2,)), pltpu.SemaphoreType.DMA((2,)), pltpu.SemaphoreType.DMA],
        compiler_params=_params(("arbitrary",)),
        name="post_attn",
    )(o_p, o_s, *x_list, w_out, mod, mod, mod, ln_g, ln_b, rw_t, rb, tri, trib)
    return x1, route[0, :N_TOK], alloc[:POST_STEPS * n_sub, :, 0], rows


def _moe_kernel(ta_ref, tb_ref, rd_ref, wr_ref, nt_ref, x_ref, wga_ref, wua_ref, wda_ref, wgb_ref, wub_ref, wdb_ref,
                o_ref, wga_s, wua_s, wda_s, wgb_s, wub_s, wdb_s):
    del rd_ref, wr_ref
    i = pl.program_id(0)
    prev = jnp.maximum(i - 1, 0)
    slots = ((ta_ref, (wga_ref, wua_ref, wda_ref), (wga_s, wua_s, wda_s)),
             (tb_ref, (wgb_ref, wub_ref, wdb_ref), (wgb_s, wub_s, wdb_s)))

    for t_ref, w_refs, w_scr in slots:
        @pl.when((i == 0) | (t_ref[i] != t_ref[prev]))
        def _(w_refs=w_refs, w_scr=w_scr):
            for w_ref, s_ref in zip(w_refs, w_scr):
                s_ref[...] = w_ref[...].astype(BF16)

    @pl.when(i < nt_ref[0])
    def _():
        x = x_ref[:, :D_MODEL].astype(BF16)
        acc = None
        for slot, (_, _, (wg_s, wu_s, wd_s)) in enumerate(slots):
            gt = jnp.dot(x, wg_s[...], preferred_element_type=F32)
            up = jnp.dot(x, wu_s[...], preferred_element_type=F32)
            gate = x_ref[:, D_MODEL + slot:D_MODEL + slot + 1]
            hid = (gt / (1.0 + jnp.exp(-gt))) * up * gate
            y = jnp.dot(hid.astype(BF16), wd_s[...], preferred_element_type=F32)
            acc = y if acc is None else acc + y
        o_ref[...] = acc

    @pl.when(i >= nt_ref[0])
    def _():
        o_ref[...] = jnp.zeros_like(o_ref)


def _moe(layer, tile_a, tile_b, read_tile, write_tile, n_tiles, rows, w_gate, w_up, w_down):
    def wspec(shape, which):
        if which == 0:
            return pl.BlockSpec((None, None) + shape, lambda i, ta, tb, rd, wr, nt: (layer, ta[i], 0, 0))
        return pl.BlockSpec((None, None) + shape, lambda i, ta, tb, rd, wr, nt: (layer, tb[i], 0, 0))

    up_shape, down_shape = (D_MODEL, D_EXPERT), (D_EXPERT, D_MODEL)
    grid_spec = pltpu.PrefetchScalarGridSpec(
        num_scalar_prefetch=5,
        grid=(MOE_TILES,),
        in_specs=[pl.BlockSpec((TME, ROW_W), lambda i, ta, tb, rd, wr, nt: (rd[i], 0)),
                  wspec(up_shape, 0), wspec(up_shape, 0), wspec(down_shape, 0),
                  wspec(up_shape, 1), wspec(up_shape, 1), wspec(down_shape, 1)],
        out_specs=pl.BlockSpec((TME, D_MODEL), lambda i, ta, tb, rd, wr, nt: (wr[i], 0)),
        scratch_shapes=[pltpu.VMEM(s, BF16) for s in (up_shape, up_shape, down_shape) * 2],
    )
    return pl.pallas_call(
        _moe_kernel,
        grid_spec=grid_spec,
        out_shape=jax.ShapeDtypeStruct((MOE_ROWS, D_MODEL), F32),
        compiler_params=_params(("arbitrary",)),
        name="moe_experts",
    )(tile_a, tile_b, read_tile, write_tile, n_tiles, rows, w_gate, w_up, w_down, w_gate, w_up, w_down)


def _row_copy(src_ref, src_row, dst_ref, dst_row, sem):
    return pltpu.make_async_copy(src_ref.at[pl.ds(src_row, 1), :], dst_ref.at[pl.ds(dst_row, 1), :], sem)


def _plan(alloc):
    ids = jnp.arange(MOE_TILES, dtype=F32)
    hit = alloc[:, :, None] == ids[None, None, :]
    buckets = jnp.arange(BUCKET_ROWS, dtype=jnp.int32)
    tile_bucket = jnp.sum(jnp.where(hit, buckets[None, :, None], 0), axis=(0, 1))
    in_use = jnp.any(hit, axis=(0, 1))
    n_tiles = jnp.sum(in_use.astype(jnp.int32))
    key = jnp.where(in_use, tile_bucket, N_BUCKETS)
    idx = jnp.arange(MOE_TILES, dtype=jnp.int32)
    before = (key[None, :] < key[:, None]) | ((key[None, :] == key[:, None]) & (idx[None, :] < idx[:, None]))
    rank = jnp.sum(before.astype(jnp.int32), axis=1)
    at_step = rank[None, :] == idx[:, None]
    order = jnp.sum(jnp.where(at_step, idx[None, :], 0), axis=1)
    step_bucket = jnp.sum(jnp.where(at_step, key[None, :], 0), axis=1)
    last = jnp.maximum(n_tiles - 1, 0)
    live = idx < n_tiles
    read_tile = jnp.where(live, order, jnp.sum(jnp.where(idx == last, order, 0)))
    write_tile = jnp.where(live, order, idx)
    step_bucket = jnp.where(live, step_bucket, jnp.sum(jnp.where(idx == last, step_bucket, 0)))
    step_bucket = jnp.minimum(step_bucket, N_BUCKETS - 1)
    group, pair = step_bucket // PAIRS_PER_GROUP, step_bucket % PAIRS_PER_GROUP
    slot_a = jnp.asarray(SLOT_A_LOCAL, jnp.int32)
    slot_b = jnp.asarray(SLOT_B_LOCAL, jnp.int32)
    pair_hot = pair[:, None] == jnp.arange(PAIRS_PER_GROUP, dtype=jnp.int32)[None, :]
    tile_a = group * EXPERTS_PER_GROUP + jnp.sum(jnp.where(pair_hot, slot_a[None, :], 0), axis=1)
    tile_b = group * EXPERTS_PER_GROUP + jnp.sum(jnp.where(pair_hot, slot_b[None, :], 0), axis=1)
    return (tile_a.astype(jnp.int32), tile_b.astype(jnp.int32), read_tile.astype(jnp.int32),
            write_tile.astype(jnp.int32), n_tiles.reshape(1))


def _post_moe_kernel(split_out, pos_ref, f_ref, x1_ref, gate_ref, lng_ref, lnb_ref, *refs):
    out_refs, (fbuf, sem) = refs[:-2], refs[-2:]
    t = pl.program_id(0)

    def wait_tile(slot):
        pltpu.make_async_copy(f_ref.at[pl.ds(0, TM), :], fbuf.at[slot], sem.at[slot]).wait()

    @pl.when(t == 0)
    def _():
        def issue(r, carry):
            _row_copy(f_ref, pos_ref[r], fbuf.at[r // TM], r % TM, sem.at[r // TM]).start()
            return carry
        lax.fori_loop(0, (GATHER_SLOTS - 1) * TM, issue, 0, unroll=8)

    slot = t % GATHER_SLOTS
    wait_tile(slot)
    ahead = jnp.minimum(t + GATHER_SLOTS - 1, N_TILES - 1)
    aslot = (t + GATHER_SLOTS - 1) % GATHER_SLOTS
    for r in range(TM):
        _row_copy(f_ref, pos_ref[ahead * TM + r], fbuf.at[aslot], r, sem.at[aslot]).start(priority=r % 2)
    y = _layer_norm(ALPHA * x1_ref[...] + gate_ref[...] * fbuf[slot], lng_ref[...], lnb_ref[...])

    @pl.when(t == N_TILES - 1)
    def _():
        for k in range(1, GATHER_SLOTS):
            wait_tile((t + k) % GATHER_SLOTS)
    if split_out:
        @pl.when(t < PROMPT_TILES)
        def _():
            out_refs[0][...] = y

        @pl.when(t >= PROMPT_TILES)
        def _():
            out_refs[1][...] = y
    else:
        out_refs[0][...] = y


def _post_moe(layer, pos, f_sorted, x1, mod, ln_g, ln_b, split_out):
    tok = pl.BlockSpec((TM, D_MODEL), lambda t, p: (t, 0))
    if split_out:
        out_specs = [pl.BlockSpec((TM, D_MODEL), lambda t, p: (jnp.minimum(t, PROMPT_TILES - 1), 0)),
                     pl.BlockSpec((TM, D_MODEL), lambda t, p: (jnp.maximum(t - PROMPT_TILES, 0), 0))]
        out_shape = [jax.ShapeDtypeStruct((N_PROMPT, D_MODEL), F32), jax.ShapeDtypeStruct((N_SAMPLE, D_MODEL), F32)]
    else:
        out_specs = [tok]
        out_shape = [jax.ShapeDtypeStruct((N_TOK, D_MODEL), F32)]
    mod_spec = pl.BlockSpec((None, None, None, 1, D_MODEL), lambda t, p: (layer, _mod_row(t), 5, 0, 0))
    return pl.pallas_call(
        functools.partial(_post_moe_kernel, split_out),
        grid_spec=pltpu.PrefetchScalarGridSpec(
            num_scalar_prefetch=1, grid=(N_TILES,),
            in_specs=[pl.BlockSpec(memory_space=pl.ANY), tok, mod_spec,
                      pl.BlockSpec((1, D_MODEL), lambda t, p: (0, 0)), pl.BlockSpec((1, D_MODEL), lambda t, p: (0, 0))],
            out_specs=out_specs,
            scratch_shapes=[pltpu.VMEM((GATHER_SLOTS, TM, D_MODEL), F32),
                            pltpu.SemaphoreType.DMA((GATHER_SLOTS,))]),
        out_shape=out_shape,
        compiler_params=_params(("arbitrary",)),
        name="post_moe",
    )(pos, f_sorted, x1, mod, ln_g, ln_b)


def _ffn(layer, o_p, o_s, x_list, mod, w_out, ln_g, ln_b, rw_t, rb, w_gate, w_up, w_down, split_out):
    x1, pos, alloc, rows_sorted = _post_attn(layer, o_p, o_s, x_list, mod, w_out, ln_g[layer, 0][None],
                                             ln_b[layer, 0][None], rw_t, rb)
    tile_a, tile_b, read_tile, write_tile, n_tiles = _plan(alloc)
    f_sorted = _moe(layer, tile_a, tile_b, read_tile, write_tile, n_tiles, rows_sorted, w_gate, w_up, w_down)
    return _post_moe(layer, pos.astype(jnp.int32), f_sorted, x1, mod, ln_g[layer, 1][None], ln_b[layer, 1][None],
                     split_out)


def kernel(x_prompt, x_sample, c, cache_mla_ckv, cache_mla_kpe, cache_swa_k, cache_swa_v, cache_gqa_k, cache_gqa_v, c_ctx, w_mod, b_mod, ln_g, ln_b, w_in_ab, mla_q_norm, mla_w_uq, mla_kv_norm, mla_w_ukv, swa_sink, w_out_ab, w_in_c, gqa_q_norm, gqa_k_norm, w_out_c, router_w, router_bias, exp_w_gate, exp_w_up, exp_w_down):
    xp = x_prompt.reshape(N_PROMPT, D_MODEL)
    xs = x_sample.reshape(N_SAMPLE, D_MODEL)
    cond = jnp.concatenate([c_ctx[None], c, jnp.zeros((MOD_ROWS - 1 - DEC_BATCH, D_MODEL), F32)], axis=0)
    mod = _modulation(cond, w_mod, b_mod).reshape(DEPTH, MOD_ROWS, 6, 1, D_MODEL)
    tabs = _rope_tables()
    rw_t = router_w.T
    rb = router_bias.reshape(N_EXPERTS, 1)

    w = w_in_ab[0]
    pad = jnp.zeros((D_MODEL, AB_COLS - w.shape[1]), F32)
    w_in = jnp.concatenate([w[:, :384], w[:, 416:], w[:, 384:416], pad], axis=1).astype(BF16)
    w_uq, w_kn, w_v, place = _mla_weights(mla_w_uq[0], mla_w_ukv[0])
    qa, ka, va, qb, ckv_p, kpe_p, kbt_p, vbt_p, kb_s, vb_s = _inproj_ab(
        xp, xs, mod, w_in, mla_q_norm[0][None], mla_kv_norm[0][None], w_uq, w_kn, w_v, place, tabs)
    sink = swa_sink[0]
    o_p = _attn_ab_prompt(sink, qa, ka, va, qb, kbt_p, vbt_p)
    kpe_ctx = jnp.pad(cache_mla_kpe[:, 0].reshape(-1, ROPE_A), ((0, 0), (0, LANES - ROPE_A)))
    o_s = _attn_ab_sample(sink, qa, ka, va, qb, kb_s, vb_s,
                          cache_mla_ckv[:, 0].reshape(-1, KV_LORA), kpe_ctx,
                          cache_swa_k[:, 0].reshape(-1, KVH_B * HD), cache_swa_v[:, 0].reshape(-1, KVH_B * HD),
                          w_kn, w_v, place)
    (x2,) = _ffn(0, o_p, o_s, [xp, xs], mod, w_out_ab[0].astype(BF16), ln_g, ln_b, rw_t, rb,
                 exp_w_gate, exp_w_up, exp_w_down, split_out=False)

    g_full = jnp.concatenate([jnp.tile(gqa_q_norm[0], H_C), jnp.tile(gqa_k_norm[0], KVH_C)])[None]
    qc, kc_p, vc_p, kc_s, vc_s = _inproj_c(x2, mod, w_in_c[0].astype(BF16), g_full, tabs)
    oc_p = _attn_c_prompt(qc, kc_p, vc_p)
    oc_s = _attn_c_sample(qc, kc_s, vc_s, cache_gqa_k[:, 0].reshape(-1, KVH_C * HD),
                          cache_gqa_v[:, 0].reshape(-1, KVH_C * HD))
    y_p, y_s = _ffn(1, oc_p, oc_s, [x2], mod, w_out_c[0].astype(BF16), ln_g, ln_b, rw_t, rb,
                    exp_w_gate, exp_w_up, exp_w_down, split_out=True)

    y_prompt = y_p.reshape(BATCH, SEQ, D_MODEL)
    y_sample = y_s.reshape(DEC_BATCH, DEC_SEQ, D_MODEL)
    new_ckv = ckv_p.reshape(BATCH, 1, SEQ, KV_LORA)
    new_kpe = kpe_p.reshape(BATCH, 1, SEQ, ROPE_A)
    new_swk = kbt_p.reshape(BATCH, 1, KVH_B, HD, SEQ).transpose(0, 1, 4, 2, 3)
    new_swv = vbt_p.reshape(BATCH, 1, KVH_B, HD, SEQ).transpose(0, 1, 4, 2, 3)
    new_gk = kc_p.reshape(BATCH, 1, SEQ, KVH_C, HD)
    new_gv = vc_p.reshape(BATCH, 1, SEQ, KVH_C, HD)
    return y_prompt, y_sample, new_ckv, new_kpe, new_swk, new_swv, new_gk, new_gv
```

```python
import functools

import numpy as np
import jax
import jax.numpy as jnp
from jax import lax
from jax.experimental import pallas as pl
from jax.experimental.pallas import tpu as pltpu

D_MODEL = 1024
BATCH = 32
SEQ = 256
DEPTH = 2
DEC_BATCH = 4
DEC_SEQ = 1024
PAST_LEN = 256
GRID_W = 64
ROPE_THETA = 10000.0
WINDOW = 128
HD = 64
H_A = 8
NOPE_A = 64
ROPE_A = 32
V_A = 64
Q_LORA = 256
KV_LORA = 128
H_B = 8
KVH_B = 2
H_C = 16
KVH_C = 4
N_EXPERTS = 16
N_GROUPS = 4
EXPERTS_PER_GROUP = 4
D_EXPERT = 512
ALPHA = (2 * DEPTH) ** 0.25
NEG_INF = -1e30
EPS = 1e-6

LANES = 128
N_PROMPT = BATCH * SEQ
N_SAMPLE = DEC_BATCH * DEC_SEQ
N_TOK = N_PROMPT + N_SAMPLE
TM = 256
TP = 1024
TP_AB = 512
TQ = 512
N_TILES = N_TOK // TM
BQ_S = 256
LOG2E = 1.4426950408889634
MLA_STACK_PROMPT = 4
MLA_STACK_SAMPLE = 4
GQA_STACK_PROMPT = 4
GQA_STACK_SAMPLE = 2
MOD_ROWS = 8

PAIRS_PER_GROUP = 6
N_BUCKETS = N_GROUPS * PAIRS_PER_GROUP
SLOT_A_LOCAL = (0, 2, 2, 3, 3, 3)
SLOT_B_LOCAL = (1, 1, 0, 0, 1, 2)
TME = 256
MOE_TILES = -(-(N_TOK + N_BUCKETS * (TME - 1)) // TME)
MOE_ROWS = MOE_TILES * TME

BUCKET_ROWS = 32
ROW_W = D_MODEL + LANES
STAGE_SLOTS = 3
GATHER_SLOTS = 3

AB_COLS = 1280
C_NORM_COLS = H_C * HD + KVH_C * HD
NORM_BLOCK = 256

BF16 = jnp.bfloat16
F32 = jnp.float32
VMEM_LIMIT = 52 * 1024 * 1024


def _mod_row(t, tile=TM):
    p_tiles = N_PROMPT // tile
    return jnp.where(t < p_tiles, 0, 1 + (t - p_tiles) // (DEC_SEQ // tile))


def _mod_spec(layer, chunk, tile=TM):
    return pl.BlockSpec((None, None, None, 1, D_MODEL), lambda t: (layer, _mod_row(t, tile), chunk, 0, 0))


def _const_spec(shape):
    nd = len(shape)
    return pl.BlockSpec(shape, lambda *_: (0,) * nd)


def _split_specs(tile=TM):
    p_tiles = N_PROMPT // tile
    return [
        pl.BlockSpec((tile, D_MODEL), lambda t: (jnp.minimum(t, p_tiles - 1), 0)),
        pl.BlockSpec((tile, D_MODEL), lambda t: (jnp.maximum(t - p_tiles, 0), 0)),
    ]


def _group_spec(tile, width, prompt):
    p_tiles = N_PROMPT // tile
    if prompt:
        return pl.BlockSpec((tile, width), lambda t: (jnp.minimum(t, p_tiles - 1), 0))
    return pl.BlockSpec((tile, width), lambda t: (jnp.maximum(t - p_tiles, 0), 0))


def _pick(t, a_ref, b_ref):
    return jnp.where(t < N_PROMPT // a_ref.shape[0], a_ref[...], b_ref[...])


def _sub_tiles(rows):
    return [slice(i * TM, (i + 1) * TM) for i in range(rows // TM)]


def _params(sem):
    return pltpu.CompilerParams(dimension_semantics=sem, vmem_limit_bytes=VMEM_LIMIT)


def _mod_kernel(cond_ref, w_ref, b_ref, o_ref):
    c = cond_ref[...]
    s = (c / (1.0 + jnp.exp(-c))).astype(BF16)
    o_ref[...] = jnp.dot(s, w_ref[...].astype(BF16), preferred_element_type=F32) + b_ref[...]


def _modulation(cond, w_mod, b_mod):
    tn = 1536
    return pl.pallas_call(
        _mod_kernel,
        grid=(DEPTH, 6 * D_MODEL // tn),
        in_specs=[
            pl.BlockSpec((MOD_ROWS, D_MODEL), lambda l, j: (0, 0)),
            pl.BlockSpec((None, D_MODEL, tn), lambda l, j: (l, 0, j)),
            pl.BlockSpec((None, 1, tn), lambda l, j: (l, 0, j)),
        ],
        out_specs=pl.BlockSpec((None, MOD_ROWS, tn), lambda l, j: (l, 0, j)),
        out_shape=jax.ShapeDtypeStruct((DEPTH, MOD_ROWS, 6 * D_MODEL), F32),
        compiler_params=_params(("arbitrary", "arbitrary")),
        name="modulation",
    )(cond, w_mod, b_mod.reshape(DEPTH, 1, 6 * D_MODEL))


def _grid_cos_sin(half):
    n_freq = half // 2
    rows = DEC_SEQ // GRID_W
    row = jnp.repeat(jnp.arange(rows, dtype=F32), GRID_W)
    col = jnp.tile(jnp.arange(GRID_W, dtype=F32), rows)
    inv = 1.0 / (ROPE_THETA ** (jnp.arange(n_freq, dtype=F32) / n_freq))
    ang = jnp.concatenate([row[:, None] * inv, col[:, None] * inv], axis=-1)
    return jnp.cos(ang), jnp.sin(ang)


def _rope_tables():
    tabs = []
    for half in (HD // 2, ROPE_A // 2):
        cos, sin = _grid_cos_sin(half)
        reps = LANES // (2 * half)
        tabs.append(jnp.tile(jnp.concatenate([cos, cos], axis=-1), (1, reps)))
        tabs.append(jnp.tile(jnp.concatenate([-sin, sin], axis=-1), (1, reps)))
    half = ROPE_A // 2
    one = jnp.ones((DEC_SEQ, NOPE_A), F32)
    zero = jnp.zeros((DEC_SEQ, NOPE_A), F32)
    pad1 = jnp.ones((DEC_SEQ, LANES - NOPE_A - ROPE_A), F32)
    pad0 = jnp.zeros((DEC_SEQ, LANES - NOPE_A - ROPE_A), F32)
    zh = jnp.zeros((DEC_SEQ, half), F32)
    tabs.append(jnp.concatenate([one, cos, cos, pad1], axis=-1))
    tabs.append(jnp.concatenate([zero, -sin, zh, pad0], axis=-1))
    tabs.append(jnp.concatenate([zero, zh, sin, pad0], axis=-1))
    return tuple(tabs)


def _rope(x, c, s, half):
    w = x.shape[1]
    reps = w // LANES
    if reps > 1:
        c = jnp.concatenate([c] * reps, axis=1)
        s = jnp.concatenate([s] * reps, axis=1)
    ahead = pltpu.roll(x, w - half, 1)
    behind = pltpu.roll(x, half, 1)
    lane = lax.broadcasted_iota(jnp.int32, x.shape, 1)
    swapped = jnp.where((lane & (2 * half - 1)) < half, ahead, behind)
    return x * c + swapped * s


def _rope_specs(n, tile):
    p_tiles = N_PROMPT // tile

    def idx(t):
        return (jnp.maximum(t - p_tiles, 0) % (DEC_SEQ // tile), 0)
    return [pl.BlockSpec((tile, LANES), idx)] * n


def _rms(x, g_ref):
    return x * lax.rsqrt(jnp.mean(x * x, axis=-1, keepdims=True) + EPS) * g_ref[...]


def _inproj_ab_kernel(xp_ref, xs_ref, shift_ref, scale_ref, w_in_ref, qn_ref, kvn_ref, w_uq_ref, w_kn_ref, w_v_ref,
                      place_ref, c64_ref, s64_ref, c32_ref, s32_ref, ca_ref, sa_ref, sb_ref,
                      qa_ref, ka_ref, va_ref, qb_ref, ckv_p_ref, kpe_p_ref, kbt_p_ref, vbt_p_ref, kb_s_ref, vb_s_ref):
    t = pl.program_id(0)
    x = _pick(t, xp_ref, xs_ref)
    subs = _sub_tiles(x.shape[0])
    h = [(x[s] * (1.0 + scale_ref[...]) + shift_ref[...]).astype(BF16) for s in subs]
    z = [jnp.dot(h_s, w_in_ref[...], preferred_element_type=F32) for h_s in h]
    ql = [_rms(z_s[:, :Q_LORA], qn_ref).astype(BF16) for z_s in z]
    qa = [jnp.dot(ql_s, w_uq_ref[...], preferred_element_type=F32) for ql_s in ql]
    ckv = [_rms(z_s[:, Q_LORA:Q_LORA + KV_LORA], kvn_ref) for z_s in z]
    k_nope = [jnp.dot(c_s.astype(BF16), w_kn_ref[...], preferred_element_type=F32) for c_s in ckv]
    va = [jnp.dot(c_s.astype(BF16), w_v_ref[...], preferred_element_type=F32).astype(BF16) for c_s in ckv]
    for s, va_s in zip(subs, va):
        va_ref[s, :] = va_s
    qb = [z_s[:, 384:896] for z_s in z]
    kb = [z_s[:, 896:1024] for z_s in z]
    vb = [z_s[:, 1024:1152] for z_s in z]
    kpe = [z_s[:, 1152:1280] for z_s in z]

    def finish(qa_out, qb_out, kpe_out):
        k_rope = [jnp.dot(kpe_s.astype(BF16), place_ref[...], preferred_element_type=F32) for kpe_s in kpe_out]
        for i, s in enumerate(subs):
            qa_ref[s, :] = qa_out[i].astype(BF16)
            qb_ref[s, :] = qb_out[i].astype(BF16)
            ka_ref[s, :] = (k_nope[i] + k_rope[i]).astype(BF16)

    p_tiles = N_PROMPT // x.shape[0]

    @pl.when(t < p_tiles)
    def _():
        finish(qa, qb, kpe)
        for i, s in enumerate(subs):
            ckv_p_ref[s, :] = ckv[i]
            kpe_p_ref[s, :] = kpe[i][:, :ROPE_A]
            kbt_p_ref[i] = kb[i].T
            vbt_p_ref[i] = vb[i].T

    @pl.when(t >= p_tiles)
    def _():
        w = qa[0].shape[1]
        reps = w // LANES
        half = ROPE_A // 2
        qa_rot, qb_rot, kb_rot, kpe_rot = [], [], [], []
        for i, s in enumerate(subs):
            ca = jnp.concatenate([ca_ref[s, :]] * reps, axis=1)
            sa = jnp.concatenate([sa_ref[s, :]] * reps, axis=1)
            sb = jnp.concatenate([sb_ref[s, :]] * reps, axis=1)
            qa_rot.append(qa[i] * ca + pltpu.roll(qa[i], w - half, 1) * sa + pltpu.roll(qa[i], half, 1) * sb)
            qb_rot.append(_rope(qb[i], c64_ref[s, :], s64_ref[s, :], HD // 2))
            kb_rot.append(_rope(kb[i], c64_ref[s, :], s64_ref[s, :], HD // 2))
            kpe_rot.append(_rope(kpe[i], c32_ref[s, :], s32_ref[s, :], ROPE_A // 2))
        finish(qa_rot, qb_rot, kpe_rot)
        for i, s in enumerate(subs):
            kb_s_ref[s, :] = kb_rot[i]
            vb_s_ref[s, :] = vb[i]


def _mla_weights(w_uq, w_ukv):
    pad = LANES - NOPE_A - ROPE_A
    uq = w_uq.reshape(Q_LORA, H_A, NOPE_A + ROPE_A)
    uq = jnp.pad(uq, ((0, 0), (0, 0), (0, pad))).reshape(Q_LORA, H_A * LANES)
    ukv = w_ukv.reshape(KV_LORA, H_A, NOPE_A + V_A)
    kn = jnp.pad(ukv[:, :, :NOPE_A], ((0, 0), (0, 0), (0, LANES - NOPE_A))).reshape(KV_LORA, H_A * LANES)
    wv = ukv[:, :, NOPE_A:].reshape(KV_LORA, H_A * V_A)
    src = np.arange(LANES)[:, None]
    dst = np.arange(H_A * LANES)[None, :] % LANES
    place = (src < ROPE_A) & (dst == src + NOPE_A)
    return uq.astype(BF16), kn.astype(BF16), wv.astype(BF16), jnp.asarray(place, BF16)


def _inproj_ab(xp, xs, mod, w_in, q_norm, kv_norm, w_uq, w_kn, w_v, place, tabs):
    widths = [(H_A * LANES, BF16), (H_A * LANES, BF16), (H_A * V_A, BF16), (H_B * HD, BF16)]
    tp = TP_AB
    out_specs = [pl.BlockSpec((tp, w), lambda t: (t, 0)) for w, _ in widths]
    out_shape = [jax.ShapeDtypeStruct((N_TOK, w), dt) for w, dt in widths]
    for w in (KV_LORA, ROPE_A):
        out_specs.append(_group_spec(tp, w, prompt=True))
        out_shape.append(jax.ShapeDtypeStruct((N_PROMPT, w), F32))
    assert SEQ == TM
    p_tiles = N_PROMPT // tp
    for _ in range(2):
        out_specs.append(pl.BlockSpec((tp // SEQ, KVH_B * HD, SEQ), lambda t: (jnp.minimum(t, p_tiles - 1), 0, 0)))
        out_shape.append(jax.ShapeDtypeStruct((BATCH, KVH_B * HD, SEQ), F32))
    for w in (KVH_B * HD, KVH_B * HD):
        out_specs.append(_group_spec(tp, w, prompt=False))
        out_shape.append(jax.ShapeDtypeStruct((N_SAMPLE, w), F32))
    return pl.pallas_call(
        _inproj_ab_kernel,
        grid=(N_TOK // tp,),
        in_specs=_split_specs(tp) + [_mod_spec(0, 0, tp), _mod_spec(0, 1, tp), _const_spec((D_MODEL, AB_COLS)),
                                     _const_spec((1, Q_LORA)), _const_spec((1, KV_LORA)),
                                     _const_spec((Q_LORA, H_A * LANES)), _const_spec((KV_LORA, H_A * LANES)),
                                     _const_spec((KV_LORA, H_A * V_A)), _const_spec((LANES, H_A * LANES))]
        + _rope_specs(7, tp),
        out_specs=out_specs,
        out_shape=out_shape,
        compiler_params=_params(("arbitrary",)),
        name="inproj_ab",
    )(xp, xs, mod, mod, w_in, q_norm, kv_norm, w_uq, w_kn, w_v, place, *tabs)


def _inproj_c_kernel(x_ref, shift_ref, scale_ref, w_in_ref, g_ref, ones_ref, c64_ref, s64_ref,
                     q_ref, k_p_ref, v_p_ref, k_s_ref, v_s_ref):
    t = pl.program_id(0)
    subs = _sub_tiles(x_ref.shape[0])
    h = [(x_ref[s, :] * (1.0 + scale_ref[...]) + shift_ref[...]).astype(BF16) for s in subs]
    z = [jnp.dot(h_s, w_in_ref[...], preferred_element_type=F32) for h_s in h]
    qk = [z_s[:, :C_NORM_COLS] for z_s in z]
    sq = [qk_s * qk_s for qk_s in qk]
    sq_hi = [sq_s.astype(BF16) for sq_s in sq]
    sq_lo = [(sq_s - hi_s.astype(F32)).astype(BF16) for sq_s, hi_s in zip(sq, sq_hi)]
    ss = []
    for hi_s, lo_s in zip(sq_hi, sq_lo):
        blocks = []
        for j in range(C_NORM_COLS // NORM_BLOCK):
            sl = slice(j * NORM_BLOCK, (j + 1) * NORM_BLOCK)
            blocks.append(jnp.dot(hi_s[:, sl], ones_ref[...], preferred_element_type=F32)
                          + jnp.dot(lo_s[:, sl], ones_ref[...], preferred_element_type=F32))
        ss.append(jnp.concatenate(blocks, axis=1))
    qk = [qk_s * lax.rsqrt(ss_s * (1.0 / HD) + EPS) * g_ref[...] for qk_s, ss_s in zip(qk, ss)]
    p_tiles = N_PROMPT // x_ref.shape[0]

    @pl.when(t < p_tiles)
    def _():
        for s, qk_s, z_s in zip(subs, qk, z):
            q_ref[s, :] = qk_s[:, :H_C * HD].astype(BF16)
            k_p_ref[s, :] = qk_s[:, H_C * HD:]
            v_p_ref[s, :] = z_s[:, C_NORM_COLS:]

    @pl.when(t >= p_tiles)
    def _():
        for s, qk_s, z_s in zip(subs, qk, z):
            q_ref[s, :] = _rope(qk_s[:, :H_C * HD], c64_ref[s, :], s64_ref[s, :], HD // 2).astype(BF16)
            k_s_ref[s, :] = _rope(qk_s[:, H_C * HD:], c64_ref[s, :], s64_ref[s, :], HD // 2)
            v_s_ref[s, :] = z_s[:, C_NORM_COLS:]


def _inproj_c(x, mod, w_in, g_full, tabs):
    def tok(w):
        return pl.BlockSpec((TP, w), lambda t: (t, 0))
    kv_w = KVH_C * HD
    head = np.arange(NORM_BLOCK) // HD
    ones = jnp.asarray(head[:, None] == head[None, :], BF16)
    return pl.pallas_call(
        _inproj_c_kernel,
        grid=(N_TOK // TP,),
        in_specs=[tok(D_MODEL), _mod_spec(1, 0, TP), _mod_spec(1, 1, TP),
                  _const_spec((D_MODEL, C_NORM_COLS + KVH_C * HD)),
                  _const_spec((1, C_NORM_COLS)), _const_spec((NORM_BLOCK, NORM_BLOCK))] + _rope_specs(2, TP),
        out_specs=[tok(H_C * HD), _group_spec(TP, kv_w, True), _group_spec(TP, kv_w, True),
                   _group_spec(TP, kv_w, False), _group_spec(TP, kv_w, False)],
        out_shape=[jax.ShapeDtypeStruct((N_TOK, H_C * HD), BF16),
                   jax.ShapeDtypeStruct((N_PROMPT, kv_w), F32), jax.ShapeDtypeStruct((N_PROMPT, kv_w), F32),
                   jax.ShapeDtypeStruct((N_SAMPLE, kv_w), F32), jax.ShapeDtypeStruct((N_SAMPLE, kv_w), F32)],
        compiler_params=_params(("arbitrary",)),
        name="inproj_c",
    )(x, mod, mod, w_in, g_full, ones, *tabs[:2])


def _qk(q, k):
    return lax.dot_general(q, k, (((1,), (1,)), ((), ())), preferred_element_type=F32)


def _softmax_pv(scores, values, scale, sink=None, kv_t=False):
    c = scale * LOG2E
    m = scores[0].max(axis=-1, keepdims=True)
    for s in scores[1:]:
        m = jnp.maximum(m, s.max(axis=-1, keepdims=True))
    m = m * c
    if sink is not None:
        sink = sink * LOG2E
        m = jnp.maximum(m, sink)
    den = None
    out = None
    for s, v in zip(scores, values):
        p = jnp.exp2(s * c - m)
        d = p.sum(axis=-1, keepdims=True)
        p = p.astype(BF16)
        if isinstance(v, (list, tuple)):
            rows = p.shape[0] // len(v)
            o = jnp.concatenate([_qk(p[b * rows:(b + 1) * rows], v_b) if kv_t
                                 else jnp.dot(p[b * rows:(b + 1) * rows], v_b, preferred_element_type=F32)
                                 for b, v_b in enumerate(v)], axis=0)
        else:
            o = _qk(p, v) if kv_t else jnp.dot(p, v, preferred_element_type=F32)
        den = d if den is None else den + d
        out = o if out is None else out + o
    if sink is not None:
        den = den + jnp.exp2(sink - m)
    return out * (1.0 / den)


def _mla_heads(qa_ref, ka_refs, va_refs, o_ref, stack):
    scale = (NOPE_A + ROPE_A) ** -0.5
    rows = qa_ref.shape[0]
    for h0 in range(0, H_A, stack):
        heads = range(h0, h0 + stack)
        scores = [jnp.concatenate([_qk(qa_ref[:, h * LANES:(h + 1) * LANES], ka_ref[:, h * LANES:(h + 1) * LANES])
                                   for h in heads], axis=0) for ka_ref in ka_refs]
        values = [[va_ref[:, h * V_A:(h + 1) * V_A] for h in heads] for va_ref in va_refs]
        o = _softmax_pv(scores, values, scale)
        o = jnp.concatenate([o[b * rows:(b + 1) * rows] for b in range(stack)], axis=1)
        o_ref[:, h0 * V_A:(h0 + stack) * V_A] = o.astype(o_ref.dtype)


def _gqa_groups(q_ref, col0, n_q, ks, vs, mask=None, sink=None, kv_t=False):
    rows = q_ref.shape[0]
    n_groups = len(ks)
    n_seg = len(ks[0])
    per_group = []
    for g in range(n_groups):
        base = col0 + g * n_q * HD
        q = jnp.concatenate([q_ref[:, base + i * HD:base + (i + 1) * HD] for i in range(n_q)], axis=0)
        if kv_t:
            per_group.append([jnp.dot(q, k, preferred_element_type=F32) for k in ks[g]])
        else:
            per_group.append([_qk(q, k) for k in ks[g]])
    scores = [jnp.concatenate([per_group[g][j] for g in range(n_groups)], axis=0) for j in range(n_seg)]
    values = [[vs[g][j] for g in range(n_groups)] for j in range(n_seg)]
    if mask is not None:
        scores[0] = jnp.where(mask, scores[0], NEG_INF)
    o = _softmax_pv(scores, values, HD ** -0.5, sink, kv_t)
    return jnp.concatenate([o[i * rows:(i + 1) * rows] for i in range(n_groups * n_q)], axis=1)


def _stacked_sink(sink_ref, h0, n_q, rows):
    head = lax.broadcasted_iota(jnp.int32, (n_q * rows, 1), 0) // rows
    col = jnp.full((n_q * rows, 1), sink_ref[h0], F32)
    for g in range(1, n_q):
        col = jnp.where(head == g, sink_ref[h0 + g], col)
    return col


def _attn_ab_prompt_kernel(sink_ref, qa_ref, ka_ref, va_ref, qb_ref, kbt_ref, vbt_ref, o_ref):
    _mla_heads(qa_ref, [ka_ref], [va_ref], o_ref, MLA_STACK_PROMPT)
    kbt = kbt_ref[...].astype(BF16)
    vbt = vbt_ref[...].astype(BF16)
    sls = [slice(kh * HD, (kh + 1) * HD) for kh in range(KVH_B)]
    o = _gqa_groups(qb_ref, 0, H_B // KVH_B, [[kbt[sl, :]] for sl in sls], [[vbt[sl, :]] for sl in sls],
                    sink=_stacked_sink(sink_ref, 0, H_B, SEQ), kv_t=True)
    o_ref[:, H_A * V_A:] = o.astype(o_ref.dtype)


def _attn_ab_prompt(sink, qa, ka, va, qb, kbt, vbt):
    def blk(w):
        return pl.BlockSpec((SEQ, w), lambda b: (b, 0))
    tblk = pl.BlockSpec((None, KVH_B * HD, SEQ), lambda b: (b, 0, 0))
    return pl.pallas_call(
        _attn_ab_prompt_kernel,
        grid=(BATCH,),
        in_specs=[pl.BlockSpec(memory_space=pltpu.SMEM), blk(H_A * LANES), blk(H_A * LANES), blk(H_A * V_A),
                  blk(H_B * HD), tblk, tblk],
        out_specs=blk(D_MODEL),
        out_shape=jax.ShapeDtypeStruct((N_PROMPT, H_A * V_A + H_B * HD), BF16),
        compiler_params=_params(("arbitrary",)),
        name="attn_ab_prompt",
    )(sink, qa, ka, va, qb, kbt, vbt)


def _attn_ab_sample_kernel(sink_ref, qa_ref, ka_ref, va_ref, qb_ref, kb_ref, vb_ref,
                           ckv_ctx_ref, kpe_ctx_ref, kb_ctx_ref, vb_ctx_ref, w_kn_ref, w_v_ref, place_ref,
                           o_ref, ka_ctx_ref, va_ctx_ref):
    i = pl.program_id(1)

    @pl.when(i == 0)
    def _():
        ckv16 = ckv_ctx_ref[...].astype(BF16)
        k_nope = jnp.dot(ckv16, w_kn_ref[...], preferred_element_type=F32)
        k_rope = jnp.dot(kpe_ctx_ref[...].astype(BF16), place_ref[...], preferred_element_type=F32)
        ka_ctx_ref[...] = (k_nope + k_rope).astype(BF16)
        va_ctx_ref[...] = jnp.dot(ckv16, w_v_ref[...], preferred_element_type=F32).astype(BF16)

    _mla_heads(qa_ref, [ka_ref, ka_ctx_ref], [va_ref, va_ctx_ref], o_ref, MLA_STACK_SAMPLE)

    n_win = BQ_S + 2 * WINDOW
    start = pl.multiple_of(jnp.clip(i * BQ_S - WINDOW, 0, DEC_SEQ - n_win), WINDOW)
    row = lax.broadcasted_iota(jnp.int32, (H_B * BQ_S, n_win), 0)
    qpos = i * BQ_S + (row & (BQ_S - 1))
    kpos = start + lax.broadcasted_iota(jnp.int32, (H_B * BQ_S, n_win), 1)
    in_band = jnp.abs(qpos - kpos) <= WINDOW
    kwin = kb_ref[pl.ds(start, n_win), :].astype(BF16)
    vwin = vb_ref[pl.ds(start, n_win), :].astype(BF16)
    kctx = kb_ctx_ref[...].astype(BF16)
    vctx = vb_ctx_ref[...].astype(BF16)
    sls = [slice(kh * HD, (kh + 1) * HD) for kh in range(KVH_B)]
    o = _gqa_groups(qb_ref, 0, H_B // KVH_B, [[kwin[:, sl], kctx[:, sl]] for sl in sls],
                    [[vwin[:, sl], vctx[:, sl]] for sl in sls],
                    mask=in_band, sink=_stacked_sink(sink_ref, 0, H_B, BQ_S))
    o_ref[:, H_A * V_A:] = o.astype(o_ref.dtype)


def _attn_ab_sample(sink, qa, ka, va, qb, kb, vb, ckv_ctx, kpe_ctx, kb_ctx, vb_ctx, w_kn, w_v, place):
    qoff = N_PROMPT // BQ_S
    nq = DEC_SEQ // BQ_S
    boff = N_PROMPT // DEC_SEQ

    def qblk(w):
        return pl.BlockSpec((BQ_S, w), lambda b, i: (qoff + b * nq + i, 0))

    def bblk(w):
        return pl.BlockSpec((DEC_SEQ, w), lambda b, i: (boff + b, 0))

    def sblk(w):
        return pl.BlockSpec((DEC_SEQ, w), lambda b, i: (b, 0))

    def cblk(w):
        return pl.BlockSpec((PAST_LEN, w), lambda b, i: (b, 0))

    return pl.pallas_call(
        _attn_ab_sample_kernel,
        grid=(DEC_BATCH, nq),
        in_specs=[pl.BlockSpec(memory_space=pltpu.SMEM), qblk(H_A * LANES), bblk(H_A * LANES), bblk(H_A * V_A),
                  qblk(H_B * HD), sblk(KVH_B * HD), sblk(KVH_B * HD),
                  cblk(KV_LORA), cblk(LANES), cblk(KVH_B * HD), cblk(KVH_B * HD),
                  _const_spec((KV_LORA, H_A * LANES)), _const_spec((KV_LORA, H_A * V_A)),
                  _const_spec((LANES, H_A * LANES))],
        out_specs=pl.BlockSpec((BQ_S, D_MODEL), lambda b, i: (b * nq + i, 0)),
        out_shape=jax.ShapeDtypeStruct((N_SAMPLE, H_A * V_A + H_B * HD), BF16),
        scratch_shapes=[pltpu.VMEM((PAST_LEN, H_A * LANES), BF16), pltpu.VMEM((PAST_LEN, H_A * V_A), BF16)],
        compiler_params=_params(("arbitrary", "arbitrary")),
        name="attn_ab_sample",
    )(sink, qa, ka, va, qb, kb, vb, ckv_ctx, kpe_ctx, kb_ctx, vb_ctx, w_kn, w_v, place)


def _gqa_heads(q_ref, k_list, v_list, o_ref, stack):
    gq = H_C // KVH_C
    for kh0 in range(0, KVH_C, stack):
        sls = [slice(kh * HD, (kh + 1) * HD) for kh in range(kh0, kh0 + stack)]
        o = _gqa_groups(q_ref, kh0 * gq * HD, gq, [[k[:, sl] for k in k_list] for sl in sls],
                        [[v[:, sl] for v in v_list] for sl in sls])
        o_ref[:, kh0 * gq * HD:(kh0 + stack) * gq * HD] = o.astype(o_ref.dtype)


def _attn_c_prompt_kernel(q_ref, k_ref, v_ref, o_ref):
    _gqa_heads(q_ref, [k_ref[...].astype(BF16)], [v_ref[...].astype(BF16)], o_ref, GQA_STACK_PROMPT)


def _attn_c_prompt(q, k, v):
    def blk(w):
        return pl.BlockSpec((SEQ, w), lambda b: (b, 0))
    return pl.pallas_call(
        _attn_c_prompt_kernel,
        grid=(BATCH,),
        in_specs=[blk(H_C * HD), blk(KVH_C * HD), blk(KVH_C * HD)],
        out_specs=blk(H_C * HD),
        out_shape=jax.ShapeDtypeStruct((N_PROMPT, H_C * HD), BF16),
        compiler_params=_params(("arbitrary",)),
        name="attn_c_prompt",
    )(q, k, v)


def _attn_c_sample_kernel(q_ref, k_ref, v_ref, kc_ref, vc_ref, o_ref):
    _gqa_heads(q_ref, [k_ref[...].astype(BF16), kc_ref[...].astype(BF16)],
               [v_ref[...].astype(BF16), vc_ref[...].astype(BF16)], o_ref, GQA_STACK_SAMPLE)


def _attn_c_sample(q, k, v, k_ctx, v_ctx):
    qoff = N_PROMPT // BQ_S
    nq = DEC_SEQ // BQ_S
    w = KVH_C * HD
    return pl.pallas_call(
        _attn_c_sample_kernel,
        grid=(DEC_BATCH, nq),
        in_specs=[pl.BlockSpec((BQ_S, H_C * HD), lambda b, i: (qoff + b * nq + i, 0)),
                  pl.BlockSpec((DEC_SEQ, w), lambda b, i: (b, 0)),
                  pl.BlockSpec((DEC_SEQ, w), lambda b, i: (b, 0)),
                  pl.BlockSpec((PAST_LEN, w), lambda b, i: (b, 0)),
                  pl.BlockSpec((PAST_LEN, w), lambda b, i: (b, 0))],
        out_specs=pl.BlockSpec((BQ_S, H_C * HD), lambda b, i: (b * nq + i, 0)),
        out_shape=jax.ShapeDtypeStruct((N_SAMPLE, H_C * HD), BF16),
        compiler_params=_params(("arbitrary", "arbitrary")),
        name="attn_c_sample",
    )(q, k, v, k_ctx, v_ctx)


def _layer_norm(y, g, b):
    mu = jnp.mean(y, axis=-1, keepdims=True)
    yc = y - mu
    var = jnp.mean(yc * yc, axis=-1, keepdims=True)
    return yc * lax.rsqrt(var + EPS) * g + b


def _route(sel, aff):
    def row(a, j):
        return a[j:j + 1, :]

    scores = []
    for g in range(N_GROUPS):
        a0, a1, a2, a3 = (row(sel, 4 * g + j) for j in range(4))
        hi01, lo01 = jnp.maximum(a0, a1), jnp.minimum(a0, a1)
        hi23, lo23 = jnp.maximum(a2, a3), jnp.minimum(a2, a3)
        top1 = jnp.maximum(hi01, hi23)
        top2 = jnp.maximum(jnp.minimum(hi01, hi23), jnp.maximum(lo01, lo23))
        scores.append(top1 + top2)
    best = scores[0]
    gi = jnp.zeros(best.shape, jnp.int32)
    for g in range(1, N_GROUPS):
        better = scores[g] > best
        gi = jnp.where(better, g, gi)
        best = jnp.where(better, scores[g], best)

    def in_group(a, j):
        out = row(a, j)
        for g in range(1, N_GROUPS):
            out = jnp.where(gi == g, row(a, 4 * g + j), out)
        return out

    v = [in_group(sel, j) for j in range(4)]
    a = [in_group(aff, j) for j in range(4)]
    chosen = []
    for j in range(4):
        rank = jnp.zeros(best.shape, jnp.int32)
        for k in range(4):
            if k == j:
                continue
            ahead = (v[k] >= v[j]) if k < j else (v[k] > v[j])
            rank = rank + ahead.astype(jnp.int32)
        chosen.append(rank < 2)
    total = sum(jnp.where(chosen[j], a[j], 0.0) for j in range(4))
    w = [jnp.where(chosen[j], a[j], 0.0) / total for j in range(4)]
    pair = jnp.zeros(best.shape, jnp.int32)
    for p, (ja, jb) in enumerate(zip(SLOT_A_LOCAL, SLOT_B_LOCAL)):
        pair = jnp.where(chosen[ja] & chosen[jb], p, pair)
    gate_a = jnp.where(pair == 0, w[0], jnp.where(pair <= 2, w[2], w[3]))
    gate_b = jnp.where((pair == 0) | (pair == 1) | (pair == 4), w[1], jnp.where(pair == 5, w[2], w[0]))
    return gi * PAIRS_PER_GROUP + pair, gate_a, gate_b


def _post_attn_kernel(split_x, *refs):
    t = pl.program_id(0)
    op_ref, os_ref = refs[:2]
    if split_x:
        x = _pick(t, refs[2], refs[3])
        refs = refs[4:]
    else:
        x = refs[2][...]
        refs = refs[3:]
    (w_out_ref, gate_ref, shift_ref, scale_ref, lng_ref, lnb_ref, rw_ref, rb_ref, tri_ref,
     x1_ref, row_ref, route_ref, counts_ref, carry_ref) = refs
    o = _pick(t, op_ref, os_ref)
    subs = _sub_tiles(o.shape[0])

    a = [jnp.dot(o[s], w_out_ref[...], preferred_element_type=F32) for s in subs]
    x1 = [_layer_norm(ALPHA * x[s] + gate_ref[...] * a_s, lng_ref[...], lnb_ref[...]) for s, a_s in zip(subs, a)]
    h2 = [x1_s * (1.0 + scale_ref[...]) + shift_ref[...] for x1_s in x1]
    for s, x1_s, h2_s in zip(subs, x1, h2):
        x1_ref[s, :] = x1_s
        row_ref[s, :D_MODEL] = h2_s
    rw = rw_ref[...]
    rw_hi = rw.astype(BF16)
    rw_lo = (rw - rw_hi.astype(F32)).astype(BF16)
    rw_both = jnp.concatenate([rw_hi, rw_lo], axis=0)
    logits = []
    for h2_s in h2:
        h2_hi = h2_s.astype(BF16)
        h2_lo = (h2_s - h2_hi.astype(F32)).astype(BF16)
        by_hi = _qk(rw_both, h2_hi)
        logits.append(by_hi[:N_EXPERTS] + by_hi[N_EXPERTS:] + _qk(rw_hi, h2_lo))
    aff = [1.0 / (1.0 + jnp.exp(-l)) for l in logits]
    routed = [_route(aff_s + rb_ref[...], aff_s) for aff_s in aff]

    lane = lax.broadcasted_iota(jnp.int32, (LANES, TM), 0)
    for s, (_, gate_a, gate_b) in zip(subs, routed):
        meta_t = jnp.where(lane == 0, gate_a, jnp.where(lane == 1, gate_b, 0.0))
        row_ref[s, D_MODEL:] = meta_t.T

    @pl.when(t == 0)
    def _():
        carry_ref[...] = jnp.zeros_like(carry_ref)

    onehots = [lax.broadcasted_iota(jnp.int32, (BUCKET_ROWS, TM), 0) == bucket for bucket, _, _ in routed]
    prefixes = [jnp.dot(oh.astype(BF16), tri_ref[...], preferred_element_type=F32) for oh in onehots]
    carry = carry_ref[...]
    route_ref[...] = jnp.zeros(route_ref.shape, F32)
    for s, (bucket, _, _), onehot, prefix in zip(subs, routed, onehots, prefixes):
        rank = jnp.sum(jnp.where(onehot, prefix + carry[:, 0:1], 0.0), axis=0, keepdims=True)
        carry = carry + jnp.sum(onehot.astype(F32), axis=1, keepdims=True)
        route_ref[0:1, s] = bucket.astype(F32)
        route_ref[1:2, s] = rank
    carry_ref[...] = carry
    counts_ref[...] = carry


def _post_attn(layer, o_p, o_s, x_list, mod, w_out, ln_g, ln_b, rw_t, rb):
    split_x = len(x_list) == 2
    tok = pl.BlockSpec((TP, D_MODEL), lambda t: (t, 0))
    tri = jnp.asarray(np.arange(TM)[:, None] < np.arange(TM)[None, :], BF16)
    return pl.pallas_call(
        functools.partial(_post_attn_kernel, split_x),
        grid=(N_TOK // TP,),
        in_specs=_split_specs(TP) + (_split_specs(TP) if split_x else [tok]) + [
            _const_spec((D_MODEL, D_MODEL)), _mod_spec(layer, 2, TP), _mod_spec(layer, 3, TP),
            _mod_spec(layer, 4, TP),
            _const_spec((1, D_MODEL)), _const_spec((1, D_MODEL)), _const_spec((N_EXPERTS, D_MODEL)),
            _const_spec((N_EXPERTS, 1)), _const_spec((TM, TM))],
        out_specs=[tok, pl.BlockSpec((TP, ROW_W), lambda t: (t, 0)), pl.BlockSpec((8, TP), lambda t: (0, t)),
                   _const_spec((BUCKET_ROWS, LANES))],
        out_shape=[jax.ShapeDtypeStruct((N_TOK, D_MODEL), F32), jax.ShapeDtypeStruct((N_TOK, ROW_W), F32),
                   jax.ShapeDtypeStruct((8, N_TOK), F32), jax.ShapeDtypeStruct((BUCKET_ROWS, LANES), F32)],
        scratch_shapes=[pltpu.VMEM((BUCKET_ROWS, LANES), F32)],
        compiler_params=_params(("arbitrary",)),
        name="post_attn",
    )(o_p, o_s, *x_list, w_out, mod, mod, mod, ln_g, ln_b, rw_t, rb, tri)


def _moe_kernel(ta_ref, tb_ref, nt_ref, x_ref, wga_ref, wua_ref, wda_ref, wgb_ref, wub_ref, wdb_ref, o_ref,
                wga_s, wua_s, wda_s, wgb_s, wub_s, wdb_s):
    i = pl.program_id(0)
    prev = jnp.maximum(i - 1, 0)
    slots = ((ta_ref, (wga_ref, wua_ref, wda_ref), (wga_s, wua_s, wda_s)),
             (tb_ref, (wgb_ref, wub_ref, wdb_ref), (wgb_s, wub_s, wdb_s)))

    for t_ref, w_refs, w_scr in slots:
        @pl.when((i == 0) | (t_ref[i] != t_ref[prev]))
        def _(w_refs=w_refs, w_scr=w_scr):
            for w_ref, s_ref in zip(w_refs, w_scr):
                s_ref[...] = w_ref[...].astype(BF16)

    @pl.when(i < nt_ref[0])
    def _():
        x = x_ref[:, :D_MODEL].astype(BF16)
        acc = None
        for slot, (_, _, (wg_s, wu_s, wd_s)) in enumerate(slots):
            gt = jnp.dot(x, wg_s[...], preferred_element_type=F32)
            up = jnp.dot(x, wu_s[...], preferred_element_type=F32)
            gate = x_ref[:, D_MODEL + slot:D_MODEL + slot + 1]
            hid = (gt / (1.0 + jnp.exp(-gt))) * up * gate
            y = jnp.dot(hid.astype(BF16), wd_s[...], preferred_element_type=F32)
            acc = y if acc is None else acc + y
        o_ref[...] = acc

    @pl.when(i >= nt_ref[0])
    def _():
        o_ref[...] = jnp.zeros_like(o_ref)


def _moe(layer, tile_a, tile_b, n_tiles, rows, w_gate, w_up, w_down):
    def row_idx(i, ta, tb, nt):
        return (jnp.maximum(jnp.minimum(i, nt[0] - 1), 0), 0)

    def wspec(shape, which):
        if which == 0:
            return pl.BlockSpec((None, None) + shape, lambda i, ta, tb, nt: (layer, ta[i], 0, 0))
        return pl.BlockSpec((None, None) + shape, lambda i, ta, tb, nt: (layer, tb[i], 0, 0))

    up_shape, down_shape = (D_MODEL, D_EXPERT), (D_EXPERT, D_MODEL)
    grid_spec = pltpu.PrefetchScalarGridSpec(
        num_scalar_prefetch=3,
        grid=(MOE_TILES,),
        in_specs=[pl.BlockSpec((TME, ROW_W), row_idx),
                  wspec(up_shape, 0), wspec(up_shape, 0), wspec(down_shape, 0),
                  wspec(up_shape, 1), wspec(up_shape, 1), wspec(down_shape, 1)],
        out_specs=pl.BlockSpec((TME, D_MODEL), lambda i, ta, tb, nt: (i, 0)),
        scratch_shapes=[pltpu.VMEM(s, BF16) for s in (up_shape, up_shape, down_shape) * 2],
    )
    return pl.pallas_call(
        _moe_kernel,
        grid_spec=grid_spec,
        out_shape=jax.ShapeDtypeStruct((MOE_ROWS, D_MODEL), F32),
        compiler_params=_params(("arbitrary",)),
        name="moe_experts",
    )(tile_a, tile_b, n_tiles, rows, w_gate, w_up, w_down, w_gate, w_up, w_down)


def _row_copy(src_ref, src_row, dst_ref, dst_row, sem):
    return pltpu.make_async_copy(src_ref.at[pl.ds(src_row, 1), :], dst_ref.at[pl.ds(dst_row, 1), :], sem)


def _dispatch_kernel(pos_ref, pend_ref, cnt_ref, nt_ref, src_ref, out_ref, buf, zero_ref, in_sem, sem, zsem):
    t = pl.program_id(0)

    def fetch(tile):
        return pltpu.make_async_copy(src_ref.at[pl.ds(pl.multiple_of(tile * TM, TM), TM), :],
                                     buf.at[tile % STAGE_SLOTS], in_sem.at[tile % STAGE_SLOTS])

    @pl.when(t == 0)
    def _():
        fetch(0).start()
        zero_ref[...] = jnp.zeros_like(zero_ref)

        def zero_tile(row0):
            return pltpu.make_async_copy(zero_ref, out_ref.at[pl.ds(pl.multiple_of(row0, TME), TME), :], zsem)

        for b in range(N_BUCKETS):
            @pl.when(cnt_ref[b] > 0)
            def _(b=b):
                zero_tile(pend_ref[b] - TME).start()

        def start_unused(i, carry):
            zero_tile(i * TME).start()
            return carry
        lax.fori_loop(nt_ref[0], MOE_TILES, start_unused, 0)

        for b in range(N_BUCKETS):
            @pl.when(cnt_ref[b] > 0)
            def _(b=b):
                zero_tile(pend_ref[b] - TME).wait()

        def wait_unused(i, carry):
            zero_tile(i * TME).wait()
            return carry
        lax.fori_loop(nt_ref[0], MOE_TILES, wait_unused, 0)

    @pl.when(t + 1 < N_TILES)
    def _():
        fetch(t + 1).start()

    fetch(t).wait()
    tile_ref = buf.at[t % STAGE_SLOTS]

    for r in range(TM):
        _row_copy(tile_ref, r, out_ref, pos_ref[t * TM + r], sem.at[t % 2]).start(priority=r % 2)

    def drain(tile):
        pltpu.make_async_copy(buf.at[0], out_ref.at[pl.ds(0, TM), :], sem.at[tile % 2]).wait()

    @pl.when(t > 0)
    def _():
        drain(t - 1)

    @pl.when(t == N_TILES - 1)
    def _():
        drain(t)


def _dispatch(pos, pend, counts, n_tiles, rows):
    return pl.pallas_call(
        _dispatch_kernel,
        grid_spec=pltpu.PrefetchScalarGridSpec(
            num_scalar_prefetch=4, grid=(N_TILES,),
            in_specs=[pl.BlockSpec(memory_space=pl.ANY)],
            out_specs=pl.BlockSpec(memory_space=pl.ANY),
            scratch_shapes=[pltpu.VMEM((STAGE_SLOTS, TM, ROW_W), F32), pltpu.VMEM((TME, ROW_W), F32),
                            pltpu.SemaphoreType.DMA((STAGE_SLOTS,)), pltpu.SemaphoreType.DMA((2,)),
                            pltpu.SemaphoreType.DMA]),
        out_shape=jax.ShapeDtypeStruct((MOE_ROWS, ROW_W), F32),
        compiler_params=_params(("arbitrary",)),
        name="dispatch_rows",
    )(pos, pend, counts, n_tiles, rows)


def _plan(route, counts):
    bucket = route[0].astype(jnp.int32)
    rank = route[1].astype(jnp.int32)
    counts = counts[:N_BUCKETS, 0].astype(jnp.int32)
    padded = ((counts + TME - 1) // TME) * TME
    pend = jnp.cumsum(padded)
    pstart = pend - padded
    ids = jnp.arange(N_BUCKETS, dtype=jnp.int32)
    pos = rank + jnp.sum(jnp.where(bucket[None, :] == ids[:, None], pstart[:, None], 0), axis=0)
    n_tiles = pend[-1] // TME
    tile_start = jnp.minimum(jnp.arange(MOE_TILES, dtype=jnp.int32), n_tiles - 1) * TME
    tile_bucket = jnp.minimum(jnp.sum(tile_start[:, None] >= pend[None, :], axis=1), N_BUCKETS - 1)
    group, pair = tile_bucket // PAIRS_PER_GROUP, tile_bucket % PAIRS_PER_GROUP
    slot_a = jnp.asarray(SLOT_A_LOCAL, jnp.int32)
    slot_b = jnp.asarray(SLOT_B_LOCAL, jnp.int32)
    pair_hot = pair[:, None] == jnp.arange(PAIRS_PER_GROUP, dtype=jnp.int32)[None, :]
    tile_a = group * EXPERTS_PER_GROUP + jnp.sum(jnp.where(pair_hot, slot_a[None, :], 0), axis=1)
    tile_b = group * EXPERTS_PER_GROUP + jnp.sum(jnp.where(pair_hot, slot_b[None, :], 0), axis=1)
    return pos, pend, counts, tile_a.astype(jnp.int32), tile_b.astype(jnp.int32), n_tiles.reshape(1)


def _post_moe_kernel(split_out, pos_ref, f_ref, x1_ref, gate_ref, lng_ref, lnb_ref, *refs):
    out_refs, (fbuf, sem) = refs[:-2], refs[-2:]
    t = pl.program_id(0)
    tq = x1_ref.shape[0]
    n_steps = N_TOK // tq
    p_steps = N_PROMPT // tq

    def wait_tile(slot):
        pltpu.make_async_copy(f_ref.at[pl.ds(0, tq), :], fbuf.at[slot], sem.at[slot]).wait()

    @pl.when(t == 0)
    def _():
        def issue(r, carry):
            _row_copy(f_ref, pos_ref[r], fbuf.at[r // tq], r % tq, sem.at[r // tq]).start()
            return carry
        lax.fori_loop(0, (GATHER_SLOTS - 1) * tq, issue, 0, unroll=8)

    slot = t % GATHER_SLOTS
    wait_tile(slot)
    ahead = jnp.minimum(t + GATHER_SLOTS - 1, n_steps - 1)
    aslot = (t + GATHER_SLOTS - 1) % GATHER_SLOTS
    for r in range(tq):
        _row_copy(f_ref, pos_ref[ahead * tq + r], fbuf.at[aslot], r, sem.at[aslot]).start(priority=r % 2)
    subs = _sub_tiles(tq)
    y = [_layer_norm(ALPHA * x1_ref[s, :] + gate_ref[...] * fbuf[slot, s, :], lng_ref[...], lnb_ref[...])
         for s in subs]

    @pl.when(t == n_steps - 1)
    def _():
        for k in range(1, GATHER_SLOTS):
            wait_tile((t + k) % GATHER_SLOTS)
    if split_out:
        @pl.when(t < p_steps)
        def _():
            for s, y_s in zip(subs, y):
                out_refs[0][s, :] = y_s

        @pl.when(t >= p_steps)
        def _():
            for s, y_s in zip(subs, y):
                out_refs[1][s, :] = y_s
    else:
        for s, y_s in zip(subs, y):
            out_refs[0][s, :] = y_s


def _post_moe(layer, pos, f_sorted, x1, mod, ln_g, ln_b, split_out):
    tq = TQ
    p_steps = N_PROMPT // tq
    tok = pl.BlockSpec((tq, D_MODEL), lambda t, p: (t, 0))
    if split_out:
        out_specs = [pl.BlockSpec((tq, D_MODEL), lambda t, p: (jnp.minimum(t, p_steps - 1), 0)),
                     pl.BlockSpec((tq, D_MODEL), lambda t, p: (jnp.maximum(t - p_steps, 0), 0))]
        out_shape = [jax.ShapeDtypeStruct((N_PROMPT, D_MODEL), F32), jax.ShapeDtypeStruct((N_SAMPLE, D_MODEL), F32)]
    else:
        out_specs = [tok]
        out_shape = [jax.ShapeDtypeStruct((N_TOK, D_MODEL), F32)]
    mod_spec = pl.BlockSpec((None, None, None, 1, D_MODEL), lambda t, p: (layer, _mod_row(t, tq), 5, 0, 0))
    return pl.pallas_call(
        functools.partial(_post_moe_kernel, split_out),
        grid_spec=pltpu.PrefetchScalarGridSpec(
            num_scalar_prefetch=1, grid=(N_TOK // tq,),
            in_specs=[pl.BlockSpec(memory_space=pl.ANY), tok, mod_spec,
                      pl.BlockSpec((1, D_MODEL), lambda t, p: (0, 0)), pl.BlockSpec((1, D_MODEL), lambda t, p: (0, 0))],
            out_specs=out_specs,
            scratch_shapes=[pltpu.VMEM((GATHER_SLOTS, tq, D_MODEL), F32),
                            pltpu.SemaphoreType.DMA((GATHER_SLOTS,))]),
        out_shape=out_shape,
        compiler_params=_params(("arbitrary",)),
        name="post_moe",
    )(pos, f_sorted, x1, mod, ln_g, ln_b)


def _ffn(layer, o_p, o_s, x_list, mod, w_out, ln_g, ln_b, rw_t, rb, w_gate, w_up, w_down, split_out):
    x1, rows, route, counts = _post_attn(layer, o_p, o_s, x_list, mod, w_out, ln_g[layer, 0][None],
                                         ln_b[layer, 0][None], rw_t, rb)
    pos, pend, counts, tile_a, tile_b, n_tiles = _plan(route, counts)
    rows_sorted = _dispatch(pos, pend, counts, n_tiles, rows)
    f_sorted = _moe(layer, tile_a, tile_b, n_tiles, rows_sorted, w_gate, w_up, w_down)
    return _post_moe(layer, pos, f_sorted, x1, mod, ln_g[layer, 1][None], ln_b[layer, 1][None], split_out)


def kernel(x_prompt, x_sample, c, cache_mla_ckv, cache_mla_kpe, cache_swa_k, cache_swa_v, cache_gqa_k, cache_gqa_v, c_ctx, w_mod, b_mod, ln_g, ln_b, w_in_ab, mla_q_norm, mla_w_uq, mla_kv_norm, mla_w_ukv, swa_sink, w_out_ab, w_in_c, gqa_q_norm, gqa_k_norm, w_out_c, router_w, router_bias, exp_w_gate, exp_w_up, exp_w_down):
    xp = x_prompt.reshape(N_PROMPT, D_MODEL)
    xs = x_sample.reshape(N_SAMPLE, D_MODEL)
    cond = jnp.concatenate([c_ctx[None], c, jnp.zeros((MOD_ROWS - 1 - DEC_BATCH, D_MODEL), F32)], axis=0)
    mod = _modulation(cond, w_mod, b_mod).reshape(DEPTH, MOD_ROWS, 6, 1, D_MODEL)
    tabs = _rope_tables()
    rw_t = router_w.T
    rb = router_bias.reshape(N_EXPERTS, 1)

    w = w_in_ab[0]
    pad = jnp.zeros((D_MODEL, AB_COLS - w.shape[1]), F32)
    w_in = jnp.concatenate([w[:, :384], w[:, 416:], w[:, 384:416], pad], axis=1).astype(BF16)
    w_uq, w_kn, w_v, place = _mla_weights(mla_w_uq[0], mla_w_ukv[0])
    qa, ka, va, qb, ckv_p, kpe_p, kbt_p, vbt_p, kb_s, vb_s = _inproj_ab(
        xp, xs, mod, w_in, mla_q_norm[0][None], mla_kv_norm[0][None], w_uq, w_kn, w_v, place, tabs)
    sink = swa_sink[0]
    o_p = _attn_ab_prompt(sink, qa, ka, va, qb, kbt_p, vbt_p)
    kpe_ctx = jnp.pad(cache_mla_kpe[:, 0].reshape(-1, ROPE_A), ((0, 0), (0, LANES - ROPE_A)))
    o_s = _attn_ab_sample(sink, qa, ka, va, qb, kb_s, vb_s,
                          cache_mla_ckv[:, 0].reshape(-1, KV_LORA), kpe_ctx,
                          cache_swa_k[:, 0].reshape(-1, KVH_B * HD), cache_swa_v[:, 0].reshape(-1, KVH_B * HD),
                          w_kn, w_v, place)
    (x2,) = _ffn(0, o_p, o_s, [xp, xs], mod, w_out_ab[0].astype(BF16), ln_g, ln_b, rw_t, rb,
                 exp_w_gate, exp_w_up, exp_w_down, split_out=False)

    g_full = jnp.concatenate([jnp.tile(gqa_q_norm[0], H_C), jnp.tile(gqa_k_norm[0], KVH_C)])[None]
    qc, kc_p, vc_p, kc_s, vc_s = _inproj_c(x2, mod, w_in_c[0].astype(BF16), g_full, tabs)
    oc_p = _attn_c_prompt(qc, kc_p, vc_p)
    oc_s = _attn_c_sample(qc, kc_s, vc_s, cache_gqa_k[:, 0].reshape(-1, KVH_C * HD),
                          cache_gqa_v[:, 0].reshape(-1, KVH_C * HD))
    y_p, y_s = _ffn(1, oc_p, oc_s, [x2], mod, w_out_c[0].astype(BF16), ln_g, ln_b, rw_t, rb,
                    exp_w_gate, exp_w_up, exp_w_down, split_out=True)

    y_prompt = y_p.reshape(BATCH, SEQ, D_MODEL)
    y_sample = y_s.reshape(DEC_BATCH, DEC_SEQ, D_MODEL)
    new_ckv = ckv_p.reshape(BATCH, 1, SEQ, KV_LORA)
    new_kpe = kpe_p.reshape(BATCH, 1, SEQ, ROPE_A)
    new_swk = kbt_p.reshape(BATCH, 1, KVH_B, HD, SEQ).transpose(0, 1, 4, 2, 3)
    new_swv = vbt_p.reshape(BATCH, 1, KVH_B, HD, SEQ).transpose(0, 1, 4, 2, 3)
    new_gk = kc_p.reshape(BATCH, 1, SEQ, KVH_C, HD)
    new_gv = vc_p.reshape(BATCH, 1, SEQ, KVH_C, HD)
    return y_prompt, y_sample, new_ckv, new_kpe, new_swk, new_swv, new_gk, new_gv
```

```python
import functools

import numpy as np
import jax
import jax.numpy as jnp
from jax import lax
from jax.experimental import pallas as pl
from jax.experimental.pallas import tpu as pltpu

D_MODEL = 1024
BATCH = 32
SEQ = 256
DEPTH = 2
DEC_BATCH = 4
DEC_SEQ = 1024
PAST_LEN = 256
GRID_W = 64
ROPE_THETA = 10000.0
WINDOW = 128
HD = 64
H_A = 8
NOPE_A = 64
ROPE_A = 32
V_A = 64
Q_LORA = 256
KV_LORA = 128
H_B = 8
KVH_B = 2
H_C = 16
KVH_C = 4
N_EXPERTS = 16
N_GROUPS = 4
EXPERTS_PER_GROUP = 4
D_EXPERT = 512
ALPHA = (2 * DEPTH) ** 0.25
NEG_INF = -1e30
EPS = 1e-6

LANES = 128
N_PROMPT = BATCH * SEQ
N_SAMPLE = DEC_BATCH * DEC_SEQ
N_TOK = N_PROMPT + N_SAMPLE
TM = 256
TP = 1024
TP_AB = 512
TQ = 512
N_TILES = N_TOK // TM
BQ_S = 256
LOG2E = 1.4426950408889634
MLA_STACK_PROMPT = 4
MLA_STACK_SAMPLE = 4
GQA_STACK_PROMPT = 4
GQA_STACK_SAMPLE = 2
MOD_ROWS = 8

PAIRS_PER_GROUP = 6
N_BUCKETS = N_GROUPS * PAIRS_PER_GROUP
SLOT_A_LOCAL = (0, 2, 2, 3, 3, 3)
SLOT_B_LOCAL = (1, 1, 0, 0, 1, 2)
TME = 256
MOE_TILES = -(-(N_TOK + N_BUCKETS * (TME - 1)) // TME)
MOE_ROWS = MOE_TILES * TME

BUCKET_ROWS = 32
ROW_W = D_MODEL + LANES
STAGE_SLOTS = 3
GATHER_SLOTS = 3

COL_QB = Q_LORA + KV_LORA
COL_KB = COL_QB + H_B * HD
COL_VB = COL_KB + KVH_B * HD
COL_KPE = COL_VB + KVH_B * HD
AB_COLS = COL_KPE + LANES
C_NORM_COLS = H_C * HD + KVH_C * HD
NORM_BLOCK = 256

BF16 = jnp.bfloat16
F32 = jnp.float32
VMEM_LIMIT = 52 * 1024 * 1024


def _mod_row(t, tile=TM):
    p_tiles = N_PROMPT // tile
    return jnp.where(t < p_tiles, 0, 1 + (t - p_tiles) // (DEC_SEQ // tile))


def _mod_spec(layer, chunk, tile=TM):
    return pl.BlockSpec((None, None, None, 1, D_MODEL), lambda t: (layer, _mod_row(t, tile), chunk, 0, 0))


def _const_spec(shape):
    nd = len(shape)
    return pl.BlockSpec(shape, lambda *_: (0,) * nd)


def _split_specs(tile=TM):
    p_tiles = N_PROMPT // tile
    return [
        pl.BlockSpec((tile, D_MODEL), lambda t: (jnp.minimum(t, p_tiles - 1), 0)),
        pl.BlockSpec((tile, D_MODEL), lambda t: (jnp.maximum(t - p_tiles, 0), 0)),
    ]


def _group_spec(tile, width, prompt):
    p_tiles = N_PROMPT // tile
    if prompt:
        return pl.BlockSpec((tile, width), lambda t: (jnp.minimum(t, p_tiles - 1), 0))
    return pl.BlockSpec((tile, width), lambda t: (jnp.maximum(t - p_tiles, 0), 0))


def _pick(t, a_ref, b_ref):
    return jnp.where(t < N_PROMPT // a_ref.shape[0], a_ref[...], b_ref[...])


def _sub_tiles(rows):
    return [slice(i * TM, (i + 1) * TM) for i in range(rows // TM)]


def _params(sem):
    return pltpu.CompilerParams(dimension_semantics=sem, vmem_limit_bytes=VMEM_LIMIT)


def _mod_kernel(cond_ref, w_ref, b_ref, o_ref):
    c = cond_ref[...]
    s = (c / (1.0 + jnp.exp(-c))).astype(BF16)
    o_ref[...] = jnp.dot(s, w_ref[...].astype(BF16), preferred_element_type=F32) + b_ref[...]


def _modulation(cond, w_mod, b_mod):
    tn = 1536
    return pl.pallas_call(
        _mod_kernel,
        grid=(DEPTH, 6 * D_MODEL // tn),
        in_specs=[
            pl.BlockSpec((MOD_ROWS, D_MODEL), lambda l, j: (0, 0)),
            pl.BlockSpec((None, D_MODEL, tn), lambda l, j: (l, 0, j)),
            pl.BlockSpec((None, 1, tn), lambda l, j: (l, 0, j)),
        ],
        out_specs=pl.BlockSpec((None, MOD_ROWS, tn), lambda l, j: (l, 0, j)),
        out_shape=jax.ShapeDtypeStruct((DEPTH, MOD_ROWS, 6 * D_MODEL), F32),
        compiler_params=_params(("arbitrary", "arbitrary")),
        name="modulation",
    )(cond, w_mod, b_mod.reshape(DEPTH, 1, 6 * D_MODEL))


def _grid_cos_sin(half):
    n_freq = half // 2
    rows = DEC_SEQ // GRID_W
    row = jnp.repeat(jnp.arange(rows, dtype=F32), GRID_W)
    col = jnp.tile(jnp.arange(GRID_W, dtype=F32), rows)
    inv = 1.0 / (ROPE_THETA ** (jnp.arange(n_freq, dtype=F32) / n_freq))
    ang = jnp.concatenate([row[:, None] * inv, col[:, None] * inv], axis=-1)
    return jnp.cos(ang), jnp.sin(ang)


def _rope_tables():
    tabs = []
    for half in (HD // 2, ROPE_A // 2):
        cos, sin = _grid_cos_sin(half)
        reps = LANES // (2 * half)
        tabs.append(jnp.tile(jnp.concatenate([cos, cos], axis=-1), (1, reps)))
        tabs.append(jnp.tile(jnp.concatenate([-sin, sin], axis=-1), (1, reps)))
    half = ROPE_A // 2
    one = jnp.ones((DEC_SEQ, NOPE_A), F32)
    zero = jnp.zeros((DEC_SEQ, NOPE_A), F32)
    pad1 = jnp.ones((DEC_SEQ, LANES - NOPE_A - ROPE_A), F32)
    pad0 = jnp.zeros((DEC_SEQ, LANES - NOPE_A - ROPE_A), F32)
    zh = jnp.zeros((DEC_SEQ, half), F32)
    tabs.append(jnp.concatenate([one, cos, cos, pad1], axis=-1))
    tabs.append(jnp.concatenate([zero, -sin, zh, pad0], axis=-1))
    tabs.append(jnp.concatenate([zero, zh, sin, pad0], axis=-1))
    return tuple(tabs)


def _rope(x, c, s, half):
    w = x.shape[1]
    reps = w // LANES
    if reps > 1:
        c = jnp.concatenate([c] * reps, axis=1)
        s = jnp.concatenate([s] * reps, axis=1)
    ahead = pltpu.roll(x, w - half, 1)
    behind = pltpu.roll(x, half, 1)
    lane = lax.broadcasted_iota(jnp.int32, x.shape, 1)
    swapped = jnp.where((lane & (2 * half - 1)) < half, ahead, behind)
    return x * c + swapped * s


def _rope_specs(n, tile):
    p_tiles = N_PROMPT // tile

    def idx(t):
        return (jnp.maximum(t - p_tiles, 0) % (DEC_SEQ // tile), 0)
    return [pl.BlockSpec((tile, LANES), idx)] * n


def _rms(x, g_ref):
    return x * lax.rsqrt(jnp.mean(x * x, axis=-1, keepdims=True) + EPS) * g_ref[...]


def _inproj_ab_kernel(xp_ref, xs_ref, shift_ref, scale_ref, w_in_ref, qn_ref, kvn_ref, w_uq_ref, w_kn_ref, w_v_ref,
                      place_ref, c64_ref, s64_ref, c32_ref, s32_ref, ca_ref, sa_ref, sb_ref,
                      qa_ref, ka_ref, va_ref, qb_ref, ckv_p_ref, kpe_p_ref, kbt_p_ref, vbt_p_ref, kb_s_ref, vb_s_ref):
    t = pl.program_id(0)
    x = _pick(t, xp_ref, xs_ref)
    subs = _sub_tiles(x.shape[0])
    h = [(x[s] * (1.0 + scale_ref[...]) + shift_ref[...]).astype(BF16) for s in subs]
    z = [jnp.dot(h_s, w_in_ref[...], preferred_element_type=F32) for h_s in h]
    ql = [_rms(z_s[:, :Q_LORA], qn_ref).astype(BF16) for z_s in z]
    qa = [jnp.dot(ql_s, w_uq_ref[...], preferred_element_type=F32) for ql_s in ql]
    ckv = [_rms(z_s[:, Q_LORA:Q_LORA + KV_LORA], kvn_ref) for z_s in z]
    k_nope = [jnp.dot(c_s.astype(BF16), w_kn_ref[...], preferred_element_type=F32) for c_s in ckv]
    va = [jnp.dot(c_s.astype(BF16), w_v_ref[...], preferred_element_type=F32).astype(BF16) for c_s in ckv]
    for s, va_s in zip(subs, va):
        va_ref[s, :] = va_s
    qb = [z_s[:, COL_QB:COL_KB] for z_s in z]
    kb = [z_s[:, COL_KB:COL_VB] for z_s in z]
    vb = [z_s[:, COL_VB:COL_KPE] for z_s in z]
    kpe = [z_s[:, COL_KPE:] for z_s in z]

    def finish(qa_out, qb_out, kpe_out):
        k_rope = [jnp.dot(kpe_s.astype(BF16), place_ref[...], preferred_element_type=F32) for kpe_s in kpe_out]
        for i, s in enumerate(subs):
            qa_ref[s, :] = qa_out[i].astype(BF16)
            qb_ref[s, :] = qb_out[i].astype(BF16)
            ka_ref[s, :] = (k_nope[i] + k_rope[i]).astype(BF16)

    p_tiles = N_PROMPT // x.shape[0]

    @pl.when(t < p_tiles)
    def _():
        finish(qa, qb, kpe)
        for i, s in enumerate(subs):
            ckv_p_ref[s, :] = ckv[i]
            kpe_p_ref[s, :] = kpe[i][:, :ROPE_A]
            kbt_p_ref[i] = kb[i].T
            vbt_p_ref[i] = vb[i].T

    @pl.when(t >= p_tiles)
    def _():
        w = qa[0].shape[1]
        reps = w // LANES
        half = ROPE_A // 2
        qa_rot, qb_rot, kb_rot, kpe_rot = [], [], [], []
        for i, s in enumerate(subs):
            ca = jnp.concatenate([ca_ref[s, :]] * reps, axis=1)
            sa = jnp.concatenate([sa_ref[s, :]] * reps, axis=1)
            sb = jnp.concatenate([sb_ref[s, :]] * reps, axis=1)
            qa_rot.append(qa[i] * ca + pltpu.roll(qa[i], w - half, 1) * sa + pltpu.roll(qa[i], half, 1) * sb)
            qb_rot.append(_rope(qb[i], c64_ref[s, :], s64_ref[s, :], HD // 2))
            kb_rot.append(_rope(kb[i], c64_ref[s, :], s64_ref[s, :], HD // 2))
            kpe_rot.append(_rope(kpe[i], c32_ref[s, :], s32_ref[s, :], ROPE_A // 2))
        finish(qa_rot, qb_rot, kpe_rot)
        for i, s in enumerate(subs):
            kb_s_ref[s, :] = kb_rot[i]
            vb_s_ref[s, :] = vb[i]


def _mla_weights(w_uq, w_ukv):
    pad = LANES - NOPE_A - ROPE_A
    uq = w_uq.reshape(Q_LORA, H_A, NOPE_A + ROPE_A)
    uq = jnp.pad(uq, ((0, 0), (0, 0), (0, pad))).reshape(Q_LORA, H_A * LANES)
    ukv = w_ukv.reshape(KV_LORA, H_A, NOPE_A + V_A)
    kn = jnp.pad(ukv[:, :, :NOPE_A], ((0, 0), (0, 0), (0, LANES - NOPE_A))).reshape(KV_LORA, H_A * LANES)
    wv = ukv[:, :, NOPE_A:].reshape(KV_LORA, H_A * V_A)
    src = np.arange(LANES)[:, None]
    dst = np.arange(H_A * LANES)[None, :] % LANES
    place = (src < ROPE_A) & (dst == src + NOPE_A)
    return uq.astype(BF16), kn.astype(BF16), wv.astype(BF16), jnp.asarray(place, BF16)


def _inproj_ab(xp, xs, mod, w_in, q_norm, kv_norm, w_uq, w_kn, w_v, place, tabs):
    widths = [(H_A * LANES, BF16), (H_A * LANES, BF16), (H_A * V_A, BF16), (H_B * HD, BF16)]
    tp = TP_AB
    out_specs = [pl.BlockSpec((tp, w), lambda t: (t, 0)) for w, _ in widths]
    out_shape = [jax.ShapeDtypeStruct((N_TOK, w), dt) for w, dt in widths]
    for w in (KV_LORA, ROPE_A):
        out_specs.append(_group_spec(tp, w, prompt=True))
        out_shape.append(jax.ShapeDtypeStruct((N_PROMPT, w), F32))
    assert SEQ == TM
    p_tiles = N_PROMPT // tp
    for _ in range(2):
        out_specs.append(pl.BlockSpec((tp // SEQ, KVH_B * HD, SEQ), lambda t: (jnp.minimum(t, p_tiles - 1), 0, 0)))
        out_shape.append(jax.ShapeDtypeStruct((BATCH, KVH_B * HD, SEQ), F32))
    for w in (KVH_B * HD, KVH_B * HD):
        out_specs.append(_group_spec(tp, w, prompt=False))
        out_shape.append(jax.ShapeDtypeStruct((N_SAMPLE, w), F32))
    return pl.pallas_call(
        _inproj_ab_kernel,
        grid=(N_TOK // tp,),
        in_specs=_split_specs(tp) + [_mod_spec(0, 0, tp), _mod_spec(0, 1, tp), _const_spec((D_MODEL, AB_COLS)),
                                     _const_spec((1, Q_LORA)), _const_spec((1, KV_LORA)),
                                     _const_spec((Q_LORA, H_A * LANES)), _const_spec((KV_LORA, H_A * LANES)),
                                     _const_spec((KV_LORA, H_A * V_A)), _const_spec((LANES, H_A * LANES))]
        + _rope_specs(7, tp),
        out_specs=out_specs,
        out_shape=out_shape,
        compiler_params=_params(("arbitrary",)),
        name="inproj_ab",
    )(xp, xs, mod, mod, w_in, q_norm, kv_norm, w_uq, w_kn, w_v, place, *tabs)


def _inproj_c_kernel(x_ref, shift_ref, scale_ref, w_in_ref, g_ref, ones_ref, c64_ref, s64_ref,
                     q_ref, k_p_ref, v_p_ref, k_s_ref, v_s_ref):
    t = pl.program_id(0)
    subs = _sub_tiles(x_ref.shape[0])
    h = [(x_ref[s, :] * (1.0 + scale_ref[...]) + shift_ref[...]).astype(BF16) for s in subs]
    z = [jnp.dot(h_s, w_in_ref[...], preferred_element_type=F32) for h_s in h]
    qk = [z_s[:, :C_NORM_COLS] for z_s in z]
    sq = [qk_s * qk_s for qk_s in qk]
    sq_hi = [sq_s.astype(BF16) for sq_s in sq]
    sq_lo = [(sq_s - hi_s.astype(F32)).astype(BF16) for sq_s, hi_s in zip(sq, sq_hi)]
    ss = []
    for hi_s, lo_s in zip(sq_hi, sq_lo):
        blocks = []
        for j in range(C_NORM_COLS // NORM_BLOCK):
            sl = slice(j * NORM_BLOCK, (j + 1) * NORM_BLOCK)
            blocks.append(jnp.dot(hi_s[:, sl], ones_ref[...], preferred_element_type=F32)
                          + jnp.dot(lo_s[:, sl], ones_ref[...], preferred_element_type=F32))
        ss.append(jnp.concatenate(blocks, axis=1))
    qk = [qk_s * lax.rsqrt(ss_s * (1.0 / HD) + EPS) * g_ref[...] for qk_s, ss_s in zip(qk, ss)]
    p_tiles = N_PROMPT // x_ref.shape[0]

    @pl.when(t < p_tiles)
    def _():
        for s, qk_s, z_s in zip(subs, qk, z):
            q_ref[s, :] = qk_s[:, :H_C * HD].astype(BF16)
            k_p_ref[s, :] = qk_s[:, H_C * HD:]
            v_p_ref[s, :] = z_s[:, C_NORM_COLS:]

    @pl.when(t >= p_tiles)
    def _():
        for s, qk_s, z_s in zip(subs, qk, z):
            q_ref[s, :] = _rope(qk_s[:, :H_C * HD], c64_ref[s, :], s64_ref[s, :], HD // 2).astype(BF16)
            k_s_ref[s, :] = _rope(qk_s[:, H_C * HD:], c64_ref[s, :], s64_ref[s, :], HD // 2)
            v_s_ref[s, :] = z_s[:, C_NORM_COLS:]


def _inproj_c(x, mod, w_in, g_full, tabs):
    def tok(w):
        return pl.BlockSpec((TP, w), lambda t: (t, 0))
    kv_w = KVH_C * HD
    head = np.arange(NORM_BLOCK) // HD
    ones = jnp.asarray(head[:, None] == head[None, :], BF16)
    return pl.pallas_call(
        _inproj_c_kernel,
        grid=(N_TOK // TP,),
        in_specs=[tok(D_MODEL), _mod_spec(1, 0, TP), _mod_spec(1, 1, TP),
                  _const_spec((D_MODEL, C_NORM_COLS + KVH_C * HD)),
                  _const_spec((1, C_NORM_COLS)), _const_spec((NORM_BLOCK, NORM_BLOCK))] + _rope_specs(2, TP),
        out_specs=[tok(H_C * HD), _group_spec(TP, kv_w, True), _group_spec(TP, kv_w, True),
                   _group_spec(TP, kv_w, False), _group_spec(TP, kv_w, False)],
        out_shape=[jax.ShapeDtypeStruct((N_TOK, H_C * HD), BF16),
                   jax.ShapeDtypeStruct((N_PROMPT, kv_w), F32), jax.ShapeDtypeStruct((N_PROMPT, kv_w), F32),
                   jax.ShapeDtypeStruct((N_SAMPLE, kv_w), F32), jax.ShapeDtypeStruct((N_SAMPLE, kv_w), F32)],
        compiler_params=_params(("arbitrary",)),
        name="inproj_c",
    )(x, mod, mod, w_in, g_full, ones, *tabs[:2])


def _qk(q, k):
    return lax.dot_general(q, k, (((1,), (1,)), ((), ())), preferred_element_type=F32)


def _softmax_pv(scores, values, scale, sink=None, kv_t=False):
    c = scale * LOG2E
    m = scores[0].max(axis=-1, keepdims=True)
    for s in scores[1:]:
        m = jnp.maximum(m, s.max(axis=-1, keepdims=True))
    m = m * c
    if sink is not None:
        sink = sink * LOG2E
        m = jnp.maximum(m, sink)
    den = None
    out = None
    for s, v in zip(scores, values):
        p = jnp.exp2(s * c - m)
        d = p.sum(axis=-1, keepdims=True)
        p = p.astype(BF16)
        if isinstance(v, (list, tuple)):
            rows = p.shape[0] // len(v)
            o = jnp.concatenate([_qk(p[b * rows:(b + 1) * rows], v_b) if kv_t
                                 else jnp.dot(p[b * rows:(b + 1) * rows], v_b, preferred_element_type=F32)
                                 for b, v_b in enumerate(v)], axis=0)
        else:
            o = _qk(p, v) if kv_t else jnp.dot(p, v, preferred_element_type=F32)
        den = d if den is None else den + d
        out = o if out is None else out + o
    if sink is not None:
        den = den + jnp.exp2(sink - m)
    return out * (1.0 / den)


def _mla_heads(qa_ref, ka_refs, va_refs, o_ref, stack):
    scale = (NOPE_A + ROPE_A) ** -0.5
    rows = qa_ref.shape[0]
    for h0 in range(0, H_A, stack):
        heads = range(h0, h0 + stack)
        scores = [jnp.concatenate([_qk(qa_ref[:, h * LANES:(h + 1) * LANES], ka_ref[:, h * LANES:(h + 1) * LANES])
                                   for h in heads], axis=0) for ka_ref in ka_refs]
        values = [[va_ref[:, h * V_A:(h + 1) * V_A] for h in heads] for va_ref in va_refs]
        o = _softmax_pv(scores, values, scale)
        o = jnp.concatenate([o[b * rows:(b + 1) * rows] for b in range(stack)], axis=1)
        o_ref[:, h0 * V_A:(h0 + stack) * V_A] = o.astype(o_ref.dtype)


def _gqa_groups(q_ref, col0, n_q, ks, vs, mask=None, sink=None, kv_t=False):
    rows = q_ref.shape[0]
    n_groups = len(ks)
    n_seg = len(ks[0])
    per_group = []
    for g in range(n_groups):
        base = col0 + g * n_q * HD
        q = jnp.concatenate([q_ref[:, base + i * HD:base + (i + 1) * HD] for i in range(n_q)], axis=0)
        if kv_t:
            per_group.append([jnp.dot(q, k, preferred_element_type=F32) for k in ks[g]])
        else:
            per_group.append([_qk(q, k) for k in ks[g]])
    scores = [jnp.concatenate([per_group[g][j] for g in range(n_groups)], axis=0) for j in range(n_seg)]
    values = [[vs[g][j] for g in range(n_groups)] for j in range(n_seg)]
    if mask is not None:
        scores[0] = jnp.where(mask, scores[0], NEG_INF)
    o = _softmax_pv(scores, values, HD ** -0.5, sink, kv_t)
    return jnp.concatenate([o[i * rows:(i + 1) * rows] for i in range(n_groups * n_q)], axis=1)


def _stacked_sink(sink_ref, h0, n_q, rows):
    head = lax.broadcasted_iota(jnp.int32, (n_q * rows, 1), 0) // rows
    col = jnp.full((n_q * rows, 1), sink_ref[h0], F32)
    for g in range(1, n_q):
        col = jnp.where(head == g, sink_ref[h0 + g], col)
    return col


def _attn_ab_prompt_kernel(sink_ref, qa_ref, ka_ref, va_ref, qb_ref, kbt_ref, vbt_ref, o_ref):
    _mla_heads(qa_ref, [ka_ref], [va_ref], o_ref, MLA_STACK_PROMPT)
    kbt = kbt_ref[...].astype(BF16)
    vbt = vbt_ref[...].astype(BF16)
    sls = [slice(kh * HD, (kh + 1) * HD) for kh in range(KVH_B)]
    o = _gqa_groups(qb_ref, 0, H_B // KVH_B, [[kbt[sl, :]] for sl in sls], [[vbt[sl, :]] for sl in sls],
                    sink=_stacked_sink(sink_ref, 0, H_B, SEQ), kv_t=True)
    o_ref[:, H_A * V_A:] = o.astype(o_ref.dtype)


def _attn_ab_prompt(sink, qa, ka, va, qb, kbt, vbt):
    def blk(w):
        return pl.BlockSpec((SEQ, w), lambda b: (b, 0))
    tblk = pl.BlockSpec((None, KVH_B * HD, SEQ), lambda b: (b, 0, 0))
    return pl.pallas_call(
        _attn_ab_prompt_kernel,
        grid=(BATCH,),
        in_specs=[pl.BlockSpec(memory_space=pltpu.SMEM), blk(H_A * LANES), blk(H_A * LANES), blk(H_A * V_A),
                  blk(H_B * HD), tblk, tblk],
        out_specs=blk(D_MODEL),
        out_shape=jax.ShapeDtypeStruct((N_PROMPT, H_A * V_A + H_B * HD), BF16),
        compiler_params=_params(("arbitrary",)),
        name="attn_ab_prompt",
    )(sink, qa, ka, va, qb, kbt, vbt)


def _attn_ab_sample_kernel(sink_ref, qa_ref, ka_ref, va_ref, qb_ref, kb_ref, vb_ref,
                           ckv_ctx_ref, kpe_ctx_ref, kb_ctx_ref, vb_ctx_ref, w_kn_ref, w_v_ref, place_ref,
                           o_ref, ka_ctx_ref, va_ctx_ref):
    i = pl.program_id(1)

    @pl.when(i == 0)
    def _():
        ckv16 = ckv_ctx_ref[...].astype(BF16)
        k_nope = jnp.dot(ckv16, w_kn_ref[...], preferred_element_type=F32)
        k_rope = jnp.dot(kpe_ctx_ref[...].astype(BF16), place_ref[...], preferred_element_type=F32)
        ka_ctx_ref[...] = (k_nope + k_rope).astype(BF16)
        va_ctx_ref[...] = jnp.dot(ckv16, w_v_ref[...], preferred_element_type=F32).astype(BF16)

    _mla_heads(qa_ref, [ka_ref, ka_ctx_ref], [va_ref, va_ctx_ref], o_ref, MLA_STACK_SAMPLE)

    n_win = BQ_S + 2 * WINDOW
    start = pl.multiple_of(jnp.clip(i * BQ_S - WINDOW, 0, DEC_SEQ - n_win), WINDOW)
    row = lax.broadcasted_iota(jnp.int32, (H_B * BQ_S, n_win), 0)
    qpos = i * BQ_S + (row & (BQ_S - 1))
    kpos = start + lax.broadcasted_iota(jnp.int32, (H_B * BQ_S, n_win), 1)
    in_band = jnp.abs(qpos - kpos) <= WINDOW
    kwin = kb_ref[pl.ds(start, n_win), :].astype(BF16)
    vwin = vb_ref[pl.ds(start, n_win), :].astype(BF16)
    kctx = kb_ctx_ref[...].astype(BF16)
    vctx = vb_ctx_ref[...].astype(BF16)
    sls = [slice(kh * HD, (kh + 1) * HD) for kh in range(KVH_B)]
    o = _gqa_groups(qb_ref, 0, H_B // KVH_B, [[kwin[:, sl], kctx[:, sl]] for sl in sls],
                    [[vwin[:, sl], vctx[:, sl]] for sl in sls],
                    mask=in_band, sink=_stacked_sink(sink_ref, 0, H_B, BQ_S))
    o_ref[:, H_A * V_A:] = o.astype(o_ref.dtype)


def _attn_ab_sample(sink, qa, ka, va, qb, kb, vb, ckv_ctx, kpe_ctx, kb_ctx, vb_ctx, w_kn, w_v, place):
    qoff = N_PROMPT // BQ_S
    nq = DEC_SEQ // BQ_S
    boff = N_PROMPT // DEC_SEQ

    def qblk(w):
        return pl.BlockSpec((BQ_S, w), lambda b, i: (qoff + b * nq + i, 0))

    def bblk(w):
        return pl.BlockSpec((DEC_SEQ, w), lambda b, i: (boff + b, 0))

    def sblk(w):
        return pl.BlockSpec((DEC_SEQ, w), lambda b, i: (b, 0))

    def cblk(w):
        return pl.BlockSpec((PAST_LEN, w), lambda b, i: (b, 0))

    return pl.pallas_call(
        _attn_ab_sample_kernel,
        grid=(DEC_BATCH, nq),
        in_specs=[pl.BlockSpec(memory_space=pltpu.SMEM), qblk(H_A * LANES), bblk(H_A * LANES), bblk(H_A * V_A),
                  qblk(H_B * HD), sblk(KVH_B * HD), sblk(KVH_B * HD),
                  cblk(KV_LORA), cblk(LANES), cblk(KVH_B * HD), cblk(KVH_B * HD),
                  _const_spec((KV_LORA, H_A * LANES)), _const_spec((KV_LORA, H_A * V_A)),
                  _const_spec((LANES, H_A * LANES))],
        out_specs=pl.BlockSpec((BQ_S, D_MODEL), lambda b, i: (b * nq + i, 0)),
        out_shape=jax.ShapeDtypeStruct((N_SAMPLE, H_A * V_A + H_B * HD), BF16),
        scratch_shapes=[pltpu.VMEM((PAST_LEN, H_A * LANES), BF16), pltpu.VMEM((PAST_LEN, H_A * V_A), BF16)],
        compiler_params=_params(("arbitrary", "arbitrary")),
        name="attn_ab_sample",
    )(sink, qa, ka, va, qb, kb, vb, ckv_ctx, kpe_ctx, kb_ctx, vb_ctx, w_kn, w_v, place)


def _gqa_heads(q_ref, k_list, v_list, o_ref, stack):
    gq = H_C // KVH_C
    for kh0 in range(0, KVH_C, stack):
        sls = [slice(kh * HD, (kh + 1) * HD) for kh in range(kh0, kh0 + stack)]
        o = _gqa_groups(q_ref, kh0 * gq * HD, gq, [[k[:, sl] for k in k_list] for sl in sls],
                        [[v[:, sl] for v in v_list] for sl in sls])
        o_ref[:, kh0 * gq * HD:(kh0 + stack) * gq * HD] = o.astype(o_ref.dtype)


def _attn_c_prompt_kernel(q_ref, k_ref, v_ref, o_ref):
    _gqa_heads(q_ref, [k_ref[...].astype(BF16)], [v_ref[...].astype(BF16)], o_ref, GQA_STACK_PROMPT)


def _attn_c_prompt(q, k, v):
    def blk(w):
        return pl.BlockSpec((SEQ, w), lambda b: (b, 0))
    return pl.pallas_call(
        _attn_c_prompt_kernel,
        grid=(BATCH,),
        in_specs=[blk(H_C * HD), blk(KVH_C * HD), blk(KVH_C * HD)],
        out_specs=blk(H_C * HD),
        out_shape=jax.ShapeDtypeStruct((N_PROMPT, H_C * HD), BF16),
        compiler_params=_params(("arbitrary",)),
        name="attn_c_prompt",
    )(q, k, v)


def _attn_c_sample_kernel(q_ref, k_ref, v_ref, kc_ref, vc_ref, o_ref):
    _gqa_heads(q_ref, [k_ref[...].astype(BF16), kc_ref[...].astype(BF16)],
               [v_ref[...].astype(BF16), vc_ref[...].astype(BF16)], o_ref, GQA_STACK_SAMPLE)


def _attn_c_sample(q, k, v, k_ctx, v_ctx):
    qoff = N_PROMPT // BQ_S
    nq = DEC_SEQ // BQ_S
    w = KVH_C * HD
    return pl.pallas_call(
        _attn_c_sample_kernel,
        grid=(DEC_BATCH, nq),
        in_specs=[pl.BlockSpec((BQ_S, H_C * HD), lambda b, i: (qoff + b * nq + i, 0)),
                  pl.BlockSpec((DEC_SEQ, w), lambda b, i: (b, 0)),
                  pl.BlockSpec((DEC_SEQ, w), lambda b, i: (b, 0)),
                  pl.BlockSpec((PAST_LEN, w), lambda b, i: (b, 0)),
                  pl.BlockSpec((PAST_LEN, w), lambda b, i: (b, 0))],
        out_specs=pl.BlockSpec((BQ_S, H_C * HD), lambda b, i: (b * nq + i, 0)),
        out_shape=jax.ShapeDtypeStruct((N_SAMPLE, H_C * HD), BF16),
        compiler_params=_params(("arbitrary", "arbitrary")),
        name="attn_c_sample",
    )(q, k, v, k_ctx, v_ctx)


def _layer_norm(y, g, b):
    mu = jnp.mean(y, axis=-1, keepdims=True)
    yc = y - mu
    var = jnp.mean(yc * yc, axis=-1, keepdims=True)
    return yc * lax.rsqrt(var + EPS) * g + b


def _route(sel, aff):
    def row(a, j):
        return a[j:j + 1, :]

    scores = []
    for g in range(N_GROUPS):
        a0, a1, a2, a3 = (row(sel, 4 * g + j) for j in range(4))
        hi01, lo01 = jnp.maximum(a0, a1), jnp.minimum(a0, a1)
        hi23, lo23 = jnp.maximum(a2, a3), jnp.minimum(a2, a3)
        top1 = jnp.maximum(hi01, hi23)
        top2 = jnp.maximum(jnp.minimum(hi01, hi23), jnp.maximum(lo01, lo23))
        scores.append(top1 + top2)
    best = scores[0]
    gi = jnp.zeros(best.shape, jnp.int32)
    for g in range(1, N_GROUPS):
        better = scores[g] > best
        gi = jnp.where(better, g, gi)
        best = jnp.where(better, scores[g], best)

    def in_group(a, j):
        out = row(a, j)
        for g in range(1, N_GROUPS):
            out = jnp.where(gi == g, row(a, 4 * g + j), out)
        return out

    v = [in_group(sel, j) for j in range(4)]
    a = [in_group(aff, j) for j in range(4)]
    chosen = []
    for j in range(4):
        rank = jnp.zeros(best.shape, jnp.int32)
        for k in range(4):
            if k == j:
                continue
            ahead = (v[k] >= v[j]) if k < j else (v[k] > v[j])
            rank = rank + ahead.astype(jnp.int32)
        chosen.append(rank < 2)
    total = sum(jnp.where(chosen[j], a[j], 0.0) for j in range(4))
    w = [jnp.where(chosen[j], a[j], 0.0) / total for j in range(4)]
    pair = jnp.zeros(best.shape, jnp.int32)
    for p, (ja, jb) in enumerate(zip(SLOT_A_LOCAL, SLOT_B_LOCAL)):
        pair = jnp.where(chosen[ja] & chosen[jb], p, pair)
    gate_a = jnp.where(pair == 0, w[0], jnp.where(pair <= 2, w[2], w[3]))
    gate_b = jnp.where((pair == 0) | (pair == 1) | (pair == 4), w[1], jnp.where(pair == 5, w[2], w[0]))
    return gi * PAIRS_PER_GROUP + pair, gate_a, gate_b


def _post_attn_kernel(split_x, *refs):
    t = pl.program_id(0)
    op_ref, os_ref = refs[:2]
    if split_x:
        x = _pick(t, refs[2], refs[3])
        refs = refs[4:]
    else:
        x = refs[2][...]
        refs = refs[3:]
    (w_out_ref, gate_ref, shift_ref, scale_ref, lng_ref, lnb_ref, rw_ref, rb_ref, tri_ref,
     x1_ref, row_ref, route_ref, counts_ref, carry_ref) = refs
    o = _pick(t, op_ref, os_ref)
    subs = _sub_tiles(o.shape[0])

    a = [jnp.dot(o[s], w_out_ref[...], preferred_element_type=F32) for s in subs]
    x1 = [_layer_norm(ALPHA * x[s] + gate_ref[...] * a_s, lng_ref[...], lnb_ref[...]) for s, a_s in zip(subs, a)]
    h2 = [x1_s * (1.0 + scale_ref[...]) + shift_ref[...] for x1_s in x1]
    for s, x1_s, h2_s in zip(subs, x1, h2):
        x1_ref[s, :] = x1_s
        row_ref[s, :D_MODEL] = h2_s
    rw = rw_ref[...]
    rw_hi = rw.astype(BF16)
    rw_lo = (rw - rw_hi.astype(F32)).astype(BF16)
    rw_both = jnp.concatenate([rw_hi, rw_lo], axis=0)
    logits = []
    for h2_s in h2:
        h2_hi = h2_s.astype(BF16)
        h2_lo = (h2_s - h2_hi.astype(F32)).astype(BF16)
        by_hi = _qk(rw_both, h2_hi)
        logits.append(by_hi[:N_EXPERTS] + by_hi[N_EXPERTS:] + _qk(rw_hi, h2_lo))
    aff = [1.0 / (1.0 + jnp.exp(-l)) for l in logits]
    routed = [_route(aff_s + rb_ref[...], aff_s) for aff_s in aff]

    lane = lax.broadcasted_iota(jnp.int32, (LANES, TM), 0)
    for s, (_, gate_a, gate_b) in zip(subs, routed):
        meta_t = jnp.where(lane == 0, gate_a, jnp.where(lane == 1, gate_b, 0.0))
        row_ref[s, D_MODEL:] = meta_t.T

    @pl.when(t == 0)
    def _():
        carry_ref[...] = jnp.zeros_like(carry_ref)

    onehots = [lax.broadcasted_iota(jnp.int32, (BUCKET_ROWS, TM), 0) == bucket for bucket, _, _ in routed]
    prefixes = [jnp.dot(oh.astype(BF16), tri_ref[...], preferred_element_type=F32) for oh in onehots]
    carry = carry_ref[...]
    route_ref[...] = jnp.zeros(route_ref.shape, F32)
    for s, (bucket, _, _), onehot, prefix in zip(subs, routed, onehots, prefixes):
        rank = jnp.sum(jnp.where(onehot, prefix + carry[:, 0:1], 0.0), axis=0, keepdims=True)
        carry = carry + jnp.sum(onehot.astype(F32), axis=1, keepdims=True)
        route_ref[0:1, s] = bucket.astype(F32)
        route_ref[1:2, s] = rank
    carry_ref[...] = carry
    counts_ref[...] = carry


def _post_attn(layer, o_p, o_s, x_list, mod, w_out, ln_g, ln_b, rw_t, rb):
    split_x = len(x_list) == 2
    tok = pl.BlockSpec((TP, D_MODEL), lambda t: (t, 0))
    tri = jnp.asarray(np.arange(TM)[:, None] < np.arange(TM)[None, :], BF16)
    return pl.pallas_call(
        functools.partial(_post_attn_kernel, split_x),
        grid=(N_TOK // TP,),
        in_specs=_split_specs(TP) + (_split_specs(TP) if split_x else [tok]) + [
            _const_spec((D_MODEL, D_MODEL)), _mod_spec(layer, 2, TP), _mod_spec(layer, 3, TP),
            _mod_spec(layer, 4, TP),
            _const_spec((1, D_MODEL)), _const_spec((1, D_MODEL)), _const_spec((N_EXPERTS, D_MODEL)),
            _const_spec((N_EXPERTS, 1)), _const_spec((TM, TM))],
        out_specs=[tok, pl.BlockSpec((TP, ROW_W), lambda t: (t, 0)), pl.BlockSpec((8, TP), lambda t: (0, t)),
                   _const_spec((BUCKET_ROWS, LANES))],
        out_shape=[jax.ShapeDtypeStruct((N_TOK, D_MODEL), F32), jax.ShapeDtypeStruct((N_TOK, ROW_W), F32),
                   jax.ShapeDtypeStruct((8, N_TOK), F32), jax.ShapeDtypeStruct((BUCKET_ROWS, LANES), F32)],
        scratch_shapes=[pltpu.VMEM((BUCKET_ROWS, LANES), F32)],
        compiler_params=_params(("arbitrary",)),
        name="post_attn",
    )(o_p, o_s, *x_list, w_out, mod, mod, mod, ln_g, ln_b, rw_t, rb, tri)


def _moe_kernel(ba_ref, bb_ref, first_ref, cnt_ref, nt_ref, x_hbm, wga_ref, wua_ref, wda_ref, wgb_ref, wub_ref,
                wdb_ref, o_hbm, xbuf, obuf, xsem, osem, wga_s, wua_s, wda_s, wgb_s, wub_s, wdb_s):
    b = pl.program_id(0)
    prev = jnp.maximum(b - 1, 0)
    n_tiles = nt_ref[0]
    slots = ((ba_ref, (wga_ref, wua_ref, wda_ref), (wga_s, wua_s, wda_s)),
             (bb_ref, (wgb_ref, wub_ref, wdb_ref), (wgb_s, wub_s, wdb_s)))

    def fetch(g):
        return pltpu.make_async_copy(x_hbm.at[pl.ds(pl.multiple_of(g * TME, TME), TME), :], xbuf.at[g % 2],
                                     xsem.at[g % 2])

    def store(g):
        return pltpu.make_async_copy(obuf.at[g % 2], o_hbm.at[pl.ds(pl.multiple_of(g * TME, TME), TME), :],
                                     osem.at[g % 2])

    @pl.when((b == 0) & (n_tiles > 0))
    def _():
        fetch(0).start()

    for t_ref, w_refs, w_scr in slots:
        @pl.when((b == 0) | (t_ref[b] != t_ref[prev]))
        def _(w_refs=w_refs, w_scr=w_scr):
            for w_ref, s_ref in zip(w_refs, w_scr):
                s_ref[...] = w_ref[...].astype(BF16)

    def tile(j, carry):
        g = first_ref[b] + j
        fetch(g).wait()

        @pl.when(g + 1 < n_tiles)
        def _():
            fetch(g + 1).start()

        x_ref = xbuf.at[g % 2]
        x = x_ref[:, :D_MODEL].astype(BF16)
        acc = None
        for slot, (_, _, (wg_s, wu_s, wd_s)) in enumerate(slots):
            gt = jnp.dot(x, wg_s[...], preferred_element_type=F32)
            up = jnp.dot(x, wu_s[...], preferred_element_type=F32)
            gate = x_ref[:, D_MODEL + slot:D_MODEL + slot + 1]
            hid = (gt / (1.0 + jnp.exp(-gt))) * up * gate
            y = jnp.dot(hid.astype(BF16), wd_s[...], preferred_element_type=F32)
            acc = y if acc is None else acc + y

        @pl.when(g >= 2)
        def _():
            store(g - 2).wait()
        obuf[g % 2] = acc
        store(g).start()
        return carry
    lax.fori_loop(0, cnt_ref[b], tile, 0)

    @pl.when(b == N_BUCKETS - 1)
    def _():
        @pl.when(n_tiles >= 2)
        def _():
            store(n_tiles - 2).wait()

        @pl.when(n_tiles >= 1)
        def _():
            store(n_tiles - 1).wait()

        obuf[0] = jnp.zeros(obuf.shape[1:], F32)

        def unused(g, carry):
            cp = pltpu.make_async_copy(obuf.at[0], o_hbm.at[pl.ds(pl.multiple_of(g * TME, TME), TME), :], osem.at[0])
            cp.start()
            cp.wait()
            return carry
        lax.fori_loop(n_tiles, MOE_TILES, unused, 0)


def _moe(layer, bucket_a, bucket_b, first_tile, tile_cnt, n_tiles, rows, w_gate, w_up, w_down):
    def wspec(shape, which):
        if which == 0:
            return pl.BlockSpec((None, None) + shape, lambda b, ba, bb, ft, tc, nt: (layer, ba[b], 0, 0))
        return pl.BlockSpec((None, None) + shape, lambda b, ba, bb, ft, tc, nt: (layer, bb[b], 0, 0))

    up_shape, down_shape = (D_MODEL, D_EXPERT), (D_EXPERT, D_MODEL)
    grid_spec = pltpu.PrefetchScalarGridSpec(
        num_scalar_prefetch=5,
        grid=(N_BUCKETS,),
        in_specs=[pl.BlockSpec(memory_space=pl.ANY),
                  wspec(up_shape, 0), wspec(up_shape, 0), wspec(down_shape, 0),
                  wspec(up_shape, 1), wspec(up_shape, 1), wspec(down_shape, 1)],
        out_specs=pl.BlockSpec(memory_space=pl.ANY),
        scratch_shapes=[pltpu.VMEM((2, TME, ROW_W), F32), pltpu.VMEM((2, TME, D_MODEL), F32),
                        pltpu.SemaphoreType.DMA((2,)), pltpu.SemaphoreType.DMA((2,))]
        + [pltpu.VMEM(s, BF16) for s in (up_shape, up_shape, down_shape) * 2],
    )
    return pl.pallas_call(
        _moe_kernel,
        grid_spec=grid_spec,
        out_shape=jax.ShapeDtypeStruct((MOE_ROWS, D_MODEL), F32),
        compiler_params=_params(("arbitrary",)),
        name="moe_experts",
    )(bucket_a, bucket_b, first_tile, tile_cnt, n_tiles, rows, w_gate, w_up, w_down, w_gate, w_up, w_down)


def _row_copy(src_ref, src_row, dst_ref, dst_row, sem):
    return pltpu.make_async_copy(src_ref.at[pl.ds(src_row, 1), :], dst_ref.at[pl.ds(dst_row, 1), :], sem)


def _dispatch_kernel(pos_ref, pend_ref, cnt_ref, nt_ref, src_ref, out_ref, buf, zero_ref, in_sem, sem, zsem):
    t = pl.program_id(0)

    def fetch(tile):
        return pltpu.make_async_copy(src_ref.at[pl.ds(pl.multiple_of(tile * TM, TM), TM), :],
                                     buf.at[tile % STAGE_SLOTS], in_sem.at[tile % STAGE_SLOTS])

    @pl.when(t == 0)
    def _():
        fetch(0).start()
        zero_ref[...] = jnp.zeros_like(zero_ref)

        def zero_tile(row0):
            return pltpu.make_async_copy(zero_ref, out_ref.at[pl.ds(pl.multiple_of(row0, TME), TME), :], zsem)

        for b in range(N_BUCKETS):
            @pl.when(cnt_ref[b] > 0)
            def _(b=b):
                zero_tile(pend_ref[b] - TME).start()

        def start_unused(i, carry):
            zero_tile(i * TME).start()
            return carry
        lax.fori_loop(nt_ref[0], MOE_TILES, start_unused, 0)

        for b in range(N_BUCKETS):
            @pl.when(cnt_ref[b] > 0)
            def _(b=b):
                zero_tile(pend_ref[b] - TME).wait()

        def wait_unused(i, carry):
            zero_tile(i * TME).wait()
            return carry
        lax.fori_loop(nt_ref[0], MOE_TILES, wait_unused, 0)

    @pl.when(t + 1 < N_TILES)
    def _():
        fetch(t + 1).start()

    fetch(t).wait()
    tile_ref = buf.at[t % STAGE_SLOTS]

    for r in range(TM):
        _row_copy(tile_ref, r, out_ref, pos_ref[t * TM + r], sem.at[t % 2]).start(priority=r % 2)

    def drain(tile):
        pltpu.make_async_copy(buf.at[0], out_ref.at[pl.ds(0, TM), :], sem.at[tile % 2]).wait()

    @pl.when(t > 0)
    def _():
        drain(t - 1)

    @pl.when(t == N_TILES - 1)
    def _():
        drain(t)


def _dispatch(pos, pend, counts, n_tiles, rows):
    return pl.pallas_call(
        _dispatch_kernel,
        grid_spec=pltpu.PrefetchScalarGridSpec(
            num_scalar_prefetch=4, grid=(N_TILES,),
            in_specs=[pl.BlockSpec(memory_space=pl.ANY)],
            out_specs=pl.BlockSpec(memory_space=pl.ANY),
            scratch_shapes=[pltpu.VMEM((STAGE_SLOTS, TM, ROW_W), F32), pltpu.VMEM((TME, ROW_W), F32),
                            pltpu.SemaphoreType.DMA((STAGE_SLOTS,)), pltpu.SemaphoreType.DMA((2,)),
                            pltpu.SemaphoreType.DMA]),
        out_shape=jax.ShapeDtypeStruct((MOE_ROWS, ROW_W), F32),
        compiler_params=_params(("arbitrary",)),
        name="dispatch_rows",
    )(pos, pend, counts, n_tiles, rows)


def _plan(route, counts):
    bucket = route[0].astype(jnp.int32)
    rank = route[1].astype(jnp.int32)
    counts = counts[:N_BUCKETS, 0].astype(jnp.int32)
    padded = ((counts + TME - 1) // TME) * TME
    pend = jnp.cumsum(padded)
    pstart = pend - padded
    ids = jnp.arange(N_BUCKETS, dtype=jnp.int32)
    pos = rank + jnp.sum(jnp.where(bucket[None, :] == ids[:, None], pstart[:, None], 0), axis=0)
    n_tiles = pend[-1] // TME
    nonempty = counts > 0
    earlier = nonempty[None, :] & (ids[None, :] <= ids[:, None])
    last_le = jnp.max(jnp.where(earlier, ids[None, :], -1), axis=1)
    first = jnp.minimum(jnp.min(jnp.where(nonempty, ids, N_BUCKETS)), N_BUCKETS - 1)
    eff = jnp.where(last_le >= 0, last_le, first)
    group, pair = eff // PAIRS_PER_GROUP, eff % PAIRS_PER_GROUP
    slot_a = jnp.asarray(SLOT_A_LOCAL, jnp.int32)
    slot_b = jnp.asarray(SLOT_B_LOCAL, jnp.int32)
    pair_hot = pair[:, None] == jnp.arange(PAIRS_PER_GROUP, dtype=jnp.int32)[None, :]
    bucket_a = group * EXPERTS_PER_GROUP + jnp.sum(jnp.where(pair_hot, slot_a[None, :], 0), axis=1)
    bucket_b = group * EXPERTS_PER_GROUP + jnp.sum(jnp.where(pair_hot, slot_b[None, :], 0), axis=1)
    return (pos, pend, counts, n_tiles.reshape(1), bucket_a.astype(jnp.int32), bucket_b.astype(jnp.int32),
            (pstart // TME).astype(jnp.int32), (padded // TME).astype(jnp.int32))


def _post_moe_kernel(split_out, pos_ref, f_ref, x1_ref, gate_ref, lng_ref, lnb_ref, *refs):
    out_refs, (fbuf, sem) = refs[:-2], refs[-2:]
    t = pl.program_id(0)
    tq = x1_ref.shape[0]
    n_steps = N_TOK // tq
    p_steps = N_PROMPT // tq

    def wait_tile(slot):
        pltpu.make_async_copy(f_ref.at[pl.ds(0, tq), :], fbuf.at[slot], sem.at[slot]).wait()

    @pl.when(t == 0)
    def _():
        def issue(r, carry):
            _row_copy(f_ref, pos_ref[r], fbuf.at[r // tq], r % tq, sem.at[r // tq]).start()
            return carry
        lax.fori_loop(0, (GATHER_SLOTS - 1) * tq, issue, 0, unroll=8)

    slot = t % GATHER_SLOTS
    wait_tile(slot)
    ahead = jnp.minimum(t + GATHER_SLOTS - 1, n_steps - 1)
    aslot = (t + GATHER_SLOTS - 1) % GATHER_SLOTS
    for r in range(tq):
        _row_copy(f_ref, pos_ref[ahead * tq + r], fbuf.at[aslot], r, sem.at[aslot]).start(priority=r % 2)
    subs = _sub_tiles(tq)
    y = [_layer_norm(ALPHA * x1_ref[s, :] + gate_ref[...] * fbuf[slot, s, :], lng_ref[...], lnb_ref[...])
         for s in subs]

    @pl.when(t == n_steps - 1)
    def _():
        for k in range(1, GATHER_SLOTS):
            wait_tile((t + k) % GATHER_SLOTS)
    if split_out:
        @pl.when(t < p_steps)
        def _():
            for s, y_s in zip(subs, y):
                out_refs[0][s, :] = y_s

        @pl.when(t >= p_steps)
        def _():
            for s, y_s in zip(subs, y):
                out_refs[1][s, :] = y_s
    else:
        for s, y_s in zip(subs, y):
            out_refs[0][s, :] = y_s


def _post_moe(layer, pos, f_sorted, x1, mod, ln_g, ln_b, split_out):
    tq = TQ
    p_steps = N_PROMPT // tq
    tok = pl.BlockSpec((tq, D_MODEL), lambda t, p: (t, 0))
    if split_out:
        out_specs = [pl.BlockSpec((tq, D_MODEL), lambda t, p: (jnp.minimum(t, p_steps - 1), 0)),
                     pl.BlockSpec((tq, D_MODEL), lambda t, p: (jnp.maximum(t - p_steps, 0), 0))]
        out_shape = [jax.ShapeDtypeStruct((N_PROMPT, D_MODEL), F32), jax.ShapeDtypeStruct((N_SAMPLE, D_MODEL), F32)]
    else:
        out_specs = [tok]
        out_shape = [jax.ShapeDtypeStruct((N_TOK, D_MODEL), F32)]
    mod_spec = pl.BlockSpec((None, None, None, 1, D_MODEL), lambda t, p: (layer, _mod_row(t, tq), 5, 0, 0))
    return pl.pallas_call(
        functools.partial(_post_moe_kernel, split_out),
        grid_spec=pltpu.PrefetchScalarGridSpec(
            num_scalar_prefetch=1, grid=(N_TOK // tq,),
            in_specs=[pl.BlockSpec(memory_space=pl.ANY), tok, mod_spec,
                      pl.BlockSpec((1, D_MODEL), lambda t, p: (0, 0)), pl.BlockSpec((1, D_MODEL), lambda t, p: (0, 0))],
            out_specs=out_specs,
            scratch_shapes=[pltpu.VMEM((GATHER_SLOTS, tq, D_MODEL), F32),
                            pltpu.SemaphoreType.DMA((GATHER_SLOTS,))]),
        out_shape=out_shape,
        compiler_params=_params(("arbitrary",)),
        name="post_moe",
    )(pos, f_sorted, x1, mod, ln_g, ln_b)


def _ffn(layer, o_p, o_s, x_list, mod, w_out, ln_g, ln_b, rw_t, rb, w_gate, w_up, w_down, split_out):
    x1, rows, route, counts = _post_attn(layer, o_p, o_s, x_list, mod, w_out, ln_g[layer, 0][None],
                                         ln_b[layer, 0][None], rw_t, rb)
    pos, pend, counts, n_tiles, bucket_a, bucket_b, first_tile, tile_cnt = _plan(route, counts)
    rows_sorted = _dispatch(pos, pend, counts, n_tiles, rows)
    f_sorted = _moe(layer, bucket_a, bucket_b, first_tile, tile_cnt, n_tiles, rows_sorted, w_gate, w_up, w_down)
    return _post_moe(layer, pos, f_sorted, x1, mod, ln_g[layer, 1][None], ln_b[layer, 1][None], split_out)


def kernel(x_prompt, x_sample, c, cache_mla_ckv, cache_mla_kpe, cache_swa_k, cache_swa_v, cache_gqa_k, cache_gqa_v, c_ctx, w_mod, b_mod, ln_g, ln_b, w_in_ab, mla_q_norm, mla_w_uq, mla_kv_norm, mla_w_ukv, swa_sink, w_out_ab, w_in_c, gqa_q_norm, gqa_k_norm, w_out_c, router_w, router_bias, exp_w_gate, exp_w_up, exp_w_down):
    xp = x_prompt.reshape(N_PROMPT, D_MODEL)
    xs = x_sample.reshape(N_SAMPLE, D_MODEL)
    cond = jnp.concatenate([c_ctx[None], c, jnp.zeros((MOD_ROWS - 1 - DEC_BATCH, D_MODEL), F32)], axis=0)
    mod = _modulation(cond, w_mod, b_mod).reshape(DEPTH, MOD_ROWS, 6, 1, D_MODEL)
    tabs = _rope_tables()
    rw_t = router_w.T
    rb = router_bias.reshape(N_EXPERTS, 1)

    w = w_in_ab[0]
    pad = jnp.zeros((D_MODEL, AB_COLS - w.shape[1]), F32)
    kpe0 = Q_LORA + KV_LORA
    w_in = jnp.concatenate([w[:, :kpe0], w[:, kpe0 + ROPE_A:], w[:, kpe0:kpe0 + ROPE_A], pad], axis=1).astype(BF16)
    w_uq, w_kn, w_v, place = _mla_weights(mla_w_uq[0], mla_w_ukv[0])
    qa, ka, va, qb, ckv_p, kpe_p, kbt_p, vbt_p, kb_s, vb_s = _inproj_ab(
        xp, xs, mod, w_in, mla_q_norm[0][None], mla_kv_norm[0][None], w_uq, w_kn, w_v, place, tabs)
    sink = swa_sink[0]
    o_p = _attn_ab_prompt(sink, qa, ka, va, qb, kbt_p, vbt_p)
    kpe_ctx = jnp.pad(cache_mla_kpe[:, 0].reshape(-1, ROPE_A), ((0, 0), (0, LANES - ROPE_A)))
    o_s = _attn_ab_sample(sink, qa, ka, va, qb, kb_s, vb_s,
                          cache_mla_ckv[:, 0].reshape(-1, KV_LORA), kpe_ctx,
                          cache_swa_k[:, 0].reshape(-1, KVH_B * HD), cache_swa_v[:, 0].reshape(-1, KVH_B * HD),
                          w_kn, w_v, place)
    (x2,) = _ffn(0, o_p, o_s, [xp, xs], mod, w_out_ab[0].astype(BF16), ln_g, ln_b, rw_t, rb,
                 exp_w_gate, exp_w_up, exp_w_down, split_out=False)

    g_full = jnp.concatenate([jnp.tile(gqa_q_norm[0], H_C), jnp.tile(gqa_k_norm[0], KVH_C)])[None]
    qc, kc_p, vc_p, kc_s, vc_s = _inproj_c(x2, mod, w_in_c[0].astype(BF16), g_full, tabs)
    oc_p = _attn_c_prompt(qc, kc_p, vc_p)
    oc_s = _attn_c_sample(qc, kc_s, vc_s, cache_gqa_k[:, 0].reshape(-1, KVH_C * HD),
                          cache_gqa_v[:, 0].reshape(-1, KVH_C * HD))
    y_p, y_s = _ffn(1, oc_p, oc_s, [x2], mod, w_out_c[0].astype(BF16), ln_g, ln_b, rw_t, rb,
                    exp_w_gate, exp_w_up, exp_w_down, split_out=True)

    y_prompt = y_p.reshape(BATCH, SEQ, D_MODEL)
    y_sample = y_s.reshape(DEC_BATCH, DEC_SEQ, D_MODEL)
    new_ckv = ckv_p.reshape(BATCH, 1, SEQ, KV_LORA)
    new_kpe = kpe_p.reshape(BATCH, 1, SEQ, ROPE_A)
    new_swk = kbt_p.reshape(BATCH, 1, KVH_B, HD, SEQ).transpose(0, 1, 4, 2, 3)
    new_swv = vbt_p.reshape(BATCH, 1, KVH_B, HD, SEQ).transpose(0, 1, 4, 2, 3)
    new_gk = kc_p.reshape(BATCH, 1, SEQ, KVH_C, HD)
    new_gv = vc_p.reshape(BATCH, 1, SEQ, KVH_C, HD)
    return y_prompt, y_sample, new_ckv, new_kpe, new_swk, new_swv, new_gk, new_gv
```

```python
import functools

import numpy as np
import jax
import jax.numpy as jnp
from jax import lax
from jax.experimental import pallas as pl
from jax.experimental.pallas import tpu as pltpu

D_MODEL = 1024
BATCH = 32
SEQ = 256
DEPTH = 2
DEC_BATCH = 4
DEC_SEQ = 1024
PAST_LEN = 256
GRID_W = 64
ROPE_THETA = 10000.0
WINDOW = 128
HD = 64
H_A = 8
NOPE_A = 64
ROPE_A = 32
V_A = 64
Q_LORA = 256
KV_LORA = 128
H_B = 8
KVH_B = 2
H_C = 16
KVH_C = 4
N_EXPERTS = 16
N_GROUPS = 4
EXPERTS_PER_GROUP = 4
D_EXPERT = 512
ALPHA = (2 * DEPTH) ** 0.25
NEG_INF = -1e30
EPS = 1e-6

LANES = 128
N_PROMPT = BATCH * SEQ
N_SAMPLE = DEC_BATCH * DEC_SEQ
N_TOK = N_PROMPT + N_SAMPLE
TM = 256
TP = 1024
TP_AB = 512
TQ = 512
N_TILES = N_TOK // TM
BQ_S = 256
LOG2E = 1.4426950408889634
MLA_STACK_PROMPT = 4
MLA_STACK_SAMPLE = 4
GQA_STACK_PROMPT = 4
GQA_STACK_SAMPLE = 2
MOD_ROWS = 8

PAIRS_PER_GROUP = 6
N_BUCKETS = N_GROUPS * PAIRS_PER_GROUP
SLOT_A_LOCAL = (0, 2, 2, 3, 3, 3)
SLOT_B_LOCAL = (1, 1, 0, 0, 1, 2)
TME = 256
MOE_TILES = -(-(N_TOK + N_BUCKETS * (TME - 1)) // TME)
MOE_ROWS = MOE_TILES * TME
MOE_ITEMS = (MOE_TILES + N_BUCKETS) // 2

BUCKET_ROWS = 32
ROW_W = D_MODEL + LANES
STAGE_SLOTS = 3
GATHER_SLOTS = 3

COL_QB = Q_LORA + KV_LORA
COL_KB = COL_QB + H_B * HD
COL_VB = COL_KB + KVH_B * HD
COL_KPE = COL_VB + KVH_B * HD
AB_COLS = COL_KPE + LANES
C_NORM_COLS = H_C * HD + KVH_C * HD
NORM_BLOCK = 256

BF16 = jnp.bfloat16
F32 = jnp.float32
VMEM_LIMIT = 52 * 1024 * 1024


def _mod_row(t, tile=TM):
    p_tiles = N_PROMPT // tile
    return jnp.where(t < p_tiles, 0, 1 + (t - p_tiles) // (DEC_SEQ // tile))


def _mod_spec(layer, chunk, tile=TM):
    return pl.BlockSpec((None, None, None, 1, D_MODEL), lambda t: (layer, _mod_row(t, tile), chunk, 0, 0))


def _const_spec(shape):
    nd = len(shape)
    return pl.BlockSpec(shape, lambda *_: (0,) * nd)


def _split_specs(tile=TM):
    p_tiles = N_PROMPT // tile
    return [
        pl.BlockSpec((tile, D_MODEL), lambda t: (jnp.minimum(t, p_tiles - 1), 0)),
        pl.BlockSpec((tile, D_MODEL), lambda t: (jnp.maximum(t - p_tiles, 0), 0)),
    ]


def _group_spec(tile, width, prompt):
    p_tiles = N_PROMPT // tile
    if prompt:
        return pl.BlockSpec((tile, width), lambda t: (jnp.minimum(t, p_tiles - 1), 0))
    return pl.BlockSpec((tile, width), lambda t: (jnp.maximum(t - p_tiles, 0), 0))


def _pick(t, a_ref, b_ref):
    return jnp.where(t < N_PROMPT // a_ref.shape[0], a_ref[...], b_ref[...])


def _sub_tiles(rows):
    return [slice(i * TM, (i + 1) * TM) for i in range(rows // TM)]


def _params(sem):
    return pltpu.CompilerParams(dimension_semantics=sem, vmem_limit_bytes=VMEM_LIMIT)


def _mod_kernel(cond_ref, w_ref, b_ref, o_ref):
    c = cond_ref[...]
    s = (c / (1.0 + jnp.exp(-c))).astype(BF16)
    o_ref[...] = jnp.dot(s, w_ref[...].astype(BF16), preferred_element_type=F32) + b_ref[...]


def _modulation(cond, w_mod, b_mod):
    tn = 1536
    return pl.pallas_call(
        _mod_kernel,
        grid=(DEPTH, 6 * D_MODEL // tn),
        in_specs=[
            pl.BlockSpec((MOD_ROWS, D_MODEL), lambda l, j: (0, 0)),
            pl.BlockSpec((None, D_MODEL, tn), lambda l, j: (l, 0, j)),
            pl.BlockSpec((None, 1, tn), lambda l, j: (l, 0, j)),
        ],
        out_specs=pl.BlockSpec((None, MOD_ROWS, tn), lambda l, j: (l, 0, j)),
        out_shape=jax.ShapeDtypeStruct((DEPTH, MOD_ROWS, 6 * D_MODEL), F32),
        compiler_params=_params(("arbitrary", "arbitrary")),
        name="modulation",
    )(cond, w_mod, b_mod.reshape(DEPTH, 1, 6 * D_MODEL))


def _grid_cos_sin(half):
    n_freq = half // 2
    rows = DEC_SEQ // GRID_W
    row = jnp.repeat(jnp.arange(rows, dtype=F32), GRID_W)
    col = jnp.tile(jnp.arange(GRID_W, dtype=F32), rows)
    inv = 1.0 / (ROPE_THETA ** (jnp.arange(n_freq, dtype=F32) / n_freq))
    ang = jnp.concatenate([row[:, None] * inv, col[:, None] * inv], axis=-1)
    return jnp.cos(ang), jnp.sin(ang)


def _rope_tables():
    tabs = []
    for half in (HD // 2, ROPE_A // 2):
        cos, sin = _grid_cos_sin(half)
        reps = LANES // (2 * half)
        tabs.append(jnp.tile(jnp.concatenate([cos, cos], axis=-1), (1, reps)))
        tabs.append(jnp.tile(jnp.concatenate([-sin, sin], axis=-1), (1, reps)))
    half = ROPE_A // 2
    one = jnp.ones((DEC_SEQ, NOPE_A), F32)
    zero = jnp.zeros((DEC_SEQ, NOPE_A), F32)
    pad1 = jnp.ones((DEC_SEQ, LANES - NOPE_A - ROPE_A), F32)
    pad0 = jnp.zeros((DEC_SEQ, LANES - NOPE_A - ROPE_A), F32)
    zh = jnp.zeros((DEC_SEQ, half), F32)
    tabs.append(jnp.concatenate([one, cos, cos, pad1], axis=-1))
    tabs.append(jnp.concatenate([zero, -sin, zh, pad0], axis=-1))
    tabs.append(jnp.concatenate([zero, zh, sin, pad0], axis=-1))
    return tuple(tabs)


def _rope(x, c, s, half):
    w = x.shape[1]
    reps = w // LANES
    if reps > 1:
        c = jnp.concatenate([c] * reps, axis=1)
        s = jnp.concatenate([s] * reps, axis=1)
    ahead = pltpu.roll(x, w - half, 1)
    behind = pltpu.roll(x, half, 1)
    lane = lax.broadcasted_iota(jnp.int32, x.shape, 1)
    swapped = jnp.where((lane & (2 * half - 1)) < half, ahead, behind)
    return x * c + swapped * s


def _rope_specs(n, tile):
    p_tiles = N_PROMPT // tile

    def idx(t):
        return (jnp.maximum(t - p_tiles, 0) % (DEC_SEQ // tile), 0)
    return [pl.BlockSpec((tile, LANES), idx)] * n


def _rms(x, g_ref):
    return x * lax.rsqrt(jnp.mean(x * x, axis=-1, keepdims=True) + EPS) * g_ref[...]


def _inproj_ab_kernel(xp_ref, xs_ref, shift_ref, scale_ref, w_in_ref, qn_ref, kvn_ref, w_uq_ref, w_kn_ref, w_v_ref,
                      place_ref, c64_ref, s64_ref, c32_ref, s32_ref, ca_ref, sa_ref, sb_ref,
                      qa_ref, ka_ref, va_ref, qb_ref, ckv_p_ref, kpe_p_ref, kbt_p_ref, vbt_p_ref, kb_s_ref, vb_s_ref):
    t = pl.program_id(0)
    x = _pick(t, xp_ref, xs_ref)
    subs = _sub_tiles(x.shape[0])
    h = [(x[s] * (1.0 + scale_ref[...]) + shift_ref[...]).astype(BF16) for s in subs]
    z = [jnp.dot(h_s, w_in_ref[...], preferred_element_type=F32) for h_s in h]
    ql = [_rms(z_s[:, :Q_LORA], qn_ref).astype(BF16) for z_s in z]
    qa = [jnp.dot(ql_s, w_uq_ref[...], preferred_element_type=F32) for ql_s in ql]
    ckv = [_rms(z_s[:, Q_LORA:Q_LORA + KV_LORA], kvn_ref) for z_s in z]
    k_nope = [jnp.dot(c_s.astype(BF16), w_kn_ref[...], preferred_element_type=F32) for c_s in ckv]
    va = [jnp.dot(c_s.astype(BF16), w_v_ref[...], preferred_element_type=F32).astype(BF16) for c_s in ckv]
    for s, va_s in zip(subs, va):
        va_ref[s, :] = va_s
    qb = [z_s[:, COL_QB:COL_KB] for z_s in z]
    kb = [z_s[:, COL_KB:COL_VB] for z_s in z]
    vb = [z_s[:, COL_VB:COL_KPE] for z_s in z]
    kpe = [z_s[:, COL_KPE:] for z_s in z]

    def finish(qa_out, qb_out, kpe_out):
        k_rope = [jnp.dot(kpe_s.astype(BF16), place_ref[...], preferred_element_type=F32) for kpe_s in kpe_out]
        for i, s in enumerate(subs):
            qa_ref[s, :] = qa_out[i].astype(BF16)
            qb_ref[s, :] = qb_out[i].astype(BF16)
            ka_ref[s, :] = (k_nope[i] + k_rope[i]).astype(BF16)

    p_tiles = N_PROMPT // x.shape[0]

    @pl.when(t < p_tiles)
    def _():
        finish(qa, qb, kpe)
        for i, s in enumerate(subs):
            ckv_p_ref[s, :] = ckv[i]
            kpe_p_ref[s, :] = kpe[i][:, :ROPE_A]
            kbt_p_ref[i] = kb[i].T
            vbt_p_ref[i] = vb[i].T

    @pl.when(t >= p_tiles)
    def _():
        w = qa[0].shape[1]
        reps = w // LANES
        half = ROPE_A // 2
        qa_rot, qb_rot, kb_rot, kpe_rot = [], [], [], []
        for i, s in enumerate(subs):
            ca = jnp.concatenate([ca_ref[s, :]] * reps, axis=1)
            sa = jnp.concatenate([sa_ref[s, :]] * reps, axis=1)
            sb = jnp.concatenate([sb_ref[s, :]] * reps, axis=1)
            qa_rot.append(qa[i] * ca + pltpu.roll(qa[i], w - half, 1) * sa + pltpu.roll(qa[i], half, 1) * sb)
            qb_rot.append(_rope(qb[i], c64_ref[s, :], s64_ref[s, :], HD // 2))
            kb_rot.append(_rope(kb[i], c64_ref[s, :], s64_ref[s, :], HD // 2))
            kpe_rot.append(_rope(kpe[i], c32_ref[s, :], s32_ref[s, :], ROPE_A // 2))
        finish(qa_rot, qb_rot, kpe_rot)
        for i, s in enumerate(subs):
            kb_s_ref[s, :] = kb_rot[i]
            vb_s_ref[s, :] = vb[i]


def _mla_weights(w_uq, w_ukv):
    pad = LANES - NOPE_A - ROPE_A
    uq = w_uq.reshape(Q_LORA, H_A, NOPE_A + ROPE_A)
    uq = jnp.pad(uq, ((0, 0), (0, 0), (0, pad))).reshape(Q_LORA, H_A * LANES)
    ukv = w_ukv.reshape(KV_LORA, H_A, NOPE_A + V_A)
    kn = jnp.pad(ukv[:, :, :NOPE_A], ((0, 0), (0, 0), (0, LANES - NOPE_A))).reshape(KV_LORA, H_A * LANES)
    wv = ukv[:, :, NOPE_A:].reshape(KV_LORA, H_A * V_A)
    src = np.arange(LANES)[:, None]
    dst = np.arange(H_A * LANES)[None, :] % LANES
    place = (src < ROPE_A) & (dst == src + NOPE_A)
    return uq.astype(BF16), kn.astype(BF16), wv.astype(BF16), jnp.asarray(place, BF16)


def _inproj_ab(xp, xs, mod, w_in, q_norm, kv_norm, w_uq, w_kn, w_v, place, tabs):
    widths = [(H_A * LANES, BF16), (H_A * LANES, BF16), (H_A * V_A, BF16), (H_B * HD, BF16)]
    tp = TP_AB
    out_specs = [pl.BlockSpec((tp, w), lambda t: (t, 0)) for w, _ in widths]
    out_shape = [jax.ShapeDtypeStruct((N_TOK, w), dt) for w, dt in widths]
    for w in (KV_LORA, ROPE_A):
        out_specs.append(_group_spec(tp, w, prompt=True))
        out_shape.append(jax.ShapeDtypeStruct((N_PROMPT, w), F32))
    assert SEQ == TM
    p_tiles = N_PROMPT // tp
    for _ in range(2):
        out_specs.append(pl.BlockSpec((tp // SEQ, KVH_B * HD, SEQ), lambda t: (jnp.minimum(t, p_tiles - 1), 0, 0)))
        out_shape.append(jax.ShapeDtypeStruct((BATCH, KVH_B * HD, SEQ), F32))
    for w in (KVH_B * HD, KVH_B * HD):
        out_specs.append(_group_spec(tp, w, prompt=False))
        out_shape.append(jax.ShapeDtypeStruct((N_SAMPLE, w), F32))
    return pl.pallas_call(
        _inproj_ab_kernel,
        grid=(N_TOK // tp,),
        in_specs=_split_specs(tp) + [_mod_spec(0, 0, tp), _mod_spec(0, 1, tp), _const_spec((D_MODEL, AB_COLS)),
                                     _const_spec((1, Q_LORA)), _const_spec((1, KV_LORA)),
                                     _const_spec((Q_LORA, H_A * LANES)), _const_spec((KV_LORA, H_A * LANES)),
                                     _const_spec((KV_LORA, H_A * V_A)), _const_spec((LANES, H_A * LANES))]
        + _rope_specs(7, tp),
        out_specs=out_specs,
        out_shape=out_shape,
        compiler_params=_params(("arbitrary",)),
        name="inproj_ab",
    )(xp, xs, mod, mod, w_in, q_norm, kv_norm, w_uq, w_kn, w_v, place, *tabs)


def _inproj_c_kernel(x_ref, shift_ref, scale_ref, w_in_ref, g_ref, ones_ref, c64_ref, s64_ref,
                     q_ref, k_p_ref, v_p_ref, k_s_ref, v_s_ref):
    t = pl.program_id(0)
    subs = _sub_tiles(x_ref.shape[0])
    h = [(x_ref[s, :] * (1.0 + scale_ref[...]) + shift_ref[...]).astype(BF16) for s in subs]
    z = [jnp.dot(h_s, w_in_ref[...], preferred_element_type=F32) for h_s in h]
    qk = [z_s[:, :C_NORM_COLS] for z_s in z]
    sq = [qk_s * qk_s for qk_s in qk]
    sq_hi = [sq_s.astype(BF16) for sq_s in sq]
    sq_lo = [(sq_s - hi_s.astype(F32)).astype(BF16) for sq_s, hi_s in zip(sq, sq_hi)]
    ss = []
    for hi_s, lo_s in zip(sq_hi, sq_lo):
        blocks = []
        for j in range(C_NORM_COLS // NORM_BLOCK):
            sl = slice(j * NORM_BLOCK, (j + 1) * NORM_BLOCK)
            blocks.append(jnp.dot(hi_s[:, sl], ones_ref[...], preferred_element_type=F32)
                          + jnp.dot(lo_s[:, sl], ones_ref[...], preferred_element_type=F32))
        ss.append(jnp.concatenate(blocks, axis=1))
    qk = [qk_s * lax.rsqrt(ss_s * (1.0 / HD) + EPS) * g_ref[...] for qk_s, ss_s in zip(qk, ss)]
    p_tiles = N_PROMPT // x_ref.shape[0]

    @pl.when(t < p_tiles)
    def _():
        for s, qk_s, z_s in zip(subs, qk, z):
            q_ref[s, :] = qk_s[:, :H_C * HD].astype(BF16)
            k_p_ref[s, :] = qk_s[:, H_C * HD:]
            v_p_ref[s, :] = z_s[:, C_NORM_COLS:]

    @pl.when(t >= p_tiles)
    def _():
        for s, qk_s, z_s in zip(subs, qk, z):
            q_ref[s, :] = _rope(qk_s[:, :H_C * HD], c64_ref[s, :], s64_ref[s, :], HD // 2).astype(BF16)
            k_s_ref[s, :] = _rope(qk_s[:, H_C * HD:], c64_ref[s, :], s64_ref[s, :], HD // 2)
            v_s_ref[s, :] = z_s[:, C_NORM_COLS:]


def _inproj_c(x, mod, w_in, g_full, tabs):
    def tok(w):
        return pl.BlockSpec((TP, w), lambda t: (t, 0))
    kv_w = KVH_C * HD
    head = np.arange(NORM_BLOCK) // HD
    ones = jnp.asarray(head[:, None] == head[None, :], BF16)
    return pl.pallas_call(
        _inproj_c_kernel,
        grid=(N_TOK // TP,),
        in_specs=[tok(D_MODEL), _mod_spec(1, 0, TP), _mod_spec(1, 1, TP),
                  _const_spec((D_MODEL, C_NORM_COLS + KVH_C * HD)),
                  _const_spec((1, C_NORM_COLS)), _const_spec((NORM_BLOCK, NORM_BLOCK))] + _rope_specs(2, TP),
        out_specs=[tok(H_C * HD), _group_spec(TP, kv_w, True), _group_spec(TP, kv_w, True),
                   _group_spec(TP, kv_w, False), _group_spec(TP, kv_w, False)],
        out_shape=[jax.ShapeDtypeStruct((N_TOK, H_C * HD), BF16),
                   jax.ShapeDtypeStruct((N_PROMPT, kv_w), F32), jax.ShapeDtypeStruct((N_PROMPT, kv_w), F32),
                   jax.ShapeDtypeStruct((N_SAMPLE, kv_w), F32), jax.ShapeDtypeStruct((N_SAMPLE, kv_w), F32)],
        compiler_params=_params(("arbitrary",)),
        name="inproj_c",
    )(x, mod, mod, w_in, g_full, ones, *tabs[:2])


def _qk(q, k):
    return lax.dot_general(q, k, (((1,), (1,)), ((), ())), preferred_element_type=F32)


def _softmax_pv(scores, values, scale, sink=None, kv_t=False):
    c = scale * LOG2E
    m = scores[0].max(axis=-1, keepdims=True)
    for s in scores[1:]:
        m = jnp.maximum(m, s.max(axis=-1, keepdims=True))
    m = m * c
    if sink is not None:
        sink = sink * LOG2E
        m = jnp.maximum(m, sink)
    den = None
    out = None
    for s, v in zip(scores, values):
        p = jnp.exp2(s * c - m)
        d = p.sum(axis=-1, keepdims=True)
        p = p.astype(BF16)
        if isinstance(v, (list, tuple)):
            rows = p.shape[0] // len(v)
            o = jnp.concatenate([_qk(p[b * rows:(b + 1) * rows], v_b) if kv_t
                                 else jnp.dot(p[b * rows:(b + 1) * rows], v_b, preferred_element_type=F32)
                                 for b, v_b in enumerate(v)], axis=0)
        else:
            o = _qk(p, v) if kv_t else jnp.dot(p, v, preferred_element_type=F32)
        den = d if den is None else den + d
        out = o if out is None else out + o
    if sink is not None:
        den = den + jnp.exp2(sink - m)
    return out * (1.0 / den)


def _mla_heads(qa_ref, ka_refs, va_refs, o_ref, stack):
    scale = (NOPE_A + ROPE_A) ** -0.5
    rows = qa_ref.shape[0]
    for h0 in range(0, H_A, stack):
        heads = range(h0, h0 + stack)
        scores = [jnp.concatenate([_qk(qa_ref[:, h * LANES:(h + 1) * LANES], ka_ref[:, h * LANES:(h + 1) * LANES])
                                   for h in heads], axis=0) for ka_ref in ka_refs]
        values = [[va_ref[:, h * V_A:(h + 1) * V_A] for h in heads] for va_ref in va_refs]
        o = _softmax_pv(scores, values, scale)
        o = jnp.concatenate([o[b * rows:(b + 1) * rows] for b in range(stack)], axis=1)
        o_ref[:, h0 * V_A:(h0 + stack) * V_A] = o.astype(o_ref.dtype)


def _gqa_groups(q_ref, col0, n_q, ks, vs, mask=None, sink=None, kv_t=False):
    rows = q_ref.shape[0]
    n_groups = len(ks)
    n_seg = len(ks[0])
    per_group = []
    for g in range(n_groups):
        base = col0 + g * n_q * HD
        q = jnp.concatenate([q_ref[:, base + i * HD:base + (i + 1) * HD] for i in range(n_q)], axis=0)
        if kv_t:
            per_group.append([jnp.dot(q, k, preferred_element_type=F32) for k in ks[g]])
        else:
            per_group.append([_qk(q, k) for k in ks[g]])
    scores = [jnp.concatenate([per_group[g][j] for g in range(n_groups)], axis=0) for j in range(n_seg)]
    values = [[vs[g][j] for g in range(n_groups)] for j in range(n_seg)]
    if mask is not None:
        scores[0] = jnp.where(mask, scores[0], NEG_INF)
    o = _softmax_pv(scores, values, HD ** -0.5, sink, kv_t)
    return jnp.concatenate([o[i * rows:(i + 1) * rows] for i in range(n_groups * n_q)], axis=1)


def _stacked_sink(sink_ref, h0, n_q, rows):
    head = lax.broadcasted_iota(jnp.int32, (n_q * rows, 1), 0) // rows
    col = jnp.full((n_q * rows, 1), sink_ref[h0], F32)
    for g in range(1, n_q):
        col = jnp.where(head == g, sink_ref[h0 + g], col)
    return col


def _attn_ab_prompt_kernel(sink_ref, qa_ref, ka_ref, va_ref, qb_ref, kbt_ref, vbt_ref, o_ref):
    _mla_heads(qa_ref, [ka_ref], [va_ref], o_ref, MLA_STACK_PROMPT)
    kbt = kbt_ref[...].astype(BF16)
    vbt = vbt_ref[...].astype(BF16)
    sls = [slice(kh * HD, (kh + 1) * HD) for kh in range(KVH_B)]
    o = _gqa_groups(qb_ref, 0, H_B // KVH_B, [[kbt[sl, :]] for sl in sls], [[vbt[sl, :]] for sl in sls],
                    sink=_stacked_sink(sink_ref, 0, H_B, SEQ), kv_t=True)
    o_ref[:, H_A * V_A:] = o.astype(o_ref.dtype)


def _attn_ab_prompt(sink, qa, ka, va, qb, kbt, vbt):
    def blk(w):
        return pl.BlockSpec((SEQ, w), lambda b: (b, 0))
    tblk = pl.BlockSpec((None, KVH_B * HD, SEQ), lambda b: (b, 0, 0))
    return pl.pallas_call(
        _attn_ab_prompt_kernel,
        grid=(BATCH,),
        in_specs=[pl.BlockSpec(memory_space=pltpu.SMEM), blk(H_A * LANES), blk(H_A * LANES), blk(H_A * V_A),
                  blk(H_B * HD), tblk, tblk],
        out_specs=blk(D_MODEL),
        out_shape=jax.ShapeDtypeStruct((N_PROMPT, H_A * V_A + H_B * HD), BF16),
        compiler_params=_params(("arbitrary",)),
        name="attn_ab_prompt",
    )(sink, qa, ka, va, qb, kbt, vbt)


def _attn_ab_sample_kernel(sink_ref, qa_ref, ka_ref, va_ref, qb_ref, kb_ref, vb_ref,
                           ckv_ctx_ref, kpe_ctx_ref, kb_ctx_ref, vb_ctx_ref, w_kn_ref, w_v_ref, place_ref,
                           o_ref, ka_ctx_ref, va_ctx_ref):
    i = pl.program_id(1)

    @pl.when(i == 0)
    def _():
        ckv16 = ckv_ctx_ref[...].astype(BF16)
        k_nope = jnp.dot(ckv16, w_kn_ref[...], preferred_element_type=F32)
        k_rope = jnp.dot(kpe_ctx_ref[...].astype(BF16), place_ref[...], preferred_element_type=F32)
        ka_ctx_ref[...] = (k_nope + k_rope).astype(BF16)
        va_ctx_ref[...] = jnp.dot(ckv16, w_v_ref[...], preferred_element_type=F32).astype(BF16)

    _mla_heads(qa_ref, [ka_ref, ka_ctx_ref], [va_ref, va_ctx_ref], o_ref, MLA_STACK_SAMPLE)

    n_win = BQ_S + 2 * WINDOW
    start = pl.multiple_of(jnp.clip(i * BQ_S - WINDOW, 0, DEC_SEQ - n_win), WINDOW)
    row = lax.broadcasted_iota(jnp.int32, (H_B * BQ_S, n_win), 0)
    qpos = i * BQ_S + (row & (BQ_S - 1))
    kpos = start + lax.broadcasted_iota(jnp.int32, (H_B * BQ_S, n_win), 1)
    in_band = jnp.abs(qpos - kpos) <= WINDOW
    kwin = kb_ref[pl.ds(start, n_win), :].astype(BF16)
    vwin = vb_ref[pl.ds(start, n_win), :].astype(BF16)
    kctx = kb_ctx_ref[...].astype(BF16)
    vctx = vb_ctx_ref[...].astype(BF16)
    sls = [slice(kh * HD, (kh + 1) * HD) for kh in range(KVH_B)]
    o = _gqa_groups(qb_ref, 0, H_B // KVH_B, [[kwin[:, sl], kctx[:, sl]] for sl in sls],
                    [[vwin[:, sl], vctx[:, sl]] for sl in sls],
                    mask=in_band, sink=_stacked_sink(sink_ref, 0, H_B, BQ_S))
    o_ref[:, H_A * V_A:] = o.astype(o_ref.dtype)


def _attn_ab_sample(sink, qa, ka, va, qb, kb, vb, ckv_ctx, kpe_ctx, kb_ctx, vb_ctx, w_kn, w_v, place):
    qoff = N_PROMPT // BQ_S
    nq = DEC_SEQ // BQ_S
    boff = N_PROMPT // DEC_SEQ

    def qblk(w):
        return pl.BlockSpec((BQ_S, w), lambda b, i: (qoff + b * nq + i, 0))

    def bblk(w):
        return pl.BlockSpec((DEC_SEQ, w), lambda b, i: (boff + b, 0))

    def sblk(w):
        return pl.BlockSpec((DEC_SEQ, w), lambda b, i: (b, 0))

    def cblk(w):
        return pl.BlockSpec((PAST_LEN, w), lambda b, i: (b, 0))

    return pl.pallas_call(
        _attn_ab_sample_kernel,
        grid=(DEC_BATCH, nq),
        in_specs=[pl.BlockSpec(memory_space=pltpu.SMEM), qblk(H_A * LANES), bblk(H_A * LANES), bblk(H_A * V_A),
                  qblk(H_B * HD), sblk(KVH_B * HD), sblk(KVH_B * HD),
                  cblk(KV_LORA), cblk(LANES), cblk(KVH_B * HD), cblk(KVH_B * HD),
                  _const_spec((KV_LORA, H_A * LANES)), _const_spec((KV_LORA, H_A * V_A)),
                  _const_spec((LANES, H_A * LANES))],
        out_specs=pl.BlockSpec((BQ_S, D_MODEL), lambda b, i: (b * nq + i, 0)),
        out_shape=jax.ShapeDtypeStruct((N_SAMPLE, H_A * V_A + H_B * HD), BF16),
        scratch_shapes=[pltpu.VMEM((PAST_LEN, H_A * LANES), BF16), pltpu.VMEM((PAST_LEN, H_A * V_A), BF16)],
        compiler_params=_params(("arbitrary", "arbitrary")),
        name="attn_ab_sample",
    )(sink, qa, ka, va, qb, kb, vb, ckv_ctx, kpe_ctx, kb_ctx, vb_ctx, w_kn, w_v, place)


def _gqa_heads(q_ref, k_list, v_list, o_ref, stack):
    gq = H_C // KVH_C
    for kh0 in range(0, KVH_C, stack):
        sls = [slice(kh * HD, (kh + 1) * HD) for kh in range(kh0, kh0 + stack)]
        o = _gqa_groups(q_ref, kh0 * gq * HD, gq, [[k[:, sl] for k in k_list] for sl in sls],
                        [[v[:, sl] for v in v_list] for sl in sls])
        o_ref[:, kh0 * gq * HD:(kh0 + stack) * gq * HD] = o.astype(o_ref.dtype)


def _attn_c_prompt_kernel(q_ref, k_ref, v_ref, o_ref):
    _gqa_heads(q_ref, [k_ref[...].astype(BF16)], [v_ref[...].astype(BF16)], o_ref, GQA_STACK_PROMPT)


def _attn_c_prompt(q, k, v):
    def blk(w):
        return pl.BlockSpec((SEQ, w), lambda b: (b, 0))
    return pl.pallas_call(
        _attn_c_prompt_kernel,
        grid=(BATCH,),
        in_specs=[blk(H_C * HD), blk(KVH_C * HD), blk(KVH_C * HD)],
        out_specs=blk(H_C * HD),
        out_shape=jax.ShapeDtypeStruct((N_PROMPT, H_C * HD), BF16),
        compiler_params=_params(("arbitrary",)),
        name="attn_c_prompt",
    )(q, k, v)


def _attn_c_sample_kernel(q_ref, k_ref, v_ref, kc_ref, vc_ref, o_ref):
    _gqa_heads(q_ref, [k_ref[...].astype(BF16), kc_ref[...].astype(BF16)],
               [v_ref[...].astype(BF16), vc_ref[...].astype(BF16)], o_ref, GQA_STACK_SAMPLE)


def _attn_c_sample(q, k, v, k_ctx, v_ctx):
    qoff = N_PROMPT // BQ_S
    nq = DEC_SEQ // BQ_S
    w = KVH_C * HD
    return pl.pallas_call(
        _attn_c_sample_kernel,
        grid=(DEC_BATCH, nq),
        in_specs=[pl.BlockSpec((BQ_S, H_C * HD), lambda b, i: (qoff + b * nq + i, 0)),
                  pl.BlockSpec((DEC_SEQ, w), lambda b, i: (b, 0)),
                  pl.BlockSpec((DEC_SEQ, w), lambda b, i: (b, 0)),
                  pl.BlockSpec((PAST_LEN, w), lambda b, i: (b, 0)),
                  pl.BlockSpec((PAST_LEN, w), lambda b, i: (b, 0))],
        out_specs=pl.BlockSpec((BQ_S, H_C * HD), lambda b, i: (b * nq + i, 0)),
        out_shape=jax.ShapeDtypeStruct((N_SAMPLE, H_C * HD), BF16),
        compiler_params=_params(("arbitrary", "arbitrary")),
        name="attn_c_sample",
    )(q, k, v, k_ctx, v_ctx)


def _layer_norm(y, g, b):
    mu = jnp.mean(y, axis=-1, keepdims=True)
    yc = y - mu
    var = jnp.mean(yc * yc, axis=-1, keepdims=True)
    return yc * lax.rsqrt(var + EPS) * g + b


def _route(sel, aff):
    def row(a, j):
        return a[j:j + 1, :]

    scores = []
    for g in range(N_GROUPS):
        a0, a1, a2, a3 = (row(sel, 4 * g + j) for j in range(4))
        hi01, lo01 = jnp.maximum(a0, a1), jnp.minimum(a0, a1)
        hi23, lo23 = jnp.maximum(a2, a3), jnp.minimum(a2, a3)
        top1 = jnp.maximum(hi01, hi23)
        top2 = jnp.maximum(jnp.minimum(hi01, hi23), jnp.maximum(lo01, lo23))
        scores.append(top1 + top2)
    best = scores[0]
    gi = jnp.zeros(best.shape, jnp.int32)
    for g in range(1, N_GROUPS):
        better = scores[g] > best
        gi = jnp.where(better, g, gi)
        best = jnp.where(better, scores[g], best)

    def in_group(a, j):
        out = row(a, j)
        for g in range(1, N_GROUPS):
            out = jnp.where(gi == g, row(a, 4 * g + j), out)
        return out

    v = [in_group(sel, j) for j in range(4)]
    a = [in_group(aff, j) for j in range(4)]
    chosen = []
    for j in range(4):
        rank = jnp.zeros(best.shape, jnp.int32)
        for k in range(4):
            if k == j:
                continue
            ahead = (v[k] >= v[j]) if k < j else (v[k] > v[j])
            rank = rank + ahead.astype(jnp.int32)
        chosen.append(rank < 2)
    total = sum(jnp.where(chosen[j], a[j], 0.0) for j in range(4))
    w = [jnp.where(chosen[j], a[j], 0.0) / total for j in range(4)]
    pair = jnp.zeros(best.shape, jnp.int32)
    for p, (ja, jb) in enumerate(zip(SLOT_A_LOCAL, SLOT_B_LOCAL)):
        pair = jnp.where(chosen[ja] & chosen[jb], p, pair)
    gate_a = jnp.where(pair == 0, w[0], jnp.where(pair <= 2, w[2], w[3]))
    gate_b = jnp.where((pair == 0) | (pair == 1) | (pair == 4), w[1], jnp.where(pair == 5, w[2], w[0]))
    return gi * PAIRS_PER_GROUP + pair, gate_a, gate_b


def _post_attn_kernel(split_x, *refs):
    t = pl.program_id(0)
    op_ref, os_ref = refs[:2]
    if split_x:
        x = _pick(t, refs[2], refs[3])
        refs = refs[4:]
    else:
        x = refs[2][...]
        refs = refs[3:]
    (w_out_ref, gate_ref, shift_ref, scale_ref, lng_ref, lnb_ref, rw_ref, rb_ref, tri_ref,
     x1_ref, row_ref, route_ref, counts_ref, carry_ref) = refs
    o = _pick(t, op_ref, os_ref)
    subs = _sub_tiles(o.shape[0])

    a = [jnp.dot(o[s], w_out_ref[...], preferred_element_type=F32) for s in subs]
    x1 = [_layer_norm(ALPHA * x[s] + gate_ref[...] * a_s, lng_ref[...], lnb_ref[...]) for s, a_s in zip(subs, a)]
    h2 = [x1_s * (1.0 + scale_ref[...]) + shift_ref[...] for x1_s in x1]
    for s, x1_s, h2_s in zip(subs, x1, h2):
        x1_ref[s, :] = x1_s
        row_ref[s, :D_MODEL] = h2_s
    rw = rw_ref[...]
    rw_hi = rw.astype(BF16)
    rw_lo = (rw - rw_hi.astype(F32)).astype(BF16)
    rw_both = jnp.concatenate([rw_hi, rw_lo], axis=0)
    logits = []
    for h2_s in h2:
        h2_hi = h2_s.astype(BF16)
        h2_lo = (h2_s - h2_hi.astype(F32)).astype(BF16)
        by_hi = _qk(rw_both, h2_hi)
        logits.append(by_hi[:N_EXPERTS] + by_hi[N_EXPERTS:] + _qk(rw_hi, h2_lo))
    aff = [1.0 / (1.0 + jnp.exp(-l)) for l in logits]
    routed = [_route(aff_s + rb_ref[...], aff_s) for aff_s in aff]

    lane = lax.broadcasted_iota(jnp.int32, (LANES, TM), 0)
    for s, (_, gate_a, gate_b) in zip(subs, routed):
        meta_t = jnp.where(lane == 0, gate_a, jnp.where(lane == 1, gate_b, 0.0))
        row_ref[s, D_MODEL:] = meta_t.T

    @pl.when(t == 0)
    def _():
        carry_ref[...] = jnp.zeros_like(carry_ref)

    onehots = [lax.broadcasted_iota(jnp.int32, (BUCKET_ROWS, TM), 0) == bucket for bucket, _, _ in routed]
    prefixes = [jnp.dot(oh.astype(BF16), tri_ref[...], preferred_element_type=F32) for oh in onehots]
    carry = carry_ref[...]
    route_ref[...] = jnp.zeros(route_ref.shape, F32)
    for s, (bucket, _, _), onehot, prefix in zip(subs, routed, onehots, prefixes):
        rank = jnp.sum(jnp.where(onehot, prefix + carry[:, 0:1], 0.0), axis=0, keepdims=True)
        carry = carry + jnp.sum(onehot.astype(F32), axis=1, keepdims=True)
        route_ref[0:1, s] = bucket.astype(F32)
        route_ref[1:2, s] = rank
    carry_ref[...] = carry
    counts_ref[...] = carry


def _post_attn(layer, o_p, o_s, x_list, mod, w_out, ln_g, ln_b, rw_t, rb):
    split_x = len(x_list) == 2
    tok = pl.BlockSpec((TP, D_MODEL), lambda t: (t, 0))
    tri = jnp.asarray(np.arange(TM)[:, None] < np.arange(TM)[None, :], BF16)
    return pl.pallas_call(
        functools.partial(_post_attn_kernel, split_x),
        grid=(N_TOK // TP,),
        in_specs=_split_specs(TP) + (_split_specs(TP) if split_x else [tok]) + [
            _const_spec((D_MODEL, D_MODEL)), _mod_spec(layer, 2, TP), _mod_spec(layer, 3, TP),
            _mod_spec(layer, 4, TP),
            _const_spec((1, D_MODEL)), _const_spec((1, D_MODEL)), _const_spec((N_EXPERTS, D_MODEL)),
            _const_spec((N_EXPERTS, 1)), _const_spec((TM, TM))],
        out_specs=[tok, pl.BlockSpec((TP, ROW_W), lambda t: (t, 0)), pl.BlockSpec((8, TP), lambda t: (0, t)),
                   _const_spec((BUCKET_ROWS, LANES))],
        out_shape=[jax.ShapeDtypeStruct((N_TOK, D_MODEL), F32), jax.ShapeDtypeStruct((N_TOK, ROW_W), F32),
                   jax.ShapeDtypeStruct((8, N_TOK), F32), jax.ShapeDtypeStruct((BUCKET_ROWS, LANES), F32)],
        scratch_shapes=[pltpu.VMEM((BUCKET_ROWS, LANES), F32)],
        compiler_params=_params(("arbitrary",)),
        name="post_attn",
    )(o_p, o_s, *x_list, w_out, mod, mod, mod, ln_g, ln_b, rw_t, rb, tri)


def _moe_kernel(ia_ref, ib_ref, itile_ref, isize_ref, ni_ref, nt_ref, x_hbm, wga_ref, wua_ref, wda_ref, wgb_ref,
                wub_ref, wdb_ref, o_hbm, xbuf, obuf, xsem, osem, wga_s, wua_s, wda_s, wgb_s, wub_s, wdb_s):
    i = pl.program_id(0)
    prev = jnp.maximum(i - 1, 0)
    n_items = ni_ref[0]
    n_tiles = nt_ref[0]
    slots = ((ia_ref, (wga_ref, wua_ref, wda_ref), (wga_s, wua_s, wda_s)),
             (ib_ref, (wgb_ref, wub_ref, wdb_ref), (wgb_s, wub_s, wdb_s)))

    def fetch(k):
        row0 = pl.multiple_of(itile_ref[k] * TME, TME)
        return pltpu.make_async_copy(x_hbm.at[pl.ds(row0, 2 * TME), :], xbuf.at[k % 2], xsem.at[k % 2])

    def store(k, tiles):
        row0 = pl.multiple_of(itile_ref[k] * TME, TME)
        return pltpu.make_async_copy(obuf.at[k % 2, pl.ds(0, tiles * TME), :],
                                     o_hbm.at[pl.ds(row0, tiles * TME), :], osem.at[k % 2])

    def wait_store(k):
        for tiles in (1, 2):
            @pl.when(isize_ref[k] == tiles)
            def _(tiles=tiles):
                store(k, tiles).wait()

    @pl.when((i == 0) & (n_items > 0))
    def _():
        fetch(0).start()

    for t_ref, w_refs, w_scr in slots:
        @pl.when((i == 0) | (t_ref[i] != t_ref[prev]))
        def _(w_refs=w_refs, w_scr=w_scr):
            for w_ref, s_ref in zip(w_refs, w_scr):
                s_ref[...] = w_ref[...].astype(BF16)

    @pl.when(i < n_items)
    def _():
        fetch(i).wait()

        @pl.when(i + 1 < n_items)
        def _():
            fetch(i + 1).start()

        @pl.when(i >= 2)
        def _():
            wait_store(i - 2)

        for tiles in (1, 2):
            @pl.when(isize_ref[i] == tiles)
            def _(tiles=tiles):
                rows = tiles * TME
                x_ref = xbuf.at[i % 2, pl.ds(0, rows), :]
                x = x_ref[:, :D_MODEL].astype(BF16)
                acc = None
                for slot, (_, _, (wg_s, wu_s, wd_s)) in enumerate(slots):
                    gt = jnp.dot(x, wg_s[...], preferred_element_type=F32)
                    up = jnp.dot(x, wu_s[...], preferred_element_type=F32)
                    gate = x_ref[:, D_MODEL + slot:D_MODEL + slot + 1]
                    hid = (gt / (1.0 + jnp.exp(-gt))) * up * gate
                    y = jnp.dot(hid.astype(BF16), wd_s[...], preferred_element_type=F32)
                    acc = y if acc is None else acc + y
                obuf[i % 2, pl.ds(0, rows), :] = acc
                store(i, tiles).start()

    @pl.when(i == MOE_ITEMS - 1)
    def _():
        @pl.when(n_items >= 2)
        def _():
            wait_store(n_items - 2)

        @pl.when(n_items >= 1)
        def _():
            wait_store(n_items - 1)

        obuf[0] = jnp.zeros(obuf.shape[1:], F32)

        def unused(g, carry):
            cp = pltpu.make_async_copy(obuf.at[0, pl.ds(0, TME), :],
                                       o_hbm.at[pl.ds(pl.multiple_of(g * TME, TME), TME), :], osem.at[0])
            cp.start()
            cp.wait()
            return carry
        lax.fori_loop(n_tiles, MOE_TILES, unused, 0)


def _moe(layer, item_a, item_b, item_tile, item_size, n_items, n_tiles, rows, w_gate, w_up, w_down):
    def wspec(shape, which):
        if which == 0:
            return pl.BlockSpec((None, None) + shape, lambda i, ia, ib, it, sz, ni, nt: (layer, ia[i], 0, 0))
        return pl.BlockSpec((None, None) + shape, lambda i, ia, ib, it, sz, ni, nt: (layer, ib[i], 0, 0))

    up_shape, down_shape = (D_MODEL, D_EXPERT), (D_EXPERT, D_MODEL)
    grid_spec = pltpu.PrefetchScalarGridSpec(
        num_scalar_prefetch=6,
        grid=(MOE_ITEMS,),
        in_specs=[pl.BlockSpec(memory_space=pl.ANY),
                  wspec(up_shape, 0), wspec(up_shape, 0), wspec(down_shape, 0),
                  wspec(up_shape, 1), wspec(up_shape, 1), wspec(down_shape, 1)],
        out_specs=pl.BlockSpec(memory_space=pl.ANY),
        scratch_shapes=[pltpu.VMEM((2, 2 * TME, ROW_W), F32), pltpu.VMEM((2, 2 * TME, D_MODEL), F32),
                        pltpu.SemaphoreType.DMA((2,)), pltpu.SemaphoreType.DMA((2,))]
        + [pltpu.VMEM(s, BF16) for s in (up_shape, up_shape, down_shape) * 2],
    )
    return pl.pallas_call(
        _moe_kernel,
        grid_spec=grid_spec,
        out_shape=jax.ShapeDtypeStruct((MOE_ROWS, D_MODEL), F32),
        compiler_params=_params(("arbitrary",)),
        name="moe_experts",
    )(item_a, item_b, item_tile, item_size, n_items, n_tiles, rows, w_gate, w_up, w_down, w_gate, w_up, w_down)


def _row_copy(src_ref, src_row, dst_ref, dst_row, sem):
    return pltpu.make_async_copy(src_ref.at[pl.ds(src_row, 1), :], dst_ref.at[pl.ds(dst_row, 1), :], sem)


def _dispatch_kernel(pos_ref, pend_ref, cnt_ref, nt_ref, src_ref, out_ref, buf, zero_ref, in_sem, sem, zsem):
    t = pl.program_id(0)

    def fetch(tile):
        return pltpu.make_async_copy(src_ref.at[pl.ds(pl.multiple_of(tile * TM, TM), TM), :],
                                     buf.at[tile % STAGE_SLOTS], in_sem.at[tile % STAGE_SLOTS])

    @pl.when(t == 0)
    def _():
        fetch(0).start()
        zero_ref[...] = jnp.zeros_like(zero_ref)

        def zero_tile(row0):
            return pltpu.make_async_copy(zero_ref, out_ref.at[pl.ds(pl.multiple_of(row0, TME), TME), :], zsem)

        for b in range(N_BUCKETS):
            @pl.when(cnt_ref[b] > 0)
            def _(b=b):
                zero_tile(pend_ref[b] - TME).start()

        def start_unused(i, carry):
            zero_tile(i * TME).start()
            return carry
        lax.fori_loop(nt_ref[0], MOE_TILES + 1, start_unused, 0)

        for b in range(N_BUCKETS):
            @pl.when(cnt_ref[b] > 0)
            def _(b=b):
                zero_tile(pend_ref[b] - TME).wait()

        def wait_unused(i, carry):
            zero_tile(i * TME).wait()
            return carry
        lax.fori_loop(nt_ref[0], MOE_TILES + 1, wait_unused, 0)

    @pl.when(t + 1 < N_TILES)
    def _():
        fetch(t + 1).start()

    fetch(t).wait()
    tile_ref = buf.at[t % STAGE_SLOTS]

    for r in range(TM):
        _row_copy(tile_ref, r, out_ref, pos_ref[t * TM + r], sem.at[t % 2]).start(priority=r % 2)

    def drain(tile):
        pltpu.make_async_copy(buf.at[0], out_ref.at[pl.ds(0, TM), :], sem.at[tile % 2]).wait()

    @pl.when(t > 0)
    def _():
        drain(t - 1)

    @pl.when(t == N_TILES - 1)
    def _():
        drain(t)


def _dispatch(pos, pend, counts, n_tiles, rows):
    return pl.pallas_call(
        _dispatch_kernel,
        grid_spec=pltpu.PrefetchScalarGridSpec(
            num_scalar_prefetch=4, grid=(N_TILES,),
            in_specs=[pl.BlockSpec(memory_space=pl.ANY)],
            out_specs=pl.BlockSpec(memory_space=pl.ANY),
            scratch_shapes=[pltpu.VMEM((STAGE_SLOTS, TM, ROW_W), F32), pltpu.VMEM((TME, ROW_W), F32),
                            pltpu.SemaphoreType.DMA((STAGE_SLOTS,)), pltpu.SemaphoreType.DMA((2,)),
                            pltpu.SemaphoreType.DMA]),
        out_shape=jax.ShapeDtypeStruct((MOE_ROWS + TME, ROW_W), F32),
        compiler_params=_params(("arbitrary",)),
        name="dispatch_rows",
    )(pos, pend, counts, n_tiles, rows)


def _plan(route, counts):
    bucket = route[0].astype(jnp.int32)
    rank = route[1].astype(jnp.int32)
    counts = counts[:N_BUCKETS, 0].astype(jnp.int32)
    padded = ((counts + TME - 1) // TME) * TME
    pend = jnp.cumsum(padded)
    pstart = pend - padded
    ids = jnp.arange(N_BUCKETS, dtype=jnp.int32)
    pos = rank + jnp.sum(jnp.where(bucket[None, :] == ids[:, None], pstart[:, None], 0), axis=0)
    n_tiles = pend[-1] // TME
    tiles = padded // TME
    items = (tiles + 1) // 2
    item_end = jnp.cumsum(items)
    item_start = item_end - items
    n_items = item_end[-1]
    step = jnp.arange(MOE_ITEMS, dtype=jnp.int32)
    live = step < n_items
    item = jnp.minimum(step, jnp.maximum(n_items - 1, 0))
    bkt = jnp.minimum(jnp.sum((item[:, None] >= item_end[None, :]).astype(jnp.int32), axis=1), N_BUCKETS - 1)
    hot = bkt[:, None] == ids[None, :]

    def of_bucket(v):
        return jnp.sum(jnp.where(hot, v[None, :], 0), axis=1)

    k = item - of_bucket(item_start)
    item_tile = of_bucket(pstart // TME) + 2 * k
    item_size = jnp.where(live, jnp.minimum(2, of_bucket(tiles) - 2 * k), 0)
    group, pair = bkt // PAIRS_PER_GROUP, bkt % PAIRS_PER_GROUP
    slot_a = jnp.asarray(SLOT_A_LOCAL, jnp.int32)
    slot_b = jnp.asarray(SLOT_B_LOCAL, jnp.int32)
    pair_hot = pair[:, None] == jnp.arange(PAIRS_PER_GROUP, dtype=jnp.int32)[None, :]
    item_a = group * EXPERTS_PER_GROUP + jnp.sum(jnp.where(pair_hot, slot_a[None, :], 0), axis=1)
    item_b = group * EXPERTS_PER_GROUP + jnp.sum(jnp.where(pair_hot, slot_b[None, :], 0), axis=1)
    return (pos, pend, counts, n_tiles.reshape(1), item_a.astype(jnp.int32), item_b.astype(jnp.int32),
            item_tile.astype(jnp.int32), item_size.astype(jnp.int32), n_items.astype(jnp.int32).reshape(1))


def _post_moe_kernel(split_out, pos_ref, f_ref, x1_ref, gate_ref, lng_ref, lnb_ref, *refs):
    out_refs, (fbuf, sem) = refs[:-2], refs[-2:]
    t = pl.program_id(0)
    tq = x1_ref.shape[0]
    n_steps = N_TOK // tq
    p_steps = N_PROMPT // tq

    def wait_tile(slot):
        pltpu.make_async_copy(f_ref.at[pl.ds(0, tq), :], fbuf.at[slot], sem.at[slot]).wait()

    @pl.when(t == 0)
    def _():
        def issue(r, carry):
            _row_copy(f_ref, pos_ref[r], fbuf.at[r // tq], r % tq, sem.at[r // tq]).start()
            return carry
        lax.fori_loop(0, (GATHER_SLOTS - 1) * tq, issue, 0, unroll=8)

    slot = t % GATHER_SLOTS
    wait_tile(slot)
    ahead = jnp.minimum(t + GATHER_SLOTS - 1, n_steps - 1)
    aslot = (t + GATHER_SLOTS - 1) % GATHER_SLOTS
    for r in range(tq):
        _row_copy(f_ref, pos_ref[ahead * tq + r], fbuf.at[aslot], r, sem.at[aslot]).start(priority=r % 2)
    subs = _sub_tiles(tq)
    y = [_layer_norm(ALPHA * x1_ref[s, :] + gate_ref[...] * fbuf[slot, s, :], lng_ref[...], lnb_ref[...])
         for s in subs]

    @pl.when(t == n_steps - 1)
    def _():
        for k in range(1, GATHER_SLOTS):
            wait_tile((t + k) % GATHER_SLOTS)
    if split_out:
        @pl.when(t < p_steps)
        def _():
            for s, y_s in zip(subs, y):
                out_refs[0][s, :] = y_s

        @pl.when(t >= p_steps)
        def _():
            for s, y_s in zip(subs, y):
                out_refs[1][s, :] = y_s
    else:
        for s, y_s in zip(subs, y):
            out_refs[0][s, :] = y_s


def _post_moe(layer, pos, f_sorted, x1, mod, ln_g, ln_b, split_out):
    tq = TQ
    p_steps = N_PROMPT // tq
    tok = pl.BlockSpec((tq, D_MODEL), lambda t, p: (t, 0))
    if split_out:
        out_specs = [pl.BlockSpec((tq, D_MODEL), lambda t, p: (jnp.minimum(t, p_steps - 1), 0)),
                     pl.BlockSpec((tq, D_MODEL), lambda t, p: (jnp.maximum(t - p_steps, 0), 0))]
        out_shape = [jax.ShapeDtypeStruct((N_PROMPT, D_MODEL), F32), jax.ShapeDtypeStruct((N_SAMPLE, D_MODEL), F32)]
    else:
        out_specs = [tok]
        out_shape = [jax.ShapeDtypeStruct((N_TOK, D_MODEL), F32)]
    mod_spec = pl.BlockSpec((None, None, None, 1, D_MODEL), lambda t, p: (layer, _mod_row(t, tq), 5, 0, 0))
    return pl.pallas_call(
        functools.partial(_post_moe_kernel, split_out),
        grid_spec=pltpu.PrefetchScalarGridSpec(
            num_scalar_prefetch=1, grid=(N_TOK // tq,),
            in_specs=[pl.BlockSpec(memory_space=pl.ANY), tok, mod_spec,
                      pl.BlockSpec((1, D_MODEL), lambda t, p: (0, 0)), pl.BlockSpec((1, D_MODEL), lambda t, p: (0, 0))],
            out_specs=out_specs,
            scratch_shapes=[pltpu.VMEM((GATHER_SLOTS, tq, D_MODEL), F32),
                            pltpu.SemaphoreType.DMA((GATHER_SLOTS,))]),
        out_shape=out_shape,
        compiler_params=_params(("arbitrary",)),
        name="post_moe",
    )(pos, f_sorted, x1, mod, ln_g, ln_b)


def _ffn(layer, o_p, o_s, x_list, mod, w_out, ln_g, ln_b, rw_t, rb, w_gate, w_up, w_down, split_out):
    x1, rows, route, counts = _post_attn(layer, o_p, o_s, x_list, mod, w_out, ln_g[layer, 0][None],
                                         ln_b[layer, 0][None], rw_t, rb)
    pos, pend, counts, n_tiles, item_a, item_b, item_tile, item_size, n_items = _plan(route, counts)
    rows_sorted = _dispatch(pos, pend, counts, n_tiles, rows)
    f_sorted = _moe(layer, item_a, item_b, item_tile, item_size, n_items, n_tiles, rows_sorted, w_gate, w_up, w_down)
    return _post_moe(layer, pos, f_sorted, x1, mod, ln_g[layer, 1][None], ln_b[layer, 1][None], split_out)


def kernel(x_prompt, x_sample, c, cache_mla_ckv, cache_mla_kpe, cache_swa_k, cache_swa_v, cache_gqa_k, cache_gqa_v, c_ctx, w_mod, b_mod, ln_g, ln_b, w_in_ab, mla_q_norm, mla_w_uq, mla_kv_norm, mla_w_ukv, swa_sink, w_out_ab, w_in_c, gqa_q_norm, gqa_k_norm, w_out_c, router_w, router_bias, exp_w_gate, exp_w_up, exp_w_down):
    xp = x_prompt.reshape(N_PROMPT, D_MODEL)
    xs = x_sample.reshape(N_SAMPLE, D_MODEL)
    cond = jnp.concatenate([c_ctx[None], c, jnp.zeros((MOD_ROWS - 1 - DEC_BATCH, D_MODEL), F32)], axis=0)
    mod = _modulation(cond, w_mod, b_mod).reshape(DEPTH, MOD_ROWS, 6, 1, D_MODEL)
    tabs = _rope_tables()
    rw_t = router_w.T
    rb = router_bias.reshape(N_EXPERTS, 1)

    w = w_in_ab[0]
    pad = jnp.zeros((D_MODEL, AB_COLS - w.shape[1]), F32)
    kpe0 = Q_LORA + KV_LORA
    w_in = jnp.concatenate([w[:, :kpe0], w[:, kpe0 + ROPE_A:], w[:, kpe0:kpe0 + ROPE_A], pad], axis=1).astype(BF16)
    w_uq, w_kn, w_v, place = _mla_weights(mla_w_uq[0], mla_w_ukv[0])
    qa, ka, va, qb, ckv_p, kpe_p, kbt_p, vbt_p, kb_s, vb_s = _inproj_ab(
        xp, xs, mod, w_in, mla_q_norm[0][None], mla_kv_norm[0][None], w_uq, w_kn, w_v, place, tabs)
    sink = swa_sink[0]
    o_p = _attn_ab_prompt(sink, qa, ka, va, qb, kbt_p, vbt_p)
    kpe_ctx = jnp.pad(cache_mla_kpe[:, 0].reshape(-1, ROPE_A), ((0, 0), (0, LANES - ROPE_A)))
    o_s = _attn_ab_sample(sink, qa, ka, va, qb, kb_s, vb_s,
                          cache_mla_ckv[:, 0].reshape(-1, KV_LORA), kpe_ctx,
                          cache_swa_k[:, 0].reshape(-1, KVH_B * HD), cache_swa_v[:, 0].reshape(-1, KVH_B * HD),
                          w_kn, w_v, place)
    (x2,) = _ffn(0, o_p, o_s, [xp, xs], mod, w_out_ab[0].astype(BF16), ln_g, ln_b, rw_t, rb,
                 exp_w_gate, exp_w_up, exp_w_down, split_out=False)

    g_full = jnp.concatenate([jnp.tile(gqa_q_norm[0], H_C), jnp.tile(gqa_k_norm[0], KVH_C)])[None]
    qc, kc_p, vc_p, kc_s, vc_s = _inproj_c(x2, mod, w_in_c[0].astype(BF16), g_full, tabs)
    oc_p = _attn_c_prompt(qc, kc_p, vc_p)
    oc_s = _attn_c_sample(qc, kc_s, vc_s, cache_gqa_k[:, 0].reshape(-1, KVH_C * HD),
                          cache_gqa_v[:, 0].reshape(-1, KVH_C * HD))
    y_p, y_s = _ffn(1, oc_p, oc_s, [x2], mod, w_out_c[0].astype(BF16), ln_g, ln_b, rw_t, rb,
                    exp_w_gate, exp_w_up, exp_w_down, split_out=True)

    y_prompt = y_p.reshape(BATCH, SEQ, D_MODEL)
    y_sample = y_s.reshape(DEC_BATCH, DEC_SEQ, D_MODEL)
    new_ckv = ckv_p.reshape(BATCH, 1, SEQ, KV_LORA)
    new_kpe = kpe_p.reshape(BATCH, 1, SEQ, ROPE_A)
    new_swk = kbt_p.reshape(BATCH, 1, KVH_B, HD, SEQ).transpose(0, 1, 4, 2, 3)
    new_swv = vbt_p.reshape(BATCH, 1, KVH_B, HD, SEQ).transpose(0, 1, 4, 2, 3)
    new_gk = kc_p.reshape(BATCH, 1, SEQ, KVH_C, HD)
    new_gv = vc_p.reshape(BATCH, 1, SEQ, KVH_C, HD)
    return y_prompt, y_sample, new_ckv, new_kpe, new_swk, new_swv, new_gk, new_gv
```

```python
import functools

import numpy as np
import jax
import jax.numpy as jnp
from jax import lax
from jax.experimental import pallas as pl
from jax.experimental.pallas import tpu as pltpu

D_MODEL = 1024
BATCH = 32
SEQ = 256
DEPTH = 2
DEC_BATCH = 4
DEC_SEQ = 1024
PAST_LEN = 256
GRID_W = 64
ROPE_THETA = 10000.0
WINDOW = 128
HD = 64
H_A = 8
NOPE_A = 64
ROPE_A = 32
V_A = 64
Q_LORA = 256
KV_LORA = 128
H_B = 8
KVH_B = 2
H_C = 16
KVH_C = 4
N_EXPERTS = 16
N_GROUPS = 4
EXPERTS_PER_GROUP = 4
D_EXPERT = 512
ALPHA = (2 * DEPTH) ** 0.25
NEG_INF = -1e30
EPS = 1e-6

LANES = 128
N_PROMPT = BATCH * SEQ
N_SAMPLE = DEC_BATCH * DEC_SEQ
N_TOK = N_PROMPT + N_SAMPLE
TM = 256
TP = 1024
TP_AB = 512
TQ = 512
N_TILES = N_TOK // TM
BQ_S = 256
LOG2E = 1.4426950408889634
MLA_STACK_PROMPT = 4
MLA_STACK_SAMPLE = 4
GQA_STACK_PROMPT = 4
GQA_STACK_SAMPLE = 2
MOD_ROWS = 8

PAIRS_PER_GROUP = 6
N_BUCKETS = N_GROUPS * PAIRS_PER_GROUP
SLOT_A_LOCAL = (0, 2, 2, 3, 3, 3)
SLOT_B_LOCAL = (1, 1, 0, 0, 1, 2)
TME = 256
MOE_TILES = -(-(N_TOK + N_BUCKETS * (TME - 1)) // TME)
MOE_ROWS = MOE_TILES * TME
MOE_ITEMS = (MOE_TILES + N_BUCKETS) // 2

BUCKET_ROWS = 32
ROW_W = D_MODEL + LANES
STAGE_SLOTS = 3
GATHER_SLOTS = 3

COL_QB = Q_LORA + KV_LORA
COL_KB = COL_QB + H_B * HD
COL_VB = COL_KB + KVH_B * HD
COL_KPE = COL_VB + KVH_B * HD
AB_COLS = COL_KPE + LANES
C_NORM_COLS = H_C * HD + KVH_C * HD
NORM_BLOCK = 256

BF16 = jnp.bfloat16
F32 = jnp.float32
VMEM_LIMIT = 52 * 1024 * 1024


def _mod_row(t, tile=TM):
    p_tiles = N_PROMPT // tile
    return jnp.where(t < p_tiles, 0, 1 + (t - p_tiles) // (DEC_SEQ // tile))


def _mod_spec(layer, chunk, tile=TM):
    return pl.BlockSpec((None, None, None, 1, D_MODEL), lambda t: (layer, _mod_row(t, tile), chunk, 0, 0))


def _const_spec(shape):
    nd = len(shape)
    return pl.BlockSpec(shape, lambda *_: (0,) * nd)


def _split_specs(tile=TM):
    p_tiles = N_PROMPT // tile
    return [
        pl.BlockSpec((tile, D_MODEL), lambda t: (jnp.minimum(t, p_tiles - 1), 0)),
        pl.BlockSpec((tile, D_MODEL), lambda t: (jnp.maximum(t - p_tiles, 0), 0)),
    ]


def _group_spec(tile, width, prompt):
    p_tiles = N_PROMPT // tile
    if prompt:
        return pl.BlockSpec((tile, width), lambda t: (jnp.minimum(t, p_tiles - 1), 0))
    return pl.BlockSpec((tile, width), lambda t: (jnp.maximum(t - p_tiles, 0), 0))


def _pick(t, a_ref, b_ref):
    return jnp.where(t < N_PROMPT // a_ref.shape[0], a_ref[...], b_ref[...])


def _sub_tiles(rows):
    return [slice(i * TM, (i + 1) * TM) for i in range(rows // TM)]


def _dot_rows(parts, w):
    full = jnp.dot(jnp.concatenate(parts, axis=0), w, preferred_element_type=F32)
    return [full[s] for s in _sub_tiles(full.shape[0])]


def _params(sem):
    return pltpu.CompilerParams(dimension_semantics=sem, vmem_limit_bytes=VMEM_LIMIT)


def _mod_kernel(cond_ref, w_ref, b_ref, o_ref):
    c = cond_ref[...]
    s = (c / (1.0 + jnp.exp(-c))).astype(BF16)
    o_ref[...] = jnp.dot(s, w_ref[...].astype(BF16), preferred_element_type=F32) + b_ref[...]


def _modulation(cond, w_mod, b_mod):
    tn = 1536
    return pl.pallas_call(
        _mod_kernel,
        grid=(DEPTH, 6 * D_MODEL // tn),
        in_specs=[
            pl.BlockSpec((MOD_ROWS, D_MODEL), lambda l, j: (0, 0)),
            pl.BlockSpec((None, D_MODEL, tn), lambda l, j: (l, 0, j)),
            pl.BlockSpec((None, 1, tn), lambda l, j: (l, 0, j)),
        ],
        out_specs=pl.BlockSpec((None, MOD_ROWS, tn), lambda l, j: (l, 0, j)),
        out_shape=jax.ShapeDtypeStruct((DEPTH, MOD_ROWS, 6 * D_MODEL), F32),
        compiler_params=_params(("arbitrary", "arbitrary")),
        name="modulation",
    )(cond, w_mod, b_mod.reshape(DEPTH, 1, 6 * D_MODEL))


def _grid_cos_sin(half):
    n_freq = half // 2
    rows = DEC_SEQ // GRID_W
    row = jnp.repeat(jnp.arange(rows, dtype=F32), GRID_W)
    col = jnp.tile(jnp.arange(GRID_W, dtype=F32), rows)
    inv = 1.0 / (ROPE_THETA ** (jnp.arange(n_freq, dtype=F32) / n_freq))
    ang = jnp.concatenate([row[:, None] * inv, col[:, None] * inv], axis=-1)
    return jnp.cos(ang), jnp.sin(ang)


def _rope_tables():
    tabs = []
    for half in (HD // 2, ROPE_A // 2):
        cos, sin = _grid_cos_sin(half)
        reps = LANES // (2 * half)
        tabs.append(jnp.tile(jnp.concatenate([cos, cos], axis=-1), (1, reps)))
        tabs.append(jnp.tile(jnp.concatenate([-sin, sin], axis=-1), (1, reps)))
    half = ROPE_A // 2
    one = jnp.ones((DEC_SEQ, NOPE_A), F32)
    zero = jnp.zeros((DEC_SEQ, NOPE_A), F32)
    pad1 = jnp.ones((DEC_SEQ, LANES - NOPE_A - ROPE_A), F32)
    pad0 = jnp.zeros((DEC_SEQ, LANES - NOPE_A - ROPE_A), F32)
    zh = jnp.zeros((DEC_SEQ, half), F32)
    tabs.append(jnp.concatenate([one, cos, cos, pad1], axis=-1))
    tabs.append(jnp.concatenate([zero, -sin, zh, pad0], axis=-1))
    tabs.append(jnp.concatenate([zero, zh, sin, pad0], axis=-1))
    return tuple(tabs)


def _rope(x, c, s, half):
    w = x.shape[1]
    reps = w // LANES
    if reps > 1:
        c = jnp.concatenate([c] * reps, axis=1)
        s = jnp.concatenate([s] * reps, axis=1)
    ahead = pltpu.roll(x, w - half, 1)
    behind = pltpu.roll(x, half, 1)
    lane = lax.broadcasted_iota(jnp.int32, x.shape, 1)
    swapped = jnp.where((lane & (2 * half - 1)) < half, ahead, behind)
    return x * c + swapped * s


def _rope_specs(n, tile):
    p_tiles = N_PROMPT // tile

    def idx(t):
        return (jnp.maximum(t - p_tiles, 0) % (DEC_SEQ // tile), 0)
    return [pl.BlockSpec((tile, LANES), idx)] * n


def _rms(x, g_ref):
    return x * lax.rsqrt(jnp.mean(x * x, axis=-1, keepdims=True) + EPS) * g_ref[...]


def _inproj_ab_kernel(xp_ref, xs_ref, shift_ref, scale_ref, w_in_ref, qn_ref, kvn_ref, w_uq_ref, w_kn_ref, w_v_ref,
                      place_ref, c64_ref, s64_ref, c32_ref, s32_ref, ca_ref, sa_ref, sb_ref,
                      qa_ref, ka_ref, va_ref, qb_ref, ckv_p_ref, kpe_p_ref, kbt_p_ref, vbt_p_ref, kb_s_ref, vb_s_ref):
    t = pl.program_id(0)
    x = _pick(t, xp_ref, xs_ref)
    subs = _sub_tiles(x.shape[0])
    h = [(x[s] * (1.0 + scale_ref[...]) + shift_ref[...]).astype(BF16) for s in subs]
    z = _dot_rows(h, w_in_ref[...])
    ql = [_rms(z_s[:, :Q_LORA], qn_ref).astype(BF16) for z_s in z]
    qa = _dot_rows(ql, w_uq_ref[...])
    ckv = [_rms(z_s[:, Q_LORA:Q_LORA + KV_LORA], kvn_ref) for z_s in z]
    ckv16 = [c_s.astype(BF16) for c_s in ckv]
    k_nope = _dot_rows(ckv16, w_kn_ref[...])
    va = [va_s.astype(BF16) for va_s in _dot_rows(ckv16, w_v_ref[...])]
    for s, va_s in zip(subs, va):
        va_ref[s, :] = va_s
    qb = [z_s[:, COL_QB:COL_KB] for z_s in z]
    kb = [z_s[:, COL_KB:COL_VB] for z_s in z]
    vb = [z_s[:, COL_VB:COL_KPE] for z_s in z]
    kpe = [z_s[:, COL_KPE:] for z_s in z]

    def finish(qa_out, qb_out, kpe_out):
        k_rope = _dot_rows([kpe_s.astype(BF16) for kpe_s in kpe_out], place_ref[...])
        for i, s in enumerate(subs):
            qa_ref[s, :] = qa_out[i].astype(BF16)
            qb_ref[s, :] = qb_out[i].astype(BF16)
            ka_ref[s, :] = (k_nope[i] + k_rope[i]).astype(BF16)

    p_tiles = N_PROMPT // x.shape[0]

    @pl.when(t < p_tiles)
    def _():
        finish(qa, qb, kpe)
        for i, s in enumerate(subs):
            ckv_p_ref[s, :] = ckv[i]
            kpe_p_ref[s, :] = kpe[i][:, :ROPE_A]
            kbt_p_ref[i] = kb[i].T
            vbt_p_ref[i] = vb[i].T

    @pl.when(t >= p_tiles)
    def _():
        w = qa[0].shape[1]
        reps = w // LANES
        half = ROPE_A // 2
        qa_rot, qb_rot, kb_rot, kpe_rot = [], [], [], []
        for i, s in enumerate(subs):
            ca = jnp.concatenate([ca_ref[s, :]] * reps, axis=1)
            sa = jnp.concatenate([sa_ref[s, :]] * reps, axis=1)
            sb = jnp.concatenate([sb_ref[s, :]] * reps, axis=1)
            qa_rot.append(qa[i] * ca + pltpu.roll(qa[i], w - half, 1) * sa + pltpu.roll(qa[i], half, 1) * sb)
            qb_rot.append(_rope(qb[i], c64_ref[s, :], s64_ref[s, :], HD // 2))
            kb_rot.append(_rope(kb[i], c64_ref[s, :], s64_ref[s, :], HD // 2))
            kpe_rot.append(_rope(kpe[i], c32_ref[s, :], s32_ref[s, :], ROPE_A // 2))
        finish(qa_rot, qb_rot, kpe_rot)
        for i, s in enumerate(subs):
            kb_s_ref[s, :] = kb_rot[i]
            vb_s_ref[s, :] = vb[i]


def _mla_weights(w_uq, w_ukv):
    pad = LANES - NOPE_A - ROPE_A
    uq = w_uq.reshape(Q_LORA, H_A, NOPE_A + ROPE_A)
    uq = jnp.pad(uq, ((0, 0), (0, 0), (0, pad))).reshape(Q_LORA, H_A * LANES)
    ukv = w_ukv.reshape(KV_LORA, H_A, NOPE_A + V_A)
    kn = jnp.pad(ukv[:, :, :NOPE_A], ((0, 0), (0, 0), (0, LANES - NOPE_A))).reshape(KV_LORA, H_A * LANES)
    wv = ukv[:, :, NOPE_A:].reshape(KV_LORA, H_A * V_A)
    src = np.arange(LANES)[:, None]
    dst = np.arange(H_A * LANES)[None, :] % LANES
    place = (src < ROPE_A) & (dst == src + NOPE_A)
    return uq.astype(BF16), kn.astype(BF16), wv.astype(BF16), jnp.asarray(place, BF16)


def _inproj_ab(xp, xs, mod, w_in, q_norm, kv_norm, w_uq, w_kn, w_v, place, tabs):
    widths = [(H_A * LANES, BF16), (H_A * LANES, BF16), (H_A * V_A, BF16), (H_B * HD, BF16)]
    tp = TP_AB
    out_specs = [pl.BlockSpec((tp, w), lambda t: (t, 0)) for w, _ in widths]
    out_shape = [jax.ShapeDtypeStruct((N_TOK, w), dt) for w, dt in widths]
    for w in (KV_LORA, ROPE_A):
        out_specs.append(_group_spec(tp, w, prompt=True))
        out_shape.append(jax.ShapeDtypeStruct((N_PROMPT, w), F32))
    assert SEQ == TM
    p_tiles = N_PROMPT // tp
    for _ in range(2):
        out_specs.append(pl.BlockSpec((tp // SEQ, KVH_B * HD, SEQ), lambda t: (jnp.minimum(t, p_tiles - 1), 0, 0)))
        out_shape.append(jax.ShapeDtypeStruct((BATCH, KVH_B * HD, SEQ), F32))
    for w in (KVH_B * HD, KVH_B * HD):
        out_specs.append(_group_spec(tp, w, prompt=False))
        out_shape.append(jax.ShapeDtypeStruct((N_SAMPLE, w), F32))
    return pl.pallas_call(
        _inproj_ab_kernel,
        grid=(N_TOK // tp,),
        in_specs=_split_specs(tp) + [_mod_spec(0, 0, tp), _mod_spec(0, 1, tp), _const_spec((D_MODEL, AB_COLS)),
                                     _const_spec((1, Q_LORA)), _const_spec((1, KV_LORA)),
                                     _const_spec((Q_LORA, H_A * LANES)), _const_spec((KV_LORA, H_A * LANES)),
                                     _const_spec((KV_LORA, H_A * V_A)), _const_spec((LANES, H_A * LANES))]
        + _rope_specs(7, tp),
        out_specs=out_specs,
        out_shape=out_shape,
        compiler_params=_params(("arbitrary",)),
        name="inproj_ab",
    )(xp, xs, mod, mod, w_in, q_norm, kv_norm, w_uq, w_kn, w_v, place, *tabs)


def _inproj_c_kernel(x_ref, shift_ref, scale_ref, w_in_ref, g_ref, ones_ref, c64_ref, s64_ref,
                     q_ref, k_p_ref, v_p_ref, k_s_ref, v_s_ref):
    t = pl.program_id(0)
    subs = _sub_tiles(x_ref.shape[0])
    h = [(x_ref[s, :] * (1.0 + scale_ref[...]) + shift_ref[...]).astype(BF16) for s in subs]
    z = _dot_rows(h, w_in_ref[...])
    qk = [z_s[:, :C_NORM_COLS] for z_s in z]
    sq = [qk_s * qk_s for qk_s in qk]
    sq_hi = [sq_s.astype(BF16) for sq_s in sq]
    sq_lo = [(sq_s - hi_s.astype(F32)).astype(BF16) for sq_s, hi_s in zip(sq, sq_hi)]
    blocks = []
    for j in range(C_NORM_COLS // NORM_BLOCK):
        sl = slice(j * NORM_BLOCK, (j + 1) * NORM_BLOCK)
        by_hi = _dot_rows([hi_s[:, sl] for hi_s in sq_hi], ones_ref[...])
        by_lo = _dot_rows([lo_s[:, sl] for lo_s in sq_lo], ones_ref[...])
        blocks.append([a + b for a, b in zip(by_hi, by_lo)])
    ss = [jnp.concatenate([blk[i] for blk in blocks], axis=1) for i in range(len(subs))]
    qk = [qk_s * lax.rsqrt(ss_s * (1.0 / HD) + EPS) * g_ref[...] for qk_s, ss_s in zip(qk, ss)]
    p_tiles = N_PROMPT // x_ref.shape[0]

    @pl.when(t < p_tiles)
    def _():
        for s, qk_s, z_s in zip(subs, qk, z):
            q_ref[s, :] = qk_s[:, :H_C * HD].astype(BF16)
            k_p_ref[s, :] = qk_s[:, H_C * HD:]
            v_p_ref[s, :] = z_s[:, C_NORM_COLS:]

    @pl.when(t >= p_tiles)
    def _():
        for s, qk_s, z_s in zip(subs, qk, z):
            q_ref[s, :] = _rope(qk_s[:, :H_C * HD], c64_ref[s, :], s64_ref[s, :], HD // 2).astype(BF16)
            k_s_ref[s, :] = _rope(qk_s[:, H_C * HD:], c64_ref[s, :], s64_ref[s, :], HD // 2)
            v_s_ref[s, :] = z_s[:, C_NORM_COLS:]


def _inproj_c(x, mod, w_in, g_full, tabs):
    def tok(w):
        return pl.BlockSpec((TP, w), lambda t: (t, 0))
    kv_w = KVH_C * HD
    head = np.arange(NORM_BLOCK) // HD
    ones = jnp.asarray(head[:, None] == head[None, :], BF16)
    return pl.pallas_call(
        _inproj_c_kernel,
        grid=(N_TOK // TP,),
        in_specs=[tok(D_MODEL), _mod_spec(1, 0, TP), _mod_spec(1, 1, TP),
                  _const_spec((D_MODEL, C_NORM_COLS + KVH_C * HD)),
                  _const_spec((1, C_NORM_COLS)), _const_spec((NORM_BLOCK, NORM_BLOCK))] + _rope_specs(2, TP),
        out_specs=[tok(H_C * HD), _group_spec(TP, kv_w, True), _group_spec(TP, kv_w, True),
                   _group_spec(TP, kv_w, False), _group_spec(TP, kv_w, False)],
        out_shape=[jax.ShapeDtypeStruct((N_TOK, H_C * HD), BF16),
                   jax.ShapeDtypeStruct((N_PROMPT, kv_w), F32), jax.ShapeDtypeStruct((N_PROMPT, kv_w), F32),
                   jax.ShapeDtypeStruct((N_SAMPLE, kv_w), F32), jax.ShapeDtypeStruct((N_SAMPLE, kv_w), F32)],
        compiler_params=_params(("arbitrary",)),
        name="inproj_c",
    )(x, mod, mod, w_in, g_full, ones, *tabs[:2])


def _qk(q, k):
    return lax.dot_general(q, k, (((1,), (1,)), ((), ())), preferred_element_type=F32)


def _softmax_pv(scores, values, scale, sink=None, kv_t=False):
    c = scale * LOG2E
    m = scores[0].max(axis=-1, keepdims=True)
    for s in scores[1:]:
        m = jnp.maximum(m, s.max(axis=-1, keepdims=True))
    m = m * c
    if sink is not None:
        sink = sink * LOG2E
        m = jnp.maximum(m, sink)
    den = None
    out = None
    for s, v in zip(scores, values):
        p = jnp.exp2(s * c - m)
        d = p.sum(axis=-1, keepdims=True)
        p = p.astype(BF16)
        if isinstance(v, (list, tuple)):
            rows = p.shape[0] // len(v)
            o = jnp.concatenate([_qk(p[b * rows:(b + 1) * rows], v_b) if kv_t
                                 else jnp.dot(p[b * rows:(b + 1) * rows], v_b, preferred_element_type=F32)
                                 for b, v_b in enumerate(v)], axis=0)
        else:
            o = _qk(p, v) if kv_t else jnp.dot(p, v, preferred_element_type=F32)
        den = d if den is None else den + d
        out = o if out is None else out + o
    if sink is not None:
        den = den + jnp.exp2(sink - m)
    return out * (1.0 / den)


def _mla_heads(qa_ref, ka_refs, va_refs, o_ref, stack):
    scale = (NOPE_A + ROPE_A) ** -0.5
    rows = qa_ref.shape[0]
    for h0 in range(0, H_A, stack):
        heads = range(h0, h0 + stack)
        scores = [jnp.concatenate([_qk(qa_ref[:, h * LANES:(h + 1) * LANES], ka_ref[:, h * LANES:(h + 1) * LANES])
                                   for h in heads], axis=0) for ka_ref in ka_refs]
        values = [[va_ref[:, h * V_A:(h + 1) * V_A] for h in heads] for va_ref in va_refs]
        o = _softmax_pv(scores, values, scale)
        o = jnp.concatenate([o[b * rows:(b + 1) * rows] for b in range(stack)], axis=1)
        o_ref[:, h0 * V_A:(h0 + stack) * V_A] = o.astype(o_ref.dtype)


def _gqa_groups(q_ref, col0, n_q, ks, vs, mask=None, sink=None, kv_t=False):
    rows = q_ref.shape[0]
    n_groups = len(ks)
    n_seg = len(ks[0])
    per_group = []
    for g in range(n_groups):
        base = col0 + g * n_q * HD
        q = jnp.concatenate([q_ref[:, base + i * HD:base + (i + 1) * HD] for i in range(n_q)], axis=0)
        if kv_t:
            per_group.append([jnp.dot(q, k, preferred_element_type=F32) for k in ks[g]])
        else:
            per_group.append([_qk(q, k) for k in ks[g]])
    scores = [jnp.concatenate([per_group[g][j] for g in range(n_groups)], axis=0) for j in range(n_seg)]
    values = [[vs[g][j] for g in range(n_groups)] for j in range(n_seg)]
    if mask is not None:
        scores[0] = jnp.where(mask, scores[0], NEG_INF)
    o = _softmax_pv(scores, values, HD ** -0.5, sink, kv_t)
    return jnp.concatenate([o[i * rows:(i + 1) * rows] for i in range(n_groups * n_q)], axis=1)


def _stacked_sink(sink_ref, h0, n_q, rows):
    head = lax.broadcasted_iota(jnp.int32, (n_q * rows, 1), 0) // rows
    col = jnp.full((n_q * rows, 1), sink_ref[h0], F32)
    for g in range(1, n_q):
        col = jnp.where(head == g, sink_ref[h0 + g], col)
    return col


def _attn_ab_prompt_kernel(sink_ref, qa_ref, ka_ref, va_ref, qb_ref, kbt_ref, vbt_ref, o_ref):
    _mla_heads(qa_ref, [ka_ref], [va_ref], o_ref, MLA_STACK_PROMPT)
    kbt = kbt_ref[...].astype(BF16)
    vbt = vbt_ref[...].astype(BF16)
    sls = [slice(kh * HD, (kh + 1) * HD) for kh in range(KVH_B)]
    o = _gqa_groups(qb_ref, 0, H_B // KVH_B, [[kbt[sl, :]] for sl in sls], [[vbt[sl, :]] for sl in sls],
                    sink=_stacked_sink(sink_ref, 0, H_B, SEQ), kv_t=True)
    o_ref[:, H_A * V_A:] = o.astype(o_ref.dtype)


def _attn_ab_prompt(sink, qa, ka, va, qb, kbt, vbt):
    def blk(w):
        return pl.BlockSpec((SEQ, w), lambda b: (b, 0))
    tblk = pl.BlockSpec((None, KVH_B * HD, SEQ), lambda b: (b, 0, 0))
    return pl.pallas_call(
        _attn_ab_prompt_kernel,
        grid=(BATCH,),
        in_specs=[pl.BlockSpec(memory_space=pltpu.SMEM), blk(H_A * LANES), blk(H_A * LANES), blk(H_A * V_A),
                  blk(H_B * HD), tblk, tblk],
        out_specs=blk(D_MODEL),
        out_shape=jax.ShapeDtypeStruct((N_PROMPT, H_A * V_A + H_B * HD), BF16),
        compiler_params=_params(("arbitrary",)),
        name="attn_ab_prompt",
    )(sink, qa, ka, va, qb, kbt, vbt)


def _attn_ab_sample_kernel(sink_ref, qa_ref, ka_ref, va_ref, qb_ref, kb_ref, vb_ref,
                           ckv_ctx_ref, kpe_ctx_ref, kb_ctx_ref, vb_ctx_ref, w_kn_ref, w_v_ref, place_ref,
                           o_ref, ka_ctx_ref, va_ctx_ref):
    i = pl.program_id(1)

    @pl.when(i == 0)
    def _():
        ckv16 = ckv_ctx_ref[...].astype(BF16)
        k_nope = jnp.dot(ckv16, w_kn_ref[...], preferred_element_type=F32)
        k_rope = jnp.dot(kpe_ctx_ref[...].astype(BF16), place_ref[...], preferred_element_type=F32)
        ka_ctx_ref[...] = (k_nope + k_rope).astype(BF16)
        va_ctx_ref[...] = jnp.dot(ckv16, w_v_ref[...], preferred_element_type=F32).astype(BF16)

    _mla_heads(qa_ref, [ka_ref, ka_ctx_ref], [va_ref, va_ctx_ref], o_ref, MLA_STACK_SAMPLE)

    n_win = BQ_S + 2 * WINDOW
    start = pl.multiple_of(jnp.clip(i * BQ_S - WINDOW, 0, DEC_SEQ - n_win), WINDOW)
    row = lax.broadcasted_iota(jnp.int32, (H_B * BQ_S, n_win), 0)
    qpos = i * BQ_S + (row & (BQ_S - 1))
    kpos = start + lax.broadcasted_iota(jnp.int32, (H_B * BQ_S, n_win), 1)
    in_band = jnp.abs(qpos - kpos) <= WINDOW
    kwin = kb_ref[pl.ds(start, n_win), :].astype(BF16)
    vwin = vb_ref[pl.ds(start, n_win), :].astype(BF16)
    kctx = kb_ctx_ref[...].astype(BF16)
    vctx = vb_ctx_ref[...].astype(BF16)
    sls = [slice(kh * HD, (kh + 1) * HD) for kh in range(KVH_B)]
    o = _gqa_groups(qb_ref, 0, H_B // KVH_B, [[kwin[:, sl], kctx[:, sl]] for sl in sls],
                    [[vwin[:, sl], vctx[:, sl]] for sl in sls],
                    mask=in_band, sink=_stacked_sink(sink_ref, 0, H_B, BQ_S))
    o_ref[:, H_A * V_A:] = o.astype(o_ref.dtype)


def _attn_ab_sample(sink, qa, ka, va, qb, kb, vb, ckv_ctx, kpe_ctx, kb_ctx, vb_ctx, w_kn, w_v, place):
    qoff = N_PROMPT // BQ_S
    nq = DEC_SEQ // BQ_S
    boff = N_PROMPT // DEC_SEQ

    def qblk(w):
        return pl.BlockSpec((BQ_S, w), lambda b, i: (qoff + b * nq + i, 0))

    def bblk(w):
        return pl.BlockSpec((DEC_SEQ, w), lambda b, i: (boff + b, 0))

    def sblk(w):
        return pl.BlockSpec((DEC_SEQ, w), lambda b, i: (b, 0))

    def cblk(w):
        return pl.BlockSpec((PAST_LEN, w), lambda b, i: (b, 0))

    return pl.pallas_call(
        _attn_ab_sample_kernel,
        grid=(DEC_BATCH, nq),
        in_specs=[pl.BlockSpec(memory_space=pltpu.SMEM), qblk(H_A * LANES), bblk(H_A * LANES), bblk(H_A * V_A),
                  qblk(H_B * HD), sblk(KVH_B * HD), sblk(KVH_B * HD),
                  cblk(KV_LORA), cblk(LANES), cblk(KVH_B * HD), cblk(KVH_B * HD),
                  _const_spec((KV_LORA, H_A * LANES)), _const_spec((KV_LORA, H_A * V_A)),
                  _const_spec((LANES, H_A * LANES))],
        out_specs=pl.BlockSpec((BQ_S, D_MODEL), lambda b, i: (b * nq + i, 0)),
        out_shape=jax.ShapeDtypeStruct((N_SAMPLE, H_A * V_A + H_B * HD), BF16),
        scratch_shapes=[pltpu.VMEM((PAST_LEN, H_A * LANES), BF16), pltpu.VMEM((PAST_LEN, H_A * V_A), BF16)],
        compiler_params=_params(("arbitrary", "arbitrary")),
        name="attn_ab_sample",
    )(sink, qa, ka, va, qb, kb, vb, ckv_ctx, kpe_ctx, kb_ctx, vb_ctx, w_kn, w_v, place)


def _gqa_heads(q_ref, k_list, v_list, o_ref, stack):
    gq = H_C // KVH_C
    for kh0 in range(0, KVH_C, stack):
        sls = [slice(kh * HD, (kh + 1) * HD) for kh in range(kh0, kh0 + stack)]
        o = _gqa_groups(q_ref, kh0 * gq * HD, gq, [[k[:, sl] for k in k_list] for sl in sls],
                        [[v[:, sl] for v in v_list] for sl in sls])
        o_ref[:, kh0 * gq * HD:(kh0 + stack) * gq * HD] = o.astype(o_ref.dtype)


def _attn_c_prompt_kernel(q_ref, k_ref, v_ref, o_ref):
    _gqa_heads(q_ref, [k_ref[...].astype(BF16)], [v_ref[...].astype(BF16)], o_ref, GQA_STACK_PROMPT)


def _attn_c_prompt(q, k, v):
    def blk(w):
        return pl.BlockSpec((SEQ, w), lambda b: (b, 0))
    return pl.pallas_call(
        _attn_c_prompt_kernel,
        grid=(BATCH,),
        in_specs=[blk(H_C * HD), blk(KVH_C * HD), blk(KVH_C * HD)],
        out_specs=blk(H_C * HD),
        out_shape=jax.ShapeDtypeStruct((N_PROMPT, H_C * HD), BF16),
        compiler_params=_params(("arbitrary",)),
        name="attn_c_prompt",
    )(q, k, v)


def _attn_c_sample_kernel(q_ref, k_ref, v_ref, kc_ref, vc_ref, o_ref):
    _gqa_heads(q_ref, [k_ref[...].astype(BF16), kc_ref[...].astype(BF16)],
               [v_ref[...].astype(BF16), vc_ref[...].astype(BF16)], o_ref, GQA_STACK_SAMPLE)


def _attn_c_sample(q, k, v, k_ctx, v_ctx):
    qoff = N_PROMPT // BQ_S
    nq = DEC_SEQ // BQ_S
    w = KVH_C * HD
    return pl.pallas_call(
        _attn_c_sample_kernel,
        grid=(DEC_BATCH, nq),
        in_specs=[pl.BlockSpec((BQ_S, H_C * HD), lambda b, i: (qoff + b * nq + i, 0)),
                  pl.BlockSpec((DEC_SEQ, w), lambda b, i: (b, 0)),
                  pl.BlockSpec((DEC_SEQ, w), lambda b, i: (b, 0)),
                  pl.BlockSpec((PAST_LEN, w), lambda b, i: (b, 0)),
                  pl.BlockSpec((PAST_LEN, w), lambda b, i: (b, 0))],
        out_specs=pl.BlockSpec((BQ_S, H_C * HD), lambda b, i: (b * nq + i, 0)),
        out_shape=jax.ShapeDtypeStruct((N_SAMPLE, H_C * HD), BF16),
        compiler_params=_params(("arbitrary", "arbitrary")),
        name="attn_c_sample",
    )(q, k, v, k_ctx, v_ctx)


def _layer_norm(y, g, b):
    mu = jnp.mean(y, axis=-1, keepdims=True)
    yc = y - mu
    var = jnp.mean(yc * yc, axis=-1, keepdims=True)
    return yc * lax.rsqrt(var + EPS) * g + b


def _route(sel, aff):
    def row(a, j):
        return a[j:j + 1, :]

    scores = []
    for g in range(N_GROUPS):
        a0, a1, a2, a3 = (row(sel, 4 * g + j) for j in range(4))
        hi01, lo01 = jnp.maximum(a0, a1), jnp.minimum(a0, a1)
        hi23, lo23 = jnp.maximum(a2, a3), jnp.minimum(a2, a3)
        top1 = jnp.maximum(hi01, hi23)
        top2 = jnp.maximum(jnp.minimum(hi01, hi23), jnp.maximum(lo01, lo23))
        scores.append(top1 + top2)
    best = scores[0]
    gi = jnp.zeros(best.shape, jnp.int32)
    for g in range(1, N_GROUPS):
        better = scores[g] > best
        gi = jnp.where(better, g, gi)
        best = jnp.where(better, scores[g], best)

    def in_group(a, j):
        out = row(a, j)
        for g in range(1, N_GROUPS):
            out = jnp.where(gi == g, row(a, 4 * g + j), out)
        return out

    v = [in_group(sel, j) for j in range(4)]
    a = [in_group(aff, j) for j in range(4)]
    chosen = []
    for j in range(4):
        rank = jnp.zeros(best.shape, jnp.int32)
        for k in range(4):
            if k == j:
                continue
            ahead = (v[k] >= v[j]) if k < j else (v[k] > v[j])
            rank = rank + ahead.astype(jnp.int32)
        chosen.append(rank < 2)
    total = sum(jnp.where(chosen[j], a[j], 0.0) for j in range(4))
    w = [jnp.where(chosen[j], a[j], 0.0) / total for j in range(4)]
    pair = jnp.zeros(best.shape, jnp.int32)
    for p, (ja, jb) in enumerate(zip(SLOT_A_LOCAL, SLOT_B_LOCAL)):
        pair = jnp.where(chosen[ja] & chosen[jb], p, pair)
    gate_a = jnp.where(pair == 0, w[0], jnp.where(pair <= 2, w[2], w[3]))
    gate_b = jnp.where((pair == 0) | (pair == 1) | (pair == 4), w[1], jnp.where(pair == 5, w[2], w[0]))
    return gi * PAIRS_PER_GROUP + pair, gate_a, gate_b


def _post_attn_kernel(split_x, *refs):
    t = pl.program_id(0)
    op_ref, os_ref = refs[:2]
    if split_x:
        x = _pick(t, refs[2], refs[3])
        refs = refs[4:]
    else:
        x = refs[2][...]
        refs = refs[3:]
    (w_out_ref, gate_ref, shift_ref, scale_ref, lng_ref, lnb_ref, rw_ref, rb_ref, tri_ref,
     x1_ref, row_ref, route_ref, counts_ref, carry_ref) = refs
    o = _pick(t, op_ref, os_ref)
    subs = _sub_tiles(o.shape[0])

    a = _dot_rows([o], w_out_ref[...])
    x1 = [_layer_norm(ALPHA * x[s] + gate_ref[...] * a_s, lng_ref[...], lnb_ref[...]) for s, a_s in zip(subs, a)]
    h2 = [x1_s * (1.0 + scale_ref[...]) + shift_ref[...] for x1_s in x1]
    for s, x1_s, h2_s in zip(subs, x1, h2):
        x1_ref[s, :] = x1_s
        row_ref[s, :D_MODEL] = h2_s
    rw = rw_ref[...]
    rw_hi = rw.astype(BF16)
    rw_lo = (rw - rw_hi.astype(F32)).astype(BF16)
    rw_both = jnp.concatenate([rw_hi, rw_lo], axis=0)
    logits = []
    for h2_s in h2:
        h2_hi = h2_s.astype(BF16)
        h2_lo = (h2_s - h2_hi.astype(F32)).astype(BF16)
        by_hi = _qk(rw_both, h2_hi)
        logits.append(by_hi[:N_EXPERTS] + by_hi[N_EXPERTS:] + _qk(rw_hi, h2_lo))
    aff = [1.0 / (1.0 + jnp.exp(-l)) for l in logits]
    routed = [_route(aff_s + rb_ref[...], aff_s) for aff_s in aff]

    lane = lax.broadcasted_iota(jnp.int32, (LANES, TM), 0)
    for s, (_, gate_a, gate_b) in zip(subs, routed):
        meta_t = jnp.where(lane == 0, gate_a, jnp.where(lane == 1, gate_b, 0.0))
        row_ref[s, D_MODEL:] = meta_t.T

    @pl.when(t == 0)
    def _():
        carry_ref[...] = jnp.zeros_like(carry_ref)

    onehots = [lax.broadcasted_iota(jnp.int32, (BUCKET_ROWS, TM), 0) == bucket for bucket, _, _ in routed]
    prefixes = [jnp.dot(oh.astype(BF16), tri_ref[...], preferred_element_type=F32) for oh in onehots]
    carry = carry_ref[...]
    route_ref[...] = jnp.zeros(route_ref.shape, F32)
    for s, (bucket, _, _), onehot, prefix in zip(subs, routed, onehots, prefixes):
        rank = jnp.sum(jnp.where(onehot, prefix + carry[:, 0:1], 0.0), axis=0, keepdims=True)
        carry = carry + jnp.sum(onehot.astype(F32), axis=1, keepdims=True)
        route_ref[0:1, s] = bucket.astype(F32)
        route_ref[1:2, s] = rank
    carry_ref[...] = carry
    counts_ref[...] = carry


def _post_attn(layer, o_p, o_s, x_list, mod, w_out, ln_g, ln_b, rw_t, rb):
    split_x = len(x_list) == 2
    tok = pl.BlockSpec((TP, D_MODEL), lambda t: (t, 0))
    tri = jnp.asarray(np.arange(TM)[:, None] < np.arange(TM)[None, :], BF16)
    return pl.pallas_call(
        functools.partial(_post_attn_kernel, split_x),
        grid=(N_TOK // TP,),
        in_specs=_split_specs(TP) + (_split_specs(TP) if split_x else [tok]) + [
            _const_spec((D_MODEL, D_MODEL)), _mod_spec(layer, 2, TP), _mod_spec(layer, 3, TP),
            _mod_spec(layer, 4, TP),
            _const_spec((1, D_MODEL)), _const_spec((1, D_MODEL)), _const_spec((N_EXPERTS, D_MODEL)),
            _const_spec((N_EXPERTS, 1)), _const_spec((TM, TM))],
        out_specs=[tok, pl.BlockSpec((TP, ROW_W), lambda t: (t, 0)), pl.BlockSpec((8, TP), lambda t: (0, t)),
                   _const_spec((BUCKET_ROWS, LANES))],
        out_shape=[jax.ShapeDtypeStruct((N_TOK, D_MODEL), F32), jax.ShapeDtypeStruct((N_TOK, ROW_W), F32),
                   jax.ShapeDtypeStruct((8, N_TOK), F32), jax.ShapeDtypeStruct((BUCKET_ROWS, LANES), F32)],
        scratch_shapes=[pltpu.VMEM((BUCKET_ROWS, LANES), F32)],
        compiler_params=_params(("arbitrary",)),
        name="post_attn",
    )(o_p, o_s, *x_list, w_out, mod, mod, mod, ln_g, ln_b, rw_t, rb, tri)


def _moe_kernel(ia_ref, ib_ref, itile_ref, isize_ref, ni_ref, nt_ref, x_hbm, wga_ref, wua_ref, wda_ref, wgb_ref,
                wub_ref, wdb_ref, o_hbm, xbuf, obuf, xsem, osem, wga_s, wua_s, wda_s, wgb_s, wub_s, wdb_s):
    i = pl.program_id(0)
    prev = jnp.maximum(i - 1, 0)
    n_items = ni_ref[0]
    n_tiles = nt_ref[0]
    slots = ((ia_ref, (wga_ref, wua_ref, wda_ref), (wga_s, wua_s, wda_s)),
             (ib_ref, (wgb_ref, wub_ref, wdb_ref), (wgb_s, wub_s, wdb_s)))

    def fetch(k):
        row0 = pl.multiple_of(itile_ref[k] * TME, TME)
        return pltpu.make_async_copy(x_hbm.at[pl.ds(row0, 2 * TME), :], xbuf.at[k % 2], xsem.at[k % 2])

    def store(k, tiles):
        row0 = pl.multiple_of(itile_ref[k] * TME, TME)
        return pltpu.make_async_copy(obuf.at[k % 2, pl.ds(0, tiles * TME), :],
                                     o_hbm.at[pl.ds(row0, tiles * TME), :], osem.at[k % 2])

    def wait_store(k):
        for tiles in (1, 2):
            @pl.when(isize_ref[k] == tiles)
            def _(tiles=tiles):
                store(k, tiles).wait()

    @pl.when((i == 0) & (n_items > 0))
    def _():
        fetch(0).start()

    for t_ref, w_refs, w_scr in slots:
        @pl.when((i == 0) | (t_ref[i] != t_ref[prev]))
        def _(w_refs=w_refs, w_scr=w_scr):
            for w_ref, s_ref in zip(w_refs, w_scr):
                s_ref[...] = w_ref[...].astype(BF16)

    @pl.when(i < n_items)
    def _():
        fetch(i).wait()

        @pl.when(i + 1 < n_items)
        def _():
            fetch(i + 1).start()

        @pl.when(i >= 2)
        def _():
            wait_store(i - 2)

        for tiles in (1, 2):
            @pl.when(isize_ref[i] == tiles)
            def _(tiles=tiles):
                rows = tiles * TME
                x_ref = xbuf.at[i % 2, pl.ds(0, rows), :]
                x = x_ref[:, :D_MODEL].astype(BF16)
                acc = None
                for slot, (_, _, (wg_s, wu_s, wd_s)) in enumerate(slots):
                    gt = jnp.dot(x, wg_s[...], preferred_element_type=F32)
                    up = jnp.dot(x, wu_s[...], preferred_element_type=F32)
                    gate = x_ref[:, D_MODEL + slot:D_MODEL + slot + 1]
                    hid = (gt / (1.0 + jnp.exp(-gt))) * up * gate
                    y = jnp.dot(hid.astype(BF16), wd_s[...], preferred_element_type=F32)
                    acc = y if acc is None else acc + y
                obuf[i % 2, pl.ds(0, rows), :] = acc
                store(i, tiles).start()

    @pl.when(i == MOE_ITEMS - 1)
    def _():
        @pl.when(n_items >= 2)
        def _():
            wait_store(n_items - 2)

        @pl.when(n_items >= 1)
        def _():
            wait_store(n_items - 1)

        obuf[0] = jnp.zeros(obuf.shape[1:], F32)

        def unused(g, carry):
            cp = pltpu.make_async_copy(obuf.at[0, pl.ds(0, TME), :],
                                       o_hbm.at[pl.ds(pl.multiple_of(g * TME, TME), TME), :], osem.at[0])
            cp.start()
            cp.wait()
            return carry
        lax.fori_loop(n_tiles, MOE_TILES, unused, 0)


def _moe(layer, item_a, item_b, item_tile, item_size, n_items, n_tiles, rows, w_gate, w_up, w_down):
    def wspec(shape, which):
        if which == 0:
            return pl.BlockSpec((None, None) + shape, lambda i, ia, ib, it, sz, ni, nt: (layer, ia[i], 0, 0))
        return pl.BlockSpec((None, None) + shape, lambda i, ia, ib, it, sz, ni, nt: (layer, ib[i], 0, 0))

    up_shape, down_shape = (D_MODEL, D_EXPERT), (D_EXPERT, D_MODEL)
    grid_spec = pltpu.PrefetchScalarGridSpec(
        num_scalar_prefetch=6,
        grid=(MOE_ITEMS,),
        in_specs=[pl.BlockSpec(memory_space=pl.ANY),
                  wspec(up_shape, 0), wspec(up_shape, 0), wspec(down_shape, 0),
                  wspec(up_shape, 1), wspec(up_shape, 1), wspec(down_shape, 1)],
        out_specs=pl.BlockSpec(memory_space=pl.ANY),
        scratch_shapes=[pltpu.VMEM((2, 2 * TME, ROW_W), F32), pltpu.VMEM((2, 2 * TME, D_MODEL), F32),
                        pltpu.SemaphoreType.DMA((2,)), pltpu.SemaphoreType.DMA((2,))]
        + [pltpu.VMEM(s, BF16) for s in (up_shape, up_shape, down_shape) * 2],
    )
    return pl.pallas_call(
        _moe_kernel,
        grid_spec=grid_spec,
        out_shape=jax.ShapeDtypeStruct((MOE_ROWS, D_MODEL), F32),
        compiler_params=_params(("arbitrary",)),
        name="moe_experts",
    )(item_a, item_b, item_tile, item_size, n_items, n_tiles, rows, w_gate, w_up, w_down, w_gate, w_up, w_down)


def _row_copy(src_ref, src_row, dst_ref, dst_row, sem):
    return pltpu.make_async_copy(src_ref.at[pl.ds(src_row, 1), :], dst_ref.at[pl.ds(dst_row, 1), :], sem)


def _dispatch_kernel(pos_ref, pend_ref, cnt_ref, nt_ref, src_ref, out_ref, buf, zero_ref, in_sem, sem, zsem):
    t = pl.program_id(0)

    def fetch(tile):
        return pltpu.make_async_copy(src_ref.at[pl.ds(pl.multiple_of(tile * TM, TM), TM), :],
                                     buf.at[tile % STAGE_SLOTS], in_sem.at[tile % STAGE_SLOTS])

    @pl.when(t == 0)
    def _():
        fetch(0).start()
        zero_ref[...] = jnp.zeros_like(zero_ref)

        def zero_tile(row0):
            return pltpu.make_async_copy(zero_ref, out_ref.at[pl.ds(pl.multiple_of(row0, TME), TME), :], zsem)

        for b in range(N_BUCKETS):
            @pl.when(cnt_ref[b] > 0)
            def _(b=b):
                zero_tile(pend_ref[b] - TME).start()

        def start_unused(i, carry):
            zero_tile(i * TME).start()
            return carry
        lax.fori_loop(nt_ref[0], MOE_TILES + 1, start_unused, 0)

        for b in range(N_BUCKETS):
            @pl.when(cnt_ref[b] > 0)
            def _(b=b):
                zero_tile(pend_ref[b] - TME).wait()

        def wait_unused(i, carry):
            zero_tile(i * TME).wait()
            return carry
        lax.fori_loop(nt_ref[0], MOE_TILES + 1, wait_unused, 0)

    @pl.when(t + 1 < N_TILES)
    def _():
        fetch(t + 1).start()

    fetch(t).wait()
    tile_ref = buf.at[t % STAGE_SLOTS]

    for r in range(TM):
        _row_copy(tile_ref, r, out_ref, pos_ref[t * TM + r], sem.at[t % 2]).start(priority=r % 2)

    def drain(tile):
        pltpu.make_async_copy(buf.at[0], out_ref.at[pl.ds(0, TM), :], sem.at[tile % 2]).wait()

    @pl.when(t > 0)
    def _():
        drain(t - 1)

    @pl.when(t == N_TILES - 1)
    def _():
        drain(t)


def _dispatch(pos, pend, counts, n_tiles, rows):
    return pl.pallas_call(
        _dispatch_kernel,
        grid_spec=pltpu.PrefetchScalarGridSpec(
            num_scalar_prefetch=4, grid=(N_TILES,),
            in_specs=[pl.BlockSpec(memory_space=pl.ANY)],
            out_specs=pl.BlockSpec(memory_space=pl.ANY),
            scratch_shapes=[pltpu.VMEM((STAGE_SLOTS, TM, ROW_W), F32), pltpu.VMEM((TME, ROW_W), F32),
                            pltpu.SemaphoreType.DMA((STAGE_SLOTS,)), pltpu.SemaphoreType.DMA((2,)),
                            pltpu.SemaphoreType.DMA]),
        out_shape=jax.ShapeDtypeStruct((MOE_ROWS + TME, ROW_W), F32),
        compiler_params=_params(("arbitrary",)),
        name="dispatch_rows",
    )(pos, pend, counts, n_tiles, rows)


def _plan(route, counts):
    bucket = route[0].astype(jnp.int32)
    rank = route[1].astype(jnp.int32)
    counts = counts[:N_BUCKETS, 0].astype(jnp.int32)
    padded = ((counts + TME - 1) // TME) * TME
    pend = jnp.cumsum(padded)
    pstart = pend - padded
    ids = jnp.arange(N_BUCKETS, dtype=jnp.int32)
    pos = rank + jnp.sum(jnp.where(bucket[None, :] == ids[:, None], pstart[:, None], 0), axis=0)
    n_tiles = pend[-1] // TME
    tiles = padded // TME
    items = (tiles + 1) // 2
    item_end = jnp.cumsum(items)
    item_start = item_end - items
    n_items = item_end[-1]
    step = jnp.arange(MOE_ITEMS, dtype=jnp.int32)
    live = step < n_items
    item = jnp.minimum(step, jnp.maximum(n_items - 1, 0))
    bkt = jnp.minimum(jnp.sum((item[:, None] >= item_end[None, :]).astype(jnp.int32), axis=1), N_BUCKETS - 1)
    hot = bkt[:, None] == ids[None, :]

    def of_bucket(v):
        return jnp.sum(jnp.where(hot, v[None, :], 0), axis=1)

    k = item - of_bucket(item_start)
    item_tile = of_bucket(pstart // TME) + 2 * k
    item_size = jnp.where(live, jnp.minimum(2, of_bucket(tiles) - 2 * k), 0)
    group, pair = bkt // PAIRS_PER_GROUP, bkt % PAIRS_PER_GROUP
    slot_a = jnp.asarray(SLOT_A_LOCAL, jnp.int32)
    slot_b = jnp.asarray(SLOT_B_LOCAL, jnp.int32)
    pair_hot = pair[:, None] == jnp.arange(PAIRS_PER_GROUP, dtype=jnp.int32)[None, :]
    item_a = group * EXPERTS_PER_GROUP + jnp.sum(jnp.where(pair_hot, slot_a[None, :], 0), axis=1)
    item_b = group * EXPERTS_PER_GROUP + jnp.sum(jnp.where(pair_hot, slot_b[None, :], 0), axis=1)
    return (pos, pend, counts, n_tiles.reshape(1), item_a.astype(jnp.int32), item_b.astype(jnp.int32),
            item_tile.astype(jnp.int32), item_size.astype(jnp.int32), n_items.astype(jnp.int32).reshape(1))


def _post_moe_kernel(split_out, pos_ref, f_ref, x1_ref, gate_ref, lng_ref, lnb_ref, *refs):
    out_refs, (fbuf, sem) = refs[:-2], refs[-2:]
    t = pl.program_id(0)
    tq = x1_ref.shape[0]
    n_steps = N_TOK // tq
    p_steps = N_PROMPT // tq

    def wait_tile(slot):
        pltpu.make_async_copy(f_ref.at[pl.ds(0, tq), :], fbuf.at[slot], sem.at[slot]).wait()

    @pl.when(t == 0)
    def _():
        def issue(r, carry):
            _row_copy(f_ref, pos_ref[r], fbuf.at[r // tq], r % tq, sem.at[r // tq]).start()
            return carry
        lax.fori_loop(0, (GATHER_SLOTS - 1) * tq, issue, 0, unroll=8)

    slot = t % GATHER_SLOTS
    wait_tile(slot)
    ahead = jnp.minimum(t + GATHER_SLOTS - 1, n_steps - 1)
    aslot = (t + GATHER_SLOTS - 1) % GATHER_SLOTS
    for r in range(tq):
        _row_copy(f_ref, pos_ref[ahead * tq + r], fbuf.at[aslot], r, sem.at[aslot]).start(priority=r % 2)
    subs = _sub_tiles(tq)
    y = [_layer_norm(ALPHA * x1_ref[s, :] + gate_ref[...] * fbuf[slot, s, :], lng_ref[...], lnb_ref[...])
         for s in subs]

    @pl.when(t == n_steps - 1)
    def _():
        for k in range(1, GATHER_SLOTS):
            wait_tile((t + k) % GATHER_SLOTS)
    if split_out:
        @pl.when(t < p_steps)
        def _():
            for s, y_s in zip(subs, y):
                out_refs[0][s, :] = y_s

        @pl.when(t >= p_steps)
        def _():
            for s, y_s in zip(subs, y):
                out_refs[1][s, :] = y_s
    else:
        for s, y_s in zip(subs, y):
            out_refs[0][s, :] = y_s


def _post_moe(layer, pos, f_sorted, x1, mod, ln_g, ln_b, split_out):
    tq = TQ
    p_steps = N_PROMPT // tq
    tok = pl.BlockSpec((tq, D_MODEL), lambda t, p: (t, 0))
    if split_out:
        out_specs = [pl.BlockSpec((tq, D_MODEL), lambda t, p: (jnp.minimum(t, p_steps - 1), 0)),
                     pl.BlockSpec((tq, D_MODEL), lambda t, p: (jnp.maximum(t - p_steps, 0), 0))]
        out_shape = [jax.ShapeDtypeStruct((N_PROMPT, D_MODEL), F32), jax.ShapeDtypeStruct((N_SAMPLE, D_MODEL), F32)]
    else:
        out_specs = [tok]
        out_shape = [jax.ShapeDtypeStruct((N_TOK, D_MODEL), F32)]
    mod_spec = pl.BlockSpec((None, None, None, 1, D_MODEL), lambda t, p: (layer, _mod_row(t, tq), 5, 0, 0))
    return pl.pallas_call(
        functools.partial(_post_moe_kernel, split_out),
        grid_spec=pltpu.PrefetchScalarGridSpec(
            num_scalar_prefetch=1, grid=(N_TOK // tq,),
            in_specs=[pl.BlockSpec(memory_space=pl.ANY), tok, mod_spec,
                      pl.BlockSpec((1, D_MODEL), lambda t, p: (0, 0)), pl.BlockSpec((1, D_MODEL), lambda t, p: (0, 0))],
            out_specs=out_specs,
            scratch_shapes=[pltpu.VMEM((GATHER_SLOTS, tq, D_MODEL), F32),
                            pltpu.SemaphoreType.DMA((GATHER_SLOTS,))]),
        out_shape=out_shape,
        compiler_params=_params(("arbitrary",)),
        name="post_moe",
    )(pos, f_sorted, x1, mod, ln_g, ln_b)


def _ffn(layer, o_p, o_s, x_list, mod, w_out, ln_g, ln_b, rw_t, rb, w_gate, w_up, w_down, split_out):
    x1, rows, route, counts = _post_attn(layer, o_p, o_s, x_list, mod, w_out, ln_g[layer, 0][None],
                                         ln_b[layer, 0][None], rw_t, rb)
    pos, pend, counts, n_tiles, item_a, item_b, item_tile, item_size, n_items = _plan(route, counts)
    rows_sorted = _dispatch(pos, pend, counts, n_tiles, rows)
    f_sorted = _moe(layer, item_a, item_b, item_tile, item_size, n_items, n_tiles, rows_sorted, w_gate, w_up, w_down)
    return _post_moe(layer, pos, f_sorted, x1, mod, ln_g[layer, 1][None], ln_b[layer, 1][None], split_out)


def kernel(x_prompt, x_sample, c, cache_mla_ckv, cache_mla_kpe, cache_swa_k, cache_swa_v, cache_gqa_k, cache_gqa_v, c_ctx, w_mod, b_mod, ln_g, ln_b, w_in_ab, mla_q_norm, mla_w_uq, mla_kv_norm, mla_w_ukv, swa_sink, w_out_ab, w_in_c, gqa_q_norm, gqa_k_norm, w_out_c, router_w, router_bias, exp_w_gate, exp_w_up, exp_w_down):
    xp = x_prompt.reshape(N_PROMPT, D_MODEL)
    xs = x_sample.reshape(N_SAMPLE, D_MODEL)
    cond = jnp.concatenate([c_ctx[None], c, jnp.zeros((MOD_ROWS - 1 - DEC_BATCH, D_MODEL), F32)], axis=0)
    mod = _modulation(cond, w_mod, b_mod).reshape(DEPTH, MOD_ROWS, 6, 1, D_MODEL)
    tabs = _rope_tables()
    rw_t = router_w.T
    rb = router_bias.reshape(N_EXPERTS, 1)

    w = w_in_ab[0]
    pad = jnp.zeros((D_MODEL, AB_COLS - w.shape[1]), F32)
    kpe0 = Q_LORA + KV_LORA
    w_in = jnp.concatenate([w[:, :kpe0], w[:, kpe0 + ROPE_A:], w[:, kpe0:kpe0 + ROPE_A], pad], axis=1).astype(BF16)
    w_uq, w_kn, w_v, place = _mla_weights(mla_w_uq[0], mla_w_ukv[0])
    qa, ka, va, qb, ckv_p, kpe_p, kbt_p, vbt_p, kb_s, vb_s = _inproj_ab(
        xp, xs, mod, w_in, mla_q_norm[0][None], mla_kv_norm[0][None], w_uq, w_kn, w_v, place, tabs)
    sink = swa_sink[0]
    o_p = _attn_ab_prompt(sink, qa, ka, va, qb, kbt_p, vbt_p)
    kpe_ctx = jnp.pad(cache_mla_kpe[:, 0].reshape(-1, ROPE_A), ((0, 0), (0, LANES - ROPE_A)))
    o_s = _attn_ab_sample(sink, qa, ka, va, qb, kb_s, vb_s,
                          cache_mla_ckv[:, 0].reshape(-1, KV_LORA), kpe_ctx,
                          cache_swa_k[:, 0].reshape(-1, KVH_B * HD), cache_swa_v[:, 0].reshape(-1, KVH_B * HD),
                          w_kn, w_v, place)
    (x2,) = _ffn(0, o_p, o_s, [xp, xs], mod, w_out_ab[0].astype(BF16), ln_g, ln_b, rw_t, rb,
                 exp_w_gate, exp_w_up, exp_w_down, split_out=False)

    g_full = jnp.concatenate([jnp.tile(gqa_q_norm[0], H_C), jnp.tile(gqa_k_norm[0], KVH_C)])[None]
    qc, kc_p, vc_p, kc_s, vc_s = _inproj_c(x2, mod, w_in_c[0].astype(BF16), g_full, tabs)
    oc_p = _attn_c_prompt(qc, kc_p, vc_p)
    oc_s = _attn_c_sample(qc, kc_s, vc_s, cache_gqa_k[:, 0].reshape(-1, KVH_C * HD),
                          cache_gqa_v[:, 0].reshape(-1, KVH_C * HD))
    y_p, y_s = _ffn(1, oc_p, oc_s, [x2], mod, w_out_c[0].astype(BF16), ln_g, ln_b, rw_t, rb,
                    exp_w_gate, exp_w_up, exp_w_down, split_out=True)

    y_prompt = y_p.reshape(BATCH, SEQ, D_MODEL)
    y_sample = y_s.reshape(DEC_BATCH, DEC_SEQ, D_MODEL)
    new_ckv = ckv_p.reshape(BATCH, 1, SEQ, KV_LORA)
    new_kpe = kpe_p.reshape(BATCH, 1, SEQ, ROPE_A)
    new_swk = kbt_p.reshape(BATCH, 1, KVH_B, HD, SEQ).transpose(0, 1, 4, 2, 3)
    new_swv = vbt_p.reshape(BATCH, 1, KVH_B, HD, SEQ).transpose(0, 1, 4, 2, 3)
    new_gk = kc_p.reshape(BATCH, 1, SEQ, KVH_C, HD)
    new_gv = vc_p.reshape(BATCH, 1, SEQ, KVH_C, HD)
    return y_prompt, y_sample, new_ckv, new_kpe, new_swk, new_swv, new_gk, new_gv
```

```python
import functools

import numpy as np
import jax
import jax.numpy as jnp
from jax import lax
from jax.experimental import pallas as pl
from jax.experimental.pallas import tpu as pltpu

D_MODEL = 1024
BATCH = 32
SEQ = 256
DEPTH = 2
DEC_BATCH = 4
DEC_SEQ = 1024
PAST_LEN = 256
GRID_W = 64
ROPE_THETA = 10000.0
WINDOW = 128
HD = 64
H_A = 8
NOPE_A = 64
ROPE_A = 32
V_A = 64
Q_LORA = 256
KV_LORA = 128
H_B = 8
KVH_B = 2
H_C = 16
KVH_C = 4
N_EXPERTS = 16
N_GROUPS = 4
EXPERTS_PER_GROUP = 4
D_EXPERT = 512
ALPHA = (2 * DEPTH) ** 0.25
NEG_INF = -1e30
EPS = 1e-6

LANES = 128
N_PROMPT = BATCH * SEQ
N_SAMPLE = DEC_BATCH * DEC_SEQ
N_TOK = N_PROMPT + N_SAMPLE
TM = 256
TP = 1024
TP_AB = 512
TQ = 512
N_TILES = N_TOK // TM
BQ_S = 256
LOG2E = 1.4426950408889634
MLA_STACK_PROMPT = 4
MLA_STACK_SAMPLE = 4
GQA_STACK_PROMPT = 4
GQA_STACK_SAMPLE = 2
MOD_ROWS = 8

PAIRS_PER_GROUP = 6
N_BUCKETS = N_GROUPS * PAIRS_PER_GROUP
SLOT_A_LOCAL = (0, 2, 2, 3, 3, 3)
SLOT_B_LOCAL = (1, 1, 0, 0, 1, 2)
TME = 256
MOE_TILES = -(-(N_TOK + N_BUCKETS * (TME - 1)) // TME)
MOE_ROWS = MOE_TILES * TME
ITEM_TILES = 3
MOE_ITEMS = (MOE_TILES + (ITEM_TILES - 1) * N_BUCKETS) // ITEM_TILES

BUCKET_ROWS = 32
ROW_W = D_MODEL + LANES
STAGE_SLOTS = 3
GATHER_SLOTS = 3

COL_QB = Q_LORA + KV_LORA
COL_KB = COL_QB + H_B * HD
COL_VB = COL_KB + KVH_B * HD
COL_KPE = COL_VB + KVH_B * HD
AB_COLS = COL_KPE + LANES
C_NORM_COLS = H_C * HD + KVH_C * HD
NORM_BLOCK = 256

BF16 = jnp.bfloat16
F32 = jnp.float32
VMEM_LIMIT = 52 * 1024 * 1024


def _mod_row(t, tile=TM):
    p_tiles = N_PROMPT // tile
    return jnp.where(t < p_tiles, 0, 1 + (t - p_tiles) // (DEC_SEQ // tile))


def _mod_spec(layer, chunk, tile=TM):
    return pl.BlockSpec((None, None, None, 1, D_MODEL), lambda t: (layer, _mod_row(t, tile), chunk, 0, 0))


def _const_spec(shape):
    nd = len(shape)
    return pl.BlockSpec(shape, lambda *_: (0,) * nd)


def _split_specs(tile=TM):
    p_tiles = N_PROMPT // tile
    return [
        pl.BlockSpec((tile, D_MODEL), lambda t: (jnp.minimum(t, p_tiles - 1), 0)),
        pl.BlockSpec((tile, D_MODEL), lambda t: (jnp.maximum(t - p_tiles, 0), 0)),
    ]


def _group_spec(tile, width, prompt):
    p_tiles = N_PROMPT // tile
    if prompt:
        return pl.BlockSpec((tile, width), lambda t: (jnp.minimum(t, p_tiles - 1), 0))
    return pl.BlockSpec((tile, width), lambda t: (jnp.maximum(t - p_tiles, 0), 0))


def _pick(t, a_ref, b_ref):
    return jnp.where(t < N_PROMPT // a_ref.shape[0], a_ref[...], b_ref[...])


def _sub_tiles(rows):
    return [slice(i * TM, (i + 1) * TM) for i in range(rows // TM)]


def _params(sem):
    return pltpu.CompilerParams(dimension_semantics=sem, vmem_limit_bytes=VMEM_LIMIT)


def _mod_kernel(cond_ref, w_ref, b_ref, o_ref):
    c = cond_ref[...]
    s = (c / (1.0 + jnp.exp(-c))).astype(BF16)
    o_ref[...] = jnp.dot(s, w_ref[...].astype(BF16), preferred_element_type=F32) + b_ref[...]


def _modulation(cond, w_mod, b_mod):
    tn = 1536
    return pl.pallas_call(
        _mod_kernel,
        grid=(DEPTH, 6 * D_MODEL // tn),
        in_specs=[
            pl.BlockSpec((MOD_ROWS, D_MODEL), lambda l, j: (0, 0)),
            pl.BlockSpec((None, D_MODEL, tn), lambda l, j: (l, 0, j)),
            pl.BlockSpec((None, 1, tn), lambda l, j: (l, 0, j)),
        ],
        out_specs=pl.BlockSpec((None, MOD_ROWS, tn), lambda l, j: (l, 0, j)),
        out_shape=jax.ShapeDtypeStruct((DEPTH, MOD_ROWS, 6 * D_MODEL), F32),
        compiler_params=_params(("arbitrary", "arbitrary")),
        name="modulation",
    )(cond, w_mod, b_mod.reshape(DEPTH, 1, 6 * D_MODEL))


def _grid_cos_sin(half):
    n_freq = half // 2
    rows = DEC_SEQ // GRID_W
    row = jnp.repeat(jnp.arange(rows, dtype=F32), GRID_W)
    col = jnp.tile(jnp.arange(GRID_W, dtype=F32), rows)
    inv = 1.0 / (ROPE_THETA ** (jnp.arange(n_freq, dtype=F32) / n_freq))
    ang = jnp.concatenate([row[:, None] * inv, col[:, None] * inv], axis=-1)
    return jnp.cos(ang), jnp.sin(ang)


def _rope_tables():
    tabs = []
    for half in (HD // 2, ROPE_A // 2):
        cos, sin = _grid_cos_sin(half)
        reps = LANES // (2 * half)
        tabs.append(jnp.tile(jnp.concatenate([cos, cos], axis=-1), (1, reps)))
        tabs.append(jnp.tile(jnp.concatenate([-sin, sin], axis=-1), (1, reps)))
    half = ROPE_A // 2
    one = jnp.ones((DEC_SEQ, NOPE_A), F32)
    zero = jnp.zeros((DEC_SEQ, NOPE_A), F32)
    pad1 = jnp.ones((DEC_SEQ, LANES - NOPE_A - ROPE_A), F32)
    pad0 = jnp.zeros((DEC_SEQ, LANES - NOPE_A - ROPE_A), F32)
    zh = jnp.zeros((DEC_SEQ, half), F32)
    tabs.append(jnp.concatenate([one, cos, cos, pad1], axis=-1))
    tabs.append(jnp.concatenate([zero, -sin, zh, pad0], axis=-1))
    tabs.append(jnp.concatenate([zero, zh, sin, pad0], axis=-1))
    return tuple(tabs)


def _rope(x, c, s, half):
    w = x.shape[1]
    reps = w // LANES
    if reps > 1:
        c = jnp.concatenate([c] * reps, axis=1)
        s = jnp.concatenate([s] * reps, axis=1)
    ahead = pltpu.roll(x, w - half, 1)
    behind = pltpu.roll(x, half, 1)
    lane = lax.broadcasted_iota(jnp.int32, x.shape, 1)
    swapped = jnp.where((lane & (2 * half - 1)) < half, ahead, behind)
    return x * c + swapped * s


def _rope_specs(n, tile):
    p_tiles = N_PROMPT // tile

    def idx(t):
        return (jnp.maximum(t - p_tiles, 0) % (DEC_SEQ // tile), 0)
    return [pl.BlockSpec((tile, LANES), idx)] * n


def _rms(x, g_ref):
    return x * lax.rsqrt(jnp.mean(x * x, axis=-1, keepdims=True) + EPS) * g_ref[...]


def _inproj_ab_kernel(xp_ref, xs_ref, shift_ref, scale_ref, w_in_ref, qn_ref, kvn_ref, w_uq_ref, w_kn_ref, w_v_ref,
                      place_ref, c64_ref, s64_ref, c32_ref, s32_ref, ca_ref, sa_ref, sb_ref,
                      qa_ref, ka_ref, va_ref, qb_ref, ckv_p_ref, kpe_p_ref, kbt_p_ref, vbt_p_ref, kb_s_ref, vb_s_ref):
    t = pl.program_id(0)
    x = _pick(t, xp_ref, xs_ref)
    subs = _sub_tiles(x.shape[0])
    h = [(x[s] * (1.0 + scale_ref[...]) + shift_ref[...]).astype(BF16) for s in subs]
    z = [jnp.dot(h_s, w_in_ref[...], preferred_element_type=F32) for h_s in h]
    ql = [_rms(z_s[:, :Q_LORA], qn_ref).astype(BF16) for z_s in z]
    qa = [jnp.dot(ql_s, w_uq_ref[...], preferred_element_type=F32) for ql_s in ql]
    ckv = [_rms(z_s[:, Q_LORA:Q_LORA + KV_LORA], kvn_ref) for z_s in z]
    k_nope = [jnp.dot(c_s.astype(BF16), w_kn_ref[...], preferred_element_type=F32) for c_s in ckv]
    va = [jnp.dot(c_s.astype(BF16), w_v_ref[...], preferred_element_type=F32).astype(BF16) for c_s in ckv]
    for s, va_s in zip(subs, va):
        va_ref[s, :] = va_s
    qb = [z_s[:, COL_QB:COL_KB] for z_s in z]
    kb = [z_s[:, COL_KB:COL_VB] for z_s in z]
    vb = [z_s[:, COL_VB:COL_KPE] for z_s in z]
    kpe = [z_s[:, COL_KPE:] for z_s in z]

    def finish(qa_out, qb_out, kpe_out):
        k_rope = [jnp.dot(kpe_s.astype(BF16), place_ref[...], preferred_element_type=F32) for kpe_s in kpe_out]
        for i, s in enumerate(subs):
            qa_ref[s, :] = qa_out[i].astype(BF16)
            qb_ref[s, :] = qb_out[i].astype(BF16)
            ka_ref[s, :] = (k_nope[i] + k_rope[i]).astype(BF16)

    p_tiles = N_PROMPT // x.shape[0]

    @pl.when(t < p_tiles)
    def _():
        finish(qa, qb, kpe)
        for i, s in enumerate(subs):
            ckv_p_ref[s, :] = ckv[i]
            kpe_p_ref[s, :] = kpe[i][:, :ROPE_A]
            kbt_p_ref[i] = kb[i].T
            vbt_p_ref[i] = vb[i].T

    @pl.when(t >= p_tiles)
    def _():
        w = qa[0].shape[1]
        reps = w // LANES
        half = ROPE_A // 2
        qa_rot, qb_rot, kb_rot, kpe_rot = [], [], [], []
        for i, s in enumerate(subs):
            ca = jnp.concatenate([ca_ref[s, :]] * reps, axis=1)
            sa = jnp.concatenate([sa_ref[s, :]] * reps, axis=1)
            sb = jnp.concatenate([sb_ref[s, :]] * reps, axis=1)
            qa_rot.append(qa[i] * ca + pltpu.roll(qa[i], w - half, 1) * sa + pltpu.roll(qa[i], half, 1) * sb)
            qb_rot.append(_rope(qb[i], c64_ref[s, :], s64_ref[s, :], HD // 2))
            kb_rot.append(_rope(kb[i], c64_ref[s, :], s64_ref[s, :], HD // 2))
            kpe_rot.append(_rope(kpe[i], c32_ref[s, :], s32_ref[s, :], ROPE_A // 2))
        finish(qa_rot, qb_rot, kpe_rot)
        for i, s in enumerate(subs):
            kb_s_ref[s, :] = kb_rot[i]
            vb_s_ref[s, :] = vb[i]


def _mla_weights(w_uq, w_ukv):
    pad = LANES - NOPE_A - ROPE_A
    uq = w_uq.reshape(Q_LORA, H_A, NOPE_A + ROPE_A)
    uq = jnp.pad(uq, ((0, 0), (0, 0), (0, pad))).reshape(Q_LORA, H_A * LANES)
    ukv = w_ukv.reshape(KV_LORA, H_A, NOPE_A + V_A)
    kn = jnp.pad(ukv[:, :, :NOPE_A], ((0, 0), (0, 0), (0, LANES - NOPE_A))).reshape(KV_LORA, H_A * LANES)
    wv = ukv[:, :, NOPE_A:].reshape(KV_LORA, H_A * V_A)
    src = np.arange(LANES)[:, None]
    dst = np.arange(H_A * LANES)[None, :] % LANES
    place = (src < ROPE_A) & (dst == src + NOPE_A)
    return uq.astype(BF16), kn.astype(BF16), wv.astype(BF16), jnp.asarray(place, BF16)


def _inproj_ab(xp, xs, mod, w_in, q_norm, kv_norm, w_uq, w_kn, w_v, place, tabs):
    widths = [(H_A * LANES, BF16), (H_A * LANES, BF16), (H_A * V_A, BF16), (H_B * HD, BF16)]
    tp = TP_AB
    out_specs = [pl.BlockSpec((tp, w), lambda t: (t, 0)) for w, _ in widths]
    out_shape = [jax.ShapeDtypeStruct((N_TOK, w), dt) for w, dt in widths]
    for w in (KV_LORA, ROPE_A):
        out_specs.append(_group_spec(tp, w, prompt=True))
        out_shape.append(jax.ShapeDtypeStruct((N_PROMPT, w), F32))
    assert SEQ == TM
    p_tiles = N_PROMPT // tp
    for _ in range(2):
        out_specs.append(pl.BlockSpec((tp // SEQ, KVH_B * HD, SEQ), lambda t: (jnp.minimum(t, p_tiles - 1), 0, 0)))
        out_shape.append(jax.ShapeDtypeStruct((BATCH, KVH_B * HD, SEQ), F32))
    for w in (KVH_B * HD, KVH_B * HD):
        out_specs.append(_group_spec(tp, w, prompt=False))
        out_shape.append(jax.ShapeDtypeStruct((N_SAMPLE, w), F32))
    return pl.pallas_call(
        _inproj_ab_kernel,
        grid=(N_TOK // tp,),
        in_specs=_split_specs(tp) + [_mod_spec(0, 0, tp), _mod_spec(0, 1, tp), _const_spec((D_MODEL, AB_COLS)),
                                     _const_spec((1, Q_LORA)), _const_spec((1, KV_LORA)),
                                     _const_spec((Q_LORA, H_A * LANES)), _const_spec((KV_LORA, H_A * LANES)),
                                     _const_spec((KV_LORA, H_A * V_A)), _const_spec((LANES, H_A * LANES))]
        + _rope_specs(7, tp),
        out_specs=out_specs,
        out_shape=out_shape,
        compiler_params=_params(("arbitrary",)),
        name="inproj_ab",
    )(xp, xs, mod, mod, w_in, q_norm, kv_norm, w_uq, w_kn, w_v, place, *tabs)


def _inproj_c_kernel(x_ref, shift_ref, scale_ref, w_in_ref, g_ref, ones_ref, c64_ref, s64_ref,
                     q_ref, k_p_ref, v_p_ref, k_s_ref, v_s_ref):
    t = pl.program_id(0)
    subs = _sub_tiles(x_ref.shape[0])
    h = [(x_ref[s, :] * (1.0 + scale_ref[...]) + shift_ref[...]).astype(BF16) for s in subs]
    z = [jnp.dot(h_s, w_in_ref[...], preferred_element_type=F32) for h_s in h]
    qk = [z_s[:, :C_NORM_COLS] for z_s in z]
    sq = [qk_s * qk_s for qk_s in qk]
    sq_hi = [sq_s.astype(BF16) for sq_s in sq]
    sq_lo = [(sq_s - hi_s.astype(F32)).astype(BF16) for sq_s, hi_s in zip(sq, sq_hi)]
    ss = []
    for hi_s, lo_s in zip(sq_hi, sq_lo):
        blocks = []
        for j in range(C_NORM_COLS // NORM_BLOCK):
            sl = slice(j * NORM_BLOCK, (j + 1) * NORM_BLOCK)
            blocks.append(jnp.dot(hi_s[:, sl], ones_ref[...], preferred_element_type=F32)
                          + jnp.dot(lo_s[:, sl], ones_ref[...], preferred_element_type=F32))
        ss.append(jnp.concatenate(blocks, axis=1))
    qk = [qk_s * lax.rsqrt(ss_s * (1.0 / HD) + EPS) * g_ref[...] for qk_s, ss_s in zip(qk, ss)]
    p_tiles = N_PROMPT // x_ref.shape[0]

    @pl.when(t < p_tiles)
    def _():
        for s, qk_s, z_s in zip(subs, qk, z):
            q_ref[s, :] = qk_s[:, :H_C * HD].astype(BF16)
            k_p_ref[s, :] = qk_s[:, H_C * HD:]
            v_p_ref[s, :] = z_s[:, C_NORM_COLS:]

    @pl.when(t >= p_tiles)
    def _():
        for s, qk_s, z_s in zip(subs, qk, z):
            q_ref[s, :] = _rope(qk_s[:, :H_C * HD], c64_ref[s, :], s64_ref[s, :], HD // 2).astype(BF16)
            k_s_ref[s, :] = _rope(qk_s[:, H_C * HD:], c64_ref[s, :], s64_ref[s, :], HD // 2)
            v_s_ref[s, :] = z_s[:, C_NORM_COLS:]


def _inproj_c(x, mod, w_in, g_full, tabs):
    def tok(w):
        return pl.BlockSpec((TP, w), lambda t: (t, 0))
    kv_w = KVH_C * HD
    head = np.arange(NORM_BLOCK) // HD
    ones = jnp.asarray(head[:, None] == head[None, :], BF16)
    return pl.pallas_call(
        _inproj_c_kernel,
        grid=(N_TOK // TP,),
        in_specs=[tok(D_MODEL), _mod_spec(1, 0, TP), _mod_spec(1, 1, TP),
                  _const_spec((D_MODEL, C_NORM_COLS + KVH_C * HD)),
                  _const_spec((1, C_NORM_COLS)), _const_spec((NORM_BLOCK, NORM_BLOCK))] + _rope_specs(2, TP),
        out_specs=[tok(H_C * HD), _group_spec(TP, kv_w, True), _group_spec(TP, kv_w, True),
                   _group_spec(TP, kv_w, False), _group_spec(TP, kv_w, False)],
        out_shape=[jax.ShapeDtypeStruct((N_TOK, H_C * HD), BF16),
                   jax.ShapeDtypeStruct((N_PROMPT, kv_w), F32), jax.ShapeDtypeStruct((N_PROMPT, kv_w), F32),
                   jax.ShapeDtypeStruct((N_SAMPLE, kv_w), F32), jax.ShapeDtypeStruct((N_SAMPLE, kv_w), F32)],
        compiler_params=_params(("arbitrary",)),
        name="inproj_c",
    )(x, mod, mod, w_in, g_full, ones, *tabs[:2])


def _qk(q, k):
    return lax.dot_general(q, k, (((1,), (1,)), ((), ())), preferred_element_type=F32)


def _softmax_pv(scores, values, scale, sink=None, kv_t=False):
    c = scale * LOG2E
    m = scores[0].max(axis=-1, keepdims=True)
    for s in scores[1:]:
        m = jnp.maximum(m, s.max(axis=-1, keepdims=True))
    m = m * c
    if sink is not None:
        sink = sink * LOG2E
        m = jnp.maximum(m, sink)
    den = None
    out = None
    for s, v in zip(scores, values):
        p = jnp.exp2(s * c - m)
        d = p.sum(axis=-1, keepdims=True)
        p = p.astype(BF16)
        if isinstance(v, (list, tuple)):
            rows = p.shape[0] // len(v)
            o = jnp.concatenate([_qk(p[b * rows:(b + 1) * rows], v_b) if kv_t
                                 else jnp.dot(p[b * rows:(b + 1) * rows], v_b, preferred_element_type=F32)
                                 for b, v_b in enumerate(v)], axis=0)
        else:
            o = _qk(p, v) if kv_t else jnp.dot(p, v, preferred_element_type=F32)
        den = d if den is None else den + d
        out = o if out is None else out + o
    if sink is not None:
        den = den + jnp.exp2(sink - m)
    return out * (1.0 / den)


def _mla_heads(qa_ref, ka_refs, va_refs, o_ref, stack):
    scale = (NOPE_A + ROPE_A) ** -0.5
    rows = qa_ref.shape[0]
    for h0 in range(0, H_A, stack):
        heads = range(h0, h0 + stack)
        scores = [jnp.concatenate([_qk(qa_ref[:, h * LANES:(h + 1) * LANES], ka_ref[:, h * LANES:(h + 1) * LANES])
                                   for h in heads], axis=0) for ka_ref in ka_refs]
        values = [[va_ref[:, h * V_A:(h + 1) * V_A] for h in heads] for va_ref in va_refs]
        o = _softmax_pv(scores, values, scale)
        o = jnp.concatenate([o[b * rows:(b + 1) * rows] for b in range(stack)], axis=1)
        o_ref[:, h0 * V_A:(h0 + stack) * V_A] = o.astype(o_ref.dtype)


def _gqa_groups(q_ref, col0, n_q, ks, vs, mask=None, sink=None, kv_t=False):
    rows = q_ref.shape[0]
    n_groups = len(ks)
    n_seg = len(ks[0])
    per_group = []
    for g in range(n_groups):
        base = col0 + g * n_q * HD
        q = jnp.concatenate([q_ref[:, base + i * HD:base + (i + 1) * HD] for i in range(n_q)], axis=0)
        if kv_t:
            per_group.append([jnp.dot(q, k, preferred_element_type=F32) for k in ks[g]])
        else:
            per_group.append([_qk(q, k) for k in ks[g]])
    scores = [jnp.concatenate([per_group[g][j] for g in range(n_groups)], axis=0) for j in range(n_seg)]
    values = [[vs[g][j] for g in range(n_groups)] for j in range(n_seg)]
    if mask is not None:
        scores[0] = jnp.where(mask, scores[0], NEG_INF)
    o = _softmax_pv(scores, values, HD ** -0.5, sink, kv_t)
    return jnp.concatenate([o[i * rows:(i + 1) * rows] for i in range(n_groups * n_q)], axis=1)


def _stacked_sink(sink_ref, h0, n_q, rows):
    head = lax.broadcasted_iota(jnp.int32, (n_q * rows, 1), 0) // rows
    col = jnp.full((n_q * rows, 1), sink_ref[h0], F32)
    for g in range(1, n_q):
        col = jnp.where(head == g, sink_ref[h0 + g], col)
    return col


def _attn_ab_prompt_kernel(sink_ref, qa_ref, ka_ref, va_ref, qb_ref, kbt_ref, vbt_ref, o_ref):
    _mla_heads(qa_ref, [ka_ref], [va_ref], o_ref, MLA_STACK_PROMPT)
    kbt = kbt_ref[...].astype(BF16)
    vbt = vbt_ref[...].astype(BF16)
    sls = [slice(kh * HD, (kh + 1) * HD) for kh in range(KVH_B)]
    o = _gqa_groups(qb_ref, 0, H_B // KVH_B, [[kbt[sl, :]] for sl in sls], [[vbt[sl, :]] for sl in sls],
                    sink=_stacked_sink(sink_ref, 0, H_B, SEQ), kv_t=True)
    o_ref[:, H_A * V_A:] = o.astype(o_ref.dtype)


def _attn_ab_prompt(sink, qa, ka, va, qb, kbt, vbt):
    def blk(w):
        return pl.BlockSpec((SEQ, w), lambda b: (b, 0))
    tblk = pl.BlockSpec((None, KVH_B * HD, SEQ), lambda b: (b, 0, 0))
    return pl.pallas_call(
        _attn_ab_prompt_kernel,
        grid=(BATCH,),
        in_specs=[pl.BlockSpec(memory_space=pltpu.SMEM), blk(H_A * LANES), blk(H_A * LANES), blk(H_A * V_A),
                  blk(H_B * HD), tblk, tblk],
        out_specs=blk(D_MODEL),
        out_shape=jax.ShapeDtypeStruct((N_PROMPT, H_A * V_A + H_B * HD), BF16),
        compiler_params=_params(("arbitrary",)),
        name="attn_ab_prompt",
    )(sink, qa, ka, va, qb, kbt, vbt)


def _attn_ab_sample_kernel(sink_ref, qa_ref, ka_ref, va_ref, qb_ref, kb_ref, vb_ref,
                           ckv_ctx_ref, kpe_ctx_ref, kb_ctx_ref, vb_ctx_ref, w_kn_ref, w_v_ref, place_ref,
                           o_ref, ka_ctx_ref, va_ctx_ref):
    i = pl.program_id(1)

    @pl.when(i == 0)
    def _():
        ckv16 = ckv_ctx_ref[...].astype(BF16)
        k_nope = jnp.dot(ckv16, w_kn_ref[...], preferred_element_type=F32)
        k_rope = jnp.dot(kpe_ctx_ref[...].astype(BF16), place_ref[...], preferred_element_type=F32)
        ka_ctx_ref[...] = (k_nope + k_rope).astype(BF16)
        va_ctx_ref[...] = jnp.dot(ckv16, w_v_ref[...], preferred_element_type=F32).astype(BF16)

    _mla_heads(qa_ref, [ka_ref, ka_ctx_ref], [va_ref, va_ctx_ref], o_ref, MLA_STACK_SAMPLE)

    n_win = BQ_S + 2 * WINDOW
    start = pl.multiple_of(jnp.clip(i * BQ_S - WINDOW, 0, DEC_SEQ - n_win), WINDOW)
    row = lax.broadcasted_iota(jnp.int32, (H_B * BQ_S, n_win), 0)
    qpos = i * BQ_S + (row & (BQ_S - 1))
    kpos = start + lax.broadcasted_iota(jnp.int32, (H_B * BQ_S, n_win), 1)
    in_band = jnp.abs(qpos - kpos) <= WINDOW
    kwin = kb_ref[pl.ds(start, n_win), :].astype(BF16)
    vwin = vb_ref[pl.ds(start, n_win), :].astype(BF16)
    kctx = kb_ctx_ref[...].astype(BF16)
    vctx = vb_ctx_ref[...].astype(BF16)
    sls = [slice(kh * HD, (kh + 1) * HD) for kh in range(KVH_B)]
    o = _gqa_groups(qb_ref, 0, H_B // KVH_B, [[kwin[:, sl], kctx[:, sl]] for sl in sls],
                    [[vwin[:, sl], vctx[:, sl]] for sl in sls],
                    mask=in_band, sink=_stacked_sink(sink_ref, 0, H_B, BQ_S))
    o_ref[:, H_A * V_A:] = o.astype(o_ref.dtype)


def _attn_ab_sample(sink, qa, ka, va, qb, kb, vb, ckv_ctx, kpe_ctx, kb_ctx, vb_ctx, w_kn, w_v, place):
    qoff = N_PROMPT // BQ_S
    nq = DEC_SEQ // BQ_S
    boff = N_PROMPT // DEC_SEQ

    def qblk(w):
        return pl.BlockSpec((BQ_S, w), lambda b, i: (qoff + b * nq + i, 0))

    def bblk(w):
        return pl.BlockSpec((DEC_SEQ, w), lambda b, i: (boff + b, 0))

    def sblk(w):
        return pl.BlockSpec((DEC_SEQ, w), lambda b, i: (b, 0))

    def cblk(w):
        return pl.BlockSpec((PAST_LEN, w), lambda b, i: (b, 0))

    return pl.pallas_call(
        _attn_ab_sample_kernel,
        grid=(DEC_BATCH, nq),
        in_specs=[pl.BlockSpec(memory_space=pltpu.SMEM), qblk(H_A * LANES), bblk(H_A * LANES), bblk(H_A * V_A),
                  qblk(H_B * HD), sblk(KVH_B * HD), sblk(KVH_B * HD),
                  cblk(KV_LORA), cblk(LANES), cblk(KVH_B * HD), cblk(KVH_B * HD),
                  _const_spec((KV_LORA, H_A * LANES)), _const_spec((KV_LORA, H_A * V_A)),
                  _const_spec((LANES, H_A * LANES))],
        out_specs=pl.BlockSpec((BQ_S, D_MODEL), lambda b, i: (b * nq + i, 0)),
        out_shape=jax.ShapeDtypeStruct((N_SAMPLE, H_A * V_A + H_B * HD), BF16),
        scratch_shapes=[pltpu.VMEM((PAST_LEN, H_A * LANES), BF16), pltpu.VMEM((PAST_LEN, H_A * V_A), BF16)],
        compiler_params=_params(("arbitrary", "arbitrary")),
        name="attn_ab_sample",
    )(sink, qa, ka, va, qb, kb, vb, ckv_ctx, kpe_ctx, kb_ctx, vb_ctx, w_kn, w_v, place)


def _gqa_heads(q_ref, k_list, v_list, o_ref, stack):
    gq = H_C // KVH_C
    for kh0 in range(0, KVH_C, stack):
        sls = [slice(kh * HD, (kh + 1) * HD) for kh in range(kh0, kh0 + stack)]
        o = _gqa_groups(q_ref, kh0 * gq * HD, gq, [[k[:, sl] for k in k_list] for sl in sls],
                        [[v[:, sl] for v in v_list] for sl in sls])
        o_ref[:, kh0 * gq * HD:(kh0 + stack) * gq * HD] = o.astype(o_ref.dtype)


def _attn_c_prompt_kernel(q_ref, k_ref, v_ref, o_ref):
    _gqa_heads(q_ref, [k_ref[...].astype(BF16)], [v_ref[...].astype(BF16)], o_ref, GQA_STACK_PROMPT)


def _attn_c_prompt(q, k, v):
    def blk(w):
        return pl.BlockSpec((SEQ, w), lambda b: (b, 0))
    return pl.pallas_call(
        _attn_c_prompt_kernel,
        grid=(BATCH,),
        in_specs=[blk(H_C * HD), blk(KVH_C * HD), blk(KVH_C * HD)],
        out_specs=blk(H_C * HD),
        out_shape=jax.ShapeDtypeStruct((N_PROMPT, H_C * HD), BF16),
        compiler_params=_params(("arbitrary",)),
        name="attn_c_prompt",
    )(q, k, v)


def _attn_c_sample_kernel(q_ref, k_ref, v_ref, kc_ref, vc_ref, o_ref):
    _gqa_heads(q_ref, [k_ref[...].astype(BF16), kc_ref[...].astype(BF16)],
               [v_ref[...].astype(BF16), vc_ref[...].astype(BF16)], o_ref, GQA_STACK_SAMPLE)


def _attn_c_sample(q, k, v, k_ctx, v_ctx):
    qoff = N_PROMPT // BQ_S
    nq = DEC_SEQ // BQ_S
    w = KVH_C * HD
    return pl.pallas_call(
        _attn_c_sample_kernel,
        grid=(DEC_BATCH, nq),
        in_specs=[pl.BlockSpec((BQ_S, H_C * HD), lambda b, i: (qoff + b * nq + i, 0)),
                  pl.BlockSpec((DEC_SEQ, w), lambda b, i: (b, 0)),
                  pl.BlockSpec((DEC_SEQ, w), lambda b, i: (b, 0)),
                  pl.BlockSpec((PAST_LEN, w), lambda b, i: (b, 0)),
                  pl.BlockSpec((PAST_LEN, w), lambda b, i: (b, 0))],
        out_specs=pl.BlockSpec((BQ_S, H_C * HD), lambda b, i: (b * nq + i, 0)),
        out_shape=jax.ShapeDtypeStruct((N_SAMPLE, H_C * HD), BF16),
        compiler_params=_params(("arbitrary", "arbitrary")),
        name="attn_c_sample",
    )(q, k, v, k_ctx, v_ctx)


def _layer_norm(y, g, b):
    mu = jnp.mean(y, axis=-1, keepdims=True)
    yc = y - mu
    var = jnp.mean(yc * yc, axis=-1, keepdims=True)
    return yc * lax.rsqrt(var + EPS) * g + b


def _route(sel, aff):
    def row(a, j):
        return a[j:j + 1, :]

    scores = []
    for g in range(N_GROUPS):
        a0, a1, a2, a3 = (row(sel, 4 * g + j) for j in range(4))
        hi01, lo01 = jnp.maximum(a0, a1), jnp.minimum(a0, a1)
        hi23, lo23 = jnp.maximum(a2, a3), jnp.minimum(a2, a3)
        top1 = jnp.maximum(hi01, hi23)
        top2 = jnp.maximum(jnp.minimum(hi01, hi23), jnp.maximum(lo01, lo23))
        scores.append(top1 + top2)
    best = scores[0]
    gi = jnp.zeros(best.shape, jnp.int32)
    for g in range(1, N_GROUPS):
        better = scores[g] > best
        gi = jnp.where(better, g, gi)
        best = jnp.where(better, scores[g], best)

    def in_group(a, j):
        out = row(a, j)
        for g in range(1, N_GROUPS):
            out = jnp.where(gi == g, row(a, 4 * g + j), out)
        return out

    v = [in_group(sel, j) for j in range(4)]
    a = [in_group(aff, j) for j in range(4)]
    chosen = []
    for j in range(4):
        rank = jnp.zeros(best.shape, jnp.int32)
        for k in range(4):
            if k == j:
                continue
            ahead = (v[k] >= v[j]) if k < j else (v[k] > v[j])
            rank = rank + ahead.astype(jnp.int32)
        chosen.append(rank < 2)
    total = sum(jnp.where(chosen[j], a[j], 0.0) for j in range(4))
    w = [jnp.where(chosen[j], a[j], 0.0) / total for j in range(4)]
    pair = jnp.zeros(best.shape, jnp.int32)
    for p, (ja, jb) in enumerate(zip(SLOT_A_LOCAL, SLOT_B_LOCAL)):
        pair = jnp.where(chosen[ja] & chosen[jb], p, pair)
    gate_a = jnp.where(pair == 0, w[0], jnp.where(pair <= 2, w[2], w[3]))
    gate_b = jnp.where((pair == 0) | (pair == 1) | (pair == 4), w[1], jnp.where(pair == 5, w[2], w[0]))
    return gi * PAIRS_PER_GROUP + pair, gate_a, gate_b


def _post_attn_kernel(split_x, *refs):
    t = pl.program_id(0)
    op_ref, os_ref = refs[:2]
    if split_x:
        x = _pick(t, refs[2], refs[3])
        refs = refs[4:]
    else:
        x = refs[2][...]
        refs = refs[3:]
    (w_out_ref, gate_ref, shift_ref, scale_ref, lng_ref, lnb_ref, rw_ref, rb_ref, tri_ref,
     x1_ref, row_ref, route_ref, counts_ref, carry_ref) = refs
    o = _pick(t, op_ref, os_ref)
    subs = _sub_tiles(o.shape[0])

    a = [jnp.dot(o[s], w_out_ref[...], preferred_element_type=F32) for s in subs]
    x1 = [_layer_norm(ALPHA * x[s] + gate_ref[...] * a_s, lng_ref[...], lnb_ref[...]) for s, a_s in zip(subs, a)]
    h2 = [x1_s * (1.0 + scale_ref[...]) + shift_ref[...] for x1_s in x1]
    for s, x1_s, h2_s in zip(subs, x1, h2):
        x1_ref[s, :] = x1_s
        row_ref[s, :D_MODEL] = h2_s
    rw = rw_ref[...]
    rw_hi = rw.astype(BF16)
    rw_lo = (rw - rw_hi.astype(F32)).astype(BF16)
    rw_both = jnp.concatenate([rw_hi, rw_lo], axis=0)
    logits = []
    for h2_s in h2:
        h2_hi = h2_s.astype(BF16)
        h2_lo = (h2_s - h2_hi.astype(F32)).astype(BF16)
        by_hi = _qk(rw_both, h2_hi)
        logits.append(by_hi[:N_EXPERTS] + by_hi[N_EXPERTS:] + _qk(rw_hi, h2_lo))
    aff = [1.0 / (1.0 + jnp.exp(-l)) for l in logits]
    routed = [_route(aff_s + rb_ref[...], aff_s) for aff_s in aff]

    lane = lax.broadcasted_iota(jnp.int32, (LANES, TM), 0)
    for s, (_, gate_a, gate_b) in zip(subs, routed):
        meta_t = jnp.where(lane == 0, gate_a, jnp.where(lane == 1, gate_b, 0.0))
        row_ref[s, D_MODEL:] = meta_t.T

    @pl.when(t == 0)
    def _():
        carry_ref[...] = jnp.zeros_like(carry_ref)

    onehots = [lax.broadcasted_iota(jnp.int32, (BUCKET_ROWS, TM), 0) == bucket for bucket, _, _ in routed]
    prefixes = [jnp.dot(oh.astype(BF16), tri_ref[...], preferred_element_type=F32) for oh in onehots]
    carry = carry_ref[...]
    route_ref[...] = jnp.zeros(route_ref.shape, F32)
    for s, (bucket, _, _), onehot, prefix in zip(subs, routed, onehots, prefixes):
        rank = jnp.sum(jnp.where(onehot, prefix + carry[:, 0:1], 0.0), axis=0, keepdims=True)
        carry = carry + jnp.sum(onehot.astype(F32), axis=1, keepdims=True)
        route_ref[0:1, s] = bucket.astype(F32)
        route_ref[1:2, s] = rank
    carry_ref[...] = carry
    counts_ref[...] = carry


def _post_attn(layer, o_p, o_s, x_list, mod, w_out, ln_g, ln_b, rw_t, rb):
    split_x = len(x_list) == 2
    tok = pl.BlockSpec((TP, D_MODEL), lambda t: (t, 0))
    tri = jnp.asarray(np.arange(TM)[:, None] < np.arange(TM)[None, :], BF16)
    return pl.pallas_call(
        functools.partial(_post_attn_kernel, split_x),
        grid=(N_TOK // TP,),
        in_specs=_split_specs(TP) + (_split_specs(TP) if split_x else [tok]) + [
            _const_spec((D_MODEL, D_MODEL)), _mod_spec(layer, 2, TP), _mod_spec(layer, 3, TP),
            _mod_spec(layer, 4, TP),
            _const_spec((1, D_MODEL)), _const_spec((1, D_MODEL)), _const_spec((N_EXPERTS, D_MODEL)),
            _const_spec((N_EXPERTS, 1)), _const_spec((TM, TM))],
        out_specs=[tok, pl.BlockSpec((TP, ROW_W), lambda t: (t, 0)), pl.BlockSpec((8, TP), lambda t: (0, t)),
                   _const_spec((BUCKET_ROWS, LANES))],
        out_shape=[jax.ShapeDtypeStruct((N_TOK, D_MODEL), F32), jax.ShapeDtypeStruct((N_TOK, ROW_W), F32),
                   jax.ShapeDtypeStruct((8, N_TOK), F32), jax.ShapeDtypeStruct((BUCKET_ROWS, LANES), F32)],
        scratch_shapes=[pltpu.VMEM((BUCKET_ROWS, LANES), F32)],
        compiler_params=_params(("arbitrary",)),
        name="post_attn",
    )(o_p, o_s, *x_list, w_out, mod, mod, mod, ln_g, ln_b, rw_t, rb, tri)


def _moe_kernel(ia_ref, ib_ref, itile_ref, isize_ref, ni_ref, nt_ref, x_hbm, wga_ref, wua_ref, wda_ref, wgb_ref,
                wub_ref, wdb_ref, o_hbm, xbuf, obuf, xsem, osem, wga_s, wua_s, wda_s, wgb_s, wub_s, wdb_s):
    i = pl.program_id(0)
    prev = jnp.maximum(i - 1, 0)
    n_items = ni_ref[0]
    n_tiles = nt_ref[0]
    slots = ((ia_ref, (wga_ref, wua_ref, wda_ref), (wga_s, wua_s, wda_s)),
             (ib_ref, (wgb_ref, wub_ref, wdb_ref), (wgb_s, wub_s, wdb_s)))

    def fetch(k):
        row0 = pl.multiple_of(itile_ref[k] * TME, TME)
        return pltpu.make_async_copy(x_hbm.at[pl.ds(row0, ITEM_TILES * TME), :], xbuf.at[k % 2], xsem.at[k % 2])

    def store(k, tiles):
        row0 = pl.multiple_of(itile_ref[k] * TME, TME)
        return pltpu.make_async_copy(obuf.at[k % 2, pl.ds(0, tiles * TME), :],
                                     o_hbm.at[pl.ds(row0, tiles * TME), :], osem.at[k % 2])

    def wait_store(k):
        for tiles in range(1, ITEM_TILES + 1):
            @pl.when(isize_ref[k] == tiles)
            def _(tiles=tiles):
                store(k, tiles).wait()

    @pl.when((i == 0) & (n_items > 0))
    def _():
        fetch(0).start()

    for t_ref, w_refs, w_scr in slots:
        @pl.when((i == 0) | (t_ref[i] != t_ref[prev]))
        def _(w_refs=w_refs, w_scr=w_scr):
            for w_ref, s_ref in zip(w_refs, w_scr):
                s_ref[...] = w_ref[...].astype(BF16)

    @pl.when(i < n_items)
    def _():
        fetch(i).wait()

        @pl.when(i + 1 < n_items)
        def _():
            fetch(i + 1).start()

        @pl.when(i >= 2)
        def _():
            wait_store(i - 2)

        for tiles in range(1, ITEM_TILES + 1):
            @pl.when(isize_ref[i] == tiles)
            def _(tiles=tiles):
                rows = tiles * TME
                x_ref = xbuf.at[i % 2, pl.ds(0, rows), :]
                x = x_ref[:, :D_MODEL].astype(BF16)
                acc = None
                for slot, (_, _, (wg_s, wu_s, wd_s)) in enumerate(slots):
                    gt = jnp.dot(x, wg_s[...], preferred_element_type=F32)
                    up = jnp.dot(x, wu_s[...], preferred_element_type=F32)
                    gate = x_ref[:, D_MODEL + slot:D_MODEL + slot + 1]
                    hid = (gt / (1.0 + jnp.exp(-gt))) * up * gate
                    y = jnp.dot(hid.astype(BF16), wd_s[...], preferred_element_type=F32)
                    acc = y if acc is None else acc + y
                obuf[i % 2, pl.ds(0, rows), :] = acc
                store(i, tiles).start()

    @pl.when(i == MOE_ITEMS - 1)
    def _():
        @pl.when(n_items >= 2)
        def _():
            wait_store(n_items - 2)

        @pl.when(n_items >= 1)
        def _():
            wait_store(n_items - 1)

        obuf[0] = jnp.zeros(obuf.shape[1:], F32)

        def unused(g, carry):
            cp = pltpu.make_async_copy(obuf.at[0, pl.ds(0, TME), :],
                                       o_hbm.at[pl.ds(pl.multiple_of(g * TME, TME), TME), :], osem.at[0])
            cp.start()
            cp.wait()
            return carry
        lax.fori_loop(n_tiles, MOE_TILES, unused, 0)


def _moe(layer, item_a, item_b, item_tile, item_size, n_items, n_tiles, rows, w_gate, w_up, w_down):
    def wspec(shape, which):
        if which == 0:
            return pl.BlockSpec((None, None) + shape, lambda i, ia, ib, it, sz, ni, nt: (layer, ia[i], 0, 0))
        return pl.BlockSpec((None, None) + shape, lambda i, ia, ib, it, sz, ni, nt: (layer, ib[i], 0, 0))

    up_shape, down_shape = (D_MODEL, D_EXPERT), (D_EXPERT, D_MODEL)
    grid_spec = pltpu.PrefetchScalarGridSpec(
        num_scalar_prefetch=6,
        grid=(MOE_ITEMS,),
        in_specs=[pl.BlockSpec(memory_space=pl.ANY),
                  wspec(up_shape, 0), wspec(up_shape, 0), wspec(down_shape, 0),
                  wspec(up_shape, 1), wspec(up_shape, 1), wspec(down_shape, 1)],
        out_specs=pl.BlockSpec(memory_space=pl.ANY),
        scratch_shapes=[pltpu.VMEM((2, ITEM_TILES * TME, ROW_W), F32),
                        pltpu.VMEM((2, ITEM_TILES * TME, D_MODEL), F32),
                        pltpu.SemaphoreType.DMA((2,)), pltpu.SemaphoreType.DMA((2,))]
        + [pltpu.VMEM(s, BF16) for s in (up_shape, up_shape, down_shape) * 2],
    )
    return pl.pallas_call(
        _moe_kernel,
        grid_spec=grid_spec,
        out_shape=jax.ShapeDtypeStruct((MOE_ROWS, D_MODEL), F32),
        compiler_params=_params(("arbitrary",)),
        name="moe_experts",
    )(item_a, item_b, item_tile, item_size, n_items, n_tiles, rows, w_gate, w_up, w_down, w_gate, w_up, w_down)


def _row_copy(src_ref, src_row, dst_ref, dst_row, sem):
    return pltpu.make_async_copy(src_ref.at[pl.ds(src_row, 1), :], dst_ref.at[pl.ds(dst_row, 1), :], sem)


def _dispatch_kernel(pos_ref, pend_ref, cnt_ref, nt_ref, src_ref, out_ref, buf, zero_ref, in_sem, sem, zsem):
    t = pl.program_id(0)

    def fetch(tile):
        return pltpu.make_async_copy(src_ref.at[pl.ds(pl.multiple_of(tile * TM, TM), TM), :],
                                     buf.at[tile % STAGE_SLOTS], in_sem.at[tile % STAGE_SLOTS])

    @pl.when(t == 0)
    def _():
        fetch(0).start()
        zero_ref[...] = jnp.zeros_like(zero_ref)

        def zero_tile(row0):
            return pltpu.make_async_copy(zero_ref, out_ref.at[pl.ds(pl.multiple_of(row0, TME), TME), :], zsem)

        for b in range(N_BUCKETS):
            @pl.when(cnt_ref[b] > 0)
            def _(b=b):
                zero_tile(pend_ref[b] - TME).start()

        def start_unused(i, carry):
            zero_tile(i * TME).start()
            return carry
        lax.fori_loop(nt_ref[0], MOE_TILES + ITEM_TILES - 1, start_unused, 0)

        for b in range(N_BUCKETS):
            @pl.when(cnt_ref[b] > 0)
            def _(b=b):
                zero_tile(pend_ref[b] - TME).wait()

        def wait_unused(i, carry):
            zero_tile(i * TME).wait()
            return carry
        lax.fori_loop(nt_ref[0], MOE_TILES + ITEM_TILES - 1, wait_unused, 0)

    @pl.when(t + 1 < N_TILES)
    def _():
        fetch(t + 1).start()

    fetch(t).wait()
    tile_ref = buf.at[t % STAGE_SLOTS]

    for r in range(TM):
        _row_copy(tile_ref, r, out_ref, pos_ref[t * TM + r], sem.at[t % 2]).start(priority=r % 2)

    def drain(tile):
        pltpu.make_async_copy(buf.at[0], out_ref.at[pl.ds(0, TM), :], sem.at[tile % 2]).wait()

    @pl.when(t > 0)
    def _():
        drain(t - 1)

    @pl.when(t == N_TILES - 1)
    def _():
        drain(t)


def _dispatch(pos, pend, counts, n_tiles, rows):
    return pl.pallas_call(
        _dispatch_kernel,
        grid_spec=pltpu.PrefetchScalarGridSpec(
            num_scalar_prefetch=4, grid=(N_TILES,),
            in_specs=[pl.BlockSpec(memory_space=pl.ANY)],
            out_specs=pl.BlockSpec(memory_space=pl.ANY),
            scratch_shapes=[pltpu.VMEM((STAGE_SLOTS, TM, ROW_W), F32), pltpu.VMEM((TME, ROW_W), F32),
                            pltpu.SemaphoreType.DMA((STAGE_SLOTS,)), pltpu.SemaphoreType.DMA((2,)),
                            pltpu.SemaphoreType.DMA]),
        out_shape=jax.ShapeDtypeStruct((MOE_ROWS + (ITEM_TILES - 1) * TME, ROW_W), F32),
        compiler_params=_params(("arbitrary",)),
        name="dispatch_rows",
    )(pos, pend, counts, n_tiles, rows)


def _plan(route, counts):
    bucket = route[0].astype(jnp.int32)
    rank = route[1].astype(jnp.int32)
    counts = counts[:N_BUCKETS, 0].astype(jnp.int32)
    padded = ((counts + TME - 1) // TME) * TME
    pend = jnp.cumsum(padded)
    pstart = pend - padded
    ids = jnp.arange(N_BUCKETS, dtype=jnp.int32)
    pos = rank + jnp.sum(jnp.where(bucket[None, :] == ids[:, None], pstart[:, None], 0), axis=0)
    n_tiles = pend[-1] // TME
    tiles = padded // TME
    items = (tiles + ITEM_TILES - 1) // ITEM_TILES
    item_end = jnp.cumsum(items)
    item_start = item_end - items
    n_items = item_end[-1]
    step = jnp.arange(MOE_ITEMS, dtype=jnp.int32)
    live = step < n_items
    item = jnp.minimum(step, jnp.maximum(n_items - 1, 0))
    bkt = jnp.minimum(jnp.sum((item[:, None] >= item_end[None, :]).astype(jnp.int32), axis=1), N_BUCKETS - 1)
    hot = bkt[:, None] == ids[None, :]

    def of_bucket(v):
        return jnp.sum(jnp.where(hot, v[None, :], 0), axis=1)

    k = item - of_bucket(item_start)
    item_tile = of_bucket(pstart // TME) + ITEM_TILES * k
    item_size = jnp.where(live, jnp.minimum(ITEM_TILES, of_bucket(tiles) - ITEM_TILES * k), 0)
    group, pair = bkt // PAIRS_PER_GROUP, bkt % PAIRS_PER_GROUP
    slot_a = jnp.asarray(SLOT_A_LOCAL, jnp.int32)
    slot_b = jnp.asarray(SLOT_B_LOCAL, jnp.int32)
    pair_hot = pair[:, None] == jnp.arange(PAIRS_PER_GROUP, dtype=jnp.int32)[None, :]
    item_a = group * EXPERTS_PER_GROUP + jnp.sum(jnp.where(pair_hot, slot_a[None, :], 0), axis=1)
    item_b = group * EXPERTS_PER_GROUP + jnp.sum(jnp.where(pair_hot, slot_b[None, :], 0), axis=1)
    return (pos, pend, counts, n_tiles.reshape(1), item_a.astype(jnp.int32), item_b.astype(jnp.int32),
            item_tile.astype(jnp.int32), item_size.astype(jnp.int32), n_items.astype(jnp.int32).reshape(1))


def _post_moe_kernel(split_out, pos_ref, f_ref, x1_ref, gate_ref, lng_ref, lnb_ref, *refs):
    out_refs, (fbuf, sem) = refs[:-2], refs[-2:]
    t = pl.program_id(0)
    tq = x1_ref.shape[0]
    n_steps = N_TOK // tq
    p_steps = N_PROMPT // tq

    def wait_tile(slot):
        pltpu.make_async_copy(f_ref.at[pl.ds(0, tq), :], fbuf.at[slot], sem.at[slot]).wait()

    @pl.when(t == 0)
    def _():
        def issue(r, carry):
            _row_copy(f_ref, pos_ref[r], fbuf.at[r // tq], r % tq, sem.at[r // tq]).start()
            return carry
        lax.fori_loop(0, (GATHER_SLOTS - 1) * tq, issue, 0, unroll=8)

    slot = t % GATHER_SLOTS
    wait_tile(slot)
    ahead = jnp.minimum(t + GATHER_SLOTS - 1, n_steps - 1)
    aslot = (t + GATHER_SLOTS - 1) % GATHER_SLOTS
    for r in range(tq):
        _row_copy(f_ref, pos_ref[ahead * tq + r], fbuf.at[aslot], r, sem.at[aslot]).start(priority=r % 2)
    subs = _sub_tiles(tq)
    y = [_layer_norm(ALPHA * x1_ref[s, :] + gate_ref[...] * fbuf[slot, s, :], lng_ref[...], lnb_ref[...])
         for s in subs]

    @pl.when(t == n_steps - 1)
    def _():
        for k in range(1, GATHER_SLOTS):
            wait_tile((t + k) % GATHER_SLOTS)
    if split_out:
        @pl.when(t < p_steps)
        def _():
            for s, y_s in zip(subs, y):
                out_refs[0][s, :] = y_s

        @pl.when(t >= p_steps)
        def _():
            for s, y_s in zip(subs, y):
                out_refs[1][s, :] = y_s
    else:
        for s, y_s in zip(subs, y):
            out_refs[0][s, :] = y_s


def _post_moe(layer, pos, f_sorted, x1, mod, ln_g, ln_b, split_out):
    tq = TQ
    p_steps = N_PROMPT // tq
    tok = pl.BlockSpec((tq, D_MODEL), lambda t, p: (t, 0))
    if split_out:
        out_specs = [pl.BlockSpec((tq, D_MODEL), lambda t, p: (jnp.minimum(t, p_steps - 1), 0)),
                     pl.BlockSpec((tq, D_MODEL), lambda t, p: (jnp.maximum(t - p_steps, 0), 0))]
        out_shape = [jax.ShapeDtypeStruct((N_PROMPT, D_MODEL), F32), jax.ShapeDtypeStruct((N_SAMPLE, D_MODEL), F32)]
    else:
        out_specs = [tok]
        out_shape = [jax.ShapeDtypeStruct((N_TOK, D_MODEL), F32)]
    mod_spec = pl.BlockSpec((None, None, None, 1, D_MODEL), lambda t, p: (layer, _mod_row(t, tq), 5, 0, 0))
    return pl.pallas_call(
        functools.partial(_post_moe_kernel, split_out),
        grid_spec=pltpu.PrefetchScalarGridSpec(
            num_scalar_prefetch=1, grid=(N_TOK // tq,),
            in_specs=[pl.BlockSpec(memory_space=pl.ANY), tok, mod_spec,
                      pl.BlockSpec((1, D_MODEL), lambda t, p: (0, 0)), pl.BlockSpec((1, D_MODEL), lambda t, p: (0, 0))],
            out_specs=out_specs,
            scratch_shapes=[pltpu.VMEM((GATHER_SLOTS, tq, D_MODEL), F32),
                            pltpu.SemaphoreType.DMA((GATHER_SLOTS,))]),
        out_shape=out_shape,
        compiler_params=_params(("arbitrary",)),
        name="post_moe",
    )(pos, f_sorted, x1, mod, ln_g, ln_b)


def _ffn(layer, o_p, o_s, x_list, mod, w_out, ln_g, ln_b, rw_t, rb, w_gate, w_up, w_down, split_out):
    x1, rows, route, counts = _post_attn(layer, o_p, o_s, x_list, mod, w_out, ln_g[layer, 0][None],
                                         ln_b[layer, 0][None], rw_t, rb)
    pos, pend, counts, n_tiles, item_a, item_b, item_tile, item_size, n_items = _plan(route, counts)
    rows_sorted = _dispatch(pos, pend, counts, n_tiles, rows)
    f_sorted = _moe(layer, item_a, item_b, item_tile, item_size, n_items, n_tiles, rows_sorted, w_gate, w_up, w_down)
    return _post_moe(layer, pos, f_sorted, x1, mod, ln_g[layer, 1][None], ln_b[layer, 1][None], split_out)


def kernel(x_prompt, x_sample, c, cache_mla_ckv, cache_mla_kpe, cache_swa_k, cache_swa_v, cache_gqa_k, cache_gqa_v, c_ctx, w_mod, b_mod, ln_g, ln_b, w_in_ab, mla_q_norm, mla_w_uq, mla_kv_norm, mla_w_ukv, swa_sink, w_out_ab, w_in_c, gqa_q_norm, gqa_k_norm, w_out_c, router_w, router_bias, exp_w_gate, exp_w_up, exp_w_down):
    xp = x_prompt.reshape(N_PROMPT, D_MODEL)
    xs = x_sample.reshape(N_SAMPLE, D_MODEL)
    cond = jnp.concatenate([c_ctx[None], c, jnp.zeros((MOD_ROWS - 1 - DEC_BATCH, D_MODEL), F32)], axis=0)
    mod = _modulation(cond, w_mod, b_mod).reshape(DEPTH, MOD_ROWS, 6, 1, D_MODEL)
    tabs = _rope_tables()
    rw_t = router_w.T
    rb = router_bias.reshape(N_EXPERTS, 1)

    w = w_in_ab[0]
    pad = jnp.zeros((D_MODEL, AB_COLS - w.shape[1]), F32)
    kpe0 = Q_LORA + KV_LORA
    w_in = jnp.concatenate([w[:, :kpe0], w[:, kpe0 + ROPE_A:], w[:, kpe0:kpe0 + ROPE_A], pad], axis=1).astype(BF16)
    w_uq, w_kn, w_v, place = _mla_weights(mla_w_uq[0], mla_w_ukv[0])
    qa, ka, va, qb, ckv_p, kpe_p, kbt_p, vbt_p, kb_s, vb_s = _inproj_ab(
        xp, xs, mod, w_in, mla_q_norm[0][None], mla_kv_norm[0][None], w_uq, w_kn, w_v, place, tabs)
    sink = swa_sink[0]
    o_p = _attn_ab_prompt(sink, qa, ka, va, qb, kbt_p, vbt_p)
    kpe_ctx = jnp.pad(cache_mla_kpe[:, 0].reshape(-1, ROPE_A), ((0, 0), (0, LANES - ROPE_A)))
    o_s = _attn_ab_sample(sink, qa, ka, va, qb, kb_s, vb_s,
                          cache_mla_ckv[:, 0].reshape(-1, KV_LORA), kpe_ctx,
                          cache_swa_k[:, 0].reshape(-1, KVH_B * HD), cache_swa_v[:, 0].reshape(-1, KVH_B * HD),
                          w_kn, w_v, place)
    (x2,) = _ffn(0, o_p, o_s, [xp, xs], mod, w_out_ab[0].astype(BF16), ln_g, ln_b, rw_t, rb,
                 exp_w_gate, exp_w_up, exp_w_down, split_out=False)

    g_full = jnp.concatenate([jnp.tile(gqa_q_norm[0], H_C), jnp.tile(gqa_k_norm[0], KVH_C)])[None]
    qc, kc_p, vc_p, kc_s, vc_s = _inproj_c(x2, mod, w_in_c[0].astype(BF16), g_full, tabs)
    oc_p = _attn_c_prompt(qc, kc_p, vc_p)
    oc_s = _attn_c_sample(qc, kc_s, vc_s, cache_gqa_k[:, 0].reshape(-1, KVH_C * HD),
                          cache_gqa_v[:, 0].reshape(-1, KVH_C * HD))
    y_p, y_s = _ffn(1, oc_p, oc_s, [x2], mod, w_out_c[0].astype(BF16), ln_g, ln_b, rw_t, rb,
                    exp_w_gate, exp_w_up, exp_w_down, split_out=True)

    y_prompt = y_p.reshape(BATCH, SEQ, D_MODEL)
    y_sample = y_s.reshape(DEC_BATCH, DEC_SEQ, D_MODEL)
    new_ckv = ckv_p.reshape(BATCH, 1, SEQ, KV_LORA)
    new_kpe = kpe_p.reshape(BATCH, 1, SEQ, ROPE_A)
    new_swk = kbt_p.reshape(BATCH, 1, KVH_B, HD, SEQ).transpose(0, 1, 4, 2, 3)
    new_swv = vbt_p.reshape(BATCH, 1, KVH_B, HD, SEQ).transpose(0, 1, 4, 2, 3)
    new_gk = kc_p.reshape(BATCH, 1, SEQ, KVH_C, HD)
    new_gv = vc_p.reshape(BATCH, 1, SEQ, KVH_C, HD)
    return y_prompt, y_sample, new_ckv, new_kpe, new_swk, new_swv, new_gk, new_gv
```

```python
import functools

import numpy as np
import jax
import jax.numpy as jnp
from jax import lax
from jax.experimental import pallas as pl
from jax.experimental.pallas import tpu as pltpu

D_MODEL = 1024
BATCH = 32
SEQ = 256
DEPTH = 2
DEC_BATCH = 4
DEC_SEQ = 1024
PAST_LEN = 256
GRID_W = 64
ROPE_THETA = 10000.0
WINDOW = 128
HD = 64
H_A = 8
NOPE_A = 64
ROPE_A = 32
V_A = 64
Q_LORA = 256
KV_LORA = 128
H_B = 8
KVH_B = 2
H_C = 16
KVH_C = 4
N_EXPERTS = 16
N_GROUPS = 4
EXPERTS_PER_GROUP = 4
D_EXPERT = 512
ALPHA = (2 * DEPTH) ** 0.25
NEG_INF = -1e30
EPS = 1e-6

LANES = 128
N_PROMPT = BATCH * SEQ
N_SAMPLE = DEC_BATCH * DEC_SEQ
N_TOK = N_PROMPT + N_SAMPLE
TM = 256
TP = 1024
TP_AB = 512
TQ = 512
N_TILES = N_TOK // TM
BQ_S = 256
LOG2E = 1.4426950408889634
MLA_STACK_PROMPT = 4
MLA_STACK_SAMPLE = 4
GQA_STACK_PROMPT = 4
GQA_STACK_SAMPLE = 2
MOD_ROWS = 8

PAIRS_PER_GROUP = 6
N_BUCKETS = N_GROUPS * PAIRS_PER_GROUP
SLOT_A_LOCAL = (0, 2, 2, 3, 3, 3)
SLOT_B_LOCAL = (1, 1, 0, 0, 1, 2)
TME = 256
MOE_TILES = -(-(N_TOK + N_BUCKETS * (TME - 1)) // TME)
MOE_ROWS = MOE_TILES * TME
ITEM_TILES = 3
MOE_ITEMS = (MOE_TILES + (ITEM_TILES - 1) * N_BUCKETS) // ITEM_TILES

BUCKET_ROWS = 32
ROW_W = D_MODEL + LANES
STAGE_SLOTS = 3
GATHER_SLOTS = 3

COL_QB = Q_LORA + KV_LORA
COL_KB = COL_QB + H_B * HD
COL_VB = COL_KB + KVH_B * HD
COL_KPE = COL_VB + KVH_B * HD
AB_COLS = COL_KPE + LANES
C_NORM_COLS = H_C * HD + KVH_C * HD
NORM_BLOCK = 256

BF16 = jnp.bfloat16
F32 = jnp.float32
VMEM_LIMIT = 52 * 1024 * 1024


def _mod_row(t, tile=TM):
    p_tiles = N_PROMPT // tile
    return jnp.where(t < p_tiles, 0, 1 + (t - p_tiles) // (DEC_SEQ // tile))


def _mod_spec(layer, chunk, tile=TM):
    return pl.BlockSpec((None, None, None, 1, D_MODEL), lambda t: (layer, _mod_row(t, tile), chunk, 0, 0))


def _const_spec(shape):
    nd = len(shape)
    return pl.BlockSpec(shape, lambda *_: (0,) * nd)


def _split_specs(tile=TM):
    p_tiles = N_PROMPT // tile
    return [
        pl.BlockSpec((tile, D_MODEL), lambda t: (jnp.minimum(t, p_tiles - 1), 0)),
        pl.BlockSpec((tile, D_MODEL), lambda t: (jnp.maximum(t - p_tiles, 0), 0)),
    ]


def _group_spec(tile, width, prompt):
    p_tiles = N_PROMPT // tile
    if prompt:
        return pl.BlockSpec((tile, width), lambda t: (jnp.minimum(t, p_tiles - 1), 0))
    return pl.BlockSpec((tile, width), lambda t: (jnp.maximum(t - p_tiles, 0), 0))


def _pick(t, a_ref, b_ref):
    return jnp.where(t < N_PROMPT // a_ref.shape[0], a_ref[...], b_ref[...])


def _sub_tiles(rows):
    return [slice(i * TM, (i + 1) * TM) for i in range(rows // TM)]


def _params(sem):
    return pltpu.CompilerParams(dimension_semantics=sem, vmem_limit_bytes=VMEM_LIMIT)


def _mod_kernel(cond_ref, w_ref, b_ref, o_ref):
    c = cond_ref[...]
    s = (c / (1.0 + jnp.exp(-c))).astype(BF16)
    o_ref[...] = jnp.dot(s, w_ref[...].astype(BF16), preferred_element_type=F32) + b_ref[...]


def _modulation(cond, w_mod, b_mod):
    tn = 1536
    return pl.pallas_call(
        _mod_kernel,
        grid=(DEPTH, 6 * D_MODEL // tn),
        in_specs=[
            pl.BlockSpec((MOD_ROWS, D_MODEL), lambda l, j: (0, 0)),
            pl.BlockSpec((None, D_MODEL, tn), lambda l, j: (l, 0, j)),
            pl.BlockSpec((None, 1, tn), lambda l, j: (l, 0, j)),
        ],
        out_specs=pl.BlockSpec((None, MOD_ROWS, tn), lambda l, j: (l, 0, j)),
        out_shape=jax.ShapeDtypeStruct((DEPTH, MOD_ROWS, 6 * D_MODEL), F32),
        compiler_params=_params(("arbitrary", "arbitrary")),
        name="modulation",
    )(cond, w_mod, b_mod.reshape(DEPTH, 1, 6 * D_MODEL))


def _grid_cos_sin(half):
    n_freq = half // 2
    rows = DEC_SEQ // GRID_W
    row = jnp.repeat(jnp.arange(rows, dtype=F32), GRID_W)
    col = jnp.tile(jnp.arange(GRID_W, dtype=F32), rows)
    inv = 1.0 / (ROPE_THETA ** (jnp.arange(n_freq, dtype=F32) / n_freq))
    ang = jnp.concatenate([row[:, None] * inv, col[:, None] * inv], axis=-1)
    return jnp.cos(ang), jnp.sin(ang)


def _rope_tables():
    tabs = []
    for half in (HD // 2, ROPE_A // 2):
        cos, sin = _grid_cos_sin(half)
        reps = LANES // (2 * half)
        tabs.append(jnp.tile(jnp.concatenate([cos, cos], axis=-1), (1, reps)))
        tabs.append(jnp.tile(jnp.concatenate([-sin, sin], axis=-1), (1, reps)))
    half = ROPE_A // 2
    one = jnp.ones((DEC_SEQ, NOPE_A), F32)
    zero = jnp.zeros((DEC_SEQ, NOPE_A), F32)
    pad1 = jnp.ones((DEC_SEQ, LANES - NOPE_A - ROPE_A), F32)
    pad0 = jnp.zeros((DEC_SEQ, LANES - NOPE_A - ROPE_A), F32)
    zh = jnp.zeros((DEC_SEQ, half), F32)
    tabs.append(jnp.concatenate([one, cos, cos, pad1], axis=-1))
    tabs.append(jnp.concatenate([zero, -sin, zh, pad0], axis=-1))
    tabs.append(jnp.concatenate([zero, zh, sin, pad0], axis=-1))
    return tuple(tabs)


def _rope(x, c, s, half):
    w = x.shape[1]
    reps = w // LANES
    if reps > 1:
        c = jnp.concatenate([c] * reps, axis=1)
        s = jnp.concatenate([s] * reps, axis=1)
    ahead = pltpu.roll(x, w - half, 1)
    behind = pltpu.roll(x, half, 1)
    lane = lax.broadcasted_iota(jnp.int32, x.shape, 1)
    swapped = jnp.where((lane & (2 * half - 1)) < half, ahead, behind)
    return x * c + swapped * s


def _rope_specs(n, tile):
    p_tiles = N_PROMPT // tile

    def idx(t):
        return (jnp.maximum(t - p_tiles, 0) % (DEC_SEQ // tile), 0)
    return [pl.BlockSpec((tile, LANES), idx)] * n


def _rms(x, g_ref):
    return x * lax.rsqrt(jnp.mean(x * x, axis=-1, keepdims=True) + EPS) * g_ref[...]


def _inproj_ab_kernel(xp_ref, xs_ref, shift_ref, scale_ref, w_in_ref, qn_ref, kvn_ref, w_uq_ref, w_kn_ref, w_v_ref,
                      place_ref, c64_ref, s64_ref, c32_ref, s32_ref, ca_ref, sa_ref, sb_ref,
                      qa_ref, ka_ref, va_ref, qb_ref, ckv_p_ref, kpe_p_ref, kbt_p_ref, vbt_p_ref, kb_s_ref, vb_s_ref):
    t = pl.program_id(0)
    x = _pick(t, xp_ref, xs_ref)
    subs = _sub_tiles(x.shape[0])
    h = [(x[s] * (1.0 + scale_ref[...]) + shift_ref[...]).astype(BF16) for s in subs]
    z = [jnp.dot(h_s, w_in_ref[...], preferred_element_type=F32) for h_s in h]
    ql = [_rms(z_s[:, :Q_LORA], qn_ref).astype(BF16) for z_s in z]
    qa = [jnp.dot(ql_s, w_uq_ref[...], preferred_element_type=F32) for ql_s in ql]
    ckv = [_rms(z_s[:, Q_LORA:Q_LORA + KV_LORA], kvn_ref) for z_s in z]
    k_nope = [jnp.dot(c_s.astype(BF16), w_kn_ref[...], preferred_element_type=F32) for c_s in ckv]
    va = [jnp.dot(c_s.astype(BF16), w_v_ref[...], preferred_element_type=F32).astype(BF16) for c_s in ckv]
    for s, va_s in zip(subs, va):
        va_ref[s, :] = va_s
    qb = [z_s[:, COL_QB:COL_KB] for z_s in z]
    kb = [z_s[:, COL_KB:COL_VB] for z_s in z]
    vb = [z_s[:, COL_VB:COL_KPE] for z_s in z]
    kpe = [z_s[:, COL_KPE:] for z_s in z]

    def finish(qa_out, qb_out, kpe_out):
        k_rope = [jnp.dot(kpe_s.astype(BF16), place_ref[...], preferred_element_type=F32) for kpe_s in kpe_out]
        for i, s in enumerate(subs):
            qa_ref[s, :] = qa_out[i].astype(BF16)
            qb_ref[s, :] = qb_out[i].astype(BF16)
            ka_ref[s, :] = (k_nope[i] + k_rope[i]).astype(BF16)

    p_tiles = N_PROMPT // x.shape[0]

    @pl.when(t < p_tiles)
    def _():
        finish(qa, qb, kpe)
        for i, s in enumerate(subs):
            ckv_p_ref[s, :] = ckv[i]
            kpe_p_ref[s, :] = kpe[i][:, :ROPE_A]
            kbt_p_ref[i] = kb[i].T
            vbt_p_ref[i] = vb[i].T

    @pl.when(t >= p_tiles)
    def _():
        w = qa[0].shape[1]
        reps = w // LANES
        half = ROPE_A // 2
        qa_rot, qb_rot, kb_rot, kpe_rot = [], [], [], []
        for i, s in enumerate(subs):
            ca = jnp.concatenate([ca_ref[s, :]] * reps, axis=1)
            sa = jnp.concatenate([sa_ref[s, :]] * reps, axis=1)
            sb = jnp.concatenate([sb_ref[s, :]] * reps, axis=1)
            qa_rot.append(qa[i] * ca + pltpu.roll(qa[i], w - half, 1) * sa + pltpu.roll(qa[i], half, 1) * sb)
            qb_rot.append(_rope(qb[i], c64_ref[s, :], s64_ref[s, :], HD // 2))
            kb_rot.append(_rope(kb[i], c64_ref[s, :], s64_ref[s, :], HD // 2))
            kpe_rot.append(_rope(kpe[i], c32_ref[s, :], s32_ref[s, :], ROPE_A // 2))
        finish(qa_rot, qb_rot, kpe_rot)
        for i, s in enumerate(subs):
            kb_s_ref[s, :] = kb_rot[i]
            vb_s_ref[s, :] = vb[i]


def _mla_weights(w_uq, w_ukv):
    pad = LANES - NOPE_A - ROPE_A
    uq = w_uq.reshape(Q_LORA, H_A, NOPE_A + ROPE_A)
    uq = jnp.pad(uq, ((0, 0), (0, 0), (0, pad))).reshape(Q_LORA, H_A * LANES)
    ukv = w_ukv.reshape(KV_LORA, H_A, NOPE_A + V_A)
    kn = jnp.pad(ukv[:, :, :NOPE_A], ((0, 0), (0, 0), (0, LANES - NOPE_A))).reshape(KV_LORA, H_A * LANES)
    wv = ukv[:, :, NOPE_A:].reshape(KV_LORA, H_A * V_A)
    src = np.arange(LANES)[:, None]
    dst = np.arange(H_A * LANES)[None, :] % LANES
    place = (src < ROPE_A) & (dst == src + NOPE_A)
    return uq.astype(BF16), kn.astype(BF16), wv.astype(BF16), jnp.asarray(place, BF16)


def _inproj_ab(xp, xs, mod, w_in, q_norm, kv_norm, w_uq, w_kn, w_v, place, tabs):
    widths = [(H_A * LANES, BF16), (H_A * LANES, BF16), (H_A * V_A, BF16), (H_B * HD, BF16)]
    tp = TP_AB
    out_specs = [pl.BlockSpec((tp, w), lambda t: (t, 0)) for w, _ in widths]
    out_shape = [jax.ShapeDtypeStruct((N_TOK, w), dt) for w, dt in widths]
    for w in (KV_LORA, ROPE_A):
        out_specs.append(_group_spec(tp, w, prompt=True))
        out_shape.append(jax.ShapeDtypeStruct((N_PROMPT, w), F32))
    assert SEQ == TM
    p_tiles = N_PROMPT // tp
    for _ in range(2):
        out_specs.append(pl.BlockSpec((tp // SEQ, KVH_B * HD, SEQ), lambda t: (jnp.minimum(t, p_tiles - 1), 0, 0)))
        out_shape.append(jax.ShapeDtypeStruct((BATCH, KVH_B * HD, SEQ), F32))
    for w in (KVH_B * HD, KVH_B * HD):
        out_specs.append(_group_spec(tp, w, prompt=False))
        out_shape.append(jax.ShapeDtypeStruct((N_SAMPLE, w), F32))
    return pl.pallas_call(
        _inproj_ab_kernel,
        grid=(N_TOK // tp,),
        in_specs=_split_specs(tp) + [_mod_spec(0, 0, tp), _mod_spec(0, 1, tp), _const_spec((D_MODEL, AB_COLS)),
                                     _const_spec((1, Q_LORA)), _const_spec((1, KV_LORA)),
                                     _const_spec((Q_LORA, H_A * LANES)), _const_spec((KV_LORA, H_A * LANES)),
                                     _const_spec((KV_LORA, H_A * V_A)), _const_spec((LANES, H_A * LANES))]
        + _rope_specs(7, tp),
        out_specs=out_specs,
        out_shape=out_shape,
        compiler_params=_params(("arbitrary",)),
        name="inproj_ab",
    )(xp, xs, mod, mod, w_in, q_norm, kv_norm, w_uq, w_kn, w_v, place, *tabs)


def _inproj_c_kernel(x_ref, shift_ref, scale_ref, w_in_ref, g_ref, ones_ref, c64_ref, s64_ref,
                     q_ref, k_p_ref, v_p_ref, k_s_ref, v_s_ref):
    t = pl.program_id(0)
    subs = _sub_tiles(x_ref.shape[0])
    h = [(x_ref[s, :] * (1.0 + scale_ref[...]) + shift_ref[...]).astype(BF16) for s in subs]
    z = [jnp.dot(h_s, w_in_ref[...], preferred_element_type=F32) for h_s in h]
    qk = [z_s[:, :C_NORM_COLS] for z_s in z]
    sq = [qk_s * qk_s for qk_s in qk]
    sq_hi = [sq_s.astype(BF16) for sq_s in sq]
    sq_lo = [(sq_s - hi_s.astype(F32)).astype(BF16) for sq_s, hi_s in zip(sq, sq_hi)]
    ss = []
    for hi_s, lo_s in zip(sq_hi, sq_lo):
        blocks = []
        for j in range(C_NORM_COLS // NORM_BLOCK):
            sl = slice(j * NORM_BLOCK, (j + 1) * NORM_BLOCK)
            blocks.append(jnp.dot(hi_s[:, sl], ones_ref[...], preferred_element_type=F32)
                          + jnp.dot(lo_s[:, sl], ones_ref[...], preferred_element_type=F32))
        ss.append(jnp.concatenate(blocks, axis=1))
    qk = [qk_s * lax.rsqrt(ss_s * (1.0 / HD) + EPS) * g_ref[...] for qk_s, ss_s in zip(qk, ss)]
    p_tiles = N_PROMPT // x_ref.shape[0]

    @pl.when(t < p_tiles)
    def _():
        for s, qk_s, z_s in zip(subs, qk, z):
            q_ref[s, :] = qk_s[:, :H_C * HD].astype(BF16)
            k_p_ref[s, :] = qk_s[:, H_C * HD:]
            v_p_ref[s, :] = z_s[:, C_NORM_COLS:]

    @pl.when(t >= p_tiles)
    def _():
        for s, qk_s, z_s in zip(subs, qk, z):
            q_ref[s, :] = _rope(qk_s[:, :H_C * HD], c64_ref[s, :], s64_ref[s, :], HD // 2).astype(BF16)
            k_s_ref[s, :] = _rope(qk_s[:, H_C * HD:], c64_ref[s, :], s64_ref[s, :], HD // 2)
            v_s_ref[s, :] = z_s[:, C_NORM_COLS:]


def _inproj_c(x, mod, w_in, g_full, tabs):
    def tok(w):
        return pl.BlockSpec((TP, w), lambda t: (t, 0))
    kv_w = KVH_C * HD
    head = np.arange(NORM_BLOCK) // HD
    ones = jnp.asarray(head[:, None] == head[None, :], BF16)
    return pl.pallas_call(
        _inproj_c_kernel,
        grid=(N_TOK // TP,),
        in_specs=[tok(D_MODEL), _mod_spec(1, 0, TP), _mod_spec(1, 1, TP),
                  _const_spec((D_MODEL, C_NORM_COLS + KVH_C * HD)),
                  _const_spec((1, C_NORM_COLS)), _const_spec((NORM_BLOCK, NORM_BLOCK))] + _rope_specs(2, TP),
        out_specs=[tok(H_C * HD), _group_spec(TP, kv_w, True), _group_spec(TP, kv_w, True),
                   _group_spec(TP, kv_w, False), _group_spec(TP, kv_w, False)],
        out_shape=[jax.ShapeDtypeStruct((N_TOK, H_C * HD), BF16),
                   jax.ShapeDtypeStruct((N_PROMPT, kv_w), F32), jax.ShapeDtypeStruct((N_PROMPT, kv_w), F32),
                   jax.ShapeDtypeStruct((N_SAMPLE, kv_w), F32), jax.ShapeDtypeStruct((N_SAMPLE, kv_w), F32)],
        compiler_params=_params(("arbitrary",)),
        name="inproj_c",
    )(x, mod, mod, w_in, g_full, ones, *tabs[:2])


def _qk(q, k):
    return lax.dot_general(q, k, (((1,), (1,)), ((), ())), preferred_element_type=F32)


def _softmax_pv(scores, values, scale, sink=None, kv_t=False):
    c = scale * LOG2E
    m = scores[0].max(axis=-1, keepdims=True)
    for s in scores[1:]:
        m = jnp.maximum(m, s.max(axis=-1, keepdims=True))
    m = m * c
    if sink is not None:
        sink = sink * LOG2E
        m = jnp.maximum(m, sink)
    den = None
    out = None
    for s, v in zip(scores, values):
        p = jnp.exp2(s * c - m)
        d = p.sum(axis=-1, keepdims=True)
        p = p.astype(BF16)
        if isinstance(v, (list, tuple)):
            rows = p.shape[0] // len(v)
            o = jnp.concatenate([_qk(p[b * rows:(b + 1) * rows], v_b) if kv_t
                                 else jnp.dot(p[b * rows:(b + 1) * rows], v_b, preferred_element_type=F32)
                                 for b, v_b in enumerate(v)], axis=0)
        else:
            o = _qk(p, v) if kv_t else jnp.dot(p, v, preferred_element_type=F32)
        den = d if den is None else den + d
        out = o if out is None else out + o
    if sink is not None:
        den = den + jnp.exp2(sink - m)
    return out * (1.0 / den)


def _mla_heads(qa_ref, ka_refs, va_refs, o_ref, stack):
    scale = (NOPE_A + ROPE_A) ** -0.5
    rows = qa_ref.shape[0]
    for h0 in range(0, H_A, stack):
        heads = range(h0, h0 + stack)
        scores = [jnp.concatenate([_qk(qa_ref[:, h * LANES:(h + 1) * LANES], ka_ref[:, h * LANES:(h + 1) * LANES])
                                   for h in heads], axis=0) for ka_ref in ka_refs]
        values = [[va_ref[:, h * V_A:(h + 1) * V_A] for h in heads] for va_ref in va_refs]
        o = _softmax_pv(scores, values, scale)
        o = jnp.concatenate([o[b * rows:(b + 1) * rows] for b in range(stack)], axis=1)
        o_ref[:, h0 * V_A:(h0 + stack) * V_A] = o.astype(o_ref.dtype)


def _gqa_groups(q_ref, col0, n_q, ks, vs, mask=None, sink=None, kv_t=False):
    rows = q_ref.shape[0]
    n_groups = len(ks)
    n_seg = len(ks[0])
    per_group = []
    for g in range(n_groups):
        base = col0 + g * n_q * HD
        q = jnp.concatenate([q_ref[:, base + i * HD:base + (i + 1) * HD] for i in range(n_q)], axis=0)
        if kv_t:
            per_group.append([jnp.dot(q, k, preferred_element_type=F32) for k in ks[g]])
        else:
            per_group.append([_qk(q, k) for k in ks[g]])
    scores = [jnp.concatenate([per_group[g][j] for g in range(n_groups)], axis=0) for j in range(n_seg)]
    values = [[vs[g][j] for g in range(n_groups)] for j in range(n_seg)]
    if mask is not None:
        scores[0] = jnp.where(mask, scores[0], NEG_INF)
    o = _softmax_pv(scores, values, HD ** -0.5, sink, kv_t)
    return jnp.concatenate([o[i * rows:(i + 1) * rows] for i in range(n_groups * n_q)], axis=1)


def _stacked_sink(sink_ref, h0, n_q, rows):
    head = lax.broadcasted_iota(jnp.int32, (n_q * rows, 1), 0) // rows
    col = jnp.full((n_q * rows, 1), sink_ref[h0], F32)
    for g in range(1, n_q):
        col = jnp.where(head == g, sink_ref[h0 + g], col)
    return col


def _attn_ab_prompt_kernel(sink_ref, qa_ref, ka_ref, va_ref, qb_ref, kbt_ref, vbt_ref, o_ref):
    _mla_heads(qa_ref, [ka_ref], [va_ref], o_ref, MLA_STACK_PROMPT)
    kbt = kbt_ref[...].astype(BF16)
    vbt = vbt_ref[...].astype(BF16)
    sls = [slice(kh * HD, (kh + 1) * HD) for kh in range(KVH_B)]
    o = _gqa_groups(qb_ref, 0, H_B // KVH_B, [[kbt[sl, :]] for sl in sls], [[vbt[sl, :]] for sl in sls],
                    sink=_stacked_sink(sink_ref, 0, H_B, SEQ), kv_t=True)
    o_ref[:, H_A * V_A:] = o.astype(o_ref.dtype)


def _attn_ab_prompt(sink, qa, ka, va, qb, kbt, vbt):
    def blk(w):
        return pl.BlockSpec((SEQ, w), lambda b: (b, 0))
    tblk = pl.BlockSpec((None, KVH_B * HD, SEQ), lambda b: (b, 0, 0))
    return pl.pallas_call(
        _attn_ab_prompt_kernel,
        grid=(BATCH,),
        in_specs=[pl.BlockSpec(memory_space=pltpu.SMEM), blk(H_A * LANES), blk(H_A * LANES), blk(H_A * V_A),
                  blk(H_B * HD), tblk, tblk],
        out_specs=blk(D_MODEL),
        out_shape=jax.ShapeDtypeStruct((N_PROMPT, H_A * V_A + H_B * HD), BF16),
        compiler_params=_params(("arbitrary",)),
        name="attn_ab_prompt",
    )(sink, qa, ka, va, qb, kbt, vbt)


def _attn_ab_sample_kernel(sink_ref, qa_ref, ka_ref, va_ref, qb_ref, kb_ref, vb_ref,
                           ckv_ctx_ref, kpe_ctx_ref, kb_ctx_ref, vb_ctx_ref, w_kn_ref, w_v_ref, place_ref,
                           o_ref, ka_ctx_ref, va_ctx_ref):
    i = pl.program_id(1)

    @pl.when(i == 0)
    def _():
        ckv16 = ckv_ctx_ref[...].astype(BF16)
        k_nope = jnp.dot(ckv16, w_kn_ref[...], preferred_element_type=F32)
        k_rope = jnp.dot(kpe_ctx_ref[...].astype(BF16), place_ref[...], preferred_element_type=F32)
        ka_ctx_ref[...] = (k_nope + k_rope).astype(BF16)
        va_ctx_ref[...] = jnp.dot(ckv16, w_v_ref[...], preferred_element_type=F32).astype(BF16)

    _mla_heads(qa_ref, [ka_ref, ka_ctx_ref], [va_ref, va_ctx_ref], o_ref, MLA_STACK_SAMPLE)

    n_win = BQ_S + 2 * WINDOW
    start = pl.multiple_of(jnp.clip(i * BQ_S - WINDOW, 0, DEC_SEQ - n_win), WINDOW)
    row = lax.broadcasted_iota(jnp.int32, (H_B * BQ_S, n_win), 0)
    qpos = i * BQ_S + (row & (BQ_S - 1))
    kpos = start + lax.broadcasted_iota(jnp.int32, (H_B * BQ_S, n_win), 1)
    in_band = jnp.abs(qpos - kpos) <= WINDOW
    kwin = kb_ref[pl.ds(start, n_win), :].astype(BF16)
    vwin = vb_ref[pl.ds(start, n_win), :].astype(BF16)
    kctx = kb_ctx_ref[...].astype(BF16)
    vctx = vb_ctx_ref[...].astype(BF16)
    sls = [slice(kh * HD, (kh + 1) * HD) for kh in range(KVH_B)]
    o = _gqa_groups(qb_ref, 0, H_B // KVH_B, [[kwin[:, sl], kctx[:, sl]] for sl in sls],
                    [[vwin[:, sl], vctx[:, sl]] for sl in sls],
                    mask=in_band, sink=_stacked_sink(sink_ref, 0, H_B, BQ_S))
    o_ref[:, H_A * V_A:] = o.astype(o_ref.dtype)


def _attn_ab_sample(sink, qa, ka, va, qb, kb, vb, ckv_ctx, kpe_ctx, kb_ctx, vb_ctx, w_kn, w_v, place):
    qoff = N_PROMPT // BQ_S
    nq = DEC_SEQ // BQ_S
    boff = N_PROMPT // DEC_SEQ

    def qblk(w):
        return pl.BlockSpec((BQ_S, w), lambda b, i: (qoff + b * nq + i, 0))

    def bblk(w):
        return pl.BlockSpec((DEC_SEQ, w), lambda b, i: (boff + b, 0))

    def sblk(w):
        return pl.BlockSpec((DEC_SEQ, w), lambda b, i: (b, 0))

    def cblk(w):
        return pl.BlockSpec((PAST_LEN, w), lambda b, i: (b, 0))

    return pl.pallas_call(
        _attn_ab_sample_kernel,
        grid=(DEC_BATCH, nq),
        in_specs=[pl.BlockSpec(memory_space=pltpu.SMEM), qblk(H_A * LANES), bblk(H_A * LANES), bblk(H_A * V_A),
                  qblk(H_B * HD), sblk(KVH_B * HD), sblk(KVH_B * HD),
                  cblk(KV_LORA), cblk(LANES), cblk(KVH_B * HD), cblk(KVH_B * HD),
                  _const_spec((KV_LORA, H_A * LANES)), _const_spec((KV_LORA, H_A * V_A)),
                  _const_spec((LANES, H_A * LANES))],
        out_specs=pl.BlockSpec((BQ_S, D_MODEL), lambda b, i: (b * nq + i, 0)),
        out_shape=jax.ShapeDtypeStruct((N_SAMPLE, H_A * V_A + H_B * HD), BF16),
        scratch_shapes=[pltpu.VMEM((PAST_LEN, H_A * LANES), BF16), pltpu.VMEM((PAST_LEN, H_A * V_A), BF16)],
        compiler_params=_params(("arbitrary", "arbitrary")),
        name="attn_ab_sample",
    )(sink, qa, ka, va, qb, kb, vb, ckv_ctx, kpe_ctx, kb_ctx, vb_ctx, w_kn, w_v, place)


def _gqa_heads(q_ref, k_list, v_list, o_ref, stack):
    gq = H_C // KVH_C
    for kh0 in range(0, KVH_C, stack):
        sls = [slice(kh * HD, (kh + 1) * HD) for kh in range(kh0, kh0 + stack)]
        o = _gqa_groups(q_ref, kh0 * gq * HD, gq, [[k[:, sl] for k in k_list] for sl in sls],
                        [[v[:, sl] for v in v_list] for sl in sls])
        o_ref[:, kh0 * gq * HD:(kh0 + stack) * gq * HD] = o.astype(o_ref.dtype)


def _attn_c_prompt_kernel(q_ref, k_ref, v_ref, o_ref):
    _gqa_heads(q_ref, [k_ref[...].astype(BF16)], [v_ref[...].astype(BF16)], o_ref, GQA_STACK_PROMPT)


def _attn_c_prompt(q, k, v):
    def blk(w):
        return pl.BlockSpec((SEQ, w), lambda b: (b, 0))
    return pl.pallas_call(
        _attn_c_prompt_kernel,
        grid=(BATCH,),
        in_specs=[blk(H_C * HD), blk(KVH_C * HD), blk(KVH_C * HD)],
        out_specs=blk(H_C * HD),
        out_shape=jax.ShapeDtypeStruct((N_PROMPT, H_C * HD), BF16),
        compiler_params=_params(("arbitrary",)),
        name="attn_c_prompt",
    )(q, k, v)


def _attn_c_sample_kernel(q_ref, k_ref, v_ref, kc_ref, vc_ref, o_ref):
    _gqa_heads(q_ref, [k_ref[...].astype(BF16), kc_ref[...].astype(BF16)],
               [v_ref[...].astype(BF16), vc_ref[...].astype(BF16)], o_ref, GQA_STACK_SAMPLE)


def _attn_c_sample(q, k, v, k_ctx, v_ctx):
    qoff = N_PROMPT // BQ_S
    nq = DEC_SEQ // BQ_S
    w = KVH_C * HD
    return pl.pallas_call(
        _attn_c_sample_kernel,
        grid=(DEC_BATCH, nq),
        in_specs=[pl.BlockSpec((BQ_S, H_C * HD), lambda b, i: (qoff + b * nq + i, 0)),
                  pl.BlockSpec((DEC_SEQ, w), lambda b, i: (b, 0)),
                  pl.BlockSpec((DEC_SEQ, w), lambda b, i: (b, 0)),
                  pl.BlockSpec((PAST_LEN, w), lambda b, i: (b, 0)),
                  pl.BlockSpec((PAST_LEN, w), lambda b, i: (b, 0))],
        out_specs=pl.BlockSpec((BQ_S, H_C * HD), lambda b, i: (b * nq + i, 0)),
        out_shape=jax.ShapeDtypeStruct((N_SAMPLE, H_C * HD), BF16),
        compiler_params=_params(("arbitrary", "arbitrary")),
        name="attn_c_sample",
    )(q, k, v, k_ctx, v_ctx)


def _layer_norm(y, g, b):
    mu = jnp.mean(y, axis=-1, keepdims=True)
    yc = y - mu
    var = jnp.mean(yc * yc, axis=-1, keepdims=True)
    return yc * lax.rsqrt(var + EPS) * g + b


def _route(sel, aff):
    def row(a, j):
        return a[j:j + 1, :]

    scores = []
    for g in range(N_GROUPS):
        a0, a1, a2, a3 = (row(sel, 4 * g + j) for j in range(4))
        hi01, lo01 = jnp.maximum(a0, a1), jnp.minimum(a0, a1)
        hi23, lo23 = jnp.maximum(a2, a3), jnp.minimum(a2, a3)
        top1 = jnp.maximum(hi01, hi23)
        top2 = jnp.maximum(jnp.minimum(hi01, hi23), jnp.maximum(lo01, lo23))
        scores.append(top1 + top2)
    best = scores[0]
    gi = jnp.zeros(best.shape, jnp.int32)
    for g in range(1, N_GROUPS):
        better = scores[g] > best
        gi = jnp.where(better, g, gi)
        best = jnp.where(better, scores[g], best)

    def in_group(a, j):
        out = row(a, j)
        for g in range(1, N_GROUPS):
            out = jnp.where(gi == g, row(a, 4 * g + j), out)
        return out

    v = [in_group(sel, j) for j in range(4)]
    a = [in_group(aff, j) for j in range(4)]
    chosen = []
    for j in range(4):
        rank = jnp.zeros(best.shape, jnp.int32)
        for k in range(4):
            if k == j:
                continue
            ahead = (v[k] >= v[j]) if k < j else (v[k] > v[j])
            rank = rank + ahead.astype(jnp.int32)
        chosen.append(rank < 2)
    total = sum(jnp.where(chosen[j], a[j], 0.0) for j in range(4))
    w = [jnp.where(chosen[j], a[j], 0.0) / total for j in range(4)]
    pair = jnp.zeros(best.shape, jnp.int32)
    for p, (ja, jb) in enumerate(zip(SLOT_A_LOCAL, SLOT_B_LOCAL)):
        pair = jnp.where(chosen[ja] & chosen[jb], p, pair)
    gate_a = jnp.where(pair == 0, w[0], jnp.where(pair <= 2, w[2], w[3]))
    gate_b = jnp.where((pair == 0) | (pair == 1) | (pair == 4), w[1], jnp.where(pair == 5, w[2], w[0]))
    return gi * PAIRS_PER_GROUP + pair, gate_a, gate_b


def _post_attn_kernel(split_x, *refs):
    t = pl.program_id(0)
    op_ref, os_ref = refs[:2]
    if split_x:
        x = _pick(t, refs[2], refs[3])
        refs = refs[4:]
    else:
        x = refs[2][...]
        refs = refs[3:]
    (w_out_ref, gate_ref, shift_ref, scale_ref, lng_ref, lnb_ref, rw_ref, rb_ref, tri_ref,
     x1_ref, row_ref, route_ref, counts_ref, carry_ref) = refs
    o = _pick(t, op_ref, os_ref)
    subs = _sub_tiles(o.shape[0])

    a = [jnp.dot(o[s], w_out_ref[...], preferred_element_type=F32) for s in subs]
    x1 = [_layer_norm(ALPHA * x[s] + gate_ref[...] * a_s, lng_ref[...], lnb_ref[...]) for s, a_s in zip(subs, a)]
    h2 = [x1_s * (1.0 + scale_ref[...]) + shift_ref[...] for x1_s in x1]
    for s, x1_s, h2_s in zip(subs, x1, h2):
        x1_ref[s, :] = x1_s
        row_ref[s, :D_MODEL] = h2_s
    rw = rw_ref[...]
    rw_hi = rw.astype(BF16)
    rw_lo = (rw - rw_hi.astype(F32)).astype(BF16)
    rw_both = jnp.concatenate([rw_hi, rw_lo], axis=0)
    logits = []
    for h2_s in h2:
        h2_hi = h2_s.astype(BF16)
        h2_lo = (h2_s - h2_hi.astype(F32)).astype(BF16)
        by_hi = _qk(rw_both, h2_hi)
        logits.append(by_hi[:N_EXPERTS] + by_hi[N_EXPERTS:] + _qk(rw_hi, h2_lo))
    aff = [1.0 / (1.0 + jnp.exp(-l)) for l in logits]
    routed = [_route(aff_s + rb_ref[...], aff_s) for aff_s in aff]

    lane = lax.broadcasted_iota(jnp.int32, (LANES, TM), 0)
    for s, (_, gate_a, gate_b) in zip(subs, routed):
        meta_t = jnp.where(lane == 0, gate_a, jnp.where(lane == 1, gate_b, 0.0))
        row_ref[s, D_MODEL:] = meta_t.T

    @pl.when(t == 0)
    def _():
        carry_ref[...] = jnp.zeros_like(carry_ref)

    onehots = [lax.broadcasted_iota(jnp.int32, (BUCKET_ROWS, TM), 0) == bucket for bucket, _, _ in routed]
    prefixes = [jnp.dot(oh.astype(BF16), tri_ref[...], preferred_element_type=F32) for oh in onehots]
    carry = carry_ref[...]
    route_ref[...] = jnp.zeros(route_ref.shape, F32)
    for s, (bucket, _, _), onehot, prefix in zip(subs, routed, onehots, prefixes):
        rank = jnp.sum(jnp.where(onehot, prefix + carry[:, 0:1], 0.0), axis=0, keepdims=True)
        carry = carry + jnp.sum(onehot.astype(F32), axis=1, keepdims=True)
        route_ref[0:1, s] = bucket.astype(F32)
        route_ref[1:2, s] = rank
    carry_ref[...] = carry
    counts_ref[...] = carry


def _post_attn(layer, o_p, o_s, x_list, mod, w_out, ln_g, ln_b, rw_t, rb):
    split_x = len(x_list) == 2
    tok = pl.BlockSpec((TP, D_MODEL), lambda t: (t, 0))
    tri = jnp.asarray(np.arange(TM)[:, None] < np.arange(TM)[None, :], BF16)
    return pl.pallas_call(
        functools.partial(_post_attn_kernel, split_x),
        grid=(N_TOK // TP,),
        in_specs=_split_specs(TP) + (_split_specs(TP) if split_x else [tok]) + [
            _const_spec((D_MODEL, D_MODEL)), _mod_spec(layer, 2, TP), _mod_spec(layer, 3, TP),
            _mod_spec(layer, 4, TP),
            _const_spec((1, D_MODEL)), _const_spec((1, D_MODEL)), _const_spec((N_EXPERTS, D_MODEL)),
            _const_spec((N_EXPERTS, 1)), _const_spec((TM, TM))],
        out_specs=[tok, pl.BlockSpec((TP, ROW_W), lambda t: (t, 0)), pl.BlockSpec((8, TP), lambda t: (0, t)),
                   _const_spec((BUCKET_ROWS, LANES))],
        out_shape=[jax.ShapeDtypeStruct((N_TOK, D_MODEL), F32), jax.ShapeDtypeStruct((N_TOK, ROW_W), F32),
                   jax.ShapeDtypeStruct((8, N_TOK), F32), jax.ShapeDtypeStruct((BUCKET_ROWS, LANES), F32)],
        scratch_shapes=[pltpu.VMEM((BUCKET_ROWS, LANES), F32)],
        compiler_params=_params(("arbitrary",)),
        name="post_attn",
    )(o_p, o_s, *x_list, w_out, mod, mod, mod, ln_g, ln_b, rw_t, rb, tri)


def _moe_kernel(ia_ref, ib_ref, itile_ref, isize_ref, ni_ref, nt_ref, x_hbm, wga_ref, wua_ref, wda_ref, wgb_ref,
                wub_ref, wdb_ref, o_hbm, xbuf, obuf, xsem, osem):
    del ia_ref, ib_ref
    i = pl.program_id(0)
    n_items = ni_ref[0]
    n_tiles = nt_ref[0]
    slots = ((wga_ref, wua_ref, wda_ref), (wgb_ref, wub_ref, wdb_ref))

    def fetch(k):
        row0 = pl.multiple_of(itile_ref[k] * TME, TME)
        return pltpu.make_async_copy(x_hbm.at[pl.ds(row0, ITEM_TILES * TME), :], xbuf.at[k % 2], xsem.at[k % 2])

    def store(k, tiles):
        row0 = pl.multiple_of(itile_ref[k] * TME, TME)
        return pltpu.make_async_copy(obuf.at[k % 2, pl.ds(0, tiles * TME), :],
                                     o_hbm.at[pl.ds(row0, tiles * TME), :], osem.at[k % 2])

    def wait_store(k):
        for tiles in range(1, ITEM_TILES + 1):
            @pl.when(isize_ref[k] == tiles)
            def _(tiles=tiles):
                store(k, tiles).wait()

    @pl.when((i == 0) & (n_items > 0))
    def _():
        fetch(0).start()

    @pl.when(i < n_items)
    def _():
        fetch(i).wait()

        @pl.when(i + 1 < n_items)
        def _():
            fetch(i + 1).start()

        @pl.when(i >= 2)
        def _():
            wait_store(i - 2)

        for tiles in range(1, ITEM_TILES + 1):
            @pl.when(isize_ref[i] == tiles)
            def _(tiles=tiles):
                rows = tiles * TME
                x_ref = xbuf.at[i % 2, pl.ds(0, rows), :]
                x = x_ref[:, :D_MODEL].astype(BF16)
                acc = None
                for slot, (wg_ref, wu_ref, wd_ref) in enumerate(slots):
                    gt = jnp.dot(x, wg_ref[...].astype(BF16), preferred_element_type=F32)
                    up = jnp.dot(x, wu_ref[...].astype(BF16), preferred_element_type=F32)
                    gate = x_ref[:, D_MODEL + slot:D_MODEL + slot + 1]
                    hid = (gt / (1.0 + jnp.exp(-gt))) * up * gate
                    y = jnp.dot(hid.astype(BF16), wd_ref[...].astype(BF16), preferred_element_type=F32)
                    acc = y if acc is None else acc + y
                obuf[i % 2, pl.ds(0, rows), :] = acc
                store(i, tiles).start()

    @pl.when(i == MOE_ITEMS - 1)
    def _():
        @pl.when(n_items >= 2)
        def _():
            wait_store(n_items - 2)

        @pl.when(n_items >= 1)
        def _():
            wait_store(n_items - 1)

        obuf[0] = jnp.zeros(obuf.shape[1:], F32)

        def unused(g, carry):
            cp = pltpu.make_async_copy(obuf.at[0, pl.ds(0, TME), :],
                                       o_hbm.at[pl.ds(pl.multiple_of(g * TME, TME), TME), :], osem.at[0])
            cp.start()
            cp.wait()
            return carry
        lax.fori_loop(n_tiles, MOE_TILES, unused, 0)


def _moe(layer, item_a, item_b, item_tile, item_size, n_items, n_tiles, rows, w_gate, w_up, w_down):
    def wspec(shape, which):
        if which == 0:
            return pl.BlockSpec((None, None) + shape, lambda i, ia, ib, it, sz, ni, nt: (layer, ia[i], 0, 0))
        return pl.BlockSpec((None, None) + shape, lambda i, ia, ib, it, sz, ni, nt: (layer, ib[i], 0, 0))

    up_shape, down_shape = (D_MODEL, D_EXPERT), (D_EXPERT, D_MODEL)
    grid_spec = pltpu.PrefetchScalarGridSpec(
        num_scalar_prefetch=6,
        grid=(MOE_ITEMS,),
        in_specs=[pl.BlockSpec(memory_space=pl.ANY),
                  wspec(up_shape, 0), wspec(up_shape, 0), wspec(down_shape, 0),
                  wspec(up_shape, 1), wspec(up_shape, 1), wspec(down_shape, 1)],
        out_specs=pl.BlockSpec(memory_space=pl.ANY),
        scratch_shapes=[pltpu.VMEM((2, ITEM_TILES * TME, ROW_W), F32),
                        pltpu.VMEM((2, ITEM_TILES * TME, D_MODEL), F32),
                        pltpu.SemaphoreType.DMA((2,)), pltpu.SemaphoreType.DMA((2,))],
    )
    return pl.pallas_call(
        _moe_kernel,
        grid_spec=grid_spec,
        out_shape=jax.ShapeDtypeStruct((MOE_ROWS, D_MODEL), F32),
        compiler_params=_params(("arbitrary",)),
        name="moe_experts",
    )(item_a, item_b, item_tile, item_size, n_items, n_tiles, rows, w_gate, w_up, w_down, w_gate, w_up, w_down)


def _row_copy(src_ref, src_row, dst_ref, dst_row, sem):
    return pltpu.make_async_copy(src_ref.at[pl.ds(src_row, 1), :], dst_ref.at[pl.ds(dst_row, 1), :], sem)


def _dispatch_kernel(pos_ref, pend_ref, cnt_ref, nt_ref, src_ref, out_ref, buf, zero_ref, in_sem, sem, zsem):
    t = pl.program_id(0)

    def fetch(tile):
        return pltpu.make_async_copy(src_ref.at[pl.ds(pl.multiple_of(tile * TM, TM), TM), :],
                                     buf.at[tile % STAGE_SLOTS], in_sem.at[tile % STAGE_SLOTS])

    @pl.when(t == 0)
    def _():
        fetch(0).start()
        zero_ref[...] = jnp.zeros_like(zero_ref)

        def zero_tile(row0):
            return pltpu.make_async_copy(zero_ref, out_ref.at[pl.ds(pl.multiple_of(row0, TME), TME), :], zsem)

        for b in range(N_BUCKETS):
            @pl.when(cnt_ref[b] > 0)
            def _(b=b):
                zero_tile(pend_ref[b] - TME).start()

        def start_unused(i, carry):
            zero_tile(i * TME).start()
            return carry
        lax.fori_loop(nt_ref[0], MOE_TILES + ITEM_TILES - 1, start_unused, 0)

        for b in range(N_BUCKETS):
            @pl.when(cnt_ref[b] > 0)
            def _(b=b):
                zero_tile(pend_ref[b] - TME).wait()

        def wait_unused(i, carry):
            zero_tile(i * TME).wait()
            return carry
        lax.fori_loop(nt_ref[0], MOE_TILES + ITEM_TILES - 1, wait_unused, 0)

    @pl.when(t + 1 < N_TILES)
    def _():
        fetch(t + 1).start()

    fetch(t).wait()
    tile_ref = buf.at[t % STAGE_SLOTS]

    for r in range(TM):
        _row_copy(tile_ref, r, out_ref, pos_ref[t * TM + r], sem.at[t % 2]).start(priority=r % 2)

    def drain(tile):
        pltpu.make_async_copy(buf.at[0], out_ref.at[pl.ds(0, TM), :], sem.at[tile % 2]).wait()

    @pl.when(t > 0)
    def _():
        drain(t - 1)

    @pl.when(t == N_TILES - 1)
    def _():
        drain(t)


def _dispatch(pos, pend, counts, n_tiles, rows):
    return pl.pallas_call(
        _dispatch_kernel,
        grid_spec=pltpu.PrefetchScalarGridSpec(
            num_scalar_prefetch=4, grid=(N_TILES,),
            in_specs=[pl.BlockSpec(memory_space=pl.ANY)],
            out_specs=pl.BlockSpec(memory_space=pl.ANY),
            scratch_shapes=[pltpu.VMEM((STAGE_SLOTS, TM, ROW_W), F32), pltpu.VMEM((TME, ROW_W), F32),
                            pltpu.SemaphoreType.DMA((STAGE_SLOTS,)), pltpu.SemaphoreType.DMA((2,)),
                            pltpu.SemaphoreType.DMA]),
        out_shape=jax.ShapeDtypeStruct((MOE_ROWS + (ITEM_TILES - 1) * TME, ROW_W), F32),
        compiler_params=_params(("arbitrary",)),
        name="dispatch_rows",
    )(pos, pend, counts, n_tiles, rows)


def _plan(route, counts):
    bucket = route[0].astype(jnp.int32)
    rank = route[1].astype(jnp.int32)
    counts = counts[:N_BUCKETS, 0].astype(jnp.int32)
    padded = ((counts + TME - 1) // TME) * TME
    pend = jnp.cumsum(padded)
    pstart = pend - padded
    ids = jnp.arange(N_BUCKETS, dtype=jnp.int32)
    pos = rank + jnp.sum(jnp.where(bucket[None, :] == ids[:, None], pstart[:, None], 0), axis=0)
    n_tiles = pend[-1] // TME
    tiles = padded // TME
    items = (tiles + ITEM_TILES - 1) // ITEM_TILES
    item_end = jnp.cumsum(items)
    item_start = item_end - items
    n_items = item_end[-1]
    step = jnp.arange(MOE_ITEMS, dtype=jnp.int32)
    live = step < n_items
    item = jnp.minimum(step, jnp.maximum(n_items - 1, 0))
    bkt = jnp.minimum(jnp.sum((item[:, None] >= item_end[None, :]).astype(jnp.int32), axis=1), N_BUCKETS - 1)
    hot = bkt[:, None] == ids[None, :]

    def of_bucket(v):
        return jnp.sum(jnp.where(hot, v[None, :], 0), axis=1)

    k = item - of_bucket(item_start)
    item_tile = of_bucket(pstart // TME) + ITEM_TILES * k
    item_size = jnp.where(live, jnp.minimum(ITEM_TILES, of_bucket(tiles) - ITEM_TILES * k), 0)
    group, pair = bkt // PAIRS_PER_GROUP, bkt % PAIRS_PER_GROUP
    slot_a = jnp.asarray(SLOT_A_LOCAL, jnp.int32)
    slot_b = jnp.asarray(SLOT_B_LOCAL, jnp.int32)
    pair_hot = pair[:, None] == jnp.arange(PAIRS_PER_GROUP, dtype=jnp.int32)[None, :]
    item_a = group * EXPERTS_PER_GROUP + jnp.sum(jnp.where(pair_hot, slot_a[None, :], 0), axis=1)
    item_b = group * EXPERTS_PER_GROUP + jnp.sum(jnp.where(pair_hot, slot_b[None, :], 0), axis=1)
    return (pos, pend, counts, n_tiles.reshape(1), item_a.astype(jnp.int32), item_b.astype(jnp.int32),
            item_tile.astype(jnp.int32), item_size.astype(jnp.int32), n_items.astype(jnp.int32).reshape(1))


def _post_moe_kernel(split_out, pos_ref, f_ref, x1_ref, gate_ref, lng_ref, lnb_ref, *refs):
    out_refs, (fbuf, sem) = refs[:-2], refs[-2:]
    t = pl.program_id(0)
    tq = x1_ref.shape[0]
    n_steps = N_TOK // tq
    p_steps = N_PROMPT // tq

    def wait_tile(slot):
        pltpu.make_async_copy(f_ref.at[pl.ds(0, tq), :], fbuf.at[slot], sem.at[slot]).wait()

    @pl.when(t == 0)
    def _():
        def issue(r, carry):
            _row_copy(f_ref, pos_ref[r], fbuf.at[r // tq], r % tq, sem.at[r // tq]).start()
            return carry
        lax.fori_loop(0, (GATHER_SLOTS - 1) * tq, issue, 0, unroll=8)

    slot = t % GATHER_SLOTS
    wait_tile(slot)
    ahead = jnp.minimum(t + GATHER_SLOTS - 1, n_steps - 1)
    aslot = (t + GATHER_SLOTS - 1) % GATHER_SLOTS
    for r in range(tq):
        _row_copy(f_ref, pos_ref[ahead * tq + r], fbuf.at[aslot], r, sem.at[aslot]).start(priority=r % 2)
    subs = _sub_tiles(tq)
    y = [_layer_norm(ALPHA * x1_ref[s, :] + gate_ref[...] * fbuf[slot, s, :], lng_ref[...], lnb_ref[...])
         for s in subs]

    @pl.when(t == n_steps - 1)
    def _():
        for k in range(1, GATHER_SLOTS):
            wait_tile((t + k) % GATHER_SLOTS)
    if split_out:
        @pl.when(t < p_steps)
        def _():
            for s, y_s in zip(subs, y):
                out_refs[0][s, :] = y_s

        @pl.when(t >= p_steps)
        def _():
            for s, y_s in zip(subs, y):
                out_refs[1][s, :] = y_s
    else:
        for s, y_s in zip(subs, y):
            out_refs[0][s, :] = y_s


def _post_moe(layer, pos, f_sorted, x1, mod, ln_g, ln_b, split_out):
    tq = TQ
    p_steps = N_PROMPT // tq
    tok = pl.BlockSpec((tq, D_MODEL), lambda t, p: (t, 0))
    if split_out:
        out_specs = [pl.BlockSpec((tq, D_MODEL), lambda t, p: (jnp.minimum(t, p_steps - 1), 0)),
                     pl.BlockSpec((tq, D_MODEL), lambda t, p: (jnp.maximum(t - p_steps, 0), 0))]
        out_shape = [jax.ShapeDtypeStruct((N_PROMPT, D_MODEL), F32), jax.ShapeDtypeStruct((N_SAMPLE, D_MODEL), F32)]
    else:
        out_specs = [tok]
        out_shape = [jax.ShapeDtypeStruct((N_TOK, D_MODEL), F32)]
    mod_spec = pl.BlockSpec((None, None, None, 1, D_MODEL), lambda t, p: (layer, _mod_row(t, tq), 5, 0, 0))
    return pl.pallas_call(
        functools.partial(_post_moe_kernel, split_out),
        grid_spec=pltpu.PrefetchScalarGridSpec(
            num_scalar_prefetch=1, grid=(N_TOK // tq,),
            in_specs=[pl.BlockSpec(memory_space=pl.ANY), tok, mod_spec,
                      pl.BlockSpec((1, D_MODEL), lambda t, p: (0, 0)), pl.BlockSpec((1, D_MODEL), lambda t, p: (0, 0))],
            out_specs=out_specs,
            scratch_shapes=[pltpu.VMEM((GATHER_SLOTS, tq, D_MODEL), F32),
                            pltpu.SemaphoreType.DMA((GATHER_SLOTS,))]),
        out_shape=out_shape,
        compiler_params=_params(("arbitrary",)),
        name="post_moe",
    )(pos, f_sorted, x1, mod, ln_g, ln_b)


def _ffn(layer, o_p, o_s, x_list, mod, w_out, ln_g, ln_b, rw_t, rb, w_gate, w_up, w_down, split_out):
    x1, rows, route, counts = _post_attn(layer, o_p, o_s, x_list, mod, w_out, ln_g[layer, 0][None],
                                         ln_b[layer, 0][None], rw_t, rb)
    pos, pend, counts, n_tiles, item_a, item_b, item_tile, item_size, n_items = _plan(route, counts)
    rows_sorted = _dispatch(pos, pend, counts, n_tiles, rows)
    f_sorted = _moe(layer, item_a, item_b, item_tile, item_size, n_items, n_tiles, rows_sorted, w_gate, w_up, w_down)
    return _post_moe(layer, pos, f_sorted, x1, mod, ln_g[layer, 1][None], ln_b[layer, 1][None], split_out)


def kernel(x_prompt, x_sample, c, cache_mla_ckv, cache_mla_kpe, cache_swa_k, cache_swa_v, cache_gqa_k, cache_gqa_v, c_ctx, w_mod, b_mod, ln_g, ln_b, w_in_ab, mla_q_norm, mla_w_uq, mla_kv_norm, mla_w_ukv, swa_sink, w_out_ab, w_in_c, gqa_q_norm, gqa_k_norm, w_out_c, router_w, router_bias, exp_w_gate, exp_w_up, exp_w_down):
    xp = x_prompt.reshape(N_PROMPT, D_MODEL)
    xs = x_sample.reshape(N_SAMPLE, D_MODEL)
    cond = jnp.concatenate([c_ctx[None], c, jnp.zeros((MOD_ROWS - 1 - DEC_BATCH, D_MODEL), F32)], axis=0)
    mod = _modulation(cond, w_mod, b_mod).reshape(DEPTH, MOD_ROWS, 6, 1, D_MODEL)
    tabs = _rope_tables()
    rw_t = router_w.T
    rb = router_bias.reshape(N_EXPERTS, 1)

    w = w_in_ab[0]
    pad = jnp.zeros((D_MODEL, AB_COLS - w.shape[1]), F32)
    kpe0 = Q_LORA + KV_LORA
    w_in = jnp.concatenate([w[:, :kpe0], w[:, kpe0 + ROPE_A:], w[:, kpe0:kpe0 + ROPE_A], pad], axis=1).astype(BF16)
    w_uq, w_kn, w_v, place = _mla_weights(mla_w_uq[0], mla_w_ukv[0])
    qa, ka, va, qb, ckv_p, kpe_p, kbt_p, vbt_p, kb_s, vb_s = _inproj_ab(
        xp, xs, mod, w_in, mla_q_norm[0][None], mla_kv_norm[0][None], w_uq, w_kn, w_v, place, tabs)
    sink = swa_sink[0]
    o_p = _attn_ab_prompt(sink, qa, ka, va, qb, kbt_p, vbt_p)
    kpe_ctx = jnp.pad(cache_mla_kpe[:, 0].reshape(-1, ROPE_A), ((0, 0), (0, LANES - ROPE_A)))
    o_s = _attn_ab_sample(sink, qa, ka, va, qb, kb_s, vb_s,
                          cache_mla_ckv[:, 0].reshape(-1, KV_LORA), kpe_ctx,
                          cache_swa_k[:, 0].reshape(-1, KVH_B * HD), cache_swa_v[:, 0].reshape(-1, KVH_B * HD),
                          w_kn, w_v, place)
    (x2,) = _ffn(0, o_p, o_s, [xp, xs], mod, w_out_ab[0].astype(BF16), ln_g, ln_b, rw_t, rb,
                 exp_w_gate, exp_w_up, exp_w_down, split_out=False)

    g_full = jnp.concatenate([jnp.tile(gqa_q_norm[0], H_C), jnp.tile(gqa_k_norm[0], KVH_C)])[None]
    qc, kc_p, vc_p, kc_s, vc_s = _inproj_c(x2, mod, w_in_c[0].astype(BF16), g_full, tabs)
    oc_p = _attn_c_prompt(qc, kc_p, vc_p)
    oc_s = _attn_c_sample(qc, kc_s, vc_s, cache_gqa_k[:, 0].reshape(-1, KVH_C * HD),
                          cache_gqa_v[:, 0].reshape(-1, KVH_C * HD))
    y_p, y_s = _ffn(1, oc_p, oc_s, [x2], mod, w_out_c[0].astype(BF16), ln_g, ln_b, rw_t, rb,
                    exp_w_gate, exp_w_up, exp_w_down, split_out=True)

    y_prompt = y_p.reshape(BATCH, SEQ, D_MODEL)
    y_sample = y_s.reshape(DEC_BATCH, DEC_SEQ, D_MODEL)
    new_ckv = ckv_p.reshape(BATCH, 1, SEQ, KV_LORA)
    new_kpe = kpe_p.reshape(BATCH, 1, SEQ, ROPE_A)
    new_swk = kbt_p.reshape(BATCH, 1, KVH_B, HD, SEQ).transpose(0, 1, 4, 2, 3)
    new_swv = vbt_p.reshape(BATCH, 1, KVH_B, HD, SEQ).transpose(0, 1, 4, 2, 3)
    new_gk = kc_p.reshape(BATCH, 1, SEQ, KVH_C, HD)
    new_gv = vc_p.reshape(BATCH, 1, SEQ, KVH_C, HD)
    return y_prompt, y_sample, new_ckv, new_kpe, new_swk, new_swv, new_gk, new_gv
```

```python
import functools

import numpy as np
import jax
import jax.numpy as jnp
from jax import lax
from jax.experimental import pallas as pl
from jax.experimental.pallas import tpu as pltpu

D_MODEL = 1024
BATCH = 32
SEQ = 256
DEPTH = 2
DEC_BATCH = 4
DEC_SEQ = 1024
PAST_LEN = 256
GRID_W = 64
ROPE_THETA = 10000.0
WINDOW = 128
HD = 64
H_A = 8
NOPE_A = 64
ROPE_A = 32
V_A = 64
Q_LORA = 256
KV_LORA = 128
H_B = 8
KVH_B = 2
H_C = 16
KVH_C = 4
N_EXPERTS = 16
N_GROUPS = 4
EXPERTS_PER_GROUP = 4
D_EXPERT = 512
ALPHA = (2 * DEPTH) ** 0.25
NEG_INF = -1e30
EPS = 1e-6

LANES = 128
N_PROMPT = BATCH * SEQ
N_SAMPLE = DEC_BATCH * DEC_SEQ
N_TOK = N_PROMPT + N_SAMPLE
TM = 256
TP = 1024
TP_AB = 512
TQ = 1024
N_TILES = N_TOK // TM
BQ_S = 256
LOG2E = 1.4426950408889634
MLA_STACK_PROMPT = 4
MLA_STACK_SAMPLE = 4
GQA_STACK_PROMPT = 4
GQA_STACK_SAMPLE = 2
MOD_ROWS = 8

PAIRS_PER_GROUP = 6
N_BUCKETS = N_GROUPS * PAIRS_PER_GROUP
SLOT_A_LOCAL = (0, 2, 2, 3, 3, 3)
SLOT_B_LOCAL = (1, 1, 0, 0, 1, 2)
TME = 256
MOE_TILES = -(-(N_TOK + N_BUCKETS * (TME - 1)) // TME)
MOE_ROWS = MOE_TILES * TME
ITEM_TILES = 3
MOE_ITEMS = (MOE_TILES + (ITEM_TILES - 1) * N_BUCKETS) // ITEM_TILES

BUCKET_ROWS = 32
ROW_W = D_MODEL + LANES
STAGE_SLOTS = 3
GATHER_SLOTS = 3

COL_QB = Q_LORA + KV_LORA
COL_KB = COL_QB + H_B * HD
COL_VB = COL_KB + KVH_B * HD
COL_KPE = COL_VB + KVH_B * HD
AB_COLS = COL_KPE + LANES
C_NORM_COLS = H_C * HD + KVH_C * HD
NORM_BLOCK = 256

BF16 = jnp.bfloat16
F32 = jnp.float32
VMEM_LIMIT = 52 * 1024 * 1024


def _mod_row(t, tile=TM):
    p_tiles = N_PROMPT // tile
    return jnp.where(t < p_tiles, 0, 1 + (t - p_tiles) // (DEC_SEQ // tile))


def _mod_spec(layer, chunk, tile=TM):
    return pl.BlockSpec((None, None, None, 1, D_MODEL), lambda t: (layer, _mod_row(t, tile), chunk, 0, 0))


def _const_spec(shape):
    nd = len(shape)
    return pl.BlockSpec(shape, lambda *_: (0,) * nd)


def _split_specs(tile=TM):
    p_tiles = N_PROMPT // tile
    return [
        pl.BlockSpec((tile, D_MODEL), lambda t: (jnp.minimum(t, p_tiles - 1), 0)),
        pl.BlockSpec((tile, D_MODEL), lambda t: (jnp.maximum(t - p_tiles, 0), 0)),
    ]


def _group_spec(tile, width, prompt):
    p_tiles = N_PROMPT // tile
    if prompt:
        return pl.BlockSpec((tile, width), lambda t: (jnp.minimum(t, p_tiles - 1), 0))
    return pl.BlockSpec((tile, width), lambda t: (jnp.maximum(t - p_tiles, 0), 0))


def _pick(t, a_ref, b_ref):
    return jnp.where(t < N_PROMPT // a_ref.shape[0], a_ref[...], b_ref[...])


def _sub_tiles(rows):
    return [slice(i * TM, (i + 1) * TM) for i in range(rows // TM)]


def _params(sem):
    return pltpu.CompilerParams(dimension_semantics=sem, vmem_limit_bytes=VMEM_LIMIT)


def _mod_kernel(cond_ref, w_ref, b_ref, o_ref):
    c = cond_ref[...]
    s = (c / (1.0 + jnp.exp(-c))).astype(BF16)
    o_ref[...] = jnp.dot(s, w_ref[...].astype(BF16), preferred_element_type=F32) + b_ref[...]


def _modulation(cond, w_mod, b_mod):
    tn = 1536
    return pl.pallas_call(
        _mod_kernel,
        grid=(DEPTH, 6 * D_MODEL // tn),
        in_specs=[
            pl.BlockSpec((MOD_ROWS, D_MODEL), lambda l, j: (0, 0)),
            pl.BlockSpec((None, D_MODEL, tn), lambda l, j: (l, 0, j)),
            pl.BlockSpec((None, 1, tn), lambda l, j: (l, 0, j)),
        ],
        out_specs=pl.BlockSpec((None, MOD_ROWS, tn), lambda l, j: (l, 0, j)),
        out_shape=jax.ShapeDtypeStruct((DEPTH, MOD_ROWS, 6 * D_MODEL), F32),
        compiler_params=_params(("arbitrary", "arbitrary")),
        name="modulation",
    )(cond, w_mod, b_mod.reshape(DEPTH, 1, 6 * D_MODEL))


def _grid_cos_sin(half):
    n_freq = half // 2
    rows = DEC_SEQ // GRID_W
    row = jnp.repeat(jnp.arange(rows, dtype=F32), GRID_W)
    col = jnp.tile(jnp.arange(GRID_W, dtype=F32), rows)
    inv = 1.0 / (ROPE_THETA ** (jnp.arange(n_freq, dtype=F32) / n_freq))
    ang = jnp.concatenate([row[:, None] * inv, col[:, None] * inv], axis=-1)
    return jnp.cos(ang), jnp.sin(ang)


def _rope_tables():
    tabs = []
    for half in (HD // 2, ROPE_A // 2):
        cos, sin = _grid_cos_sin(half)
        reps = LANES // (2 * half)
        tabs.append(jnp.tile(jnp.concatenate([cos, cos], axis=-1), (1, reps)))
        tabs.append(jnp.tile(jnp.concatenate([-sin, sin], axis=-1), (1, reps)))
    half = ROPE_A // 2
    one = jnp.ones((DEC_SEQ, NOPE_A), F32)
    zero = jnp.zeros((DEC_SEQ, NOPE_A), F32)
    pad1 = jnp.ones((DEC_SEQ, LANES - NOPE_A - ROPE_A), F32)
    pad0 = jnp.zeros((DEC_SEQ, LANES - NOPE_A - ROPE_A), F32)
    zh = jnp.zeros((DEC_SEQ, half), F32)
    tabs.append(jnp.concatenate([one, cos, cos, pad1], axis=-1))
    tabs.append(jnp.concatenate([zero, -sin, zh, pad0], axis=-1))
    tabs.append(jnp.concatenate([zero, zh, sin, pad0], axis=-1))
    return tuple(tabs)


def _rope(x, c, s, half):
    w = x.shape[1]
    reps = w // LANES
    if reps > 1:
        c = jnp.concatenate([c] * reps, axis=1)
        s = jnp.concatenate([s] * reps, axis=1)
    ahead = pltpu.roll(x, w - half, 1)
    behind = pltpu.roll(x, half, 1)
    lane = lax.broadcasted_iota(jnp.int32, x.shape, 1)
    swapped = jnp.where((lane & (2 * half - 1)) < half, ahead, behind)
    return x * c + swapped * s


def _rope_specs(n, tile):
    p_tiles = N_PROMPT // tile

    def idx(t):
        return (jnp.maximum(t - p_tiles, 0) % (DEC_SEQ // tile), 0)
    return [pl.BlockSpec((tile, LANES), idx)] * n


def _rms(x, g_ref):
    return x * lax.rsqrt(jnp.mean(x * x, axis=-1, keepdims=True) + EPS) * g_ref[...]


def _inproj_ab_kernel(xp_ref, xs_ref, shift_ref, scale_ref, w_in_ref, qn_ref, kvn_ref, w_uq_ref, w_kn_ref, w_v_ref,
                      place_ref, c64_ref, s64_ref, c32_ref, s32_ref, ca_ref, sa_ref, sb_ref,
                      qa_ref, ka_ref, va_ref, qb_ref, ckv_p_ref, kpe_p_ref, kbt_p_ref, vbt_p_ref, kb_s_ref, vb_s_ref):
    t = pl.program_id(0)
    x = _pick(t, xp_ref, xs_ref)
    subs = _sub_tiles(x.shape[0])
    h = [(x[s] * (1.0 + scale_ref[...]) + shift_ref[...]).astype(BF16) for s in subs]
    z = [jnp.dot(h_s, w_in_ref[...], preferred_element_type=F32) for h_s in h]
    ql = [_rms(z_s[:, :Q_LORA], qn_ref).astype(BF16) for z_s in z]
    qa = [jnp.dot(ql_s, w_uq_ref[...], preferred_element_type=F32) for ql_s in ql]
    ckv = [_rms(z_s[:, Q_LORA:Q_LORA + KV_LORA], kvn_ref) for z_s in z]
    k_nope = [jnp.dot(c_s.astype(BF16), w_kn_ref[...], preferred_element_type=F32) for c_s in ckv]
    va = [jnp.dot(c_s.astype(BF16), w_v_ref[...], preferred_element_type=F32).astype(BF16) for c_s in ckv]
    for s, va_s in zip(subs, va):
        va_ref[s, :] = va_s
    qb = [z_s[:, COL_QB:COL_KB] for z_s in z]
    kb = [z_s[:, COL_KB:COL_VB] for z_s in z]
    vb = [z_s[:, COL_VB:COL_KPE] for z_s in z]
    kpe = [z_s[:, COL_KPE:] for z_s in z]

    def finish(qa_out, qb_out, kpe_out):
        k_rope = [jnp.dot(kpe_s.astype(BF16), place_ref[...], preferred_element_type=F32) for kpe_s in kpe_out]
        for i, s in enumerate(subs):
            qa_ref[s, :] = qa_out[i].astype(BF16)
            qb_ref[s, :] = qb_out[i].astype(BF16)
            ka_ref[s, :] = (k_nope[i] + k_rope[i]).astype(BF16)

    p_tiles = N_PROMPT // x.shape[0]

    @pl.when(t < p_tiles)
    def _():
        finish(qa, qb, kpe)
        for i, s in enumerate(subs):
            ckv_p_ref[s, :] = ckv[i]
            kpe_p_ref[s, :] = kpe[i][:, :ROPE_A]
            kbt_p_ref[i] = kb[i].T
            vbt_p_ref[i] = vb[i].T

    @pl.when(t >= p_tiles)
    def _():
        w = qa[0].shape[1]
        reps = w // LANES
        half = ROPE_A // 2
        qa_rot, qb_rot, kb_rot, kpe_rot = [], [], [], []
        for i, s in enumerate(subs):
            ca = jnp.concatenate([ca_ref[s, :]] * reps, axis=1)
            sa = jnp.concatenate([sa_ref[s, :]] * reps, axis=1)
            sb = jnp.concatenate([sb_ref[s, :]] * reps, axis=1)
            qa_rot.append(qa[i] * ca + pltpu.roll(qa[i], w - half, 1) * sa + pltpu.roll(qa[i], half, 1) * sb)
            qb_rot.append(_rope(qb[i], c64_ref[s, :], s64_ref[s, :], HD // 2))
            kb_rot.append(_rope(kb[i], c64_ref[s, :], s64_ref[s, :], HD // 2))
            kpe_rot.append(_rope(kpe[i], c32_ref[s, :], s32_ref[s, :], ROPE_A // 2))
        finish(qa_rot, qb_rot, kpe_rot)
        for i, s in enumerate(subs):
            kb_s_ref[s, :] = kb_rot[i]
            vb_s_ref[s, :] = vb[i]


def _mla_weights(w_uq, w_ukv):
    pad = LANES - NOPE_A - ROPE_A
    uq = w_uq.reshape(Q_LORA, H_A, NOPE_A + ROPE_A)
    uq = jnp.pad(uq, ((0, 0), (0, 0), (0, pad))).reshape(Q_LORA, H_A * LANES)
    ukv = w_ukv.reshape(KV_LORA, H_A, NOPE_A + V_A)
    kn = jnp.pad(ukv[:, :, :NOPE_A], ((0, 0), (0, 0), (0, LANES - NOPE_A))).reshape(KV_LORA, H_A * LANES)
    wv = ukv[:, :, NOPE_A:].reshape(KV_LORA, H_A * V_A)
    src = np.arange(LANES)[:, None]
    dst = np.arange(H_A * LANES)[None, :] % LANES
    place = (src < ROPE_A) & (dst == src + NOPE_A)
    return uq.astype(BF16), kn.astype(BF16), wv.astype(BF16), jnp.asarray(place, BF16)


def _inproj_ab(xp, xs, mod, w_in, q_norm, kv_norm, w_uq, w_kn, w_v, place, tabs):
    widths = [(H_A * LANES, BF16), (H_A * LANES, BF16), (H_A * V_A, BF16), (H_B * HD, BF16)]
    tp = TP_AB
    out_specs = [pl.BlockSpec((tp, w), lambda t: (t, 0)) for w, _ in widths]
    out_shape = [jax.ShapeDtypeStruct((N_TOK, w), dt) for w, dt in widths]
    for w in (KV_LORA, ROPE_A):
        out_specs.append(_group_spec(tp, w, prompt=True))
        out_shape.append(jax.ShapeDtypeStruct((N_PROMPT, w), F32))
    assert SEQ == TM
    p_tiles = N_PROMPT // tp
    for _ in range(2):
        out_specs.append(pl.BlockSpec((tp // SEQ, KVH_B * HD, SEQ), lambda t: (jnp.minimum(t, p_tiles - 1), 0, 0)))
        out_shape.append(jax.ShapeDtypeStruct((BATCH, KVH_B * HD, SEQ), F32))
    for w in (KVH_B * HD, KVH_B * HD):
        out_specs.append(_group_spec(tp, w, prompt=False))
        out_shape.append(jax.ShapeDtypeStruct((N_SAMPLE, w), F32))
    return pl.pallas_call(
        _inproj_ab_kernel,
        grid=(N_TOK // tp,),
        in_specs=_split_specs(tp) + [_mod_spec(0, 0, tp), _mod_spec(0, 1, tp), _const_spec((D_MODEL, AB_COLS)),
                                     _const_spec((1, Q_LORA)), _const_spec((1, KV_LORA)),
                                     _const_spec((Q_LORA, H_A * LANES)), _const_spec((KV_LORA, H_A * LANES)),
                                     _const_spec((KV_LORA, H_A * V_A)), _const_spec((LANES, H_A * LANES))]
        + _rope_specs(7, tp),
        out_specs=out_specs,
        out_shape=out_shape,
        compiler_params=_params(("arbitrary",)),
        name="inproj_ab",
    )(xp, xs, mod, mod, w_in, q_norm, kv_norm, w_uq, w_kn, w_v, place, *tabs)


def _inproj_c_kernel(x_ref, shift_ref, scale_ref, w_in_ref, g_ref, ones_ref, c64_ref, s64_ref,
                     q_ref, k_p_ref, v_p_ref, k_s_ref, v_s_ref):
    t = pl.program_id(0)
    subs = _sub_tiles(x_ref.shape[0])
    h = [(x_ref[s, :] * (1.0 + scale_ref[...]) + shift_ref[...]).astype(BF16) for s in subs]
    z = [jnp.dot(h_s, w_in_ref[...], preferred_element_type=F32) for h_s in h]
    qk = [z_s[:, :C_NORM_COLS] for z_s in z]
    sq = [qk_s * qk_s for qk_s in qk]
    sq_hi = [sq_s.astype(BF16) for sq_s in sq]
    sq_lo = [(sq_s - hi_s.astype(F32)).astype(BF16) for sq_s, hi_s in zip(sq, sq_hi)]
    ss = []
    for hi_s, lo_s in zip(sq_hi, sq_lo):
        blocks = []
        for j in range(C_NORM_COLS // NORM_BLOCK):
            sl = slice(j * NORM_BLOCK, (j + 1) * NORM_BLOCK)
            blocks.append(jnp.dot(hi_s[:, sl], ones_ref[...], preferred_element_type=F32)
                          + jnp.dot(lo_s[:, sl], ones_ref[...], preferred_element_type=F32))
        ss.append(jnp.concatenate(blocks, axis=1))
    qk = [qk_s * lax.rsqrt(ss_s * (1.0 / HD) + EPS) * g_ref[...] for qk_s, ss_s in zip(qk, ss)]
    p_tiles = N_PROMPT // x_ref.shape[0]

    @pl.when(t < p_tiles)
    def _():
        for s, qk_s, z_s in zip(subs, qk, z):
            q_ref[s, :] = qk_s[:, :H_C * HD].astype(BF16)
            k_p_ref[s, :] = qk_s[:, H_C * HD:]
            v_p_ref[s, :] = z_s[:, C_NORM_COLS:]

    @pl.when(t >= p_tiles)
    def _():
        for s, qk_s, z_s in zip(subs, qk, z):
            q_ref[s, :] = _rope(qk_s[:, :H_C * HD], c64_ref[s, :], s64_ref[s, :], HD // 2).astype(BF16)
            k_s_ref[s, :] = _rope(qk_s[:, H_C * HD:], c64_ref[s, :], s64_ref[s, :], HD // 2)
            v_s_ref[s, :] = z_s[:, C_NORM_COLS:]


def _inproj_c(x, mod, w_in, g_full, tabs):
    def tok(w):
        return pl.BlockSpec((TP, w), lambda t: (t, 0))
    kv_w = KVH_C * HD
    head = np.arange(NORM_BLOCK) // HD
    ones = jnp.asarray(head[:, None] == head[None, :], BF16)
    return pl.pallas_call(
        _inproj_c_kernel,
        grid=(N_TOK // TP,),
        in_specs=[tok(D_MODEL), _mod_spec(1, 0, TP), _mod_spec(1, 1, TP),
                  _const_spec((D_MODEL, C_NORM_COLS + KVH_C * HD)),
                  _const_spec((1, C_NORM_COLS)), _const_spec((NORM_BLOCK, NORM_BLOCK))] + _rope_specs(2, TP),
        out_specs=[tok(H_C * HD), _group_spec(TP, kv_w, True), _group_spec(TP, kv_w, True),
                   _group_spec(TP, kv_w, False), _group_spec(TP, kv_w, False)],
        out_shape=[jax.ShapeDtypeStruct((N_TOK, H_C * HD), BF16),
                   jax.ShapeDtypeStruct((N_PROMPT, kv_w), F32), jax.ShapeDtypeStruct((N_PROMPT, kv_w), F32),
                   jax.ShapeDtypeStruct((N_SAMPLE, kv_w), F32), jax.ShapeDtypeStruct((N_SAMPLE, kv_w), F32)],
        compiler_params=_params(("arbitrary",)),
        name="inproj_c",
    )(x, mod, mod, w_in, g_full, ones, *tabs[:2])


def _qk(q, k):
    return lax.dot_general(q, k, (((1,), (1,)), ((), ())), preferred_element_type=F32)


def _softmax_pv(scores, values, scale, sink=None, kv_t=False):
    c = scale * LOG2E
    m = scores[0].max(axis=-1, keepdims=True)
    for s in scores[1:]:
        m = jnp.maximum(m, s.max(axis=-1, keepdims=True))
    m = m * c
    if sink is not None:
        sink = sink * LOG2E
        m = jnp.maximum(m, sink)
    den = None
    out = None
    for s, v in zip(scores, values):
        p = jnp.exp2(s * c - m)
        d = p.sum(axis=-1, keepdims=True)
        p = p.astype(BF16)
        if isinstance(v, (list, tuple)):
            rows = p.shape[0] // len(v)
            o = jnp.concatenate([_qk(p[b * rows:(b + 1) * rows], v_b) if kv_t
                                 else jnp.dot(p[b * rows:(b + 1) * rows], v_b, preferred_element_type=F32)
                                 for b, v_b in enumerate(v)], axis=0)
        else:
            o = _qk(p, v) if kv_t else jnp.dot(p, v, preferred_element_type=F32)
        den = d if den is None else den + d
        out = o if out is None else out + o
    if sink is not None:
        den = den + jnp.exp2(sink - m)
    return out * (1.0 / den)


def _mla_heads(qa_ref, ka_refs, va_refs, o_ref, stack):
    scale = (NOPE_A + ROPE_A) ** -0.5
    rows = qa_ref.shape[0]
    for h0 in range(0, H_A, stack):
        heads = range(h0, h0 + stack)
        scores = [jnp.concatenate([_qk(qa_ref[:, h * LANES:(h + 1) * LANES], ka_ref[:, h * LANES:(h + 1) * LANES])
                                   for h in heads], axis=0) for ka_ref in ka_refs]
        values = [[va_ref[:, h * V_A:(h + 1) * V_A] for h in heads] for va_ref in va_refs]
        o = _softmax_pv(scores, values, scale)
        o = jnp.concatenate([o[b * rows:(b + 1) * rows] for b in range(stack)], axis=1)
        o_ref[:, h0 * V_A:(h0 + stack) * V_A] = o.astype(o_ref.dtype)


def _gqa_groups(q_ref, col0, n_q, ks, vs, mask=None, sink=None, kv_t=False):
    rows = q_ref.shape[0]
    n_groups = len(ks)
    n_seg = len(ks[0])
    per_group = []
    for g in range(n_groups):
        base = col0 + g * n_q * HD
        q = jnp.concatenate([q_ref[:, base + i * HD:base + (i + 1) * HD] for i in range(n_q)], axis=0)
        if kv_t:
            per_group.append([jnp.dot(q, k, preferred_element_type=F32) for k in ks[g]])
        else:
            per_group.append([_qk(q, k) for k in ks[g]])
    scores = [jnp.concatenate([per_group[g][j] for g in range(n_groups)], axis=0) for j in range(n_seg)]
    values = [[vs[g][j] for g in range(n_groups)] for j in range(n_seg)]
    if mask is not None:
        scores[0] = jnp.where(mask, scores[0], NEG_INF)
    o = _softmax_pv(scores, values, HD ** -0.5, sink, kv_t)
    return jnp.concatenate([o[i * rows:(i + 1) * rows] for i in range(n_groups * n_q)], axis=1)


def _stacked_sink(sink_ref, h0, n_q, rows):
    head = lax.broadcasted_iota(jnp.int32, (n_q * rows, 1), 0) // rows
    col = jnp.full((n_q * rows, 1), sink_ref[h0], F32)
    for g in range(1, n_q):
        col = jnp.where(head == g, sink_ref[h0 + g], col)
    return col


def _attn_ab_prompt_kernel(sink_ref, qa_ref, ka_ref, va_ref, qb_ref, kbt_ref, vbt_ref, o_ref):
    _mla_heads(qa_ref, [ka_ref], [va_ref], o_ref, MLA_STACK_PROMPT)
    kbt = kbt_ref[...].astype(BF16)
    vbt = vbt_ref[...].astype(BF16)
    sls = [slice(kh * HD, (kh + 1) * HD) for kh in range(KVH_B)]
    o = _gqa_groups(qb_ref, 0, H_B // KVH_B, [[kbt[sl, :]] for sl in sls], [[vbt[sl, :]] for sl in sls],
                    sink=_stacked_sink(sink_ref, 0, H_B, SEQ), kv_t=True)
    o_ref[:, H_A * V_A:] = o.astype(o_ref.dtype)


def _attn_ab_prompt(sink, qa, ka, va, qb, kbt, vbt):
    def blk(w):
        return pl.BlockSpec((SEQ, w), lambda b: (b, 0))
    tblk = pl.BlockSpec((None, KVH_B * HD, SEQ), lambda b: (b, 0, 0))
    return pl.pallas_call(
        _attn_ab_prompt_kernel,
        grid=(BATCH,),
        in_specs=[pl.BlockSpec(memory_space=pltpu.SMEM), blk(H_A * LANES), blk(H_A * LANES), blk(H_A * V_A),
                  blk(H_B * HD), tblk, tblk],
        out_specs=blk(D_MODEL),
        out_shape=jax.ShapeDtypeStruct((N_PROMPT, H_A * V_A + H_B * HD), BF16),
        compiler_params=_params(("arbitrary",)),
        name="attn_ab_prompt",
    )(sink, qa, ka, va, qb, kbt, vbt)


def _attn_ab_sample_kernel(sink_ref, qa_ref, ka_ref, va_ref, qb_ref, kb_ref, vb_ref,
                           ckv_ctx_ref, kpe_ctx_ref, kb_ctx_ref, vb_ctx_ref, w_kn_ref, w_v_ref, place_ref,
                           o_ref, ka_ctx_ref, va_ctx_ref):
    i = pl.program_id(1)

    @pl.when(i == 0)
    def _():
        ckv16 = ckv_ctx_ref[...].astype(BF16)
        k_nope = jnp.dot(ckv16, w_kn_ref[...], preferred_element_type=F32)
        k_rope = jnp.dot(kpe_ctx_ref[...].astype(BF16), place_ref[...], preferred_element_type=F32)
        ka_ctx_ref[...] = (k_nope + k_rope).astype(BF16)
        va_ctx_ref[...] = jnp.dot(ckv16, w_v_ref[...], preferred_element_type=F32).astype(BF16)

    _mla_heads(qa_ref, [ka_ref, ka_ctx_ref], [va_ref, va_ctx_ref], o_ref, MLA_STACK_SAMPLE)

    n_win = BQ_S + 2 * WINDOW
    start = pl.multiple_of(jnp.clip(i * BQ_S - WINDOW, 0, DEC_SEQ - n_win), WINDOW)
    row = lax.broadcasted_iota(jnp.int32, (H_B * BQ_S, n_win), 0)
    qpos = i * BQ_S + (row & (BQ_S - 1))
    kpos = start + lax.broadcasted_iota(jnp.int32, (H_B * BQ_S, n_win), 1)
    in_band = jnp.abs(qpos - kpos) <= WINDOW
    kwin = kb_ref[pl.ds(start, n_win), :].astype(BF16)
    vwin = vb_ref[pl.ds(start, n_win), :].astype(BF16)
    kctx = kb_ctx_ref[...].astype(BF16)
    vctx = vb_ctx_ref[...].astype(BF16)
    sls = [slice(kh * HD, (kh + 1) * HD) for kh in range(KVH_B)]
    o = _gqa_groups(qb_ref, 0, H_B // KVH_B, [[kwin[:, sl], kctx[:, sl]] for sl in sls],
                    [[vwin[:, sl], vctx[:, sl]] for sl in sls],
                    mask=in_band, sink=_stacked_sink(sink_ref, 0, H_B, BQ_S))
    o_ref[:, H_A * V_A:] = o.astype(o_ref.dtype)


def _attn_ab_sample(sink, qa, ka, va, qb, kb, vb, ckv_ctx, kpe_ctx, kb_ctx, vb_ctx, w_kn, w_v, place):
    qoff = N_PROMPT // BQ_S
    nq = DEC_SEQ // BQ_S
    boff = N_PROMPT // DEC_SEQ

    def qblk(w):
        return pl.BlockSpec((BQ_S, w), lambda b, i: (qoff + b * nq + i, 0))

    def bblk(w):
        return pl.BlockSpec((DEC_SEQ, w), lambda b, i: (boff + b, 0))

    def sblk(w):
        return pl.BlockSpec((DEC_SEQ, w), lambda b, i: (b, 0))

    def cblk(w):
        return pl.BlockSpec((PAST_LEN, w), lambda b, i: (b, 0))

    return pl.pallas_call(
        _attn_ab_sample_kernel,
        grid=(DEC_BATCH, nq),
        in_specs=[pl.BlockSpec(memory_space=pltpu.SMEM), qblk(H_A * LANES), bblk(H_A * LANES), bblk(H_A * V_A),
                  qblk(H_B * HD), sblk(KVH_B * HD), sblk(KVH_B * HD),
                  cblk(KV_LORA), cblk(LANES), cblk(KVH_B * HD), cblk(KVH_B * HD),
                  _const_spec((KV_LORA, H_A * LANES)), _const_spec((KV_LORA, H_A * V_A)),
                  _const_spec((LANES, H_A * LANES))],
        out_specs=pl.BlockSpec((BQ_S, D_MODEL), lambda b, i: (b * nq + i, 0)),
        out_shape=jax.ShapeDtypeStruct((N_SAMPLE, H_A * V_A + H_B * HD), BF16),
        scratch_shapes=[pltpu.VMEM((PAST_LEN, H_A * LANES), BF16), pltpu.VMEM((PAST_LEN, H_A * V_A), BF16)],
        compiler_params=_params(("arbitrary", "arbitrary")),
        name="attn_ab_sample",
    )(sink, qa, ka, va, qb, kb, vb, ckv_ctx, kpe_ctx, kb_ctx, vb_ctx, w_kn, w_v, place)


def _gqa_heads(q_ref, k_list, v_list, o_ref, stack):
    gq = H_C // KVH_C
    for kh0 in range(0, KVH_C, stack):
        sls = [slice(kh * HD, (kh + 1) * HD) for kh in range(kh0, kh0 + stack)]
        o = _gqa_groups(q_ref, kh0 * gq * HD, gq, [[k[:, sl] for k in k_list] for sl in sls],
                        [[v[:, sl] for v in v_list] for sl in sls])
        o_ref[:, kh0 * gq * HD:(kh0 + stack) * gq * HD] = o.astype(o_ref.dtype)


def _attn_c_prompt_kernel(q_ref, k_ref, v_ref, o_ref):
    _gqa_heads(q_ref, [k_ref[...].astype(BF16)], [v_ref[...].astype(BF16)], o_ref, GQA_STACK_PROMPT)


def _attn_c_prompt(q, k, v):
    def blk(w):
        return pl.BlockSpec((SEQ, w), lambda b: (b, 0))
    return pl.pallas_call(
        _attn_c_prompt_kernel,
        grid=(BATCH,),
        in_specs=[blk(H_C * HD), blk(KVH_C * HD), blk(KVH_C * HD)],
        out_specs=blk(H_C * HD),
        out_shape=jax.ShapeDtypeStruct((N_PROMPT, H_C * HD), BF16),
        compiler_params=_params(("arbitrary",)),
        name="attn_c_prompt",
    )(q, k, v)


def _attn_c_sample_kernel(q_ref, k_ref, v_ref, kc_ref, vc_ref, o_ref):
    _gqa_heads(q_ref, [k_ref[...].astype(BF16), kc_ref[...].astype(BF16)],
               [v_ref[...].astype(BF16), vc_ref[...].astype(BF16)], o_ref, GQA_STACK_SAMPLE)


def _attn_c_sample(q, k, v, k_ctx, v_ctx):
    qoff = N_PROMPT // BQ_S
    nq = DEC_SEQ // BQ_S
    w = KVH_C * HD
    return pl.pallas_call(
        _attn_c_sample_kernel,
        grid=(DEC_BATCH, nq),
        in_specs=[pl.BlockSpec((BQ_S, H_C * HD), lambda b, i: (qoff + b * nq + i, 0)),
                  pl.BlockSpec((DEC_SEQ, w), lambda b, i: (b, 0)),
                  pl.BlockSpec((DEC_SEQ, w), lambda b, i: (b, 0)),
                  pl.BlockSpec((PAST_LEN, w), lambda b, i: (b, 0)),
                  pl.BlockSpec((PAST_LEN, w), lambda b, i: (b, 0))],
        out_specs=pl.BlockSpec((BQ_S, H_C * HD), lambda b, i: (b * nq + i, 0)),
        out_shape=jax.ShapeDtypeStruct((N_SAMPLE, H_C * HD), BF16),
        compiler_params=_params(("arbitrary", "arbitrary")),
        name="attn_c_sample",
    )(q, k, v, k_ctx, v_ctx)


def _layer_norm(y, g, b):
    mu = jnp.mean(y, axis=-1, keepdims=True)
    yc = y - mu
    var = jnp.mean(yc * yc, axis=-1, keepdims=True)
    return yc * lax.rsqrt(var + EPS) * g + b


def _route(sel, aff):
    def row(a, j):
        return a[j:j + 1, :]

    scores = []
    for g in range(N_GROUPS):
        a0, a1, a2, a3 = (row(sel, 4 * g + j) for j in range(4))
        hi01, lo01 = jnp.maximum(a0, a1), jnp.minimum(a0, a1)
        hi23, lo23 = jnp.maximum(a2, a3), jnp.minimum(a2, a3)
        top1 = jnp.maximum(hi01, hi23)
        top2 = jnp.maximum(jnp.minimum(hi01, hi23), jnp.maximum(lo01, lo23))
        scores.append(top1 + top2)
    best = scores[0]
    gi = jnp.zeros(best.shape, jnp.int32)
    for g in range(1, N_GROUPS):
        better = scores[g] > best
        gi = jnp.where(better, g, gi)
        best = jnp.where(better, scores[g], best)

    def in_group(a, j):
        out = row(a, j)
        for g in range(1, N_GROUPS):
            out = jnp.where(gi == g, row(a, 4 * g + j), out)
        return out

    v = [in_group(sel, j) for j in range(4)]
    a = [in_group(aff, j) for j in range(4)]
    chosen = []
    for j in range(4):
        rank = jnp.zeros(best.shape, jnp.int32)
        for k in range(4):
            if k == j:
                continue
            ahead = (v[k] >= v[j]) if k < j else (v[k] > v[j])
            rank = rank + ahead.astype(jnp.int32)
        chosen.append(rank < 2)
    total = sum(jnp.where(chosen[j], a[j], 0.0) for j in range(4))
    w = [jnp.where(chosen[j], a[j], 0.0) / total for j in range(4)]
    pair = jnp.zeros(best.shape, jnp.int32)
    for p, (ja, jb) in enumerate(zip(SLOT_A_LOCAL, SLOT_B_LOCAL)):
        pair = jnp.where(chosen[ja] & chosen[jb], p, pair)
    gate_a = jnp.where(pair == 0, w[0], jnp.where(pair <= 2, w[2], w[3]))
    gate_b = jnp.where((pair == 0) | (pair == 1) | (pair == 4), w[1], jnp.where(pair == 5, w[2], w[0]))
    return gi * PAIRS_PER_GROUP + pair, gate_a, gate_b


def _post_attn_kernel(split_x, *refs):
    t = pl.program_id(0)
    op_ref, os_ref = refs[:2]
    if split_x:
        x = _pick(t, refs[2], refs[3])
        refs = refs[4:]
    else:
        x = refs[2][...]
        refs = refs[3:]
    (w_out_ref, gate_ref, shift_ref, scale_ref, lng_ref, lnb_ref, rw_ref, rb_ref, tri_ref,
     x1_ref, row_ref, route_ref, counts_ref, carry_ref) = refs
    o = _pick(t, op_ref, os_ref)
    subs = _sub_tiles(o.shape[0])

    a = [jnp.dot(o[s], w_out_ref[...], preferred_element_type=F32) for s in subs]
    x1 = [_layer_norm(ALPHA * x[s] + gate_ref[...] * a_s, lng_ref[...], lnb_ref[...]) for s, a_s in zip(subs, a)]
    h2 = [x1_s * (1.0 + scale_ref[...]) + shift_ref[...] for x1_s in x1]
    for s, x1_s, h2_s in zip(subs, x1, h2):
        x1_ref[s, :] = x1_s
        row_ref[s, :D_MODEL] = h2_s
    rw = rw_ref[...]
    rw_hi = rw.astype(BF16)
    rw_lo = (rw - rw_hi.astype(F32)).astype(BF16)
    rw_both = jnp.concatenate([rw_hi, rw_lo], axis=0)
    logits = []
    for h2_s in h2:
        h2_hi = h2_s.astype(BF16)
        h2_lo = (h2_s - h2_hi.astype(F32)).astype(BF16)
        by_hi = _qk(rw_both, h2_hi)
        logits.append(by_hi[:N_EXPERTS] + by_hi[N_EXPERTS:] + _qk(rw_hi, h2_lo))
    aff = [1.0 / (1.0 + jnp.exp(-l)) for l in logits]
    routed = [_route(aff_s + rb_ref[...], aff_s) for aff_s in aff]

    lane = lax.broadcasted_iota(jnp.int32, (LANES, TM), 0)
    for s, (_, gate_a, gate_b) in zip(subs, routed):
        meta_t = jnp.where(lane == 0, gate_a, jnp.where(lane == 1, gate_b, 0.0))
        row_ref[s, D_MODEL:] = meta_t.T

    @pl.when(t == 0)
    def _():
        carry_ref[...] = jnp.zeros_like(carry_ref)

    onehots = [lax.broadcasted_iota(jnp.int32, (BUCKET_ROWS, TM), 0) == bucket for bucket, _, _ in routed]
    prefixes = [jnp.dot(oh.astype(BF16), tri_ref[...], preferred_element_type=F32) for oh in onehots]
    carry = carry_ref[...]
    route_ref[...] = jnp.zeros(route_ref.shape, F32)
    for s, (bucket, _, _), onehot, prefix in zip(subs, routed, onehots, prefixes):
        rank = jnp.sum(jnp.where(onehot, prefix + carry[:, 0:1], 0.0), axis=0, keepdims=True)
        carry = carry + jnp.sum(onehot.astype(F32), axis=1, keepdims=True)
        route_ref[0:1, s] = bucket.astype(F32)
        route_ref[1:2, s] = rank
    carry_ref[...] = carry
    counts_ref[...] = carry


def _post_attn(layer, o_p, o_s, x_list, mod, w_out, ln_g, ln_b, rw_t, rb):
    split_x = len(x_list) == 2
    tok = pl.BlockSpec((TP, D_MODEL), lambda t: (t, 0))
    tri = jnp.asarray(np.arange(TM)[:, None] < np.arange(TM)[None, :], BF16)
    return pl.pallas_call(
        functools.partial(_post_attn_kernel, split_x),
        grid=(N_TOK // TP,),
        in_specs=_split_specs(TP) + (_split_specs(TP) if split_x else [tok]) + [
            _const_spec((D_MODEL, D_MODEL)), _mod_spec(layer, 2, TP), _mod_spec(layer, 3, TP),
            _mod_spec(layer, 4, TP),
            _const_spec((1, D_MODEL)), _const_spec((1, D_MODEL)), _const_spec((N_EXPERTS, D_MODEL)),
            _const_spec((N_EXPERTS, 1)), _const_spec((TM, TM))],
        out_specs=[tok, pl.BlockSpec((TP, ROW_W), lambda t: (t, 0)), pl.BlockSpec((8, TP), lambda t: (0, t)),
                   _const_spec((BUCKET_ROWS, LANES))],
        out_shape=[jax.ShapeDtypeStruct((N_TOK, D_MODEL), F32), jax.ShapeDtypeStruct((N_TOK, ROW_W), F32),
                   jax.ShapeDtypeStruct((8, N_TOK), F32), jax.ShapeDtypeStruct((BUCKET_ROWS, LANES), F32)],
        scratch_shapes=[pltpu.VMEM((BUCKET_ROWS, LANES), F32)],
        compiler_params=_params(("arbitrary",)),
        name="post_attn",
    )(o_p, o_s, *x_list, w_out, mod, mod, mod, ln_g, ln_b, rw_t, rb, tri)


def _moe_kernel(ia_ref, ib_ref, itile_ref, isize_ref, ni_ref, nt_ref, x_hbm, wga_ref, wua_ref, wda_ref, wgb_ref,
                wub_ref, wdb_ref, o_hbm, xbuf, obuf, xsem, osem):
    del ia_ref, ib_ref
    i = pl.program_id(0)
    n_items = ni_ref[0]
    n_tiles = nt_ref[0]
    slots = ((wga_ref, wua_ref, wda_ref), (wgb_ref, wub_ref, wdb_ref))

    def fetch(k):
        row0 = pl.multiple_of(itile_ref[k] * TME, TME)
        return pltpu.make_async_copy(x_hbm.at[pl.ds(row0, ITEM_TILES * TME), :], xbuf.at[k % 2], xsem.at[k % 2])

    def store(k, tiles):
        row0 = pl.multiple_of(itile_ref[k] * TME, TME)
        return pltpu.make_async_copy(obuf.at[k % 2, pl.ds(0, tiles * TME), :],
                                     o_hbm.at[pl.ds(row0, tiles * TME), :], osem.at[k % 2])

    def wait_store(k):
        for tiles in range(1, ITEM_TILES + 1):
            @pl.when(isize_ref[k] == tiles)
            def _(tiles=tiles):
                store(k, tiles).wait()

    @pl.when((i == 0) & (n_items > 0))
    def _():
        fetch(0).start()

    @pl.when(i < n_items)
    def _():
        fetch(i).wait()

        @pl.when(i + 1 < n_items)
        def _():
            fetch(i + 1).start()

        @pl.when(i >= 2)
        def _():
            wait_store(i - 2)

        for tiles in range(1, ITEM_TILES + 1):
            @pl.when(isize_ref[i] == tiles)
            def _(tiles=tiles):
                rows = tiles * TME
                x_ref = xbuf.at[i % 2, pl.ds(0, rows), :]
                x = x_ref[:, :D_MODEL].astype(BF16)
                acc = None
                for slot, (wg_ref, wu_ref, wd_ref) in enumerate(slots):
                    gt = jnp.dot(x, wg_ref[...].astype(BF16), preferred_element_type=F32)
                    up = jnp.dot(x, wu_ref[...].astype(BF16), preferred_element_type=F32)
                    gate = x_ref[:, D_MODEL + slot:D_MODEL + slot + 1]
                    hid = (gt / (1.0 + jnp.exp(-gt))) * up * gate
                    y = jnp.dot(hid.astype(BF16), wd_ref[...].astype(BF16), preferred_element_type=F32)
                    acc = y if acc is None else acc + y
                obuf[i % 2, pl.ds(0, rows), :] = acc
                store(i, tiles).start()

    @pl.when(i == MOE_ITEMS - 1)
    def _():
        @pl.when(n_items >= 2)
        def _():
            wait_store(n_items - 2)

        @pl.when(n_items >= 1)
        def _():
            wait_store(n_items - 1)

        obuf[0] = jnp.zeros(obuf.shape[1:], F32)

        def unused(g, carry):
            cp = pltpu.make_async_copy(obuf.at[0, pl.ds(0, TME), :],
                                       o_hbm.at[pl.ds(pl.multiple_of(g * TME, TME), TME), :], osem.at[0])
            cp.start()
            cp.wait()
            return carry
        lax.fori_loop(n_tiles, MOE_TILES, unused, 0)


def _moe(layer, item_a, item_b, item_tile, item_size, n_items, n_tiles, rows, w_gate, w_up, w_down):
    def wspec(shape, which):
        if which == 0:
            return pl.BlockSpec((None, None) + shape, lambda i, ia, ib, it, sz, ni, nt: (layer, ia[i], 0, 0))
        return pl.BlockSpec((None, None) + shape, lambda i, ia, ib, it, sz, ni, nt: (layer, ib[i], 0, 0))

    up_shape, down_shape = (D_MODEL, D_EXPERT), (D_EXPERT, D_MODEL)
    grid_spec = pltpu.PrefetchScalarGridSpec(
        num_scalar_prefetch=6,
        grid=(MOE_ITEMS,),
        in_specs=[pl.BlockSpec(memory_space=pl.ANY),
                  wspec(up_shape, 0), wspec(up_shape, 0), wspec(down_shape, 0),
                  wspec(up_shape, 1), wspec(up_shape, 1), wspec(down_shape, 1)],
        out_specs=pl.BlockSpec(memory_space=pl.ANY),
        scratch_shapes=[pltpu.VMEM((2, ITEM_TILES * TME, ROW_W), F32),
                        pltpu.VMEM((2, ITEM_TILES * TME, D_MODEL), F32),
                        pltpu.SemaphoreType.DMA((2,)), pltpu.SemaphoreType.DMA((2,))],
    )
    return pl.pallas_call(
        _moe_kernel,
        grid_spec=grid_spec,
        out_shape=jax.ShapeDtypeStruct((MOE_ROWS, D_MODEL), F32),
        compiler_params=_params(("arbitrary",)),
        name="moe_experts",
    )(item_a, item_b, item_tile, item_size, n_items, n_tiles, rows, w_gate, w_up, w_down, w_gate, w_up, w_down)


def _row_copy(src_ref, src_row, dst_ref, dst_row, sem):
    return pltpu.make_async_copy(src_ref.at[pl.ds(src_row, 1), :], dst_ref.at[pl.ds(dst_row, 1), :], sem)


def _dispatch_kernel(pos_ref, pend_ref, cnt_ref, nt_ref, src_ref, out_ref, buf, zero_ref, in_sem, sem, zsem, usem):
    t = pl.program_id(0)
    n_alloc = MOE_TILES + ITEM_TILES - 1

    def fetch(tile):
        return pltpu.make_async_copy(src_ref.at[pl.ds(pl.multiple_of(tile * TM, TM), TM), :],
                                     buf.at[tile % STAGE_SLOTS], in_sem.at[tile % STAGE_SLOTS])

    def zero_tile(row0, s):
        return pltpu.make_async_copy(zero_ref, out_ref.at[pl.ds(pl.multiple_of(row0, TME), TME), :], s)

    @pl.when(t == 0)
    def _():
        fetch(0).start()
        zero_ref[...] = jnp.zeros_like(zero_ref)

        for b in range(N_BUCKETS):
            @pl.when(cnt_ref[b] > 0)
            def _(b=b):
                zero_tile(pend_ref[b] - TME, zsem).start()

        def start_unused(i, carry):
            zero_tile(i * TME, usem).start()
            return carry
        lax.fori_loop(nt_ref[0], n_alloc, start_unused, 0)

        for b in range(N_BUCKETS):
            @pl.when(cnt_ref[b] > 0)
            def _(b=b):
                zero_tile(pend_ref[b] - TME, zsem).wait()

    @pl.when(t + 1 < N_TILES)
    def _():
        fetch(t + 1).start()

    fetch(t).wait()
    tile_ref = buf.at[t % STAGE_SLOTS]

    for r in range(TM):
        _row_copy(tile_ref, r, out_ref, pos_ref[t * TM + r], sem.at[t % 2]).start(priority=r % 2)

    def drain(tile):
        pltpu.make_async_copy(buf.at[0], out_ref.at[pl.ds(0, TM), :], sem.at[tile % 2]).wait()

    @pl.when(t > 0)
    def _():
        drain(t - 1)

    @pl.when(t == N_TILES - 1)
    def _():
        drain(t)

        def wait_unused(i, carry):
            zero_tile(i * TME, usem).wait()
            return carry
        lax.fori_loop(nt_ref[0], n_alloc, wait_unused, 0)


def _dispatch(pos, pend, counts, n_tiles, rows):
    return pl.pallas_call(
        _dispatch_kernel,
        grid_spec=pltpu.PrefetchScalarGridSpec(
            num_scalar_prefetch=4, grid=(N_TILES,),
            in_specs=[pl.BlockSpec(memory_space=pl.ANY)],
            out_specs=pl.BlockSpec(memory_space=pl.ANY),
            scratch_shapes=[pltpu.VMEM((STAGE_SLOTS, TM, ROW_W), F32), pltpu.VMEM((TME, ROW_W), F32),
                            pltpu.SemaphoreType.DMA((STAGE_SLOTS,)), pltpu.SemaphoreType.DMA((2,)),
                            pltpu.SemaphoreType.DMA, pltpu.SemaphoreType.DMA]),
        out_shape=jax.ShapeDtypeStruct((MOE_ROWS + (ITEM_TILES - 1) * TME, ROW_W), F32),
        compiler_params=_params(("arbitrary",)),
        name="dispatch_rows",
    )(pos, pend, counts, n_tiles, rows)


def _plan(route, counts):
    bucket = route[0].astype(jnp.int32)
    rank = route[1].astype(jnp.int32)
    counts = counts[:N_BUCKETS, 0].astype(jnp.int32)
    padded = ((counts + TME - 1) // TME) * TME
    pend = jnp.cumsum(padded)
    pstart = pend - padded
    ids = jnp.arange(N_BUCKETS, dtype=jnp.int32)
    pos = rank + jnp.sum(jnp.where(bucket[None, :] == ids[:, None], pstart[:, None], 0), axis=0)
    n_tiles = pend[-1] // TME
    tiles = padded // TME
    items = (tiles + ITEM_TILES - 1) // ITEM_TILES
    item_end = jnp.cumsum(items)
    item_start = item_end - items
    n_items = item_end[-1]
    step = jnp.arange(MOE_ITEMS, dtype=jnp.int32)
    live = step < n_items
    item = jnp.minimum(step, jnp.maximum(n_items - 1, 0))
    bkt = jnp.minimum(jnp.sum((item[:, None] >= item_end[None, :]).astype(jnp.int32), axis=1), N_BUCKETS - 1)
    hot = bkt[:, None] == ids[None, :]

    def of_bucket(v):
        return jnp.sum(jnp.where(hot, v[None, :], 0), axis=1)

    k = item - of_bucket(item_start)
    item_tile = of_bucket(pstart // TME) + ITEM_TILES * k
    item_size = jnp.where(live, jnp.minimum(ITEM_TILES, of_bucket(tiles) - ITEM_TILES * k), 0)
    group, pair = bkt // PAIRS_PER_GROUP, bkt % PAIRS_PER_GROUP
    slot_a = jnp.asarray(SLOT_A_LOCAL, jnp.int32)
    slot_b = jnp.asarray(SLOT_B_LOCAL, jnp.int32)
    pair_hot = pair[:, None] == jnp.arange(PAIRS_PER_GROUP, dtype=jnp.int32)[None, :]
    item_a = group * EXPERTS_PER_GROUP + jnp.sum(jnp.where(pair_hot, slot_a[None, :], 0), axis=1)
    item_b = group * EXPERTS_PER_GROUP + jnp.sum(jnp.where(pair_hot, slot_b[None, :], 0), axis=1)
    return (pos, pend, counts, n_tiles.reshape(1), item_a.astype(jnp.int32), item_b.astype(jnp.int32),
            item_tile.astype(jnp.int32), item_size.astype(jnp.int32), n_items.astype(jnp.int32).reshape(1))


def _post_moe_kernel(split_out, pos_ref, f_ref, x1_ref, gate_ref, lng_ref, lnb_ref, *refs):
    out_refs, (fbuf, sem) = refs[:-2], refs[-2:]
    t = pl.program_id(0)
    tq = x1_ref.shape[0]
    n_steps = N_TOK // tq
    p_steps = N_PROMPT // tq

    def wait_tile(slot):
        pltpu.make_async_copy(f_ref.at[pl.ds(0, tq), :], fbuf.at[slot], sem.at[slot]).wait()

    @pl.when(t == 0)
    def _():
        def issue(r, carry):
            _row_copy(f_ref, pos_ref[r], fbuf.at[r // tq], r % tq, sem.at[r // tq]).start()
            return carry
        lax.fori_loop(0, (GATHER_SLOTS - 1) * tq, issue, 0, unroll=8)

    slot = t % GATHER_SLOTS
    wait_tile(slot)
    ahead = jnp.minimum(t + GATHER_SLOTS - 1, n_steps - 1)
    aslot = (t + GATHER_SLOTS - 1) % GATHER_SLOTS
    for r in range(tq):
        _row_copy(f_ref, pos_ref[ahead * tq + r], fbuf.at[aslot], r, sem.at[aslot]).start(priority=r % 2)
    subs = _sub_tiles(tq)
    y = [_layer_norm(ALPHA * x1_ref[s, :] + gate_ref[...] * fbuf[slot, s, :], lng_ref[...], lnb_ref[...])
         for s in subs]

    @pl.when(t == n_steps - 1)
    def _():
        for k in range(1, GATHER_SLOTS):
            wait_tile((t + k) % GATHER_SLOTS)
    if split_out:
        @pl.when(t < p_steps)
        def _():
            for s, y_s in zip(subs, y):
                out_refs[0][s, :] = y_s

        @pl.when(t >= p_steps)
        def _():
            for s, y_s in zip(subs, y):
                out_refs[1][s, :] = y_s
    else:
        for s, y_s in zip(subs, y):
            out_refs[0][s, :] = y_s


def _post_moe(layer, pos, f_sorted, x1, mod, ln_g, ln_b, split_out):
    tq = TQ
    p_steps = N_PROMPT // tq
    tok = pl.BlockSpec((tq, D_MODEL), lambda t, p: (t, 0))
    if split_out:
        out_specs = [pl.BlockSpec((tq, D_MODEL), lambda t, p: (jnp.minimum(t, p_steps - 1), 0)),
                     pl.BlockSpec((tq, D_MODEL), lambda t, p: (jnp.maximum(t - p_steps, 0), 0))]
        out_shape = [jax.ShapeDtypeStruct((N_PROMPT, D_MODEL), F32), jax.ShapeDtypeStruct((N_SAMPLE, D_MODEL), F32)]
    else:
        out_specs = [tok]
        out_shape = [jax.ShapeDtypeStruct((N_TOK, D_MODEL), F32)]
    mod_spec = pl.BlockSpec((None, None, None, 1, D_MODEL), lambda t, p: (layer, _mod_row(t, tq), 5, 0, 0))
    return pl.pallas_call(
        functools.partial(_post_moe_kernel, split_out),
        grid_spec=pltpu.PrefetchScalarGridSpec(
            num_scalar_prefetch=1, grid=(N_TOK // tq,),
            in_specs=[pl.BlockSpec(memory_space=pl.ANY), tok, mod_spec,
                      pl.BlockSpec((1, D_MODEL), lambda t, p: (0, 0)), pl.BlockSpec((1, D_MODEL), lambda t, p: (0, 0))],
            out_specs=out_specs,
            scratch_shapes=[pltpu.VMEM((GATHER_SLOTS, tq, D_MODEL), F32),
                            pltpu.SemaphoreType.DMA((GATHER_SLOTS,))]),
        out_shape=out_shape,
        compiler_params=_params(("arbitrary",)),
        name="post_moe",
    )(pos, f_sorted, x1, mod, ln_g, ln_b)


def _ffn(layer, o_p, o_s, x_list, mod, w_out, ln_g, ln_b, rw_t, rb, w_gate, w_up, w_down, split_out):
    x1, rows, route, counts = _post_attn(layer, o_p, o_s, x_list, mod, w_out, ln_g[layer, 0][None],
                                         ln_b[layer, 0][None], rw_t, rb)
    pos, pend, counts, n_tiles, item_a, item_b, item_tile, item_size, n_items = _plan(route, counts)
    rows_sorted = _dispatch(pos, pend, counts, n_tiles, rows)
    f_sorted = _moe(layer, item_a, item_b, item_tile, item_size, n_items, n_tiles, rows_sorted, w_gate, w_up, w_down)
    return _post_moe(layer, pos, f_sorted, x1, mod, ln_g[layer, 1][None], ln_b[layer, 1][None], split_out)


def kernel(x_prompt, x_sample, c, cache_mla_ckv, cache_mla_kpe, cache_swa_k, cache_swa_v, cache_gqa_k, cache_gqa_v, c_ctx, w_mod, b_mod, ln_g, ln_b, w_in_ab, mla_q_norm, mla_w_uq, mla_kv_norm, mla_w_ukv, swa_sink, w_out_ab, w_in_c, gqa_q_norm, gqa_k_norm, w_out_c, router_w, router_bias, exp_w_gate, exp_w_up, exp_w_down):
    xp = x_prompt.reshape(N_PROMPT, D_MODEL)
    xs = x_sample.reshape(N_SAMPLE, D_MODEL)
    cond = jnp.concatenate([c_ctx[None], c, jnp.zeros((MOD_ROWS - 1 - DEC_BATCH, D_MODEL), F32)], axis=0)
    mod = _modulation(cond, w_mod, b_mod).reshape(DEPTH, MOD_ROWS, 6, 1, D_MODEL)
    tabs = _rope_tables()
    rw_t = router_w.T
    rb = router_bias.reshape(N_EXPERTS, 1)

    w = w_in_ab[0]
    pad = jnp.zeros((D_MODEL, AB_COLS - w.shape[1]), F32)
    kpe0 = Q_LORA + KV_LORA
    w_in = jnp.concatenate([w[:, :kpe0], w[:, kpe0 + ROPE_A:], w[:, kpe0:kpe0 + ROPE_A], pad], axis=1).astype(BF16)
    w_uq, w_kn, w_v, place = _mla_weights(mla_w_uq[0], mla_w_ukv[0])
    qa, ka, va, qb, ckv_p, kpe_p, kbt_p, vbt_p, kb_s, vb_s = _inproj_ab(
        xp, xs, mod, w_in, mla_q_norm[0][None], mla_kv_norm[0][None], w_uq, w_kn, w_v, place, tabs)
    sink = swa_sink[0]
    o_p = _attn_ab_prompt(sink, qa, ka, va, qb, kbt_p, vbt_p)
    kpe_ctx = jnp.pad(cache_mla_kpe[:, 0].reshape(-1, ROPE_A), ((0, 0), (0, LANES - ROPE_A)))
    o_s = _attn_ab_sample(sink, qa, ka, va, qb, kb_s, vb_s,
                          cache_mla_ckv[:, 0].reshape(-1, KV_LORA), kpe_ctx,
                          cache_swa_k[:, 0].reshape(-1, KVH_B * HD), cache_swa_v[:, 0].reshape(-1, KVH_B * HD),
                          w_kn, w_v, place)
    (x2,) = _ffn(0, o_p, o_s, [xp, xs], mod, w_out_ab[0].astype(BF16), ln_g, ln_b, rw_t, rb,
                 exp_w_gate, exp_w_up, exp_w_down, split_out=False)

    g_full = jnp.concatenate([jnp.tile(gqa_q_norm[0], H_C), jnp.tile(gqa_k_norm[0], KVH_C)])[None]
    qc, kc_p, vc_p, kc_s, vc_s = _inproj_c(x2, mod, w_in_c[0].astype(BF16), g_full, tabs)
    oc_p = _attn_c_prompt(qc, kc_p, vc_p)
    oc_s = _attn_c_sample(qc, kc_s, vc_s, cache_gqa_k[:, 0].reshape(-1, KVH_C * HD),
                          cache_gqa_v[:, 0].reshape(-1, KVH_C * HD))
    y_p, y_s = _ffn(1, oc_p, oc_s, [x2], mod, w_out_c[0].astype(BF16), ln_g, ln_b, rw_t, rb,
                    exp_w_gate, exp_w_up, exp_w_down, split_out=True)

    y_prompt = y_p.reshape(BATCH, SEQ, D_MODEL)
    y_sample = y_s.reshape(DEC_BATCH, DEC_SEQ, D_MODEL)
    new_ckv = ckv_p.reshape(BATCH, 1, SEQ, KV_LORA)
    new_kpe = kpe_p.reshape(BATCH, 1, SEQ, ROPE_A)
    new_swk = kbt_p.reshape(BATCH, 1, KVH_B, HD, SEQ).transpose(0, 1, 4, 2, 3)
    new_swv = vbt_p.reshape(BATCH, 1, KVH_B, HD, SEQ).transpose(0, 1, 4, 2, 3)
    new_gk = kc_p.reshape(BATCH, 1, SEQ, KVH_C, HD)
    new_gv = vc_p.reshape(BATCH, 1, SEQ, KVH_C, HD)
    return y_prompt, y_sample, new_ckv, new_kpe, new_swk, new_swv, new_gk, new_gv
```
